```python
import jax, jax.numpy as jnp
from jax import lax
import numpy as np

D_MODEL = 1024
BATCH = 8
SEQ = 4096
DEPTH = 4

CHUNK = 64
GMLP_BLOCK = 128
GMLP_GROUPS = 8
GMLP_WIDTH = D_MODEL
GMLP_GROUP_DIM = GMLP_WIDTH // GMLP_GROUPS
LRU_WIDTH = D_MODEL
LRU_HEADS = 8
LRU_HEAD_DIM = LRU_WIDTH // LRU_HEADS
LRU_C = 8.0
CONV_WIDTH = 4
MEM_LEN = 256
MEM_HEADS = 4
MEM_HEAD_DIM = D_MODEL // MEM_HEADS
MEM_WIDTH = MEM_HEADS * MEM_HEAD_DIM
N_BRANCH = 3
EPS = 1e-6
IN_SPLITS = (GMLP_WIDTH, GMLP_WIDTH, GMLP_WIDTH, LRU_WIDTH, LRU_WIDTH, MEM_WIDTH, MEM_WIDTH, N_BRANCH * D_MODEL)
IN_WIDTH = sum(IN_SPLITS)

kernel_name = 'hybrid_gmlp_rglru_memxattn_trunk'


def _rmsnorm(x, g):
    xf = x.astype(jnp.float32)
    y = xf * lax.rsqrt(jnp.mean(xf * xf, axis=-1, keepdims=True) + EPS)
    return (y * g.astype(jnp.float32)).astype(x.dtype)


def _layernorm(x, g, b):
    xf = x.astype(jnp.float32)
    mu = jnp.mean(xf, axis=-1, keepdims=True)
    var = jnp.mean(jnp.square(xf - mu), axis=-1, keepdims=True)
    y = (xf - mu) * lax.rsqrt(var + EPS)
    return (y * g.astype(jnp.float32) + b.astype(jnp.float32)).astype(x.dtype)


def _split_columns(z):
    idx, acc = [], 0
    for w in IN_SPLITS[:-1]:
        acc += w
        idx.append(acc)
    return jnp.split(z, idx, axis=-1)


def _gmlp_spatial(u, v, ln_g, ln_b, ws, bs):
    b, s, _ = u.shape
    n = s // GMLP_BLOCK
    vn = _layernorm(v, ln_g, ln_b).reshape(b, n, GMLP_BLOCK, GMLP_GROUPS, GMLP_GROUP_DIM)
    chunk_id = jnp.arange(GMLP_BLOCK) // CHUNK
    mask = chunk_id[:, None] >= chunk_id[None, :]
    w = jnp.where(mask[None], ws, jnp.zeros_like(ws))
    sv = jnp.einsum('gij,bnjgc->bnigc', w, vn) + jnp.transpose(bs)[None, None, :, :, None]
    return u * sv.reshape(b, s, GMLP_WIDTH)


def _rg_lru(xb, conv_w, conv_b, wr, br, wi, bi, lam):
    b, s, _ = xb.shape
    xp = jnp.pad(xb, ((0, 0), (CONV_WIDTH - 1, 0), (0, 0)))
    xc = conv_b + sum(xp[:, k:k + s] * conv_w[k] for k in range(CONV_WIDTH))
    xh = xc.reshape(b, s, LRU_HEADS, LRU_HEAD_DIM)
    r = jax.nn.sigmoid(jnp.einsum('bshc,hcd->bshd', xh, wr).reshape(b, s, LRU_WIDTH) + br)
    i = jax.nn.sigmoid(jnp.einsum('bshc,hcd->bshd', xh, wi).reshape(b, s, LRU_WIDTH) + bi)
    log_a = -LRU_C * r.astype(jnp.float32) * jax.nn.softplus(-lam.astype(jnp.float32))
    a = jnp.exp(log_a)
    mult = jnp.sqrt(-jnp.expm1(2.0 * log_a))
    u = mult * (i * xc).astype(jnp.float32)

    def combine(left, right):
        a1, h1 = left
        a2, h2 = right
        return a1 * a2, a2 * h1 + h2

    _, h = lax.associative_scan(combine, (a, u), axis=1)
    return h.astype(xb.dtype)


def _mem_attention(q, k, v):
    b, s, _ = q.shape
    qh = q.reshape(b, s, MEM_HEADS, MEM_HEAD_DIM)
    kh = k.reshape(b, MEM_LEN, MEM_HEADS, MEM_HEAD_DIM)
    vh = v.reshape(b, MEM_LEN, MEM_HEADS, MEM_HEAD_DIM)
    scores = jnp.einsum('bshd,bmhd->bhsm', qh, kh).astype(jnp.float32) * (MEM_HEAD_DIM ** -0.5)
    p = jax.nn.softmax(scores, axis=-1).astype(v.dtype)
    return jnp.einsum('bhsm,bmhd->bshd', p, vh).reshape(b, s, MEM_WIDTH)


def _fwd_setup_inputs(seed: int = 0) -> dict:
    key = jax.random.key(seed)
    ks = jax.random.split(key, 32)

    def nrm(k, shape, scale):
        return jax.random.normal(k, shape, jnp.float32) * scale

    L = DEPTH
    u_a = jax.random.uniform(ks[20], (L, LRU_WIDTH), jnp.float32, 0.9, 0.999)
    a_base = u_a ** (1.0 / LRU_C)
    lru_lambda = jnp.log(a_base) - jnp.log1p(-a_base)
    return {
        'x': nrm(ks[0], (BATCH, SEQ, D_MODEL), 1.0),
        'mem': nrm(ks[1], (BATCH, MEM_LEN, D_MODEL), 1.0),
        'mem_norm_g': 1.0 + nrm(ks[2], (D_MODEL,), 0.05),
        'pre_norm_g': 1.0 + nrm(ks[3], (L, D_MODEL), 0.05),
        'post_norm_g': 1.0 + nrm(ks[4], (L, D_MODEL), 0.05),
        'w_in': nrm(ks[5], (L, D_MODEL, IN_WIDTH), D_MODEL ** -0.5),
        'gmlp_ln_g': 1.0 + nrm(ks[6], (L, GMLP_WIDTH), 0.05),
        'gmlp_ln_b': nrm(ks[7], (L, GMLP_WIDTH), 0.02),
        'gmlp_ws': nrm(ks[8], (L, GMLP_GROUPS, GMLP_BLOCK, GMLP_BLOCK), GMLP_BLOCK ** -0.5),
        'gmlp_bs': 1.0 + nrm(ks[9], (L, GMLP_GROUPS, GMLP_BLOCK), 0.05),
        'conv_w': nrm(ks[10], (L, CONV_WIDTH, LRU_WIDTH), CONV_WIDTH ** -0.5),
        'conv_b': nrm(ks[11], (L, LRU_WIDTH), 0.02),
        'lru_wr': nrm(ks[12], (L, LRU_HEADS, LRU_HEAD_DIM, LRU_HEAD_DIM), LRU_HEAD_DIM ** -0.5),
        'lru_br': nrm(ks[13], (L, LRU_WIDTH), 0.02),
        'lru_wi': nrm(ks[14], (L, LRU_HEADS, LRU_HEAD_DIM, LRU_HEAD_DIM), LRU_HEAD_DIM ** -0.5),
        'lru_bi': nrm(ks[15], (L, LRU_WIDTH), 0.02),
        'lru_lambda': lru_lambda,
        'w_kv': nrm(ks[16], (L, D_MODEL, 2 * MEM_WIDTH), D_MODEL ** -0.5),
        'w_pa': nrm(ks[17], (L, GMLP_WIDTH, D_MODEL), GMLP_WIDTH ** -0.5),
        'w_pb': nrm(ks[18], (L, LRU_WIDTH, D_MODEL), LRU_WIDTH ** -0.5),
        'w_pc': nrm(ks[19], (L, MEM_WIDTH, D_MODEL), MEM_WIDTH ** -0.5),
        'w_out': nrm(ks[21], (L, D_MODEL, D_MODEL), D_MODEL ** -0.5),
    }


def _fwd_reference(x, mem, mem_norm_g, pre_norm_g, post_norm_g, w_in, gmlp_ln_g, gmlp_ln_b, gmlp_ws, gmlp_bs,
              conv_w, conv_b, lru_wr, lru_br, lru_wi, lru_bi, lru_lambda, w_kv, w_pa, w_pb, w_pc, w_out):
    b, s, _ = x.shape
    mem_n = _rmsnorm(mem, mem_norm_g)
    for l in range(DEPTH):
        h = _rmsnorm(x, pre_norm_g[l])
        z = h @ w_in[l]
        u, v, g_a, xb, g_b, q, g_c, merge_logits = _split_columns(z)
        y_a = _gmlp_spatial(u, v, gmlp_ln_g[l], gmlp_ln_b[l], gmlp_ws[l], gmlp_bs[l]) * jax.nn.silu(g_a)
        y_b = _rg_lru(xb, conv_w[l], conv_b[l], lru_wr[l], lru_br[l], lru_wi[l], lru_bi[l], lru_lambda[l]) * jax.nn.silu(g_b)
        k_m, v_m = jnp.split(mem_n @ w_kv[l], 2, axis=-1)
        y_c = _mem_attention(q, k_m, v_m) * jax.nn.silu(g_c)
        gates = jax.nn.sigmoid(merge_logits).reshape(b, s, N_BRANCH, D_MODEL)
        merged = (gates[:, :, 0] * (y_a @ w_pa[l])
                  + gates[:, :, 1] * (y_b @ w_pb[l])
                  + gates[:, :, 2] * (y_c @ w_pc[l]))
        x = x + _rmsnorm(merged @ w_out[l], post_norm_g[l])
    return x


import jax as _jax
import jax.numpy as _jnp

TWIN_FORMAT = 'train_step'
FWD_PARAMS = ['x', 'mem', 'mem_norm_g', 'pre_norm_g', 'post_norm_g', 'w_in', 'gmlp_ln_g', 'gmlp_ln_b', 'gmlp_ws', 'gmlp_bs', 'conv_w', 'conv_b', 'lru_wr', 'lru_br', 'lru_wi', 'lru_bi', 'lru_lambda', 'w_kv', 'w_pa', 'w_pb', 'w_pc', 'w_out']
TWIN_WEIGHTS = ['mem_norm_g', 'pre_norm_g', 'post_norm_g', 'w_in', 'gmlp_ln_g', 'gmlp_ln_b', 'gmlp_ws', 'gmlp_bs', 'conv_w', 'conv_b', 'lru_wr', 'lru_br', 'lru_wi', 'lru_bi', 'lru_lambda', 'w_kv', 'w_pa', 'w_pb', 'w_pc', 'w_out']
TWIN_DIFF_INPUT = 'x'
TWIN_INPUTS = ['x', 'mem', 'mem_norm_g', 'pre_norm_g', 'post_norm_g', 'w_in', 'gmlp_ln_g', 'gmlp_ln_b', 'gmlp_ws', 'gmlp_bs', 'conv_w', 'conv_b', 'lru_wr', 'lru_br', 'lru_wi', 'lru_bi', 'lru_lambda', 'w_kv', 'w_pa', 'w_pb', 'w_pc', 'w_out', 'loss_target', 'm_mem_norm_g', 'm_pre_norm_g', 'm_post_norm_g', 'm_w_in', 'm_gmlp_ln_g', 'm_gmlp_ln_b', 'm_gmlp_ws', 'm_gmlp_bs', 'm_conv_w', 'm_conv_b', 'm_lru_wr', 'm_lru_br', 'm_lru_wi', 'm_lru_bi', 'm_lru_lambda', 'm_w_kv', 'm_w_pa', 'm_w_pb', 'm_w_pc', 'm_w_out', 'v_mem_norm_g', 'v_pre_norm_g', 'v_post_norm_g', 'v_w_in', 'v_gmlp_ln_g', 'v_gmlp_ln_b', 'v_gmlp_ws', 'v_gmlp_bs', 'v_conv_w', 'v_conv_b', 'v_lru_wr', 'v_lru_br', 'v_lru_wi', 'v_lru_bi', 'v_lru_lambda', 'v_w_kv', 'v_w_pa', 'v_w_pb', 'v_w_pc', 'v_w_out']
TWIN_OUTPUTS = ['loss', 'grad_x', 'grad_mem_norm_g', 'grad_pre_norm_g', 'grad_post_norm_g', 'grad_w_in', 'grad_gmlp_ln_g', 'grad_gmlp_ln_b', 'grad_gmlp_ws', 'grad_gmlp_bs', 'grad_conv_w', 'grad_conv_b', 'grad_lru_wr', 'grad_lru_br', 'grad_lru_wi', 'grad_lru_bi', 'grad_lru_lambda', 'grad_w_kv', 'grad_w_pa', 'grad_w_pb', 'grad_w_pc', 'grad_w_out', 'delta_mem_norm_g', 'delta_pre_norm_g', 'delta_post_norm_g', 'delta_w_in', 'delta_gmlp_ln_g', 'delta_gmlp_ln_b', 'delta_gmlp_ws', 'delta_gmlp_bs', 'delta_conv_w', 'delta_conv_b', 'delta_lru_wr', 'delta_lru_br', 'delta_lru_wi', 'delta_lru_bi', 'delta_lru_lambda', 'delta_w_kv', 'delta_w_pa', 'delta_w_pb', 'delta_w_pc', 'delta_w_out', 'new_m_mem_norm_g', 'new_m_pre_norm_g', 'new_m_post_norm_g', 'new_m_w_in', 'new_m_gmlp_ln_g', 'new_m_gmlp_ln_b', 'new_m_gmlp_ws', 'new_m_gmlp_bs', 'new_m_conv_w', 'new_m_conv_b', 'new_m_lru_wr', 'new_m_lru_br', 'new_m_lru_wi', 'new_m_lru_bi', 'new_m_lru_lambda', 'new_m_w_kv', 'new_m_w_pa', 'new_m_w_pb', 'new_m_w_pc', 'new_m_w_out', 'new_v_mem_norm_g', 'new_v_pre_norm_g', 'new_v_post_norm_g', 'new_v_w_in', 'new_v_gmlp_ln_g', 'new_v_gmlp_ln_b', 'new_v_gmlp_ws', 'new_v_gmlp_bs', 'new_v_conv_w', 'new_v_conv_b', 'new_v_lru_wr', 'new_v_lru_br', 'new_v_lru_wi', 'new_v_lru_bi', 'new_v_lru_lambda', 'new_v_w_kv', 'new_v_w_pa', 'new_v_w_pb', 'new_v_w_pc', 'new_v_w_out']
TWIN_LEAF_KINDS = {'loss': 'loss', 'grad_x': 'grad_x', 'grad_mem_norm_g': 'grad_w', 'grad_pre_norm_g': 'grad_w', 'grad_post_norm_g': 'grad_w', 'grad_w_in': 'grad_w', 'grad_gmlp_ln_g': 'grad_w', 'grad_gmlp_ln_b': 'grad_w', 'grad_gmlp_ws': 'grad_w', 'grad_gmlp_bs': 'grad_w', 'grad_conv_w': 'grad_w', 'grad_conv_b': 'grad_w', 'grad_lru_wr': 'grad_w', 'grad_lru_br': 'grad_w', 'grad_lru_wi': 'grad_w', 'grad_lru_bi': 'grad_w', 'grad_lru_lambda': 'grad_w', 'grad_w_kv': 'grad_w', 'grad_w_pa': 'grad_w', 'grad_w_pb': 'grad_w', 'grad_w_pc': 'grad_w', 'grad_w_out': 'grad_w', 'delta_mem_norm_g': 'delta_w', 'delta_pre_norm_g': 'delta_w', 'delta_post_norm_g': 'delta_w', 'delta_w_in': 'delta_w', 'delta_gmlp_ln_g': 'delta_w', 'delta_gmlp_ln_b': 'delta_w', 'delta_gmlp_ws': 'delta_w', 'delta_gmlp_bs': 'delta_w', 'delta_conv_w': 'delta_w', 'delta_conv_b': 'delta_w', 'delta_lru_wr': 'delta_w', 'delta_lru_br': 'delta_w', 'delta_lru_wi': 'delta_w', 'delta_lru_bi': 'delta_w', 'delta_lru_lambda': 'delta_w', 'delta_w_kv': 'delta_w', 'delta_w_pa': 'delta_w', 'delta_w_pb': 'delta_w', 'delta_w_pc': 'delta_w', 'delta_w_out': 'delta_w', 'new_m_mem_norm_g': 'new_m', 'new_m_pre_norm_g': 'new_m', 'new_m_post_norm_g': 'new_m', 'new_m_w_in': 'new_m', 'new_m_gmlp_ln_g': 'new_m', 'new_m_gmlp_ln_b': 'new_m', 'new_m_gmlp_ws': 'new_m', 'new_m_gmlp_bs': 'new_m', 'new_m_conv_w': 'new_m', 'new_m_conv_b': 'new_m', 'new_m_lru_wr': 'new_m', 'new_m_lru_br': 'new_m', 'new_m_lru_wi': 'new_m', 'new_m_lru_bi': 'new_m', 'new_m_lru_lambda': 'new_m', 'new_m_w_kv': 'new_m', 'new_m_w_pa': 'new_m', 'new_m_w_pb': 'new_m', 'new_m_w_pc': 'new_m', 'new_m_w_out': 'new_m', 'new_v_mem_norm_g': 'new_v', 'new_v_pre_norm_g': 'new_v', 'new_v_post_norm_g': 'new_v', 'new_v_w_in': 'new_v', 'new_v_gmlp_ln_g': 'new_v', 'new_v_gmlp_ln_b': 'new_v', 'new_v_gmlp_ws': 'new_v', 'new_v_gmlp_bs': 'new_v', 'new_v_conv_w': 'new_v', 'new_v_conv_b': 'new_v', 'new_v_lru_wr': 'new_v', 'new_v_lru_br': 'new_v', 'new_v_lru_wi': 'new_v', 'new_v_lru_bi': 'new_v', 'new_v_lru_lambda': 'new_v', 'new_v_w_kv': 'new_v', 'new_v_w_pa': 'new_v', 'new_v_w_pb': 'new_v', 'new_v_w_pc': 'new_v', 'new_v_w_out': 'new_v'}


def _forward(args):
    return _fwd_reference(*[args[k] for k in FWD_PARAMS])


def _output_shape():
    out = _jax.eval_shape(lambda: _forward(_fwd_setup_inputs(0)))
    return out.shape, out.dtype

N_MICROBATCH = 1
ADAM_LR = 0.001
ADAM_B1 = 0.9
ADAM_B2 = 0.999
ADAM_EPS = 1e-08
ADAM_WD = 0.01
ADAM_STEP = 10
PER_EXAMPLE_BATCH_AXIS = {'x': 0, 'mem': 0, 'loss_target': 0}
SHARED_INPUTS = []
_WEIGHT_DTYPES = {'mem_norm_g': _jnp.float32, 'pre_norm_g': _jnp.float32, 'post_norm_g': _jnp.float32, 'w_in': _jnp.float32, 'gmlp_ln_g': _jnp.float32, 'gmlp_ln_b': _jnp.float32, 'gmlp_ws': _jnp.float32, 'gmlp_bs': _jnp.float32, 'conv_w': _jnp.float32, 'conv_b': _jnp.float32, 'lru_wr': _jnp.float32, 'lru_br': _jnp.float32, 'lru_wi': _jnp.float32, 'lru_bi': _jnp.float32, 'lru_lambda': _jnp.float32, 'w_kv': _jnp.float32, 'w_pa': _jnp.float32, 'w_pb': _jnp.float32, 'w_pc': _jnp.float32, 'w_out': _jnp.float32}
MOMENT_SCALE = {'mem_norm_g': 1.680701e-01, 'pre_norm_g': 1.227662e+00, 'post_norm_g': 3.188052e+01, 'w_in': 3.722279e-01, 'gmlp_ln_g': 4.111109e-01, 'gmlp_ln_b': 4.220331e-01, 'gmlp_ws': 4.165672e-01, 'gmlp_bs': 4.918469e-01, 'conv_w': 6.719742e-01, 'conv_b': 1.310946e+01, 'lru_wr': 2.494064e-01, 'lru_br': 1.723294e-01, 'lru_wi': 4.667152e-01, 'lru_bi': 2.256498e-01, 'lru_lambda': 2.907249e-01, 'w_kv': 5.638428e-02, 'w_pa': 6.535355e-01, 'w_pb': 7.410479e-01, 'w_pc': 6.427348e-02, 'w_out': 9.012453e-01}


def _to_microbatches(a, axis):
    t = _jnp.moveaxis(a, axis, 0)
    t = t.reshape((N_MICROBATCH, t.shape[0] // N_MICROBATCH) + t.shape[1:])
    return _jnp.moveaxis(t, 1, axis + 1)


def setup_inputs(seed: int = 0) -> dict:
    inp = _fwd_setup_inputs(seed)
    key = _jax.random.fold_in(_jax.random.key(seed), 7919)
    shape, _ = _output_shape()
    out = dict(inp)
    out["loss_target"] = _jax.random.normal(_jax.random.fold_in(key, 0), shape, _jnp.float32)
    for i, name in enumerate(TWIN_WEIGHTS):
        w = inp[name].astype(_jnp.float32)
        if MOMENT_SCALE is None:
            s = _jnp.sqrt(_jnp.mean(_jnp.square(w)) + 1e-30)
        else:
            s = MOMENT_SCALE[name]
        km, kv = _jax.random.split(_jax.random.fold_in(key, i + 1))
        out[name] = w
        out["m_" + name] = s * _jax.random.normal(km, w.shape, _jnp.float32)
        out["v_" + name] = (s * s) * _jax.random.uniform(kv, w.shape, _jnp.float32, 0.5, 1.5)
    if N_MICROBATCH > 1:
        for name, axis in PER_EXAMPLE_BATCH_AXIS.items():
            out[name] = _to_microbatches(out[name], axis)
    return {'x': out['x'], 'mem': out['mem'], 'mem_norm_g': out['mem_norm_g'], 'pre_norm_g': out['pre_norm_g'], 'post_norm_g': out['post_norm_g'], 'w_in': out['w_in'], 'gmlp_ln_g': out['gmlp_ln_g'], 'gmlp_ln_b': out['gmlp_ln_b'], 'gmlp_ws': out['gmlp_ws'], 'gmlp_bs': out['gmlp_bs'], 'conv_w': out['conv_w'], 'conv_b': out['conv_b'], 'lru_wr': out['lru_wr'], 'lru_br': out['lru_br'], 'lru_wi': out['lru_wi'], 'lru_bi': out['lru_bi'], 'lru_lambda': out['lru_lambda'], 'w_kv': out['w_kv'], 'w_pa': out['w_pa'], 'w_pb': out['w_pb'], 'w_pc': out['w_pc'], 'w_out': out['w_out'], 'loss_target': out['loss_target'], 'm_mem_norm_g': out['m_mem_norm_g'], 'm_pre_norm_g': out['m_pre_norm_g'], 'm_post_norm_g': out['m_post_norm_g'], 'm_w_in': out['m_w_in'], 'm_gmlp_ln_g': out['m_gmlp_ln_g'], 'm_gmlp_ln_b': out['m_gmlp_ln_b'], 'm_gmlp_ws': out['m_gmlp_ws'], 'm_gmlp_bs': out['m_gmlp_bs'], 'm_conv_w': out['m_conv_w'], 'm_conv_b': out['m_conv_b'], 'm_lru_wr': out['m_lru_wr'], 'm_lru_br': out['m_lru_br'], 'm_lru_wi': out['m_lru_wi'], 'm_lru_bi': out['m_lru_bi'], 'm_lru_lambda': out['m_lru_lambda'], 'm_w_kv': out['m_w_kv'], 'm_w_pa': out['m_w_pa'], 'm_w_pb': out['m_w_pb'], 'm_w_pc': out['m_w_pc'], 'm_w_out': out['m_w_out'], 'v_mem_norm_g': out['v_mem_norm_g'], 'v_pre_norm_g': out['v_pre_norm_g'], 'v_post_norm_g': out['v_post_norm_g'], 'v_w_in': out['v_w_in'], 'v_gmlp_ln_g': out['v_gmlp_ln_g'], 'v_gmlp_ln_b': out['v_gmlp_ln_b'], 'v_gmlp_ws': out['v_gmlp_ws'], 'v_gmlp_bs': out['v_gmlp_bs'], 'v_conv_w': out['v_conv_w'], 'v_conv_b': out['v_conv_b'], 'v_lru_wr': out['v_lru_wr'], 'v_lru_br': out['v_lru_br'], 'v_lru_wi': out['v_lru_wi'], 'v_lru_bi': out['v_lru_bi'], 'v_lru_lambda': out['v_lru_lambda'], 'v_w_kv': out['v_w_kv'], 'v_w_pa': out['v_w_pa'], 'v_w_pb': out['v_w_pb'], 'v_w_pc': out['v_w_pc'], 'v_w_out': out['v_w_out']}


def _loss(weights, diff, rest, loss_target):
    with _jax.named_scope("forward"):
        args = {**rest, TWIN_DIFF_INPUT: diff, **{k: w.astype(_WEIGHT_DTYPES[k]) for k, w in weights.items()}}
        y = _forward(args)
    with _jax.named_scope("loss_head"):
        err = _jnp.square(y.astype(_jnp.float32) - loss_target)
        return 0.5 * _jnp.sum(_jnp.mean(err, axis=-1)) if err.ndim else 0.5 * err


def _adamw(w, g, m, v):
    m = ADAM_B1 * m + (1.0 - ADAM_B1) * g
    v = ADAM_B2 * v + (1.0 - ADAM_B2) * _jnp.square(g)
    m_hat = m / (1.0 - ADAM_B1 ** ADAM_STEP)
    v_hat = v / (1.0 - ADAM_B2 ** ADAM_STEP)
    delta = -ADAM_LR * (m_hat / (_jnp.sqrt(v_hat) + ADAM_EPS) + ADAM_WD * w)
    return delta, m, v


def reference(x, mem, mem_norm_g, pre_norm_g, post_norm_g, w_in, gmlp_ln_g, gmlp_ln_b, gmlp_ws, gmlp_bs, conv_w, conv_b, lru_wr, lru_br, lru_wi, lru_bi, lru_lambda, w_kv, w_pa, w_pb, w_pc, w_out, loss_target, m_mem_norm_g, m_pre_norm_g, m_post_norm_g, m_w_in, m_gmlp_ln_g, m_gmlp_ln_b, m_gmlp_ws, m_gmlp_bs, m_conv_w, m_conv_b, m_lru_wr, m_lru_br, m_lru_wi, m_lru_bi, m_lru_lambda, m_w_kv, m_w_pa, m_w_pb, m_w_pc, m_w_out, v_mem_norm_g, v_pre_norm_g, v_post_norm_g, v_w_in, v_gmlp_ln_g, v_gmlp_ln_b, v_gmlp_ws, v_gmlp_bs, v_conv_w, v_conv_b, v_lru_wr, v_lru_br, v_lru_wi, v_lru_bi, v_lru_lambda, v_w_kv, v_w_pa, v_w_pb, v_w_pc, v_w_out):
    given = dict(x=x, mem=mem, mem_norm_g=mem_norm_g, pre_norm_g=pre_norm_g, post_norm_g=post_norm_g, w_in=w_in, gmlp_ln_g=gmlp_ln_g, gmlp_ln_b=gmlp_ln_b, gmlp_ws=gmlp_ws, gmlp_bs=gmlp_bs, conv_w=conv_w, conv_b=conv_b, lru_wr=lru_wr, lru_br=lru_br, lru_wi=lru_wi, lru_bi=lru_bi, lru_lambda=lru_lambda, w_kv=w_kv, w_pa=w_pa, w_pb=w_pb, w_pc=w_pc, w_out=w_out, loss_target=loss_target, m_mem_norm_g=m_mem_norm_g, m_pre_norm_g=m_pre_norm_g, m_post_norm_g=m_post_norm_g, m_w_in=m_w_in, m_gmlp_ln_g=m_gmlp_ln_g, m_gmlp_ln_b=m_gmlp_ln_b, m_gmlp_ws=m_gmlp_ws, m_gmlp_bs=m_gmlp_bs, m_conv_w=m_conv_w, m_conv_b=m_conv_b, m_lru_wr=m_lru_wr, m_lru_br=m_lru_br, m_lru_wi=m_lru_wi, m_lru_bi=m_lru_bi, m_lru_lambda=m_lru_lambda, m_w_kv=m_w_kv, m_w_pa=m_w_pa, m_w_pb=m_w_pb, m_w_pc=m_w_pc, m_w_out=m_w_out, v_mem_norm_g=v_mem_norm_g, v_pre_norm_g=v_pre_norm_g, v_post_norm_g=v_post_norm_g, v_w_in=v_w_in, v_gmlp_ln_g=v_gmlp_ln_g, v_gmlp_ln_b=v_gmlp_ln_b, v_gmlp_ws=v_gmlp_ws, v_gmlp_bs=v_gmlp_bs, v_conv_w=v_conv_w, v_conv_b=v_conv_b, v_lru_wr=v_lru_wr, v_lru_br=v_lru_br, v_lru_wi=v_lru_wi, v_lru_bi=v_lru_bi, v_lru_lambda=v_lru_lambda, v_w_kv=v_w_kv, v_w_pa=v_w_pa, v_w_pb=v_w_pb, v_w_pc=v_w_pc, v_w_out=v_w_out)
    weights = {n: given[n] for n in TWIN_WEIGHTS}
    shared = {n: given[n] for n in SHARED_INPUTS}
    per_example = {n: given[n] for n in ['x', 'mem']}
    grad_fn = _jax.value_and_grad(_loss, argnums=(0, 1))

    def one_microbatch(ex, loss_target):
        ex = dict(ex)
        diff = ex.pop(TWIN_DIFF_INPUT)
        return grad_fn(weights, diff, {**shared, **ex}, loss_target)

    if N_MICROBATCH == 1:
        loss, (grad_w, grad_x) = one_microbatch(per_example, given["loss_target"])
    else:
        def body(carry, xs):
            loss_sum, grad_sum = carry
            l_k, (gw_k, gx_k) = one_microbatch(xs[0], xs[1])
            with _jax.named_scope("update"):
                return (loss_sum + l_k, _jax.tree.map(_jnp.add, grad_sum, gw_k)), gx_k

        init = (_jnp.zeros((), _jnp.float32), _jax.tree.map(_jnp.zeros_like, weights))
        (loss, grad_w), grad_x = _jax.lax.scan(body, init, (per_example, given["loss_target"]))
    with _jax.named_scope("update"):
        delta_w, new_m, new_v = {}, {}, {}
        for n in TWIN_WEIGHTS:
            delta_w[n], new_m[n], new_v[n] = _adamw(weights[n], grad_w[n], given["m_" + n], given["v_" + n])
    return (loss, grad_x, *[grad_w[n] for n in TWIN_WEIGHTS], *[delta_w[n] for n in TWIN_WEIGHTS],
            *[new_m[n] for n in TWIN_WEIGHTS], *[new_v[n] for n in TWIN_WEIGHTS])
```

```python
import functools

import jax
import jax.numpy as jnp
from jax import lax
from jax.experimental import pallas as pl
from jax.experimental.pallas import tpu as pltpu

F32 = jnp.float32
BF16 = jnp.bfloat16
SDS = jax.ShapeDtypeStruct
MESH = pl.DeviceIdType.MESH

D = 1024
NIN = 10 * D
MEM = 256
GB = 128
NG = 8
NH = 4
HD = D // NH
EPS = 1e-6
LRU_C = 8.0
ADAM_LR, ADAM_B1, ADAM_B2, ADAM_EPS, ADAM_WD, ADAM_STEP = 0.001, 0.9, 0.999, 1e-08, 0.01, 10
NDEV = 8
SMALL_ROWS = 12800

_CALL_KW = {}
HBM = pl.BlockSpec(memory_space=pltpu.HBM)
VMEM = pl.BlockSpec(memory_space=pltpu.VMEM)


def _pcall(body, *, name, in_specs, out_specs, out_shape, grid=None, scratch=(), vmem_mb=48, aliases=None):
    kw = dict(_CALL_KW)
    if grid is not None:
        kw["grid"] = grid
    if aliases:
        kw["input_output_aliases"] = aliases
    params = dict(vmem_limit_bytes=vmem_mb << 20)
    if grid is not None:
        params["dimension_semantics"] = ("arbitrary",) * len(grid)
    return pl.pallas_call(body, name=name, in_specs=in_specs, out_specs=out_specs, out_shape=out_shape,
                          scratch_shapes=list(scratch), compiler_params=pltpu.CompilerParams(**params), **kw)


def _full(shape):
    nd = len(shape)
    return pl.BlockSpec(shape, lambda *_: (0,) * nd)


def _dot(a, b):
    return jnp.dot(a, b, preferred_element_type=F32)


def _dot_nt(a, b):
    return lax.dot_general(a, b, (((1,), (1,)), ((), ())), preferred_element_type=F32)


def _dot_tn(a, b):
    return lax.dot_general(a, b, (((0,), (0,)), ((), ())), preferred_element_type=F32)


def _rowsum(a):
    return jnp.sum(a, axis=0, keepdims=True)


def _silu_parts(g):
    s = jax.nn.sigmoid(g)
    return g * s, s * (1.0 + g * (1.0 - s))


def _rms_scale(x):
    return lax.rsqrt(jnp.mean(x * x, axis=-1, keepdims=True) + EPS)


def _dz_col(k):
    return jnp.where(k < 3, k, jnp.where(k < 6, k + 4, k - 3))


def _cast_stack(w, rows):
    L, R, C = w.shape

    def body(w_ref, o_ref):
        o_ref[...] = w_ref[...].astype(BF16)

    spec = pl.BlockSpec((1, rows, C), lambda l, i: (l, i, 0))
    return _pcall(body, name="cast_w", grid=(L, R // rows), in_specs=[spec], out_specs=spec,
                  out_shape=SDS(w.shape, BF16))(w)


def _cast_proj(w_pa, w_pb, w_pc, w_out):
    L, R, C = w_pa.shape

    def body(a, b, c, d, o_ref):
        o_ref[0, 0] = a[0].astype(BF16)
        o_ref[0, 1] = b[0].astype(BF16)
        o_ref[0, 2] = c[0].astype(BF16)
        o_ref[0, 3] = d[0].astype(BF16)

    spec = pl.BlockSpec((1, R, C), lambda l: (l, 0, 0))
    return _pcall(body, name="cast_proj", grid=(L,), in_specs=[spec] * 4,
                  out_specs=pl.BlockSpec((1, 4, R, C), lambda l: (l, 0, 0, 0)),
                  out_shape=SDS((L, 4, R, C), BF16))(w_pa, w_pb, w_pc, w_out)


def _coords():
    return lax.axis_index("x"), lax.axis_index("y"), lax.axis_index("c")


def _gather_layer(l, win_b, wkv_b, wp_b, cw8):
    kin, nsh = win_b.shape[1], win_b.shape[2]
    nkv = wkv_b.shape[2]
    rp = wp_b.shape[2]

    def body(win, wkv, wp, cw, Win, Wkv, Wp, Cw, ssem, rsem, lsem):
        x, y, c = _coords()
        j = 2 * x + y
        sib = (x, y, 1 - c)
        chips = [(1 - x, y), (x, 1 - y), (1 - x, 1 - y)]

        src_half = [
            lambda cc: win.at[l, pl.ds(cc * (kin // 2), kin // 2), :],
            lambda cc: wkv.at[l, pl.ds(cc * (kin // 2), kin // 2), :],
            lambda cc: wp.at[l, :, pl.ds(cc * (rp // 2), rp // 2), :],
        ]
        place = [
            lambda jj, cc: Win.at[pl.ds(cc * (kin // 2), kin // 2), pl.ds(jj * nsh, nsh)],
            lambda jj, cc: Wkv.at[pl.ds(cc * (kin // 2), kin // 2), pl.ds(jj * nkv, nkv)],
            lambda jj, cc: Wp.at[:, pl.ds(jj * rp + cc * (rp // 2), rp // 2), :],
        ]
        src_full = [win.at[l], wkv.at[l], wp.at[l], cw.at[l]]
        place_full = [
            lambda jj: Win.at[:, pl.ds(jj * nsh, nsh)],
            lambda jj: Wkv.at[:, pl.ds(jj * nkv, nkv)],
            lambda jj: Wp.at[:, pl.ds(jj * rp, rp), :],
            lambda jj: Cw.at[:, pl.ds(jj * 256, 256)],
        ]

        def remote(src, dst, k, to):
            return pltpu.make_async_remote_copy(src_ref=src, dst_ref=dst, send_sem=ssem.at[k], recv_sem=rsem.at[k],
                                                device_id=to, device_id_type=MESH)

        local = [pltpu.make_async_copy(src_full[a], place_full[a](j), lsem.at[a]) for a in range(4)]
        for cp in local:
            cp.start()
        sent = []
        for a in range(3):
            for k, chip in enumerate(chips):
                sent.append(remote(src_half[a](c), place[a](j, c), a * 6 + k, (chip[0], chip[1], c)))
        for k, chip in enumerate(chips):
            sent.append(remote(src_full[3], place_full[3](j), 18 + k, (chip[0], chip[1], c)))
        for cp in sent:
            cp.start()
        passed = []
        for a in range(3):
            for k, chip in enumerate(chips):
                jk = 2 * chip[0] + chip[1]
                got = place[a](jk, c)
                remote(got, got, a * 6 + k, (x, y, c)).wait_recv()
                fw = remote(got, got, a * 6 + 3 + k, sib)
                fw.start()
                passed.append(fw)
        for a in range(3):
            for k, chip in enumerate(chips):
                jk = 2 * chip[0] + chip[1]
                got = place[a](jk, 1 - c)
                remote(got, got, a * 6 + 3 + k, (x, y, c)).wait_recv()
        for k, chip in enumerate(chips):
            got = place_full[3](2 * chip[0] + chip[1])
            remote(got, got, 18 + k, (x, y, c)).wait_recv()
        for cp in sent + passed:
            cp.wait_send()
        for cp in local:
            cp.wait()

    return _pcall(
        body, name="gather_w", in_specs=[HBM] * 4, out_specs=[HBM] * 4,
        out_shape=[SDS((kin, 4 * nsh), BF16), SDS((kin, 4 * nkv), BF16), SDS((4, 4 * rp, D), BF16), SDS((8, D), F32)],
        scratch=[pltpu.SemaphoreType.DMA((21,)), pltpu.SemaphoreType.DMA((21,)), pltpu.SemaphoreType.DMA((4,))],
    )(win_b, wkv_b, wp_b, cw8)


def _a2a_layer(g_in, g_kv, g_pa, g_pb, g_pc, g_out):
    kin = g_in.shape[0]
    nsh, nkv, rp = g_in.shape[1] // 4, g_kv.shape[1] // 4, D // 4

    def body(gin, gkv, gpa, gpb, gpc, gout, lin, lkv, lp, ssem, rsem, lsem):
        x, y, c = _coords()
        me = 4 * x + 2 * y + c
        srcs = [gin, gkv, gpa, gpb, gpc, gout]

        def piece(a, jd, dc):
            if a == 0:
                return gin.at[pl.ds(dc * (kin // 2), kin // 2), pl.ds(jd * nsh, nsh)]
            if a == 1:
                return gkv.at[pl.ds(dc * (kin // 2), kin // 2), pl.ds(jd * nkv, nkv)]
            return srcs[a].at[pl.ds(jd * rp + dc * (rp // 2), rp // 2), :]

        def slot(a, s):
            if a == 0:
                return lin.at[s]
            if a == 1:
                return lkv.at[s]
            return lp.at[s, a - 2]

        started = []
        for mask in range(NDEV):
            dx = 1 - x if mask & 4 else x
            dy = 1 - y if mask & 2 else y
            dc = 1 - c if mask & 1 else c
            for a in range(6):
                src = piece(a, 2 * dx + dy, dc)
                if mask == 0:
                    cp = pltpu.make_async_copy(src, slot(a, me), lsem.at[a])
                else:
                    k = a * 7 + mask - 1
                    cp = pltpu.make_async_remote_copy(src_ref=src, dst_ref=slot(a, me), send_sem=ssem.at[k],
                                                      recv_sem=rsem.at[k], device_id=(dx, dy, dc), device_id_type=MESH)
                cp.start()
                started.append((mask, a, cp))
        for mask in range(1, NDEV):
            dx = 1 - x if mask & 4 else x
            dy = 1 - y if mask & 2 else y
            dc = 1 - c if mask & 1 else c
            sender = 4 * dx + 2 * dy + dc
            for a in range(6):
                k = a * 7 + mask - 1
                got = slot(a, sender)
                pltpu.make_async_remote_copy(src_ref=got, dst_ref=got, send_sem=ssem.at[k], recv_sem=rsem.at[k],
                                             device_id=(x, y, c), device_id_type=MESH).wait_recv()
        for mask, a, cp in started:
            if mask == 0:
                cp.wait()
            else:
                cp.wait_send()

    return _pcall(
        body, name="a2a_grads", in_specs=[HBM] * 6, out_specs=[HBM] * 3,
        out_shape=[SDS((NDEV, kin // 2, nsh), BF16), SDS((NDEV, kin // 2, nkv), BF16), SDS((NDEV, 4, rp // 2, D), BF16)],
        scratch=[pltpu.SemaphoreType.DMA((42,)), pltpu.SemaphoreType.DMA((42,)), pltpu.SemaphoreType.DMA((6,))],
    )(g_in, g_kv, g_pa, g_pb, g_pc, g_out)


def _sum_slots(land, rows):
    _, R, C = land.shape

    def body(l_ref, o_ref):
        acc = l_ref[0].astype(F32)
        for k in range(1, NDEV):
            acc = acc + l_ref[k].astype(F32)
        o_ref[...] = acc

    return _pcall(body, name="sum_slots", grid=(R // rows,),
                  in_specs=[pl.BlockSpec((NDEV, rows, C), lambda i: (0, i, 0))],
                  out_specs=pl.BlockSpec((rows, C), lambda i: (i, 0)), out_shape=SDS((R, C), F32))(land)


def _sibling_exchange(s_in, s_kv, s_p):
    hin, nsh = s_in.shape
    nkv = s_kv.shape[1]
    hp = s_p.shape[1]

    def body(sin, skv, sp, gin, gkv, gp, ssem, rsem, lsem):
        x, y, c = _coords()
        sib = (x, y, 1 - c)
        srcs = [sin, skv, sp]
        place = [lambda cc: gin.at[pl.ds(cc * hin, hin), :], lambda cc: gkv.at[pl.ds(cc * hin, hin), :],
                 lambda cc: gp.at[:, pl.ds(cc * hp, hp), :]]
        cps = []
        for a in range(3):
            lc = pltpu.make_async_copy(srcs[a], place[a](c), lsem.at[a])
            rc = pltpu.make_async_remote_copy(src_ref=srcs[a], dst_ref=place[a](c), send_sem=ssem.at[a],
                                              recv_sem=rsem.at[a], device_id=sib, device_id_type=MESH)
            lc.start()
            rc.start()
            cps.append((lc, rc))
        for a in range(3):
            got = place[a](1 - c)
            pltpu.make_async_remote_copy(src_ref=got, dst_ref=got, send_sem=ssem.at[a], recv_sem=rsem.at[a],
                                         device_id=(x, y, c), device_id_type=MESH).wait_recv()
        for lc, rc in cps:
            rc.wait_send()
            lc.wait()

    return _pcall(
        body, name="sibling_exchange", in_specs=[HBM] * 3, out_specs=[HBM] * 3,
        out_shape=[SDS((2 * hin, nsh), F32), SDS((2 * hin, nkv), F32), SDS((4, 2 * hp, D), F32)],
        scratch=[pltpu.SemaphoreType.DMA((3,)), pltpu.SemaphoreType.DMA((3,)), pltpu.SemaphoreType.DMA((3,))],
    )(s_in, s_kv, s_p)


def _allreduce_small(pack):
    R = pack.shape[0]
    r8 = R // NDEV

    def body(p_ref, full, land, red, ssem, rsem, ssem2, rsem2, lsem):
        x, y, c = _coords()
        me = 4 * x + 2 * y + c

        def peer(mask):
            return (1 - x if mask & 4 else x, 1 - y if mask & 2 else y, 1 - c if mask & 1 else c)

        def idx(p):
            return 4 * p[0] + 2 * p[1] + p[2]

        own = pltpu.make_async_copy(p_ref.at[pl.ds(me * r8, r8), :], land.at[me], lsem.at[0])
        own.start()
        sent = []
        for mask in range(1, NDEV):
            p = peer(mask)
            cp = pltpu.make_async_remote_copy(src_ref=p_ref.at[pl.ds(idx(p) * r8, r8), :], dst_ref=land.at[me],
                                              send_sem=ssem.at[mask - 1], recv_sem=rsem.at[mask - 1], device_id=p,
                                              device_id_type=MESH)
            cp.start()
            sent.append(cp)
        for mask in range(1, NDEV):
            got = land.at[idx(peer(mask))]
            pltpu.make_async_remote_copy(src_ref=got, dst_ref=got, send_sem=ssem.at[mask - 1], recv_sem=rsem.at[mask - 1],
                                         device_id=(x, y, c), device_id_type=MESH).wait_recv()
        own.wait()
        for cp in sent:
            cp.wait_send()
        acc = land[0]
        for k in range(1, NDEV):
            acc = acc + land[k]
        red[...] = acc
        mine = full.at[pl.ds(me * r8, r8), :]
        own2 = pltpu.make_async_copy(red, mine, lsem.at[1])
        own2.start()
        sent2 = []
        for mask in range(1, NDEV):
            cp = pltpu.make_async_remote_copy(src_ref=red, dst_ref=mine, send_sem=ssem2.at[mask - 1],
                                              recv_sem=rsem2.at[mask - 1], device_id=peer(mask), device_id_type=MESH)
            cp.start()
            sent2.append(cp)
        for mask in range(1, NDEV):
            got = full.at[pl.ds(idx(peer(mask)) * r8, r8), :]
            pltpu.make_async_remote_copy(src_ref=got, dst_ref=got, send_sem=ssem2.at[mask - 1], recv_sem=rsem2.at[mask - 1],
                                         device_id=(x, y, c), device_id_type=MESH).wait_recv()
        own2.wait()
        for cp in sent2:
            cp.wait_send()

    return _pcall(
        body, name="allreduce_small", in_specs=[HBM], out_specs=HBM, out_shape=SDS((R, 128), F32),
        scratch=[pltpu.VMEM((NDEV, r8, 128), F32), pltpu.VMEM((r8, 128), F32)]
        + [pltpu.SemaphoreType.DMA((NDEV - 1,))] * 4 + [pltpu.SemaphoreType.DMA((2,))], vmem_mb=32,
    )(pack)


def _mm_in(x, g, w):
    S = x.shape[0]
    tm, tn = min(1024, S), 1280

    def body(x_ref, g_ref, w_ref, z_ref, h_ref, hs):
        @pl.when(pl.program_id(1) == 0)
        def _():
            xv = x_ref[...]
            hb = (xv * _rms_scale(xv) * g_ref[...]).astype(BF16)
            hs[...] = hb
            h_ref[...] = hb

        z_ref[...] = _dot(hs[...], w_ref[...]).astype(BF16)

    return _pcall(
        body, name="mm_in", grid=(S // tm, NIN // tn),
        in_specs=[pl.BlockSpec((tm, D), lambda i, j: (i, 0)), _full((1, D)), pl.BlockSpec((D, tn), lambda i, j: (0, j))],
        out_specs=[pl.BlockSpec((tm, tn), lambda i, j: (i, j)), pl.BlockSpec((tm, D), lambda i, j: (i, 0))],
        out_shape=[SDS((S, NIN), BF16), SDS((S, D), BF16)], scratch=[pltpu.VMEM((tm, D), BF16)],
    )(x, g, w)


def _chunk_mask():
    ri = lax.broadcasted_iota(jnp.int32, (GB, GB), 0)
    ci = lax.broadcasted_iota(jnp.int32, (GB, GB), 1)
    return (ri >= 64) | (ci < 64)


def _layernorm_parts(v):
    mu = jnp.mean(v, axis=-1, keepdims=True)
    d = v - mu
    rs = lax.rsqrt(jnp.mean(d * d, axis=-1, keepdims=True) + EPS)
    return d * rs, rs


def _branch_a(z, lg, lb, ws, bsb):
    S = z.shape[0]
    T = min(512, S)

    def body(zu, zv, zg, lg_r, lb_r, ws_r, bs_r, ya):
        vhat, _ = _layernorm_parts(zv[...].astype(F32))
        vnb = (vhat * lg_r[...] + lb_r[...]).astype(BF16)
        sil, _ = _silu_parts(zg[...].astype(F32))
        t = zu[...].astype(F32) * sil
        mask = _chunk_mask()
        for g in range(NG):
            wg = jnp.where(mask, ws_r[g], 0.0).astype(BF16)
            cs = slice(g * GB, (g + 1) * GB)
            for n in range(T // GB):
                rs = slice(n * GB, (n + 1) * GB)
                sv = _dot(wg, vnb[rs, cs]) + bs_r[g]
                ya[rs, cs] = (t[rs, cs] * sv).astype(BF16)

    zs = lambda k: pl.BlockSpec((T, D), lambda i: (i, k))
    return _pcall(
        body, name="branch_a", grid=(S // T,),
        in_specs=[zs(0), zs(1), zs(2), _full((1, D)), _full((1, D)), _full((NG, GB, GB)), _full((NG, GB, GB))],
        out_specs=pl.BlockSpec((T, D), lambda i: (i, 0)), out_shape=SDS((S, D), BF16),
    )(z, z, z, lg, lb, ws, bsb)


def _softplus_neg(lam):
    e = jnp.exp(-jnp.abs(lam))
    l1p = jnp.where(e < 1e-2, e * (1.0 - e * (0.5 - e * (1.0 / 3.0))), jnp.log(1.0 + e))
    return jnp.maximum(-lam, 0.0) + l1p


def _lru_pre(xk, cw, cb, wr_r, br, wi_r, bi, lam):
    xc = cb + (((xk[0] * cw[0:1] + xk[1] * cw[1:2]) + xk[2] * cw[2:3]) + xk[3] * cw[3:4])
    xcb = xc.astype(BF16)
    pr, pi = [], []
    for h in range(NG):
        cs = slice(h * GB, (h + 1) * GB)
        pr.append(_dot(xcb[:, cs], wr_r[h].astype(BF16)))
        pi.append(_dot(xcb[:, cs], wi_r[h].astype(BF16)))
    r = jax.nn.sigmoid(jnp.concatenate(pr, axis=1) + br)
    ig = jax.nn.sigmoid(jnp.concatenate(pi, axis=1) + bi)
    sp = _softplus_neg(lam)
    la = (-LRU_C * sp) * r
    a = jnp.exp(la)
    a2 = a * a
    mult = jnp.sqrt(-jnp.tanh(la) * (a2 + 1.0))
    return xc, xcb, r, ig, sp, a, a2, mult


def _tile_rows():
    return lax.broadcasted_iota(jnp.int32, (8, D), 0)


def _scan_forward(a_s, u_s, h_s, hcar, T):
    row = _tile_rows()

    def tile(i, hp):
        o = pl.multiple_of(i * 8, 8)
        A = a_s[pl.ds(o, 8), :]
        U = u_s[pl.ds(o, 8), :]
        for s in (1, 2, 4):
            m = row >= s
            U = jnp.where(m, U + A * pltpu.roll(U, s, 0), U)
            A = jnp.where(m, A * pltpu.roll(A, s, 0), A)
        H = U + A * hp
        h_s[pl.ds(o, 8), :] = H
        return jnp.broadcast_to(H[7:8, :], (8, D))

    hcar[...] = lax.fori_loop(0, T // 8, tile, hcar[...])


def _scan_reverse(b_s, d_s, l_s, lcar, T):
    row = _tile_rows()
    n = T // 8

    def tile(i, lp):
        o = pl.multiple_of((n - 1 - i) * 8, 8)
        B = b_s[pl.ds(o, 8), :]
        U = d_s[pl.ds(o, 8), :]
        for s in (1, 2, 4):
            m = row < 8 - s
            U = jnp.where(m, U + B * pltpu.roll(U, 8 - s, 0), U)
            B = jnp.where(m, B * pltpu.roll(B, 8 - s, 0), B)
        Lm = U + B * lp
        l_s[pl.ds(o, 8), :] = Lm
        return jnp.broadcast_to(Lm[0:1, :], (8, D))

    lcar[...] = lax.fori_loop(0, n, tile, lcar[...])


def _branch_b(z, cw, cb, wr, br, wi, bi, lam):
    S = z.shape[0]
    T = min(256, S)

    def body(zxb, zgb, cw_r, cb_r, wr_r, br_r, wi_r, bi_r, lam_r, yb, hs_o, xpad, a_s, u_s, hcar):
        @pl.when(pl.program_id(0) == 0)
        def _():
            xpad[pl.ds(0, 8), :] = jnp.zeros((8, D), F32)
            hcar[...] = jnp.zeros((8, D), F32)

        xpad[pl.ds(8, T), :] = zxb[...].astype(F32)
        xk = [xpad[pl.ds(5 + k, T), :] for k in range(4)]
        xc, _, _, ig, _, a, _, mult = _lru_pre(xk, cw_r[...], cb_r[...], wr_r, br_r[...], wi_r, bi_r[...], lam_r[...])
        a_s[...] = a
        u_s[...] = mult * (ig * xc)
        _scan_forward(a_s, u_s, hs_o, hcar, T)
        xpad[pl.ds(0, 8), :] = xpad[pl.ds(T, 8), :]
        sil, _ = _silu_parts(zgb[...].astype(F32))
        yb[...] = (hs_o[...] * sil).astype(BF16)

    zs = lambda k: pl.BlockSpec((T, D), lambda i: (i, k))
    row = pl.BlockSpec((T, D), lambda i: (i, 0))
    return _pcall(
        body, name="branch_b", grid=(S // T,),
        in_specs=[zs(3), zs(4), _full((8, D)), _full((1, D)), _full((NG, GB, GB)), _full((1, D)), _full((NG, GB, GB)),
                  _full((1, D)), _full((1, D))],
        out_specs=[row, row], out_shape=[SDS((S, D), BF16), SDS((S, D), F32)],
        scratch=[pltpu.VMEM((T + 8, D), F32), pltpu.VMEM((T, D), F32), pltpu.VMEM((T, D), F32), pltpu.VMEM((8, D), F32)],
    )(z, z, cw, cb, wr, br, wi, bi, lam)


def _kv(mem, g, wkv):
    def body(m_ref, g_ref, w_ref, kv_ref):
        m = m_ref[...]
        mn = (m * _rms_scale(m) * g_ref[...]).astype(BF16)
        kv_ref[...] = _dot(mn, w_ref[...]).astype(BF16)

    return _pcall(body, name="mem_kv", in_specs=[VMEM] * 3, out_specs=VMEM, out_shape=SDS((MEM, 2 * D), BF16),
                  vmem_mb=32)(mem, g, wkv)


def _softmax_rows(s):
    e = jnp.exp(s - jnp.max(s, axis=-1, keepdims=True))
    return e / jnp.sum(e, axis=-1, keepdims=True)


def _branch_c(z, kv):
    S = z.shape[0]
    T = min(512, S)

    def body(zq, zg, kv_r, yc):
        sil, _ = _silu_parts(zg[...].astype(F32))
        for h in range(NH):
            cs = slice(h * HD, (h + 1) * HD)
            p = _softmax_rows(_dot_nt(zq[:, cs], kv_r[:, cs]) * (HD ** -0.5))
            att = _dot(p.astype(BF16), kv_r[:, D + h * HD:D + (h + 1) * HD])
            yc[:, cs] = (att * sil[:, cs]).astype(BF16)

    zs = lambda k: pl.BlockSpec((T, D), lambda i: (i, k))
    return _pcall(body, name="branch_c", grid=(S // T,), in_specs=[zs(5), zs(6), _full((MEM, 2 * D))],
                  out_specs=pl.BlockSpec((T, D), lambda i: (i, 0)), out_shape=SDS((S, D), BF16))(z, z, kv)


def _merge_out(ya, yb, yc, z, wp, x, pg):
    S = x.shape[0]
    T = min(256, S)

    def body(ya_r, yb_r, yc_r, m0, m1, m2, wp_r, x_r, pg_r, pa_o, pb_o, pc_o, mg_o, o_o, xn_o):
        merged = None
        for y_r, ml, p_o, k in ((ya_r, m0, pa_o, 0), (yb_r, m1, pb_o, 1), (yc_r, m2, pc_o, 2)):
            p = _dot(y_r[...], wp_r[k])
            p_o[...] = p.astype(BF16)
            t = jax.nn.sigmoid(ml[...].astype(F32)) * p
            merged = t if merged is None else merged + t
        mb = merged.astype(BF16)
        mg_o[...] = mb
        o = _dot(mb, wp_r[3])
        o_o[...] = o.astype(BF16)
        xn_o[...] = x_r[...] + o * _rms_scale(o) * pg_r[...]

    row = pl.BlockSpec((T, D), lambda i: (i, 0))
    zs = lambda k: pl.BlockSpec((T, D), lambda i: (i, k))
    return _pcall(
        body, name="merge_out", grid=(S // T,),
        in_specs=[row, row, row, zs(7), zs(8), zs(9), _full((4, D, D)), row, _full((1, D))],
        out_specs=[row] * 6, out_shape=[SDS((S, D), BF16)] * 5 + [SDS((S, D), F32)], vmem_mb=56,
    )(ya, yb, yc, z, z, z, wp, x, pg)


def _loss_head(y, t):
    S = y.shape[0]
    T = min(512, S)

    def body(y_r, t_r, loss_o, dy_o):
        @pl.when(pl.program_id(0) == 0)
        def _():
            loss_o[...] = jnp.zeros((1, 1), F32)

        e = y_r[...] - t_r[...]
        dy_o[...] = e * (1.0 / D)
        loss_o[...] += 0.5 * _rowsum(jnp.sum(e * e, axis=1, keepdims=True) * (1.0 / D))

    row = pl.BlockSpec((T, D), lambda i: (i, 0))
    return _pcall(body, name="loss_head", grid=(S // T,), in_specs=[row, row], out_specs=[_full((1, 1)), row],
                  out_shape=[SDS((1, 1), F32), SDS((S, D), F32)])(y, t)


def _accumulate(first, ref, val):
    @pl.when(first)
    def _():
        ref[...] = val

    @pl.when(jnp.logical_not(first))
    def _():
        ref[...] += val


def _out_bwd(dxn, o, pg, wp, z, pa, pb, pc):
    S = dxn.shape[0]
    T = min(256, S)

    def body(dy_r, o_r, pg_r, wp_r, m0, m1, m2, pa_r, pb_r, pc_r, do_o, dpa_o, dpb_o, dpc_o, dya_o, dyb_o, dyc_o, dz_o, dg_o):
        dy = dy_r[...]
        o = o_r[...].astype(F32)
        r2 = _rms_scale(o)
        w = dy * pg_r[...]
        do = r2 * w - o * (r2 * r2 * r2) * jnp.mean(w * o, axis=-1, keepdims=True)
        _accumulate(pl.program_id(0) == 0, dg_o, _rowsum(dy * o * r2))
        dob = do.astype(BF16)
        do_o[...] = dob
        dm = _dot_nt(dob, wp_r[3])
        for k, (ml, p_r, dp_o, dy_o) in enumerate(((m0, pa_r, dpa_o, dya_o), (m1, pb_r, dpb_o, dyb_o), (m2, pc_r, dpc_o, dyc_o))):
            gk = jax.nn.sigmoid(ml[...].astype(F32))
            dz_o[k] = (dm * p_r[...].astype(F32) * gk * (1.0 - gk)).astype(BF16)
            dpk = (gk * dm).astype(BF16)
            dp_o[...] = dpk
            dy_o[...] = _dot_nt(dpk, wp_r[k]).astype(BF16)

    row = pl.BlockSpec((T, D), lambda i: (i, 0))
    zs = lambda k: pl.BlockSpec((T, D), lambda i: (i, k))
    return _pcall(
        body, name="out_bwd", grid=(S // T,),
        in_specs=[row, row, _full((1, D)), _full((4, D, D)), zs(7), zs(8), zs(9), row, row, row],
        out_specs=[row] * 7 + [pl.BlockSpec((3, T, D), lambda i: (1, i, 0)), _full((1, D))],
        out_shape=[SDS((S, D), BF16)] * 7 + [SDS((10, S, D), BF16), SDS((1, D), F32)], vmem_mb=56,
    )(dxn, o, pg, wp, z, z, z, pa, pb, pc)


def _branch_a_bwd(z, dya, lg, lb, ws, bsb, dz):
    S = z.shape[0]
    T = min(512, S)
    nblk = S // T

    def body(zu, zv, zg, dy_r, lg_r, lb_r, ws_r, bs_r, dz_in, dz_o, dws_o, dbs_o, dlg_o, dlb_o, dvn_s, bacc):
        i = pl.program_id(0)

        @pl.when(i == 0)
        def _():
            dws_o[...] = jnp.zeros((NG, GB, GB), F32)
            bacc[...] = jnp.zeros((NG, GB, GB), F32)

        vhat, rs = _layernorm_parts(zv[...].astype(F32))
        vnb = (vhat * lg_r[...] + lb_r[...]).astype(BF16)
        ga = zg[...].astype(F32)
        sil, dsil = _silu_parts(ga)
        u = zu[...].astype(F32)
        dy = dy_r[...].astype(F32)
        t = dy * sil
        dsv_all = t * u
        dga_pre = dy * u * dsil
        mask = _chunk_mask()
        for g in range(NG):
            wf = jnp.where(mask, ws_r[g], 0.0)
            wg = wf.astype(BF16)
            wgt = wf.T.astype(BF16)
            cs = slice(g * GB, (g + 1) * GB)
            dw = jnp.zeros((GB, GB), F32)
            db = jnp.zeros((GB, GB), F32)
            for n in range(T // GB):
                rsl = slice(n * GB, (n + 1) * GB)
                vb = vnb[rsl, cs]
                sv = _dot(wg, vb) + bs_r[g]
                dz_o[0, rsl, cs] = (t[rsl, cs] * sv).astype(BF16)
                dz_o[2, rsl, cs] = (dga_pre[rsl, cs] * sv).astype(BF16)
                dsv = dsv_all[rsl, cs]
                dsb = dsv.astype(BF16)
                dvn_s[rsl, cs] = _dot(wgt, dsb)
                dw = dw + _dot_nt(dsb, vb)
                db = db + dsv
            dws_o[g] += jnp.where(mask, dw, 0.0)
            bacc[g] += db
        dvn = dvn_s[...]
        dvh = dvn * lg_r[...]
        dv = rs * (dvh - jnp.mean(dvh, axis=-1, keepdims=True) - vhat * jnp.mean(dvh * vhat, axis=-1, keepdims=True))
        dz_o[1] = dv.astype(BF16)
        _accumulate(i == 0, dlg_o, _rowsum(dvn * vhat))
        _accumulate(i == 0, dlb_o, _rowsum(dvn))

        @pl.when(i == nblk - 1)
        def _():
            for g in range(NG):
                dbs_o[g:g + 1, :] = _rowsum(bacc[g].T)

    zs = lambda k: pl.BlockSpec((T, D), lambda i: (i, k))
    return _pcall(
        body, name="branch_a_bwd", grid=(nblk,),
        in_specs=[zs(0), zs(1), zs(2), pl.BlockSpec((T, D), lambda i: (i, 0)), _full((1, D)), _full((1, D)),
                  _full((NG, GB, GB)), _full((NG, GB, GB)), HBM],
        out_specs=[pl.BlockSpec((3, T, D), lambda i: (0, i, 0)), _full((NG, GB, GB)), _full((NG, GB)), _full((1, D)),
                   _full((1, D))],
        out_shape=[SDS((10, S, D), BF16), SDS((NG, GB, GB), F32), SDS((NG, GB), F32), SDS((1, D), F32), SDS((1, D), F32)],
        scratch=[pltpu.VMEM((T, D), F32), pltpu.VMEM((NG, GB, GB), F32)], aliases={8: 0},
    )(z, z, z, dya, lg, lb, ws, bsb, dz)


def _branch_b_bwd(z, hs, dyb, cw, cb, wr, br, wi, bi, lam, dz):
    S = z.shape[0]
    T = min(256, S)
    nblk = S // T

    def body(zxb, zprev, zgb, hs_r, hprev_r, dy_r, cw_r, cb_r, wr_r, br_r, wi_r, bi_r, lam_r, dz_in,
             dz_o, dcw_o, dcb_o, dwr_o, dbr_o, dwi_o, dbi_o, dlam_o, xpad, hpad, apad, dpad, b_s, d_s, l_s, lcar):
        i = pl.program_id(0)
        blk = nblk - 1 - i
        first = i == 0

        @pl.when(first)
        def _():
            apad[pl.ds(T, 8), :] = jnp.zeros((8, D), F32)
            dpad[pl.ds(T, 8), :] = jnp.zeros((8, D), F32)
            lcar[...] = jnp.zeros((8, D), F32)
            dcw_o[...] = jnp.zeros((8, D), F32)
            dwr_o[...] = jnp.zeros((NG, GB, GB), F32)
            dwi_o[...] = jnp.zeros((NG, GB, GB), F32)

        keep = (blk > 0).astype(F32)
        xpad[pl.ds(8, T), :] = zxb[...].astype(F32)
        xpad[pl.ds(0, 8), :] = zprev[...].astype(F32)[8:16, :] * keep
        hsv = hs_r[...]
        hpad[pl.ds(8, T), :] = hsv
        hpad[pl.ds(0, 8), :] = hprev_r[...] * keep
        xk = [xpad[pl.ds(5 + k, T), :] for k in range(4)]
        cw = cw_r[...]
        lam = lam_r[...]
        xc, xcb, r, ig, sp, a, a2, mult = _lru_pre(xk, cw, cb_r[...], wr_r, br_r[...], wi_r, bi_r[...], lam)
        gb = zgb[...].astype(F32)
        sil, dsil = _silu_parts(gb)
        dy = dy_r[...].astype(F32)
        dz_o[1] = (dy * hsv * dsil).astype(BF16)
        apad[pl.ds(0, T), :] = a
        b_s[...] = apad[pl.ds(1, T), :]
        d_s[...] = dy * sil
        _scan_reverse(b_s, d_s, l_s, lcar, T)
        lm = l_s[...]
        da = lm * hpad[pl.ds(7, T), :]
        t = lm * mult
        dmult = lm * ig * xc
        di = t * xc
        dxc = t * ig
        dl = da * a - dmult * a2 / mult
        dr = dl * (-LRU_C * sp)
        dsp = _rowsum(dl * r) * (-LRU_C)
        dpr = dr * r * (1.0 - r)
        dpi = di * ig * (1.0 - ig)
        _accumulate(first, dlam_o, dsp * (-jax.nn.sigmoid(-lam)))
        _accumulate(first, dbr_o, _rowsum(dpr))
        _accumulate(first, dbi_o, _rowsum(dpi))
        dprb = dpr.astype(BF16)
        dpib = dpi.astype(BF16)
        back = []
        for h in range(NG):
            cs = slice(h * GB, (h + 1) * GB)
            back.append(_dot_nt(dprb[:, cs], wr_r[h].astype(BF16)) + _dot_nt(dpib[:, cs], wi_r[h].astype(BF16)))
            dwr_o[h] += _dot_tn(xcb[:, cs], dprb[:, cs])
            dwi_o[h] += _dot_tn(xcb[:, cs], dpib[:, cs])
        dxc = dxc + jnp.concatenate(back, axis=1)
        _accumulate(first, dcb_o, _rowsum(dxc))
        for k in range(4):
            dcw_o[k:k + 1, :] += _rowsum(xk[k] * dxc)
        dpad[pl.ds(0, T), :] = dxc
        dxb = ((dpad[pl.ds(3, T), :] * cw[0:1] + dpad[pl.ds(2, T), :] * cw[1:2]) + dpad[pl.ds(1, T), :] * cw[2:3]) + dxc * cw[3:4]
        dz_o[0] = dxb.astype(BF16)
        apad[pl.ds(T, 8), :] = apad[pl.ds(0, 8), :]
        dpad[pl.ds(T, 8), :] = dpad[pl.ds(0, 8), :]

    rev = lambda k: pl.BlockSpec((T, D), lambda i: (nblk - 1 - i, k))
    prev16 = pl.BlockSpec((16, D), lambda i: (jnp.maximum((nblk - 1 - i) * (T // 16) - 1, 0), 3))
    prev8 = pl.BlockSpec((8, D), lambda i: (jnp.maximum((nblk - 1 - i) * (T // 8) - 1, 0), 0))
    vec, mat = _full((1, D)), _full((NG, GB, GB))
    return _pcall(
        body, name="branch_b_bwd", grid=(nblk,),
        in_specs=[rev(3), prev16, rev(4), rev(0), prev8, rev(0), _full((8, D)), vec, mat, vec, mat, vec, vec, HBM],
        out_specs=[pl.BlockSpec((2, T, D), lambda i: (3, nblk - 1 - i, 0)), _full((8, D)), vec, mat, vec, mat, vec, vec],
        out_shape=[SDS((10, S, D), BF16), SDS((8, D), F32), SDS((1, D), F32), SDS((NG, GB, GB), F32), SDS((1, D), F32),
                   SDS((NG, GB, GB), F32), SDS((1, D), F32), SDS((1, D), F32)],
        scratch=[pltpu.VMEM((T + 8, D), F32)] * 4 + [pltpu.VMEM((T, D), F32)] * 3 + [pltpu.VMEM((8, D), F32)],
        aliases={13: 0}, vmem_mb=56,
    )(z, z, z, hs, hs, dyb, cw, cb, wr, br, wi, bi, lam, dz)


def _branch_c_bwd(z, kv, dyc, dz):
    S = z.shape[0]
    T = min(512, S)

    def body(zq, zg, kv_r, dy_r, dz_in, dz_o, dkv_o):
        @pl.when(pl.program_id(0) == 0)
        def _():
            dkv_o[...] = jnp.zeros((MEM, 2 * D), F32)

        gc = zg[...].astype(F32)
        sil, dsil = _silu_parts(gc)
        dy = dy_r[...].astype(F32)
        datt = dy * sil
        dgc_pre = dy * dsil
        scale = HD ** -0.5
        for h in range(NH):
            cs = slice(h * HD, (h + 1) * HD)
            vs = slice(D + h * HD, D + (h + 1) * HD)
            qh = zq[:, cs]
            p = _softmax_rows(_dot_nt(qh, kv_r[:, cs]) * scale)
            pb = p.astype(BF16)
            att = _dot(pb, kv_r[:, vs])
            dz_o[1, :, cs] = (dgc_pre[:, cs] * att).astype(BF16)
            dab = datt[:, cs].astype(BF16)
            dp = _dot_nt(dab, kv_r[:, vs])
            ds = (p * (dp - jnp.sum(p * dp, axis=-1, keepdims=True)) * scale).astype(BF16)
            dz_o[0, :, cs] = _dot(ds, kv_r[:, cs]).astype(BF16)
            dkv_o[:, cs] += _dot_tn(ds, qh)
            dkv_o[:, vs] += _dot_tn(pb, dab)

    zs = lambda k: pl.BlockSpec((T, D), lambda i: (i, k))
    return _pcall(
        body, name="branch_c_bwd", grid=(S // T,),
        in_specs=[zs(5), zs(6), _full((MEM, 2 * D)), pl.BlockSpec((T, D), lambda i: (i, 0)), HBM],
        out_specs=[pl.BlockSpec((2, T, D), lambda i: (4, i, 0)), _full((MEM, 2 * D))],
        out_shape=[SDS((10, S, D), BF16), SDS((MEM, 2 * D), F32)], aliases={4: 0},
    )(z, z, kv, dyc, dz)


def _mm_dh(dz, w, x, dxn, g):
    S = x.shape[0]
    tm = min(1024, S)

    def body(dz_r, w_r, x_r, dxn_r, g_r, dx_o, dg_o, acc):
        i, k = pl.program_id(0), pl.program_id(1)
        _accumulate(k == 0, acc, _dot_nt(dz_r[0], w_r[...]))

        @pl.when(k == 9)
        def _():
            dh = acc[...]
            xv = x_r[...]
            r1 = _rms_scale(xv)
            wv = dh * g_r[...]
            dx_o[...] = dxn_r[...] + r1 * wv - xv * (r1 * r1 * r1) * jnp.mean(wv * xv, axis=-1, keepdims=True)
            _accumulate(i == 0, dg_o, _rowsum(dh * xv * r1))

    row = pl.BlockSpec((tm, D), lambda i, k: (i, 0))
    return _pcall(
        body, name="mm_dh", grid=(S // tm, 10),
        in_specs=[pl.BlockSpec((1, tm, D), lambda i, k: (k, i, 0)), pl.BlockSpec((D, D), lambda i, k: (0, _dz_col(k))),
                  row, row, _full((1, D))],
        out_specs=[row, _full((1, D))], out_shape=[SDS((S, D), F32), SDS((1, D), F32)],
        scratch=[pltpu.VMEM((tm, D), F32)],
    )(dz, w, x, dxn, g)


def _mm_dwin(h, dz):
    S = h.shape[0]
    tk = min(1024, S)
    nk = S // tk

    def body(h_r, dz_r, o_r, acc):
        k = pl.program_id(1)
        _accumulate(k == 0, acc, _dot_tn(h_r[...], dz_r[0]))

        @pl.when(k == nk - 1)
        def _():
            o_r[...] = acc[...].astype(BF16)

    return _pcall(
        body, name="mm_dwin", grid=(10, nk),
        in_specs=[pl.BlockSpec((tk, D), lambda n, k: (k, 0)), pl.BlockSpec((1, tk, D), lambda n, k: (n, k, 0))],
        out_specs=pl.BlockSpec((D, D), lambda n, k: (0, _dz_col(n))), out_shape=SDS((D, NIN), BF16),
        scratch=[pltpu.VMEM((D, D), F32)],
    )(h, dz)


def _mm_tn(a, b):
    S = a.shape[0]
    tk = min(1024, S)
    nk = S // tk

    def body(a_r, b_r, o_r, acc):
        k = pl.program_id(0)
        _accumulate(k == 0, acc, _dot_tn(a_r[...], b_r[...]))

        @pl.when(k == nk - 1)
        def _():
            o_r[...] = acc[...].astype(BF16)

    blk = pl.BlockSpec((tk, D), lambda k: (k, 0))
    return _pcall(body, name="mm_tn", grid=(nk,), in_specs=[blk, blk], out_specs=_full((D, D)),
                  out_shape=SDS((D, D), BF16), scratch=[pltpu.VMEM((D, D), F32)])(a, b)


def _mem_bwd(mem, g, wkv, dkv, dg_acc):
    def body(m_ref, g_ref, w_ref, dkv_ref, acc_ref, dw_ref, dg_ref):
        m = m_ref[...]
        mr = m * _rms_scale(m)
        mn = (mr * g_ref[...]).astype(BF16)
        dkb = dkv_ref[...].astype(BF16)
        dw_ref[...] = _dot_tn(mn, dkb).astype(BF16)
        dg_ref[...] = acc_ref[...] + _rowsum(_dot_nt(dkb, w_ref[...]) * mr)

    return _pcall(body, name="mem_bwd", in_specs=[VMEM] * 5, out_specs=[VMEM] * 2,
                  out_shape=[SDS((D, 2 * D), BF16), SDS((1, D), F32)], vmem_mb=48)(mem, g, wkv, dkv, dg_acc)


def _adamw_math(w, g, m, v):
    m2 = ADAM_B1 * m + (1.0 - ADAM_B1) * g
    v2 = ADAM_B2 * v + (1.0 - ADAM_B2) * (g * g)
    mh = m2 / (1.0 - ADAM_B1 ** ADAM_STEP)
    vh = v2 / (1.0 - ADAM_B2 ** ADAM_STEP)
    return -ADAM_LR * (mh / (jnp.sqrt(vh) + ADAM_EPS) + ADAM_WD * w), m2, v2


def _adamw_stack(w, m, v, gs, which=None, rows=128):
    L, R, C = w.shape

    def body(w_r, m_r, v_r, *rest):
        g_rs, (g_o, d_o, m_o, v_o) = rest[:L], rest[L:]
        l = pl.program_id(0)
        g = g_rs[0][...]
        for a in range(1, L):
            g = jnp.where(l == a, g_rs[a][...], g)
        d, m2, v2 = _adamw_math(w_r[...], g, m_r[...], v_r[...])
        g_o[...] = g
        d_o[...] = d
        m_o[...] = m2
        v_o[...] = v2

    st = pl.BlockSpec((None, rows, C), lambda l, i: (l, i, 0))

    def gspec(a):
        if which is None:
            return pl.BlockSpec((rows, C), lambda l, i: (jnp.where(l == a, i, 0), 0))
        return pl.BlockSpec((None, rows, C), lambda l, i: (which, jnp.where(l == a, i, 0), 0))

    return _pcall(body, name="adamw_shard", grid=(L, R // rows), in_specs=[st] * 3 + [gspec(a) for a in range(L)],
                  out_specs=[st] * 4, out_shape=[SDS(w.shape, F32)] * 4)(w, m, v, *gs)


def _adamw_flat(w, m, v, g, rows):
    R, C = w.shape

    def body(w_r, m_r, v_r, g_r, d_o, m_o, v_o):
        d, m2, v2 = _adamw_math(w_r[...], g_r[...], m_r[...], v_r[...])
        d_o[...] = d
        m_o[...] = m2
        v_o[...] = v2

    blk = pl.BlockSpec((rows, C), lambda i: (i, 0))
    return _pcall(body, name="adamw_flat", grid=(R // rows,), in_specs=[blk] * 4, out_specs=[blk] * 3,
                  out_shape=[SDS((R, C), F32)] * 3)(w, m, v, g)


_SMALL = ("mem_norm_g", "pre_norm_g", "post_norm_g", "gmlp_ln_g", "gmlp_ln_b", "gmlp_ws", "gmlp_bs", "conv_b", "lru_wr",
          "lru_br", "lru_wi", "lru_bi", "lru_lambda")


def _pack_small(parts, conv_w_part):
    rows = [parts[n].reshape(-1, 128) for n in _SMALL] + [conv_w_part.reshape(-1, 128)]
    used = sum(r.shape[0] for r in rows)
    rows.append(jnp.zeros((SMALL_ROWS - used, 128), F32))
    return jnp.concatenate(rows, axis=0)


def _unpack_small(pack, shapes):
    out, at = {}, 0
    for n in _SMALL:
        size = 1
        for s in shapes[n]:
            size *= s
        out[n] = pack[at:at + size // 128].reshape(shapes[n])
        at += size // 128
    return out, at


def kernel(x, mem, mem_norm_g, pre_norm_g, post_norm_g, w_in, gmlp_ln_g, gmlp_ln_b, gmlp_ws, gmlp_bs, conv_w, conv_b, lru_wr, lru_br, lru_wi, lru_bi, lru_lambda, w_kv, w_pa, w_pb, w_pc, w_out, loss_target, m_mem_norm_g, m_pre_norm_g, m_post_norm_g, m_w_in, m_gmlp_ln_g, m_gmlp_ln_b, m_gmlp_ws, m_gmlp_bs, m_conv_w, m_conv_b, m_lru_wr, m_lru_br, m_lru_wi, m_lru_bi, m_lru_lambda, m_w_kv, m_w_pa, m_w_pb, m_w_pc, m_w_out, v_mem_norm_g, v_pre_norm_g, v_post_norm_g, v_w_in, v_gmlp_ln_g, v_gmlp_ln_b, v_gmlp_ws, v_gmlp_bs, v_conv_w, v_conv_b, v_lru_wr, v_lru_br, v_lru_wi, v_lru_bi, v_lru_lambda, v_w_kv, v_w_pa, v_w_pb, v_w_pc, v_w_out):
    L = w_in.shape[0]
    S = x.shape[1]
    xs = [x[0]]
    mem2 = mem[0]
    mg = mem_norm_g.reshape(1, D)
    vec = lambda a, l: a[l].reshape(1, D)

    win_b = _cast_stack(w_in, 512)
    wkv_b = _cast_stack(w_kv, 1024)
    wp_b = _cast_proj(w_pa, w_pb, w_pc, w_out)
    cw8 = jnp.pad(conv_w, ((0, 0), (0, 4), (0, 0)))
    W = [_gather_layer(l, win_b, wkv_b, wp_b, cw8) for l in range(L)]

    saved = []
    for l in range(L):
        Win, Wkv, Wp, Cw = W[l]
        z, h = _mm_in(xs[l], vec(pre_norm_g, l), Win)
        bsb = jnp.broadcast_to(gmlp_bs[l][:, :, None], (NG, GB, GB))
        ya = _branch_a(z, vec(gmlp_ln_g, l), vec(gmlp_ln_b, l), gmlp_ws[l], bsb)
        yb, hs = _branch_b(z, Cw, vec(conv_b, l), lru_wr[l], vec(lru_br, l), lru_wi[l], vec(lru_bi, l), vec(lru_lambda, l))
        kv = _kv(mem2, mg, Wkv)
        yc = _branch_c(z, kv)
        pa, pb, pc, mgd, o, xn = _merge_out(ya, yb, yc, z, Wp, xs[l], vec(post_norm_g, l))
        xs.append(xn)
        saved.append((z, h, ya, yb, yc, hs, kv, pa, pb, pc, mgd, o, bsb))

    loss11, dxn = _loss_head(xs[L], loss_target[0])
    loss = lax.psum(loss11[0, 0], ("x", "y", "c"))

    small = {n: [None] * L for n in _SMALL}
    dconv_w = [None] * L
    lands = [None] * L
    dg_mem = jnp.zeros((1, D), F32)
    for l in reversed(range(L)):
        Win, Wkv, Wp, Cw = W[l]
        z, h, ya, yb, yc, hs, kv, pa, pb, pc, mgd, o, bsb = saved[l]
        do, dpa, dpb, dpc, dya, dyb, dyc, dz, dgpost = _out_bwd(dxn, o, vec(post_norm_g, l), Wp, z, pa, pb, pc)
        dz, dws, dbs, dlg, dlb = _branch_a_bwd(z, dya, vec(gmlp_ln_g, l), vec(gmlp_ln_b, l), gmlp_ws[l], bsb, dz)
        dz, dcw, dcb, dwr, dbr, dwi, dbi, dlam = _branch_b_bwd(
            z, hs, dyb, Cw, vec(conv_b, l), lru_wr[l], vec(lru_br, l), lru_wi[l], vec(lru_bi, l), vec(lru_lambda, l), dz)
        dz, dkv = _branch_c_bwd(z, kv, dyc, dz)
        dx, dgpre = _mm_dh(dz, Win, xs[l], dxn, vec(pre_norm_g, l))
        g_in = _mm_dwin(h, dz)
        g_kv, dg_mem = _mem_bwd(mem2, mg, Wkv, dkv, dg_mem)
        lands[l] = _a2a_layer(g_in, g_kv, _mm_tn(ya, dpa), _mm_tn(yb, dpb), _mm_tn(yc, dpc), _mm_tn(mgd, do))
        for n, val in (("pre_norm_g", dgpre), ("post_norm_g", dgpost), ("gmlp_ln_g", dlg), ("gmlp_ln_b", dlb), ("gmlp_ws", dws),
                       ("gmlp_bs", dbs), ("conv_b", dcb), ("lru_wr", dwr), ("lru_br", dbr), ("lru_wi", dwi), ("lru_bi", dbi),
                       ("lru_lambda", dlam)):
            small[n][l] = val
        dconv_w[l] = dcw[0:4]
        dxn = dx
    grad_x = dxn.reshape(1, S, D)

    parts = {n: jnp.stack(small[n]) for n in _SMALL if n != "mem_norm_g"}
    parts["mem_norm_g"] = dg_mem
    gsum = _allreduce_small(_pack_small(parts, jnp.stack(dconv_w)))
    given = dict(mem_norm_g=(mem_norm_g, m_mem_norm_g, v_mem_norm_g), pre_norm_g=(pre_norm_g, m_pre_norm_g, v_pre_norm_g),
                 post_norm_g=(post_norm_g, m_post_norm_g, v_post_norm_g), gmlp_ln_g=(gmlp_ln_g, m_gmlp_ln_g, v_gmlp_ln_g),
                 gmlp_ln_b=(gmlp_ln_b, m_gmlp_ln_b, v_gmlp_ln_b), gmlp_ws=(gmlp_ws, m_gmlp_ws, v_gmlp_ws),
                 gmlp_bs=(gmlp_bs, m_gmlp_bs, v_gmlp_bs), conv_b=(conv_b, m_conv_b, v_conv_b), lru_wr=(lru_wr, m_lru_wr, v_lru_wr),
                 lru_br=(lru_br, m_lru_br, v_lru_br), lru_wi=(lru_wi, m_lru_wi, v_lru_wi), lru_bi=(lru_bi, m_lru_bi, v_lru_bi),
                 lru_lambda=(lru_lambda, m_lru_lambda, v_lru_lambda))
    shapes = {n: given[n][0].shape for n in _SMALL}
    zero_cw = jnp.zeros((L, 4, D), F32)
    packs = [_pack_small({n: given[n][k] for n in _SMALL}, zero_cw) for k in range(3)]
    dsm, msm, vsm = _adamw_flat(packs[0], packs[1], packs[2], gsum, 2560)
    out = {}
    g_small, at = _unpack_small(gsum, shapes)
    d_small, _ = _unpack_small(dsm, shapes)
    m_small, _ = _unpack_small(msm, shapes)
    v_small, _ = _unpack_small(vsm, shapes)
    for n in _SMALL:
        out[n] = (g_small[n], d_small[n], m_small[n], v_small[n])
    jx = 2 * lax.axis_index("x") + lax.axis_index("y")
    g_cw = lax.dynamic_slice_in_dim(gsum[at:at + L * 4 * D // 128].reshape(L * 4, D), jx * 256, 256, axis=1)
    d_cw, m_cw, v_cw = _adamw_flat(conv_w.reshape(L * 4, 256), m_conv_w.reshape(L * 4, 256), v_conv_w.reshape(L * 4, 256), g_cw, L * 4)
    out["conv_w"] = tuple(a.reshape(L, 4, 256) for a in (g_cw, d_cw, m_cw, v_cw))

    g_in, g_kv, g_p = [], [], []
    for l in range(L):
        lin, lkv, lp = lands[l]
        s_in = _sum_slots(lin, 128)
        s_kv = _sum_slots(lkv, 512)
        s_p = _sum_slots(lp.reshape(NDEV, 4 * lp.shape[2], D), 128).reshape(4, lp.shape[2], D)
        a, b, c = _sibling_exchange(s_in, s_kv, s_p)
        g_in.append(a)
        g_kv.append(b)
        g_p.append(c)
    out["w_in"] = _adamw_stack(w_in, m_w_in, v_w_in, g_in)
    out["w_kv"] = _adamw_stack(w_kv, m_w_kv, v_w_kv, g_kv)
    for k, (n, trio) in enumerate((("w_pa", (w_pa, m_w_pa, v_w_pa)), ("w_pb", (w_pb, m_w_pb, v_w_pb)),
                                   ("w_pc", (w_pc, m_w_pc, v_w_pc)), ("w_out", (w_out, m_w_out, v_w_out)))):
        out[n] = _adamw_stack(*trio, g_p, which=k)

    order = ("mem_norm_g", "pre_norm_g", "post_norm_g", "w_in", "gmlp_ln_g", "gmlp_ln_b", "gmlp_ws", "gmlp_bs", "conv_w", "conv_b",
             "lru_wr", "lru_br", "lru_wi", "lru_bi", "lru_lambda", "w_kv", "w_pa", "w_pb", "w_pc", "w_out")
    return (loss, grad_x, *[out[n][0] for n in order], *[out[n][1] for n in order], *[out[n][2] for n in order],
            *[out[n][3] for n in order])
```

```python
import functools

import jax
import jax.numpy as jnp
from jax import lax
from jax.experimental import pallas as pl
from jax.experimental.pallas import tpu as pltpu

F32 = jnp.float32
BF16 = jnp.bfloat16
SDS = jax.ShapeDtypeStruct
MESH = pl.DeviceIdType.MESH

D = 1024
NIN = 10 * D
MEM = 256
GB = 128
NG = 8
NH = 4
HD = D // NH
EPS = 1e-6
LRU_C = 8.0
ADAM_LR, ADAM_B1, ADAM_B2, ADAM_EPS, ADAM_WD, ADAM_STEP = 0.001, 0.9, 0.999, 1e-08, 0.01, 10
NDEV = 8
SMALL_ROWS = 12800

_CALL_KW = {}
HBM = pl.BlockSpec(memory_space=pltpu.HBM)
VMEM = pl.BlockSpec(memory_space=pltpu.VMEM)
SEM = pl.BlockSpec(memory_space=pltpu.SEMAPHORE)
ANY = pl.BlockSpec(memory_space=pl.ANY)
TOKEN = SDS((8, 128), F32)


def _pcall(body, *, name, in_specs, out_specs, out_shape, grid=None, scratch=(), vmem_mb=48, aliases=None, effect=False,
           prefetch=0):
    kw = dict(_CALL_KW)
    if aliases:
        kw["input_output_aliases"] = aliases
    params = dict(vmem_limit_bytes=vmem_mb << 20)
    if grid is not None:
        params["dimension_semantics"] = ("arbitrary",) * len(grid)
    if effect:
        params["has_side_effects"] = pltpu.SideEffectType.DATAFLOW_SIDE_EFFECTING
    if prefetch:
        kw["grid_spec"] = pltpu.PrefetchScalarGridSpec(num_scalar_prefetch=prefetch, grid=grid, in_specs=in_specs,
                                                       out_specs=out_specs, scratch_shapes=list(scratch))
    else:
        kw.update(in_specs=in_specs, out_specs=out_specs, scratch_shapes=list(scratch))
        if grid is not None:
            kw["grid"] = grid
    return pl.pallas_call(body, name=name, out_shape=out_shape, compiler_params=pltpu.CompilerParams(**params), **kw)


def _full(shape):
    nd = len(shape)
    return pl.BlockSpec(shape, lambda *_: (0,) * nd)


def _dot(a, b):
    return jnp.dot(a, b, preferred_element_type=F32)


def _dot_nt(a, b):
    return lax.dot_general(a, b, (((1,), (1,)), ((), ())), preferred_element_type=F32)


def _dot_tn(a, b):
    return lax.dot_general(a, b, (((0,), (0,)), ((), ())), preferred_element_type=F32)


def _rowsum(a):
    return jnp.sum(a, axis=0, keepdims=True)


def _silu_parts(g):
    s = jax.nn.sigmoid(g)
    return g * s, s * (1.0 + g * (1.0 - s))


def _rms_scale(x):
    return lax.rsqrt(jnp.mean(x * x, axis=-1, keepdims=True) + EPS)


def _dz_col(k):
    return jnp.where(k < 3, k, jnp.where(k < 6, k + 4, k - 3))


def _coords():
    return lax.axis_index("x"), lax.axis_index("y"), lax.axis_index("c")


def _other_chips(x, y):
    return [(1 - x, y), (x, 1 - y), (1 - x, 1 - y)]


def _peer(x, y, c, mask):
    return (1 - x if mask & 4 else x, 1 - y if mask & 2 else y, 1 - c if mask & 1 else c)


def _remote(src, dst, ssem, rsem, k, to):
    return pltpu.make_async_remote_copy(src_ref=src, dst_ref=dst, send_sem=ssem.at[k], recv_sem=rsem.at[k], device_id=to,
                                        device_id_type=MESH)


def _w_half(a, ref, jj, cc):
    if a == 2:
        rp = ref.shape[1] // 4
        return ref.at[:, pl.ds(jj * rp + cc * (rp // 2), rp // 2), :]
    kin, nsh = ref.shape[0], ref.shape[1] // 4
    return ref.at[pl.ds(cc * (kin // 2), kin // 2), pl.ds(jj * nsh, nsh)]


def _cw_block(ref, jj):
    return ref.at[:, pl.ds(jj * (D // 4), D // 4)]


def _cast_place(l, pos, w_in, w_kv, w_pa, w_pb, w_pc, w_out, cw8):
    kin, nsh = w_in.shape[1], w_in.shape[2]
    nkv, rp = w_kv.shape[2], w_pa.shape[1]
    half = kin // 2

    def body(pos_r, win, wkv, pa, pb, pc, po, cw, Win, Wkv, Wp, Cw):
        Win[...] = win[...].astype(BF16)
        Wkv[...] = wkv[...].astype(BF16)

        @pl.when(pl.program_id(0) == 0)
        def _():
            for k, r in enumerate((pa, pb, pc, po)):
                Wp[k] = r[...].astype(BF16)
            Cw[...] = cw[...]

    proj = pl.BlockSpec((None, rp, D), lambda i, p: (l, 0, 0))
    return _pcall(
        body, name="cast_place", grid=(2,), prefetch=1,
        in_specs=[pl.BlockSpec((None, half, nsh), lambda i, p: (l, i, 0)), pl.BlockSpec((None, half, nkv), lambda i, p: (l, i, 0)),
                  proj, proj, proj, proj, pl.BlockSpec((None, 8, D // 4), lambda i, p: (l, 0, 0))],
        out_specs=[pl.BlockSpec((half, nsh), lambda i, p: (i, p[0])), pl.BlockSpec((half, nkv), lambda i, p: (i, p[0])),
                   pl.BlockSpec((4, rp, D), lambda i, p: (0, p[0], 0)), pl.BlockSpec((8, D // 4), lambda i, p: (0, p[0]))],
        out_shape=[SDS((kin, 4 * nsh), BF16), SDS((kin, 4 * nkv), BF16), SDS((4, 4 * rp, D), BF16), SDS((8, D), F32)],
    )(pos, w_in, w_kv, w_pa, w_pb, w_pc, w_out, cw8)


def _hbm_like(bufs):
    return [pltpu.HBM(b.shape, b.dtype) for b in bufs]


def _gather_start(l, bufs):
    def body(win, wkv, wp, cw, ssem, rsem, o0, o1, o2, o3, token):
        x, y, c = _coords()
        j = 2 * x + y
        refs = (win, wkv, wp)
        for k, chip in enumerate(_other_chips(x, y)):
            to = (chip[0], chip[1], c)
            for a in range(3):
                half = _w_half(a, refs[a], j, c)
                _remote(half, half, ssem, rsem, a * 3 + k, to).start()
            mine = _cw_block(cw, j)
            _remote(mine, mine, ssem, rsem, 9 + k, to).start()
        token[...] = jnp.zeros((8, 128), F32)

    return _pcall(
        body, name=f"gather_start_{l}", in_specs=[HBM] * 4, out_specs=[SEM, SEM] + [HBM] * 4 + [VMEM],
        out_shape=[pltpu.SemaphoreType.DMA((12,)), pltpu.SemaphoreType.DMA((12,))] + _hbm_like(bufs) + [TOKEN],
        aliases={0: 2, 1: 3, 2: 4, 3: 5}, effect=True,
    )(*[pltpu.with_memory_space_constraint(b, pltpu.HBM) for b in bufs])


def _gather_mid(l, started, after):
    ssem, rsem, b0, b1, b2, b3, _ = started

    def body(win, wkv, wp, cw, ssem, rsem, after_r, ssem2, rsem2, o0, o1, o2, o3):
        x, y, c = _coords()
        j = 2 * x + y
        me, sib = (x, y, c), (x, y, 1 - c)
        refs = (win, wkv, wp)
        chips = _other_chips(x, y)
        for k, chip in enumerate(chips):
            jk = 2 * chip[0] + chip[1]
            for a in range(3):
                got = _w_half(a, refs[a], jk, c)
                _remote(got, got, ssem, rsem, a * 3 + k, me).wait_recv()
            got = _cw_block(cw, jk)
            _remote(got, got, ssem, rsem, 9 + k, me).wait_recv()
        for k in range(3):
            for a in range(3):
                half = _w_half(a, refs[a], j, c)
                _remote(half, half, ssem, rsem, a * 3 + k, me).wait_send()
            mine = _cw_block(cw, j)
            _remote(mine, mine, ssem, rsem, 9 + k, me).wait_send()
        for k, chip in enumerate(chips):
            jk = 2 * chip[0] + chip[1]
            for a in range(3):
                got = _w_half(a, refs[a], jk, c)
                _remote(got, got, ssem2, rsem2, a * 3 + k, sib).start()

    bufs = (b0, b1, b2, b3)
    return _pcall(
        body, name=f"gather_mid_{l}", in_specs=[HBM] * 4 + [SEM, SEM, ANY], out_specs=[SEM, SEM] + [HBM] * 4,
        out_shape=[pltpu.SemaphoreType.DMA((9,)), pltpu.SemaphoreType.DMA((9,))] + _hbm_like(bufs),
        aliases={0: 2, 1: 3, 2: 4, 3: 5}, effect=True,
    )(b0, b1, b2, b3, ssem, rsem, after)


def _gather_end(l, mid, after):
    ssem2, rsem2, b0, b1, b2, b3 = mid

    def body(win, wkv, wp, cw, ssem2, rsem2, after_r, o0, o1, o2, o3):
        x, y, c = _coords()
        me = (x, y, c)
        refs = (win, wkv, wp)
        for k, chip in enumerate(_other_chips(x, y)):
            jk = 2 * chip[0] + chip[1]
            for a in range(3):
                got = _w_half(a, refs[a], jk, 1 - c)
                _remote(got, got, ssem2, rsem2, a * 3 + k, me).wait_recv()
                sent = _w_half(a, refs[a], jk, c)
                _remote(sent, sent, ssem2, rsem2, a * 3 + k, me).wait_send()

    bufs = (b0, b1, b2, b3)
    return _pcall(
        body, name=f"gather_end_{l}", in_specs=[HBM] * 4 + [SEM, SEM, ANY], out_specs=[HBM] * 4, out_shape=_hbm_like(bufs),
        aliases={0: 0, 1: 1, 2: 2, 3: 3}, effect=True,
    )(b0, b1, b2, b3, ssem2, rsem2, after)


def _g_piece(a, ref, jd, dc):
    if a >= 2:
        rp = ref.shape[0] // 4
        return ref.at[pl.ds(jd * rp + dc * (rp // 2), rp // 2), :]
    kin, nsh = ref.shape[0], ref.shape[1] // 4
    return ref.at[pl.ds(dc * (kin // 2), kin // 2), pl.ds(jd * nsh, nsh)]


def _land_slot(a, lands, s):
    if a < 2:
        return lands[a].at[s]
    return lands[2].at[s, a - 2]


def _a2a_start(l, grads, lands):
    def body(*refs):
        g, ld, ssem, rsem, token = refs[0:6], refs[6:9], refs[9], refs[10], refs[-1]
        x, y, c = _coords()
        for mask in range(1, NDEV):
            p = _peer(x, y, c, mask)
            for a in range(6):
                _remote(_g_piece(a, g[a], 2 * p[0] + p[1], p[2]), _land_slot(a, ld, mask - 1), ssem, rsem, a * 7 + mask - 1, p).start()
        token[...] = jnp.zeros((8, 128), F32)

    bufs = tuple(grads) + tuple(lands)
    return _pcall(
        body, name=f"a2a_start_{l}", in_specs=[HBM] * 9, out_specs=[SEM, SEM] + [HBM] * 9 + [VMEM],
        out_shape=[pltpu.SemaphoreType.DMA((42,)), pltpu.SemaphoreType.DMA((42,))] + _hbm_like(bufs) + [TOKEN],
        aliases={i: 2 + i for i in range(9)}, effect=True,
    )(*[pltpu.with_memory_space_constraint(b, pltpu.HBM) for b in bufs])


def _a2a_wait(l, started, after):
    ssem, rsem = started[0], started[1]
    bufs = tuple(started[2:11])

    def body(*refs):
        g, ld, ssem, rsem = refs[0:6], refs[6:9], refs[9], refs[10]
        x, y, c = _coords()
        me = (x, y, c)
        for mask in range(1, NDEV):
            for a in range(6):
                got = _land_slot(a, ld, mask - 1)
                _remote(got, got, ssem, rsem, a * 7 + mask - 1, me).wait_recv()
        for mask in range(1, NDEV):
            p = _peer(x, y, c, mask)
            for a in range(6):
                sent = _g_piece(a, g[a], 2 * p[0] + p[1], p[2])
                _remote(sent, sent, ssem, rsem, a * 7 + mask - 1, me).wait_send()

    return _pcall(
        body, name=f"a2a_wait_{l}", in_specs=[HBM] * 9 + [SEM, SEM, ANY], out_specs=[HBM] * 9, out_shape=_hbm_like(bufs),
        aliases={i: i for i in range(9)}, effect=True,
    )(*bufs, ssem, rsem, after)


def _sum_share(lands, owns):
    rows, n = 128, 4
    widths = [o.shape[1] for o in owns]

    def body(l0, w0, l1, w1, l2, w2, g0, g1, g2, b0, b1, b2, lsem, ssem, rsem):
        i = pl.program_id(0)
        x, y, c = _coords()
        sib = (x, y, 1 - c)
        ld, ow, gs, bufs = (l0, l1, l2), (w0, w1, w2), (g0, g1, g2), (b0, b1, b2)

        def dst(a, step):
            row = step * (2 * rows) + c * rows if a == 2 else c * (n * rows) + step * rows
            return gs[a].at[pl.ds(row, rows), :]

        def copies(a, step, sl):
            src = bufs[a].at[sl]
            lc = pltpu.make_async_copy(src, dst(a, step), lsem.at[a, sl])
            rc = pltpu.make_async_remote_copy(src_ref=src, dst_ref=dst(a, step), send_sem=ssem.at[a, sl], recv_sem=rsem.at[a],
                                              device_id=sib, device_id_type=MESH)
            return lc, rc

        def drain(a, step, sl):
            lc, rc = copies(a, step, sl)
            lc.wait()
            rc.wait_send()

        slot = i % 2

        @pl.when(i >= 2)
        def _():
            for a in range(3):
                drain(a, i - 2, slot)

        for a in range(3):
            acc = ow[a][...].astype(F32)
            for k in range(NDEV - 1):
                acc = acc + ld[a][k].astype(F32)
            bufs[a][slot] = acc
            lc, rc = copies(a, i, slot)
            lc.start()
            rc.start()

        @pl.when(i == n - 1)
        def _():
            for a in range(3):
                drain(a, n - 2, (n - 2) % 2)
                drain(a, n - 1, (n - 1) % 2)
                whole = gs[a].at[pl.ds(0, n * rows), :]
                pltpu.make_async_remote_copy(src_ref=whole, dst_ref=whole, send_sem=ssem.at[a, 0], recv_sem=rsem.at[a],
                                             device_id=(x, y, c), device_id_type=MESH).wait_recv()

    in_specs = []
    for w in widths:
        in_specs += [pl.BlockSpec((NDEV - 1, rows, w), lambda i: (0, i, 0)), pl.BlockSpec((rows, w), lambda i: (i, 0))]
    args = [t for pair in zip(lands, owns) for t in pair]
    return _pcall(
        body, name="sum_share", grid=(n,), in_specs=in_specs, out_specs=[HBM] * 3,
        out_shape=[SDS((2 * n * rows, w), F32) for w in widths],
        scratch=[pltpu.VMEM((2, rows, w), F32) for w in widths]
        + [pltpu.SemaphoreType.DMA((3, 2)), pltpu.SemaphoreType.DMA((3, 2)), pltpu.SemaphoreType.DMA((3,))],
    )(*args)


def _allreduce_small(pack):
    R = pack.shape[0]
    r8 = R // NDEV

    def body(p_ref, full, land, red, ssem, rsem, ssem2, rsem2, lsem):
        x, y, c = _coords()
        me = 4 * x + 2 * y + c

        def idx(p):
            return 4 * p[0] + 2 * p[1] + p[2]

        own = pltpu.make_async_copy(p_ref.at[pl.ds(me * r8, r8), :], land.at[me], lsem.at[0])
        own.start()
        sent = []
        for mask in range(1, NDEV):
            p = _peer(x, y, c, mask)
            cp = _remote(p_ref.at[pl.ds(idx(p) * r8, r8), :], land.at[me], ssem, rsem, mask - 1, p)
            cp.start()
            sent.append(cp)
        for mask in range(1, NDEV):
            got = land.at[idx(_peer(x, y, c, mask))]
            _remote(got, got, ssem, rsem, mask - 1, (x, y, c)).wait_recv()
        own.wait()
        for cp in sent:
            cp.wait_send()
        acc = land[0]
        for k in range(1, NDEV):
            acc = acc + land[k]
        red[...] = acc
        mine = full.at[pl.ds(me * r8, r8), :]
        own2 = pltpu.make_async_copy(red, mine, lsem.at[1])
        own2.start()
        sent2 = []
        for mask in range(1, NDEV):
            cp = _remote(red, mine, ssem2, rsem2, mask - 1, _peer(x, y, c, mask))
            cp.start()
            sent2.append(cp)
        for mask in range(1, NDEV):
            got = full.at[pl.ds(idx(_peer(x, y, c, mask)) * r8, r8), :]
            _remote(got, got, ssem2, rsem2, mask - 1, (x, y, c)).wait_recv()
        own2.wait()
        for cp in sent2:
            cp.wait_send()

    return _pcall(
        body, name="allreduce_small", in_specs=[HBM], out_specs=HBM, out_shape=SDS((R, 128), F32),
        scratch=[pltpu.VMEM((NDEV, r8, 128), F32), pltpu.VMEM((r8, 128), F32)]
        + [pltpu.SemaphoreType.DMA((NDEV - 1,))] * 4 + [pltpu.SemaphoreType.DMA((2,))], vmem_mb=32,
    )(pack)


def _mm_in(x, g, w):
    S = x.shape[0]
    tm, tn = min(1024, S), 1280

    def body(x_ref, g_ref, w_ref, z_ref, h_ref, hs):
        @pl.when(pl.program_id(1) == 0)
        def _():
            xv = x_ref[...]
            hb = (xv * _rms_scale(xv) * g_ref[...]).astype(BF16)
            hs[...] = hb
            h_ref[...] = hb

        z_ref[...] = _dot(hs[...], w_ref[...]).astype(BF16)

    return _pcall(
        body, name="mm_in", grid=(S // tm, NIN // tn),
        in_specs=[pl.BlockSpec((tm, D), lambda i, j: (i, 0)), _full((1, D)), pl.BlockSpec((D, tn), lambda i, j: (0, j))],
        out_specs=[pl.BlockSpec((tm, tn), lambda i, j: (i, j)), pl.BlockSpec((tm, D), lambda i, j: (i, 0))],
        out_shape=[SDS((S, NIN), BF16), SDS((S, D), BF16)], scratch=[pltpu.VMEM((tm, D), BF16)],
    )(x, g, w)


def _chunk_mask():
    ri = lax.broadcasted_iota(jnp.int32, (GB, GB), 0)
    ci = lax.broadcasted_iota(jnp.int32, (GB, GB), 1)
    return (ri >= 64) | (ci < 64)


def _layernorm_parts(v):
    mu = jnp.mean(v, axis=-1, keepdims=True)
    d = v - mu
    rs = lax.rsqrt(jnp.mean(d * d, axis=-1, keepdims=True) + EPS)
    return d * rs, rs


def _branch_a(z, lg, lb, ws, bsb):
    S = z.shape[0]
    T = min(512, S)

    def body(zu, zv, zg, lg_r, lb_r, ws_r, bs_r, ya):
        vhat, _ = _layernorm_parts(zv[...].astype(F32))
        vnb = (vhat * lg_r[...] + lb_r[...]).astype(BF16)
        sil, _ = _silu_parts(zg[...].astype(F32))
        t = zu[...].astype(F32) * sil
        mask = _chunk_mask()
        for g in range(NG):
            wg = jnp.where(mask, ws_r[g], 0.0).astype(BF16)
            cs = slice(g * GB, (g + 1) * GB)
            for n in range(T // GB):
                rs = slice(n * GB, (n + 1) * GB)
                sv = _dot(wg, vnb[rs, cs]) + bs_r[g]
                ya[rs, cs] = (t[rs, cs] * sv).astype(BF16)

    zs = lambda k: pl.BlockSpec((T, D), lambda i: (i, k))
    return _pcall(
        body, name="branch_a", grid=(S // T,),
        in_specs=[zs(0), zs(1), zs(2), _full((1, D)), _full((1, D)), _full((NG, GB, GB)), _full((NG, GB, GB))],
        out_specs=pl.BlockSpec((T, D), lambda i: (i, 0)), out_shape=SDS((S, D), BF16),
    )(z, z, z, lg, lb, ws, bsb)


def _softplus_neg(lam):
    e = jnp.exp(-jnp.abs(lam))
    l1p = jnp.where(e < 1e-2, e * (1.0 - e * (0.5 - e * (1.0 / 3.0))), jnp.log(1.0 + e))
    return jnp.maximum(-lam, 0.0) + l1p


def _lru_pre(xk, cw, cb, wr_r, br, wi_r, bi, lam):
    xc = cb + (((xk[0] * cw[0:1] + xk[1] * cw[1:2]) + xk[2] * cw[2:3]) + xk[3] * cw[3:4])
    xcb = xc.astype(BF16)
    pr, pi = [], []
    for h in range(NG):
        cs = slice(h * GB, (h + 1) * GB)
        pr.append(_dot(xcb[:, cs], wr_r[h].astype(BF16)))
        pi.append(_dot(xcb[:, cs], wi_r[h].astype(BF16)))
    r = jax.nn.sigmoid(jnp.concatenate(pr, axis=1) + br)
    ig = jax.nn.sigmoid(jnp.concatenate(pi, axis=1) + bi)
    sp = _softplus_neg(lam)
    la = (-LRU_C * sp) * r
    a = jnp.exp(la)
    a2 = a * a
    mult = jnp.sqrt(-jnp.tanh(la) * (a2 + 1.0))
    return xc, xcb, r, ig, sp, a, a2, mult


def _tile_rows():
    return lax.broadcasted_iota(jnp.int32, (8, D), 0)


def _scan_forward(a_s, u_s, h_s, hcar, T):
    row = _tile_rows()

    def tile(i, hp):
        o = pl.multiple_of(i * 8, 8)
        A = a_s[pl.ds(o, 8), :]
        U = u_s[pl.ds(o, 8), :]
        for s in (1, 2, 4):
            m = row >= s
            U = jnp.where(m, U + A * pltpu.roll(U, s, 0), U)
            A = jnp.where(m, A * pltpu.roll(A, s, 0), A)
        H = U + A * hp
        h_s[pl.ds(o, 8), :] = H
        return jnp.broadcast_to(H[7:8, :], (8, D))

    hcar[...] = lax.fori_loop(0, T // 8, tile, hcar[...])


def _scan_reverse(b_s, d_s, l_s, lcar, T):
    row = _tile_rows()
    n = T // 8

    def tile(i, lp):
        o = pl.multiple_of((n - 1 - i) * 8, 8)
        B = b_s[pl.ds(o, 8), :]
        U = d_s[pl.ds(o, 8), :]
        for s in (1, 2, 4):
            m = row < 8 - s
            U = jnp.where(m, U + B * pltpu.roll(U, 8 - s, 0), U)
            B = jnp.where(m, B * pltpu.roll(B, 8 - s, 0), B)
        Lm = U + B * lp
        l_s[pl.ds(o, 8), :] = Lm
        return jnp.broadcast_to(Lm[0:1, :], (8, D))

    lcar[...] = lax.fori_loop(0, n, tile, lcar[...])


def _branch_b(z, cw, cb, wr, br, wi, bi, lam):
    S = z.shape[0]
    T = min(256, S)

    def body(zxb, zgb, cw_r, cb_r, wr_r, br_r, wi_r, bi_r, lam_r, yb, hs_o, xpad, a_s, u_s, hcar):
        @pl.when(pl.program_id(0) == 0)
        def _():
            xpad[pl.ds(0, 8), :] = jnp.zeros((8, D), F32)
            hcar[...] = jnp.zeros((8, D), F32)

        xpad[pl.ds(8, T), :] = zxb[...].astype(F32)
        xk = [xpad[pl.ds(5 + k, T), :] for k in range(4)]
        xc, _, _, ig, _, a, _, mult = _lru_pre(xk, cw_r[...], cb_r[...], wr_r, br_r[...], wi_r, bi_r[...], lam_r[...])
        a_s[...] = a
        u_s[...] = mult * (ig * xc)
        _scan_forward(a_s, u_s, hs_o, hcar, T)
        xpad[pl.ds(0, 8), :] = xpad[pl.ds(T, 8), :]
        sil, _ = _silu_parts(zgb[...].astype(F32))
        yb[...] = (hs_o[...] * sil).astype(BF16)

    zs = lambda k: pl.BlockSpec((T, D), lambda i: (i, k))
    row = pl.BlockSpec((T, D), lambda i: (i, 0))
    return _pcall(
        body, name="branch_b", grid=(S // T,),
        in_specs=[zs(3), zs(4), _full((8, D)), _full((1, D)), _full((NG, GB, GB)), _full((1, D)), _full((NG, GB, GB)),
                  _full((1, D)), _full((1, D))],
        out_specs=[row, row], out_shape=[SDS((S, D), BF16), SDS((S, D), F32)],
        scratch=[pltpu.VMEM((T + 8, D), F32), pltpu.VMEM((T, D), F32), pltpu.VMEM((T, D), F32), pltpu.VMEM((8, D), F32)],
    )(z, z, cw, cb, wr, br, wi, bi, lam)


def _kv(mem, g, wkv):
    def body(m_ref, g_ref, w_ref, kv_ref):
        m = m_ref[...]
        mn = (m * _rms_scale(m) * g_ref[...]).astype(BF16)
        kv_ref[...] = _dot(mn, w_ref[...]).astype(BF16)

    return _pcall(body, name="mem_kv", in_specs=[VMEM] * 3, out_specs=VMEM, out_shape=SDS((MEM, 2 * D), BF16),
                  vmem_mb=32)(mem, g, wkv)


def _softmax_rows(s):
    e = jnp.exp(s - jnp.max(s, axis=-1, keepdims=True))
    return e / jnp.sum(e, axis=-1, keepdims=True)


def _branch_c(z, kv):
    S = z.shape[0]
    T = min(512, S)

    def body(zq, zg, kv_r, yc):
        sil, _ = _silu_parts(zg[...].astype(F32))
        for h in range(NH):
            cs = slice(h * HD, (h + 1) * HD)
            p = _softmax_rows(_dot_nt(zq[:, cs], kv_r[:, cs]) * (HD ** -0.5))
            att = _dot(p.astype(BF16), kv_r[:, D + h * HD:D + (h + 1) * HD])
            yc[:, cs] = (att * sil[:, cs]).astype(BF16)

    zs = lambda k: pl.BlockSpec((T, D), lambda i: (i, k))
    return _pcall(body, name="branch_c", grid=(S // T,), in_specs=[zs(5), zs(6), _full((MEM, 2 * D))],
                  out_specs=pl.BlockSpec((T, D), lambda i: (i, 0)), out_shape=SDS((S, D), BF16))(z, z, kv)


def _merge_out(ya, yb, yc, z, wp, x, pg):
    S = x.shape[0]
    T = min(256, S)

    def body(ya_r, yb_r, yc_r, m0, m1, m2, wp_r, x_r, pg_r, pa_o, pb_o, pc_o, mg_o, o_o, xn_o):
        merged = None
        for y_r, ml, p_o, k in ((ya_r, m0, pa_o, 0), (yb_r, m1, pb_o, 1), (yc_r, m2, pc_o, 2)):
            p = _dot(y_r[...], wp_r[k])
            p_o[...] = p.astype(BF16)
            t = jax.nn.sigmoid(ml[...].astype(F32)) * p
            merged = t if merged is None else merged + t
        mb = merged.astype(BF16)
        mg_o[...] = mb
        o = _dot(mb, wp_r[3])
        o_o[...] = o.astype(BF16)
        xn_o[...] = x_r[...] + o * _rms_scale(o) * pg_r[...]

    row = pl.BlockSpec((T, D), lambda i: (i, 0))
    zs = lambda k: pl.BlockSpec((T, D), lambda i: (i, k))
    return _pcall(
        body, name="merge_out", grid=(S // T,),
        in_specs=[row, row, row, zs(7), zs(8), zs(9), _full((4, D, D)), row, _full((1, D))],
        out_specs=[row] * 6, out_shape=[SDS((S, D), BF16)] * 5 + [SDS((S, D), F32)], vmem_mb=56,
    )(ya, yb, yc, z, z, z, wp, x, pg)


def _loss_head(y, t):
    S = y.shape[0]
    T = min(512, S)

    def body(y_r, t_r, loss_o, dy_o):
        @pl.when(pl.program_id(0) == 0)
        def _():
            loss_o[...] = jnp.zeros((1, 1), F32)

        e = y_r[...] - t_r[...]
        dy_o[...] = e * (1.0 / D)
        loss_o[...] += 0.5 * _rowsum(jnp.sum(e * e, axis=1, keepdims=True) * (1.0 / D))

    row = pl.BlockSpec((T, D), lambda i: (i, 0))
    return _pcall(body, name="loss_head", grid=(S // T,), in_specs=[row, row], out_specs=[_full((1, 1)), row],
                  out_shape=[SDS((1, 1), F32), SDS((S, D), F32)])(y, t)


def _accumulate(first, ref, val):
    @pl.when(first)
    def _():
        ref[...] = val

    @pl.when(jnp.logical_not(first))
    def _():
        ref[...] += val


def _out_bwd(dxn, o, pg, wp, z, pa, pb, pc):
    S = dxn.shape[0]
    T = min(256, S)

    def body(dy_r, o_r, pg_r, wp_r, m0, m1, m2, pa_r, pb_r, pc_r, do_o, dpa_o, dpb_o, dpc_o, dya_o, dyb_o, dyc_o, dz_o, dg_o):
        dy = dy_r[...]
        o = o_r[...].astype(F32)
        r2 = _rms_scale(o)
        w = dy * pg_r[...]
        do = r2 * w - o * (r2 * r2 * r2) * jnp.mean(w * o, axis=-1, keepdims=True)
        _accumulate(pl.program_id(0) == 0, dg_o, _rowsum(dy * o * r2))
        dob = do.astype(BF16)
        do_o[...] = dob
        dm = _dot_nt(dob, wp_r[3])
        for k, (ml, p_r, dp_o, dy_o) in enumerate(((m0, pa_r, dpa_o, dya_o), (m1, pb_r, dpb_o, dyb_o), (m2, pc_r, dpc_o, dyc_o))):
            gk = jax.nn.sigmoid(ml[...].astype(F32))
            dz_o[k] = (dm * p_r[...].astype(F32) * gk * (1.0 - gk)).astype(BF16)
            dpk = (gk * dm).astype(BF16)
            dp_o[...] = dpk
            dy_o[...] = _dot_nt(dpk, wp_r[k]).astype(BF16)

    row = pl.BlockSpec((T, D), lambda i: (i, 0))
    zs = lambda k: pl.BlockSpec((T, D), lambda i: (i, k))
    return _pcall(
        body, name="out_bwd", grid=(S // T,),
        in_specs=[row, row, _full((1, D)), _full((4, D, D)), zs(7), zs(8), zs(9), row, row, row],
        out_specs=[row] * 7 + [pl.BlockSpec((3, T, D), lambda i: (1, i, 0)), _full((1, D))],
        out_shape=[SDS((S, D), BF16)] * 7 + [SDS((10, S, D), BF16), SDS((1, D), F32)], vmem_mb=56,
    )(dxn, o, pg, wp, z, z, z, pa, pb, pc)


def _branch_a_bwd(z, dya, lg, lb, ws, bsb, dz):
    S = z.shape[0]
    T = min(512, S)
    nblk = S // T

    def body(zu, zv, zg, dy_r, lg_r, lb_r, ws_r, bs_r, dz_in, dz_o, dws_o, dbs_o, dlg_o, dlb_o, dvn_s, bacc):
        i = pl.program_id(0)

        @pl.when(i == 0)
        def _():
            dws_o[...] = jnp.zeros((NG, GB, GB), F32)
            bacc[...] = jnp.zeros((NG, GB, GB), F32)

        vhat, rs = _layernorm_parts(zv[...].astype(F32))
        vnb = (vhat * lg_r[...] + lb_r[...]).astype(BF16)
        ga = zg[...].astype(F32)
        sil, dsil = _silu_parts(ga)
        u = zu[...].astype(F32)
        dy = dy_r[...].astype(F32)
        t = dy * sil
        dsv_all = t * u
        dga_pre = dy * u * dsil
        mask = _chunk_mask()
        for g in range(NG):
            wf = jnp.where(mask, ws_r[g], 0.0)
            wg = wf.astype(BF16)
            wgt = wf.T.astype(BF16)
            cs = slice(g * GB, (g + 1) * GB)
            dw = jnp.zeros((GB, GB), F32)
            db = jnp.zeros((GB, GB), F32)
            for n in range(T // GB):
                rsl = slice(n * GB, (n + 1) * GB)
                vb = vnb[rsl, cs]
                sv = _dot(wg, vb) + bs_r[g]
                dz_o[0, rsl, cs] = (t[rsl, cs] * sv).astype(BF16)
                dz_o[2, rsl, cs] = (dga_pre[rsl, cs] * sv).astype(BF16)
                dsv = dsv_all[rsl, cs]
                dsb = dsv.astype(BF16)
                dvn_s[rsl, cs] = _dot(wgt, dsb)
                dw = dw + _dot_nt(dsb, vb)
                db = db + dsv
            dws_o[g] += jnp.where(mask, dw, 0.0)
            bacc[g] += db
        dvn = dvn_s[...]
        dvh = dvn * lg_r[...]
        dv = rs * (dvh - jnp.mean(dvh, axis=-1, keepdims=True) - vhat * jnp.mean(dvh * vhat, axis=-1, keepdims=True))
        dz_o[1] = dv.astype(BF16)
        _accumulate(i == 0, dlg_o, _rowsum(dvn * vhat))
        _accumulate(i == 0, dlb_o, _rowsum(dvn))

        @pl.when(i == nblk - 1)
        def _():
            for g in range(NG):
                dbs_o[g:g + 1, :] = _rowsum(bacc[g].T)

    zs = lambda k: pl.BlockSpec((T, D), lambda i: (i, k))
    return _pcall(
        body, name="branch_a_bwd", grid=(nblk,),
        in_specs=[zs(0), zs(1), zs(2), pl.BlockSpec((T, D), lambda i: (i, 0)), _full((1, D)), _full((1, D)),
                  _full((NG, GB, GB)), _full((NG, GB, GB)), HBM],
        out_specs=[pl.BlockSpec((3, T, D), lambda i: (0, i, 0)), _full((NG, GB, GB)), _full((NG, GB)), _full((1, D)),
                   _full((1, D))],
        out_shape=[SDS((10, S, D), BF16), SDS((NG, GB, GB), F32), SDS((NG, GB), F32), SDS((1, D), F32), SDS((1, D), F32)],
        scratch=[pltpu.VMEM((T, D), F32), pltpu.VMEM((NG, GB, GB), F32)], aliases={8: 0},
    )(z, z, z, dya, lg, lb, ws, bsb, dz)


def _branch_b_bwd(z, hs, dyb, cw, cb, wr, br, wi, bi, lam, dz):
    S = z.shape[0]
    T = min(256, S)
    nblk = S // T

    def body(zxb, zprev, zgb, hs_r, hprev_r, dy_r, cw_r, cb_r, wr_r, br_r, wi_r, bi_r, lam_r, dz_in,
             dz_o, dcw_o, dcb_o, dwr_o, dbr_o, dwi_o, dbi_o, dlam_o, xpad, hpad, apad, dpad, b_s, d_s, l_s, lcar):
        i = pl.program_id(0)
        blk = nblk - 1 - i
        first = i == 0

        @pl.when(first)
        def _():
            apad[pl.ds(T, 8), :] = jnp.zeros((8, D), F32)
            dpad[pl.ds(T, 8), :] = jnp.zeros((8, D), F32)
            lcar[...] = jnp.zeros((8, D), F32)
            dcw_o[...] = jnp.zeros((8, D), F32)
            dwr_o[...] = jnp.zeros((NG, GB, GB), F32)
            dwi_o[...] = jnp.zeros((NG, GB, GB), F32)

        keep = (blk > 0).astype(F32)
        xpad[pl.ds(8, T), :] = zxb[...].astype(F32)
        xpad[pl.ds(0, 8), :] = zprev[...].astype(F32)[8:16, :] * keep
        hsv = hs_r[...]
        hpad[pl.ds(8, T), :] = hsv
        hpad[pl.ds(0, 8), :] = hprev_r[...] * keep
        xk = [xpad[pl.ds(5 + k, T), :] for k in range(4)]
        cw = cw_r[...]
        lam = lam_r[...]
        xc, xcb, r, ig, sp, a, a2, mult = _lru_pre(xk, cw, cb_r[...], wr_r, br_r[...], wi_r, bi_r[...], lam)
        gb = zgb[...].astype(F32)
        sil, dsil = _silu_parts(gb)
        dy = dy_r[...].astype(F32)
        dz_o[1] = (dy * hsv * dsil).astype(BF16)
        apad[pl.ds(0, T), :] = a
        b_s[...] = apad[pl.ds(1, T), :]
        d_s[...] = dy * sil
        _scan_reverse(b_s, d_s, l_s, lcar, T)
        lm = l_s[...]
        da = lm * hpad[pl.ds(7, T), :]
        t = lm * mult
        dmult = lm * ig * xc
        di = t * xc
        dxc = t * ig
        dl = da * a - dmult * a2 / mult
        dr = dl * (-LRU_C * sp)
        dsp = _rowsum(dl * r) * (-LRU_C)
        dpr = dr * r * (1.0 - r)
        dpi = di * ig * (1.0 - ig)
        _accumulate(first, dlam_o, dsp * (-jax.nn.sigmoid(-lam)))
        _accumulate(first, dbr_o, _rowsum(dpr))
        _accumulate(first, dbi_o, _rowsum(dpi))
        dprb = dpr.astype(BF16)
        dpib = dpi.astype(BF16)
        back = []
        for h in range(NG):
            cs = slice(h * GB, (h + 1) * GB)
            back.append(_dot_nt(dprb[:, cs], wr_r[h].astype(BF16)) + _dot_nt(dpib[:, cs], wi_r[h].astype(BF16)))
            dwr_o[h] += _dot_tn(xcb[:, cs], dprb[:, cs])
            dwi_o[h] += _dot_tn(xcb[:, cs], dpib[:, cs])
        dxc = dxc + jnp.concatenate(back, axis=1)
        _accumulate(first, dcb_o, _rowsum(dxc))
        for k in range(4):
            dcw_o[k:k + 1, :] += _rowsum(xk[k] * dxc)
        dpad[pl.ds(0, T), :] = dxc
        dxb = ((dpad[pl.ds(3, T), :] * cw[0:1] + dpad[pl.ds(2, T), :] * cw[1:2]) + dpad[pl.ds(1, T), :] * cw[2:3]) + dxc * cw[3:4]
        dz_o[0] = dxb.astype(BF16)
        apad[pl.ds(T, 8), :] = apad[pl.ds(0, 8), :]
        dpad[pl.ds(T, 8), :] = dpad[pl.ds(0, 8), :]

    rev = lambda k: pl.BlockSpec((T, D), lambda i: (nblk - 1 - i, k))
    prev16 = pl.BlockSpec((16, D), lambda i: (jnp.maximum((nblk - 1 - i) * (T // 16) - 1, 0), 3))
    prev8 = pl.BlockSpec((8, D), lambda i: (jnp.maximum((nblk - 1 - i) * (T // 8) - 1, 0), 0))
    vec, mat = _full((1, D)), _full((NG, GB, GB))
    return _pcall(
        body, name="branch_b_bwd", grid=(nblk,),
        in_specs=[rev(3), prev16, rev(4), rev(0), prev8, rev(0), _full((8, D)), vec, mat, vec, mat, vec, vec, HBM],
        out_specs=[pl.BlockSpec((2, T, D), lambda i: (3, nblk - 1 - i, 0)), _full((8, D)), vec, mat, vec, mat, vec, vec],
        out_shape=[SDS((10, S, D), BF16), SDS((8, D), F32), SDS((1, D), F32), SDS((NG, GB, GB), F32), SDS((1, D), F32),
                   SDS((NG, GB, GB), F32), SDS((1, D), F32), SDS((1, D), F32)],
        scratch=[pltpu.VMEM((T + 8, D), F32)] * 4 + [pltpu.VMEM((T, D), F32)] * 3 + [pltpu.VMEM((8, D), F32)],
        aliases={13: 0}, vmem_mb=56,
    )(z, z, z, hs, hs, dyb, cw, cb, wr, br, wi, bi, lam, dz)


def _branch_c_bwd(z, kv, dyc, dz):
    S = z.shape[0]
    T = min(512, S)

    def body(zq, zg, kv_r, dy_r, dz_in, dz_o, dkv_o):
        @pl.when(pl.program_id(0) == 0)
        def _():
            dkv_o[...] = jnp.zeros((MEM, 2 * D), F32)

        gc = zg[...].astype(F32)
        sil, dsil = _silu_parts(gc)
        dy = dy_r[...].astype(F32)
        datt = dy * sil
        dgc_pre = dy * dsil
        scale = HD ** -0.5
        for h in range(NH):
            cs = slice(h * HD, (h + 1) * HD)
            vs = slice(D + h * HD, D + (h + 1) * HD)
            qh = zq[:, cs]
            p = _softmax_rows(_dot_nt(qh, kv_r[:, cs]) * scale)
            pb = p.astype(BF16)
            att = _dot(pb, kv_r[:, vs])
            dz_o[1, :, cs] = (dgc_pre[:, cs] * att).astype(BF16)
            dab = datt[:, cs].astype(BF16)
            dp = _dot_nt(dab, kv_r[:, vs])
            ds = (p * (dp - jnp.sum(p * dp, axis=-1, keepdims=True)) * scale).astype(BF16)
            dz_o[0, :, cs] = _dot(ds, kv_r[:, cs]).astype(BF16)
            dkv_o[:, cs] += _dot_tn(ds, qh)
            dkv_o[:, vs] += _dot_tn(pb, dab)

    zs = lambda k: pl.BlockSpec((T, D), lambda i: (i, k))
    return _pcall(
        body, name="branch_c_bwd", grid=(S // T,),
        in_specs=[zs(5), zs(6), _full((MEM, 2 * D)), pl.BlockSpec((T, D), lambda i: (i, 0)), HBM],
        out_specs=[pl.BlockSpec((2, T, D), lambda i: (4, i, 0)), _full((MEM, 2 * D))],
        out_shape=[SDS((10, S, D), BF16), SDS((MEM, 2 * D), F32)], aliases={4: 0},
    )(z, z, kv, dyc, dz)


def _mm_dh(dz, w, x, dxn, g):
    S = x.shape[0]
    tm = min(1024, S)

    def body(dz_r, w_r, x_r, dxn_r, g_r, dx_o, dg_o, acc):
        i, k = pl.program_id(0), pl.program_id(1)
        _accumulate(k == 0, acc, _dot_nt(dz_r[0], w_r[...]))

        @pl.when(k == 9)
        def _():
            dh = acc[...]
            xv = x_r[...]
            r1 = _rms_scale(xv)
            wv = dh * g_r[...]
            dx_o[...] = dxn_r[...] + r1 * wv - xv * (r1 * r1 * r1) * jnp.mean(wv * xv, axis=-1, keepdims=True)
            _accumulate(i == 0, dg_o, _rowsum(dh * xv * r1))

    row = pl.BlockSpec((tm, D), lambda i, k: (i, 0))
    return _pcall(
        body, name="mm_dh", grid=(S // tm, 10),
        in_specs=[pl.BlockSpec((1, tm, D), lambda i, k: (k, i, 0)), pl.BlockSpec((D, D), lambda i, k: (0, _dz_col(k))),
                  row, row, _full((1, D))],
        out_specs=[row, _full((1, D))], out_shape=[SDS((S, D), F32), SDS((1, D), F32)],
        scratch=[pltpu.VMEM((tm, D), F32)],
    )(dz, w, x, dxn, g)


def _mm_dwin(h, dz):
    S = h.shape[0]
    tk = min(1024, S)
    nk = S // tk

    def body(h_r, dz_r, o_r, acc):
        k = pl.program_id(1)
        _accumulate(k == 0, acc, _dot_tn(h_r[...], dz_r[0]))

        @pl.when(k == nk - 1)
        def _():
            o_r[...] = acc[...].astype(BF16)

    return _pcall(
        body, name="mm_dwin", grid=(10, nk),
        in_specs=[pl.BlockSpec((tk, D), lambda n, k: (k, 0)), pl.BlockSpec((1, tk, D), lambda n, k: (n, k, 0))],
        out_specs=pl.BlockSpec((D, D), lambda n, k: (0, _dz_col(n))), out_shape=SDS((D, NIN), BF16),
        scratch=[pltpu.VMEM((D, D), F32)],
    )(h, dz)


def _mm_tn(a, b):
    S = a.shape[0]
    tk = min(1024, S)
    nk = S // tk

    def body(a_r, b_r, o_r, acc):
        k = pl.program_id(0)
        _accumulate(k == 0, acc, _dot_tn(a_r[...], b_r[...]))

        @pl.when(k == nk - 1)
        def _():
            o_r[...] = acc[...].astype(BF16)

    blk = pl.BlockSpec((tk, D), lambda k: (k, 0))
    return _pcall(body, name="mm_tn", grid=(nk,), in_specs=[blk, blk], out_specs=_full((D, D)),
                  out_shape=SDS((D, D), BF16), scratch=[pltpu.VMEM((D, D), F32)])(a, b)


def _mem_bwd(mem, g, wkv, dkv, dg_acc):
    def body(m_ref, g_ref, w_ref, dkv_ref, acc_ref, dw_ref, dg_ref):
        m = m_ref[...]
        mr = m * _rms_scale(m)
        mn = (mr * g_ref[...]).astype(BF16)
        dkb = dkv_ref[...].astype(BF16)
        dw_ref[...] = _dot_tn(mn, dkb).astype(BF16)
        dg_ref[...] = acc_ref[...] + _rowsum(_dot_nt(dkb, w_ref[...]) * mr)

    return _pcall(body, name="mem_bwd", in_specs=[VMEM] * 5, out_specs=[VMEM] * 2,
                  out_shape=[SDS((D, 2 * D), BF16), SDS((1, D), F32)], vmem_mb=48)(mem, g, wkv, dkv, dg_acc)


def _adamw_math(w, g, m, v):
    m2 = ADAM_B1 * m + (1.0 - ADAM_B1) * g
    v2 = ADAM_B2 * v + (1.0 - ADAM_B2) * (g * g)
    mh = m2 / (1.0 - ADAM_B1 ** ADAM_STEP)
    vh = v2 / (1.0 - ADAM_B2 ** ADAM_STEP)
    return -ADAM_LR * (mh / (jnp.sqrt(vh) + ADAM_EPS) + ADAM_WD * w), m2, v2


def _adamw_layer(l, w, m, v, g, prev, which=None, rows=128):
    L, R, C = w.shape

    def body(w_r, m_r, v_r, g_r, *rest):
        g_o, d_o, m_o, v_o = rest[-4:]
        g = g_r[...]
        d, m2, v2 = _adamw_math(w_r[...], g, m_r[...], v_r[...])
        g_o[...] = g
        d_o[...] = d
        m_o[...] = m2
        v_o[...] = v2

    st = pl.BlockSpec((None, rows, C), lambda i: (l, i, 0))
    gs = pl.BlockSpec((rows, C), lambda i: (i, 0)) if which is None else pl.BlockSpec((None, rows, C), lambda i: (which, i, 0))
    carried = list(prev) if prev is not None else []
    return _pcall(body, name="adamw_layer", grid=(R // rows,), in_specs=[st] * 3 + [gs] + [HBM] * len(carried),
                  out_specs=[st] * 4, out_shape=[SDS(w.shape, F32)] * 4,
                  aliases={4 + k: k for k in range(len(carried))} or None)(w, m, v, g, *carried)


def _adamw_flat(w, m, v, g, rows):
    R, C = w.shape

    def body(w_r, m_r, v_r, g_r, d_o, m_o, v_o):
        d, m2, v2 = _adamw_math(w_r[...], g_r[...], m_r[...], v_r[...])
        d_o[...] = d
        m_o[...] = m2
        v_o[...] = v2

    blk = pl.BlockSpec((rows, C), lambda i: (i, 0))
    return _pcall(body, name="adamw_flat", grid=(R // rows,), in_specs=[blk] * 4, out_specs=[blk] * 3,
                  out_shape=[SDS((R, C), F32)] * 3)(w, m, v, g)


_SMALL = ("mem_norm_g", "pre_norm_g", "post_norm_g", "gmlp_ln_g", "gmlp_ln_b", "gmlp_ws", "gmlp_bs", "conv_b", "lru_wr",
          "lru_br", "lru_wi", "lru_bi", "lru_lambda")


def _pack_small(parts, conv_w_part):
    rows = [parts[n].reshape(-1, 128) for n in _SMALL] + [conv_w_part.reshape(-1, 128)]
    used = sum(r.shape[0] for r in rows)
    rows.append(jnp.zeros((SMALL_ROWS - used, 128), F32))
    return jnp.concatenate(rows, axis=0)


def _unpack_small(pack, shapes):
    out, at = {}, 0
    for n in _SMALL:
        size = 1
        for s in shapes[n]:
            size *= s
        out[n] = pack[at:at + size // 128].reshape(shapes[n])
        at += size // 128
    return out, at


def kernel(x, mem, mem_norm_g, pre_norm_g, post_norm_g, w_in, gmlp_ln_g, gmlp_ln_b, gmlp_ws, gmlp_bs, conv_w, conv_b, lru_wr, lru_br, lru_wi, lru_bi, lru_lambda, w_kv, w_pa, w_pb, w_pc, w_out, loss_target, m_mem_norm_g, m_pre_norm_g, m_post_norm_g, m_w_in, m_gmlp_ln_g, m_gmlp_ln_b, m_gmlp_ws, m_gmlp_bs, m_conv_w, m_conv_b, m_lru_wr, m_lru_br, m_lru_wi, m_lru_bi, m_lru_lambda, m_w_kv, m_w_pa, m_w_pb, m_w_pc, m_w_out, v_mem_norm_g, v_pre_norm_g, v_post_norm_g, v_w_in, v_gmlp_ln_g, v_gmlp_ln_b, v_gmlp_ws, v_gmlp_bs, v_conv_w, v_conv_b, v_lru_wr, v_lru_br, v_lru_wi, v_lru_bi, v_lru_lambda, v_w_kv, v_w_pa, v_w_pb, v_w_pc, v_w_out):
    L = w_in.shape[0]
    S = x.shape[1]
    xs = [x[0]]
    mem2 = mem[0]
    mg = mem_norm_g.reshape(1, D)
    vec = lambda a, l: a[l].reshape(1, D)
    ci = lax.axis_index("c")
    jpos = 2 * lax.axis_index("x") + lax.axis_index("y")
    pos = jnp.reshape(jpos, (1,)).astype(jnp.int32)

    cw8 = jnp.pad(conv_w, ((0, 0), (0, 4), (0, 0)))
    started = [_gather_start(l, _cast_place(l, pos, w_in, w_kv, w_pa, w_pb, w_pc, w_out, cw8)) for l in range(L)]
    tok = started[0][-1][0, 0]
    for st in started[1:]:
        tok = tok + st[-1][0, 0]
    W = [None] * L
    W[0] = _gather_end(0, _gather_mid(0, started[0], started[L - 1][-1]), started[L - 1][-1])

    saved = []
    for l in range(L):
        Win, Wkv, Wp, Cw = W[l]
        z, h = _mm_in(xs[l], vec(pre_norm_g, l) + tok if l == 0 else vec(pre_norm_g, l), Win)
        bsb = jnp.broadcast_to(gmlp_bs[l][:, :, None], (NG, GB, GB))
        ya = _branch_a(z, vec(gmlp_ln_g, l), vec(gmlp_ln_b, l), gmlp_ws[l], bsb)
        if l + 1 < L:
            mid = _gather_mid(l + 1, started[l + 1], ya)
        yb, hs = _branch_b(z, Cw, vec(conv_b, l), lru_wr[l], vec(lru_br, l), lru_wi[l], vec(lru_bi, l), vec(lru_lambda, l))
        kv = _kv(mem2, mg, Wkv)
        yc = _branch_c(z, kv)
        pa, pb, pc, mgd, o, xn = _merge_out(ya, yb, yc, z, Wp, xs[l], vec(post_norm_g, l))
        if l + 1 < L:
            W[l + 1] = _gather_end(l + 1, mid, xn)
        xs.append(xn)
        saved.append((z, h, ya, yb, yc, hs, kv, pa, pb, pc, mgd, o, bsb))

    loss11, dxn = _loss_head(xs[L], loss_target[0])
    loss = lax.psum(loss11[0, 0], ("x", "y", "c"))

    big = dict(w_in=(w_in, m_w_in, v_w_in), w_kv=(w_kv, m_w_kv, v_w_kv), w_pa=(w_pa, m_w_pa, v_w_pa),
               w_pb=(w_pb, m_w_pb, v_w_pb), w_pc=(w_pc, m_w_pc, v_w_pc), w_out=(w_out, m_w_out, v_w_out))
    out = {n: None for n in big}
    kin, nsh, nkv, rp = w_in.shape[1], w_in.shape[2], w_kv.shape[2], w_pa.shape[1]

    def finish_layer(l, a2a, after):
        thru = _a2a_wait(l, a2a, after)
        g6, (lin, lkv, lp) = thru[0:6], thru[6:9]
        own_in = lax.dynamic_slice(g6[0], (ci * (kin // 2), jpos * nsh), (kin // 2, nsh))
        own_kv = lax.dynamic_slice(g6[1], (ci * (kin // 2), jpos * nkv), (kin // 2, nkv))
        own_p = jnp.concatenate([lax.dynamic_slice(g6[2 + k], (jpos * rp + ci * (rp // 2), 0), (rp // 2, D)) for k in range(4)], axis=0)
        g_in, g_kv, g_p = _sum_share((lin, lkv, lp.reshape(NDEV - 1, 2 * rp, D)), (own_in, own_kv, own_p))
        g_p = g_p.reshape(4, rp, D)
        out["w_in"] = _adamw_layer(l, *big["w_in"], g_in, out["w_in"])
        out["w_kv"] = _adamw_layer(l, *big["w_kv"], g_kv, out["w_kv"])
        for k, n in enumerate(("w_pa", "w_pb", "w_pc", "w_out")):
            out[n] = _adamw_layer(l, *big[n], g_p, out[n], which=k)

    small = {n: [None] * L for n in _SMALL}
    dconv_w = [None] * L
    dg_mem = jnp.zeros((1, D), F32)
    pending = None
    for l in reversed(range(L)):
        Win, Wkv, Wp, Cw = W[l]
        z, h, ya, yb, yc, hs, kv, pa, pb, pc, mgd, o, bsb = saved[l]
        pg = vec(post_norm_g, l) if pending is None else vec(post_norm_g, l) + pending[1][-1][0, 0]
        do, dpa, dpb, dpc, dya, dyb, dyc, dz, dgpost = _out_bwd(dxn, o, pg, Wp, z, pa, pb, pc)
        dz, dws, dbs, dlg, dlb = _branch_a_bwd(z, dya, vec(gmlp_ln_g, l), vec(gmlp_ln_b, l), gmlp_ws[l], bsb, dz)
        dz, dcw, dcb, dwr, dbr, dwi, dbi, dlam = _branch_b_bwd(
            z, hs, dyb, Cw, vec(conv_b, l), lru_wr[l], vec(lru_br, l), lru_wi[l], vec(lru_bi, l), vec(lru_lambda, l), dz)
        dz, dkv = _branch_c_bwd(z, kv, dyc, dz)
        g_in = _mm_dwin(h, dz)
        g_kv, dg_mem = _mem_bwd(mem2, mg, Wkv, dkv, dg_mem)
        grads = (g_in, g_kv, _mm_tn(ya, dpa), _mm_tn(yb, dpb), _mm_tn(yc, dpc), _mm_tn(mgd, do))
        lands = (lax.empty((NDEV - 1, kin // 2, nsh), BF16), lax.empty((NDEV - 1, kin // 2, nkv), BF16),
                 lax.empty((NDEV - 1, 4, rp // 2, D), BF16))
        a2a = _a2a_start(l, grads, lands)
        dx, dgpre = _mm_dh(dz, Win, xs[l], dxn, vec(pre_norm_g, l) + a2a[-1][0, 0])
        if pending is not None:
            finish_layer(pending[0], pending[1], dx)
        pending = (l, a2a)
        for n, val in (("pre_norm_g", dgpre), ("post_norm_g", dgpost), ("gmlp_ln_g", dlg), ("gmlp_ln_b", dlb), ("gmlp_ws", dws),
                       ("gmlp_bs", dbs), ("conv_b", dcb), ("lru_wr", dwr), ("lru_br", dbr), ("lru_wi", dwi), ("lru_bi", dbi),
                       ("lru_lambda", dlam)):
            small[n][l] = val
        dconv_w[l] = dcw[0:4]
        dxn = dx
    grad_x = dxn.reshape(1, S, D)

    parts = {n: jnp.stack(small[n]) for n in _SMALL if n != "mem_norm_g"}
    parts["mem_norm_g"] = dg_mem
    gsum = _allreduce_small(_pack_small(parts, jnp.stack(dconv_w)))
    given = dict(mem_norm_g=(mem_norm_g, m_mem_norm_g, v_mem_norm_g), pre_norm_g=(pre_norm_g, m_pre_norm_g, v_pre_norm_g),
                 post_norm_g=(post_norm_g, m_post_norm_g, v_post_norm_g), gmlp_ln_g=(gmlp_ln_g, m_gmlp_ln_g, v_gmlp_ln_g),
                 gmlp_ln_b=(gmlp_ln_b, m_gmlp_ln_b, v_gmlp_ln_b), gmlp_ws=(gmlp_ws, m_gmlp_ws, v_gmlp_ws),
                 gmlp_bs=(gmlp_bs, m_gmlp_bs, v_gmlp_bs), conv_b=(conv_b, m_conv_b, v_conv_b), lru_wr=(lru_wr, m_lru_wr, v_lru_wr),
                 lru_br=(lru_br, m_lru_br, v_lru_br), lru_wi=(lru_wi, m_lru_wi, v_lru_wi), lru_bi=(lru_bi, m_lru_bi, v_lru_bi),
                 lru_lambda=(lru_lambda, m_lru_lambda, v_lru_lambda))
    shapes = {n: given[n][0].shape for n in _SMALL}
    zero_cw = jnp.zeros((L, 4, D), F32)
    packs = [_pack_small({n: given[n][k] for n in _SMALL}, zero_cw) for k in range(3)]
    dsm, msm, vsm = _adamw_flat(packs[0], packs[1], packs[2], gsum, 2560)
    g_small, at = _unpack_small(gsum, shapes)
    d_small, _ = _unpack_small(dsm, shapes)
    m_small, _ = _unpack_small(msm, shapes)
    v_small, _ = _unpack_small(vsm, shapes)
    for n in _SMALL:
        out[n] = (g_small[n], d_small[n], m_small[n], v_small[n])
    g_cw = lax.dynamic_slice_in_dim(gsum[at:at + L * 4 * D // 128].reshape(L * 4, D), jpos * (D // 4), D // 4, axis=1)
    d_cw, m_cw, v_cw = _adamw_flat(conv_w.reshape(L * 4, D // 4), m_conv_w.reshape(L * 4, D // 4),
                                   v_conv_w.reshape(L * 4, D // 4), g_cw, L * 4)
    out["conv_w"] = tuple(a.reshape(L, 4, D // 4) for a in (g_cw, d_cw, m_cw, v_cw))

    finish_layer(pending[0], pending[1], dsm)

    order = ("mem_norm_g", "pre_norm_g", "post_norm_g", "w_in", "gmlp_ln_g", "gmlp_ln_b", "gmlp_ws", "gmlp_bs", "conv_w", "conv_b",
             "lru_wr", "lru_br", "lru_wi", "lru_bi", "lru_lambda", "w_kv", "w_pa", "w_pb", "w_pc", "w_out")
    return (loss, grad_x, *[out[n][0] for n in order], *[out[n][1] for n in order], *[out[n][2] for n in order],
            *[out[n][3] for n in order])
```

```python
import functools

import jax
import jax.numpy as jnp
from jax import lax
from jax.experimental import pallas as pl
from jax.experimental.pallas import tpu as pltpu

F32 = jnp.float32
BF16 = jnp.bfloat16
SDS = jax.ShapeDtypeStruct
MESH = pl.DeviceIdType.MESH

D = 1024
NIN = 10 * D
MEM = 256
GB = 128
NG = 8
NH = 4
HD = D // NH
EPS = 1e-6
LRU_C = 8.0
ADAM_LR, ADAM_B1, ADAM_B2, ADAM_EPS, ADAM_WD, ADAM_STEP = 0.001, 0.9, 0.999, 1e-08, 0.01, 10
NDEV = 8
SMALL_ROWS = 12800

_CALL_KW = {}
HBM = pl.BlockSpec(memory_space=pltpu.HBM)
VMEM = pl.BlockSpec(memory_space=pltpu.VMEM)
SEM = pl.BlockSpec(memory_space=pltpu.SEMAPHORE)
ANY = pl.BlockSpec(memory_space=pl.ANY)
TOKEN = SDS((8, 128), F32)


def _pcall(body, *, name, in_specs, out_specs, out_shape, grid=None, scratch=(), vmem_mb=48, aliases=None, effect=False,
           prefetch=0):
    kw = dict(_CALL_KW)
    if aliases:
        kw["input_output_aliases"] = aliases
    params = dict(vmem_limit_bytes=vmem_mb << 20)
    if grid is not None:
        params["dimension_semantics"] = ("arbitrary",) * len(grid)
    if effect:
        params["has_side_effects"] = pltpu.SideEffectType.DATAFLOW_SIDE_EFFECTING
    if prefetch:
        kw["grid_spec"] = pltpu.PrefetchScalarGridSpec(num_scalar_prefetch=prefetch, grid=grid, in_specs=in_specs,
                                                       out_specs=out_specs, scratch_shapes=list(scratch))
    else:
        kw.update(in_specs=in_specs, out_specs=out_specs, scratch_shapes=list(scratch))
        if grid is not None:
            kw["grid"] = grid
    return pl.pallas_call(body, name=name, out_shape=out_shape, compiler_params=pltpu.CompilerParams(**params), **kw)


def _full(shape):
    nd = len(shape)
    return pl.BlockSpec(shape, lambda *_: (0,) * nd)


def _dot(a, b):
    return jnp.dot(a, b, preferred_element_type=F32)


def _dot_nt(a, b):
    return lax.dot_general(a, b, (((1,), (1,)), ((), ())), preferred_element_type=F32)


def _dot_tn(a, b):
    return lax.dot_general(a, b, (((0,), (0,)), ((), ())), preferred_element_type=F32)


def _rowsum(a):
    return jnp.sum(a, axis=0, keepdims=True)


def _silu_parts(g):
    s = jax.nn.sigmoid(g)
    return g * s, s * (1.0 + g * (1.0 - s))


def _rms_scale(x):
    return lax.rsqrt(jnp.mean(x * x, axis=-1, keepdims=True) + EPS)


def _dz_col(k):
    return jnp.where(k < 3, k, jnp.where(k < 6, k + 4, k - 3))


def _coords():
    return lax.axis_index("x"), lax.axis_index("y"), lax.axis_index("c")


def _other_chips(x, y):
    return [(1 - x, y), (x, 1 - y), (1 - x, 1 - y)]


def _peer(x, y, c, mask):
    return (1 - x if mask & 4 else x, 1 - y if mask & 2 else y, 1 - c if mask & 1 else c)


def _remote(src, dst, ssem, rsem, k, to):
    return pltpu.make_async_remote_copy(src_ref=src, dst_ref=dst, send_sem=ssem.at[k], recv_sem=rsem.at[k], device_id=to,
                                        device_id_type=MESH)


def _w_half(a, ref, jj, cc):
    if a == 2:
        rp = ref.shape[1] // 4
        return ref.at[:, pl.ds(jj * rp + cc * (rp // 2), rp // 2), :]
    kin, nsh = ref.shape[0], ref.shape[1] // 4
    return ref.at[pl.ds(cc * (kin // 2), kin // 2), pl.ds(jj * nsh, nsh)]


def _cw_block(ref, jj):
    return ref.at[:, pl.ds(jj * (D // 4), D // 4)]


def _cast_place(l, pos, w_in, w_kv, w_pa, w_pb, w_pc, w_out, cw8):
    kin, nsh = w_in.shape[1], w_in.shape[2]
    nkv, rp = w_kv.shape[2], w_pa.shape[1]
    half = kin // 2

    def body(pos_r, win, wkv, pa, pb, pc, po, cw, Win, Wkv, Wp, Cw):
        Win[...] = win[...].astype(BF16)
        Wkv[...] = wkv[...].astype(BF16)

        @pl.when(pl.program_id(0) == 0)
        def _():
            for k, r in enumerate((pa, pb, pc, po)):
                Wp[k] = r[...].astype(BF16)
            Cw[...] = cw[...]

    proj = pl.BlockSpec((None, rp, D), lambda i, p: (l, 0, 0))
    return _pcall(
        body, name="cast_place", grid=(2,), prefetch=1,
        in_specs=[pl.BlockSpec((None, half, nsh), lambda i, p: (l, i, 0)), pl.BlockSpec((None, half, nkv), lambda i, p: (l, i, 0)),
                  proj, proj, proj, proj, pl.BlockSpec((None, 8, D // 4), lambda i, p: (l, 0, 0))],
        out_specs=[pl.BlockSpec((half, nsh), lambda i, p: (i, p[0])), pl.BlockSpec((half, nkv), lambda i, p: (i, p[0])),
                   pl.BlockSpec((4, rp, D), lambda i, p: (0, p[0], 0)), pl.BlockSpec((8, D // 4), lambda i, p: (0, p[0]))],
        out_shape=[SDS((kin, 4 * nsh), BF16), SDS((kin, 4 * nkv), BF16), SDS((4, 4 * rp, D), BF16), SDS((8, D), F32)],
    )(pos, w_in, w_kv, w_pa, w_pb, w_pc, w_out, cw8)


def _hbm_like(bufs):
    return [pltpu.HBM(b.shape, b.dtype) for b in bufs]


def _gather_start(l, bufs, after=None):
    extra = [] if after is None else [after]

    def body(win, wkv, wp, cw, *rest):
        ssem, rsem, token = rest[len(extra)], rest[len(extra) + 1], rest[-1]
        x, y, c = _coords()
        j = 2 * x + y
        refs = (win, wkv, wp)
        for k, chip in enumerate(_other_chips(x, y)):
            to = (chip[0], chip[1], c)
            for a in range(3):
                half = _w_half(a, refs[a], j, c)
                _remote(half, half, ssem, rsem, a * 3 + k, to).start()
            mine = _cw_block(cw, j)
            _remote(mine, mine, ssem, rsem, 9 + k, to).start()
        token[...] = jnp.zeros((8, 128), F32)

    return _pcall(
        body, name=f"gather_start_{l}", in_specs=[HBM] * 4 + [ANY] * len(extra), out_specs=[SEM, SEM] + [HBM] * 4 + [VMEM],
        out_shape=[pltpu.SemaphoreType.DMA((12,)), pltpu.SemaphoreType.DMA((12,))] + _hbm_like(bufs) + [TOKEN],
        aliases={0: 2, 1: 3, 2: 4, 3: 5}, effect=True,
    )(*[pltpu.with_memory_space_constraint(b, pltpu.HBM) for b in bufs], *extra)


def _gather_mid(l, started, after):
    ssem, rsem, b0, b1, b2, b3, _ = started

    def body(win, wkv, wp, cw, ssem, rsem, after_r, ssem2, rsem2, o0, o1, o2, o3, token):
        x, y, c = _coords()
        j = 2 * x + y
        me, sib = (x, y, c), (x, y, 1 - c)
        token[...] = jnp.zeros((8, 128), F32)
        refs = (win, wkv, wp)
        chips = _other_chips(x, y)
        for k, chip in enumerate(chips):
            jk = 2 * chip[0] + chip[1]
            for a in range(3):
                got = _w_half(a, refs[a], jk, c)
                _remote(got, got, ssem, rsem, a * 3 + k, me).wait_recv()
            got = _cw_block(cw, jk)
            _remote(got, got, ssem, rsem, 9 + k, me).wait_recv()
        for k in range(3):
            for a in range(3):
                half = _w_half(a, refs[a], j, c)
                _remote(half, half, ssem, rsem, a * 3 + k, me).wait_send()
            mine = _cw_block(cw, j)
            _remote(mine, mine, ssem, rsem, 9 + k, me).wait_send()
        for k, chip in enumerate(chips):
            jk = 2 * chip[0] + chip[1]
            for a in range(3):
                got = _w_half(a, refs[a], jk, c)
                _remote(got, got, ssem2, rsem2, a * 3 + k, sib).start()

    bufs = (b0, b1, b2, b3)
    return _pcall(
        body, name=f"gather_mid_{l}", in_specs=[HBM] * 4 + [SEM, SEM, ANY], out_specs=[SEM, SEM] + [HBM] * 4 + [VMEM],
        out_shape=[pltpu.SemaphoreType.DMA((9,)), pltpu.SemaphoreType.DMA((9,))] + _hbm_like(bufs) + [TOKEN],
        aliases={0: 2, 1: 3, 2: 4, 3: 5}, effect=True,
    )(b0, b1, b2, b3, ssem, rsem, after)


def _gather_end(l, mid, after):
    ssem2, rsem2, b0, b1, b2, b3, _ = mid

    def body(win, wkv, wp, cw, ssem2, rsem2, after_r, o0, o1, o2, o3):
        x, y, c = _coords()
        me = (x, y, c)
        refs = (win, wkv, wp)
        for k, chip in enumerate(_other_chips(x, y)):
            jk = 2 * chip[0] + chip[1]
            for a in range(3):
                got = _w_half(a, refs[a], jk, 1 - c)
                _remote(got, got, ssem2, rsem2, a * 3 + k, me).wait_recv()
                sent = _w_half(a, refs[a], jk, c)
                _remote(sent, sent, ssem2, rsem2, a * 3 + k, me).wait_send()

    bufs = (b0, b1, b2, b3)
    return _pcall(
        body, name=f"gather_end_{l}", in_specs=[HBM] * 4 + [SEM, SEM, ANY], out_specs=[HBM] * 4, out_shape=_hbm_like(bufs),
        aliases={0: 0, 1: 1, 2: 2, 3: 3}, effect=True,
    )(b0, b1, b2, b3, ssem2, rsem2, after)


def _g_piece(a, ref, jd, dc):
    if a >= 2:
        rp = ref.shape[0] // 4
        return ref.at[pl.ds(jd * rp + dc * (rp // 2), rp // 2), :]
    kin, nsh = ref.shape[0], ref.shape[1] // 4
    return ref.at[pl.ds(dc * (kin // 2), kin // 2), pl.ds(jd * nsh, nsh)]


def _land_slot(a, lands, s):
    if a < 2:
        return lands[a].at[s]
    return lands[2].at[s, a - 2]


def _a2a_start(l, grads, lands):
    def body(*refs):
        g, ld, ssem, rsem, token = refs[0:6], refs[6:9], refs[9], refs[10], refs[-1]
        x, y, c = _coords()
        for mask in range(1, NDEV):
            p = _peer(x, y, c, mask)
            for a in range(6):
                _remote(_g_piece(a, g[a], 2 * p[0] + p[1], p[2]), _land_slot(a, ld, mask - 1), ssem, rsem, a * 7 + mask - 1, p).start()
        token[...] = jnp.zeros((8, 128), F32)

    bufs = tuple(grads) + tuple(lands)
    return _pcall(
        body, name=f"a2a_start_{l}", in_specs=[HBM] * 9, out_specs=[SEM, SEM] + [HBM] * 9 + [VMEM],
        out_shape=[pltpu.SemaphoreType.DMA((42,)), pltpu.SemaphoreType.DMA((42,))] + _hbm_like(bufs) + [TOKEN],
        aliases={i: 2 + i for i in range(9)}, effect=True,
    )(*[pltpu.with_memory_space_constraint(b, pltpu.HBM) for b in bufs])


def _a2a_wait(l, started, after):
    ssem, rsem = started[0], started[1]
    bufs = tuple(started[2:11])

    def body(*refs):
        g, ld, ssem, rsem = refs[0:6], refs[6:9], refs[9], refs[10]
        x, y, c = _coords()
        me = (x, y, c)
        for mask in range(1, NDEV):
            for a in range(6):
                got = _land_slot(a, ld, mask - 1)
                _remote(got, got, ssem, rsem, a * 7 + mask - 1, me).wait_recv()
        for mask in range(1, NDEV):
            p = _peer(x, y, c, mask)
            for a in range(6):
                sent = _g_piece(a, g[a], 2 * p[0] + p[1], p[2])
                _remote(sent, sent, ssem, rsem, a * 7 + mask - 1, me).wait_send()

    return _pcall(
        body, name=f"a2a_wait_{l}", in_specs=[HBM] * 9 + [SEM, SEM, ANY], out_specs=[HBM] * 9, out_shape=_hbm_like(bufs),
        aliases={i: i for i in range(9)}, effect=True,
    )(*bufs, ssem, rsem, after)


def _sum_share(lands, owns):
    rows, n = 128, 4
    widths = [o.shape[1] for o in owns]

    def body(l0, w0, l1, w1, l2, w2, g0, g1, g2, b0, b1, b2, lsem, ssem, rsem):
        i = pl.program_id(0)
        x, y, c = _coords()
        sib = (x, y, 1 - c)
        ld, ow, gs, bufs = (l0, l1, l2), (w0, w1, w2), (g0, g1, g2), (b0, b1, b2)

        def dst(a, step):
            row = step * (2 * rows) + c * rows if a == 2 else c * (n * rows) + step * rows
            return gs[a].at[pl.ds(row, rows), :]

        def copies(a, step, sl):
            src = bufs[a].at[sl]
            lc = pltpu.make_async_copy(src, dst(a, step), lsem.at[a, sl])
            rc = pltpu.make_async_remote_copy(src_ref=src, dst_ref=dst(a, step), send_sem=ssem.at[a, sl], recv_sem=rsem.at[a],
                                              device_id=sib, device_id_type=MESH)
            return lc, rc

        def drain(a, step, sl):
            lc, rc = copies(a, step, sl)
            lc.wait()
            rc.wait_send()

        slot = i % 2

        @pl.when(i >= 2)
        def _():
            for a in range(3):
                drain(a, i - 2, slot)

        for a in range(3):
            acc = ow[a][...].astype(F32)
            for k in range(NDEV - 1):
                acc = acc + ld[a][k].astype(F32)
            bufs[a][slot] = acc
            lc, rc = copies(a, i, slot)
            lc.start()
            rc.start()

        @pl.when(i == n - 1)
        def _():
            for a in range(3):
                drain(a, n - 2, (n - 2) % 2)
                drain(a, n - 1, (n - 1) % 2)
                whole = gs[a].at[pl.ds(0, n * rows), :]
                pltpu.make_async_remote_copy(src_ref=whole, dst_ref=whole, send_sem=ssem.at[a, 0], recv_sem=rsem.at[a],
                                             device_id=(x, y, c), device_id_type=MESH).wait_recv()

    in_specs = []
    for w in widths:
        in_specs += [pl.BlockSpec((NDEV - 1, rows, w), lambda i: (0, i, 0)), pl.BlockSpec((rows, w), lambda i: (i, 0))]
    args = [t for pair in zip(lands, owns) for t in pair]
    return _pcall(
        body, name="sum_share", grid=(n,), in_specs=in_specs, out_specs=[HBM] * 3,
        out_shape=[SDS((2 * n * rows, w), F32) for w in widths],
        scratch=[pltpu.VMEM((2, rows, w), F32) for w in widths]
        + [pltpu.SemaphoreType.DMA((3, 2)), pltpu.SemaphoreType.DMA((3, 2)), pltpu.SemaphoreType.DMA((3,))],
    )(*args)


def _allreduce_small(pack):
    R = pack.shape[0]
    r8 = R // NDEV

    def body(p_ref, full, land, red, ssem, rsem, ssem2, rsem2, lsem):
        x, y, c = _coords()
        me = 4 * x + 2 * y + c

        def idx(p):
            return 4 * p[0] + 2 * p[1] + p[2]

        own = pltpu.make_async_copy(p_ref.at[pl.ds(me * r8, r8), :], land.at[me], lsem.at[0])
        own.start()
        sent = []
        for mask in range(1, NDEV):
            p = _peer(x, y, c, mask)
            cp = _remote(p_ref.at[pl.ds(idx(p) * r8, r8), :], land.at[me], ssem, rsem, mask - 1, p)
            cp.start()
            sent.append(cp)
        for mask in range(1, NDEV):
            got = land.at[idx(_peer(x, y, c, mask))]
            _remote(got, got, ssem, rsem, mask - 1, (x, y, c)).wait_recv()
        own.wait()
        for cp in sent:
            cp.wait_send()
        acc = land[0]
        for k in range(1, NDEV):
            acc = acc + land[k]
        red[...] = acc
        mine = full.at[pl.ds(me * r8, r8), :]
        own2 = pltpu.make_async_copy(red, mine, lsem.at[1])
        own2.start()
        sent2 = []
        for mask in range(1, NDEV):
            cp = _remote(red, mine, ssem2, rsem2, mask - 1, _peer(x, y, c, mask))
            cp.start()
            sent2.append(cp)
        for mask in range(1, NDEV):
            got = full.at[pl.ds(idx(_peer(x, y, c, mask)) * r8, r8), :]
            _remote(got, got, ssem2, rsem2, mask - 1, (x, y, c)).wait_recv()
        own2.wait()
        for cp in sent2:
            cp.wait_send()

    return _pcall(
        body, name="allreduce_small", in_specs=[HBM], out_specs=HBM, out_shape=SDS((R, 128), F32),
        scratch=[pltpu.VMEM((NDEV, r8, 128), F32), pltpu.VMEM((r8, 128), F32)]
        + [pltpu.SemaphoreType.DMA((NDEV - 1,))] * 4 + [pltpu.SemaphoreType.DMA((2,))], vmem_mb=32,
    )(pack)


def _mm_in(x, g, w):
    S = x.shape[0]
    tm, tn = min(1024, S), 1280

    def body(x_ref, g_ref, w_ref, z_ref, h_ref, hs):
        @pl.when(pl.program_id(1) == 0)
        def _():
            xv = x_ref[...]
            hb = (xv * _rms_scale(xv) * g_ref[...]).astype(BF16)
            hs[...] = hb
            h_ref[...] = hb

        z_ref[...] = _dot(hs[...], w_ref[...]).astype(BF16)

    return _pcall(
        body, name="mm_in", grid=(S // tm, NIN // tn),
        in_specs=[pl.BlockSpec((tm, D), lambda i, j: (i, 0)), _full((1, D)), pl.BlockSpec((D, tn), lambda i, j: (0, j))],
        out_specs=[pl.BlockSpec((tm, tn), lambda i, j: (i, j)), pl.BlockSpec((tm, D), lambda i, j: (i, 0))],
        out_shape=[SDS((S, NIN), BF16), SDS((S, D), BF16)], scratch=[pltpu.VMEM((tm, D), BF16)],
    )(x, g, w)


def _chunk_mask():
    ri = lax.broadcasted_iota(jnp.int32, (GB, GB), 0)
    ci = lax.broadcasted_iota(jnp.int32, (GB, GB), 1)
    return (ri >= 64) | (ci < 64)


def _layernorm_parts(v):
    mu = jnp.mean(v, axis=-1, keepdims=True)
    d = v - mu
    rs = lax.rsqrt(jnp.mean(d * d, axis=-1, keepdims=True) + EPS)
    return d * rs, rs


def _branch_a(z, lg, lb, ws, bsb):
    S = z.shape[0]
    T = min(512, S)

    def body(zu, zv, zg, lg_r, lb_r, ws_r, bs_r, ya):
        vhat, _ = _layernorm_parts(zv[...].astype(F32))
        vnb = (vhat * lg_r[...] + lb_r[...]).astype(BF16)
        sil, _ = _silu_parts(zg[...].astype(F32))
        t = zu[...].astype(F32) * sil
        mask = _chunk_mask()
        for g in range(NG):
            wg = jnp.where(mask, ws_r[g], 0.0).astype(BF16)
            cs = slice(g * GB, (g + 1) * GB)
            for n in range(T // GB):
                rs = slice(n * GB, (n + 1) * GB)
                sv = _dot(wg, vnb[rs, cs]) + bs_r[g]
                ya[rs, cs] = (t[rs, cs] * sv).astype(BF16)

    zs = lambda k: pl.BlockSpec((T, D), lambda i: (i, k))
    return _pcall(
        body, name="branch_a", grid=(S // T,),
        in_specs=[zs(0), zs(1), zs(2), _full((1, D)), _full((1, D)), _full((NG, GB, GB)), _full((NG, GB, GB))],
        out_specs=pl.BlockSpec((T, D), lambda i: (i, 0)), out_shape=SDS((S, D), BF16),
    )(z, z, z, lg, lb, ws, bsb)


def _softplus_neg(lam):
    e = jnp.exp(-jnp.abs(lam))
    l1p = jnp.where(e < 1e-2, e * (1.0 - e * (0.5 - e * (1.0 / 3.0))), jnp.log(1.0 + e))
    return jnp.maximum(-lam, 0.0) + l1p


CH = 16


def _ck(c, off=0):
    return pl.ds(c * CH + off, CH)


def _half_sum(v):
    return v[0:8, :] + v[8:16, :]


def _lru_conv(xpad, c, cw, cb):
    xk = [xpad[_ck(c, 5 + k), :] for k in range(4)]
    return xk, cb + (((xk[0] * cw[0:1] + xk[1] * cw[1:2]) + xk[2] * cw[2:3]) + xk[3] * cw[3:4])


def _lru_gate_matmuls(xcb_s, wr_r, wi_r, pr_s, pi_s):
    for h in range(NG):
        cs = slice(h * GB, (h + 1) * GB)
        pr_s[:, cs] = _dot(xcb_s[:, cs], wr_r[h].astype(BF16))
        pi_s[:, cs] = _dot(xcb_s[:, cs], wi_r[h].astype(BF16))


def _lru_gates(pr, pi, br, bi, sp8):
    r = jax.nn.sigmoid(pr + br)
    ig = jax.nn.sigmoid(pi + bi)
    la = sp8 * r
    a = jnp.exp(la)
    a2 = a * a
    mult = jnp.sqrt(-jnp.tanh(la) * (a2 + 1.0))
    return r, ig, a, a2, mult


def _tile_rows():
    return lax.broadcasted_iota(jnp.int32, (8, D), 0)


def _scan_forward(a_s, u_s, h_s, hcar, T):
    row = _tile_rows()

    def tile(i, hp):
        o = pl.multiple_of(i * 8, 8)
        A = a_s[pl.ds(o, 8), :]
        U = u_s[pl.ds(o, 8), :]
        for s in (1, 2, 4):
            m = row >= s
            U = jnp.where(m, U + A * pltpu.roll(U, s, 0), U)
            A = jnp.where(m, A * pltpu.roll(A, s, 0), A)
        H = U + A * hp
        h_s[pl.ds(o, 8), :] = H
        return jnp.broadcast_to(H[7:8, :], (8, D))

    hcar[...] = lax.fori_loop(0, T // 8, tile, hcar[...])


def _scan_reverse(b_s, d_s, l_s, lcar, T):
    row = _tile_rows()
    n = T // 8

    def tile(i, lp):
        o = pl.multiple_of((n - 1 - i) * 8, 8)
        B = b_s[pl.ds(o, 8), :]
        U = d_s[pl.ds(o, 8), :]
        for s in (1, 2, 4):
            m = row < 8 - s
            U = jnp.where(m, U + B * pltpu.roll(U, 8 - s, 0), U)
            B = jnp.where(m, B * pltpu.roll(B, 8 - s, 0), B)
        Lm = U + B * lp
        l_s[pl.ds(o, 8), :] = Lm
        return jnp.broadcast_to(Lm[0:1, :], (8, D))

    lcar[...] = lax.fori_loop(0, n, tile, lcar[...])


def _branch_b(z, cw, cb, wr, br, wi, bi, lam):
    S = z.shape[0]
    T = min(256, S)

    def body(zxb, zgb, cw_r, cb_r, wr_r, br_r, wi_r, bi_r, lam_r, yb, hs_o, xpad, xc_s, xcb_s, pr_s, pi_s, a_s, u_s, hcar):
        @pl.when(pl.program_id(0) == 0)
        def _():
            xpad[pl.ds(0, 8), :] = jnp.zeros((8, D), F32)
            hcar[...] = jnp.zeros((8, D), F32)

        cw, cb, br, bi = cw_r[...], cb_r[...], br_r[...], bi_r[...]
        sp8 = -LRU_C * _softplus_neg(lam_r[...])
        for c in range(T // CH):
            xpad[_ck(c, 8), :] = zxb[_ck(c), :].astype(F32)
            _, xc = _lru_conv(xpad, c, cw, cb)
            xc_s[_ck(c), :] = xc
            xcb_s[_ck(c), :] = xc.astype(BF16)
        _lru_gate_matmuls(xcb_s, wr_r, wi_r, pr_s, pi_s)
        for c in range(T // CH):
            _, ig, a, _, mult = _lru_gates(pr_s[_ck(c), :], pi_s[_ck(c), :], br, bi, sp8)
            a_s[_ck(c), :] = a
            u_s[_ck(c), :] = mult * (ig * xc_s[_ck(c), :])
        _scan_forward(a_s, u_s, hs_o, hcar, T)
        xpad[pl.ds(0, 8), :] = xpad[pl.ds(T, 8), :]
        for c in range(T // CH):
            sil, _ = _silu_parts(zgb[_ck(c), :].astype(F32))
            yb[_ck(c), :] = (hs_o[_ck(c), :] * sil).astype(BF16)

    zs = lambda k: pl.BlockSpec((T, D), lambda i: (i, k))
    row = pl.BlockSpec((T, D), lambda i: (i, 0))
    return _pcall(
        body, name="branch_b", grid=(S // T,),
        in_specs=[zs(3), zs(4), _full((8, D)), _full((1, D)), _full((NG, GB, GB)), _full((1, D)), _full((NG, GB, GB)),
                  _full((1, D)), _full((1, D))],
        out_specs=[row, row], out_shape=[SDS((S, D), BF16), SDS((S, D), F32)],
        scratch=[pltpu.VMEM((T + 8, D), F32), pltpu.VMEM((T, D), F32), pltpu.VMEM((T, D), BF16)]
        + [pltpu.VMEM((T, D), F32)] * 4 + [pltpu.VMEM((8, D), F32)],
    )(z, z, cw, cb, wr, br, wi, bi, lam)


def _kv(mem, g, wkv):
    def body(m_ref, g_ref, w_ref, kv_ref):
        m = m_ref[...]
        mn = (m * _rms_scale(m) * g_ref[...]).astype(BF16)
        kv_ref[...] = _dot(mn, w_ref[...]).astype(BF16)

    return _pcall(body, name="mem_kv", in_specs=[VMEM] * 3, out_specs=VMEM, out_shape=SDS((MEM, 2 * D), BF16),
                  vmem_mb=32)(mem, g, wkv)


def _softmax_rows(s):
    e = jnp.exp(s - jnp.max(s, axis=-1, keepdims=True))
    return e / jnp.sum(e, axis=-1, keepdims=True)


def _branch_c(z, kv):
    S = z.shape[0]
    T = min(512, S)

    def body(zq, zg, kv_r, yc):
        sil, _ = _silu_parts(zg[...].astype(F32))
        for h in range(NH):
            cs = slice(h * HD, (h + 1) * HD)
            p = _softmax_rows(_dot_nt(zq[:, cs], kv_r[:, cs]) * (HD ** -0.5))
            att = _dot(p.astype(BF16), kv_r[:, D + h * HD:D + (h + 1) * HD])
            yc[:, cs] = (att * sil[:, cs]).astype(BF16)

    zs = lambda k: pl.BlockSpec((T, D), lambda i: (i, k))
    return _pcall(body, name="branch_c", grid=(S // T,), in_specs=[zs(5), zs(6), _full((MEM, 2 * D))],
                  out_specs=pl.BlockSpec((T, D), lambda i: (i, 0)), out_shape=SDS((S, D), BF16))(z, z, kv)


def _merge_out(ya, yb, yc, z, wp, x, pg):
    S = x.shape[0]
    T = min(256, S)

    def body(ya_r, yb_r, yc_r, m0, m1, m2, wp_r, x_r, pg_r, pa_o, pb_o, pc_o, mg_o, o_o, xn_o):
        merged = None
        for y_r, ml, p_o, k in ((ya_r, m0, pa_o, 0), (yb_r, m1, pb_o, 1), (yc_r, m2, pc_o, 2)):
            p = _dot(y_r[...], wp_r[k])
            p_o[...] = p.astype(BF16)
            t = jax.nn.sigmoid(ml[...].astype(F32)) * p
            merged = t if merged is None else merged + t
        mb = merged.astype(BF16)
        mg_o[...] = mb
        o = _dot(mb, wp_r[3])
        o_o[...] = o.astype(BF16)
        xn_o[...] = x_r[...] + o * _rms_scale(o) * pg_r[...]

    row = pl.BlockSpec((T, D), lambda i: (i, 0))
    zs = lambda k: pl.BlockSpec((T, D), lambda i: (i, k))
    return _pcall(
        body, name="merge_out", grid=(S // T,),
        in_specs=[row, row, row, zs(7), zs(8), zs(9), _full((4, D, D)), row, _full((1, D))],
        out_specs=[row] * 6, out_shape=[SDS((S, D), BF16)] * 5 + [SDS((S, D), F32)], vmem_mb=56,
    )(ya, yb, yc, z, z, z, wp, x, pg)


def _loss_head(y, t):
    S = y.shape[0]
    T = min(512, S)

    def body(y_r, t_r, loss_o, dy_o):
        @pl.when(pl.program_id(0) == 0)
        def _():
            loss_o[...] = jnp.zeros((1, 1), F32)

        e = y_r[...] - t_r[...]
        dy_o[...] = e * (1.0 / D)
        loss_o[...] += 0.5 * _rowsum(jnp.sum(e * e, axis=1, keepdims=True) * (1.0 / D))

    row = pl.BlockSpec((T, D), lambda i: (i, 0))
    return _pcall(body, name="loss_head", grid=(S // T,), in_specs=[row, row], out_specs=[_full((1, 1)), row],
                  out_shape=[SDS((1, 1), F32), SDS((S, D), F32)])(y, t)


def _accumulate(first, ref, val):
    @pl.when(first)
    def _():
        ref[...] = val

    @pl.when(jnp.logical_not(first))
    def _():
        ref[...] += val


def _out_bwd(dxn, o, pg, wp, z, pa, pb, pc):
    S = dxn.shape[0]
    T = min(256, S)

    def body(dy_r, o_r, pg_r, wp_r, m0, m1, m2, pa_r, pb_r, pc_r, do_o, dpa_o, dpb_o, dpc_o, dya_o, dyb_o, dyc_o, dz_o, dg_o):
        dy = dy_r[...]
        o = o_r[...].astype(F32)
        r2 = _rms_scale(o)
        w = dy * pg_r[...]
        do = r2 * w - o * (r2 * r2 * r2) * jnp.mean(w * o, axis=-1, keepdims=True)
        _accumulate(pl.program_id(0) == 0, dg_o, _rowsum(dy * o * r2))
        dob = do.astype(BF16)
        do_o[...] = dob
        dm = _dot_nt(dob, wp_r[3])
        for k, (ml, p_r, dp_o, dy_o) in enumerate(((m0, pa_r, dpa_o, dya_o), (m1, pb_r, dpb_o, dyb_o), (m2, pc_r, dpc_o, dyc_o))):
            gk = jax.nn.sigmoid(ml[...].astype(F32))
            dz_o[k] = (dm * p_r[...].astype(F32) * gk * (1.0 - gk)).astype(BF16)
            dpk = (gk * dm).astype(BF16)
            dp_o[...] = dpk
            dy_o[...] = _dot_nt(dpk, wp_r[k]).astype(BF16)

    row = pl.BlockSpec((T, D), lambda i: (i, 0))
    zs = lambda k: pl.BlockSpec((T, D), lambda i: (i, k))
    return _pcall(
        body, name="out_bwd", grid=(S // T,),
        in_specs=[row, row, _full((1, D)), _full((4, D, D)), zs(7), zs(8), zs(9), row, row, row],
        out_specs=[row] * 7 + [pl.BlockSpec((3, T, D), lambda i: (1, i, 0)), _full((1, D))],
        out_shape=[SDS((S, D), BF16)] * 7 + [SDS((10, S, D), BF16), SDS((1, D), F32)], vmem_mb=56,
    )(dxn, o, pg, wp, z, z, z, pa, pb, pc)


def _branch_a_bwd(z, dya, lg, lb, ws, bsb, dz):
    S = z.shape[0]
    T = min(512, S)
    nblk = S // T

    def body(zu, zv, zg, dy_r, lg_r, lb_r, ws_r, bs_r, dz_in, dz_o, dws_o, dbs_o, dlg_o, dlb_o, dvn_s, bacc):
        i = pl.program_id(0)

        @pl.when(i == 0)
        def _():
            dws_o[...] = jnp.zeros((NG, GB, GB), F32)
            bacc[...] = jnp.zeros((NG, GB, GB), F32)

        vhat, rs = _layernorm_parts(zv[...].astype(F32))
        vnb = (vhat * lg_r[...] + lb_r[...]).astype(BF16)
        ga = zg[...].astype(F32)
        sil, dsil = _silu_parts(ga)
        u = zu[...].astype(F32)
        dy = dy_r[...].astype(F32)
        t = dy * sil
        dsv_all = t * u
        dga_pre = dy * u * dsil
        mask = _chunk_mask()
        for g in range(NG):
            wf = jnp.where(mask, ws_r[g], 0.0)
            wg = wf.astype(BF16)
            wgt = wf.T.astype(BF16)
            cs = slice(g * GB, (g + 1) * GB)
            dw = jnp.zeros((GB, GB), F32)
            db = jnp.zeros((GB, GB), F32)
            for n in range(T // GB):
                rsl = slice(n * GB, (n + 1) * GB)
                vb = vnb[rsl, cs]
                sv = _dot(wg, vb) + bs_r[g]
                dz_o[0, rsl, cs] = (t[rsl, cs] * sv).astype(BF16)
                dz_o[2, rsl, cs] = (dga_pre[rsl, cs] * sv).astype(BF16)
                dsv = dsv_all[rsl, cs]
                dsb = dsv.astype(BF16)
                dvn_s[rsl, cs] = _dot(wgt, dsb)
                dw = dw + _dot_nt(dsb, vb)
                db = db + dsv
            dws_o[g] += jnp.where(mask, dw, 0.0)
            bacc[g] += db
        dvn = dvn_s[...]
        dvh = dvn * lg_r[...]
        dv = rs * (dvh - jnp.mean(dvh, axis=-1, keepdims=True) - vhat * jnp.mean(dvh * vhat, axis=-1, keepdims=True))
        dz_o[1] = dv.astype(BF16)
        _accumulate(i == 0, dlg_o, _rowsum(dvn * vhat))
        _accumulate(i == 0, dlb_o, _rowsum(dvn))

        @pl.when(i == nblk - 1)
        def _():
            for g in range(NG):
                dbs_o[g:g + 1, :] = _rowsum(bacc[g].T)

    zs = lambda k: pl.BlockSpec((T, D), lambda i: (i, k))
    return _pcall(
        body, name="branch_a_bwd", grid=(nblk,),
        in_specs=[zs(0), zs(1), zs(2), pl.BlockSpec((T, D), lambda i: (i, 0)), _full((1, D)), _full((1, D)),
                  _full((NG, GB, GB)), _full((NG, GB, GB)), HBM],
        out_specs=[pl.BlockSpec((3, T, D), lambda i: (0, i, 0)), _full((NG, GB, GB)), _full((NG, GB)), _full((1, D)),
                   _full((1, D))],
        out_shape=[SDS((10, S, D), BF16), SDS((NG, GB, GB), F32), SDS((NG, GB), F32), SDS((1, D), F32), SDS((1, D), F32)],
        scratch=[pltpu.VMEM((T, D), F32), pltpu.VMEM((NG, GB, GB), F32)], aliases={8: 0},
    )(z, z, z, dya, lg, lb, ws, bsb, dz)


def _branch_b_bwd(z, hs, dyb, cw, cb, wr, br, wi, bi, lam, dz):
    S = z.shape[0]
    T = min(256, S)
    nblk = S // T

    def body(zxb, zprev, zgb, hs_r, hprev_r, dy_r, cw_r, cb_r, wr_r, br_r, wi_r, bi_r, lam_r, dz_in,
             dz_o, dcw_o, dcb_o, dwr_o, dbr_o, dwi_o, dbi_o, dlam_o, xpad, hpad, apad, dpad, xc_s, pr_s, pi_s, r_s, ig_s, m_s,
             b_s, d_s, l_s, back_s, xcb_s, dprb_s, dpib_s, lcar):
        i = pl.program_id(0)
        blk = nblk - 1 - i
        first = i == 0

        @pl.when(first)
        def _():
            apad[pl.ds(T, 8), :] = jnp.zeros((8, D), F32)
            dpad[pl.ds(T, 8), :] = jnp.zeros((8, D), F32)
            lcar[...] = jnp.zeros((8, D), F32)
            dcw_o[...] = jnp.zeros((8, D), F32)
            dwr_o[...] = jnp.zeros((NG, GB, GB), F32)
            dwi_o[...] = jnp.zeros((NG, GB, GB), F32)

        keep = (blk > 0).astype(F32)
        nck = T // CH
        cw, cb, br, bi, lam = cw_r[...], cb_r[...], br_r[...], bi_r[...], lam_r[...]
        sp8 = -LRU_C * _softplus_neg(lam)
        xpad[pl.ds(0, 8), :] = zprev[...].astype(F32)[8:16, :] * keep
        hpad[pl.ds(0, 8), :] = hprev_r[...] * keep
        for c in range(nck):
            xpad[_ck(c, 8), :] = zxb[_ck(c), :].astype(F32)
            hpad[_ck(c, 8), :] = hs_r[_ck(c), :]
            _, xc = _lru_conv(xpad, c, cw, cb)
            xc_s[_ck(c), :] = xc
            xcb_s[_ck(c), :] = xc.astype(BF16)
        _lru_gate_matmuls(xcb_s, wr_r, wi_r, pr_s, pi_s)
        for c in range(nck):
            r, ig, a, _, mult = _lru_gates(pr_s[_ck(c), :], pi_s[_ck(c), :], br, bi, sp8)
            r_s[_ck(c), :] = r
            ig_s[_ck(c), :] = ig
            m_s[_ck(c), :] = mult
            apad[_ck(c), :] = a
            sil, dsil = _silu_parts(zgb[_ck(c), :].astype(F32))
            dy = dy_r[_ck(c), :].astype(F32)
            dz_o[1, _ck(c), :] = (dy * hs_r[_ck(c), :] * dsil).astype(BF16)
            d_s[_ck(c), :] = dy * sil
        for c in range(nck):
            b_s[_ck(c), :] = apad[_ck(c, 1), :]
        _scan_reverse(b_s, d_s, l_s, lcar, T)
        s_sp = s_br = s_bi = jnp.zeros((8, D), F32)
        for c in range(nck):
            lm, r, ig, mult, a, xc = l_s[_ck(c), :], r_s[_ck(c), :], ig_s[_ck(c), :], m_s[_ck(c), :], apad[_ck(c), :], xc_s[_ck(c), :]
            t = lm * mult
            dpad[_ck(c), :] = t * ig
            dl = lm * hpad[_ck(c, 7), :] * a - (lm * ig * xc) * (a * a) / mult
            dpr = dl * sp8 * r * (1.0 - r)
            dpi = t * xc * ig * (1.0 - ig)
            s_sp = s_sp + _half_sum(dl * r)
            s_br = s_br + _half_sum(dpr)
            s_bi = s_bi + _half_sum(dpi)
            dprb_s[_ck(c), :] = dpr.astype(BF16)
            dpib_s[_ck(c), :] = dpi.astype(BF16)
        _accumulate(first, dlam_o, _rowsum(s_sp) * (LRU_C * jax.nn.sigmoid(-lam)))
        _accumulate(first, dbr_o, _rowsum(s_br))
        _accumulate(first, dbi_o, _rowsum(s_bi))
        for h in range(NG):
            cs = slice(h * GB, (h + 1) * GB)
            back_s[:, cs] = _dot_nt(dprb_s[:, cs], wr_r[h].astype(BF16)) + _dot_nt(dpib_s[:, cs], wi_r[h].astype(BF16))
            dwr_o[h] += _dot_tn(xcb_s[:, cs], dprb_s[:, cs])
            dwi_o[h] += _dot_tn(xcb_s[:, cs], dpib_s[:, cs])
        s_cb = jnp.zeros((8, D), F32)
        s_cw = [jnp.zeros((8, D), F32)] * 4
        for c in range(nck):
            dxc = dpad[_ck(c), :] + back_s[_ck(c), :]
            dpad[_ck(c), :] = dxc
            s_cb = s_cb + _half_sum(dxc)
            s_cw = [s_cw[k] + _half_sum(xpad[_ck(c, 5 + k), :] * dxc) for k in range(4)]
        _accumulate(first, dcb_o, _rowsum(s_cb))
        for k in range(4):
            dcw_o[k:k + 1, :] += _rowsum(s_cw[k])
        for c in range(nck):
            dxb = ((dpad[_ck(c, 3), :] * cw[0:1] + dpad[_ck(c, 2), :] * cw[1:2]) + dpad[_ck(c, 1), :] * cw[2:3]) + dpad[_ck(c), :] * cw[3:4]
            dz_o[0, _ck(c), :] = dxb.astype(BF16)
        apad[pl.ds(T, 8), :] = apad[pl.ds(0, 8), :]
        dpad[pl.ds(T, 8), :] = dpad[pl.ds(0, 8), :]

    rev = lambda k: pl.BlockSpec((T, D), lambda i: (nblk - 1 - i, k))
    prev16 = pl.BlockSpec((16, D), lambda i: (jnp.maximum((nblk - 1 - i) * (T // 16) - 1, 0), 3))
    prev8 = pl.BlockSpec((8, D), lambda i: (jnp.maximum((nblk - 1 - i) * (T // 8) - 1, 0), 0))
    vec, mat = _full((1, D)), _full((NG, GB, GB))
    return _pcall(
        body, name="branch_b_bwd", grid=(nblk,),
        in_specs=[rev(3), prev16, rev(4), rev(0), prev8, rev(0), _full((8, D)), vec, mat, vec, mat, vec, vec, HBM],
        out_specs=[pl.BlockSpec((2, T, D), lambda i: (3, nblk - 1 - i, 0)), _full((8, D)), vec, mat, vec, mat, vec, vec],
        out_shape=[SDS((10, S, D), BF16), SDS((8, D), F32), SDS((1, D), F32), SDS((NG, GB, GB), F32), SDS((1, D), F32),
                   SDS((NG, GB, GB), F32), SDS((1, D), F32), SDS((1, D), F32)],
        scratch=[pltpu.VMEM((T + 8, D), F32)] * 4 + [pltpu.VMEM((T, D), F32)] * 10 + [pltpu.VMEM((T, D), BF16)] * 3
        + [pltpu.VMEM((8, D), F32)],
        aliases={13: 0}, vmem_mb=56,
    )(z, z, z, hs, hs, dyb, cw, cb, wr, br, wi, bi, lam, dz)


def _branch_c_bwd(z, kv, dyc, dz):
    S = z.shape[0]
    T = min(512, S)

    def body(zq, zg, kv_r, dy_r, dz_in, dz_o, dkv_o):
        @pl.when(pl.program_id(0) == 0)
        def _():
            dkv_o[...] = jnp.zeros((MEM, 2 * D), F32)

        gc = zg[...].astype(F32)
        sil, dsil = _silu_parts(gc)
        dy = dy_r[...].astype(F32)
        datt = dy * sil
        dgc_pre = dy * dsil
        scale = HD ** -0.5
        for h in range(NH):
            cs = slice(h * HD, (h + 1) * HD)
            vs = slice(D + h * HD, D + (h + 1) * HD)
            qh = zq[:, cs]
            p = _softmax_rows(_dot_nt(qh, kv_r[:, cs]) * scale)
            pb = p.astype(BF16)
            att = _dot(pb, kv_r[:, vs])
            dz_o[1, :, cs] = (dgc_pre[:, cs] * att).astype(BF16)
            dab = datt[:, cs].astype(BF16)
            dp = _dot_nt(dab, kv_r[:, vs])
            ds = (p * (dp - jnp.sum(p * dp, axis=-1, keepdims=True)) * scale).astype(BF16)
            dz_o[0, :, cs] = _dot(ds, kv_r[:, cs]).astype(BF16)
            dkv_o[:, cs] += _dot_tn(ds, qh)
            dkv_o[:, vs] += _dot_tn(pb, dab)

    zs = lambda k: pl.BlockSpec((T, D), lambda i: (i, k))
    return _pcall(
        body, name="branch_c_bwd", grid=(S // T,),
        in_specs=[zs(5), zs(6), _full((MEM, 2 * D)), pl.BlockSpec((T, D), lambda i: (i, 0)), HBM],
        out_specs=[pl.BlockSpec((2, T, D), lambda i: (4, i, 0)), _full((MEM, 2 * D))],
        out_shape=[SDS((10, S, D), BF16), SDS((MEM, 2 * D), F32)], aliases={4: 0},
    )(z, z, kv, dyc, dz)


def _mm_dh(dz, w, x, dxn, g):
    S = x.shape[0]
    tm = min(1024, S)

    def body(dz_r, w_r, x_r, dxn_r, g_r, dx_o, dg_o, acc):
        i, k = pl.program_id(0), pl.program_id(1)
        _accumulate(k == 0, acc, _dot_nt(dz_r[0], w_r[...]))

        @pl.when(k == 9)
        def _():
            dh = acc[...]
            xv = x_r[...]
            r1 = _rms_scale(xv)
            wv = dh * g_r[...]
            dx_o[...] = dxn_r[...] + r1 * wv - xv * (r1 * r1 * r1) * jnp.mean(wv * xv, axis=-1, keepdims=True)
            _accumulate(i == 0, dg_o, _rowsum(dh * xv * r1))

    row = pl.BlockSpec((tm, D), lambda i, k: (i, 0))
    return _pcall(
        body, name="mm_dh", grid=(S // tm, 10),
        in_specs=[pl.BlockSpec((1, tm, D), lambda i, k: (k, i, 0)), pl.BlockSpec((D, D), lambda i, k: (0, _dz_col(k))),
                  row, row, _full((1, D))],
        out_specs=[row, _full((1, D))], out_shape=[SDS((S, D), F32), SDS((1, D), F32)],
        scratch=[pltpu.VMEM((tm, D), F32)],
    )(dz, w, x, dxn, g)


def _mm_dwin(h, dz):
    S = h.shape[0]
    tk = min(1024, S)
    nk = S // tk

    def body(h_r, dz_r, o_r, acc):
        k = pl.program_id(1)
        _accumulate(k == 0, acc, _dot_tn(h_r[...], dz_r[0]))

        @pl.when(k == nk - 1)
        def _():
            o_r[...] = acc[...].astype(BF16)

    return _pcall(
        body, name="mm_dwin", grid=(10, nk),
        in_specs=[pl.BlockSpec((tk, D), lambda n, k: (k, 0)), pl.BlockSpec((1, tk, D), lambda n, k: (n, k, 0))],
        out_specs=pl.BlockSpec((D, D), lambda n, k: (0, _dz_col(n))), out_shape=SDS((D, NIN), BF16),
        scratch=[pltpu.VMEM((D, D), F32)],
    )(h, dz)


def _mm_tn(a, b):
    S = a.shape[0]
    tk = min(1024, S)
    nk = S // tk

    def body(a_r, b_r, o_r, acc):
        k = pl.program_id(0)
        _accumulate(k == 0, acc, _dot_tn(a_r[...], b_r[...]))

        @pl.when(k == nk - 1)
        def _():
            o_r[...] = acc[...].astype(BF16)

    blk = pl.BlockSpec((tk, D), lambda k: (k, 0))
    return _pcall(body, name="mm_tn", grid=(nk,), in_specs=[blk, blk], out_specs=_full((D, D)),
                  out_shape=SDS((D, D), BF16), scratch=[pltpu.VMEM((D, D), F32)])(a, b)


def _mem_bwd(mem, g, wkv, dkv, dg_acc):
    def body(m_ref, g_ref, w_ref, dkv_ref, acc_ref, dw_ref, dg_ref):
        m = m_ref[...]
        mr = m * _rms_scale(m)
        mn = (mr * g_ref[...]).astype(BF16)
        dkb = dkv_ref[...].astype(BF16)
        dw_ref[...] = _dot_tn(mn, dkb).astype(BF16)
        dg_ref[...] = acc_ref[...] + _rowsum(_dot_nt(dkb, w_ref[...]) * mr)

    return _pcall(body, name="mem_bwd", in_specs=[VMEM] * 5, out_specs=[VMEM] * 2,
                  out_shape=[SDS((D, 2 * D), BF16), SDS((1, D), F32)], vmem_mb=48)(mem, g, wkv, dkv, dg_acc)


def _adamw_math(w, g, m, v):
    m2 = ADAM_B1 * m + (1.0 - ADAM_B1) * g
    v2 = ADAM_B2 * v + (1.0 - ADAM_B2) * (g * g)
    mh = m2 / (1.0 - ADAM_B1 ** ADAM_STEP)
    vh = v2 / (1.0 - ADAM_B2 ** ADAM_STEP)
    return -ADAM_LR * (mh / (jnp.sqrt(vh) + ADAM_EPS) + ADAM_WD * w), m2, v2


def _adamw_layer(l, w, m, v, g, prev, which=None, rows=256):
    L, R, C = w.shape

    def body(w_r, m_r, v_r, g_r, *rest):
        g_o, d_o, m_o, v_o = rest[-4:]
        g = g_r[...]
        d, m2, v2 = _adamw_math(w_r[...], g, m_r[...], v_r[...])
        g_o[...] = g
        d_o[...] = d
        m_o[...] = m2
        v_o[...] = v2

    st = pl.BlockSpec((None, rows, C), lambda i: (l, i, 0))
    gs = pl.BlockSpec((rows, C), lambda i: (i, 0)) if which is None else pl.BlockSpec((None, rows, C), lambda i: (which, i, 0))
    carried = list(prev) if prev is not None else []
    return _pcall(body, name="adamw_layer", grid=(R // rows,), in_specs=[st] * 3 + [gs] + [HBM] * len(carried),
                  out_specs=[st] * 4, out_shape=[SDS(w.shape, F32)] * 4, vmem_mb=56,
                  aliases={4 + k: k for k in range(len(carried))} or None)(w, m, v, g, *carried)


def _adamw_flat(w, m, v, g, rows):
    R, C = w.shape

    def body(w_r, m_r, v_r, g_r, d_o, m_o, v_o):
        d, m2, v2 = _adamw_math(w_r[...], g_r[...], m_r[...], v_r[...])
        d_o[...] = d
        m_o[...] = m2
        v_o[...] = v2

    blk = pl.BlockSpec((rows, C), lambda i: (i, 0))
    return _pcall(body, name="adamw_flat", grid=(R // rows,), in_specs=[blk] * 4, out_specs=[blk] * 3,
                  out_shape=[SDS((R, C), F32)] * 3)(w, m, v, g)


_SMALL = ("mem_norm_g", "pre_norm_g", "post_norm_g", "gmlp_ln_g", "gmlp_ln_b", "gmlp_ws", "gmlp_bs", "conv_b", "lru_wr",
          "lru_br", "lru_wi", "lru_bi", "lru_lambda")


def _pack_small(parts, conv_w_part):
    rows = [parts[n].reshape(-1, 128) for n in _SMALL] + [conv_w_part.reshape(-1, 128)]
    used = sum(r.shape[0] for r in rows)
    rows.append(jnp.zeros((SMALL_ROWS - used, 128), F32))
    return jnp.concatenate(rows, axis=0)


def _unpack_small(pack, shapes):
    out, at = {}, 0
    for n in _SMALL:
        size = 1
        for s in shapes[n]:
            size *= s
        out[n] = pack[at:at + size // 128].reshape(shapes[n])
        at += size // 128
    return out, at


def kernel(x, mem, mem_norm_g, pre_norm_g, post_norm_g, w_in, gmlp_ln_g, gmlp_ln_b, gmlp_ws, gmlp_bs, conv_w, conv_b, lru_wr, lru_br, lru_wi, lru_bi, lru_lambda, w_kv, w_pa, w_pb, w_pc, w_out, loss_target, m_mem_norm_g, m_pre_norm_g, m_post_norm_g, m_w_in, m_gmlp_ln_g, m_gmlp_ln_b, m_gmlp_ws, m_gmlp_bs, m_conv_w, m_conv_b, m_lru_wr, m_lru_br, m_lru_wi, m_lru_bi, m_lru_lambda, m_w_kv, m_w_pa, m_w_pb, m_w_pc, m_w_out, v_mem_norm_g, v_pre_norm_g, v_post_norm_g, v_w_in, v_gmlp_ln_g, v_gmlp_ln_b, v_gmlp_ws, v_gmlp_bs, v_conv_w, v_conv_b, v_lru_wr, v_lru_br, v_lru_wi, v_lru_bi, v_lru_lambda, v_w_kv, v_w_pa, v_w_pb, v_w_pc, v_w_out):
    L = w_in.shape[0]
    S = x.shape[1]
    xs = [x[0]]
    mem2 = mem[0]
    mg = mem_norm_g.reshape(1, D)
    vec = lambda a, l: a[l].reshape(1, D)
    ci = lax.axis_index("c")
    jpos = 2 * lax.axis_index("x") + lax.axis_index("y")
    pos = jnp.reshape(jpos, (1,)).astype(jnp.int32)

    cw8 = jnp.pad(conv_w, ((0, 0), (0, 4), (0, 0)))
    placed = [_cast_place(l, pos, w_in, w_kv, w_pa, w_pb, w_pc, w_out, cw8) for l in range(L)]
    W = [None] * L
    started = _gather_start(0, placed[0])
    mid = _gather_mid(0, started, started[-1])
    started = _gather_start(1, placed[1], mid[-1])
    W[0] = _gather_end(0, mid, started[-1])

    saved = []
    for l in range(L):
        Win, Wkv, Wp, Cw = W[l]
        z, h = _mm_in(xs[l], vec(pre_norm_g, l), Win)
        bsb = jnp.broadcast_to(gmlp_bs[l][:, :, None], (NG, GB, GB))
        ya = _branch_a(z, vec(gmlp_ln_g, l), vec(gmlp_ln_b, l), gmlp_ws[l], bsb)
        yb, hs = _branch_b(z, Cw, vec(conv_b, l), lru_wr[l], vec(lru_br, l), lru_wi[l], vec(lru_bi, l), vec(lru_lambda, l))
        kv = _kv(mem2, mg, Wkv)
        yc = _branch_c(z, kv)
        if l + 1 < L:
            mid = _gather_mid(l + 1, started, yc)
            if l + 2 < L:
                started = _gather_start(l + 2, placed[l + 2], mid[-1])
        pa, pb, pc, mgd, o, xn = _merge_out(ya, yb, yc, z, Wp, xs[l], vec(post_norm_g, l))
        if l + 1 < L:
            W[l + 1] = _gather_end(l + 1, mid, xn)
        xs.append(xn)
        saved.append((z, h, ya, yb, yc, hs, kv, pa, pb, pc, mgd, o, bsb))

    loss11, dxn = _loss_head(xs[L], loss_target[0])
    loss = lax.psum(loss11[0, 0], ("x", "y", "c"))

    big = dict(w_in=(w_in, m_w_in, v_w_in), w_kv=(w_kv, m_w_kv, v_w_kv), w_pa=(w_pa, m_w_pa, v_w_pa),
               w_pb=(w_pb, m_w_pb, v_w_pb), w_pc=(w_pc, m_w_pc, v_w_pc), w_out=(w_out, m_w_out, v_w_out))
    out = {n: None for n in big}
    kin, nsh, nkv, rp = w_in.shape[1], w_in.shape[2], w_kv.shape[2], w_pa.shape[1]

    def finish_layer(l, a2a, after):
        thru = _a2a_wait(l, a2a, after)
        g6, (lin, lkv, lp) = thru[0:6], thru[6:9]
        own_in = lax.dynamic_slice(g6[0], (ci * (kin // 2), jpos * nsh), (kin // 2, nsh))
        own_kv = lax.dynamic_slice(g6[1], (ci * (kin // 2), jpos * nkv), (kin // 2, nkv))
        own_p = jnp.concatenate([lax.dynamic_slice(g6[2 + k], (jpos * rp + ci * (rp // 2), 0), (rp // 2, D)) for k in range(4)], axis=0)
        g_in, g_kv, g_p = _sum_share((lin, lkv, lp.reshape(NDEV - 1, 2 * rp, D)), (own_in, own_kv, own_p))
        g_p = g_p.reshape(4, rp, D)
        out["w_in"] = _adamw_layer(l, *big["w_in"], g_in, out["w_in"])
        out["w_kv"] = _adamw_layer(l, *big["w_kv"], g_kv, out["w_kv"])
        for k, n in enumerate(("w_pa", "w_pb", "w_pc", "w_out")):
            out[n] = _adamw_layer(l, *big[n], g_p, out[n], which=k)

    small = {n: [None] * L for n in _SMALL}
    dconv_w = [None] * L
    dg_mem = jnp.zeros((1, D), F32)
    pending = None
    for l in reversed(range(L)):
        Win, Wkv, Wp, Cw = W[l]
        z, h, ya, yb, yc, hs, kv, pa, pb, pc, mgd, o, bsb = saved[l]
        pg = vec(post_norm_g, l) if pending is None else vec(post_norm_g, l) + pending[1][-1][0, 0]
        do, dpa, dpb, dpc, dya, dyb, dyc, dz, dgpost = _out_bwd(dxn, o, pg, Wp, z, pa, pb, pc)
        dz, dws, dbs, dlg, dlb = _branch_a_bwd(z, dya, vec(gmlp_ln_g, l), vec(gmlp_ln_b, l), gmlp_ws[l], bsb, dz)
        dz, dcw, dcb, dwr, dbr, dwi, dbi, dlam = _branch_b_bwd(
            z, hs, dyb, Cw, vec(conv_b, l), lru_wr[l], vec(lru_br, l), lru_wi[l], vec(lru_bi, l), vec(lru_lambda, l), dz)
        dz, dkv = _branch_c_bwd(z, kv, dyc, dz)
        g_in = _mm_dwin(h, dz)
        g_kv, dg_mem = _mem_bwd(mem2, mg, Wkv, dkv, dg_mem)
        grads = (g_in, g_kv, _mm_tn(ya, dpa), _mm_tn(yb, dpb), _mm_tn(yc, dpc), _mm_tn(mgd, do))
        lands = (lax.empty((NDEV - 1, kin // 2, nsh), BF16), lax.empty((NDEV - 1, kin // 2, nkv), BF16),
                 lax.empty((NDEV - 1, 4, rp // 2, D), BF16))
        a2a = _a2a_start(l, grads, lands)
        dx, dgpre = _mm_dh(dz, Win, xs[l], dxn, vec(pre_norm_g, l) + a2a[-1][0, 0])
        if pending is not None:
            finish_layer(pending[0], pending[1], dx)
        pending = (l, a2a)
        for n, val in (("pre_norm_g", dgpre), ("post_norm_g", dgpost), ("gmlp_ln_g", dlg), ("gmlp_ln_b", dlb), ("gmlp_ws", dws),
                       ("gmlp_bs", dbs), ("conv_b", dcb), ("lru_wr", dwr), ("lru_br", dbr), ("lru_wi", dwi), ("lru_bi", dbi),
                       ("lru_lambda", dlam)):
            small[n][l] = val
        dconv_w[l] = dcw[0:4]
        dxn = dx
    grad_x = dxn.reshape(1, S, D)

    parts = {n: jnp.stack(small[n]) for n in _SMALL if n != "mem_norm_g"}
    parts["mem_norm_g"] = dg_mem
    gsum = _allreduce_small(_pack_small(parts, jnp.stack(dconv_w)))
    given = dict(mem_norm_g=(mem_norm_g, m_mem_norm_g, v_mem_norm_g), pre_norm_g=(pre_norm_g, m_pre_norm_g, v_pre_norm_g),
                 post_norm_g=(post_norm_g, m_post_norm_g, v_post_norm_g), gmlp_ln_g=(gmlp_ln_g, m_gmlp_ln_g, v_gmlp_ln_g),
                 gmlp_ln_b=(gmlp_ln_b, m_gmlp_ln_b, v_gmlp_ln_b), gmlp_ws=(gmlp_ws, m_gmlp_ws, v_gmlp_ws),
                 gmlp_bs=(gmlp_bs, m_gmlp_bs, v_gmlp_bs), conv_b=(conv_b, m_conv_b, v_conv_b), lru_wr=(lru_wr, m_lru_wr, v_lru_wr),
                 lru_br=(lru_br, m_lru_br, v_lru_br), lru_wi=(lru_wi, m_lru_wi, v_lru_wi), lru_bi=(lru_bi, m_lru_bi, v_lru_bi),
                 lru_lambda=(lru_lambda, m_lru_lambda, v_lru_lambda))
    shapes = {n: given[n][0].shape for n in _SMALL}
    zero_cw = jnp.zeros((L, 4, D), F32)
    packs = [_pack_small({n: given[n][k] for n in _SMALL}, zero_cw) for k in range(3)]
    dsm, msm, vsm = _adamw_flat(packs[0], packs[1], packs[2], gsum, 2560)
    g_small, at = _unpack_small(gsum, shapes)
    d_small, _ = _unpack_small(dsm, shapes)
    m_small, _ = _unpack_small(msm, shapes)
    v_small, _ = _unpack_small(vsm, shapes)
    for n in _SMALL:
        out[n] = (g_small[n], d_small[n], m_small[n], v_small[n])
    g_cw = lax.dynamic_slice_in_dim(gsum[at:at + L * 4 * D // 128].reshape(L * 4, D), jpos * (D // 4), D // 4, axis=1)
    d_cw, m_cw, v_cw = _adamw_flat(conv_w.reshape(L * 4, D // 4), m_conv_w.reshape(L * 4, D // 4),
                                   v_conv_w.reshape(L * 4, D // 4), g_cw, L * 4)
    out["conv_w"] = tuple(a.reshape(L, 4, D // 4) for a in (g_cw, d_cw, m_cw, v_cw))

    finish_layer(pending[0], pending[1], dsm)

    order = ("mem_norm_g", "pre_norm_g", "post_norm_g", "w_in", "gmlp_ln_g", "gmlp_ln_b", "gmlp_ws", "gmlp_bs", "conv_w", "conv_b",
             "lru_wr", "lru_br", "lru_wi", "lru_bi", "lru_lambda", "w_kv", "w_pa", "w_pb", "w_pc", "w_out")
    return (loss, grad_x, *[out[n][0] for n in order], *[out[n][1] for n in order], *[out[n][2] for n in order],
            *[out[n][3] for n in order])
```

```python
import functools

import jax
import jax.numpy as jnp
from jax import lax
from jax.experimental import pallas as pl
from jax.experimental.pallas import tpu as pltpu

F32 = jnp.float32
BF16 = jnp.bfloat16
SDS = jax.ShapeDtypeStruct
MESH = pl.DeviceIdType.MESH

D = 1024
NIN = 10 * D
MEM = 256
GB = 128
NG = 8
NH = 4
HD = D // NH
EPS = 1e-6
LRU_C = 8.0
ADAM_LR, ADAM_B1, ADAM_B2, ADAM_EPS, ADAM_WD, ADAM_STEP = 0.001, 0.9, 0.999, 1e-08, 0.01, 10
NDEV = 8
SMALL_ROWS = 12800

_CALL_KW = {}
HBM = pl.BlockSpec(memory_space=pltpu.HBM)
VMEM = pl.BlockSpec(memory_space=pltpu.VMEM)
SEM = pl.BlockSpec(memory_space=pltpu.SEMAPHORE)
ANY = pl.BlockSpec(memory_space=pl.ANY)
TOKEN = SDS((8, 128), F32)


def _pcall(body, *, name, in_specs, out_specs, out_shape, grid=None, scratch=(), vmem_mb=48, aliases=None, effect=False,
           prefetch=0):
    kw = dict(_CALL_KW)
    if aliases:
        kw["input_output_aliases"] = aliases
    params = dict(vmem_limit_bytes=vmem_mb << 20)
    if grid is not None:
        params["dimension_semantics"] = ("arbitrary",) * len(grid)
    if effect:
        params["has_side_effects"] = pltpu.SideEffectType.DATAFLOW_SIDE_EFFECTING
    if prefetch:
        kw["grid_spec"] = pltpu.PrefetchScalarGridSpec(num_scalar_prefetch=prefetch, grid=grid, in_specs=in_specs,
                                                       out_specs=out_specs, scratch_shapes=list(scratch))
    else:
        kw.update(in_specs=in_specs, out_specs=out_specs, scratch_shapes=list(scratch))
        if grid is not None:
            kw["grid"] = grid
    return pl.pallas_call(body, name=name, out_shape=out_shape, compiler_params=pltpu.CompilerParams(**params), **kw)


def _full(shape):
    nd = len(shape)
    return pl.BlockSpec(shape, lambda *_: (0,) * nd)


def _dot(a, b):
    return jnp.dot(a, b, preferred_element_type=F32)


def _dot_nt(a, b):
    return lax.dot_general(a, b, (((1,), (1,)), ((), ())), preferred_element_type=F32)


def _dot_tn(a, b):
    return lax.dot_general(a, b, (((0,), (0,)), ((), ())), preferred_element_type=F32)


def _rowsum(a):
    return jnp.sum(a, axis=0, keepdims=True)


def _sigmoid(x):
    return 0.5 * jnp.tanh(0.5 * x) + 0.5


def _silu_parts(g):
    s = _sigmoid(g)
    return g * s, s * (1.0 + g * (1.0 - s))


def _rms_scale(x):
    return lax.rsqrt(jnp.mean(x * x, axis=-1, keepdims=True) + EPS)


def _dz_col(k):
    return jnp.where(k < 3, k, jnp.where(k < 6, k + 4, k - 3))


def _coords():
    return lax.axis_index("x"), lax.axis_index("y"), lax.axis_index("c")


def _other_chips(x, y):
    return [(1 - x, y), (x, 1 - y), (1 - x, 1 - y)]


def _peer(x, y, c, mask):
    return (1 - x if mask & 4 else x, 1 - y if mask & 2 else y, 1 - c if mask & 1 else c)


def _remote(src, dst, ssem, rsem, k, to):
    return pltpu.make_async_remote_copy(src_ref=src, dst_ref=dst, send_sem=ssem.at[k], recv_sem=rsem.at[k], device_id=to,
                                        device_id_type=MESH)


def _w_half(a, ref, jj, cc):
    if a == 2:
        rp = ref.shape[1] // 4
        return ref.at[:, pl.ds(jj * rp + cc * (rp // 2), rp // 2), :]
    kin, nsh = ref.shape[0], ref.shape[1] // 4
    return ref.at[pl.ds(cc * (kin // 2), kin // 2), pl.ds(jj * nsh, nsh)]


def _cw_block(ref, jj):
    return ref.at[:, pl.ds(jj * (D // 4), D // 4)]


def _cast_place(l, pos, w_in, w_kv, w_pa, w_pb, w_pc, w_out, cw8):
    kin, nsh = w_in.shape[1], w_in.shape[2]
    nkv, rp = w_kv.shape[2], w_pa.shape[1]
    half = kin // 2

    def body(pos_r, win, wkv, pa, pb, pc, po, cw, Win, Wkv, Wp, Cw):
        Win[...] = win[...].astype(BF16)
        Wkv[...] = wkv[...].astype(BF16)

        @pl.when(pl.program_id(0) == 0)
        def _():
            for k, r in enumerate((pa, pb, pc, po)):
                Wp[k] = r[...].astype(BF16)
            Cw[...] = cw[...]

    proj = pl.BlockSpec((None, rp, D), lambda i, p: (l, 0, 0))
    return _pcall(
        body, name="cast_place", grid=(2,), prefetch=1,
        in_specs=[pl.BlockSpec((None, half, nsh), lambda i, p: (l, i, 0)), pl.BlockSpec((None, half, nkv), lambda i, p: (l, i, 0)),
                  proj, proj, proj, proj, pl.BlockSpec((None, 8, D // 4), lambda i, p: (l, 0, 0))],
        out_specs=[pl.BlockSpec((half, nsh), lambda i, p: (i, p[0])), pl.BlockSpec((half, nkv), lambda i, p: (i, p[0])),
                   pl.BlockSpec((4, rp, D), lambda i, p: (0, p[0], 0)), pl.BlockSpec((8, D // 4), lambda i, p: (0, p[0]))],
        out_shape=[SDS((kin, 4 * nsh), BF16), SDS((kin, 4 * nkv), BF16), SDS((4, 4 * rp, D), BF16), SDS((8, D), F32)],
    )(pos, w_in, w_kv, w_pa, w_pb, w_pc, w_out, cw8)


def _hbm_like(bufs):
    return [pltpu.HBM(b.shape, b.dtype) for b in bufs]


def _gather_start(l, bufs, after=None):
    extra = [] if after is None else [after]

    def body(win, wkv, wp, cw, *rest):
        ssem, rsem, token = rest[len(extra)], rest[len(extra) + 1], rest[-1]
        x, y, c = _coords()
        j = 2 * x + y
        refs = (win, wkv, wp)
        for k, chip in enumerate(_other_chips(x, y)):
            to = (chip[0], chip[1], c)
            for a in range(3):
                half = _w_half(a, refs[a], j, c)
                _remote(half, half, ssem, rsem, a * 3 + k, to).start()
            mine = _cw_block(cw, j)
            _remote(mine, mine, ssem, rsem, 9 + k, to).start()
        token[...] = jnp.zeros((8, 128), F32)

    return _pcall(
        body, name=f"gather_start_{l}", in_specs=[HBM] * 4 + [ANY] * len(extra), out_specs=[SEM, SEM] + [HBM] * 4 + [VMEM],
        out_shape=[pltpu.SemaphoreType.DMA((12,)), pltpu.SemaphoreType.DMA((12,))] + _hbm_like(bufs) + [TOKEN],
        aliases={0: 2, 1: 3, 2: 4, 3: 5}, effect=True,
    )(*[pltpu.with_memory_space_constraint(b, pltpu.HBM) for b in bufs], *extra)


def _gather_mid(l, started, after):
    ssem, rsem, b0, b1, b2, b3, _ = started

    def body(win, wkv, wp, cw, ssem, rsem, after_r, ssem2, rsem2, o0, o1, o2, o3, token):
        x, y, c = _coords()
        j = 2 * x + y
        me, sib = (x, y, c), (x, y, 1 - c)
        token[...] = jnp.zeros((8, 128), F32)
        refs = (win, wkv, wp)
        chips = _other_chips(x, y)
        for k, chip in enumerate(chips):
            jk = 2 * chip[0] + chip[1]
            for a in range(3):
                got = _w_half(a, refs[a], jk, c)
                _remote(got, got, ssem, rsem, a * 3 + k, me).wait_recv()
            got = _cw_block(cw, jk)
            _remote(got, got, ssem, rsem, 9 + k, me).wait_recv()
        for k in range(3):
            for a in range(3):
                half = _w_half(a, refs[a], j, c)
                _remote(half, half, ssem, rsem, a * 3 + k, me).wait_send()
            mine = _cw_block(cw, j)
            _remote(mine, mine, ssem, rsem, 9 + k, me).wait_send()
        for k, chip in enumerate(chips):
            jk = 2 * chip[0] + chip[1]
            for a in range(3):
                got = _w_half(a, refs[a], jk, c)
                _remote(got, got, ssem2, rsem2, a * 3 + k, sib).start()

    bufs = (b0, b1, b2, b3)
    return _pcall(
        body, name=f"gather_mid_{l}", in_specs=[HBM] * 4 + [SEM, SEM, ANY], out_specs=[SEM, SEM] + [HBM] * 4 + [VMEM],
        out_shape=[pltpu.SemaphoreType.DMA((9,)), pltpu.SemaphoreType.DMA((9,))] + _hbm_like(bufs) + [TOKEN],
        aliases={0: 2, 1: 3, 2: 4, 3: 5}, effect=True,
    )(b0, b1, b2, b3, ssem, rsem, after)


def _gather_end(l, mid, after):
    ssem2, rsem2, b0, b1, b2, b3, _ = mid

    def body(win, wkv, wp, cw, ssem2, rsem2, after_r, o0, o1, o2, o3):
        x, y, c = _coords()
        me = (x, y, c)
        refs = (win, wkv, wp)
        for k, chip in enumerate(_other_chips(x, y)):
            jk = 2 * chip[0] + chip[1]
            for a in range(3):
                got = _w_half(a, refs[a], jk, 1 - c)
                _remote(got, got, ssem2, rsem2, a * 3 + k, me).wait_recv()
                sent = _w_half(a, refs[a], jk, c)
                _remote(sent, sent, ssem2, rsem2, a * 3 + k, me).wait_send()

    bufs = (b0, b1, b2, b3)
    return _pcall(
        body, name=f"gather_end_{l}", in_specs=[HBM] * 4 + [SEM, SEM, ANY], out_specs=[HBM] * 4, out_shape=_hbm_like(bufs),
        aliases={0: 0, 1: 1, 2: 2, 3: 3}, effect=True,
    )(b0, b1, b2, b3, ssem2, rsem2, after)


def _g_piece(a, ref, jd, dc):
    if a >= 2:
        rp = ref.shape[0] // 4
        return ref.at[pl.ds(jd * rp + dc * (rp // 2), rp // 2), :]
    kin, nsh = ref.shape[0], ref.shape[1] // 4
    return ref.at[pl.ds(dc * (kin // 2), kin // 2), pl.ds(jd * nsh, nsh)]


def _land_slot(a, lands, s):
    if a < 2:
        return lands[a].at[s]
    return lands[2].at[s, a - 2]


def _a2a_start(l, grads, lands):
    def body(*refs):
        g, ld, ssem, rsem, token = refs[0:6], refs[6:9], refs[9], refs[10], refs[-1]
        x, y, c = _coords()
        for mask in range(1, NDEV):
            p = _peer(x, y, c, mask)
            for a in range(6):
                _remote(_g_piece(a, g[a], 2 * p[0] + p[1], p[2]), _land_slot(a, ld, mask - 1), ssem, rsem, a * 7 + mask - 1, p).start()
        token[...] = jnp.zeros((8, 128), F32)

    bufs = tuple(grads) + tuple(lands)
    return _pcall(
        body, name=f"a2a_start_{l}", in_specs=[HBM] * 9, out_specs=[SEM, SEM] + [HBM] * 9 + [VMEM],
        out_shape=[pltpu.SemaphoreType.DMA((42,)), pltpu.SemaphoreType.DMA((42,))] + _hbm_like(bufs) + [TOKEN],
        aliases={i: 2 + i for i in range(9)}, effect=True,
    )(*[pltpu.with_memory_space_constraint(b, pltpu.HBM) for b in bufs])


def _a2a_wait(l, started, after):
    ssem, rsem = started[0], started[1]
    bufs = tuple(started[2:11])

    def body(*refs):
        g, ld, ssem, rsem = refs[0:6], refs[6:9], refs[9], refs[10]
        x, y, c = _coords()
        me = (x, y, c)
        for mask in range(1, NDEV):
            for a in range(6):
                got = _land_slot(a, ld, mask - 1)
                _remote(got, got, ssem, rsem, a * 7 + mask - 1, me).wait_recv()
        for mask in range(1, NDEV):
            p = _peer(x, y, c, mask)
            for a in range(6):
                sent = _g_piece(a, g[a], 2 * p[0] + p[1], p[2])
                _remote(sent, sent, ssem, rsem, a * 7 + mask - 1, me).wait_send()

    return _pcall(
        body, name=f"a2a_wait_{l}", in_specs=[HBM] * 9 + [SEM, SEM, ANY], out_specs=[HBM] * 9, out_shape=_hbm_like(bufs),
        aliases={i: i for i in range(9)}, effect=True,
    )(*bufs, ssem, rsem, after)


def _sum_share(lands, owns):
    rows, n = 128, 4
    widths = [o.shape[1] for o in owns]

    def body(l0, w0, l1, w1, l2, w2, g0, g1, g2, b0, b1, b2, lsem, ssem, rsem):
        i = pl.program_id(0)
        x, y, c = _coords()
        sib = (x, y, 1 - c)
        ld, ow, gs, bufs = (l0, l1, l2), (w0, w1, w2), (g0, g1, g2), (b0, b1, b2)

        def dst(a, step):
            row = step * (2 * rows) + c * rows if a == 2 else c * (n * rows) + step * rows
            return gs[a].at[pl.ds(row, rows), :]

        def copies(a, step, sl):
            src = bufs[a].at[sl]
            lc = pltpu.make_async_copy(src, dst(a, step), lsem.at[a, sl])
            rc = pltpu.make_async_remote_copy(src_ref=src, dst_ref=dst(a, step), send_sem=ssem.at[a, sl], recv_sem=rsem.at[a],
                                              device_id=sib, device_id_type=MESH)
            return lc, rc

        def drain(a, step, sl):
            lc, rc = copies(a, step, sl)
            lc.wait()
            rc.wait_send()

        slot = i % 2

        @pl.when(i >= 2)
        def _():
            for a in range(3):
                drain(a, i - 2, slot)

        for a in range(3):
            acc = ow[a][...].astype(F32)
            for k in range(NDEV - 1):
                acc = acc + ld[a][k].astype(F32)
            bufs[a][slot] = acc
            lc, rc = copies(a, i, slot)
            lc.start()
            rc.start()

        @pl.when(i == n - 1)
        def _():
            for a in range(3):
                drain(a, n - 2, (n - 2) % 2)
                drain(a, n - 1, (n - 1) % 2)
                whole = gs[a].at[pl.ds(0, n * rows), :]
                pltpu.make_async_remote_copy(src_ref=whole, dst_ref=whole, send_sem=ssem.at[a, 0], recv_sem=rsem.at[a],
                                             device_id=(x, y, c), device_id_type=MESH).wait_recv()

    in_specs = []
    for w in widths:
        in_specs += [pl.BlockSpec((NDEV - 1, rows, w), lambda i: (0, i, 0)), pl.BlockSpec((rows, w), lambda i: (i, 0))]
    args = [t for pair in zip(lands, owns) for t in pair]
    return _pcall(
        body, name="sum_share", grid=(n,), in_specs=in_specs, out_specs=[HBM] * 3,
        out_shape=[SDS((2 * n * rows, w), F32) for w in widths],
        scratch=[pltpu.VMEM((2, rows, w), F32) for w in widths]
        + [pltpu.SemaphoreType.DMA((3, 2)), pltpu.SemaphoreType.DMA((3, 2)), pltpu.SemaphoreType.DMA((3,))],
    )(*args)


def _allreduce_small(pack, after):
    R = pack.shape[0]
    r8 = R // NDEV

    def body(p_ref, after_r, full, land, red, ssem, rsem, ssem2, rsem2, lsem):
        x, y, c = _coords()
        me = 4 * x + 2 * y + c

        def idx(p):
            return 4 * p[0] + 2 * p[1] + p[2]

        own = pltpu.make_async_copy(p_ref.at[pl.ds(me * r8, r8), :], land.at[me], lsem.at[0])
        own.start()
        sent = []
        for mask in range(1, NDEV):
            p = _peer(x, y, c, mask)
            cp = _remote(p_ref.at[pl.ds(idx(p) * r8, r8), :], land.at[me], ssem, rsem, mask - 1, p)
            cp.start()
            sent.append(cp)
        for mask in range(1, NDEV):
            got = land.at[idx(_peer(x, y, c, mask))]
            _remote(got, got, ssem, rsem, mask - 1, (x, y, c)).wait_recv()
        own.wait()
        for cp in sent:
            cp.wait_send()
        acc = land[0]
        for k in range(1, NDEV):
            acc = acc + land[k]
        red[...] = acc
        mine = full.at[pl.ds(me * r8, r8), :]
        own2 = pltpu.make_async_copy(red, mine, lsem.at[1])
        own2.start()
        sent2 = []
        for mask in range(1, NDEV):
            cp = _remote(red, mine, ssem2, rsem2, mask - 1, _peer(x, y, c, mask))
            cp.start()
            sent2.append(cp)
        for mask in range(1, NDEV):
            got = full.at[pl.ds(idx(_peer(x, y, c, mask)) * r8, r8), :]
            _remote(got, got, ssem2, rsem2, mask - 1, (x, y, c)).wait_recv()
        own2.wait()
        for cp in sent2:
            cp.wait_send()

    return _pcall(
        body, name="allreduce_small", in_specs=[HBM, ANY], out_specs=HBM, out_shape=SDS((R, 128), F32),
        scratch=[pltpu.VMEM((NDEV, r8, 128), F32), pltpu.VMEM((r8, 128), F32)]
        + [pltpu.SemaphoreType.DMA((NDEV - 1,))] * 4 + [pltpu.SemaphoreType.DMA((2,))], vmem_mb=32,
    )(pack, after)


def _mm_in(x, g, w):
    S = x.shape[0]
    tm, tn = min(1024, S), 1280

    def body(x_ref, g_ref, w_ref, z_ref, h_ref, hs):
        @pl.when(pl.program_id(1) == 0)
        def _():
            xv = x_ref[...]
            hb = (xv * _rms_scale(xv) * g_ref[...]).astype(BF16)
            hs[...] = hb
            h_ref[...] = hb

        z_ref[...] = _dot(hs[...], w_ref[...]).astype(BF16)

    return _pcall(
        body, name="mm_in", grid=(S // tm, NIN // tn),
        in_specs=[pl.BlockSpec((tm, D), lambda i, j: (i, 0)), _full((1, D)), pl.BlockSpec((D, tn), lambda i, j: (0, j))],
        out_specs=[pl.BlockSpec((tm, tn), lambda i, j: (i, j)), pl.BlockSpec((tm, D), lambda i, j: (i, 0))],
        out_shape=[SDS((S, NIN), BF16), SDS((S, D), BF16)], scratch=[pltpu.VMEM((tm, D), BF16)],
    )(x, g, w)


def _chunk_mask():
    ri = lax.broadcasted_iota(jnp.int32, (GB, GB), 0)
    ci = lax.broadcasted_iota(jnp.int32, (GB, GB), 1)
    return (ri >= 64) | (ci < 64)


def _layernorm_parts(v):
    mu = jnp.mean(v, axis=-1, keepdims=True)
    d = v - mu
    rs = lax.rsqrt(jnp.mean(d * d, axis=-1, keepdims=True) + EPS)
    return d * rs, rs


def _branch_a(z, lg, lb, ws, bsb):
    S = z.shape[0]
    T = min(512, S)

    def body(zu, zv, zg, lg_r, lb_r, ws_r, bs_r, ya):
        vhat, _ = _layernorm_parts(zv[...].astype(F32))
        vnb = (vhat * lg_r[...] + lb_r[...]).astype(BF16)
        sil, _ = _silu_parts(zg[...].astype(F32))
        t = zu[...].astype(F32) * sil
        mask = _chunk_mask()
        for g in range(NG):
            wg = jnp.where(mask, ws_r[g], 0.0).astype(BF16)
            cs = slice(g * GB, (g + 1) * GB)
            for n in range(T // GB):
                rs = slice(n * GB, (n + 1) * GB)
                sv = _dot(wg, vnb[rs, cs]) + bs_r[g]
                ya[rs, cs] = (t[rs, cs] * sv).astype(BF16)

    zs = lambda k: pl.BlockSpec((T, D), lambda i: (i, k))
    return _pcall(
        body, name="branch_a", grid=(S // T,),
        in_specs=[zs(0), zs(1), zs(2), _full((1, D)), _full((1, D)), _full((NG, GB, GB)), _full((NG, GB, GB))],
        out_specs=pl.BlockSpec((T, D), lambda i: (i, 0)), out_shape=SDS((S, D), BF16),
    )(z, z, z, lg, lb, ws, bsb)


def _softplus_neg(lam):
    e = jnp.exp(-jnp.abs(lam))
    l1p = jnp.where(e < 1e-2, e * (1.0 - e * (0.5 - e * (1.0 / 3.0))), jnp.log(1.0 + e))
    return jnp.maximum(-lam, 0.0) + l1p


CH = 16


def _ck(c, off=0):
    return pl.ds(c * CH + off, CH)


def _half_sum(v):
    return v[0:8, :] + v[8:16, :]


def _lru_conv(xpad, c, cw, cb):
    xk = [xpad[_ck(c, 5 + k), :] for k in range(4)]
    return xk, cb + (((xk[0] * cw[0:1] + xk[1] * cw[1:2]) + xk[2] * cw[2:3]) + xk[3] * cw[3:4])


def _lru_gate_matmuls(xcb_s, wr_r, wi_r, pr_s, pi_s):
    for h in range(NG):
        cs = slice(h * GB, (h + 1) * GB)
        pr_s[:, cs] = _dot(xcb_s[:, cs], wr_r[h].astype(BF16))
        pi_s[:, cs] = _dot(xcb_s[:, cs], wi_r[h].astype(BF16))


def _lru_gates(pr, pi, br, bi, sp8):
    r = jax.nn.sigmoid(pr + br)
    ig = _sigmoid(pi + bi)
    la = sp8 * r
    a = jnp.exp(la)
    a2 = a * a
    mult = jnp.sqrt(-jnp.tanh(la) * (a2 + 1.0))
    return r, ig, a, a2, mult


def _tile_rows():
    return lax.broadcasted_iota(jnp.int32, (8, D), 0)


def _scan_forward(a_s, u_s, h_s, hcar, T):
    row = _tile_rows()

    def tile(i, hp):
        o = pl.multiple_of(i * 8, 8)
        A = a_s[pl.ds(o, 8), :]
        U = u_s[pl.ds(o, 8), :]
        for s in (1, 2, 4):
            m = row >= s
            U = jnp.where(m, U + A * pltpu.roll(U, s, 0), U)
            A = jnp.where(m, A * pltpu.roll(A, s, 0), A)
        H = U + A * hp
        h_s[pl.ds(o, 8), :] = H
        return jnp.broadcast_to(H[7:8, :], (8, D))

    hcar[...] = lax.fori_loop(0, T // 8, tile, hcar[...])


def _scan_reverse(b_s, d_s, l_s, lcar, T):
    row = _tile_rows()
    n = T // 8

    def tile(i, lp):
        o = pl.multiple_of((n - 1 - i) * 8, 8)
        B = b_s[pl.ds(o, 8), :]
        U = d_s[pl.ds(o, 8), :]
        for s in (1, 2, 4):
            m = row < 8 - s
            U = jnp.where(m, U + B * pltpu.roll(U, 8 - s, 0), U)
            B = jnp.where(m, B * pltpu.roll(B, 8 - s, 0), B)
        Lm = U + B * lp
        l_s[pl.ds(o, 8), :] = Lm
        return jnp.broadcast_to(Lm[0:1, :], (8, D))

    lcar[...] = lax.fori_loop(0, n, tile, lcar[...])


def _branch_b(z, cw, cb, wr, br, wi, bi, lam):
    S = z.shape[0]
    T = min(256, S)

    def body(zxb, zgb, cw_r, cb_r, wr_r, br_r, wi_r, bi_r, lam_r, yb, hs_o, xpad, a_s, u_s, hcar):
        @pl.when(pl.program_id(0) == 0)
        def _():
            xpad[pl.ds(0, 8), :] = jnp.zeros((8, D), F32)
            hcar[...] = jnp.zeros((8, D), F32)

        cw = cw_r[...]
        xpad[pl.ds(8, T), :] = zxb[...].astype(F32)
        xk = [xpad[pl.ds(5 + k, T), :] for k in range(4)]
        xc = cb_r[...] + (((xk[0] * cw[0:1] + xk[1] * cw[1:2]) + xk[2] * cw[2:3]) + xk[3] * cw[3:4])
        xcb = xc.astype(BF16)
        pr, pi = [], []
        for h in range(NG):
            cs = slice(h * GB, (h + 1) * GB)
            pr.append(_dot(xcb[:, cs], wr_r[h].astype(BF16)))
            pi.append(_dot(xcb[:, cs], wi_r[h].astype(BF16)))
        _, ig, a, _, mult = _lru_gates(jnp.concatenate(pr, axis=1), jnp.concatenate(pi, axis=1), br_r[...], bi_r[...],
                                       -LRU_C * _softplus_neg(lam_r[...]))
        a_s[...] = a
        u_s[...] = mult * (ig * xc)
        _scan_forward(a_s, u_s, hs_o, hcar, T)
        xpad[pl.ds(0, 8), :] = xpad[pl.ds(T, 8), :]
        sil, _ = _silu_parts(zgb[...].astype(F32))
        yb[...] = (hs_o[...] * sil).astype(BF16)

    zs = lambda k: pl.BlockSpec((T, D), lambda i: (i, k))
    row = pl.BlockSpec((T, D), lambda i: (i, 0))
    return _pcall(
        body, name="branch_b", grid=(S // T,),
        in_specs=[zs(3), zs(4), _full((8, D)), _full((1, D)), _full((NG, GB, GB)), _full((1, D)), _full((NG, GB, GB)),
                  _full((1, D)), _full((1, D))],
        out_specs=[row, row], out_shape=[SDS((S, D), BF16), SDS((S, D), F32)],
        scratch=[pltpu.VMEM((T + 8, D), F32), pltpu.VMEM((T, D), F32), pltpu.VMEM((T, D), F32), pltpu.VMEM((8, D), F32)],
    )(z, z, cw, cb, wr, br, wi, bi, lam)


def _kv(mem, g, wkv):
    def body(m_ref, g_ref, w_ref, kv_ref):
        m = m_ref[...]
        mn = (m * _rms_scale(m) * g_ref[...]).astype(BF16)
        kv_ref[...] = _dot(mn, w_ref[...]).astype(BF16)

    return _pcall(body, name="mem_kv", in_specs=[VMEM] * 3, out_specs=VMEM, out_shape=SDS((MEM, 2 * D), BF16),
                  vmem_mb=32)(mem, g, wkv)


def _softmax_rows(s):
    e = jnp.exp(s - jnp.max(s, axis=-1, keepdims=True))
    return e / jnp.sum(e, axis=-1, keepdims=True)


def _branch_c(z, kv):
    S = z.shape[0]
    T = min(512, S)

    def body(zq, zg, kv_r, yc):
        sil, _ = _silu_parts(zg[...].astype(F32))
        for h in range(NH):
            cs = slice(h * HD, (h + 1) * HD)
            p = _softmax_rows(_dot_nt(zq[:, cs], kv_r[:, cs]) * (HD ** -0.5))
            att = _dot(p.astype(BF16), kv_r[:, D + h * HD:D + (h + 1) * HD])
            yc[:, cs] = (att * sil[:, cs]).astype(BF16)

    zs = lambda k: pl.BlockSpec((T, D), lambda i: (i, k))
    return _pcall(body, name="branch_c", grid=(S // T,), in_specs=[zs(5), zs(6), _full((MEM, 2 * D))],
                  out_specs=pl.BlockSpec((T, D), lambda i: (i, 0)), out_shape=SDS((S, D), BF16))(z, z, kv)


def _merge_out(ya, yb, yc, z, wp, x, pg):
    S = x.shape[0]
    T = min(256, S)

    def body(ya_r, yb_r, yc_r, m0, m1, m2, wp_r, x_r, pg_r, pa_o, pb_o, pc_o, mg_o, o_o, xn_o):
        merged = None
        for y_r, ml, p_o, k in ((ya_r, m0, pa_o, 0), (yb_r, m1, pb_o, 1), (yc_r, m2, pc_o, 2)):
            p = _dot(y_r[...], wp_r[k])
            p_o[...] = p.astype(BF16)
            t = _sigmoid(ml[...].astype(F32)) * p
            merged = t if merged is None else merged + t
        mb = merged.astype(BF16)
        mg_o[...] = mb
        o = _dot(mb, wp_r[3])
        o_o[...] = o.astype(BF16)
        xn_o[...] = x_r[...] + o * _rms_scale(o) * pg_r[...]

    row = pl.BlockSpec((T, D), lambda i: (i, 0))
    zs = lambda k: pl.BlockSpec((T, D), lambda i: (i, k))
    return _pcall(
        body, name="merge_out", grid=(S // T,),
        in_specs=[row, row, row, zs(7), zs(8), zs(9), _full((4, D, D)), row, _full((1, D))],
        out_specs=[row] * 6, out_shape=[SDS((S, D), BF16)] * 5 + [SDS((S, D), F32)], vmem_mb=56,
    )(ya, yb, yc, z, z, z, wp, x, pg)


def _loss_head(y, t):
    S = y.shape[0]
    T = min(512, S)

    def body(y_r, t_r, loss_o, dy_o):
        @pl.when(pl.program_id(0) == 0)
        def _():
            loss_o[...] = jnp.zeros((1, 1), F32)

        e = y_r[...] - t_r[...]
        dy_o[...] = e * (1.0 / D)
        loss_o[...] += 0.5 * _rowsum(jnp.sum(e * e, axis=1, keepdims=True) * (1.0 / D))

    row = pl.BlockSpec((T, D), lambda i: (i, 0))
    return _pcall(body, name="loss_head", grid=(S // T,), in_specs=[row, row], out_specs=[_full((1, 1)), row],
                  out_shape=[SDS((1, 1), F32), SDS((S, D), F32)])(y, t)


def _accumulate(first, ref, val):
    @pl.when(first)
    def _():
        ref[...] = val

    @pl.when(jnp.logical_not(first))
    def _():
        ref[...] += val


def _out_bwd(dxn, o, pg, wp, z, pa, pb, pc):
    S = dxn.shape[0]
    T = min(256, S)

    def body(dy_r, o_r, pg_r, wp_r, m0, m1, m2, pa_r, pb_r, pc_r, do_o, dpa_o, dpb_o, dpc_o, dya_o, dyb_o, dyc_o, dz_o, dg_o):
        dy = dy_r[...]
        o = o_r[...].astype(F32)
        r2 = _rms_scale(o)
        w = dy * pg_r[...]
        do = r2 * w - o * (r2 * r2 * r2) * jnp.mean(w * o, axis=-1, keepdims=True)
        _accumulate(pl.program_id(0) == 0, dg_o, _rowsum(dy * o * r2))
        dob = do.astype(BF16)
        do_o[...] = dob
        dm = _dot_nt(dob, wp_r[3])
        for k, (ml, p_r, dp_o, dy_o) in enumerate(((m0, pa_r, dpa_o, dya_o), (m1, pb_r, dpb_o, dyb_o), (m2, pc_r, dpc_o, dyc_o))):
            gk = _sigmoid(ml[...].astype(F32))
            dz_o[k] = (dm * p_r[...].astype(F32) * gk * (1.0 - gk)).astype(BF16)
            dpk = (gk * dm).astype(BF16)
            dp_o[...] = dpk
            dy_o[...] = _dot_nt(dpk, wp_r[k]).astype(BF16)

    row = pl.BlockSpec((T, D), lambda i: (i, 0))
    zs = lambda k: pl.BlockSpec((T, D), lambda i: (i, k))
    return _pcall(
        body, name="out_bwd", grid=(S // T,),
        in_specs=[row, row, _full((1, D)), _full((4, D, D)), zs(7), zs(8), zs(9), row, row, row],
        out_specs=[row] * 7 + [pl.BlockSpec((3, T, D), lambda i: (1, i, 0)), _full((1, D))],
        out_shape=[SDS((S, D), BF16)] * 7 + [SDS((10, S, D), BF16), SDS((1, D), F32)], vmem_mb=56,
    )(dxn, o, pg, wp, z, z, z, pa, pb, pc)


def _branch_a_bwd(z, dya, lg, lb, ws, bsb, dz):
    S = z.shape[0]
    T = min(512, S)
    nblk = S // T

    def body(zu, zv, zg, dy_r, lg_r, lb_r, ws_r, bs_r, dz_in, dz_o, dws_o, dbs_o, dlg_o, dlb_o, dvn_s, bacc):
        i = pl.program_id(0)

        @pl.when(i == 0)
        def _():
            dws_o[...] = jnp.zeros((NG, GB, GB), F32)
            bacc[...] = jnp.zeros((NG, GB, GB), F32)

        vhat, rs = _layernorm_parts(zv[...].astype(F32))
        vnb = (vhat * lg_r[...] + lb_r[...]).astype(BF16)
        ga = zg[...].astype(F32)
        sil, dsil = _silu_parts(ga)
        u = zu[...].astype(F32)
        dy = dy_r[...].astype(F32)
        t = dy * sil
        dsv_all = t * u
        dga_pre = dy * u * dsil
        mask = _chunk_mask()
        for g in range(NG):
            wf = jnp.where(mask, ws_r[g], 0.0)
            wg = wf.astype(BF16)
            wgt = wf.T.astype(BF16)
            cs = slice(g * GB, (g + 1) * GB)
            dw = jnp.zeros((GB, GB), F32)
            db = jnp.zeros((GB, GB), F32)
            for n in range(T // GB):
                rsl = slice(n * GB, (n + 1) * GB)
                vb = vnb[rsl, cs]
                sv = _dot(wg, vb) + bs_r[g]
                dz_o[0, rsl, cs] = (t[rsl, cs] * sv).astype(BF16)
                dz_o[2, rsl, cs] = (dga_pre[rsl, cs] * sv).astype(BF16)
                dsv = dsv_all[rsl, cs]
                dsb = dsv.astype(BF16)
                dvn_s[rsl, cs] = _dot(wgt, dsb)
                dw = dw + _dot_nt(dsb, vb)
                db = db + dsv
            dws_o[g] += jnp.where(mask, dw, 0.0)
            bacc[g] += db
        dvn = dvn_s[...]
        dvh = dvn * lg_r[...]
        dv = rs * (dvh - jnp.mean(dvh, axis=-1, keepdims=True) - vhat * jnp.mean(dvh * vhat, axis=-1, keepdims=True))
        dz_o[1] = dv.astype(BF16)
        _accumulate(i == 0, dlg_o, _rowsum(dvn * vhat))
        _accumulate(i == 0, dlb_o, _rowsum(dvn))

        @pl.when(i == nblk - 1)
        def _():
            for g in range(NG):
                dbs_o[g:g + 1, :] = _rowsum(bacc[g].T)

    zs = lambda k: pl.BlockSpec((T, D), lambda i: (i, k))
    return _pcall(
        body, name="branch_a_bwd", grid=(nblk,),
        in_specs=[zs(0), zs(1), zs(2), pl.BlockSpec((T, D), lambda i: (i, 0)), _full((1, D)), _full((1, D)),
                  _full((NG, GB, GB)), _full((NG, GB, GB)), HBM],
        out_specs=[pl.BlockSpec((3, T, D), lambda i: (0, i, 0)), _full((NG, GB, GB)), _full((NG, GB)), _full((1, D)),
                   _full((1, D))],
        out_shape=[SDS((10, S, D), BF16), SDS((NG, GB, GB), F32), SDS((NG, GB), F32), SDS((1, D), F32), SDS((1, D), F32)],
        scratch=[pltpu.VMEM((T, D), F32), pltpu.VMEM((NG, GB, GB), F32)], aliases={8: 0},
    )(z, z, z, dya, lg, lb, ws, bsb, dz)


def _branch_b_bwd(z, hs, dyb, cw, cb, wr, br, wi, bi, lam, dz):
    S = z.shape[0]
    T = min(256, S)
    nblk = S // T

    def body(zxb, zprev, zgb, hs_r, hprev_r, dy_r, cw_r, cb_r, wr_r, br_r, wi_r, bi_r, lam_r, dz_in,
             dz_o, dcw_o, dcb_o, dwr_o, dbr_o, dwi_o, dbi_o, dlam_o, xpad, hpad, apad, dpad, xc_s, pr_s, pi_s, r_s, ig_s, m_s,
             b_s, d_s, l_s, back_s, xcb_s, dprb_s, dpib_s, lcar):
        i = pl.program_id(0)
        blk = nblk - 1 - i
        first = i == 0

        @pl.when(first)
        def _():
            apad[pl.ds(T, 8), :] = jnp.zeros((8, D), F32)
            dpad[pl.ds(T, 8), :] = jnp.zeros((8, D), F32)
            lcar[...] = jnp.zeros((8, D), F32)
            dcw_o[...] = jnp.zeros((8, D), F32)
            dwr_o[...] = jnp.zeros((NG, GB, GB), F32)
            dwi_o[...] = jnp.zeros((NG, GB, GB), F32)

        keep = (blk > 0).astype(F32)
        nck = T // CH
        cw, cb, br, bi, lam = cw_r[...], cb_r[...], br_r[...], bi_r[...], lam_r[...]
        sp8 = -LRU_C * _softplus_neg(lam)
        xpad[pl.ds(0, 8), :] = zprev[...].astype(F32)[8:16, :] * keep
        hpad[pl.ds(0, 8), :] = hprev_r[...] * keep
        for c in range(nck):
            xpad[_ck(c, 8), :] = zxb[_ck(c), :].astype(F32)
            hpad[_ck(c, 8), :] = hs_r[_ck(c), :]
            _, xc = _lru_conv(xpad, c, cw, cb)
            xc_s[_ck(c), :] = xc
            xcb_s[_ck(c), :] = xc.astype(BF16)
        _lru_gate_matmuls(xcb_s, wr_r, wi_r, pr_s, pi_s)
        for c in range(nck):
            r, ig, a, _, mult = _lru_gates(pr_s[_ck(c), :], pi_s[_ck(c), :], br, bi, sp8)
            r_s[_ck(c), :] = r
            ig_s[_ck(c), :] = ig
            m_s[_ck(c), :] = mult
            apad[_ck(c), :] = a
            sil, dsil = _silu_parts(zgb[_ck(c), :].astype(F32))
            dy = dy_r[_ck(c), :].astype(F32)
            dz_o[1, _ck(c), :] = (dy * hs_r[_ck(c), :] * dsil).astype(BF16)
            d_s[_ck(c), :] = dy * sil
        for c in range(nck):
            b_s[_ck(c), :] = apad[_ck(c, 1), :]
        _scan_reverse(b_s, d_s, l_s, lcar, T)
        s_sp = s_br = s_bi = jnp.zeros((8, D), F32)
        for c in range(nck):
            lm, r, ig, mult, a, xc = l_s[_ck(c), :], r_s[_ck(c), :], ig_s[_ck(c), :], m_s[_ck(c), :], apad[_ck(c), :], xc_s[_ck(c), :]
            t = lm * mult
            dpad[_ck(c), :] = t * ig
            dl = lm * hpad[_ck(c, 7), :] * a - (lm * ig * xc) * (a * a) / mult
            dpr = dl * sp8 * r * (1.0 - r)
            dpi = t * xc * ig * (1.0 - ig)
            s_sp = s_sp + _half_sum(dl * r)
            s_br = s_br + _half_sum(dpr)
            s_bi = s_bi + _half_sum(dpi)
            dprb_s[_ck(c), :] = dpr.astype(BF16)
            dpib_s[_ck(c), :] = dpi.astype(BF16)
        _accumulate(first, dlam_o, _rowsum(s_sp) * (LRU_C * jax.nn.sigmoid(-lam)))
        _accumulate(first, dbr_o, _rowsum(s_br))
        _accumulate(first, dbi_o, _rowsum(s_bi))
        for h in range(NG):
            cs = slice(h * GB, (h + 1) * GB)
            back_s[:, cs] = _dot_nt(dprb_s[:, cs], wr_r[h].astype(BF16)) + _dot_nt(dpib_s[:, cs], wi_r[h].astype(BF16))
            dwr_o[h] += _dot_tn(xcb_s[:, cs], dprb_s[:, cs])
            dwi_o[h] += _dot_tn(xcb_s[:, cs], dpib_s[:, cs])
        s_cb = jnp.zeros((8, D), F32)
        s_cw = [jnp.zeros((8, D), F32)] * 4
        for c in range(nck):
            dxc = dpad[_ck(c), :] + back_s[_ck(c), :]
            dpad[_ck(c), :] = dxc
            s_cb = s_cb + _half_sum(dxc)
            s_cw = [s_cw[k] + _half_sum(xpad[_ck(c, 5 + k), :] * dxc) for k in range(4)]
        _accumulate(first, dcb_o, _rowsum(s_cb))
        for k in range(4):
            dcw_o[k:k + 1, :] += _rowsum(s_cw[k])
        for c in range(nck):
            dxb = ((dpad[_ck(c, 3), :] * cw[0:1] + dpad[_ck(c, 2), :] * cw[1:2]) + dpad[_ck(c, 1), :] * cw[2:3]) + dpad[_ck(c), :] * cw[3:4]
            dz_o[0, _ck(c), :] = dxb.astype(BF16)
        apad[pl.ds(T, 8), :] = apad[pl.ds(0, 8), :]
        dpad[pl.ds(T, 8), :] = dpad[pl.ds(0, 8), :]

    rev = lambda k: pl.BlockSpec((T, D), lambda i: (nblk - 1 - i, k))
    prev16 = pl.BlockSpec((16, D), lambda i: (jnp.maximum((nblk - 1 - i) * (T // 16) - 1, 0), 3))
    prev8 = pl.BlockSpec((8, D), lambda i: (jnp.maximum((nblk - 1 - i) * (T // 8) - 1, 0), 0))
    vec, mat = _full((1, D)), _full((NG, GB, GB))
    return _pcall(
        body, name="branch_b_bwd", grid=(nblk,),
        in_specs=[rev(3), prev16, rev(4), rev(0), prev8, rev(0), _full((8, D)), vec, mat, vec, mat, vec, vec, HBM],
        out_specs=[pl.BlockSpec((2, T, D), lambda i: (3, nblk - 1 - i, 0)), _full((8, D)), vec, mat, vec, mat, vec, vec],
        out_shape=[SDS((10, S, D), BF16), SDS((8, D), F32), SDS((1, D), F32), SDS((NG, GB, GB), F32), SDS((1, D), F32),
                   SDS((NG, GB, GB), F32), SDS((1, D), F32), SDS((1, D), F32)],
        scratch=[pltpu.VMEM((T + 8, D), F32)] * 4 + [pltpu.VMEM((T, D), F32)] * 10 + [pltpu.VMEM((T, D), BF16)] * 3
        + [pltpu.VMEM((8, D), F32)],
        aliases={13: 0}, vmem_mb=56,
    )(z, z, z, hs, hs, dyb, cw, cb, wr, br, wi, bi, lam, dz)


def _branch_c_bwd(z, kv, dyc, dz):
    S = z.shape[0]
    T = min(512, S)

    def body(zq, zg, kv_r, dy_r, dz_in, dz_o, dkv_o):
        @pl.when(pl.program_id(0) == 0)
        def _():
            dkv_o[...] = jnp.zeros((MEM, 2 * D), F32)

        gc = zg[...].astype(F32)
        sil, dsil = _silu_parts(gc)
        dy = dy_r[...].astype(F32)
        datt = dy * sil
        dgc_pre = dy * dsil
        scale = HD ** -0.5
        for h in range(NH):
            cs = slice(h * HD, (h + 1) * HD)
            vs = slice(D + h * HD, D + (h + 1) * HD)
            qh = zq[:, cs]
            p = _softmax_rows(_dot_nt(qh, kv_r[:, cs]) * scale)
            pb = p.astype(BF16)
            att = _dot(pb, kv_r[:, vs])
            dz_o[1, :, cs] = (dgc_pre[:, cs] * att).astype(BF16)
            dab = datt[:, cs].astype(BF16)
            dp = _dot_nt(dab, kv_r[:, vs])
            ds = (p * (dp - jnp.sum(p * dp, axis=-1, keepdims=True)) * scale).astype(BF16)
            dz_o[0, :, cs] = _dot(ds, kv_r[:, cs]).astype(BF16)
            dkv_o[:, cs] += _dot_tn(ds, qh)
            dkv_o[:, vs] += _dot_tn(pb, dab)

    zs = lambda k: pl.BlockSpec((T, D), lambda i: (i, k))
    return _pcall(
        body, name="branch_c_bwd", grid=(S // T,),
        in_specs=[zs(5), zs(6), _full((MEM, 2 * D)), pl.BlockSpec((T, D), lambda i: (i, 0)), HBM],
        out_specs=[pl.BlockSpec((2, T, D), lambda i: (4, i, 0)), _full((MEM, 2 * D))],
        out_shape=[SDS((10, S, D), BF16), SDS((MEM, 2 * D), F32)], aliases={4: 0},
    )(z, z, kv, dyc, dz)


def _mm_dh(dz, w, x, dxn, g):
    S = x.shape[0]
    tm = min(1024, S)

    def body(dz_r, w_r, x_r, dxn_r, g_r, dx_o, dg_o, acc):
        i, k = pl.program_id(0), pl.program_id(1)
        _accumulate(k == 0, acc, _dot_nt(dz_r[0], w_r[...]))

        @pl.when(k == 9)
        def _():
            dh = acc[...]
            xv = x_r[...]
            r1 = _rms_scale(xv)
            wv = dh * g_r[...]
            dx_o[...] = dxn_r[...] + r1 * wv - xv * (r1 * r1 * r1) * jnp.mean(wv * xv, axis=-1, keepdims=True)
            _accumulate(i == 0, dg_o, _rowsum(dh * xv * r1))

    row = pl.BlockSpec((tm, D), lambda i, k: (i, 0))
    return _pcall(
        body, name="mm_dh", grid=(S // tm, 10),
        in_specs=[pl.BlockSpec((1, tm, D), lambda i, k: (k, i, 0)), pl.BlockSpec((D, D), lambda i, k: (0, _dz_col(k))),
                  row, row, _full((1, D))],
        out_specs=[row, _full((1, D))], out_shape=[SDS((S, D), F32), SDS((1, D), F32)],
        scratch=[pltpu.VMEM((tm, D), F32)],
    )(dz, w, x, dxn, g)


def _mm_dwin(h, dz):
    S = h.shape[0]
    tk = min(1024, S)
    nk = S // tk

    def body(h_r, dz_r, o_r, acc):
        k = pl.program_id(1)
        _accumulate(k == 0, acc, _dot_tn(h_r[...], dz_r[0]))

        @pl.when(k == nk - 1)
        def _():
            o_r[...] = acc[...].astype(BF16)

    return _pcall(
        body, name="mm_dwin", grid=(10, nk),
        in_specs=[pl.BlockSpec((tk, D), lambda n, k: (k, 0)), pl.BlockSpec((1, tk, D), lambda n, k: (n, k, 0))],
        out_specs=pl.BlockSpec((D, D), lambda n, k: (0, _dz_col(n))), out_shape=SDS((D, NIN), BF16),
        scratch=[pltpu.VMEM((D, D), F32)],
    )(h, dz)


def _mm_tn(a, b):
    S = a.shape[0]
    tk = min(1024, S)
    nk = S // tk

    def body(a_r, b_r, o_r, acc):
        k = pl.program_id(0)
        _accumulate(k == 0, acc, _dot_tn(a_r[...], b_r[...]))

        @pl.when(k == nk - 1)
        def _():
            o_r[...] = acc[...].astype(BF16)

    blk = pl.BlockSpec((tk, D), lambda k: (k, 0))
    return _pcall(body, name="mm_tn", grid=(nk,), in_specs=[blk, blk], out_specs=_full((D, D)),
                  out_shape=SDS((D, D), BF16), scratch=[pltpu.VMEM((D, D), F32)])(a, b)


def _mem_bwd(mem, g, wkv, dkv, dg_acc):
    def body(m_ref, g_ref, w_ref, dkv_ref, acc_ref, dw_ref, dg_ref):
        m = m_ref[...]
        mr = m * _rms_scale(m)
        mn = (mr * g_ref[...]).astype(BF16)
        dkb = dkv_ref[...].astype(BF16)
        dw_ref[...] = _dot_tn(mn, dkb).astype(BF16)
        dg_ref[...] = acc_ref[...] + _rowsum(_dot_nt(dkb, w_ref[...]) * mr)

    return _pcall(body, name="mem_bwd", in_specs=[VMEM] * 5, out_specs=[VMEM] * 2,
                  out_shape=[SDS((D, 2 * D), BF16), SDS((1, D), F32)], vmem_mb=48)(mem, g, wkv, dkv, dg_acc)


def _adamw_math(w, g, m, v):
    m2 = ADAM_B1 * m + (1.0 - ADAM_B1) * g
    v2 = ADAM_B2 * v + (1.0 - ADAM_B2) * (g * g)
    mh = m2 / (1.0 - ADAM_B1 ** ADAM_STEP)
    vh = v2 / (1.0 - ADAM_B2 ** ADAM_STEP)
    return -ADAM_LR * (mh / (jnp.sqrt(vh) + ADAM_EPS) + ADAM_WD * w), m2, v2


def _adamw_layer(l, w, m, v, g, prev, which=None, rows=256):
    L, R, C = w.shape

    def body(w_r, m_r, v_r, g_r, *rest):
        g_o, d_o, m_o, v_o = rest[-4:]
        g = g_r[...]
        d, m2, v2 = _adamw_math(w_r[...], g, m_r[...], v_r[...])
        g_o[...] = g
        d_o[...] = d
        m_o[...] = m2
        v_o[...] = v2

    st = pl.BlockSpec((None, rows, C), lambda i: (l, i, 0))
    gs = pl.BlockSpec((rows, C), lambda i: (i, 0)) if which is None else pl.BlockSpec((None, rows, C), lambda i: (which, i, 0))
    carried = list(prev) if prev is not None else []
    return _pcall(body, name="adamw_layer", grid=(R // rows,), in_specs=[st] * 3 + [gs] + [HBM] * len(carried),
                  out_specs=[st] * 4, out_shape=[SDS(w.shape, F32)] * 4, vmem_mb=56,
                  aliases={4 + k: k for k in range(len(carried))} or None)(w, m, v, g, *carried)


def _adamw_flat(w, m, v, g, rows):
    R, C = w.shape

    def body(w_r, m_r, v_r, g_r, d_o, m_o, v_o):
        d, m2, v2 = _adamw_math(w_r[...], g_r[...], m_r[...], v_r[...])
        d_o[...] = d
        m_o[...] = m2
        v_o[...] = v2

    blk = pl.BlockSpec((rows, C), lambda i: (i, 0))
    return _pcall(body, name="adamw_flat", grid=(R // rows,), in_specs=[blk] * 4, out_specs=[blk] * 3,
                  out_shape=[SDS((R, C), F32)] * 3)(w, m, v, g)


_SMALL = ("mem_norm_g", "pre_norm_g", "post_norm_g", "gmlp_ln_g", "gmlp_ln_b", "gmlp_ws", "gmlp_bs", "conv_b", "lru_wr",
          "lru_br", "lru_wi", "lru_bi", "lru_lambda")


def _pack_small(parts, conv_w_part):
    rows = [parts[n].reshape(-1, 128) for n in _SMALL] + [conv_w_part.reshape(-1, 128)]
    used = sum(r.shape[0] for r in rows)
    rows.append(jnp.zeros((SMALL_ROWS - used, 128), F32))
    return jnp.concatenate(rows, axis=0)


def _unpack_small(pack, shapes):
    out, at = {}, 0
    for n in _SMALL:
        size = 1
        for s in shapes[n]:
            size *= s
        out[n] = pack[at:at + size // 128].reshape(shapes[n])
        at += size // 128
    return out, at


def kernel(x, mem, mem_norm_g, pre_norm_g, post_norm_g, w_in, gmlp_ln_g, gmlp_ln_b, gmlp_ws, gmlp_bs, conv_w, conv_b, lru_wr, lru_br, lru_wi, lru_bi, lru_lambda, w_kv, w_pa, w_pb, w_pc, w_out, loss_target, m_mem_norm_g, m_pre_norm_g, m_post_norm_g, m_w_in, m_gmlp_ln_g, m_gmlp_ln_b, m_gmlp_ws, m_gmlp_bs, m_conv_w, m_conv_b, m_lru_wr, m_lru_br, m_lru_wi, m_lru_bi, m_lru_lambda, m_w_kv, m_w_pa, m_w_pb, m_w_pc, m_w_out, v_mem_norm_g, v_pre_norm_g, v_post_norm_g, v_w_in, v_gmlp_ln_g, v_gmlp_ln_b, v_gmlp_ws, v_gmlp_bs, v_conv_w, v_conv_b, v_lru_wr, v_lru_br, v_lru_wi, v_lru_bi, v_lru_lambda, v_w_kv, v_w_pa, v_w_pb, v_w_pc, v_w_out):
    L = w_in.shape[0]
    S = x.shape[1]
    xs = [x[0]]
    mem2 = mem[0]
    mg = mem_norm_g.reshape(1, D)
    vec = lambda a, l: a[l].reshape(1, D)
    ci = lax.axis_index("c")
    jpos = 2 * lax.axis_index("x") + lax.axis_index("y")
    pos = jnp.reshape(jpos, (1,)).astype(jnp.int32)

    cw8 = jnp.pad(conv_w, ((0, 0), (0, 4), (0, 0)))
    placed = [_cast_place(l, pos, w_in, w_kv, w_pa, w_pb, w_pc, w_out, cw8) for l in range(L)]
    W = [None] * L
    started = _gather_start(0, placed[0])
    mid = _gather_mid(0, started, started[-1])
    started = _gather_start(1, placed[1], mid[-1])
    W[0] = _gather_end(0, mid, started[-1])

    saved = []
    for l in range(L):
        Win, Wkv, Wp, Cw = W[l]
        z, h = _mm_in(xs[l], vec(pre_norm_g, l), Win)
        bsb = jnp.broadcast_to(gmlp_bs[l][:, :, None], (NG, GB, GB))
        ya = _branch_a(z, vec(gmlp_ln_g, l), vec(gmlp_ln_b, l), gmlp_ws[l], bsb)
        yb, hs = _branch_b(z, Cw, vec(conv_b, l), lru_wr[l], vec(lru_br, l), lru_wi[l], vec(lru_bi, l), vec(lru_lambda, l))
        kv = _kv(mem2, mg, Wkv)
        yc = _branch_c(z, kv)
        pg = vec(post_norm_g, l)
        if l + 1 < L:
            mid = _gather_mid(l + 1, started, yc)
            if l + 2 < L:
                started = _gather_start(l + 2, placed[l + 2], mid[-1])
                pg = pg + started[-1][0, 0]
        pa, pb, pc, mgd, o, xn = _merge_out(ya, yb, yc, z, Wp, xs[l], pg)
        if l + 1 < L:
            W[l + 1] = _gather_end(l + 1, mid, xn)
        xs.append(xn)
        saved.append((z, h, ya, yb, yc, hs, kv, pa, pb, pc, mgd, o, bsb))

    loss11, dxn = _loss_head(xs[L], loss_target[0])
    loss = lax.psum(loss11[0, 0], ("x", "y", "c"))

    big = dict(w_in=(w_in, m_w_in, v_w_in), w_kv=(w_kv, m_w_kv, v_w_kv), w_pa=(w_pa, m_w_pa, v_w_pa),
               w_pb=(w_pb, m_w_pb, v_w_pb), w_pc=(w_pc, m_w_pc, v_w_pc), w_out=(w_out, m_w_out, v_w_out))
    out = {n: None for n in big}
    kin, nsh, nkv, rp = w_in.shape[1], w_in.shape[2], w_kv.shape[2], w_pa.shape[1]

    def finish_layer(l, a2a, after):
        thru = _a2a_wait(l, a2a, after)
        g6, (lin, lkv, lp) = thru[0:6], thru[6:9]
        own_in = lax.dynamic_slice(g6[0], (ci * (kin // 2), jpos * nsh), (kin // 2, nsh))
        own_kv = lax.dynamic_slice(g6[1], (ci * (kin // 2), jpos * nkv), (kin // 2, nkv))
        own_p = jnp.concatenate([lax.dynamic_slice(g6[2 + k], (jpos * rp + ci * (rp // 2), 0), (rp // 2, D)) for k in range(4)], axis=0)
        g_in, g_kv, g_p = _sum_share((lin, lkv, lp.reshape(NDEV - 1, 2 * rp, D)), (own_in, own_kv, own_p))
        g_p = g_p.reshape(4, rp, D)
        out["w_in"] = _adamw_layer(l, *big["w_in"], g_in, out["w_in"])
        out["w_kv"] = _adamw_layer(l, *big["w_kv"], g_kv, out["w_kv"])
        for k, n in enumerate(("w_pa", "w_pb", "w_pc", "w_out")):
            out[n] = _adamw_layer(l, *big[n], g_p, out[n], which=k)

    small = {n: [None] * L for n in _SMALL}
    dconv_w = [None] * L
    dg_mem = jnp.zeros((1, D), F32)
    pending = None
    sent = []
    for l in reversed(range(L)):
        Win, Wkv, Wp, Cw = W[l]
        z, h, ya, yb, yc, hs, kv, pa, pb, pc, mgd, o, bsb = saved[l]
        pg = vec(post_norm_g, l) if pending is None else vec(post_norm_g, l) + pending[1][-1][0, 0]
        do, dpa, dpb, dpc, dya, dyb, dyc, dz, dgpost = _out_bwd(dxn, o, pg, Wp, z, pa, pb, pc)
        dz, dws, dbs, dlg, dlb = _branch_a_bwd(z, dya, vec(gmlp_ln_g, l), vec(gmlp_ln_b, l), gmlp_ws[l], bsb, dz)
        dz, dcw, dcb, dwr, dbr, dwi, dbi, dlam = _branch_b_bwd(
            z, hs, dyb, Cw, vec(conv_b, l), lru_wr[l], vec(lru_br, l), lru_wi[l], vec(lru_bi, l), vec(lru_lambda, l), dz)
        dz, dkv = _branch_c_bwd(z, kv, dyc, dz)
        g_in = _mm_dwin(h, dz)
        g_kv, dg_mem = _mem_bwd(mem2, mg, Wkv, dkv, dg_mem)
        grads = (g_in, g_kv, _mm_tn(ya, dpa), _mm_tn(yb, dpb), _mm_tn(yc, dpc), _mm_tn(mgd, do))
        lands = (lax.empty((NDEV - 1, kin // 2, nsh), BF16), lax.empty((NDEV - 1, kin // 2, nkv), BF16),
                 lax.empty((NDEV - 1, 4, rp // 2, D), BF16))
        a2a = _a2a_start(l, grads, lands)
        dx, dgpre = _mm_dh(dz, Win, xs[l], dxn, vec(pre_norm_g, l) + a2a[-1][0, 0])
        pending = (l, a2a)
        sent.append(pending)
        for n, val in (("pre_norm_g", dgpre), ("post_norm_g", dgpost), ("gmlp_ln_g", dlg), ("gmlp_ln_b", dlb), ("gmlp_ws", dws),
                       ("gmlp_bs", dbs), ("conv_b", dcb), ("lru_wr", dwr), ("lru_br", dbr), ("lru_wi", dwi), ("lru_bi", dbi),
                       ("lru_lambda", dlam)):
            small[n][l] = val
        dconv_w[l] = dcw[0:4]
        dxn = dx
    grad_x = dxn.reshape(1, S, D)

    for l, a2a in sent[:-1]:
        finish_layer(l, a2a, dxn)

    parts = {n: jnp.stack(small[n]) for n in _SMALL if n != "mem_norm_g"}
    parts["mem_norm_g"] = dg_mem
    gsum = _allreduce_small(_pack_small(parts, jnp.stack(dconv_w)), out["w_out"][1])
    given = dict(mem_norm_g=(mem_norm_g, m_mem_norm_g, v_mem_norm_g), pre_norm_g=(pre_norm_g, m_pre_norm_g, v_pre_norm_g),
                 post_norm_g=(post_norm_g, m_post_norm_g, v_post_norm_g), gmlp_ln_g=(gmlp_ln_g, m_gmlp_ln_g, v_gmlp_ln_g),
                 gmlp_ln_b=(gmlp_ln_b, m_gmlp_ln_b, v_gmlp_ln_b), gmlp_ws=(gmlp_ws, m_gmlp_ws, v_gmlp_ws),
                 gmlp_bs=(gmlp_bs, m_gmlp_bs, v_gmlp_bs), conv_b=(conv_b, m_conv_b, v_conv_b), lru_wr=(lru_wr, m_lru_wr, v_lru_wr),
                 lru_br=(lru_br, m_lru_br, v_lru_br), lru_wi=(lru_wi, m_lru_wi, v_lru_wi), lru_bi=(lru_bi, m_lru_bi, v_lru_bi),
                 lru_lambda=(lru_lambda, m_lru_lambda, v_lru_lambda))
    shapes = {n: given[n][0].shape for n in _SMALL}
    zero_cw = jnp.zeros((L, 4, D), F32)
    packs = [_pack_small({n: given[n][k] for n in _SMALL}, zero_cw) for k in range(3)]
    dsm, msm, vsm = _adamw_flat(packs[0], packs[1], packs[2], gsum, 2560)
    g_small, at = _unpack_small(gsum, shapes)
    d_small, _ = _unpack_small(dsm, shapes)
    m_small, _ = _unpack_small(msm, shapes)
    v_small, _ = _unpack_small(vsm, shapes)
    for n in _SMALL:
        out[n] = (g_small[n], d_small[n], m_small[n], v_small[n])
    g_cw = lax.dynamic_slice_in_dim(gsum[at:at + L * 4 * D // 128].reshape(L * 4, D), jpos * (D // 4), D // 4, axis=1)
    d_cw, m_cw, v_cw = _adamw_flat(conv_w.reshape(L * 4, D // 4), m_conv_w.reshape(L * 4, D // 4),
                                   v_conv_w.reshape(L * 4, D // 4), g_cw, L * 4)
    out["conv_w"] = tuple(a.reshape(L, 4, D // 4) for a in (g_cw, d_cw, m_cw, v_cw))

    finish_layer(pending[0], pending[1], dsm)

    order = ("mem_norm_g", "pre_norm_g", "post_norm_g", "w_in", "gmlp_ln_g", "gmlp_ln_b", "gmlp_ws", "gmlp_bs", "conv_w", "conv_b",
             "lru_wr", "lru_br", "lru_wi", "lru_bi", "lru_lambda", "w_kv", "w_pa", "w_pb", "w_pc", "w_out")
    return (loss, grad_x, *[out[n][0] for n in order], *[out[n][1] for n in order], *[out[n][2] for n in order],
            *[out[n][3] for n in order])
```

```python
import functools

import jax
import jax.numpy as jnp
from jax import lax
from jax.experimental import pallas as pl
from jax.experimental.pallas import tpu as pltpu

F32 = jnp.float32
BF16 = jnp.bfloat16
SDS = jax.ShapeDtypeStruct
MESH = pl.DeviceIdType.MESH

D = 1024
NIN = 10 * D
MEM = 256
GB = 128
NG = 8
NH = 4
HD = D // NH
EPS = 1e-6
LRU_C = 8.0
ADAM_LR, ADAM_B1, ADAM_B2, ADAM_EPS, ADAM_WD, ADAM_STEP = 0.001, 0.9, 0.999, 1e-08, 0.01, 10
NDEV = 8
SMALL_ROWS = 12800

_CALL_KW = {}
HBM = pl.BlockSpec(memory_space=pltpu.HBM)
VMEM = pl.BlockSpec(memory_space=pltpu.VMEM)
SEM = pl.BlockSpec(memory_space=pltpu.SEMAPHORE)
ANY = pl.BlockSpec(memory_space=pl.ANY)
TOKEN = SDS((8, 128), F32)


def _pcall(body, *, name, in_specs, out_specs, out_shape, grid=None, scratch=(), vmem_mb=48, aliases=None, effect=False,
           prefetch=0):
    kw = dict(_CALL_KW)
    if aliases:
        kw["input_output_aliases"] = aliases
    params = dict(vmem_limit_bytes=vmem_mb << 20)
    if grid is not None:
        params["dimension_semantics"] = ("arbitrary",) * len(grid)
    if effect:
        params["has_side_effects"] = pltpu.SideEffectType.DATAFLOW_SIDE_EFFECTING
    if prefetch:
        kw["grid_spec"] = pltpu.PrefetchScalarGridSpec(num_scalar_prefetch=prefetch, grid=grid, in_specs=in_specs,
                                                       out_specs=out_specs, scratch_shapes=list(scratch))
    else:
        kw.update(in_specs=in_specs, out_specs=out_specs, scratch_shapes=list(scratch))
        if grid is not None:
            kw["grid"] = grid
    return pl.pallas_call(body, name=name, out_shape=out_shape, compiler_params=pltpu.CompilerParams(**params), **kw)


def _full(shape):
    nd = len(shape)
    return pl.BlockSpec(shape, lambda *_: (0,) * nd)


def _dot(a, b):
    return jnp.dot(a, b, preferred_element_type=F32)


def _dot_nt(a, b):
    return lax.dot_general(a, b, (((1,), (1,)), ((), ())), preferred_element_type=F32)


def _dot_tn(a, b):
    return lax.dot_general(a, b, (((0,), (0,)), ((), ())), preferred_element_type=F32)


def _rowsum(a):
    return jnp.sum(a, axis=0, keepdims=True)


def _sigmoid(x):
    return 0.5 * jnp.tanh(0.5 * x) + 0.5


def _silu_parts(g):
    s = _sigmoid(g)
    return g * s, s * (1.0 + g * (1.0 - s))


def _rms_scale(x):
    return lax.rsqrt(jnp.mean(x * x, axis=-1, keepdims=True) + EPS)


def _dz_col(k):
    return jnp.where(k < 3, k, jnp.where(k < 6, k + 4, k - 3))


def _coords():
    return lax.axis_index("x"), lax.axis_index("y"), lax.axis_index("c")


def _other_chips(x, y):
    return [(1 - x, y), (x, 1 - y), (1 - x, 1 - y)]


def _peer(x, y, c, mask):
    return (1 - x if mask & 4 else x, 1 - y if mask & 2 else y, 1 - c if mask & 1 else c)


def _remote(src, dst, ssem, rsem, k, to):
    return pltpu.make_async_remote_copy(src_ref=src, dst_ref=dst, send_sem=ssem.at[k], recv_sem=rsem.at[k], device_id=to,
                                        device_id_type=MESH)


def _w_half(a, ref, jj, cc):
    if a == 2:
        rp = ref.shape[1] // 4
        return ref.at[:, pl.ds(jj * rp + cc * (rp // 2), rp // 2), :]
    kin, nsh = ref.shape[0], ref.shape[1] // 4
    return ref.at[pl.ds(cc * (kin // 2), kin // 2), pl.ds(jj * nsh, nsh)]


def _cw_block(ref, jj):
    return ref.at[:, pl.ds(jj * (D // 4), D // 4)]


def _cast_place(l, pos, w_in, w_kv, w_pa, w_pb, w_pc, w_out, cw8):
    kin, nsh = w_in.shape[1], w_in.shape[2]
    nkv, rp = w_kv.shape[2], w_pa.shape[1]
    half = kin // 2

    def body(pos_r, win, wkv, pa, pb, pc, po, cw, Win, Wkv, Wp, Cw):
        Win[...] = win[...].astype(BF16)
        Wkv[...] = wkv[...].astype(BF16)

        @pl.when(pl.program_id(0) == 0)
        def _():
            for k, r in enumerate((pa, pb, pc, po)):
                Wp[k] = r[...].astype(BF16)
            Cw[...] = cw[...]

    proj = pl.BlockSpec((None, rp, D), lambda i, p: (l, 0, 0))
    return _pcall(
        body, name="cast_place", grid=(2,), prefetch=1,
        in_specs=[pl.BlockSpec((None, half, nsh), lambda i, p: (l, i, 0)), pl.BlockSpec((None, half, nkv), lambda i, p: (l, i, 0)),
                  proj, proj, proj, proj, pl.BlockSpec((None, 8, D // 4), lambda i, p: (l, 0, 0))],
        out_specs=[pl.BlockSpec((half, nsh), lambda i, p: (i, p[0])), pl.BlockSpec((half, nkv), lambda i, p: (i, p[0])),
                   pl.BlockSpec((4, rp, D), lambda i, p: (0, p[0], 0)), pl.BlockSpec((8, D // 4), lambda i, p: (0, p[0]))],
        out_shape=[SDS((kin, 4 * nsh), BF16), SDS((kin, 4 * nkv), BF16), SDS((4, 4 * rp, D), BF16), SDS((8, D), F32)],
    )(pos, w_in, w_kv, w_pa, w_pb, w_pc, w_out, cw8)


def _hbm_like(bufs):
    return [pltpu.HBM(b.shape, b.dtype) for b in bufs]


def _gather_start(l, bufs, after=None):
    extra = [] if after is None else [after]

    def body(win, wkv, wp, cw, *rest):
        ssem, rsem, token = rest[len(extra)], rest[len(extra) + 1], rest[-1]
        x, y, c = _coords()
        j = 2 * x + y
        refs = (win, wkv, wp)
        for k, chip in enumerate(_other_chips(x, y)):
            to = (chip[0], chip[1], c)
            for a in range(3):
                half = _w_half(a, refs[a], j, c)
                _remote(half, half, ssem, rsem, a * 3 + k, to).start()
            mine = _cw_block(cw, j)
            _remote(mine, mine, ssem, rsem, 9 + k, to).start()
        token[...] = jnp.zeros((8, 128), F32)

    return _pcall(
        body, name=f"gather_start_{l}", in_specs=[HBM] * 4 + [ANY] * len(extra), out_specs=[SEM, SEM] + [HBM] * 4 + [VMEM],
        out_shape=[pltpu.SemaphoreType.DMA((12,)), pltpu.SemaphoreType.DMA((12,))] + _hbm_like(bufs) + [TOKEN],
        aliases={0: 2, 1: 3, 2: 4, 3: 5}, effect=True,
    )(*[pltpu.with_memory_space_constraint(b, pltpu.HBM) for b in bufs], *extra)


def _gather_mid(l, started, after):
    ssem, rsem, b0, b1, b2, b3, _ = started

    def body(win, wkv, wp, cw, ssem, rsem, after_r, ssem2, rsem2, o0, o1, o2, o3, token):
        x, y, c = _coords()
        j = 2 * x + y
        me, sib = (x, y, c), (x, y, 1 - c)
        token[...] = jnp.zeros((8, 128), F32)
        refs = (win, wkv, wp)
        chips = _other_chips(x, y)
        for k, chip in enumerate(chips):
            jk = 2 * chip[0] + chip[1]
            for a in range(3):
                got = _w_half(a, refs[a], jk, c)
                _remote(got, got, ssem, rsem, a * 3 + k, me).wait_recv()
            got = _cw_block(cw, jk)
            _remote(got, got, ssem, rsem, 9 + k, me).wait_recv()
        for k in range(3):
            for a in range(3):
                half = _w_half(a, refs[a], j, c)
                _remote(half, half, ssem, rsem, a * 3 + k, me).wait_send()
            mine = _cw_block(cw, j)
            _remote(mine, mine, ssem, rsem, 9 + k, me).wait_send()
        for k, chip in enumerate(chips):
            jk = 2 * chip[0] + chip[1]
            for a in range(3):
                got = _w_half(a, refs[a], jk, c)
                _remote(got, got, ssem2, rsem2, a * 3 + k, sib).start()

    bufs = (b0, b1, b2, b3)
    return _pcall(
        body, name=f"gather_mid_{l}", in_specs=[HBM] * 4 + [SEM, SEM, ANY], out_specs=[SEM, SEM] + [HBM] * 4 + [VMEM],
        out_shape=[pltpu.SemaphoreType.DMA((9,)), pltpu.SemaphoreType.DMA((9,))] + _hbm_like(bufs) + [TOKEN],
        aliases={0: 2, 1: 3, 2: 4, 3: 5}, effect=True,
    )(b0, b1, b2, b3, ssem, rsem, after)


def _gather_end(l, mid, after):
    ssem2, rsem2, b0, b1, b2, b3, _ = mid

    def body(win, wkv, wp, cw, ssem2, rsem2, after_r, o0, o1, o2, o3):
        x, y, c = _coords()
        me = (x, y, c)
        refs = (win, wkv, wp)
        for k, chip in enumerate(_other_chips(x, y)):
            jk = 2 * chip[0] + chip[1]
            for a in range(3):
                got = _w_half(a, refs[a], jk, 1 - c)
                _remote(got, got, ssem2, rsem2, a * 3 + k, me).wait_recv()
                sent = _w_half(a, refs[a], jk, c)
                _remote(sent, sent, ssem2, rsem2, a * 3 + k, me).wait_send()

    bufs = (b0, b1, b2, b3)
    return _pcall(
        body, name=f"gather_end_{l}", in_specs=[HBM] * 4 + [SEM, SEM, ANY], out_specs=[HBM] * 4, out_shape=_hbm_like(bufs),
        aliases={0: 0, 1: 1, 2: 2, 3: 3}, effect=True,
    )(b0, b1, b2, b3, ssem2, rsem2, after)


def _g_piece(a, ref, jd, dc):
    if a == 2:
        rp = ref.shape[1] // 4
        return ref.at[:, pl.ds(jd * rp + dc * (rp // 2), rp // 2), :]
    kin, nsh = ref.shape[0], ref.shape[1] // 4
    return ref.at[pl.ds(dc * (kin // 2), kin // 2), pl.ds(jd * nsh, nsh)]


def _land_slot(a, lands, s):
    return lands[a].at[s]


def _a2a_start(l, grads, lands):
    def body(*refs):
        g, ld, ssem, rsem, token = refs[0:3], refs[3:6], refs[6], refs[7], refs[-1]
        x, y, c = _coords()
        for mask in range(1, NDEV):
            p = _peer(x, y, c, mask)
            for a in range(3):
                _remote(_g_piece(a, g[a], 2 * p[0] + p[1], p[2]), _land_slot(a, ld, mask - 1), ssem, rsem, a * 7 + mask - 1, p).start()
        token[...] = jnp.zeros((8, 128), F32)

    bufs = tuple(grads) + tuple(lands)
    return _pcall(
        body, name=f"a2a_start_{l}", in_specs=[HBM] * 6, out_specs=[SEM, SEM] + [HBM] * 6 + [VMEM],
        out_shape=[pltpu.SemaphoreType.DMA((21,)), pltpu.SemaphoreType.DMA((21,))] + _hbm_like(bufs) + [TOKEN],
        aliases={i: 2 + i for i in range(6)}, effect=True,
    )(*[pltpu.with_memory_space_constraint(b, pltpu.HBM) for b in bufs])


def _a2a_wait(l, started, after):
    ssem, rsem = started[0], started[1]
    bufs = tuple(started[2:8])

    def body(*refs):
        g, ld, ssem, rsem = refs[0:3], refs[3:6], refs[6], refs[7]
        x, y, c = _coords()
        me = (x, y, c)
        for mask in range(1, NDEV):
            for a in range(3):
                got = _land_slot(a, ld, mask - 1)
                _remote(got, got, ssem, rsem, a * 7 + mask - 1, me).wait_recv()
        for mask in range(1, NDEV):
            p = _peer(x, y, c, mask)
            for a in range(3):
                sent = _g_piece(a, g[a], 2 * p[0] + p[1], p[2])
                _remote(sent, sent, ssem, rsem, a * 7 + mask - 1, me).wait_send()

    return _pcall(
        body, name=f"a2a_wait_{l}", in_specs=[HBM] * 6 + [SEM, SEM, ANY], out_specs=[HBM] * 6, out_shape=_hbm_like(bufs),
        aliases={i: i for i in range(6)}, effect=True,
    )(*bufs, ssem, rsem, after)


def _sum_share(pos, lands, grads):
    rows, n = 128, 4
    widths = [ld.shape[2] for ld in lands]

    def body(pos_r, l0, w0, l1, w1, l2, w2, g0, g1, g2, b0, b1, b2, lsem, ssem, rsem):
        i = pl.program_id(0)
        x, y, c = _coords()
        sib = (x, y, 1 - c)
        ld, ow, gs, bufs = (l0, l1, l2), (w0, w1, w2), (g0, g1, g2), (b0, b1, b2)

        def dst(a, step):
            row = step * (2 * rows) + c * rows if a == 2 else c * (n * rows) + step * rows
            return gs[a].at[pl.ds(row, rows), :]

        def copies(a, step, sl):
            src = bufs[a].at[sl]
            lc = pltpu.make_async_copy(src, dst(a, step), lsem.at[a, sl])
            rc = pltpu.make_async_remote_copy(src_ref=src, dst_ref=dst(a, step), send_sem=ssem.at[a, sl], recv_sem=rsem.at[a],
                                              device_id=sib, device_id_type=MESH)
            return lc, rc

        def drain(a, step, sl):
            lc, rc = copies(a, step, sl)
            lc.wait()
            rc.wait_send()

        slot = i % 2

        @pl.when(i >= 2)
        def _():
            for a in range(3):
                drain(a, i - 2, slot)

        for a in range(3):
            acc = ow[a][...].astype(F32)
            for k in range(NDEV - 1):
                acc = acc + ld[a][k].astype(F32)
            bufs[a][slot] = acc
            lc, rc = copies(a, i, slot)
            lc.start()
            rc.start()

        @pl.when(i == n - 1)
        def _():
            for a in range(3):
                drain(a, n - 2, (n - 2) % 2)
                drain(a, n - 1, (n - 1) % 2)
                whole = gs[a].at[pl.ds(0, n * rows), :]
                pltpu.make_async_remote_copy(src_ref=whole, dst_ref=whole, send_sem=ssem.at[a, 0], recv_sem=rsem.at[a],
                                             device_id=(x, y, c), device_id_type=MESH).wait_recv()

    land = lambda w: pl.BlockSpec((NDEV - 1, rows, w), lambda i, p: (0, i, 0))
    in_specs = [land(widths[0]), pl.BlockSpec((rows, widths[0]), lambda i, p: (p[1] * n + i, p[0])),
                land(widths[1]), pl.BlockSpec((rows, widths[1]), lambda i, p: (p[1] * n + i, p[0])),
                land(widths[2]), pl.BlockSpec((None, rows, widths[2]), lambda i, p: (i, 2 * p[0] + p[1], 0))]
    args = [t for pair in zip(lands, grads) for t in pair]
    return _pcall(
        body, name="sum_share", grid=(n,), prefetch=1, in_specs=in_specs, out_specs=[HBM] * 3,
        out_shape=[SDS((2 * n * rows, w), F32) for w in widths],
        scratch=[pltpu.VMEM((2, rows, w), F32) for w in widths]
        + [pltpu.SemaphoreType.DMA((3, 2)), pltpu.SemaphoreType.DMA((3, 2)), pltpu.SemaphoreType.DMA((3,))],
    )(pos, *args)


def _allreduce_small(pack, after):
    R = pack.shape[0]
    r8 = R // NDEV

    def body(p_ref, after_r, full, land, red, ssem, rsem, ssem2, rsem2, lsem):
        x, y, c = _coords()
        me = 4 * x + 2 * y + c

        def idx(p):
            return 4 * p[0] + 2 * p[1] + p[2]

        own = pltpu.make_async_copy(p_ref.at[pl.ds(me * r8, r8), :], land.at[me], lsem.at[0])
        own.start()
        sent = []
        for mask in range(1, NDEV):
            p = _peer(x, y, c, mask)
            cp = _remote(p_ref.at[pl.ds(idx(p) * r8, r8), :], land.at[me], ssem, rsem, mask - 1, p)
            cp.start()
            sent.append(cp)
        for mask in range(1, NDEV):
            got = land.at[idx(_peer(x, y, c, mask))]
            _remote(got, got, ssem, rsem, mask - 1, (x, y, c)).wait_recv()
        own.wait()
        for cp in sent:
            cp.wait_send()
        acc = land[0]
        for k in range(1, NDEV):
            acc = acc + land[k]
        red[...] = acc
        mine = full.at[pl.ds(me * r8, r8), :]
        own2 = pltpu.make_async_copy(red, mine, lsem.at[1])
        own2.start()
        sent2 = []
        for mask in range(1, NDEV):
            cp = _remote(red, mine, ssem2, rsem2, mask - 1, _peer(x, y, c, mask))
            cp.start()
            sent2.append(cp)
        for mask in range(1, NDEV):
            got = full.at[pl.ds(idx(_peer(x, y, c, mask)) * r8, r8), :]
            _remote(got, got, ssem2, rsem2, mask - 1, (x, y, c)).wait_recv()
        own2.wait()
        for cp in sent2:
            cp.wait_send()

    return _pcall(
        body, name="allreduce_small", in_specs=[HBM, ANY], out_specs=HBM, out_shape=SDS((R, 128), F32),
        scratch=[pltpu.VMEM((NDEV, r8, 128), F32), pltpu.VMEM((r8, 128), F32)]
        + [pltpu.SemaphoreType.DMA((NDEV - 1,))] * 4 + [pltpu.SemaphoreType.DMA((2,))], vmem_mb=32,
    )(pack, after)


def _mm_in(x, g, w):
    S = x.shape[0]
    tm, tn = min(1024, S), 1280

    def body(x_ref, g_ref, w_ref, z_ref, h_ref, hs):
        @pl.when(pl.program_id(1) == 0)
        def _():
            xv = x_ref[...]
            hb = (xv * _rms_scale(xv) * g_ref[...]).astype(BF16)
            hs[...] = hb
            h_ref[...] = hb

        z_ref[...] = _dot(hs[...], w_ref[...]).astype(BF16)

    return _pcall(
        body, name="mm_in", grid=(S // tm, NIN // tn),
        in_specs=[pl.BlockSpec((tm, D), lambda i, j: (i, 0)), _full((1, D)), pl.BlockSpec((D, tn), lambda i, j: (0, j))],
        out_specs=[pl.BlockSpec((tm, tn), lambda i, j: (i, j)), pl.BlockSpec((tm, D), lambda i, j: (i, 0))],
        out_shape=[SDS((S, NIN), BF16), SDS((S, D), BF16)], scratch=[pltpu.VMEM((tm, D), BF16)],
    )(x, g, w)


def _chunk_mask():
    ri = lax.broadcasted_iota(jnp.int32, (GB, GB), 0)
    ci = lax.broadcasted_iota(jnp.int32, (GB, GB), 1)
    return (ri >= 64) | (ci < 64)


def _layernorm_parts(v):
    mu = jnp.mean(v, axis=-1, keepdims=True)
    d = v - mu
    rs = lax.rsqrt(jnp.mean(d * d, axis=-1, keepdims=True) + EPS)
    return d * rs, rs


def _branch_a(z, lg, lb, ws, bsb):
    S = z.shape[0]
    T = min(512, S)

    def body(zu, zv, zg, lg_r, lb_r, ws_r, bs_r, ya):
        vhat, _ = _layernorm_parts(zv[...].astype(F32))
        vnb = (vhat * lg_r[...] + lb_r[...]).astype(BF16)
        sil, _ = _silu_parts(zg[...].astype(F32))
        t = zu[...].astype(F32) * sil
        mask = _chunk_mask()
        for g in range(NG):
            wg = jnp.where(mask, ws_r[g], 0.0).astype(BF16)
            cs = slice(g * GB, (g + 1) * GB)
            for n in range(T // GB):
                rs = slice(n * GB, (n + 1) * GB)
                sv = _dot(wg, vnb[rs, cs]) + bs_r[g]
                ya[rs, cs] = (t[rs, cs] * sv).astype(BF16)

    zs = lambda k: pl.BlockSpec((T, D), lambda i: (i, k))
    return _pcall(
        body, name="branch_a", grid=(S // T,),
        in_specs=[zs(0), zs(1), zs(2), _full((1, D)), _full((1, D)), _full((NG, GB, GB)), _full((NG, GB, GB))],
        out_specs=pl.BlockSpec((T, D), lambda i: (i, 0)), out_shape=SDS((S, D), BF16),
    )(z, z, z, lg, lb, ws, bsb)


def _softplus_neg(lam):
    e = jnp.exp(-jnp.abs(lam))
    l1p = jnp.where(e < 1e-2, e * (1.0 - e * (0.5 - e * (1.0 / 3.0))), jnp.log(1.0 + e))
    return jnp.maximum(-lam, 0.0) + l1p


CH = 16


def _ck(c, off=0):
    return pl.ds(c * CH + off, CH)


def _half_sum(v):
    return v[0:8, :] + v[8:16, :]


def _lru_conv(xpad, c, cw, cb):
    xk = [xpad[_ck(c, 5 + k), :] for k in range(4)]
    return xk, cb + (((xk[0] * cw[0:1] + xk[1] * cw[1:2]) + xk[2] * cw[2:3]) + xk[3] * cw[3:4])


def _lru_gate_matmuls(xcb_s, wr_r, wi_r, pr_s, pi_s):
    for h in range(NG):
        cs = slice(h * GB, (h + 1) * GB)
        pr_s[:, cs] = _dot(xcb_s[:, cs], wr_r[h].astype(BF16))
        pi_s[:, cs] = _dot(xcb_s[:, cs], wi_r[h].astype(BF16))


def _lru_gates(pr, pi, br, bi, sp8):
    r = jax.nn.sigmoid(pr + br)
    ig = _sigmoid(pi + bi)
    la = sp8 * r
    a = jnp.exp(la)
    a2 = a * a
    mult = jnp.sqrt(-jnp.tanh(la) * (a2 + 1.0))
    return r, ig, a, a2, mult


def _tile_rows():
    return lax.broadcasted_iota(jnp.int32, (8, D), 0)


def _scan_forward(a_s, u_s, h_s, hcar, T):
    row = _tile_rows()

    def tile(i, hp):
        o = pl.multiple_of(i * 8, 8)
        A = a_s[pl.ds(o, 8), :]
        U = u_s[pl.ds(o, 8), :]
        for s in (1, 2, 4):
            m = row >= s
            U = jnp.where(m, U + A * pltpu.roll(U, s, 0), U)
            A = jnp.where(m, A * pltpu.roll(A, s, 0), A)
        H = U + A * hp
        h_s[pl.ds(o, 8), :] = H
        return jnp.broadcast_to(H[7:8, :], (8, D))

    hcar[...] = lax.fori_loop(0, T // 8, tile, hcar[...])


def _scan_reverse(b_s, d_s, l_s, lcar, T):
    row = _tile_rows()
    n = T // 8

    def tile(i, lp):
        o = pl.multiple_of((n - 1 - i) * 8, 8)
        B = b_s[pl.ds(o, 8), :]
        U = d_s[pl.ds(o, 8), :]
        for s in (1, 2, 4):
            m = row < 8 - s
            U = jnp.where(m, U + B * pltpu.roll(U, 8 - s, 0), U)
            B = jnp.where(m, B * pltpu.roll(B, 8 - s, 0), B)
        Lm = U + B * lp
        l_s[pl.ds(o, 8), :] = Lm
        return jnp.broadcast_to(Lm[0:1, :], (8, D))

    lcar[...] = lax.fori_loop(0, n, tile, lcar[...])


def _branch_b(z, cw, cb, wr, br, wi, bi, lam):
    S = z.shape[0]
    T = min(256, S)

    def body(zxb, zgb, cw_r, cb_r, wr_r, br_r, wi_r, bi_r, lam_r, yb, hs_o, xpad, a_s, u_s, hcar):
        @pl.when(pl.program_id(0) == 0)
        def _():
            xpad[pl.ds(0, 8), :] = jnp.zeros((8, D), F32)
            hcar[...] = jnp.zeros((8, D), F32)

        cw = cw_r[...]
        xpad[pl.ds(8, T), :] = zxb[...].astype(F32)
        xk = [xpad[pl.ds(5 + k, T), :] for k in range(4)]
        xc = cb_r[...] + (((xk[0] * cw[0:1] + xk[1] * cw[1:2]) + xk[2] * cw[2:3]) + xk[3] * cw[3:4])
        xcb = xc.astype(BF16)
        pr, pi = [], []
        for h in range(NG):
            cs = slice(h * GB, (h + 1) * GB)
            pr.append(_dot(xcb[:, cs], wr_r[h].astype(BF16)))
            pi.append(_dot(xcb[:, cs], wi_r[h].astype(BF16)))
        _, ig, a, _, mult = _lru_gates(jnp.concatenate(pr, axis=1), jnp.concatenate(pi, axis=1), br_r[...], bi_r[...],
                                       -LRU_C * _softplus_neg(lam_r[...]))
        a_s[...] = a
        u_s[...] = mult * (ig * xc)
        _scan_forward(a_s, u_s, hs_o, hcar, T)
        xpad[pl.ds(0, 8), :] = xpad[pl.ds(T, 8), :]
        sil, _ = _silu_parts(zgb[...].astype(F32))
        yb[...] = (hs_o[...] * sil).astype(BF16)

    zs = lambda k: pl.BlockSpec((T, D), lambda i: (i, k))
    row = pl.BlockSpec((T, D), lambda i: (i, 0))
    return _pcall(
        body, name="branch_b", grid=(S // T,),
        in_specs=[zs(3), zs(4), _full((8, D)), _full((1, D)), _full((NG, GB, GB)), _full((1, D)), _full((NG, GB, GB)),
                  _full((1, D)), _full((1, D))],
        out_specs=[row, row], out_shape=[SDS((S, D), BF16), SDS((S, D), F32)],
        scratch=[pltpu.VMEM((T + 8, D), F32), pltpu.VMEM((T, D), F32), pltpu.VMEM((T, D), F32), pltpu.VMEM((8, D), F32)],
    )(z, z, cw, cb, wr, br, wi, bi, lam)


def _kv(mem, g, wkv):
    def body(m_ref, g_ref, w_ref, kv_ref):
        m = m_ref[...]
        mn = (m * _rms_scale(m) * g_ref[...]).astype(BF16)
        kv_ref[...] = _dot(mn, w_ref[...]).astype(BF16)

    return _pcall(body, name="mem_kv", in_specs=[VMEM] * 3, out_specs=VMEM, out_shape=SDS((MEM, 2 * D), BF16),
                  vmem_mb=32)(mem, g, wkv)


def _softmax_rows(s):
    e = jnp.exp(s - jnp.max(s, axis=-1, keepdims=True))
    return e / jnp.sum(e, axis=-1, keepdims=True)


def _branch_c(z, kv):
    S = z.shape[0]
    T = min(512, S)

    def body(zq, zg, kv_r, yc):
        sil, _ = _silu_parts(zg[...].astype(F32))
        for h in range(NH):
            cs = slice(h * HD, (h + 1) * HD)
            p = _softmax_rows(_dot_nt(zq[:, cs], kv_r[:, cs]) * (HD ** -0.5))
            att = _dot(p.astype(BF16), kv_r[:, D + h * HD:D + (h + 1) * HD])
            yc[:, cs] = (att * sil[:, cs]).astype(BF16)

    zs = lambda k: pl.BlockSpec((T, D), lambda i: (i, k))
    return _pcall(body, name="branch_c", grid=(S // T,), in_specs=[zs(5), zs(6), _full((MEM, 2 * D))],
                  out_specs=pl.BlockSpec((T, D), lambda i: (i, 0)), out_shape=SDS((S, D), BF16))(z, z, kv)


def _merge_out(ya, yb, yc, z, wp, x, pg):
    S = x.shape[0]
    T = min(256, S)

    def body(ya_r, yb_r, yc_r, m0, m1, m2, wp_r, x_r, pg_r, pa_o, pb_o, pc_o, mg_o, o_o, xn_o):
        merged = None
        for y_r, ml, p_o, k in ((ya_r, m0, pa_o, 0), (yb_r, m1, pb_o, 1), (yc_r, m2, pc_o, 2)):
            p = _dot(y_r[...], wp_r[k])
            p_o[...] = p.astype(BF16)
            t = _sigmoid(ml[...].astype(F32)) * p
            merged = t if merged is None else merged + t
        mb = merged.astype(BF16)
        mg_o[...] = mb
        o = _dot(mb, wp_r[3])
        o_o[...] = o.astype(BF16)
        xn_o[...] = x_r[...] + o * _rms_scale(o) * pg_r[...]

    row = pl.BlockSpec((T, D), lambda i: (i, 0))
    zs = lambda k: pl.BlockSpec((T, D), lambda i: (i, k))
    return _pcall(
        body, name="merge_out", grid=(S // T,),
        in_specs=[row, row, row, zs(7), zs(8), zs(9), _full((4, D, D)), row, _full((1, D))],
        out_specs=[row] * 6, out_shape=[SDS((S, D), BF16)] * 5 + [SDS((S, D), F32)], vmem_mb=56,
    )(ya, yb, yc, z, z, z, wp, x, pg)


def _loss_head(y, t):
    S = y.shape[0]
    T = min(512, S)

    def body(y_r, t_r, loss_o, dy_o):
        @pl.when(pl.program_id(0) == 0)
        def _():
            loss_o[...] = jnp.zeros((1, 1), F32)

        e = y_r[...] - t_r[...]
        dy_o[...] = e * (1.0 / D)
        loss_o[...] += 0.5 * _rowsum(jnp.sum(e * e, axis=1, keepdims=True) * (1.0 / D))

    row = pl.BlockSpec((T, D), lambda i: (i, 0))
    return _pcall(body, name="loss_head", grid=(S // T,), in_specs=[row, row], out_specs=[_full((1, 1)), row],
                  out_shape=[SDS((1, 1), F32), SDS((S, D), F32)])(y, t)


def _accumulate(first, ref, val):
    @pl.when(first)
    def _():
        ref[...] = val

    @pl.when(jnp.logical_not(first))
    def _():
        ref[...] += val


def _out_bwd(dxn, o, pg, wp, z, pa, pb, pc):
    S = dxn.shape[0]
    T = min(256, S)

    def body(dy_r, o_r, pg_r, wp_r, m0, m1, m2, pa_r, pb_r, pc_r, do_o, dpa_o, dpb_o, dpc_o, dya_o, dyb_o, dyc_o, dz_o, dg_o, dm_s):
        pg = pg_r[...]
        s = jnp.zeros((8, D), F32)
        for c in range(T // CH):
            dy, o = dy_r[_ck(c), :], o_r[_ck(c), :].astype(F32)
            r2 = _rms_scale(o)
            w = dy * pg
            do_o[_ck(c), :] = (r2 * w - o * (r2 * r2 * r2) * jnp.mean(w * o, axis=-1, keepdims=True)).astype(BF16)
            s = s + _half_sum(dy * o * r2)
        _accumulate(pl.program_id(0) == 0, dg_o, _rowsum(s))
        dm_s[...] = _dot_nt(do_o[...], wp_r[3])
        for k, (ml, p_r, dp_o, dy_o) in enumerate(((m0, pa_r, dpa_o, dya_o), (m1, pb_r, dpb_o, dyb_o), (m2, pc_r, dpc_o, dyc_o))):
            for c in range(T // CH):
                gk = _sigmoid(ml[_ck(c), :].astype(F32))
                dm = dm_s[_ck(c), :]
                dz_o[k, _ck(c), :] = (dm * p_r[_ck(c), :].astype(F32) * gk * (1.0 - gk)).astype(BF16)
                dp_o[_ck(c), :] = (gk * dm).astype(BF16)
            dy_o[...] = _dot_nt(dp_o[...], wp_r[k]).astype(BF16)

    row = pl.BlockSpec((T, D), lambda i: (i, 0))
    zs = lambda k: pl.BlockSpec((T, D), lambda i: (i, k))
    return _pcall(
        body, name="out_bwd", grid=(S // T,),
        in_specs=[row, row, _full((1, D)), _full((4, D, D)), zs(7), zs(8), zs(9), row, row, row],
        out_specs=[row] * 7 + [pl.BlockSpec((3, T, D), lambda i: (1, i, 0)), _full((1, D))],
        out_shape=[SDS((S, D), BF16)] * 7 + [SDS((10, S, D), BF16), SDS((1, D), F32)], scratch=[pltpu.VMEM((T, D), F32)],
        vmem_mb=56,
    )(dxn, o, pg, wp, z, z, z, pa, pb, pc)


def _branch_a_bwd(z, dya, lg, lb, ws, bsb, dz):
    S = z.shape[0]
    T = min(512, S)
    nblk = S // T

    def body(zu, zv, zg, dy_r, lg_r, lb_r, ws_r, bs_r, dz_in, dz_o, dws_o, dbs_o, dlg_o, dlb_o, dvn_s, bacc):
        i = pl.program_id(0)

        @pl.when(i == 0)
        def _():
            dws_o[...] = jnp.zeros((NG, GB, GB), F32)
            bacc[...] = jnp.zeros((NG, GB, GB), F32)

        vhat, rs = _layernorm_parts(zv[...].astype(F32))
        vnb = (vhat * lg_r[...] + lb_r[...]).astype(BF16)
        ga = zg[...].astype(F32)
        sil, dsil = _silu_parts(ga)
        u = zu[...].astype(F32)
        dy = dy_r[...].astype(F32)
        t = dy * sil
        dsv_all = t * u
        dga_pre = dy * u * dsil
        mask = _chunk_mask()
        for g in range(NG):
            wf = jnp.where(mask, ws_r[g], 0.0)
            wg = wf.astype(BF16)
            wgt = wf.T.astype(BF16)
            cs = slice(g * GB, (g + 1) * GB)
            dw = jnp.zeros((GB, GB), F32)
            db = jnp.zeros((GB, GB), F32)
            for n in range(T // GB):
                rsl = slice(n * GB, (n + 1) * GB)
                vb = vnb[rsl, cs]
                sv = _dot(wg, vb) + bs_r[g]
                dz_o[0, rsl, cs] = (t[rsl, cs] * sv).astype(BF16)
                dz_o[2, rsl, cs] = (dga_pre[rsl, cs] * sv).astype(BF16)
                dsv = dsv_all[rsl, cs]
                dsb = dsv.astype(BF16)
                dvn_s[rsl, cs] = _dot(wgt, dsb)
                dw = dw + _dot_nt(dsb, vb)
                db = db + dsv
            dws_o[g] += jnp.where(mask, dw, 0.0)
            bacc[g] += db
        dvn = dvn_s[...]
        dvh = dvn * lg_r[...]
        dv = rs * (dvh - jnp.mean(dvh, axis=-1, keepdims=True) - vhat * jnp.mean(dvh * vhat, axis=-1, keepdims=True))
        dz_o[1] = dv.astype(BF16)
        _accumulate(i == 0, dlg_o, _rowsum(dvn * vhat))
        _accumulate(i == 0, dlb_o, _rowsum(dvn))

        @pl.when(i == nblk - 1)
        def _():
            for g in range(NG):
                dbs_o[g:g + 1, :] = _rowsum(bacc[g].T)

    zs = lambda k: pl.BlockSpec((T, D), lambda i: (i, k))
    return _pcall(
        body, name="branch_a_bwd", grid=(nblk,),
        in_specs=[zs(0), zs(1), zs(2), pl.BlockSpec((T, D), lambda i: (i, 0)), _full((1, D)), _full((1, D)),
                  _full((NG, GB, GB)), _full((NG, GB, GB)), HBM],
        out_specs=[pl.BlockSpec((3, T, D), lambda i: (0, i, 0)), _full((NG, GB, GB)), _full((NG, GB)), _full((1, D)),
                   _full((1, D))],
        out_shape=[SDS((10, S, D), BF16), SDS((NG, GB, GB), F32), SDS((NG, GB), F32), SDS((1, D), F32), SDS((1, D), F32)],
        scratch=[pltpu.VMEM((T, D), F32), pltpu.VMEM((NG, GB, GB), F32)], aliases={8: 0},
    )(z, z, z, dya, lg, lb, ws, bsb, dz)


def _branch_b_bwd(z, hs, dyb, cw, cb, wr, br, wi, bi, lam, dz):
    S = z.shape[0]
    T = min(256, S)
    nblk = S // T

    def body(zxb, zprev, zgb, hs_r, hprev_r, dy_r, cw_r, cb_r, wr_r, br_r, wi_r, bi_r, lam_r, dz_in,
             dz_o, dcw_o, dcb_o, dwr_o, dbr_o, dwi_o, dbi_o, dlam_o, xpad, hpad, apad, dpad, xc_s, pr_s, pi_s, r_s, ig_s, m_s,
             b_s, d_s, l_s, back_s, xcb_s, dprb_s, dpib_s, lcar):
        i = pl.program_id(0)
        blk = nblk - 1 - i
        first = i == 0

        @pl.when(first)
        def _():
            apad[pl.ds(T, 8), :] = jnp.zeros((8, D), F32)
            dpad[pl.ds(T, 8), :] = jnp.zeros((8, D), F32)
            lcar[...] = jnp.zeros((8, D), F32)
            dcw_o[...] = jnp.zeros((8, D), F32)
            dwr_o[...] = jnp.zeros((NG, GB, GB), F32)
            dwi_o[...] = jnp.zeros((NG, GB, GB), F32)

        keep = (blk > 0).astype(F32)
        nck = T // CH
        cw, cb, br, bi, lam = cw_r[...], cb_r[...], br_r[...], bi_r[...], lam_r[...]
        sp8 = -LRU_C * _softplus_neg(lam)
        xpad[pl.ds(0, 8), :] = zprev[...].astype(F32)[8:16, :] * keep
        hpad[pl.ds(0, 8), :] = hprev_r[...] * keep
        for c in range(nck):
            xpad[_ck(c, 8), :] = zxb[_ck(c), :].astype(F32)
            hpad[_ck(c, 8), :] = hs_r[_ck(c), :]
            _, xc = _lru_conv(xpad, c, cw, cb)
            xc_s[_ck(c), :] = xc
            xcb_s[_ck(c), :] = xc.astype(BF16)
        _lru_gate_matmuls(xcb_s, wr_r, wi_r, pr_s, pi_s)
        for c in range(nck):
            r, ig, a, _, mult = _lru_gates(pr_s[_ck(c), :], pi_s[_ck(c), :], br, bi, sp8)
            r_s[_ck(c), :] = r
            ig_s[_ck(c), :] = ig
            m_s[_ck(c), :] = mult
            apad[_ck(c), :] = a
            sil, dsil = _silu_parts(zgb[_ck(c), :].astype(F32))
            dy = dy_r[_ck(c), :].astype(F32)
            dz_o[1, _ck(c), :] = (dy * hs_r[_ck(c), :] * dsil).astype(BF16)
            d_s[_ck(c), :] = dy * sil
        for c in range(nck):
            b_s[_ck(c), :] = apad[_ck(c, 1), :]
        _scan_reverse(b_s, d_s, l_s, lcar, T)
        s_sp = s_br = s_bi = jnp.zeros((8, D), F32)
        for c in range(nck):
            lm, r, ig, mult, a, xc = l_s[_ck(c), :], r_s[_ck(c), :], ig_s[_ck(c), :], m_s[_ck(c), :], apad[_ck(c), :], xc_s[_ck(c), :]
            t = lm * mult
            dpad[_ck(c), :] = t * ig
            dl = lm * hpad[_ck(c, 7), :] * a - (lm * ig * xc) * (a * a) / mult
            dpr = dl * sp8 * r * (1.0 - r)
            dpi = t * xc * ig * (1.0 - ig)
            s_sp = s_sp + _half_sum(dl * r)
            s_br = s_br + _half_sum(dpr)
            s_bi = s_bi + _half_sum(dpi)
            dprb_s[_ck(c), :] = dpr.astype(BF16)
            dpib_s[_ck(c), :] = dpi.astype(BF16)
        _accumulate(first, dlam_o, _rowsum(s_sp) * (LRU_C * jax.nn.sigmoid(-lam)))
        _accumulate(first, dbr_o, _rowsum(s_br))
        _accumulate(first, dbi_o, _rowsum(s_bi))
        for h in range(NG):
            cs = slice(h * GB, (h + 1) * GB)
            back_s[:, cs] = _dot_nt(dprb_s[:, cs], wr_r[h].astype(BF16)) + _dot_nt(dpib_s[:, cs], wi_r[h].astype(BF16))
            dwr_o[h] += _dot_tn(xcb_s[:, cs], dprb_s[:, cs])
            dwi_o[h] += _dot_tn(xcb_s[:, cs], dpib_s[:, cs])
        s_cb = jnp.zeros((8, D), F32)
        s_cw = [jnp.zeros((8, D), F32)] * 4
        for c in range(nck):
            dxc = dpad[_ck(c), :] + back_s[_ck(c), :]
            dpad[_ck(c), :] = dxc
            s_cb = s_cb + _half_sum(dxc)
            s_cw = [s_cw[k] + _half_sum(xpad[_ck(c, 5 + k), :] * dxc) for k in range(4)]
        _accumulate(first, dcb_o, _rowsum(s_cb))
        for k in range(4):
            dcw_o[k:k + 1, :] += _rowsum(s_cw[k])
        for c in range(nck):
            dxb = ((dpad[_ck(c, 3), :] * cw[0:1] + dpad[_ck(c, 2), :] * cw[1:2]) + dpad[_ck(c, 1), :] * cw[2:3]) + dpad[_ck(c), :] * cw[3:4]
            dz_o[0, _ck(c), :] = dxb.astype(BF16)
        apad[pl.ds(T, 8), :] = apad[pl.ds(0, 8), :]
        dpad[pl.ds(T, 8), :] = dpad[pl.ds(0, 8), :]

    rev = lambda k: pl.BlockSpec((T, D), lambda i: (nblk - 1 - i, k))
    prev16 = pl.BlockSpec((16, D), lambda i: (jnp.maximum((nblk - 1 - i) * (T // 16) - 1, 0), 3))
    prev8 = pl.BlockSpec((8, D), lambda i: (jnp.maximum((nblk - 1 - i) * (T // 8) - 1, 0), 0))
    vec, mat = _full((1, D)), _full((NG, GB, GB))
    return _pcall(
        body, name="branch_b_bwd", grid=(nblk,),
        in_specs=[rev(3), prev16, rev(4), rev(0), prev8, rev(0), _full((8, D)), vec, mat, vec, mat, vec, vec, HBM],
        out_specs=[pl.BlockSpec((2, T, D), lambda i: (3, nblk - 1 - i, 0)), _full((8, D)), vec, mat, vec, mat, vec, vec],
        out_shape=[SDS((10, S, D), BF16), SDS((8, D), F32), SDS((1, D), F32), SDS((NG, GB, GB), F32), SDS((1, D), F32),
                   SDS((NG, GB, GB), F32), SDS((1, D), F32), SDS((1, D), F32)],
        scratch=[pltpu.VMEM((T + 8, D), F32)] * 4 + [pltpu.VMEM((T, D), F32)] * 10 + [pltpu.VMEM((T, D), BF16)] * 3
        + [pltpu.VMEM((8, D), F32)],
        aliases={13: 0}, vmem_mb=56,
    )(z, z, z, hs, hs, dyb, cw, cb, wr, br, wi, bi, lam, dz)


def _branch_c_bwd(z, kv, dyc, dz):
    S = z.shape[0]
    T = min(512, S)

    def body(zq, zg, kv_r, dy_r, dz_in, dz_o, dkv_o):
        @pl.when(pl.program_id(0) == 0)
        def _():
            dkv_o[...] = jnp.zeros((MEM, 2 * D), F32)

        gc = zg[...].astype(F32)
        sil, dsil = _silu_parts(gc)
        dy = dy_r[...].astype(F32)
        datt = dy * sil
        dgc_pre = dy * dsil
        scale = HD ** -0.5
        for h in range(NH):
            cs = slice(h * HD, (h + 1) * HD)
            vs = slice(D + h * HD, D + (h + 1) * HD)
            qh = zq[:, cs]
            p = _softmax_rows(_dot_nt(qh, kv_r[:, cs]) * scale)
            pb = p.astype(BF16)
            att = _dot(pb, kv_r[:, vs])
            dz_o[1, :, cs] = (dgc_pre[:, cs] * att).astype(BF16)
            dab = datt[:, cs].astype(BF16)
            dp = _dot_nt(dab, kv_r[:, vs])
            ds = (p * (dp - jnp.sum(p * dp, axis=-1, keepdims=True)) * scale).astype(BF16)
            dz_o[0, :, cs] = _dot(ds, kv_r[:, cs]).astype(BF16)
            dkv_o[:, cs] += _dot_tn(ds, qh)
            dkv_o[:, vs] += _dot_tn(pb, dab)

    zs = lambda k: pl.BlockSpec((T, D), lambda i: (i, k))
    return _pcall(
        body, name="branch_c_bwd", grid=(S // T,),
        in_specs=[zs(5), zs(6), _full((MEM, 2 * D)), pl.BlockSpec((T, D), lambda i: (i, 0)), HBM],
        out_specs=[pl.BlockSpec((2, T, D), lambda i: (4, i, 0)), _full((MEM, 2 * D))],
        out_shape=[SDS((10, S, D), BF16), SDS((MEM, 2 * D), F32)], aliases={4: 0},
    )(z, z, kv, dyc, dz)


def _mm_dh(dz, w, x, dxn, g):
    S = x.shape[0]
    tm = min(1024, S)

    def body(dz_r, w_r, x_r, dxn_r, g_r, dx_o, dg_o, acc):
        i, k = pl.program_id(0), pl.program_id(1)
        _accumulate(k == 0, acc, _dot_nt(dz_r[0], w_r[...]))

        @pl.when(k == 9)
        def _():
            g = g_r[...]
            s = jnp.zeros((8, D), F32)
            for c in range(tm // CH):
                dh, xv = acc[_ck(c), :], x_r[_ck(c), :]
                r1 = _rms_scale(xv)
                wv = dh * g
                dx_o[_ck(c), :] = dxn_r[_ck(c), :] + r1 * wv - xv * (r1 * r1 * r1) * jnp.mean(wv * xv, axis=-1, keepdims=True)
                s = s + _half_sum(dh * xv * r1)
            _accumulate(i == 0, dg_o, _rowsum(s))

    row = pl.BlockSpec((tm, D), lambda i, k: (i, 0))
    return _pcall(
        body, name="mm_dh", grid=(S // tm, 10),
        in_specs=[pl.BlockSpec((1, tm, D), lambda i, k: (k, i, 0)), pl.BlockSpec((D, D), lambda i, k: (0, _dz_col(k))),
                  row, row, _full((1, D))],
        out_specs=[row, _full((1, D))], out_shape=[SDS((S, D), F32), SDS((1, D), F32)],
        scratch=[pltpu.VMEM((tm, D), F32)],
    )(dz, w, x, dxn, g)


def _mm_dwin(h, dz):
    S = h.shape[0]
    tk = min(1024, S)
    nk = S // tk

    def body(h_r, dz_r, o_r, acc):
        k = pl.program_id(1)
        _accumulate(k == 0, acc, _dot_tn(h_r[...], dz_r[0]))

        @pl.when(k == nk - 1)
        def _():
            o_r[...] = acc[...].astype(BF16)

    return _pcall(
        body, name="mm_dwin", grid=(10, nk),
        in_specs=[pl.BlockSpec((tk, D), lambda n, k: (k, 0)), pl.BlockSpec((1, tk, D), lambda n, k: (n, k, 0))],
        out_specs=pl.BlockSpec((D, D), lambda n, k: (0, _dz_col(n))), out_shape=SDS((D, NIN), BF16),
        scratch=[pltpu.VMEM((D, D), F32)],
    )(h, dz)


def _mm_tn4(a4, b4):
    S = a4[0].shape[0]
    tk = min(1024, S)
    nk = S // tk

    def body(*refs):
        a_r, b_r, o_r, acc = refs[0:4], refs[4:8], refs[8], refs[9]
        w, k = pl.program_id(0), pl.program_id(1)
        for a in range(4):
            @pl.when(w == a)
            def _(a=a):
                _accumulate(k == 0, acc, _dot_tn(a_r[a][...], b_r[a][...]))

        @pl.when(k == nk - 1)
        def _():
            o_r[...] = acc[...].astype(BF16)

    def blk(a):
        return pl.BlockSpec((tk, D), lambda w, k: (jnp.where(w == a, k, jnp.where(w < a, 0, nk - 1)), 0))

    return _pcall(body, name="mm_tn4", grid=(4, nk), in_specs=[blk(a) for a in range(4)] * 2,
                  out_specs=pl.BlockSpec((None, D, D), lambda w, k: (w, 0, 0)), out_shape=SDS((4, D, D), BF16),
                  scratch=[pltpu.VMEM((D, D), F32)])(*a4, *b4)


def _mem_bwd(mem, g, wkv, dkv, dg_acc):
    def body(m_ref, g_ref, w_ref, dkv_ref, acc_ref, dw_ref, dg_ref):
        m = m_ref[...]
        mr = m * _rms_scale(m)
        mn = (mr * g_ref[...]).astype(BF16)
        dkb = dkv_ref[...].astype(BF16)
        dw_ref[...] = _dot_tn(mn, dkb).astype(BF16)
        dg_ref[...] = acc_ref[...] + _rowsum(_dot_nt(dkb, w_ref[...]) * mr)

    return _pcall(body, name="mem_bwd", in_specs=[VMEM] * 5, out_specs=[VMEM] * 2,
                  out_shape=[SDS((D, 2 * D), BF16), SDS((1, D), F32)], vmem_mb=48)(mem, g, wkv, dkv, dg_acc)


def _adamw_math(w, g, m, v):
    m2 = ADAM_B1 * m + (1.0 - ADAM_B1) * g
    v2 = ADAM_B2 * v + (1.0 - ADAM_B2) * (g * g)
    mh = m2 / (1.0 - ADAM_B1 ** ADAM_STEP)
    vh = v2 / (1.0 - ADAM_B2 ** ADAM_STEP)
    return -ADAM_LR * (mh / (jnp.sqrt(vh) + ADAM_EPS) + ADAM_WD * w), m2, v2


def _adamw_layer(l, w, m, v, g, prev, which=None, rows=256):
    L, R, C = w.shape

    def body(w_r, m_r, v_r, g_r, *rest):
        g_o, d_o, m_o, v_o = rest[-4:]
        g = g_r[...]
        d, m2, v2 = _adamw_math(w_r[...], g, m_r[...], v_r[...])
        g_o[...] = g
        d_o[...] = d
        m_o[...] = m2
        v_o[...] = v2

    st = pl.BlockSpec((None, rows, C), lambda i: (l, i, 0))
    gs = pl.BlockSpec((rows, C), lambda i: (i, 0)) if which is None else pl.BlockSpec((None, rows, C), lambda i: (which, i, 0))
    carried = list(prev) if prev is not None else []
    return _pcall(body, name="adamw_layer", grid=(R // rows,), in_specs=[st] * 3 + [gs] + [HBM] * len(carried),
                  out_specs=[st] * 4, out_shape=[SDS(w.shape, F32)] * 4, vmem_mb=56,
                  aliases={4 + k: k for k in range(len(carried))} or None)(w, m, v, g, *carried)


def _adamw_flat(w, m, v, g, rows):
    R, C = w.shape

    def body(w_r, m_r, v_r, g_r, d_o, m_o, v_o):
        d, m2, v2 = _adamw_math(w_r[...], g_r[...], m_r[...], v_r[...])
        d_o[...] = d
        m_o[...] = m2
        v_o[...] = v2

    blk = pl.BlockSpec((rows, C), lambda i: (i, 0))
    return _pcall(body, name="adamw_flat", grid=(R // rows,), in_specs=[blk] * 4, out_specs=[blk] * 3,
                  out_shape=[SDS((R, C), F32)] * 3)(w, m, v, g)


_SMALL = ("mem_norm_g", "pre_norm_g", "post_norm_g", "gmlp_ln_g", "gmlp_ln_b", "gmlp_ws", "gmlp_bs", "conv_b", "lru_wr",
          "lru_br", "lru_wi", "lru_bi", "lru_lambda")


def _pack_small(parts, conv_w_part):
    rows = [parts[n].reshape(-1, 128) for n in _SMALL] + [conv_w_part.reshape(-1, 128)]
    used = sum(r.shape[0] for r in rows)
    rows.append(jnp.zeros((SMALL_ROWS - used, 128), F32))
    return jnp.concatenate(rows, axis=0)


def _unpack_small(pack, shapes):
    out, at = {}, 0
    for n in _SMALL:
        size = 1
        for s in shapes[n]:
            size *= s
        out[n] = pack[at:at + size // 128].reshape(shapes[n])
        at += size // 128
    return out, at


def kernel(x, mem, mem_norm_g, pre_norm_g, post_norm_g, w_in, gmlp_ln_g, gmlp_ln_b, gmlp_ws, gmlp_bs, conv_w, conv_b, lru_wr, lru_br, lru_wi, lru_bi, lru_lambda, w_kv, w_pa, w_pb, w_pc, w_out, loss_target, m_mem_norm_g, m_pre_norm_g, m_post_norm_g, m_w_in, m_gmlp_ln_g, m_gmlp_ln_b, m_gmlp_ws, m_gmlp_bs, m_conv_w, m_conv_b, m_lru_wr, m_lru_br, m_lru_wi, m_lru_bi, m_lru_lambda, m_w_kv, m_w_pa, m_w_pb, m_w_pc, m_w_out, v_mem_norm_g, v_pre_norm_g, v_post_norm_g, v_w_in, v_gmlp_ln_g, v_gmlp_ln_b, v_gmlp_ws, v_gmlp_bs, v_conv_w, v_conv_b, v_lru_wr, v_lru_br, v_lru_wi, v_lru_bi, v_lru_lambda, v_w_kv, v_w_pa, v_w_pb, v_w_pc, v_w_out):
    L = w_in.shape[0]
    S = x.shape[1]
    xs = [x[0]]
    mem2 = mem[0]
    mg = mem_norm_g.reshape(1, D)
    vec = lambda a, l: a[l].reshape(1, D)
    ci = lax.axis_index("c")
    jpos = 2 * lax.axis_index("x") + lax.axis_index("y")
    pos = jnp.reshape(jpos, (1,)).astype(jnp.int32)
    pos2 = jnp.stack([jpos, ci]).astype(jnp.int32)

    cw8 = jnp.pad(conv_w, ((0, 0), (0, 4), (0, 0)))
    placed = [_cast_place(l, pos, w_in, w_kv, w_pa, w_pb, w_pc, w_out, cw8) for l in range(L)]
    W = [None] * L
    started = _gather_start(0, placed[0])
    mid = _gather_mid(0, started, started[-1])
    started = _gather_start(1, placed[1], mid[-1])
    W[0] = _gather_end(0, mid, started[-1])

    saved = []
    for l in range(L):
        Win, Wkv, Wp, Cw = W[l]
        z, h = _mm_in(xs[l], vec(pre_norm_g, l), Win)
        bsb = jnp.broadcast_to(gmlp_bs[l][:, :, None], (NG, GB, GB))
        ya = _branch_a(z, vec(gmlp_ln_g, l), vec(gmlp_ln_b, l), gmlp_ws[l], bsb)
        yb, hs = _branch_b(z, Cw, vec(conv_b, l), lru_wr[l], vec(lru_br, l), lru_wi[l], vec(lru_bi, l), vec(lru_lambda, l))
        kv = _kv(mem2, mg, Wkv)
        yc = _branch_c(z, kv)
        pg = vec(post_norm_g, l)
        if l + 1 < L:
            mid = _gather_mid(l + 1, started, yc)
            if l + 2 < L:
                started = _gather_start(l + 2, placed[l + 2], mid[-1])
                pg = pg + started[-1][0, 0]
        pa, pb, pc, mgd, o, xn = _merge_out(ya, yb, yc, z, Wp, xs[l], pg)
        if l + 1 < L:
            W[l + 1] = _gather_end(l + 1, mid, xn)
        xs.append(xn)
        saved.append((z, h, ya, yb, yc, hs, kv, pa, pb, pc, mgd, o, bsb))

    loss11, dxn = _loss_head(xs[L], loss_target[0])
    loss = lax.psum(loss11[0, 0], ("x", "y", "c"))

    big = dict(w_in=(w_in, m_w_in, v_w_in), w_kv=(w_kv, m_w_kv, v_w_kv), w_pa=(w_pa, m_w_pa, v_w_pa),
               w_pb=(w_pb, m_w_pb, v_w_pb), w_pc=(w_pc, m_w_pc, v_w_pc), w_out=(w_out, m_w_out, v_w_out))
    out = {n: None for n in big}
    kin, nsh, nkv, rp = w_in.shape[1], w_in.shape[2], w_kv.shape[2], w_pa.shape[1]

    def finish_layer(l, a2a, after):
        thru = _a2a_wait(l, a2a, after)
        lin, lkv, lp = thru[3:6]
        g_in, g_kv, g_p = _sum_share(pos2, (lin, lkv, lp.reshape(NDEV - 1, 2 * rp, D)), thru[0:3])
        g_p = g_p.reshape(4, rp, D)
        out["w_in"] = _adamw_layer(l, *big["w_in"], g_in, out["w_in"])
        out["w_kv"] = _adamw_layer(l, *big["w_kv"], g_kv, out["w_kv"])
        for k, n in enumerate(("w_pa", "w_pb", "w_pc", "w_out")):
            out[n] = _adamw_layer(l, *big[n], g_p, out[n], which=k)

    small = {n: [None] * L for n in _SMALL}
    dconv_w = [None] * L
    dg_mem = jnp.zeros((1, D), F32)
    pending = None
    sent = []
    for l in reversed(range(L)):
        Win, Wkv, Wp, Cw = W[l]
        z, h, ya, yb, yc, hs, kv, pa, pb, pc, mgd, o, bsb = saved[l]
        pg = vec(post_norm_g, l) if pending is None else vec(post_norm_g, l) + pending[1][-1][0, 0]
        do, dpa, dpb, dpc, dya, dyb, dyc, dz, dgpost = _out_bwd(dxn, o, pg, Wp, z, pa, pb, pc)
        dz, dws, dbs, dlg, dlb = _branch_a_bwd(z, dya, vec(gmlp_ln_g, l), vec(gmlp_ln_b, l), gmlp_ws[l], bsb, dz)
        dz, dcw, dcb, dwr, dbr, dwi, dbi, dlam = _branch_b_bwd(
            z, hs, dyb, Cw, vec(conv_b, l), lru_wr[l], vec(lru_br, l), lru_wi[l], vec(lru_bi, l), vec(lru_lambda, l), dz)
        dz, dkv = _branch_c_bwd(z, kv, dyc, dz)
        g_in = _mm_dwin(h, dz)
        g_kv, dg_mem = _mem_bwd(mem2, mg, Wkv, dkv, dg_mem)
        grads = (g_in, g_kv, _mm_tn4((ya, yb, yc, mgd), (dpa, dpb, dpc, do)))
        lands = (lax.empty((NDEV - 1, kin // 2, nsh), BF16), lax.empty((NDEV - 1, kin // 2, nkv), BF16),
                 lax.empty((NDEV - 1, 4, rp // 2, D), BF16))
        a2a = _a2a_start(l, grads, lands)
        dx, dgpre = _mm_dh(dz, Win, xs[l], dxn, vec(pre_norm_g, l) + a2a[-1][0, 0])
        pending = (l, a2a)
        sent.append(pending)
        for n, val in (("pre_norm_g", dgpre), ("post_norm_g", dgpost), ("gmlp_ln_g", dlg), ("gmlp_ln_b", dlb), ("gmlp_ws", dws),
                       ("gmlp_bs", dbs), ("conv_b", dcb), ("lru_wr", dwr), ("lru_br", dbr), ("lru_wi", dwi), ("lru_bi", dbi),
                       ("lru_lambda", dlam)):
            small[n][l] = val
        dconv_w[l] = dcw[0:4]
        dxn = dx
    grad_x = dxn.reshape(1, S, D)

    for l, a2a in sent[:-1]:
        finish_layer(l, a2a, dxn)

    parts = {n: jnp.stack(small[n]) for n in _SMALL if n != "mem_norm_g"}
    parts["mem_norm_g"] = dg_mem
    gsum = _allreduce_small(_pack_small(parts, jnp.stack(dconv_w)), out["w_out"][1])
    given = dict(mem_norm_g=(mem_norm_g, m_mem_norm_g, v_mem_norm_g), pre_norm_g=(pre_norm_g, m_pre_norm_g, v_pre_norm_g),
                 post_norm_g=(post_norm_g, m_post_norm_g, v_post_norm_g), gmlp_ln_g=(gmlp_ln_g, m_gmlp_ln_g, v_gmlp_ln_g),
                 gmlp_ln_b=(gmlp_ln_b, m_gmlp_ln_b, v_gmlp_ln_b), gmlp_ws=(gmlp_ws, m_gmlp_ws, v_gmlp_ws),
                 gmlp_bs=(gmlp_bs, m_gmlp_bs, v_gmlp_bs), conv_b=(conv_b, m_conv_b, v_conv_b), lru_wr=(lru_wr, m_lru_wr, v_lru_wr),
                 lru_br=(lru_br, m_lru_br, v_lru_br), lru_wi=(lru_wi, m_lru_wi, v_lru_wi), lru_bi=(lru_bi, m_lru_bi, v_lru_bi),
                 lru_lambda=(lru_lambda, m_lru_lambda, v_lru_lambda))
    shapes = {n: given[n][0].shape for n in _SMALL}
    zero_cw = jnp.zeros((L, 4, D), F32)
    packs = [_pack_small({n: given[n][k] for n in _SMALL}, zero_cw) for k in range(3)]
    dsm, msm, vsm = _adamw_flat(packs[0], packs[1], packs[2], gsum, 2560)
    g_small, at = _unpack_small(gsum, shapes)
    d_small, _ = _unpack_small(dsm, shapes)
    m_small, _ = _unpack_small(msm, shapes)
    v_small, _ = _unpack_small(vsm, shapes)
    for n in _SMALL:
        out[n] = (g_small[n], d_small[n], m_small[n], v_small[n])
    g_cw = lax.dynamic_slice_in_dim(gsum[at:at + L * 4 * D // 128].reshape(L * 4, D), jpos * (D // 4), D // 4, axis=1)
    d_cw, m_cw, v_cw = _adamw_flat(conv_w.reshape(L * 4, D // 4), m_conv_w.reshape(L * 4, D // 4),
                                   v_conv_w.reshape(L * 4, D // 4), g_cw, L * 4)
    out["conv_w"] = tuple(a.reshape(L, 4, D // 4) for a in (g_cw, d_cw, m_cw, v_cw))

    finish_layer(pending[0], pending[1], dsm)

    order = ("mem_norm_g", "pre_norm_g", "post_norm_g", "w_in", "gmlp_ln_g", "gmlp_ln_b", "gmlp_ws", "gmlp_bs", "conv_w", "conv_b",
             "lru_wr", "lru_br", "lru_wi", "lru_bi", "lru_lambda", "w_kv", "w_pa", "w_pb", "w_pc", "w_out")
    return (loss, grad_x, *[out[n][0] for n in order], *[out[n][1] for n in order], *[out[n][2] for n in order],
            *[out[n][3] for n in order])
```

```python
import functools

import jax
import jax.numpy as jnp
from jax import lax
from jax.experimental import pallas as pl
from jax.experimental.pallas import tpu as pltpu

F32 = jnp.float32
BF16 = jnp.bfloat16
SDS = jax.ShapeDtypeStruct
MESH = pl.DeviceIdType.MESH

D = 1024
NIN = 10 * D
MEM = 256
GB = 128
NG = 8
NH = 4
HD = D // NH
EPS = 1e-6
LRU_C = 8.0
ADAM_LR, ADAM_B1, ADAM_B2, ADAM_EPS, ADAM_WD, ADAM_STEP = 0.001, 0.9, 0.999, 1e-08, 0.01, 10
NDEV = 8
SMALL_ROWS = 12800

_CALL_KW = {}
HBM = pl.BlockSpec(memory_space=pltpu.HBM)
VMEM = pl.BlockSpec(memory_space=pltpu.VMEM)
SEM = pl.BlockSpec(memory_space=pltpu.SEMAPHORE)
ANY = pl.BlockSpec(memory_space=pl.ANY)
TOKEN = SDS((8, 128), F32)


def _pcall(body, *, name, in_specs, out_specs, out_shape, grid=None, scratch=(), vmem_mb=48, aliases=None, effect=False,
           prefetch=0):
    kw = dict(_CALL_KW)
    if aliases:
        kw["input_output_aliases"] = aliases
    params = dict(vmem_limit_bytes=vmem_mb << 20)
    if grid is not None:
        params["dimension_semantics"] = ("arbitrary",) * len(grid)
    if effect:
        params["has_side_effects"] = pltpu.SideEffectType.DATAFLOW_SIDE_EFFECTING
    if prefetch:
        kw["grid_spec"] = pltpu.PrefetchScalarGridSpec(num_scalar_prefetch=prefetch, grid=grid, in_specs=in_specs,
                                                       out_specs=out_specs, scratch_shapes=list(scratch))
    else:
        kw.update(in_specs=in_specs, out_specs=out_specs, scratch_shapes=list(scratch))
        if grid is not None:
            kw["grid"] = grid
    return pl.pallas_call(body, name=name, out_shape=out_shape, compiler_params=pltpu.CompilerParams(**params), **kw)


def _full(shape):
    nd = len(shape)
    return pl.BlockSpec(shape, lambda *_: (0,) * nd)


def _dot(a, b):
    return jnp.dot(a, b, preferred_element_type=F32)


def _dot_nt(a, b):
    return lax.dot_general(a, b, (((1,), (1,)), ((), ())), preferred_element_type=F32)


def _dot_tn(a, b):
    return lax.dot_general(a, b, (((0,), (0,)), ((), ())), preferred_element_type=F32)


def _rowsum(a):
    return jnp.sum(a, axis=0, keepdims=True)


def _sigmoid(x):
    return 0.5 * jnp.tanh(0.5 * x) + 0.5


def _silu_parts(g):
    s = _sigmoid(g)
    return g * s, s * (1.0 + g * (1.0 - s))


def _rms_scale(x):
    return lax.rsqrt(jnp.mean(x * x, axis=-1, keepdims=True) + EPS)


def _dz_col(k):
    return jnp.where(k < 3, k, jnp.where(k < 6, k + 4, k - 3))


def _coords():
    return lax.axis_index("x"), lax.axis_index("y"), lax.axis_index("c")


def _other_chips(x, y):
    return [(1 - x, y), (x, 1 - y), (1 - x, 1 - y)]


def _peer(x, y, c, mask):
    return (1 - x if mask & 4 else x, 1 - y if mask & 2 else y, 1 - c if mask & 1 else c)


def _remote(src, dst, ssem, rsem, k, to):
    return pltpu.make_async_remote_copy(src_ref=src, dst_ref=dst, send_sem=ssem.at[k], recv_sem=rsem.at[k], device_id=to,
                                        device_id_type=MESH)


def _w_half(a, ref, jj, cc):
    if a == 2:
        rp = ref.shape[1] // 4
        return ref.at[:, pl.ds(jj * rp + cc * (rp // 2), rp // 2), :]
    kin, nsh = ref.shape[0], ref.shape[1] // 4
    return ref.at[pl.ds(cc * (kin // 2), kin // 2), pl.ds(jj * nsh, nsh)]


def _cw_block(ref, jj):
    return ref.at[:, pl.ds(jj * (D // 4), D // 4)]


def _cast_place(l, pos, w_in, w_kv, w_pa, w_pb, w_pc, w_out, cw8):
    kin, nsh = w_in.shape[1], w_in.shape[2]
    nkv, rp = w_kv.shape[2], w_pa.shape[1]
    half = kin // 2

    def body(pos_r, win, wkv, pa, pb, pc, po, cw, Win, Wkv, Wp, Cw):
        Win[...] = win[...].astype(BF16)
        Wkv[...] = wkv[...].astype(BF16)

        @pl.when(pl.program_id(0) == 0)
        def _():
            for k, r in enumerate((pa, pb, pc, po)):
                Wp[k] = r[...].astype(BF16)
            Cw[...] = cw[...]

    proj = pl.BlockSpec((None, rp, D), lambda i, p: (l, 0, 0))
    return _pcall(
        body, name="cast_place", grid=(2,), prefetch=1,
        in_specs=[pl.BlockSpec((None, half, nsh), lambda i, p: (l, i, 0)), pl.BlockSpec((None, half, nkv), lambda i, p: (l, i, 0)),
                  proj, proj, proj, proj, pl.BlockSpec((None, 8, D // 4), lambda i, p: (l, 0, 0))],
        out_specs=[pl.BlockSpec((half, nsh), lambda i, p: (i, p[0])), pl.BlockSpec((half, nkv), lambda i, p: (i, p[0])),
                   pl.BlockSpec((4, rp, D), lambda i, p: (0, p[0], 0)), pl.BlockSpec((8, D // 4), lambda i, p: (0, p[0]))],
        out_shape=[SDS((kin, 4 * nsh), BF16), SDS((kin, 4 * nkv), BF16), SDS((4, 4 * rp, D), BF16), SDS((8, D), F32)],
    )(pos, w_in, w_kv, w_pa, w_pb, w_pc, w_out, cw8)


def _hbm_like(bufs):
    return [pltpu.HBM(b.shape, b.dtype) for b in bufs]


def _gather_start(l, bufs, after=None):
    extra = [] if after is None else [after]

    def body(win, wkv, wp, cw, *rest):
        ssem, rsem, token = rest[len(extra)], rest[len(extra) + 1], rest[-1]
        x, y, c = _coords()
        j = 2 * x + y
        refs = (win, wkv, wp)
        for k, chip in enumerate(_other_chips(x, y)):
            to = (chip[0], chip[1], c)
            for a in range(3):
                half = _w_half(a, refs[a], j, c)
                _remote(half, half, ssem, rsem, a * 3 + k, to).start()
            mine = _cw_block(cw, j)
            _remote(mine, mine, ssem, rsem, 9 + k, to).start()
        token[...] = jnp.zeros((8, 128), F32)

    return _pcall(
        body, name=f"gather_start_{l}", in_specs=[HBM] * 4 + [ANY] * len(extra), out_specs=[SEM, SEM] + [HBM] * 4 + [VMEM],
        out_shape=[pltpu.SemaphoreType.DMA((12,)), pltpu.SemaphoreType.DMA((12,))] + _hbm_like(bufs) + [TOKEN],
        aliases={0: 2, 1: 3, 2: 4, 3: 5}, effect=True,
    )(*[pltpu.with_memory_space_constraint(b, pltpu.HBM) for b in bufs], *extra)


def _gather_mid(l, started, after):
    ssem, rsem, b0, b1, b2, b3, _ = started

    def body(win, wkv, wp, cw, ssem, rsem, after_r, ssem2, rsem2, o0, o1, o2, o3, token):
        x, y, c = _coords()
        j = 2 * x + y
        me, sib = (x, y, c), (x, y, 1 - c)
        token[...] = jnp.zeros((8, 128), F32)
        refs = (win, wkv, wp)
        chips = _other_chips(x, y)
        for k, chip in enumerate(chips):
            jk = 2 * chip[0] + chip[1]
            for a in range(3):
                got = _w_half(a, refs[a], jk, c)
                _remote(got, got, ssem, rsem, a * 3 + k, me).wait_recv()
            got = _cw_block(cw, jk)
            _remote(got, got, ssem, rsem, 9 + k, me).wait_recv()
        for k in range(3):
            for a in range(3):
                half = _w_half(a, refs[a], j, c)
                _remote(half, half, ssem, rsem, a * 3 + k, me).wait_send()
            mine = _cw_block(cw, j)
            _remote(mine, mine, ssem, rsem, 9 + k, me).wait_send()
        for k, chip in enumerate(chips):
            jk = 2 * chip[0] + chip[1]
            for a in range(3):
                got = _w_half(a, refs[a], jk, c)
                _remote(got, got, ssem2, rsem2, a * 3 + k, sib).start()

    bufs = (b0, b1, b2, b3)
    return _pcall(
        body, name=f"gather_mid_{l}", in_specs=[HBM] * 4 + [SEM, SEM, ANY], out_specs=[SEM, SEM] + [HBM] * 4 + [VMEM],
        out_shape=[pltpu.SemaphoreType.DMA((9,)), pltpu.SemaphoreType.DMA((9,))] + _hbm_like(bufs) + [TOKEN],
        aliases={0: 2, 1: 3, 2: 4, 3: 5}, effect=True,
    )(b0, b1, b2, b3, ssem, rsem, after)


def _gather_end(l, mid, after):
    ssem2, rsem2, b0, b1, b2, b3, _ = mid

    def body(win, wkv, wp, cw, ssem2, rsem2, after_r, o0, o1, o2, o3):
        x, y, c = _coords()
        me = (x, y, c)
        refs = (win, wkv, wp)
        for k, chip in enumerate(_other_chips(x, y)):
            jk = 2 * chip[0] + chip[1]
            for a in range(3):
                got = _w_half(a, refs[a], jk, 1 - c)
                _remote(got, got, ssem2, rsem2, a * 3 + k, me).wait_recv()
                sent = _w_half(a, refs[a], jk, c)
                _remote(sent, sent, ssem2, rsem2, a * 3 + k, me).wait_send()

    bufs = (b0, b1, b2, b3)
    return _pcall(
        body, name=f"gather_end_{l}", in_specs=[HBM] * 4 + [SEM, SEM, ANY], out_specs=[HBM] * 4, out_shape=_hbm_like(bufs),
        aliases={0: 0, 1: 1, 2: 2, 3: 3}, effect=True,
    )(b0, b1, b2, b3, ssem2, rsem2, after)


def _g_piece(a, ref, jd, dc):
    if a == 2:
        rp = ref.shape[1] // 4
        return ref.at[:, pl.ds(jd * rp + dc * (rp // 2), rp // 2), :]
    kin, nsh = ref.shape[0], ref.shape[1] // 4
    return ref.at[pl.ds(dc * (kin // 2), kin // 2), pl.ds(jd * nsh, nsh)]


def _land_slot(a, lands, s):
    return lands[a].at[s]


def _a2a_start(l, grads, lands):
    def body(*refs):
        g, ld, ssem, rsem, token = refs[0:3], refs[3:6], refs[6], refs[7], refs[-1]
        x, y, c = _coords()
        for mask in range(1, NDEV):
            p = _peer(x, y, c, mask)
            for a in range(3):
                _remote(_g_piece(a, g[a], 2 * p[0] + p[1], p[2]), _land_slot(a, ld, mask - 1), ssem, rsem, a * 7 + mask - 1, p).start()
        token[...] = jnp.zeros((8, 128), F32)

    bufs = tuple(grads) + tuple(lands)
    return _pcall(
        body, name=f"a2a_start_{l}", in_specs=[HBM] * 6, out_specs=[SEM, SEM] + [HBM] * 6 + [VMEM],
        out_shape=[pltpu.SemaphoreType.DMA((21,)), pltpu.SemaphoreType.DMA((21,))] + _hbm_like(bufs) + [TOKEN],
        aliases={i: 2 + i for i in range(6)}, effect=True,
    )(*[pltpu.with_memory_space_constraint(b, pltpu.HBM) for b in bufs])


def _a2a_wait(l, started, after):
    ssem, rsem = started[0], started[1]
    bufs = tuple(started[2:8])

    def body(*refs):
        g, ld, ssem, rsem = refs[0:3], refs[3:6], refs[6], refs[7]
        x, y, c = _coords()
        me = (x, y, c)
        for mask in range(1, NDEV):
            for a in range(3):
                got = _land_slot(a, ld, mask - 1)
                _remote(got, got, ssem, rsem, a * 7 + mask - 1, me).wait_recv()
        for mask in range(1, NDEV):
            p = _peer(x, y, c, mask)
            for a in range(3):
                sent = _g_piece(a, g[a], 2 * p[0] + p[1], p[2])
                _remote(sent, sent, ssem, rsem, a * 7 + mask - 1, me).wait_send()

    return _pcall(
        body, name=f"a2a_wait_{l}", in_specs=[HBM] * 6 + [SEM, SEM, ANY], out_specs=[HBM] * 6, out_shape=_hbm_like(bufs),
        aliases={i: i for i in range(6)}, effect=True,
    )(*bufs, ssem, rsem, after)


def _sum_share(pos, lands, grads):
    rows, n = 128, 4
    widths = [ld.shape[2] for ld in lands]

    def body(pos_r, l0, w0, l1, w1, l2, w2, g0, g1, g2, b0, b1, b2, lsem, ssem, rsem):
        i = pl.program_id(0)
        x, y, c = _coords()
        sib = (x, y, 1 - c)
        ld, ow, gs, bufs = (l0, l1, l2), (w0, w1, w2), (g0, g1, g2), (b0, b1, b2)

        def dst(a, step):
            row = step * (2 * rows) + c * rows if a == 2 else c * (n * rows) + step * rows
            return gs[a].at[pl.ds(row, rows), :]

        def copies(a, step, sl):
            src = bufs[a].at[sl]
            lc = pltpu.make_async_copy(src, dst(a, step), lsem.at[a, sl])
            rc = pltpu.make_async_remote_copy(src_ref=src, dst_ref=dst(a, step), send_sem=ssem.at[a, sl], recv_sem=rsem.at[a],
                                              device_id=sib, device_id_type=MESH)
            return lc, rc

        def drain(a, step, sl):
            lc, rc = copies(a, step, sl)
            lc.wait()
            rc.wait_send()

        slot = i % 2

        @pl.when(i >= 2)
        def _():
            for a in range(3):
                drain(a, i - 2, slot)

        for a in range(3):
            acc = ow[a][...].astype(F32)
            for k in range(NDEV - 1):
                acc = acc + ld[a][k].astype(F32)
            bufs[a][slot] = acc
            lc, rc = copies(a, i, slot)
            lc.start()
            rc.start()

        @pl.when(i == n - 1)
        def _():
            for a in range(3):
                drain(a, n - 2, (n - 2) % 2)
                drain(a, n - 1, (n - 1) % 2)
                whole = gs[a].at[pl.ds(0, n * rows), :]
                pltpu.make_async_remote_copy(src_ref=whole, dst_ref=whole, send_sem=ssem.at[a, 0], recv_sem=rsem.at[a],
                                             device_id=(x, y, c), device_id_type=MESH).wait_recv()

    land = lambda w: pl.BlockSpec((NDEV - 1, rows, w), lambda i, p: (0, i, 0))
    in_specs = [land(widths[0]), pl.BlockSpec((rows, widths[0]), lambda i, p: (p[1] * n + i, p[0])),
                land(widths[1]), pl.BlockSpec((rows, widths[1]), lambda i, p: (p[1] * n + i, p[0])),
                land(widths[2]), pl.BlockSpec((None, rows, widths[2]), lambda i, p: (i, 2 * p[0] + p[1], 0))]
    args = [t for pair in zip(lands, grads) for t in pair]
    return _pcall(
        body, name="sum_share", grid=(n,), prefetch=1, in_specs=in_specs, out_specs=[HBM] * 3,
        out_shape=[SDS((2 * n * rows, w), F32) for w in widths],
        scratch=[pltpu.VMEM((2, rows, w), F32) for w in widths]
        + [pltpu.SemaphoreType.DMA((3, 2)), pltpu.SemaphoreType.DMA((3, 2)), pltpu.SemaphoreType.DMA((3,))],
    )(pos, *args)


def _dev_index(p):
    return 4 * p[0] + 2 * p[1] + p[2]


def _small_rows(ref, d):
    r8 = ref.shape[0] // NDEV
    return ref.at[pl.ds(d * r8, r8), :]


def _rs_start(pack, land):
    def body(p_ref, ld, ssem, rsem, o0, o1, token):
        x, y, c = _coords()
        for mask in range(1, NDEV):
            p = _peer(x, y, c, mask)
            _remote(_small_rows(p_ref, _dev_index(p)), ld.at[mask - 1], ssem, rsem, mask - 1, p).start()
        token[...] = jnp.zeros((8, 128), F32)

    bufs = (pack, land)
    return _pcall(
        body, name="rs_start", in_specs=[HBM] * 2, out_specs=[SEM, SEM, HBM, HBM, VMEM],
        out_shape=[pltpu.SemaphoreType.DMA((NDEV - 1,)), pltpu.SemaphoreType.DMA((NDEV - 1,))] + _hbm_like(bufs) + [TOKEN],
        aliases={0: 2, 1: 3}, effect=True,
    )(*[pltpu.with_memory_space_constraint(b, pltpu.HBM) for b in bufs])


def _rs_wait(started, after):
    ssem, rsem, pack, land, _ = started

    def body(p_ref, ld, ssem, rsem, after_r, o0, o1):
        x, y, c = _coords()
        for mask in range(1, NDEV):
            got = ld.at[mask - 1]
            _remote(got, got, ssem, rsem, mask - 1, (x, y, c)).wait_recv()
        for mask in range(1, NDEV):
            sent = _small_rows(p_ref, _dev_index(_peer(x, y, c, mask)))
            _remote(sent, sent, ssem, rsem, mask - 1, (x, y, c)).wait_send()

    return _pcall(body, name="rs_wait", in_specs=[HBM, HBM, SEM, SEM, ANY], out_specs=[HBM, HBM],
                  out_shape=_hbm_like((pack, land)), aliases={0: 0, 1: 1}, effect=True)(pack, land, ssem, rsem, after)


def _small_sum(me1, pack, land):
    R = pack.shape[0]
    r8 = R // NDEV

    def body(me_r, p_ref, ld, full):
        acc = p_ref[...]
        for k in range(NDEV - 1):
            acc = acc + ld[k]
        full[...] = acc

    own = pl.BlockSpec((r8, 128), lambda i, m: (m[0], 0))
    return _pcall(body, name="small_sum", grid=(1,), prefetch=1,
                  in_specs=[own, pl.BlockSpec((NDEV - 1, r8, 128), lambda i, m: (0, 0, 0))], out_specs=own,
                  out_shape=SDS((R, 128), F32), vmem_mb=32)(me1, pack, land)


def _ag_start(full):
    def body(f_ref, ssem, rsem, o0, token):
        x, y, c = _coords()
        mine = _small_rows(f_ref, _dev_index((x, y, c)))
        for mask in range(1, NDEV):
            _remote(mine, mine, ssem, rsem, mask - 1, _peer(x, y, c, mask)).start()
        token[...] = jnp.zeros((8, 128), F32)

    return _pcall(
        body, name="ag_start", in_specs=[HBM], out_specs=[SEM, SEM, HBM, VMEM],
        out_shape=[pltpu.SemaphoreType.DMA((NDEV - 1,)), pltpu.SemaphoreType.DMA((NDEV - 1,))] + _hbm_like((full,)) + [TOKEN],
        aliases={0: 2}, effect=True,
    )(pltpu.with_memory_space_constraint(full, pltpu.HBM))


def _ag_wait(started, after):
    ssem, rsem, full, _ = started

    def body(f_ref, ssem, rsem, after_r, o0):
        x, y, c = _coords()
        mine = _small_rows(f_ref, _dev_index((x, y, c)))
        for mask in range(1, NDEV):
            got = _small_rows(f_ref, _dev_index(_peer(x, y, c, mask)))
            _remote(got, got, ssem, rsem, mask - 1, (x, y, c)).wait_recv()
            _remote(mine, mine, ssem, rsem, mask - 1, (x, y, c)).wait_send()

    return _pcall(body, name="ag_wait", in_specs=[HBM, SEM, SEM, ANY], out_specs=[HBM], out_shape=_hbm_like((full,)),
                  aliases={0: 0}, effect=True)(full, ssem, rsem, after)[0]


def _mm_in(x, g, w):
    S = x.shape[0]
    tm, tn = min(1024, S), 1280

    def body(x_ref, g_ref, w_ref, z_ref, h_ref, hs):
        @pl.when(pl.program_id(1) == 0)
        def _():
            xv = x_ref[...]
            hb = (xv * _rms_scale(xv) * g_ref[...]).astype(BF16)
            hs[...] = hb
            h_ref[...] = hb

        z_ref[...] = _dot(hs[...], w_ref[...]).astype(BF16)

    return _pcall(
        body, name="mm_in", grid=(S // tm, NIN // tn),
        in_specs=[pl.BlockSpec((tm, D), lambda i, j: (i, 0)), _full((1, D)), pl.BlockSpec((D, tn), lambda i, j: (0, j))],
        out_specs=[pl.BlockSpec((tm, tn), lambda i, j: (i, j)), pl.BlockSpec((tm, D), lambda i, j: (i, 0))],
        out_shape=[SDS((S, NIN), BF16), SDS((S, D), BF16)], scratch=[pltpu.VMEM((tm, D), BF16)],
    )(x, g, w)


def _chunk_mask():
    ri = lax.broadcasted_iota(jnp.int32, (GB, GB), 0)
    ci = lax.broadcasted_iota(jnp.int32, (GB, GB), 1)
    return (ri >= 64) | (ci < 64)


def _layernorm_parts(v):
    mu = jnp.mean(v, axis=-1, keepdims=True)
    d = v - mu
    rs = lax.rsqrt(jnp.mean(d * d, axis=-1, keepdims=True) + EPS)
    return d * rs, rs


def _branch_a(z, lg, lb, ws, bsb):
    S = z.shape[0]
    T = min(512, S)

    def body(zu, zv, zg, lg_r, lb_r, ws_r, bs_r, ya):
        vhat, _ = _layernorm_parts(zv[...].astype(F32))
        vnb = (vhat * lg_r[...] + lb_r[...]).astype(BF16)
        sil, _ = _silu_parts(zg[...].astype(F32))
        t = zu[...].astype(F32) * sil
        mask = _chunk_mask()
        for g in range(NG):
            wg = jnp.where(mask, ws_r[g], 0.0).astype(BF16)
            cs = slice(g * GB, (g + 1) * GB)
            for n in range(T // GB):
                rs = slice(n * GB, (n + 1) * GB)
                sv = _dot(wg, vnb[rs, cs]) + bs_r[g]
                ya[rs, cs] = (t[rs, cs] * sv).astype(BF16)

    zs = lambda k: pl.BlockSpec((T, D), lambda i: (i, k))
    return _pcall(
        body, name="branch_a", grid=(S // T,),
        in_specs=[zs(0), zs(1), zs(2), _full((1, D)), _full((1, D)), _full((NG, GB, GB)), _full((NG, GB, GB))],
        out_specs=pl.BlockSpec((T, D), lambda i: (i, 0)), out_shape=SDS((S, D), BF16),
    )(z, z, z, lg, lb, ws, bsb)


def _softplus_neg(lam):
    e = jnp.exp(-jnp.abs(lam))
    l1p = jnp.where(e < 1e-2, e * (1.0 - e * (0.5 - e * (1.0 / 3.0))), jnp.log(1.0 + e))
    return jnp.maximum(-lam, 0.0) + l1p


CH = 16


def _ck(c, off=0):
    return pl.ds(c * CH + off, CH)


def _half_sum(v):
    return v[0:8, :] + v[8:16, :]


def _lru_conv(xpad, c, cw, cb):
    xk = [xpad[_ck(c, 5 + k), :] for k in range(4)]
    return xk, cb + (((xk[0] * cw[0:1] + xk[1] * cw[1:2]) + xk[2] * cw[2:3]) + xk[3] * cw[3:4])


def _lru_gate_matmuls(xcb_s, wr_r, wi_r, pr_s, pi_s):
    for h in range(NG):
        cs = slice(h * GB, (h + 1) * GB)
        pr_s[:, cs] = _dot(xcb_s[:, cs], wr_r[h].astype(BF16))
        pi_s[:, cs] = _dot(xcb_s[:, cs], wi_r[h].astype(BF16))


def _lru_gates(pr, pi, br, bi, sp8):
    r = jax.nn.sigmoid(pr + br)
    ig = _sigmoid(pi + bi)
    la = sp8 * r
    a = jnp.exp(la)
    a2 = a * a
    mult = jnp.sqrt(-jnp.tanh(la) * (a2 + 1.0))
    return r, ig, a, a2, mult


def _tile_rows():
    return lax.broadcasted_iota(jnp.int32, (8, D), 0)


def _scan_forward(a_s, u_s, h_s, hcar, T):
    row = _tile_rows()

    def tile(i, hp):
        o = pl.multiple_of(i * 8, 8)
        A = a_s[pl.ds(o, 8), :]
        U = u_s[pl.ds(o, 8), :]
        for s in (1, 2, 4):
            m = row >= s
            U = jnp.where(m, U + A * pltpu.roll(U, s, 0), U)
            A = jnp.where(m, A * pltpu.roll(A, s, 0), A)
        H = U + A * hp
        h_s[pl.ds(o, 8), :] = H
        return jnp.broadcast_to(H[7:8, :], (8, D))

    hcar[...] = lax.fori_loop(0, T // 8, tile, hcar[...])


def _scan_reverse(b_s, d_s, l_s, lcar, T):
    row = _tile_rows()
    n = T // 8

    def tile(i, lp):
        o = pl.multiple_of((n - 1 - i) * 8, 8)
        B = b_s[pl.ds(o, 8), :]
        U = d_s[pl.ds(o, 8), :]
        for s in (1, 2, 4):
            m = row < 8 - s
            U = jnp.where(m, U + B * pltpu.roll(U, 8 - s, 0), U)
            B = jnp.where(m, B * pltpu.roll(B, 8 - s, 0), B)
        Lm = U + B * lp
        l_s[pl.ds(o, 8), :] = Lm
        return jnp.broadcast_to(Lm[0:1, :], (8, D))

    lcar[...] = lax.fori_loop(0, n, tile, lcar[...])


def _branch_b(z, cw, cb, wr, br, wi, bi, lam):
    S = z.shape[0]
    T = min(256, S)

    def body(zxb, zgb, cw_r, cb_r, wr_r, br_r, wi_r, bi_r, lam_r, yb, hs_o, xpad, a_s, u_s, hcar):
        @pl.when(pl.program_id(0) == 0)
        def _():
            xpad[pl.ds(0, 8), :] = jnp.zeros((8, D), F32)
            hcar[...] = jnp.zeros((8, D), F32)

        cw = cw_r[...]
        xpad[pl.ds(8, T), :] = zxb[...].astype(F32)
        xk = [xpad[pl.ds(5 + k, T), :] for k in range(4)]
        xc = cb_r[...] + (((xk[0] * cw[0:1] + xk[1] * cw[1:2]) + xk[2] * cw[2:3]) + xk[3] * cw[3:4])
        xcb = xc.astype(BF16)
        pr, pi = [], []
        for h in range(NG):
            cs = slice(h * GB, (h + 1) * GB)
            pr.append(_dot(xcb[:, cs], wr_r[h].astype(BF16)))
            pi.append(_dot(xcb[:, cs], wi_r[h].astype(BF16)))
        _, ig, a, _, mult = _lru_gates(jnp.concatenate(pr, axis=1), jnp.concatenate(pi, axis=1), br_r[...], bi_r[...],
                                       -LRU_C * _softplus_neg(lam_r[...]))
        a_s[...] = a
        u_s[...] = mult * (ig * xc)
        _scan_forward(a_s, u_s, hs_o, hcar, T)
        xpad[pl.ds(0, 8), :] = xpad[pl.ds(T, 8), :]
        sil, _ = _silu_parts(zgb[...].astype(F32))
        yb[...] = (hs_o[...] * sil).astype(BF16)

    zs = lambda k: pl.BlockSpec((T, D), lambda i: (i, k))
    row = pl.BlockSpec((T, D), lambda i: (i, 0))
    return _pcall(
        body, name="branch_b", grid=(S // T,),
        in_specs=[zs(3), zs(4), _full((8, D)), _full((1, D)), _full((NG, GB, GB)), _full((1, D)), _full((NG, GB, GB)),
                  _full((1, D)), _full((1, D))],
        out_specs=[row, row], out_shape=[SDS((S, D), BF16), SDS((S, D), F32)],
        scratch=[pltpu.VMEM((T + 8, D), F32), pltpu.VMEM((T, D), F32), pltpu.VMEM((T, D), F32), pltpu.VMEM((8, D), F32)],
    )(z, z, cw, cb, wr, br, wi, bi, lam)


def _kv(mem, g, wkv):
    def body(m_ref, g_ref, w_ref, kv_ref):
        m = m_ref[...]
        mn = (m * _rms_scale(m) * g_ref[...]).astype(BF16)
        kv_ref[...] = _dot(mn, w_ref[...]).astype(BF16)

    return _pcall(body, name="mem_kv", in_specs=[VMEM] * 3, out_specs=VMEM, out_shape=SDS((MEM, 2 * D), BF16),
                  vmem_mb=32)(mem, g, wkv)


def _softmax_rows(s):
    e = jnp.exp(s - jnp.max(s, axis=-1, keepdims=True))
    return e / jnp.sum(e, axis=-1, keepdims=True)


def _branch_c(z, kv):
    S = z.shape[0]
    T = min(512, S)

    def body(zq, zg, kv_r, yc):
        sil, _ = _silu_parts(zg[...].astype(F32))
        for h in range(NH):
            cs = slice(h * HD, (h + 1) * HD)
            p = _softmax_rows(_dot_nt(zq[:, cs], kv_r[:, cs]) * (HD ** -0.5))
            att = _dot(p.astype(BF16), kv_r[:, D + h * HD:D + (h + 1) * HD])
            yc[:, cs] = (att * sil[:, cs]).astype(BF16)

    zs = lambda k: pl.BlockSpec((T, D), lambda i: (i, k))
    return _pcall(body, name="branch_c", grid=(S // T,), in_specs=[zs(5), zs(6), _full((MEM, 2 * D))],
                  out_specs=pl.BlockSpec((T, D), lambda i: (i, 0)), out_shape=SDS((S, D), BF16))(z, z, kv)


def _merge_out(ya, yb, yc, z, wp, x, pg):
    S = x.shape[0]
    T = min(256, S)

    def body(ya_r, yb_r, yc_r, m0, m1, m2, wp_r, x_r, pg_r, pa_o, pb_o, pc_o, mg_o, o_o, xn_o):
        merged = None
        for y_r, ml, p_o, k in ((ya_r, m0, pa_o, 0), (yb_r, m1, pb_o, 1), (yc_r, m2, pc_o, 2)):
            p = _dot(y_r[...], wp_r[k])
            p_o[...] = p.astype(BF16)
            t = _sigmoid(ml[...].astype(F32)) * p
            merged = t if merged is None else merged + t
        mb = merged.astype(BF16)
        mg_o[...] = mb
        o = _dot(mb, wp_r[3])
        o_o[...] = o.astype(BF16)
        xn_o[...] = x_r[...] + o * _rms_scale(o) * pg_r[...]

    row = pl.BlockSpec((T, D), lambda i: (i, 0))
    zs = lambda k: pl.BlockSpec((T, D), lambda i: (i, k))
    return _pcall(
        body, name="merge_out", grid=(S // T,),
        in_specs=[row, row, row, zs(7), zs(8), zs(9), _full((4, D, D)), row, _full((1, D))],
        out_specs=[row] * 6, out_shape=[SDS((S, D), BF16)] * 5 + [SDS((S, D), F32)], vmem_mb=56,
    )(ya, yb, yc, z, z, z, wp, x, pg)


def _loss_head(y, t):
    S = y.shape[0]
    T = min(512, S)

    def body(y_r, t_r, loss_o, dy_o):
        @pl.when(pl.program_id(0) == 0)
        def _():
            loss_o[...] = jnp.zeros((1, 1), F32)

        e = y_r[...] - t_r[...]
        dy_o[...] = e * (1.0 / D)
        loss_o[...] += 0.5 * _rowsum(jnp.sum(e * e, axis=1, keepdims=True) * (1.0 / D))

    row = pl.BlockSpec((T, D), lambda i: (i, 0))
    return _pcall(body, name="loss_head", grid=(S // T,), in_specs=[row, row], out_specs=[_full((1, 1)), row],
                  out_shape=[SDS((1, 1), F32), SDS((S, D), F32)])(y, t)


def _accumulate(first, ref, val):
    @pl.when(first)
    def _():
        ref[...] = val

    @pl.when(jnp.logical_not(first))
    def _():
        ref[...] += val


def _out_bwd(dxn, o, pg, wp, z, pa, pb, pc):
    S = dxn.shape[0]
    T = min(256, S)

    def body(dy_r, o_r, pg_r, wp_r, m0, m1, m2, pa_r, pb_r, pc_r, do_o, dpa_o, dpb_o, dpc_o, dya_o, dyb_o, dyc_o, dz_o, dg_o):
        dy = dy_r[...]
        o = o_r[...].astype(F32)
        r2 = _rms_scale(o)
        w = dy * pg_r[...]
        do = r2 * w - o * (r2 * r2 * r2) * jnp.mean(w * o, axis=-1, keepdims=True)
        _accumulate(pl.program_id(0) == 0, dg_o, _rowsum(dy * o * r2))
        dob = do.astype(BF16)
        do_o[...] = dob
        dm = _dot_nt(dob, wp_r[3])
        for k, (ml, p_r, dp_o, dy_o) in enumerate(((m0, pa_r, dpa_o, dya_o), (m1, pb_r, dpb_o, dyb_o), (m2, pc_r, dpc_o, dyc_o))):
            gk = _sigmoid(ml[...].astype(F32))
            dz_o[k] = (dm * p_r[...].astype(F32) * gk * (1.0 - gk)).astype(BF16)
            dpk = (gk * dm).astype(BF16)
            dp_o[...] = dpk
            dy_o[...] = _dot_nt(dpk, wp_r[k]).astype(BF16)

    row = pl.BlockSpec((T, D), lambda i: (i, 0))
    zs = lambda k: pl.BlockSpec((T, D), lambda i: (i, k))
    return _pcall(
        body, name="out_bwd", grid=(S // T,),
        in_specs=[row, row, _full((1, D)), _full((4, D, D)), zs(7), zs(8), zs(9), row, row, row],
        out_specs=[row] * 7 + [pl.BlockSpec((3, T, D), lambda i: (1, i, 0)), _full((1, D))],
        out_shape=[SDS((S, D), BF16)] * 7 + [SDS((10, S, D), BF16), SDS((1, D), F32)], vmem_mb=56,
    )(dxn, o, pg, wp, z, z, z, pa, pb, pc)


def _branch_a_bwd(z, dya, lg, lb, ws, bsb, dz):
    S = z.shape[0]
    T = min(512, S)
    nblk = S // T

    def body(zu, zv, zg, dy_r, lg_r, lb_r, ws_r, bs_r, dz_in, dz_o, dws_o, dbs_o, dlg_o, dlb_o, dvn_s, bacc):
        i = pl.program_id(0)

        @pl.when(i == 0)
        def _():
            dws_o[...] = jnp.zeros((NG, GB, GB), F32)
            bacc[...] = jnp.zeros((NG, GB, GB), F32)

        vhat, rs = _layernorm_parts(zv[...].astype(F32))
        vnb = (vhat * lg_r[...] + lb_r[...]).astype(BF16)
        ga = zg[...].astype(F32)
        sil, dsil = _silu_parts(ga)
        u = zu[...].astype(F32)
        dy = dy_r[...].astype(F32)
        t = dy * sil
        dsv_all = t * u
        dga_pre = dy * u * dsil
        mask = _chunk_mask()
        for g in range(NG):
            wf = jnp.where(mask, ws_r[g], 0.0)
            wg = wf.astype(BF16)
            wgt = wf.T.astype(BF16)
            cs = slice(g * GB, (g + 1) * GB)
            dw = jnp.zeros((GB, GB), F32)
            db = jnp.zeros((GB, GB), F32)
            for n in range(T // GB):
                rsl = slice(n * GB, (n + 1) * GB)
                vb = vnb[rsl, cs]
                sv = _dot(wg, vb) + bs_r[g]
                dz_o[0, rsl, cs] = (t[rsl, cs] * sv).astype(BF16)
                dz_o[2, rsl, cs] = (dga_pre[rsl, cs] * sv).astype(BF16)
                dsv = dsv_all[rsl, cs]
                dsb = dsv.astype(BF16)
                dvn_s[rsl, cs] = _dot(wgt, dsb)
                dw = dw + _dot_nt(dsb, vb)
                db = db + dsv
            dws_o[g] += jnp.where(mask, dw, 0.0)
            bacc[g] += db
        dvn = dvn_s[...]
        dvh = dvn * lg_r[...]
        dv = rs * (dvh - jnp.mean(dvh, axis=-1, keepdims=True) - vhat * jnp.mean(dvh * vhat, axis=-1, keepdims=True))
        dz_o[1] = dv.astype(BF16)
        _accumulate(i == 0, dlg_o, _rowsum(dvn * vhat))
        _accumulate(i == 0, dlb_o, _rowsum(dvn))

        @pl.when(i == nblk - 1)
        def _():
            for g in range(NG):
                dbs_o[g:g + 1, :] = _rowsum(bacc[g].T)

    zs = lambda k: pl.BlockSpec((T, D), lambda i: (i, k))
    return _pcall(
        body, name="branch_a_bwd", grid=(nblk,),
        in_specs=[zs(0), zs(1), zs(2), pl.BlockSpec((T, D), lambda i: (i, 0)), _full((1, D)), _full((1, D)),
                  _full((NG, GB, GB)), _full((NG, GB, GB)), HBM],
        out_specs=[pl.BlockSpec((3, T, D), lambda i: (0, i, 0)), _full((NG, GB, GB)), _full((NG, GB)), _full((1, D)),
                   _full((1, D))],
        out_shape=[SDS((10, S, D), BF16), SDS((NG, GB, GB), F32), SDS((NG, GB), F32), SDS((1, D), F32), SDS((1, D), F32)],
        scratch=[pltpu.VMEM((T, D), F32), pltpu.VMEM((NG, GB, GB), F32)], aliases={8: 0},
    )(z, z, z, dya, lg, lb, ws, bsb, dz)


def _branch_b_bwd(z, hs, dyb, cw, cb, wr, br, wi, bi, lam, dz):
    S = z.shape[0]
    T = min(256, S)
    nblk = S // T

    def body(zxb, zprev, zgb, hs_r, hprev_r, dy_r, cw_r, cb_r, wr_r, br_r, wi_r, bi_r, lam_r, dz_in,
             dz_o, dcw_o, dcb_o, dwr_o, dbr_o, dwi_o, dbi_o, dlam_o, xpad, hpad, apad, dpad, xc_s, pr_s, pi_s, r_s, ig_s, m_s,
             b_s, d_s, l_s, back_s, xcb_s, dprb_s, dpib_s, lcar):
        i = pl.program_id(0)
        blk = nblk - 1 - i
        first = i == 0

        @pl.when(first)
        def _():
            apad[pl.ds(T, 8), :] = jnp.zeros((8, D), F32)
            dpad[pl.ds(T, 8), :] = jnp.zeros((8, D), F32)
            lcar[...] = jnp.zeros((8, D), F32)
            dcw_o[...] = jnp.zeros((8, D), F32)
            dwr_o[...] = jnp.zeros((NG, GB, GB), F32)
            dwi_o[...] = jnp.zeros((NG, GB, GB), F32)

        keep = (blk > 0).astype(F32)
        nck = T // CH
        cw, cb, br, bi, lam = cw_r[...], cb_r[...], br_r[...], bi_r[...], lam_r[...]
        sp8 = -LRU_C * _softplus_neg(lam)
        xpad[pl.ds(0, 8), :] = zprev[...].astype(F32)[8:16, :] * keep
        hpad[pl.ds(0, 8), :] = hprev_r[...] * keep
        for c in range(nck):
            xpad[_ck(c, 8), :] = zxb[_ck(c), :].astype(F32)
            hpad[_ck(c, 8), :] = hs_r[_ck(c), :]
            _, xc = _lru_conv(xpad, c, cw, cb)
            xc_s[_ck(c), :] = xc
            xcb_s[_ck(c), :] = xc.astype(BF16)
        _lru_gate_matmuls(xcb_s, wr_r, wi_r, pr_s, pi_s)
        for c in range(nck):
            r, ig, a, _, mult = _lru_gates(pr_s[_ck(c), :], pi_s[_ck(c), :], br, bi, sp8)
            r_s[_ck(c), :] = r
            ig_s[_ck(c), :] = ig
            m_s[_ck(c), :] = mult
            apad[_ck(c), :] = a
            sil, dsil = _silu_parts(zgb[_ck(c), :].astype(F32))
            dy = dy_r[_ck(c), :].astype(F32)
            dz_o[1, _ck(c), :] = (dy * hs_r[_ck(c), :] * dsil).astype(BF16)
            d_s[_ck(c), :] = dy * sil
        for c in range(nck):
            b_s[_ck(c), :] = apad[_ck(c, 1), :]
        _scan_reverse(b_s, d_s, l_s, lcar, T)
        s_sp = s_br = s_bi = jnp.zeros((8, D), F32)
        for c in range(nck):
            lm, r, ig, mult, a, xc = l_s[_ck(c), :], r_s[_ck(c), :], ig_s[_ck(c), :], m_s[_ck(c), :], apad[_ck(c), :], xc_s[_ck(c), :]
            t = lm * mult
            dpad[_ck(c), :] = t * ig
            dl = lm * hpad[_ck(c, 7), :] * a - (lm * ig * xc) * (a * a) / mult
            dpr = dl * sp8 * r * (1.0 - r)
            dpi = t * xc * ig * (1.0 - ig)
            s_sp = s_sp + _half_sum(dl * r)
            s_br = s_br + _half_sum(dpr)
            s_bi = s_bi + _half_sum(dpi)
            dprb_s[_ck(c), :] = dpr.astype(BF16)
            dpib_s[_ck(c), :] = dpi.astype(BF16)
        _accumulate(first, dlam_o, _rowsum(s_sp) * (LRU_C * jax.nn.sigmoid(-lam)))
        _accumulate(first, dbr_o, _rowsum(s_br))
        _accumulate(first, dbi_o, _rowsum(s_bi))
        for h in range(NG):
            cs = slice(h * GB, (h + 1) * GB)
            back_s[:, cs] = _dot_nt(dprb_s[:, cs], wr_r[h].astype(BF16)) + _dot_nt(dpib_s[:, cs], wi_r[h].astype(BF16))
            dwr_o[h] += _dot_tn(xcb_s[:, cs], dprb_s[:, cs])
            dwi_o[h] += _dot_tn(xcb_s[:, cs], dpib_s[:, cs])
        s_cb = jnp.zeros((8, D), F32)
        s_cw = [jnp.zeros((8, D), F32)] * 4
        for c in range(nck):
            dxc = dpad[_ck(c), :] + back_s[_ck(c), :]
            dpad[_ck(c), :] = dxc
            s_cb = s_cb + _half_sum(dxc)
            s_cw = [s_cw[k] + _half_sum(xpad[_ck(c, 5 + k), :] * dxc) for k in range(4)]
        _accumulate(first, dcb_o, _rowsum(s_cb))
        for k in range(4):
            dcw_o[k:k + 1, :] += _rowsum(s_cw[k])
        for c in range(nck):
            dxb = ((dpad[_ck(c, 3), :] * cw[0:1] + dpad[_ck(c, 2), :] * cw[1:2]) + dpad[_ck(c, 1), :] * cw[2:3]) + dpad[_ck(c), :] * cw[3:4]
            dz_o[0, _ck(c), :] = dxb.astype(BF16)
        apad[pl.ds(T, 8), :] = apad[pl.ds(0, 8), :]
        dpad[pl.ds(T, 8), :] = dpad[pl.ds(0, 8), :]

    rev = lambda k: pl.BlockSpec((T, D), lambda i: (nblk - 1 - i, k))
    prev16 = pl.BlockSpec((16, D), lambda i: (jnp.maximum((nblk - 1 - i) * (T // 16) - 1, 0), 3))
    prev8 = pl.BlockSpec((8, D), lambda i: (jnp.maximum((nblk - 1 - i) * (T // 8) - 1, 0), 0))
    vec, mat = _full((1, D)), _full((NG, GB, GB))
    return _pcall(
        body, name="branch_b_bwd", grid=(nblk,),
        in_specs=[rev(3), prev16, rev(4), rev(0), prev8, rev(0), _full((8, D)), vec, mat, vec, mat, vec, vec, HBM],
        out_specs=[pl.BlockSpec((2, T, D), lambda i: (3, nblk - 1 - i, 0)), _full((8, D)), vec, mat, vec, mat, vec, vec],
        out_shape=[SDS((10, S, D), BF16), SDS((8, D), F32), SDS((1, D), F32), SDS((NG, GB, GB), F32), SDS((1, D), F32),
                   SDS((NG, GB, GB), F32), SDS((1, D), F32), SDS((1, D), F32)],
        scratch=[pltpu.VMEM((T + 8, D), F32)] * 4 + [pltpu.VMEM((T, D), F32)] * 10 + [pltpu.VMEM((T, D), BF16)] * 3
        + [pltpu.VMEM((8, D), F32)],
        aliases={13: 0}, vmem_mb=56,
    )(z, z, z, hs, hs, dyb, cw, cb, wr, br, wi, bi, lam, dz)


def _branch_c_bwd(z, kv, dyc, dz):
    S = z.shape[0]
    T = min(512, S)

    def body(zq, zg, kv_r, dy_r, dz_in, dz_o, dkv_o):
        @pl.when(pl.program_id(0) == 0)
        def _():
            dkv_o[...] = jnp.zeros((MEM, 2 * D), F32)

        gc = zg[...].astype(F32)
        sil, dsil = _silu_parts(gc)
        dy = dy_r[...].astype(F32)
        datt = dy * sil
        dgc_pre = dy * dsil
        scale = HD ** -0.5
        for h in range(NH):
            cs = slice(h * HD, (h + 1) * HD)
            vs = slice(D + h * HD, D + (h + 1) * HD)
            qh = zq[:, cs]
            p = _softmax_rows(_dot_nt(qh, kv_r[:, cs]) * scale)
            pb = p.astype(BF16)
            att = _dot(pb, kv_r[:, vs])
            dz_o[1, :, cs] = (dgc_pre[:, cs] * att).astype(BF16)
            dab = datt[:, cs].astype(BF16)
            dp = _dot_nt(dab, kv_r[:, vs])
            ds = (p * (dp - jnp.sum(p * dp, axis=-1, keepdims=True)) * scale).astype(BF16)
            dz_o[0, :, cs] = _dot(ds, kv_r[:, cs]).astype(BF16)
            dkv_o[:, cs] += _dot_tn(ds, qh)
            dkv_o[:, vs] += _dot_tn(pb, dab)

    zs = lambda k: pl.BlockSpec((T, D), lambda i: (i, k))
    return _pcall(
        body, name="branch_c_bwd", grid=(S // T,),
        in_specs=[zs(5), zs(6), _full((MEM, 2 * D)), pl.BlockSpec((T, D), lambda i: (i, 0)), HBM],
        out_specs=[pl.BlockSpec((2, T, D), lambda i: (4, i, 0)), _full((MEM, 2 * D))],
        out_shape=[SDS((10, S, D), BF16), SDS((MEM, 2 * D), F32)], aliases={4: 0},
    )(z, z, kv, dyc, dz)


def _mm_dh(dz, w, x, dxn, g):
    S = x.shape[0]
    tm = min(1024, S)

    def body(dz_r, w_r, x_r, dxn_r, g_r, dx_o, dg_o, acc):
        i, k = pl.program_id(0), pl.program_id(1)
        _accumulate(k == 0, acc, _dot_nt(dz_r[0], w_r[...]))

        @pl.when(k == 9)
        def _():
            dh = acc[...]
            xv = x_r[...]
            r1 = _rms_scale(xv)
            wv = dh * g_r[...]
            dx_o[...] = dxn_r[...] + r1 * wv - xv * (r1 * r1 * r1) * jnp.mean(wv * xv, axis=-1, keepdims=True)
            _accumulate(i == 0, dg_o, _rowsum(dh * xv * r1))

    row = pl.BlockSpec((tm, D), lambda i, k: (i, 0))
    return _pcall(
        body, name="mm_dh", grid=(S // tm, 10),
        in_specs=[pl.BlockSpec((1, tm, D), lambda i, k: (k, i, 0)), pl.BlockSpec((D, D), lambda i, k: (0, _dz_col(k))),
                  row, row, _full((1, D))],
        out_specs=[row, _full((1, D))], out_shape=[SDS((S, D), F32), SDS((1, D), F32)],
        scratch=[pltpu.VMEM((tm, D), F32)],
    )(dz, w, x, dxn, g)


def _mm_dwin(h, dz):
    S = h.shape[0]
    tk = min(1024, S)
    nk = S // tk

    def body(h_r, dz_r, o_r, acc):
        k = pl.program_id(1)
        _accumulate(k == 0, acc, _dot_tn(h_r[...], dz_r[0]))

        @pl.when(k == nk - 1)
        def _():
            o_r[...] = acc[...].astype(BF16)

    return _pcall(
        body, name="mm_dwin", grid=(10, nk),
        in_specs=[pl.BlockSpec((tk, D), lambda n, k: (k, 0)), pl.BlockSpec((1, tk, D), lambda n, k: (n, k, 0))],
        out_specs=pl.BlockSpec((D, D), lambda n, k: (0, _dz_col(n))), out_shape=SDS((D, NIN), BF16),
        scratch=[pltpu.VMEM((D, D), F32)],
    )(h, dz)


def _mm_tn4(a4, b4):
    S = a4[0].shape[0]
    tk = min(1024, S)
    nk = S // tk

    def body(*refs):
        a_r, b_r, o_r, acc = refs[0:4], refs[4:8], refs[8], refs[9]
        w, k = pl.program_id(0), pl.program_id(1)
        for a in range(4):
            @pl.when(w == a)
            def _(a=a):
                _accumulate(k == 0, acc, _dot_tn(a_r[a][...], b_r[a][...]))

        @pl.when(k == nk - 1)
        def _():
            o_r[...] = acc[...].astype(BF16)

    def blk(a):
        return pl.BlockSpec((tk, D), lambda w, k: (jnp.where(w == a, k, jnp.where(w < a, 0, nk - 1)), 0))

    return _pcall(body, name="mm_tn4", grid=(4, nk), in_specs=[blk(a) for a in range(4)] * 2,
                  out_specs=pl.BlockSpec((None, D, D), lambda w, k: (w, 0, 0)), out_shape=SDS((4, D, D), BF16),
                  scratch=[pltpu.VMEM((D, D), F32)])(*a4, *b4)


def _mem_bwd(mem, g, wkv, dkv, dg_acc):
    def body(m_ref, g_ref, w_ref, dkv_ref, acc_ref, dw_ref, dg_ref):
        m = m_ref[...]
        mr = m * _rms_scale(m)
        mn = (mr * g_ref[...]).astype(BF16)
        dkb = dkv_ref[...].astype(BF16)
        dw_ref[...] = _dot_tn(mn, dkb).astype(BF16)
        dg_ref[...] = acc_ref[...] + _rowsum(_dot_nt(dkb, w_ref[...]) * mr)

    return _pcall(body, name="mem_bwd", in_specs=[VMEM] * 5, out_specs=[VMEM] * 2,
                  out_shape=[SDS((D, 2 * D), BF16), SDS((1, D), F32)], vmem_mb=48)(mem, g, wkv, dkv, dg_acc)


def _adamw_math(w, g, m, v):
    m2 = ADAM_B1 * m + (1.0 - ADAM_B1) * g
    v2 = ADAM_B2 * v + (1.0 - ADAM_B2) * (g * g)
    mh = m2 / (1.0 - ADAM_B1 ** ADAM_STEP)
    vh = v2 / (1.0 - ADAM_B2 ** ADAM_STEP)
    return -ADAM_LR * (mh / (jnp.sqrt(vh) + ADAM_EPS) + ADAM_WD * w), m2, v2


def _adamw_layer(l, w, m, v, g, prev, which=None, rows=256):
    L, R, C = w.shape

    def body(w_r, m_r, v_r, g_r, *rest):
        g_o, d_o, m_o, v_o = rest[-4:]
        g = g_r[...]
        d, m2, v2 = _adamw_math(w_r[...], g, m_r[...], v_r[...])
        g_o[...] = g
        d_o[...] = d
        m_o[...] = m2
        v_o[...] = v2

    st = pl.BlockSpec((None, rows, C), lambda i: (l, i, 0))
    gs = pl.BlockSpec((rows, C), lambda i: (i, 0)) if which is None else pl.BlockSpec((None, rows, C), lambda i: (which, i, 0))
    carried = list(prev) if prev is not None else []
    return _pcall(body, name="adamw_layer", grid=(R // rows,), in_specs=[st] * 3 + [gs] + [HBM] * len(carried),
                  out_specs=[st] * 4, out_shape=[SDS(w.shape, F32)] * 4, vmem_mb=56,
                  aliases={4 + k: k for k in range(len(carried))} or None)(w, m, v, g, *carried)


def _adamw_flat(w, m, v, g, rows):
    R, C = w.shape

    def body(w_r, m_r, v_r, g_r, d_o, m_o, v_o):
        d, m2, v2 = _adamw_math(w_r[...], g_r[...], m_r[...], v_r[...])
        d_o[...] = d
        m_o[...] = m2
        v_o[...] = v2

    blk = pl.BlockSpec((rows, C), lambda i: (i, 0))
    return _pcall(body, name="adamw_flat", grid=(R // rows,), in_specs=[blk] * 4, out_specs=[blk] * 3,
                  out_shape=[SDS((R, C), F32)] * 3)(w, m, v, g)


_SMALL = ("mem_norm_g", "pre_norm_g", "post_norm_g", "gmlp_ln_g", "gmlp_ln_b", "gmlp_ws", "gmlp_bs", "conv_b", "lru_wr",
          "lru_br", "lru_wi", "lru_bi", "lru_lambda")


def _pack_small(parts, conv_w_part):
    rows = [parts[n].reshape(-1, 128) for n in _SMALL] + [conv_w_part.reshape(-1, 128)]
    used = sum(r.shape[0] for r in rows)
    rows.append(jnp.zeros((SMALL_ROWS - used, 128), F32))
    return jnp.concatenate(rows, axis=0)


def _unpack_small(pack, shapes):
    out, at = {}, 0
    for n in _SMALL:
        size = 1
        for s in shapes[n]:
            size *= s
        out[n] = pack[at:at + size // 128].reshape(shapes[n])
        at += size // 128
    return out, at


def kernel(x, mem, mem_norm_g, pre_norm_g, post_norm_g, w_in, gmlp_ln_g, gmlp_ln_b, gmlp_ws, gmlp_bs, conv_w, conv_b, lru_wr, lru_br, lru_wi, lru_bi, lru_lambda, w_kv, w_pa, w_pb, w_pc, w_out, loss_target, m_mem_norm_g, m_pre_norm_g, m_post_norm_g, m_w_in, m_gmlp_ln_g, m_gmlp_ln_b, m_gmlp_ws, m_gmlp_bs, m_conv_w, m_conv_b, m_lru_wr, m_lru_br, m_lru_wi, m_lru_bi, m_lru_lambda, m_w_kv, m_w_pa, m_w_pb, m_w_pc, m_w_out, v_mem_norm_g, v_pre_norm_g, v_post_norm_g, v_w_in, v_gmlp_ln_g, v_gmlp_ln_b, v_gmlp_ws, v_gmlp_bs, v_conv_w, v_conv_b, v_lru_wr, v_lru_br, v_lru_wi, v_lru_bi, v_lru_lambda, v_w_kv, v_w_pa, v_w_pb, v_w_pc, v_w_out):
    L = w_in.shape[0]
    S = x.shape[1]
    xs = [x[0]]
    mem2 = mem[0]
    mg = mem_norm_g.reshape(1, D)
    vec = lambda a, l: a[l].reshape(1, D)
    ci = lax.axis_index("c")
    jpos = 2 * lax.axis_index("x") + lax.axis_index("y")
    pos = jnp.reshape(jpos, (1,)).astype(jnp.int32)
    pos2 = jnp.stack([jpos, ci]).astype(jnp.int32)

    cw8 = jnp.pad(conv_w, ((0, 0), (0, 4), (0, 0)))
    placed = [_cast_place(l, pos, w_in, w_kv, w_pa, w_pb, w_pc, w_out, cw8) for l in range(L)]
    W = [None] * L
    started = _gather_start(0, placed[0])
    mid = _gather_mid(0, started, started[-1])
    started = _gather_start(1, placed[1], mid[-1])
    W[0] = _gather_end(0, mid, started[-1])

    saved = []
    for l in range(L):
        Win, Wkv, Wp, Cw = W[l]
        z, h = _mm_in(xs[l], vec(pre_norm_g, l), Win)
        bsb = jnp.broadcast_to(gmlp_bs[l][:, :, None], (NG, GB, GB))
        ya = _branch_a(z, vec(gmlp_ln_g, l), vec(gmlp_ln_b, l), gmlp_ws[l], bsb)
        yb, hs = _branch_b(z, Cw, vec(conv_b, l), lru_wr[l], vec(lru_br, l), lru_wi[l], vec(lru_bi, l), vec(lru_lambda, l))
        kv = _kv(mem2, mg, Wkv)
        yc = _branch_c(z, kv)
        pg = vec(post_norm_g, l)
        if l + 1 < L:
            mid = _gather_mid(l + 1, started, yc)
            if l + 2 < L:
                started = _gather_start(l + 2, placed[l + 2], mid[-1])
                pg = pg + started[-1][0, 0]
        pa, pb, pc, mgd, o, xn = _merge_out(ya, yb, yc, z, Wp, xs[l], pg)
        if l + 1 < L:
            W[l + 1] = _gather_end(l + 1, mid, xn)
        xs.append(xn)
        saved.append((z, h, ya, yb, yc, hs, kv, pa, pb, pc, mgd, o, bsb))

    loss11, dxn = _loss_head(xs[L], loss_target[0])
    loss = lax.psum(loss11[0, 0], ("x", "y", "c"))

    big = dict(w_in=(w_in, m_w_in, v_w_in), w_kv=(w_kv, m_w_kv, v_w_kv), w_pa=(w_pa, m_w_pa, v_w_pa),
               w_pb=(w_pb, m_w_pb, v_w_pb), w_pc=(w_pc, m_w_pc, v_w_pc), w_out=(w_out, m_w_out, v_w_out))
    out = {n: None for n in big}
    kin, nsh, nkv, rp = w_in.shape[1], w_in.shape[2], w_kv.shape[2], w_pa.shape[1]

    def finish_layer(l, a2a, after):
        thru = _a2a_wait(l, a2a, after)
        lin, lkv, lp = thru[3:6]
        g_in, g_kv, g_p = _sum_share(pos2, (lin, lkv, lp.reshape(NDEV - 1, 2 * rp, D)), thru[0:3])
        g_p = g_p.reshape(4, rp, D)
        out["w_in"] = _adamw_layer(l, *big["w_in"], g_in, out["w_in"])
        out["w_kv"] = _adamw_layer(l, *big["w_kv"], g_kv, out["w_kv"])
        for k, n in enumerate(("w_pa", "w_pb", "w_pc", "w_out")):
            out[n] = _adamw_layer(l, *big[n], g_p, out[n], which=k)

    small = {n: [None] * L for n in _SMALL}
    dconv_w = [None] * L
    dg_mem = jnp.zeros((1, D), F32)
    pending = None
    sent = []
    for l in reversed(range(L)):
        Win, Wkv, Wp, Cw = W[l]
        z, h, ya, yb, yc, hs, kv, pa, pb, pc, mgd, o, bsb = saved[l]
        pg = vec(post_norm_g, l) if pending is None else vec(post_norm_g, l) + pending[1][-1][0, 0]
        do, dpa, dpb, dpc, dya, dyb, dyc, dz, dgpost = _out_bwd(dxn, o, pg, Wp, z, pa, pb, pc)
        dz, dws, dbs, dlg, dlb = _branch_a_bwd(z, dya, vec(gmlp_ln_g, l), vec(gmlp_ln_b, l), gmlp_ws[l], bsb, dz)
        dz, dcw, dcb, dwr, dbr, dwi, dbi, dlam = _branch_b_bwd(
            z, hs, dyb, Cw, vec(conv_b, l), lru_wr[l], vec(lru_br, l), lru_wi[l], vec(lru_bi, l), vec(lru_lambda, l), dz)
        dz, dkv = _branch_c_bwd(z, kv, dyc, dz)
        g_in = _mm_dwin(h, dz)
        g_kv, dg_mem = _mem_bwd(mem2, mg, Wkv, dkv, dg_mem)
        grads = (g_in, g_kv, _mm_tn4((ya, yb, yc, mgd), (dpa, dpb, dpc, do)))
        lands = (lax.empty((NDEV - 1, kin // 2, nsh), BF16), lax.empty((NDEV - 1, kin // 2, nkv), BF16),
                 lax.empty((NDEV - 1, 4, rp // 2, D), BF16))
        a2a = _a2a_start(l, grads, lands)
        dx, dgpre = _mm_dh(dz, Win, xs[l], dxn, vec(pre_norm_g, l) + a2a[-1][0, 0])
        pending = (l, a2a)
        sent.append(pending)
        for n, val in (("pre_norm_g", dgpre), ("post_norm_g", dgpost), ("gmlp_ln_g", dlg), ("gmlp_ln_b", dlb), ("gmlp_ws", dws),
                       ("gmlp_bs", dbs), ("conv_b", dcb), ("lru_wr", dwr), ("lru_br", dbr), ("lru_wi", dwi), ("lru_bi", dbi),
                       ("lru_lambda", dlam)):
            small[n][l] = val
        dconv_w[l] = dcw[0:4]
        dxn = dx
    grad_x = dxn.reshape(1, S, D)

    parts = {n: jnp.stack(small[n]) for n in _SMALL if n != "mem_norm_g"}
    parts["mem_norm_g"] = dg_mem
    me1 = jnp.reshape(2 * jpos + ci, (1,)).astype(jnp.int32)
    rs = _rs_start(_pack_small(parts, jnp.stack(dconv_w)), lax.empty((NDEV - 1, SMALL_ROWS // NDEV, 128), F32))
    for l, a2a in sent[:-1]:
        finish_layer(l, a2a, rs[-1])
    pack, land = _rs_wait(rs, out["w_out"][1])
    ag = _ag_start(_small_sum(me1, pack, land))
    finish_layer(pending[0], pending[1], ag[-1])
    gsum = _ag_wait(ag, out["w_out"][1])
    given = dict(mem_norm_g=(mem_norm_g, m_mem_norm_g, v_mem_norm_g), pre_norm_g=(pre_norm_g, m_pre_norm_g, v_pre_norm_g),
                 post_norm_g=(post_norm_g, m_post_norm_g, v_post_norm_g), gmlp_ln_g=(gmlp_ln_g, m_gmlp_ln_g, v_gmlp_ln_g),
                 gmlp_ln_b=(gmlp_ln_b, m_gmlp_ln_b, v_gmlp_ln_b), gmlp_ws=(gmlp_ws, m_gmlp_ws, v_gmlp_ws),
                 gmlp_bs=(gmlp_bs, m_gmlp_bs, v_gmlp_bs), conv_b=(conv_b, m_conv_b, v_conv_b), lru_wr=(lru_wr, m_lru_wr, v_lru_wr),
                 lru_br=(lru_br, m_lru_br, v_lru_br), lru_wi=(lru_wi, m_lru_wi, v_lru_wi), lru_bi=(lru_bi, m_lru_bi, v_lru_bi),
                 lru_lambda=(lru_lambda, m_lru_lambda, v_lru_lambda))
    shapes = {n: given[n][0].shape for n in _SMALL}
    zero_cw = jnp.zeros((L, 4, D), F32)
    packs = [_pack_small({n: given[n][k] for n in _SMALL}, zero_cw) for k in range(3)]
    dsm, msm, vsm = _adamw_flat(packs[0], packs[1], packs[2], gsum, 2560)
    g_small, at = _unpack_small(gsum, shapes)
    d_small, _ = _unpack_small(dsm, shapes)
    m_small, _ = _unpack_small(msm, shapes)
    v_small, _ = _unpack_small(vsm, shapes)
    for n in _SMALL:
        out[n] = (g_small[n], d_small[n], m_small[n], v_small[n])
    g_cw = lax.dynamic_slice_in_dim(gsum[at:at + L * 4 * D // 128].reshape(L * 4, D), jpos * (D // 4), D // 4, axis=1)
    d_cw, m_cw, v_cw = _adamw_flat(conv_w.reshape(L * 4, D // 4), m_conv_w.reshape(L * 4, D // 4),
                                   v_conv_w.reshape(L * 4, D // 4), g_cw, L * 4)
    out["conv_w"] = tuple(a.reshape(L, 4, D // 4) for a in (g_cw, d_cw, m_cw, v_cw))

    order = ("mem_norm_g", "pre_norm_g", "post_norm_g", "w_in", "gmlp_ln_g", "gmlp_ln_b", "gmlp_ws", "gmlp_bs", "conv_w", "conv_b",
             "lru_wr", "lru_br", "lru_wi", "lru_bi", "lru_lambda", "w_kv", "w_pa", "w_pb", "w_pc", "w_out")
    return (loss, grad_x, *[out[n][0] for n in order], *[out[n][1] for n in order], *[out[n][2] for n in order],
            *[out[n][3] for n in order])
```

```python
import functools

import jax
import jax.numpy as jnp
from jax import lax
from jax.experimental import pallas as pl
from jax.experimental.pallas import tpu as pltpu

F32 = jnp.float32
BF16 = jnp.bfloat16
SDS = jax.ShapeDtypeStruct
MESH = pl.DeviceIdType.MESH

D = 1024
NIN = 10 * D
MEM = 256
GB = 128
NG = 8
NH = 4
HD = D // NH
EPS = 1e-6
LRU_C = 8.0
ADAM_LR, ADAM_B1, ADAM_B2, ADAM_EPS, ADAM_WD, ADAM_STEP = 0.001, 0.9, 0.999, 1e-08, 0.01, 10
NDEV = 8
SMALL_ROWS = 12800

_CALL_KW = {}
HBM = pl.BlockSpec(memory_space=pltpu.HBM)
VMEM = pl.BlockSpec(memory_space=pltpu.VMEM)
SEM = pl.BlockSpec(memory_space=pltpu.SEMAPHORE)
ANY = pl.BlockSpec(memory_space=pl.ANY)
TOKEN = SDS((8, 128), F32)


def _pcall(body, *, name, in_specs, out_specs, out_shape, grid=None, scratch=(), vmem_mb=48, aliases=None, effect=False,
           prefetch=0):
    kw = dict(_CALL_KW)
    if aliases:
        kw["input_output_aliases"] = aliases
    params = dict(vmem_limit_bytes=vmem_mb << 20)
    if grid is not None:
        params["dimension_semantics"] = ("arbitrary",) * len(grid)
    if effect:
        params["has_side_effects"] = pltpu.SideEffectType.DATAFLOW_SIDE_EFFECTING
    if prefetch:
        kw["grid_spec"] = pltpu.PrefetchScalarGridSpec(num_scalar_prefetch=prefetch, grid=grid, in_specs=in_specs,
                                                       out_specs=out_specs, scratch_shapes=list(scratch))
    else:
        kw.update(in_specs=in_specs, out_specs=out_specs, scratch_shapes=list(scratch))
        if grid is not None:
            kw["grid"] = grid
    return pl.pallas_call(body, name=name, out_shape=out_shape, compiler_params=pltpu.CompilerParams(**params), **kw)


def _full(shape):
    nd = len(shape)
    return pl.BlockSpec(shape, lambda *_: (0,) * nd)


def _dot(a, b):
    return jnp.dot(a, b, preferred_element_type=F32)


def _dot_nt(a, b):
    return lax.dot_general(a, b, (((1,), (1,)), ((), ())), preferred_element_type=F32)


def _dot_tn(a, b):
    return lax.dot_general(a, b, (((0,), (0,)), ((), ())), preferred_element_type=F32)


def _rowsum(a):
    return jnp.sum(a, axis=0, keepdims=True)


def _sigmoid(x):
    return 0.5 * jnp.tanh(0.5 * x) + 0.5


def _silu_parts(g):
    s = _sigmoid(g)
    return g * s, s * (1.0 + g * (1.0 - s))


def _rms_scale(x):
    return lax.rsqrt(jnp.mean(x * x, axis=-1, keepdims=True) + EPS)


def _dz_col(k):
    return jnp.where(k < 3, k, jnp.where(k < 6, k + 4, k - 3))


def _coords():
    return lax.axis_index("x"), lax.axis_index("y"), lax.axis_index("c")


def _other_chips(x, y):
    return [(1 - x, y), (x, 1 - y), (1 - x, 1 - y)]


def _peer(x, y, c, mask):
    return (1 - x if mask & 4 else x, 1 - y if mask & 2 else y, 1 - c if mask & 1 else c)


def _remote(src, dst, ssem, rsem, k, to):
    return pltpu.make_async_remote_copy(src_ref=src, dst_ref=dst, send_sem=ssem.at[k], recv_sem=rsem.at[k], device_id=to,
                                        device_id_type=MESH)


def _w_half(a, ref, jj, cc):
    if a == 2:
        rp = ref.shape[1] // 4
        return ref.at[:, pl.ds(jj * rp + cc * (rp // 2), rp // 2), :]
    kin, nsh = ref.shape[0], ref.shape[1] // 4
    return ref.at[pl.ds(cc * (kin // 2), kin // 2), pl.ds(jj * nsh, nsh)]


def _cw_block(ref, jj):
    return ref.at[:, pl.ds(jj * (D // 4), D // 4)]


def _cast_place(l, pos, w_in, w_kv, w_pa, w_pb, w_pc, w_out, cw8):
    kin, nsh = w_in.shape[1], w_in.shape[2]
    nkv, rp = w_kv.shape[2], w_pa.shape[1]
    half = kin // 2

    def body(pos_r, win, wkv, pa, pb, pc, po, cw, Win, Wkv, Wp, Cw):
        Win[...] = win[...].astype(BF16)
        Wkv[...] = wkv[...].astype(BF16)

        @pl.when(pl.program_id(0) == 0)
        def _():
            for k, r in enumerate((pa, pb, pc, po)):
                Wp[k] = r[...].astype(BF16)
            Cw[...] = cw[...]

    proj = pl.BlockSpec((None, rp, D), lambda i, p: (l, 0, 0))
    return _pcall(
        body, name="cast_place", grid=(2,), prefetch=1,
        in_specs=[pl.BlockSpec((None, half, nsh), lambda i, p: (l, i, 0)), pl.BlockSpec((None, half, nkv), lambda i, p: (l, i, 0)),
                  proj, proj, proj, proj, pl.BlockSpec((None, 8, D // 4), lambda i, p: (l, 0, 0))],
        out_specs=[pl.BlockSpec((half, nsh), lambda i, p: (i, p[0])), pl.BlockSpec((half, nkv), lambda i, p: (i, p[0])),
                   pl.BlockSpec((4, rp, D), lambda i, p: (0, p[0], 0)), pl.BlockSpec((8, D // 4), lambda i, p: (0, p[0]))],
        out_shape=[SDS((kin, 4 * nsh), BF16), SDS((kin, 4 * nkv), BF16), SDS((4, 4 * rp, D), BF16), SDS((8, D), F32)],
    )(pos, w_in, w_kv, w_pa, w_pb, w_pc, w_out, cw8)


def _hbm_like(bufs):
    return [pltpu.HBM(b.shape, b.dtype) for b in bufs]


def _gather_start(l, bufs, after=None):
    extra = [] if after is None else [after]

    def body(win, wkv, wp, cw, *rest):
        ssem, rsem, token = rest[len(extra)], rest[len(extra) + 1], rest[-1]
        x, y, c = _coords()
        j = 2 * x + y
        refs = (win, wkv, wp)
        for k, chip in enumerate(_other_chips(x, y)):
            to = (chip[0], chip[1], c)
            for a in range(3):
                half = _w_half(a, refs[a], j, c)
                _remote(half, half, ssem, rsem, a * 3 + k, to).start()
            mine = _cw_block(cw, j)
            _remote(mine, mine, ssem, rsem, 9 + k, to).start()
        token[...] = jnp.zeros((8, 128), F32)

    return _pcall(
        body, name=f"gather_start_{l}", in_specs=[HBM] * 4 + [ANY] * len(extra), out_specs=[SEM, SEM] + [HBM] * 4 + [VMEM],
        out_shape=[pltpu.SemaphoreType.DMA((12,)), pltpu.SemaphoreType.DMA((12,))] + _hbm_like(bufs) + [TOKEN],
        aliases={0: 2, 1: 3, 2: 4, 3: 5}, effect=True,
    )(*[pltpu.with_memory_space_constraint(b, pltpu.HBM) for b in bufs], *extra)


def _gather_mid(l, started, after):
    ssem, rsem, b0, b1, b2, b3, _ = started

    def body(win, wkv, wp, cw, ssem, rsem, after_r, ssem2, rsem2, o0, o1, o2, o3, token):
        x, y, c = _coords()
        j = 2 * x + y
        me, sib = (x, y, c), (x, y, 1 - c)
        token[...] = jnp.zeros((8, 128), F32)
        refs = (win, wkv, wp)
        chips = _other_chips(x, y)
        for k, chip in enumerate(chips):
            jk = 2 * chip[0] + chip[1]
            for a in range(3):
                got = _w_half(a, refs[a], jk, c)
                _remote(got, got, ssem, rsem, a * 3 + k, me).wait_recv()
            got = _cw_block(cw, jk)
            _remote(got, got, ssem, rsem, 9 + k, me).wait_recv()
        for k in range(3):
            for a in range(3):
                half = _w_half(a, refs[a], j, c)
                _remote(half, half, ssem, rsem, a * 3 + k, me).wait_send()
            mine = _cw_block(cw, j)
            _remote(mine, mine, ssem, rsem, 9 + k, me).wait_send()
        for k, chip in enumerate(chips):
            jk = 2 * chip[0] + chip[1]
            for a in range(3):
                got = _w_half(a, refs[a], jk, c)
                _remote(got, got, ssem2, rsem2, a * 3 + k, sib).start()

    bufs = (b0, b1, b2, b3)
    return _pcall(
        body, name=f"gather_mid_{l}", in_specs=[HBM] * 4 + [SEM, SEM, ANY], out_specs=[SEM, SEM] + [HBM] * 4 + [VMEM],
        out_shape=[pltpu.SemaphoreType.DMA((9,)), pltpu.SemaphoreType.DMA((9,))] + _hbm_like(bufs) + [TOKEN],
        aliases={0: 2, 1: 3, 2: 4, 3: 5}, effect=True,
    )(b0, b1, b2, b3, ssem, rsem, after)


def _gather_end(l, mid, after):
    ssem2, rsem2, b0, b1, b2, b3, _ = mid

    def body(win, wkv, wp, cw, ssem2, rsem2, after_r, o0, o1, o2, o3):
        x, y, c = _coords()
        me = (x, y, c)
        refs = (win, wkv, wp)
        for k, chip in enumerate(_other_chips(x, y)):
            jk = 2 * chip[0] + chip[1]
            for a in range(3):
                got = _w_half(a, refs[a], jk, 1 - c)
                _remote(got, got, ssem2, rsem2, a * 3 + k, me).wait_recv()
                sent = _w_half(a, refs[a], jk, c)
                _remote(sent, sent, ssem2, rsem2, a * 3 + k, me).wait_send()

    bufs = (b0, b1, b2, b3)
    return _pcall(
        body, name=f"gather_end_{l}", in_specs=[HBM] * 4 + [SEM, SEM, ANY], out_specs=[HBM] * 4, out_shape=_hbm_like(bufs),
        aliases={0: 0, 1: 1, 2: 2, 3: 3}, effect=True,
    )(b0, b1, b2, b3, ssem2, rsem2, after)


def _g_piece(a, ref, jd, dc):
    if a == 2:
        rp = ref.shape[1] // 4
        return ref.at[:, pl.ds(jd * rp + dc * (rp // 2), rp // 2), :]
    kin, nsh = ref.shape[0], ref.shape[1] // 4
    return ref.at[pl.ds(dc * (kin // 2), kin // 2), pl.ds(jd * nsh, nsh)]


def _a2a_start(name, kinds, grads, lands):
    n = len(kinds)

    def body(*refs):
        g, ld, ssem, rsem, token = refs[0:n], refs[n:2 * n], refs[2 * n], refs[2 * n + 1], refs[-1]
        x, y, c = _coords()
        for mask in range(1, NDEV):
            p = _peer(x, y, c, mask)
            for i, a in enumerate(kinds):
                _remote(_g_piece(a, g[i], 2 * p[0] + p[1], p[2]), ld[i].at[mask - 1], ssem, rsem, i * 7 + mask - 1, p).start()
        token[...] = jnp.zeros((8, 128), F32)

    bufs = tuple(grads) + tuple(lands)
    return _pcall(
        body, name=name, in_specs=[HBM] * (2 * n), out_specs=[SEM, SEM] + [HBM] * (2 * n) + [VMEM],
        out_shape=[pltpu.SemaphoreType.DMA((7 * n,)), pltpu.SemaphoreType.DMA((7 * n,))] + _hbm_like(bufs) + [TOKEN],
        aliases={i: 2 + i for i in range(2 * n)}, effect=True,
    )(*[pltpu.with_memory_space_constraint(b, pltpu.HBM) for b in bufs])


def _a2a_wait(name, kinds, started, after):
    n = len(kinds)
    ssem, rsem = started[0], started[1]
    bufs = tuple(started[2:2 + 2 * n])

    def body(*refs):
        g, ld, ssem, rsem = refs[0:n], refs[n:2 * n], refs[2 * n], refs[2 * n + 1]
        x, y, c = _coords()
        me = (x, y, c)
        for mask in range(1, NDEV):
            for i in range(n):
                got = ld[i].at[mask - 1]
                _remote(got, got, ssem, rsem, i * 7 + mask - 1, me).wait_recv()
        for mask in range(1, NDEV):
            p = _peer(x, y, c, mask)
            for i, a in enumerate(kinds):
                sent = _g_piece(a, g[i], 2 * p[0] + p[1], p[2])
                _remote(sent, sent, ssem, rsem, i * 7 + mask - 1, me).wait_send()

    return _pcall(
        body, name=name, in_specs=[HBM] * (2 * n) + [SEM, SEM, ANY], out_specs=[HBM] * (2 * n), out_shape=_hbm_like(bufs),
        aliases={i: i for i in range(2 * n)}, effect=True,
    )(*bufs, ssem, rsem, after)


def _sum_share(pos, lands, grads):
    rows, n = 128, 4
    widths = [ld.shape[2] for ld in lands]

    def body(pos_r, l0, w0, l1, w1, l2, w2, g0, g1, g2, b0, b1, b2, lsem, ssem, rsem):
        i = pl.program_id(0)
        x, y, c = _coords()
        sib = (x, y, 1 - c)
        ld, ow, gs, bufs = (l0, l1, l2), (w0, w1, w2), (g0, g1, g2), (b0, b1, b2)

        def dst(a, step):
            row = step * (2 * rows) + c * rows if a == 2 else c * (n * rows) + step * rows
            return gs[a].at[pl.ds(row, rows), :]

        def copies(a, step, sl):
            src = bufs[a].at[sl]
            lc = pltpu.make_async_copy(src, dst(a, step), lsem.at[a, sl])
            rc = pltpu.make_async_remote_copy(src_ref=src, dst_ref=dst(a, step), send_sem=ssem.at[a, sl], recv_sem=rsem.at[a],
                                              device_id=sib, device_id_type=MESH)
            return lc, rc

        def drain(a, step, sl):
            lc, rc = copies(a, step, sl)
            lc.wait()
            rc.wait_send()

        slot = i % 2

        @pl.when(i >= 2)
        def _():
            for a in range(3):
                drain(a, i - 2, slot)

        for a in range(3):
            acc = ow[a][...].astype(F32)
            for k in range(NDEV - 1):
                acc = acc + ld[a][k].astype(F32)
            bufs[a][slot] = acc
            lc, rc = copies(a, i, slot)
            lc.start()
            rc.start()

        @pl.when(i == n - 1)
        def _():
            for a in range(3):
                drain(a, n - 2, (n - 2) % 2)
                drain(a, n - 1, (n - 1) % 2)
                whole = gs[a].at[pl.ds(0, n * rows), :]
                pltpu.make_async_remote_copy(src_ref=whole, dst_ref=whole, send_sem=ssem.at[a, 0], recv_sem=rsem.at[a],
                                             device_id=(x, y, c), device_id_type=MESH).wait_recv()

    land = lambda w: pl.BlockSpec((NDEV - 1, rows, w), lambda i, p: (0, i, 0))
    in_specs = [land(widths[0]), pl.BlockSpec((rows, widths[0]), lambda i, p: (p[1] * n + i, p[0])),
                land(widths[1]), pl.BlockSpec((rows, widths[1]), lambda i, p: (p[1] * n + i, p[0])),
                land(widths[2]), pl.BlockSpec((None, rows, widths[2]), lambda i, p: (i, 2 * p[0] + p[1], 0))]
    args = [t for pair in zip(lands, grads) for t in pair]
    return _pcall(
        body, name="sum_share", grid=(n,), prefetch=1, in_specs=in_specs, out_specs=[HBM] * 3,
        out_shape=[SDS((2 * n * rows, w), F32) for w in widths],
        scratch=[pltpu.VMEM((2, rows, w), F32) for w in widths]
        + [pltpu.SemaphoreType.DMA((3, 2)), pltpu.SemaphoreType.DMA((3, 2)), pltpu.SemaphoreType.DMA((3,))],
    )(pos, *args)


def _dev_index(p):
    return 4 * p[0] + 2 * p[1] + p[2]


def _small_rows(ref, d):
    r8 = ref.shape[0] // NDEV
    return ref.at[pl.ds(d * r8, r8), :]


def _rs_start(pack, land):
    def body(p_ref, ld, ssem, rsem, o0, o1, token):
        x, y, c = _coords()
        for mask in range(1, NDEV):
            p = _peer(x, y, c, mask)
            _remote(_small_rows(p_ref, _dev_index(p)), ld.at[mask - 1], ssem, rsem, mask - 1, p).start()
        token[...] = jnp.zeros((8, 128), F32)

    bufs = (pack, land)
    return _pcall(
        body, name="rs_start", in_specs=[HBM] * 2, out_specs=[SEM, SEM, HBM, HBM, VMEM],
        out_shape=[pltpu.SemaphoreType.DMA((NDEV - 1,)), pltpu.SemaphoreType.DMA((NDEV - 1,))] + _hbm_like(bufs) + [TOKEN],
        aliases={0: 2, 1: 3}, effect=True,
    )(*[pltpu.with_memory_space_constraint(b, pltpu.HBM) for b in bufs])


def _rs_wait(started, after):
    ssem, rsem, pack, land, _ = started

    def body(p_ref, ld, ssem, rsem, after_r, o0, o1):
        x, y, c = _coords()
        for mask in range(1, NDEV):
            got = ld.at[mask - 1]
            _remote(got, got, ssem, rsem, mask - 1, (x, y, c)).wait_recv()
        for mask in range(1, NDEV):
            sent = _small_rows(p_ref, _dev_index(_peer(x, y, c, mask)))
            _remote(sent, sent, ssem, rsem, mask - 1, (x, y, c)).wait_send()

    return _pcall(body, name="rs_wait", in_specs=[HBM, HBM, SEM, SEM, ANY], out_specs=[HBM, HBM],
                  out_shape=_hbm_like((pack, land)), aliases={0: 0, 1: 1}, effect=True)(pack, land, ssem, rsem, after)


def _small_sum(me1, pack, land):
    R = pack.shape[0]
    r8 = R // NDEV

    def body(me_r, p_ref, ld, full):
        acc = p_ref[...]
        for k in range(NDEV - 1):
            acc = acc + ld[k]
        full[...] = acc

    own = pl.BlockSpec((r8, 128), lambda i, m: (m[0], 0))
    return _pcall(body, name="small_sum", grid=(1,), prefetch=1,
                  in_specs=[own, pl.BlockSpec((NDEV - 1, r8, 128), lambda i, m: (0, 0, 0))], out_specs=own,
                  out_shape=SDS((R, 128), F32), vmem_mb=32)(me1, pack, land)


def _ag_start(full):
    def body(f_ref, ssem, rsem, o0, token):
        x, y, c = _coords()
        mine = _small_rows(f_ref, _dev_index((x, y, c)))
        for mask in range(1, NDEV):
            _remote(mine, mine, ssem, rsem, mask - 1, _peer(x, y, c, mask)).start()
        token[...] = jnp.zeros((8, 128), F32)

    return _pcall(
        body, name="ag_start", in_specs=[HBM], out_specs=[SEM, SEM, HBM, VMEM],
        out_shape=[pltpu.SemaphoreType.DMA((NDEV - 1,)), pltpu.SemaphoreType.DMA((NDEV - 1,))] + _hbm_like((full,)) + [TOKEN],
        aliases={0: 2}, effect=True,
    )(pltpu.with_memory_space_constraint(full, pltpu.HBM))


def _ag_wait(started, after):
    ssem, rsem, full, _ = started

    def body(f_ref, ssem, rsem, after_r, o0):
        x, y, c = _coords()
        mine = _small_rows(f_ref, _dev_index((x, y, c)))
        for mask in range(1, NDEV):
            got = _small_rows(f_ref, _dev_index(_peer(x, y, c, mask)))
            _remote(got, got, ssem, rsem, mask - 1, (x, y, c)).wait_recv()
            _remote(mine, mine, ssem, rsem, mask - 1, (x, y, c)).wait_send()

    return _pcall(body, name="ag_wait", in_specs=[HBM, SEM, SEM, ANY], out_specs=[HBM], out_shape=_hbm_like((full,)),
                  aliases={0: 0}, effect=True)(full, ssem, rsem, after)[0]


def _mm_in(x, g, w):
    S = x.shape[0]
    tm, tn = min(1024, S), 1280

    def body(x_ref, g_ref, w_ref, z_ref, h_ref, hs):
        @pl.when(pl.program_id(1) == 0)
        def _():
            xv = x_ref[...]
            hb = (xv * _rms_scale(xv) * g_ref[...]).astype(BF16)
            hs[...] = hb
            h_ref[...] = hb

        z_ref[...] = _dot(hs[...], w_ref[...]).astype(BF16)

    return _pcall(
        body, name="mm_in", grid=(S // tm, NIN // tn),
        in_specs=[pl.BlockSpec((tm, D), lambda i, j: (i, 0)), _full((1, D)), pl.BlockSpec((D, tn), lambda i, j: (0, j))],
        out_specs=[pl.BlockSpec((tm, tn), lambda i, j: (i, j)), pl.BlockSpec((tm, D), lambda i, j: (i, 0))],
        out_shape=[SDS((S, NIN), BF16), SDS((S, D), BF16)], scratch=[pltpu.VMEM((tm, D), BF16)],
    )(x, g, w)


def _chunk_mask():
    ri = lax.broadcasted_iota(jnp.int32, (GB, GB), 0)
    ci = lax.broadcasted_iota(jnp.int32, (GB, GB), 1)
    return (ri >= 64) | (ci < 64)


def _layernorm_parts(v):
    mu = jnp.mean(v, axis=-1, keepdims=True)
    d = v - mu
    rs = lax.rsqrt(jnp.mean(d * d, axis=-1, keepdims=True) + EPS)
    return d * rs, rs


def _branch_a(z, lg, lb, ws, bsb):
    S = z.shape[0]
    T = min(512, S)

    def body(zu, zv, zg, lg_r, lb_r, ws_r, bs_r, ya):
        vhat, _ = _layernorm_parts(zv[...].astype(F32))
        vnb = (vhat * lg_r[...] + lb_r[...]).astype(BF16)
        sil, _ = _silu_parts(zg[...].astype(F32))
        t = zu[...].astype(F32) * sil
        mask = _chunk_mask()
        for g in range(NG):
            wg = jnp.where(mask, ws_r[g], 0.0).astype(BF16)
            cs = slice(g * GB, (g + 1) * GB)
            for n in range(T // GB):
                rs = slice(n * GB, (n + 1) * GB)
                sv = _dot(wg, vnb[rs, cs]) + bs_r[g]
                ya[rs, cs] = (t[rs, cs] * sv).astype(BF16)

    zs = lambda k: pl.BlockSpec((T, D), lambda i: (i, k))
    return _pcall(
        body, name="branch_a", grid=(S // T,),
        in_specs=[zs(0), zs(1), zs(2), _full((1, D)), _full((1, D)), _full((NG, GB, GB)), _full((NG, GB, GB))],
        out_specs=pl.BlockSpec((T, D), lambda i: (i, 0)), out_shape=SDS((S, D), BF16),
    )(z, z, z, lg, lb, ws, bsb)


def _softplus_neg(lam):
    e = jnp.exp(-jnp.abs(lam))
    l1p = jnp.where(e < 1e-2, e * (1.0 - e * (0.5 - e * (1.0 / 3.0))), jnp.log(1.0 + e))
    return jnp.maximum(-lam, 0.0) + l1p


CH = 16


def _ck(c, off=0):
    return pl.ds(c * CH + off, CH)


def _half_sum(v):
    return v[0:8, :] + v[8:16, :]


def _lru_conv(xpad, c, cw, cb):
    xk = [xpad[_ck(c, 5 + k), :] for k in range(4)]
    return xk, cb + (((xk[0] * cw[0:1] + xk[1] * cw[1:2]) + xk[2] * cw[2:3]) + xk[3] * cw[3:4])


def _lru_gate_matmuls(xcb_s, wr_r, wi_r, pr_s, pi_s):
    for h in range(NG):
        cs = slice(h * GB, (h + 1) * GB)
        pr_s[:, cs] = _dot(xcb_s[:, cs], wr_r[h].astype(BF16))
        pi_s[:, cs] = _dot(xcb_s[:, cs], wi_r[h].astype(BF16))


def _lru_gates(pr, pi, br, bi, sp8):
    r = jax.nn.sigmoid(pr + br)
    ig = _sigmoid(pi + bi)
    la = sp8 * r
    a = jnp.exp(la)
    a2 = a * a
    mult = jnp.sqrt(-jnp.tanh(la) * (a2 + 1.0))
    return r, ig, a, a2, mult


def _tile_rows():
    return lax.broadcasted_iota(jnp.int32, (8, D), 0)


def _scan_forward(a_s, u_s, h_s, hcar, T):
    row = _tile_rows()

    def tile(i, hp):
        o = pl.multiple_of(i * 8, 8)
        A = a_s[pl.ds(o, 8), :]
        U = u_s[pl.ds(o, 8), :]
        for s in (1, 2, 4):
            m = row >= s
            U = jnp.where(m, U + A * pltpu.roll(U, s, 0), U)
            A = jnp.where(m, A * pltpu.roll(A, s, 0), A)
        H = U + A * hp
        h_s[pl.ds(o, 8), :] = H
        return jnp.broadcast_to(H[7:8, :], (8, D))

    hcar[...] = lax.fori_loop(0, T // 8, tile, hcar[...])


def _scan_reverse(b_s, d_s, l_s, lcar, T):
    row = _tile_rows()
    n = T // 8

    def tile(i, lp):
        o = pl.multiple_of((n - 1 - i) * 8, 8)
        B = b_s[pl.ds(o, 8), :]
        U = d_s[pl.ds(o, 8), :]
        for s in (1, 2, 4):
            m = row < 8 - s
            U = jnp.where(m, U + B * pltpu.roll(U, 8 - s, 0), U)
            B = jnp.where(m, B * pltpu.roll(B, 8 - s, 0), B)
        Lm = U + B * lp
        l_s[pl.ds(o, 8), :] = Lm
        return jnp.broadcast_to(Lm[0:1, :], (8, D))

    lcar[...] = lax.fori_loop(0, n, tile, lcar[...])


def _branch_b(z, cw, cb, wr, br, wi, bi, lam):
    S = z.shape[0]
    T = min(256, S)

    def body(zxb, zgb, cw_r, cb_r, wr_r, br_r, wi_r, bi_r, lam_r, yb, hs_o, xpad, a_s, u_s, hcar):
        @pl.when(pl.program_id(0) == 0)
        def _():
            xpad[pl.ds(0, 8), :] = jnp.zeros((8, D), F32)
            hcar[...] = jnp.zeros((8, D), F32)

        cw = cw_r[...]
        xpad[pl.ds(8, T), :] = zxb[...].astype(F32)
        xk = [xpad[pl.ds(5 + k, T), :] for k in range(4)]
        xc = cb_r[...] + (((xk[0] * cw[0:1] + xk[1] * cw[1:2]) + xk[2] * cw[2:3]) + xk[3] * cw[3:4])
        xcb = xc.astype(BF16)
        pr, pi = [], []
        for h in range(NG):
            cs = slice(h * GB, (h + 1) * GB)
            pr.append(_dot(xcb[:, cs], wr_r[h].astype(BF16)))
            pi.append(_dot(xcb[:, cs], wi_r[h].astype(BF16)))
        _, ig, a, _, mult = _lru_gates(jnp.concatenate(pr, axis=1), jnp.concatenate(pi, axis=1), br_r[...], bi_r[...],
                                       -LRU_C * _softplus_neg(lam_r[...]))
        a_s[...] = a
        u_s[...] = mult * (ig * xc)
        _scan_forward(a_s, u_s, hs_o, hcar, T)
        xpad[pl.ds(0, 8), :] = xpad[pl.ds(T, 8), :]
        sil, _ = _silu_parts(zgb[...].astype(F32))
        yb[...] = (hs_o[...] * sil).astype(BF16)

    zs = lambda k: pl.BlockSpec((T, D), lambda i: (i, k))
    row = pl.BlockSpec((T, D), lambda i: (i, 0))
    return _pcall(
        body, name="branch_b", grid=(S // T,),
        in_specs=[zs(3), zs(4), _full((8, D)), _full((1, D)), _full((NG, GB, GB)), _full((1, D)), _full((NG, GB, GB)),
                  _full((1, D)), _full((1, D))],
        out_specs=[row, row], out_shape=[SDS((S, D), BF16), SDS((S, D), F32)],
        scratch=[pltpu.VMEM((T + 8, D), F32), pltpu.VMEM((T, D), F32), pltpu.VMEM((T, D), F32), pltpu.VMEM((8, D), F32)],
    )(z, z, cw, cb, wr, br, wi, bi, lam)


def _kv(mem, g, wkv):
    def body(m_ref, g_ref, w_ref, kv_ref):
        m = m_ref[...]
        mn = (m * _rms_scale(m) * g_ref[...]).astype(BF16)
        kv_ref[...] = _dot(mn, w_ref[...]).astype(BF16)

    return _pcall(body, name="mem_kv", in_specs=[VMEM] * 3, out_specs=VMEM, out_shape=SDS((MEM, 2 * D), BF16),
                  vmem_mb=32)(mem, g, wkv)


def _softmax_rows(s):
    e = jnp.exp(s - jnp.max(s, axis=-1, keepdims=True))
    return e / jnp.sum(e, axis=-1, keepdims=True)


def _branch_c(z, kv):
    S = z.shape[0]
    T = min(512, S)

    def body(zq, zg, kv_r, yc):
        sil, _ = _silu_parts(zg[...].astype(F32))
        for h in range(NH):
            cs = slice(h * HD, (h + 1) * HD)
            p = _softmax_rows(_dot_nt(zq[:, cs], kv_r[:, cs]) * (HD ** -0.5))
            att = _dot(p.astype(BF16), kv_r[:, D + h * HD:D + (h + 1) * HD])
            yc[:, cs] = (att * sil[:, cs]).astype(BF16)

    zs = lambda k: pl.BlockSpec((T, D), lambda i: (i, k))
    return _pcall(body, name="branch_c", grid=(S // T,), in_specs=[zs(5), zs(6), _full((MEM, 2 * D))],
                  out_specs=pl.BlockSpec((T, D), lambda i: (i, 0)), out_shape=SDS((S, D), BF16))(z, z, kv)


def _merge_out(ya, yb, yc, z, wp, x, pg):
    S = x.shape[0]
    T = min(256, S)

    def body(ya_r, yb_r, yc_r, m0, m1, m2, wp_r, x_r, pg_r, pa_o, pb_o, pc_o, mg_o, o_o, xn_o):
        merged = None
        for y_r, ml, p_o, k in ((ya_r, m0, pa_o, 0), (yb_r, m1, pb_o, 1), (yc_r, m2, pc_o, 2)):
            p = _dot(y_r[...], wp_r[k])
            p_o[...] = p.astype(BF16)
            t = _sigmoid(ml[...].astype(F32)) * p
            merged = t if merged is None else merged + t
        mb = merged.astype(BF16)
        mg_o[...] = mb
        o = _dot(mb, wp_r[3])
        o_o[...] = o.astype(BF16)
        xn_o[...] = x_r[...] + o * _rms_scale(o) * pg_r[...]

    row = pl.BlockSpec((T, D), lambda i: (i, 0))
    zs = lambda k: pl.BlockSpec((T, D), lambda i: (i, k))
    return _pcall(
        body, name="merge_out", grid=(S // T,),
        in_specs=[row, row, row, zs(7), zs(8), zs(9), _full((4, D, D)), row, _full((1, D))],
        out_specs=[row] * 6, out_shape=[SDS((S, D), BF16)] * 5 + [SDS((S, D), F32)], vmem_mb=56,
    )(ya, yb, yc, z, z, z, wp, x, pg)


def _loss_head(y, t):
    S = y.shape[0]
    T = min(512, S)

    def body(y_r, t_r, loss_o, dy_o):
        @pl.when(pl.program_id(0) == 0)
        def _():
            loss_o[...] = jnp.zeros((1, 1), F32)

        e = y_r[...] - t_r[...]
        dy_o[...] = e * (1.0 / D)
        loss_o[...] += 0.5 * _rowsum(jnp.sum(e * e, axis=1, keepdims=True) * (1.0 / D))

    row = pl.BlockSpec((T, D), lambda i: (i, 0))
    return _pcall(body, name="loss_head", grid=(S // T,), in_specs=[row, row], out_specs=[_full((1, 1)), row],
                  out_shape=[SDS((1, 1), F32), SDS((S, D), F32)])(y, t)


def _accumulate(first, ref, val):
    @pl.when(first)
    def _():
        ref[...] = val

    @pl.when(jnp.logical_not(first))
    def _():
        ref[...] += val


def _out_bwd(dxn, o, pg, wp, z, pa, pb, pc):
    S = dxn.shape[0]
    T = min(256, S)

    def body(dy_r, o_r, pg_r, wp_r, m0, m1, m2, pa_r, pb_r, pc_r, do_o, dpa_o, dpb_o, dpc_o, dya_o, dyb_o, dyc_o, dz_o, dg_o):
        dy = dy_r[...]
        o = o_r[...].astype(F32)
        r2 = _rms_scale(o)
        w = dy * pg_r[...]
        do = r2 * w - o * (r2 * r2 * r2) * jnp.mean(w * o, axis=-1, keepdims=True)
        _accumulate(pl.program_id(0) == 0, dg_o, _rowsum(dy * o * r2))
        dob = do.astype(BF16)
        do_o[...] = dob
        dm = _dot_nt(dob, wp_r[3])
        for k, (ml, p_r, dp_o, dy_o) in enumerate(((m0, pa_r, dpa_o, dya_o), (m1, pb_r, dpb_o, dyb_o), (m2, pc_r, dpc_o, dyc_o))):
            gk = _sigmoid(ml[...].astype(F32))
            dz_o[k] = (dm * p_r[...].astype(F32) * gk * (1.0 - gk)).astype(BF16)
            dpk = (gk * dm).astype(BF16)
            dp_o[...] = dpk
            dy_o[...] = _dot_nt(dpk, wp_r[k]).astype(BF16)

    row = pl.BlockSpec((T, D), lambda i: (i, 0))
    zs = lambda k: pl.BlockSpec((T, D), lambda i: (i, k))
    return _pcall(
        body, name="out_bwd", grid=(S // T,),
        in_specs=[row, row, _full((1, D)), _full((4, D, D)), zs(7), zs(8), zs(9), row, row, row],
        out_specs=[row] * 7 + [pl.BlockSpec((3, T, D), lambda i: (1, i, 0)), _full((1, D))],
        out_shape=[SDS((S, D), BF16)] * 7 + [SDS((10, S, D), BF16), SDS((1, D), F32)], vmem_mb=56,
    )(dxn, o, pg, wp, z, z, z, pa, pb, pc)


def _branch_a_bwd(z, dya, lg, lb, ws, bsb, dz):
    S = z.shape[0]
    T = min(512, S)
    nblk = S // T

    def body(zu, zv, zg, dy_r, lg_r, lb_r, ws_r, bs_r, dz_in, dz_o, dws_o, dbs_o, dlg_o, dlb_o, dvn_s, bacc):
        i = pl.program_id(0)

        @pl.when(i == 0)
        def _():
            dws_o[...] = jnp.zeros((NG, GB, GB), F32)
            bacc[...] = jnp.zeros((NG, GB, GB), F32)

        vhat, rs = _layernorm_parts(zv[...].astype(F32))
        vnb = (vhat * lg_r[...] + lb_r[...]).astype(BF16)
        ga = zg[...].astype(F32)
        sil, dsil = _silu_parts(ga)
        u = zu[...].astype(F32)
        dy = dy_r[...].astype(F32)
        t = dy * sil
        dsv_all = t * u
        dga_pre = dy * u * dsil
        mask = _chunk_mask()
        for g in range(NG):
            wf = jnp.where(mask, ws_r[g], 0.0)
            wg = wf.astype(BF16)
            wgt = wf.T.astype(BF16)
            cs = slice(g * GB, (g + 1) * GB)
            dw = jnp.zeros((GB, GB), F32)
            db = jnp.zeros((GB, GB), F32)
            for n in range(T // GB):
                rsl = slice(n * GB, (n + 1) * GB)
                vb = vnb[rsl, cs]
                sv = _dot(wg, vb) + bs_r[g]
                dz_o[0, rsl, cs] = (t[rsl, cs] * sv).astype(BF16)
                dz_o[2, rsl, cs] = (dga_pre[rsl, cs] * sv).astype(BF16)
                dsv = dsv_all[rsl, cs]
                dsb = dsv.astype(BF16)
                dvn_s[rsl, cs] = _dot(wgt, dsb)
                dw = dw + _dot_nt(dsb, vb)
                db = db + dsv
            dws_o[g] += jnp.where(mask, dw, 0.0)
            bacc[g] += db
        dvn = dvn_s[...]
        dvh = dvn * lg_r[...]
        dv = rs * (dvh - jnp.mean(dvh, axis=-1, keepdims=True) - vhat * jnp.mean(dvh * vhat, axis=-1, keepdims=True))
        dz_o[1] = dv.astype(BF16)
        _accumulate(i == 0, dlg_o, _rowsum(dvn * vhat))
        _accumulate(i == 0, dlb_o, _rowsum(dvn))

        @pl.when(i == nblk - 1)
        def _():
            for g in range(NG):
                dbs_o[g:g + 1, :] = _rowsum(bacc[g].T)

    zs = lambda k: pl.BlockSpec((T, D), lambda i: (i, k))
    return _pcall(
        body, name="branch_a_bwd", grid=(nblk,),
        in_specs=[zs(0), zs(1), zs(2), pl.BlockSpec((T, D), lambda i: (i, 0)), _full((1, D)), _full((1, D)),
                  _full((NG, GB, GB)), _full((NG, GB, GB)), HBM],
        out_specs=[pl.BlockSpec((3, T, D), lambda i: (0, i, 0)), _full((NG, GB, GB)), _full((NG, GB)), _full((1, D)),
                   _full((1, D))],
        out_shape=[SDS((10, S, D), BF16), SDS((NG, GB, GB), F32), SDS((NG, GB), F32), SDS((1, D), F32), SDS((1, D), F32)],
        scratch=[pltpu.VMEM((T, D), F32), pltpu.VMEM((NG, GB, GB), F32)], aliases={8: 0},
    )(z, z, z, dya, lg, lb, ws, bsb, dz)


def _branch_b_bwd(z, hs, dyb, cw, cb, wr, br, wi, bi, lam, dz):
    S = z.shape[0]
    T = min(256, S)
    nblk = S // T

    def body(zxb, zprev, zgb, hs_r, hprev_r, dy_r, cw_r, cb_r, wr_r, br_r, wi_r, bi_r, lam_r, dz_in,
             dz_o, dcw_o, dcb_o, dwr_o, dbr_o, dwi_o, dbi_o, dlam_o, xpad, hpad, apad, dpad, xc_s, pr_s, pi_s, r_s, ig_s, m_s,
             b_s, d_s, l_s, back_s, xcb_s, dprb_s, dpib_s, lcar):
        i = pl.program_id(0)
        blk = nblk - 1 - i
        first = i == 0

        @pl.when(first)
        def _():
            apad[pl.ds(T, 8), :] = jnp.zeros((8, D), F32)
            dpad[pl.ds(T, 8), :] = jnp.zeros((8, D), F32)
            lcar[...] = jnp.zeros((8, D), F32)
            dcw_o[...] = jnp.zeros((8, D), F32)
            dwr_o[...] = jnp.zeros((NG, GB, GB), F32)
            dwi_o[...] = jnp.zeros((NG, GB, GB), F32)

        keep = (blk > 0).astype(F32)
        nck = T // CH
        cw, cb, br, bi, lam = cw_r[...], cb_r[...], br_r[...], bi_r[...], lam_r[...]
        sp8 = -LRU_C * _softplus_neg(lam)
        xpad[pl.ds(0, 8), :] = zprev[...].astype(F32)[8:16, :] * keep
        hpad[pl.ds(0, 8), :] = hprev_r[...] * keep
        for c in range(nck):
            xpad[_ck(c, 8), :] = zxb[_ck(c), :].astype(F32)
            hpad[_ck(c, 8), :] = hs_r[_ck(c), :]
            _, xc = _lru_conv(xpad, c, cw, cb)
            xc_s[_ck(c), :] = xc
            xcb_s[_ck(c), :] = xc.astype(BF16)
        _lru_gate_matmuls(xcb_s, wr_r, wi_r, pr_s, pi_s)
        for c in range(nck):
            r, ig, a, _, mult = _lru_gates(pr_s[_ck(c), :], pi_s[_ck(c), :], br, bi, sp8)
            r_s[_ck(c), :] = r
            ig_s[_ck(c), :] = ig
            m_s[_ck(c), :] = mult
            apad[_ck(c), :] = a
            sil, dsil = _silu_parts(zgb[_ck(c), :].astype(F32))
            dy = dy_r[_ck(c), :].astype(F32)
            dz_o[1, _ck(c), :] = (dy * hs_r[_ck(c), :] * dsil).astype(BF16)
            d_s[_ck(c), :] = dy * sil
        for c in range(nck):
            b_s[_ck(c), :] = apad[_ck(c, 1), :]
        _scan_reverse(b_s, d_s, l_s, lcar, T)
        s_sp = s_br = s_bi = jnp.zeros((8, D), F32)
        for c in range(nck):
            lm, r, ig, mult, a, xc = l_s[_ck(c), :], r_s[_ck(c), :], ig_s[_ck(c), :], m_s[_ck(c), :], apad[_ck(c), :], xc_s[_ck(c), :]
            t = lm * mult
            dpad[_ck(c), :] = t * ig
            dl = lm * hpad[_ck(c, 7), :] * a - (lm * ig * xc) * (a * a) / mult
            dpr = dl * sp8 * r * (1.0 - r)
            dpi = t * xc * ig * (1.0 - ig)
            s_sp = s_sp + _half_sum(dl * r)
            s_br = s_br + _half_sum(dpr)
            s_bi = s_bi + _half_sum(dpi)
            dprb_s[_ck(c), :] = dpr.astype(BF16)
            dpib_s[_ck(c), :] = dpi.astype(BF16)
        _accumulate(first, dlam_o, _rowsum(s_sp) * (LRU_C * jax.nn.sigmoid(-lam)))
        _accumulate(first, dbr_o, _rowsum(s_br))
        _accumulate(first, dbi_o, _rowsum(s_bi))
        for h in range(NG):
            cs = slice(h * GB, (h + 1) * GB)
            back_s[:, cs] = _dot_nt(dprb_s[:, cs], wr_r[h].astype(BF16)) + _dot_nt(dpib_s[:, cs], wi_r[h].astype(BF16))
            dwr_o[h] += _dot_tn(xcb_s[:, cs], dprb_s[:, cs])
            dwi_o[h] += _dot_tn(xcb_s[:, cs], dpib_s[:, cs])
        s_cb = jnp.zeros((8, D), F32)
        s_cw = [jnp.zeros((8, D), F32)] * 4
        for c in range(nck):
            dxc = dpad[_ck(c), :] + back_s[_ck(c), :]
            dpad[_ck(c), :] = dxc
            s_cb = s_cb + _half_sum(dxc)
            s_cw = [s_cw[k] + _half_sum(xpad[_ck(c, 5 + k), :] * dxc) for k in range(4)]
        _accumulate(first, dcb_o, _rowsum(s_cb))
        for k in range(4):
            dcw_o[k:k + 1, :] += _rowsum(s_cw[k])
        for c in range(nck):
            dxb = ((dpad[_ck(c, 3), :] * cw[0:1] + dpad[_ck(c, 2), :] * cw[1:2]) + dpad[_ck(c, 1), :] * cw[2:3]) + dpad[_ck(c), :] * cw[3:4]
            dz_o[0, _ck(c), :] = dxb.astype(BF16)
        apad[pl.ds(T, 8), :] = apad[pl.ds(0, 8), :]
        dpad[pl.ds(T, 8), :] = dpad[pl.ds(0, 8), :]

    rev = lambda k: pl.BlockSpec((T, D), lambda i: (nblk - 1 - i, k))
    prev16 = pl.BlockSpec((16, D), lambda i: (jnp.maximum((nblk - 1 - i) * (T // 16) - 1, 0), 3))
    prev8 = pl.BlockSpec((8, D), lambda i: (jnp.maximum((nblk - 1 - i) * (T // 8) - 1, 0), 0))
    vec, mat = _full((1, D)), _full((NG, GB, GB))
    return _pcall(
        body, name="branch_b_bwd", grid=(nblk,),
        in_specs=[rev(3), prev16, rev(4), rev(0), prev8, rev(0), _full((8, D)), vec, mat, vec, mat, vec, vec, HBM],
        out_specs=[pl.BlockSpec((2, T, D), lambda i: (3, nblk - 1 - i, 0)), _full((8, D)), vec, mat, vec, mat, vec, vec],
        out_shape=[SDS((10, S, D), BF16), SDS((8, D), F32), SDS((1, D), F32), SDS((NG, GB, GB), F32), SDS((1, D), F32),
                   SDS((NG, GB, GB), F32), SDS((1, D), F32), SDS((1, D), F32)],
        scratch=[pltpu.VMEM((T + 8, D), F32)] * 4 + [pltpu.VMEM((T, D), F32)] * 10 + [pltpu.VMEM((T, D), BF16)] * 3
        + [pltpu.VMEM((8, D), F32)],
        aliases={13: 0}, vmem_mb=56,
    )(z, z, z, hs, hs, dyb, cw, cb, wr, br, wi, bi, lam, dz)


def _branch_c_bwd(z, kv, dyc, dz):
    S = z.shape[0]
    T = min(512, S)

    def body(zq, zg, kv_r, dy_r, dz_in, dz_o, dkv_o):
        @pl.when(pl.program_id(0) == 0)
        def _():
            dkv_o[...] = jnp.zeros((MEM, 2 * D), F32)

        gc = zg[...].astype(F32)
        sil, dsil = _silu_parts(gc)
        dy = dy_r[...].astype(F32)
        datt = dy * sil
        dgc_pre = dy * dsil
        scale = HD ** -0.5
        for h in range(NH):
            cs = slice(h * HD, (h + 1) * HD)
            vs = slice(D + h * HD, D + (h + 1) * HD)
            qh = zq[:, cs]
            p = _softmax_rows(_dot_nt(qh, kv_r[:, cs]) * scale)
            pb = p.astype(BF16)
            att = _dot(pb, kv_r[:, vs])
            dz_o[1, :, cs] = (dgc_pre[:, cs] * att).astype(BF16)
            dab = datt[:, cs].astype(BF16)
            dp = _dot_nt(dab, kv_r[:, vs])
            ds = (p * (dp - jnp.sum(p * dp, axis=-1, keepdims=True)) * scale).astype(BF16)
            dz_o[0, :, cs] = _dot(ds, kv_r[:, cs]).astype(BF16)
            dkv_o[:, cs] += _dot_tn(ds, qh)
            dkv_o[:, vs] += _dot_tn(pb, dab)

    zs = lambda k: pl.BlockSpec((T, D), lambda i: (i, k))
    return _pcall(
        body, name="branch_c_bwd", grid=(S // T,),
        in_specs=[zs(5), zs(6), _full((MEM, 2 * D)), pl.BlockSpec((T, D), lambda i: (i, 0)), HBM],
        out_specs=[pl.BlockSpec((2, T, D), lambda i: (4, i, 0)), _full((MEM, 2 * D))],
        out_shape=[SDS((10, S, D), BF16), SDS((MEM, 2 * D), F32)], aliases={4: 0},
    )(z, z, kv, dyc, dz)


def _mm_dh(dz, w, x, dxn, g):
    S = x.shape[0]
    tm = min(1024, S)

    def body(dz_r, w_r, x_r, dxn_r, g_r, dx_o, dg_o, acc):
        i, k = pl.program_id(0), pl.program_id(1)
        _accumulate(k == 0, acc, _dot_nt(dz_r[0], w_r[...]))

        @pl.when(k == 9)
        def _():
            dh = acc[...]
            xv = x_r[...]
            r1 = _rms_scale(xv)
            wv = dh * g_r[...]
            dx_o[...] = dxn_r[...] + r1 * wv - xv * (r1 * r1 * r1) * jnp.mean(wv * xv, axis=-1, keepdims=True)
            _accumulate(i == 0, dg_o, _rowsum(dh * xv * r1))

    row = pl.BlockSpec((tm, D), lambda i, k: (i, 0))
    return _pcall(
        body, name="mm_dh", grid=(S // tm, 10),
        in_specs=[pl.BlockSpec((1, tm, D), lambda i, k: (k, i, 0)), pl.BlockSpec((D, D), lambda i, k: (0, _dz_col(k))),
                  row, row, _full((1, D))],
        out_specs=[row, _full((1, D))], out_shape=[SDS((S, D), F32), SDS((1, D), F32)],
        scratch=[pltpu.VMEM((tm, D), F32)],
    )(dz, w, x, dxn, g)


def _mm_dwin(h, dz):
    S = h.shape[0]
    tk = min(1024, S)
    nk = S // tk

    def body(h_r, dz_r, o_r, acc):
        k = pl.program_id(1)
        _accumulate(k == 0, acc, _dot_tn(h_r[...], dz_r[0]))

        @pl.when(k == nk - 1)
        def _():
            o_r[...] = acc[...].astype(BF16)

    return _pcall(
        body, name="mm_dwin", grid=(10, nk),
        in_specs=[pl.BlockSpec((tk, D), lambda n, k: (k, 0)), pl.BlockSpec((1, tk, D), lambda n, k: (n, k, 0))],
        out_specs=pl.BlockSpec((D, D), lambda n, k: (0, _dz_col(n))), out_shape=SDS((D, NIN), BF16),
        scratch=[pltpu.VMEM((D, D), F32)],
    )(h, dz)


def _mm_tn4(a4, b4):
    S = a4[0].shape[0]
    tk = min(1024, S)
    nk = S // tk

    def body(*refs):
        a_r, b_r, o_r, acc = refs[0:4], refs[4:8], refs[8], refs[9]
        w, k = pl.program_id(0), pl.program_id(1)
        for a in range(4):
            @pl.when(w == a)
            def _(a=a):
                _accumulate(k == 0, acc, _dot_tn(a_r[a][...], b_r[a][...]))

        @pl.when(k == nk - 1)
        def _():
            o_r[...] = acc[...].astype(BF16)

    def blk(a):
        return pl.BlockSpec((tk, D), lambda w, k: (jnp.where(w == a, k, jnp.where(w < a, 0, nk - 1)), 0))

    return _pcall(body, name="mm_tn4", grid=(4, nk), in_specs=[blk(a) for a in range(4)] * 2,
                  out_specs=pl.BlockSpec((None, D, D), lambda w, k: (w, 0, 0)), out_shape=SDS((4, D, D), BF16),
                  scratch=[pltpu.VMEM((D, D), F32)])(*a4, *b4)


def _mem_bwd(mem, g, wkv, dkv, dg_acc):
    def body(m_ref, g_ref, w_ref, dkv_ref, acc_ref, dw_ref, dg_ref):
        m = m_ref[...]
        mr = m * _rms_scale(m)
        mn = (mr * g_ref[...]).astype(BF16)
        dkb = dkv_ref[...].astype(BF16)
        dw_ref[...] = _dot_tn(mn, dkb).astype(BF16)
        dg_ref[...] = acc_ref[...] + _rowsum(_dot_nt(dkb, w_ref[...]) * mr)

    return _pcall(body, name="mem_bwd", in_specs=[VMEM] * 5, out_specs=[VMEM] * 2,
                  out_shape=[SDS((D, 2 * D), BF16), SDS((1, D), F32)], vmem_mb=48)(mem, g, wkv, dkv, dg_acc)


def _adamw_math(w, g, m, v):
    m2 = ADAM_B1 * m + (1.0 - ADAM_B1) * g
    v2 = ADAM_B2 * v + (1.0 - ADAM_B2) * (g * g)
    mh = m2 / (1.0 - ADAM_B1 ** ADAM_STEP)
    vh = v2 / (1.0 - ADAM_B2 ** ADAM_STEP)
    return -ADAM_LR * (mh / (jnp.sqrt(vh) + ADAM_EPS) + ADAM_WD * w), m2, v2


def _adamw_layer(l, w, m, v, g, prev, which=None, rows=256):
    L, R, C = w.shape

    def body(w_r, m_r, v_r, g_r, *rest):
        g_o, d_o, m_o, v_o = rest[-4:]
        g = g_r[...]
        d, m2, v2 = _adamw_math(w_r[...], g, m_r[...], v_r[...])
        g_o[...] = g
        d_o[...] = d
        m_o[...] = m2
        v_o[...] = v2

    st = pl.BlockSpec((None, rows, C), lambda i: (l, i, 0))
    gs = pl.BlockSpec((rows, C), lambda i: (i, 0)) if which is None else pl.BlockSpec((None, rows, C), lambda i: (which, i, 0))
    carried = list(prev) if prev is not None else []
    return _pcall(body, name="adamw_layer", grid=(R // rows,), in_specs=[st] * 3 + [gs] + [HBM] * len(carried),
                  out_specs=[st] * 4, out_shape=[SDS(w.shape, F32)] * 4, vmem_mb=56,
                  aliases={4 + k: k for k in range(len(carried))} or None)(w, m, v, g, *carried)


def _adamw_flat(w, m, v, g, rows):
    R, C = w.shape

    def body(w_r, m_r, v_r, g_r, d_o, m_o, v_o):
        d, m2, v2 = _adamw_math(w_r[...], g_r[...], m_r[...], v_r[...])
        d_o[...] = d
        m_o[...] = m2
        v_o[...] = v2

    blk = pl.BlockSpec((rows, C), lambda i: (i, 0))
    return _pcall(body, name="adamw_flat", grid=(R // rows,), in_specs=[blk] * 4, out_specs=[blk] * 3,
                  out_shape=[SDS((R, C), F32)] * 3)(w, m, v, g)


_SMALL = ("mem_norm_g", "pre_norm_g", "post_norm_g", "gmlp_ln_g", "gmlp_ln_b", "gmlp_ws", "gmlp_bs", "conv_b", "lru_wr",
          "lru_br", "lru_wi", "lru_bi", "lru_lambda")


def _pack_small(parts, conv_w_part):
    rows = [parts[n].reshape(-1, 128) for n in _SMALL] + [conv_w_part.reshape(-1, 128)]
    used = sum(r.shape[0] for r in rows)
    rows.append(jnp.zeros((SMALL_ROWS - used, 128), F32))
    return jnp.concatenate(rows, axis=0)


def _unpack_small(pack, shapes):
    out, at = {}, 0
    for n in _SMALL:
        size = 1
        for s in shapes[n]:
            size *= s
        out[n] = pack[at:at + size // 128].reshape(shapes[n])
        at += size // 128
    return out, at


def kernel(x, mem, mem_norm_g, pre_norm_g, post_norm_g, w_in, gmlp_ln_g, gmlp_ln_b, gmlp_ws, gmlp_bs, conv_w, conv_b, lru_wr, lru_br, lru_wi, lru_bi, lru_lambda, w_kv, w_pa, w_pb, w_pc, w_out, loss_target, m_mem_norm_g, m_pre_norm_g, m_post_norm_g, m_w_in, m_gmlp_ln_g, m_gmlp_ln_b, m_gmlp_ws, m_gmlp_bs, m_conv_w, m_conv_b, m_lru_wr, m_lru_br, m_lru_wi, m_lru_bi, m_lru_lambda, m_w_kv, m_w_pa, m_w_pb, m_w_pc, m_w_out, v_mem_norm_g, v_pre_norm_g, v_post_norm_g, v_w_in, v_gmlp_ln_g, v_gmlp_ln_b, v_gmlp_ws, v_gmlp_bs, v_conv_w, v_conv_b, v_lru_wr, v_lru_br, v_lru_wi, v_lru_bi, v_lru_lambda, v_w_kv, v_w_pa, v_w_pb, v_w_pc, v_w_out):
    L = w_in.shape[0]
    S = x.shape[1]
    xs = [x[0]]
    mem2 = mem[0]
    mg = mem_norm_g.reshape(1, D)
    vec = lambda a, l: a[l].reshape(1, D)
    ci = lax.axis_index("c")
    jpos = 2 * lax.axis_index("x") + lax.axis_index("y")
    pos = jnp.reshape(jpos, (1,)).astype(jnp.int32)
    pos2 = jnp.stack([jpos, ci]).astype(jnp.int32)

    cw8 = jnp.pad(conv_w, ((0, 0), (0, 4), (0, 0)))
    placed = [_cast_place(l, pos, w_in, w_kv, w_pa, w_pb, w_pc, w_out, cw8) for l in range(L)]
    W = [None] * L
    started = _gather_start(0, placed[0])
    mid = _gather_mid(0, started, started[-1])
    started = _gather_start(1, placed[1], mid[-1])
    W[0] = _gather_end(0, mid, started[-1])

    saved = []
    for l in range(L):
        Win, Wkv, Wp, Cw = W[l]
        z, h = _mm_in(xs[l], vec(pre_norm_g, l), Win)
        bsb = jnp.broadcast_to(gmlp_bs[l][:, :, None], (NG, GB, GB))
        ya = _branch_a(z, vec(gmlp_ln_g, l), vec(gmlp_ln_b, l), gmlp_ws[l], bsb)
        yb, hs = _branch_b(z, Cw, vec(conv_b, l), lru_wr[l], vec(lru_br, l), lru_wi[l], vec(lru_bi, l), vec(lru_lambda, l))
        kv = _kv(mem2, mg, Wkv)
        yc = _branch_c(z, kv)
        pg = vec(post_norm_g, l)
        if l + 1 < L:
            mid = _gather_mid(l + 1, started, yc)
            if l + 2 < L:
                started = _gather_start(l + 2, placed[l + 2], mid[-1])
                pg = pg + started[-1][0, 0]
        pa, pb, pc, mgd, o, xn = _merge_out(ya, yb, yc, z, Wp, xs[l], pg)
        if l + 1 < L:
            W[l + 1] = _gather_end(l + 1, mid, xn)
        xs.append(xn)
        saved.append((z, h, ya, yb, yc, hs, kv, pa, pb, pc, mgd, o, bsb))

    loss11, dxn = _loss_head(xs[L], loss_target[0])
    loss = lax.psum(loss11[0, 0], ("x", "y", "c"))

    big = dict(w_in=(w_in, m_w_in, v_w_in), w_kv=(w_kv, m_w_kv, v_w_kv), w_pa=(w_pa, m_w_pa, v_w_pa),
               w_pb=(w_pb, m_w_pb, v_w_pb), w_pc=(w_pc, m_w_pc, v_w_pc), w_out=(w_out, m_w_out, v_w_out))
    out = {n: None for n in big}
    kin, nsh, nkv, rp = w_in.shape[1], w_in.shape[2], w_kv.shape[2], w_pa.shape[1]

    def finish_layer(l, a2a, after):
        g_p4, lp = _a2a_wait(f"a2a_p_wait_{l}", (2,), a2a[0], after)
        g_in, g_kv, lin, lkv = _a2a_wait(f"a2a_w_wait_{l}", (0, 1), a2a[1], after)
        g_in, g_kv, g_p = _sum_share(pos2, (lin, lkv, lp.reshape(NDEV - 1, 2 * rp, D)), (g_in, g_kv, g_p4))
        g_p = g_p.reshape(4, rp, D)
        out["w_in"] = _adamw_layer(l, *big["w_in"], g_in, out["w_in"])
        out["w_kv"] = _adamw_layer(l, *big["w_kv"], g_kv, out["w_kv"])
        for k, n in enumerate(("w_pa", "w_pb", "w_pc", "w_out")):
            out[n] = _adamw_layer(l, *big[n], g_p, out[n], which=k)

    small = {n: [None] * L for n in _SMALL}
    dconv_w = [None] * L
    dg_mem = jnp.zeros((1, D), F32)
    pending = None
    sent = []
    for l in reversed(range(L)):
        Win, Wkv, Wp, Cw = W[l]
        z, h, ya, yb, yc, hs, kv, pa, pb, pc, mgd, o, bsb = saved[l]
        pg = vec(post_norm_g, l) if pending is None else vec(post_norm_g, l) + pending[1][1][-1][0, 0]
        do, dpa, dpb, dpc, dya, dyb, dyc, dz, dgpost = _out_bwd(dxn, o, pg, Wp, z, pa, pb, pc)
        a2a_p = _a2a_start(f"a2a_p_start_{l}", (2,), (_mm_tn4((ya, yb, yc, mgd), (dpa, dpb, dpc, do)),),
                           (lax.empty((NDEV - 1, 4, rp // 2, D), BF16),))
        dz, dws, dbs, dlg, dlb = _branch_a_bwd(z, dya, vec(gmlp_ln_g, l) + a2a_p[-1][0, 0], vec(gmlp_ln_b, l), gmlp_ws[l], bsb, dz)
        dz, dcw, dcb, dwr, dbr, dwi, dbi, dlam = _branch_b_bwd(
            z, hs, dyb, Cw, vec(conv_b, l), lru_wr[l], vec(lru_br, l), lru_wi[l], vec(lru_bi, l), vec(lru_lambda, l), dz)
        dz, dkv = _branch_c_bwd(z, kv, dyc, dz)
        g_in = _mm_dwin(h, dz)
        g_kv, dg_mem = _mem_bwd(mem2, mg, Wkv, dkv, dg_mem)
        a2a_w = _a2a_start(f"a2a_w_start_{l}", (0, 1), (g_in, g_kv),
                           (lax.empty((NDEV - 1, kin // 2, nsh), BF16), lax.empty((NDEV - 1, kin // 2, nkv), BF16)))
        dx, dgpre = _mm_dh(dz, Win, xs[l], dxn, vec(pre_norm_g, l) + a2a_w[-1][0, 0])
        pending = (l, (a2a_p, a2a_w))
        sent.append(pending)
        for n, val in (("pre_norm_g", dgpre), ("post_norm_g", dgpost), ("gmlp_ln_g", dlg), ("gmlp_ln_b", dlb), ("gmlp_ws", dws),
                       ("gmlp_bs", dbs), ("conv_b", dcb), ("lru_wr", dwr), ("lru_br", dbr), ("lru_wi", dwi), ("lru_bi", dbi),
                       ("lru_lambda", dlam)):
            small[n][l] = val
        dconv_w[l] = dcw[0:4]
        dxn = dx
    grad_x = dxn.reshape(1, S, D)

    parts = {n: jnp.stack(small[n]) for n in _SMALL if n != "mem_norm_g"}
    parts["mem_norm_g"] = dg_mem
    me1 = jnp.reshape(2 * jpos + ci, (1,)).astype(jnp.int32)
    rs = _rs_start(_pack_small(parts, jnp.stack(dconv_w)), lax.empty((NDEV - 1, SMALL_ROWS // NDEV, 128), F32))
    for l, a2a in sent[:-1]:
        finish_layer(l, a2a, rs[-1])
    pack, land = _rs_wait(rs, out["w_out"][1])
    ag = _ag_start(_small_sum(me1, pack, land))
    finish_layer(pending[0], pending[1], ag[-1])
    gsum = _ag_wait(ag, out["w_out"][1])
    given = dict(mem_norm_g=(mem_norm_g, m_mem_norm_g, v_mem_norm_g), pre_norm_g=(pre_norm_g, m_pre_norm_g, v_pre_norm_g),
                 post_norm_g=(post_norm_g, m_post_norm_g, v_post_norm_g), gmlp_ln_g=(gmlp_ln_g, m_gmlp_ln_g, v_gmlp_ln_g),
                 gmlp_ln_b=(gmlp_ln_b, m_gmlp_ln_b, v_gmlp_ln_b), gmlp_ws=(gmlp_ws, m_gmlp_ws, v_gmlp_ws),
                 gmlp_bs=(gmlp_bs, m_gmlp_bs, v_gmlp_bs), conv_b=(conv_b, m_conv_b, v_conv_b), lru_wr=(lru_wr, m_lru_wr, v_lru_wr),
                 lru_br=(lru_br, m_lru_br, v_lru_br), lru_wi=(lru_wi, m_lru_wi, v_lru_wi), lru_bi=(lru_bi, m_lru_bi, v_lru_bi),
                 lru_lambda=(lru_lambda, m_lru_lambda, v_lru_lambda))
    shapes = {n: given[n][0].shape for n in _SMALL}
    zero_cw = jnp.zeros((L, 4, D), F32)
    packs = [_pack_small({n: given[n][k] for n in _SMALL}, zero_cw) for k in range(3)]
    dsm, msm, vsm = _adamw_flat(packs[0], packs[1], packs[2], gsum, 2560)
    g_small, at = _unpack_small(gsum, shapes)
    d_small, _ = _unpack_small(dsm, shapes)
    m_small, _ = _unpack_small(msm, shapes)
    v_small, _ = _unpack_small(vsm, shapes)
    for n in _SMALL:
        out[n] = (g_small[n], d_small[n], m_small[n], v_small[n])
    g_cw = lax.dynamic_slice_in_dim(gsum[at:at + L * 4 * D // 128].reshape(L * 4, D), jpos * (D // 4), D // 4, axis=1)
    d_cw, m_cw, v_cw = _adamw_flat(conv_w.reshape(L * 4, D // 4), m_conv_w.reshape(L * 4, D // 4),
                                   v_conv_w.reshape(L * 4, D // 4), g_cw, L * 4)
    out["conv_w"] = tuple(a.reshape(L, 4, D // 4) for a in (g_cw, d_cw, m_cw, v_cw))

    order = ("mem_norm_g", "pre_norm_g", "post_norm_g", "w_in", "gmlp_ln_g", "gmlp_ln_b", "gmlp_ws", "gmlp_bs", "conv_w", "conv_b",
             "lru_wr", "lru_br", "lru_wi", "lru_bi", "lru_lambda", "w_kv", "w_pa", "w_pb", "w_pc", "w_out")
    return (loss, grad_x, *[out[n][0] for n in order], *[out[n][1] for n in order], *[out[n][2] for n in order],
            *[out[n][3] for n in order])
```

```python
import functools

import jax
import jax.numpy as jnp
from jax import lax
from jax.experimental import pallas as pl
from jax.experimental.pallas import tpu as pltpu

F32 = jnp.float32
BF16 = jnp.bfloat16
SDS = jax.ShapeDtypeStruct
MESH = pl.DeviceIdType.MESH

D = 1024
NIN = 10 * D
MEM = 256
GB = 128
NG = 8
NH = 4
HD = D // NH
EPS = 1e-6
LRU_C = 8.0
ADAM_LR, ADAM_B1, ADAM_B2, ADAM_EPS, ADAM_WD, ADAM_STEP = 0.001, 0.9, 0.999, 1e-08, 0.01, 10
NDEV = 8
SMALL_ROWS = 12800

_CALL_KW = {}
HBM = pl.BlockSpec(memory_space=pltpu.HBM)
VMEM = pl.BlockSpec(memory_space=pltpu.VMEM)
SEM = pl.BlockSpec(memory_space=pltpu.SEMAPHORE)
ANY = pl.BlockSpec(memory_space=pl.ANY)
TOKEN = SDS((8, 128), F32)


def _pcall(body, *, name, in_specs, out_specs, out_shape, grid=None, scratch=(), vmem_mb=48, aliases=None, effect=False,
           prefetch=0):
    kw = dict(_CALL_KW)
    if aliases:
        kw["input_output_aliases"] = aliases
    params = dict(vmem_limit_bytes=vmem_mb << 20)
    if grid is not None:
        params["dimension_semantics"] = ("arbitrary",) * len(grid)
    if effect:
        params["has_side_effects"] = pltpu.SideEffectType.DATAFLOW_SIDE_EFFECTING
    if prefetch:
        kw["grid_spec"] = pltpu.PrefetchScalarGridSpec(num_scalar_prefetch=prefetch, grid=grid, in_specs=in_specs,
                                                       out_specs=out_specs, scratch_shapes=list(scratch))
    else:
        kw.update(in_specs=in_specs, out_specs=out_specs, scratch_shapes=list(scratch))
        if grid is not None:
            kw["grid"] = grid
    return pl.pallas_call(body, name=name, out_shape=out_shape, compiler_params=pltpu.CompilerParams(**params), **kw)


def _full(shape):
    nd = len(shape)
    return pl.BlockSpec(shape, lambda *_: (0,) * nd)


def _dot(a, b):
    return jnp.dot(a, b, preferred_element_type=F32)


def _dot_nt(a, b):
    return lax.dot_general(a, b, (((1,), (1,)), ((), ())), preferred_element_type=F32)


def _dot_tn(a, b):
    return lax.dot_general(a, b, (((0,), (0,)), ((), ())), preferred_element_type=F32)


def _rowsum(a):
    return jnp.sum(a, axis=0, keepdims=True)


def _sigmoid(x):
    return 0.5 * jnp.tanh(0.5 * x) + 0.5


def _silu_parts(g):
    s = _sigmoid(g)
    return g * s, s * (1.0 + g * (1.0 - s))


def _rms_scale(x):
    return lax.rsqrt(jnp.mean(x * x, axis=-1, keepdims=True) + EPS)


def _dz_col(k):
    return jnp.where(k < 3, k, jnp.where(k < 6, k + 4, k - 3))


def _coords():
    return lax.axis_index("x"), lax.axis_index("y"), lax.axis_index("c")


def _other_chips(x, y):
    return [(1 - x, y), (x, 1 - y), (1 - x, 1 - y)]


def _peer(x, y, c, mask):
    return (1 - x if mask & 4 else x, 1 - y if mask & 2 else y, 1 - c if mask & 1 else c)


def _remote(src, dst, ssem, rsem, k, to):
    return pltpu.make_async_remote_copy(src_ref=src, dst_ref=dst, send_sem=ssem.at[k], recv_sem=rsem.at[k], device_id=to,
                                        device_id_type=MESH)


def _w_half(a, ref, jj, cc):
    if a == 2:
        rp = ref.shape[1] // 4
        return ref.at[:, pl.ds(jj * rp + cc * (rp // 2), rp // 2), :]
    kin, nsh = ref.shape[0], ref.shape[1] // 4
    return ref.at[pl.ds(cc * (kin // 2), kin // 2), pl.ds(jj * nsh, nsh)]


def _cw_block(ref, jj):
    return ref.at[:, pl.ds(jj * (D // 4), D // 4)]


def _cast_place(l, pos, w_in, w_kv, w_pa, w_pb, w_pc, w_out, cw8):
    kin, nsh = w_in.shape[1], w_in.shape[2]
    nkv, rp = w_kv.shape[2], w_pa.shape[1]
    half = kin // 2

    def body(pos_r, win, wkv, pa, pb, pc, po, cw, Win, Wkv, Wp, Cw):
        Win[...] = win[...].astype(BF16)
        Wkv[...] = wkv[...].astype(BF16)

        @pl.when(pl.program_id(0) == 0)
        def _():
            for k, r in enumerate((pa, pb, pc, po)):
                Wp[k] = r[...].astype(BF16)
            Cw[...] = cw[...]

    proj = pl.BlockSpec((None, rp, D), lambda i, p: (l, 0, 0))
    return _pcall(
        body, name="cast_place", grid=(2,), prefetch=1,
        in_specs=[pl.BlockSpec((None, half, nsh), lambda i, p: (l, i, 0)), pl.BlockSpec((None, half, nkv), lambda i, p: (l, i, 0)),
                  proj, proj, proj, proj, pl.BlockSpec((None, 8, D // 4), lambda i, p: (l, 0, 0))],
        out_specs=[pl.BlockSpec((half, nsh), lambda i, p: (i, p[0])), pl.BlockSpec((half, nkv), lambda i, p: (i, p[0])),
                   pl.BlockSpec((4, rp, D), lambda i, p: (0, p[0], 0)), pl.BlockSpec((8, D // 4), lambda i, p: (0, p[0]))],
        out_shape=[SDS((kin, 4 * nsh), BF16), SDS((kin, 4 * nkv), BF16), SDS((4, 4 * rp, D), BF16), SDS((8, D), F32)],
    )(pos, w_in, w_kv, w_pa, w_pb, w_pc, w_out, cw8)


def _hbm_like(bufs):
    return [pltpu.HBM(b.shape, b.dtype) for b in bufs]


def _w_part(a, ref, jj, cc):
    return _cw_block(ref, jj) if a == 3 else _w_half(a, ref, jj, cc)


def _gather_start(name, kinds, bufs, after=None):
    n = len(kinds)
    extra = [] if after is None else [after]

    def body(*refs):
        w, ssem, rsem, token = refs[0:n], refs[n + len(extra)], refs[n + len(extra) + 1], refs[-1]
        x, y, c = _coords()
        j = 2 * x + y
        for k, chip in enumerate(_other_chips(x, y)):
            for i, a in enumerate(kinds):
                part = _w_part(a, w[i], j, c)
                _remote(part, part, ssem, rsem, i * 3 + k, (chip[0], chip[1], c)).start()
        token[...] = jnp.zeros((8, 128), F32)

    return _pcall(
        body, name=name, in_specs=[HBM] * n + [ANY] * len(extra), out_specs=[SEM, SEM] + [HBM] * n + [VMEM],
        out_shape=[pltpu.SemaphoreType.DMA((3 * n,)), pltpu.SemaphoreType.DMA((3 * n,))] + _hbm_like(bufs) + [TOKEN],
        aliases={i: 2 + i for i in range(n)}, effect=True,
    )(*[pltpu.with_memory_space_constraint(b, pltpu.HBM) for b in bufs], *extra)


def _gather_mid(name, kinds, started, after):
    n = len(kinds)
    fw = [i for i, a in enumerate(kinds) if a != 3]
    bufs = tuple(started[2:2 + n])

    def body(*refs):
        w, ssem, rsem, ssem2, rsem2, token = refs[0:n], refs[n], refs[n + 1], refs[n + 3], refs[n + 4], refs[-1]
        x, y, c = _coords()
        j = 2 * x + y
        me, sib = (x, y, c), (x, y, 1 - c)
        token[...] = jnp.zeros((8, 128), F32)
        chips = _other_chips(x, y)
        for k, chip in enumerate(chips):
            for i, a in enumerate(kinds):
                got = _w_part(a, w[i], 2 * chip[0] + chip[1], c)
                _remote(got, got, ssem, rsem, i * 3 + k, me).wait_recv()
        for k in range(3):
            for i, a in enumerate(kinds):
                part = _w_part(a, w[i], j, c)
                _remote(part, part, ssem, rsem, i * 3 + k, me).wait_send()
        for k, chip in enumerate(chips):
            for f, i in enumerate(fw):
                got = _w_half(kinds[i], w[i], 2 * chip[0] + chip[1], c)
                _remote(got, got, ssem2, rsem2, f * 3 + k, sib).start()

    return _pcall(
        body, name=name, in_specs=[HBM] * n + [SEM, SEM, ANY], out_specs=[SEM, SEM] + [HBM] * n + [VMEM],
        out_shape=[pltpu.SemaphoreType.DMA((3 * len(fw),)), pltpu.SemaphoreType.DMA((3 * len(fw),))] + _hbm_like(bufs) + [TOKEN],
        aliases={i: 2 + i for i in range(n)}, effect=True,
    )(*bufs, started[0], started[1], after)


def _gather_end(name, kinds, mid, after):
    n = len(kinds)
    fw = [i for i, a in enumerate(kinds) if a != 3]
    bufs = tuple(mid[2:2 + n])

    def body(*refs):
        w, ssem2, rsem2 = refs[0:n], refs[n], refs[n + 1]
        x, y, c = _coords()
        me = (x, y, c)
        for k, chip in enumerate(_other_chips(x, y)):
            for f, i in enumerate(fw):
                got = _w_half(kinds[i], w[i], 2 * chip[0] + chip[1], 1 - c)
                _remote(got, got, ssem2, rsem2, f * 3 + k, me).wait_recv()
                sent = _w_half(kinds[i], w[i], 2 * chip[0] + chip[1], c)
                _remote(sent, sent, ssem2, rsem2, f * 3 + k, me).wait_send()

    return _pcall(
        body, name=name, in_specs=[HBM] * n + [SEM, SEM, ANY], out_specs=[HBM] * n, out_shape=_hbm_like(bufs),
        aliases={i: i for i in range(n)}, effect=True,
    )(*bufs, mid[0], mid[1], after)


def _g_piece(a, ref, jd, dc):
    if a == 2:
        rp = ref.shape[1] // 4
        return ref.at[:, pl.ds(jd * rp + dc * (rp // 2), rp // 2), :]
    kin, nsh = ref.shape[0], ref.shape[1] // 4
    return ref.at[pl.ds(dc * (kin // 2), kin // 2), pl.ds(jd * nsh, nsh)]


def _a2a_start(name, kinds, grads, lands):
    n = len(kinds)

    def body(*refs):
        g, ld, ssem, rsem, token = refs[0:n], refs[n:2 * n], refs[2 * n], refs[2 * n + 1], refs[-1]
        x, y, c = _coords()
        for mask in range(1, NDEV):
            p = _peer(x, y, c, mask)
            for i, a in enumerate(kinds):
                _remote(_g_piece(a, g[i], 2 * p[0] + p[1], p[2]), ld[i].at[mask - 1], ssem, rsem, i * 7 + mask - 1, p).start()
        token[...] = jnp.zeros((8, 128), F32)

    bufs = tuple(grads) + tuple(lands)
    return _pcall(
        body, name=name, in_specs=[HBM] * (2 * n), out_specs=[SEM, SEM] + [HBM] * (2 * n) + [VMEM],
        out_shape=[pltpu.SemaphoreType.DMA((7 * n,)), pltpu.SemaphoreType.DMA((7 * n,))] + _hbm_like(bufs) + [TOKEN],
        aliases={i: 2 + i for i in range(2 * n)}, effect=True,
    )(*[pltpu.with_memory_space_constraint(b, pltpu.HBM) for b in bufs])


def _a2a_wait(name, kinds, started, after):
    n = len(kinds)
    ssem, rsem = started[0], started[1]
    bufs = tuple(started[2:2 + 2 * n])

    def body(*refs):
        g, ld, ssem, rsem = refs[0:n], refs[n:2 * n], refs[2 * n], refs[2 * n + 1]
        x, y, c = _coords()
        me = (x, y, c)
        for mask in range(1, NDEV):
            for i in range(n):
                got = ld[i].at[mask - 1]
                _remote(got, got, ssem, rsem, i * 7 + mask - 1, me).wait_recv()
        for mask in range(1, NDEV):
            p = _peer(x, y, c, mask)
            for i, a in enumerate(kinds):
                sent = _g_piece(a, g[i], 2 * p[0] + p[1], p[2])
                _remote(sent, sent, ssem, rsem, i * 7 + mask - 1, me).wait_send()

    return _pcall(
        body, name=name, in_specs=[HBM] * (2 * n) + [SEM, SEM, ANY], out_specs=[HBM] * (2 * n), out_shape=_hbm_like(bufs),
        aliases={i: i for i in range(2 * n)}, effect=True,
    )(*bufs, ssem, rsem, after)


def _sum_share(pos, lands, grads):
    rows, n = 128, 4
    widths = [ld.shape[2] for ld in lands]

    def body(pos_r, l0, w0, l1, w1, l2, w2, g0, g1, g2, b0, b1, b2, lsem, ssem, rsem):
        i = pl.program_id(0)
        x, y, c = _coords()
        sib = (x, y, 1 - c)
        ld, ow, gs, bufs = (l0, l1, l2), (w0, w1, w2), (g0, g1, g2), (b0, b1, b2)

        def dst(a, step):
            row = step * (2 * rows) + c * rows if a == 2 else c * (n * rows) + step * rows
            return gs[a].at[pl.ds(row, rows), :]

        def copies(a, step, sl):
            src = bufs[a].at[sl]
            lc = pltpu.make_async_copy(src, dst(a, step), lsem.at[a, sl])
            rc = pltpu.make_async_remote_copy(src_ref=src, dst_ref=dst(a, step), send_sem=ssem.at[a, sl], recv_sem=rsem.at[a],
                                              device_id=sib, device_id_type=MESH)
            return lc, rc

        def drain(a, step, sl):
            lc, rc = copies(a, step, sl)
            lc.wait()
            rc.wait_send()

        slot = i % 2

        @pl.when(i >= 2)
        def _():
            for a in range(3):
                drain(a, i - 2, slot)

        for a in range(3):
            acc = ow[a][...].astype(F32)
            for k in range(NDEV - 1):
                acc = acc + ld[a][k].astype(F32)
            bufs[a][slot] = acc
            lc, rc = copies(a, i, slot)
            lc.start()
            rc.start()

        @pl.when(i == n - 1)
        def _():
            for a in range(3):
                drain(a, n - 2, (n - 2) % 2)
                drain(a, n - 1, (n - 1) % 2)
                whole = gs[a].at[pl.ds(0, n * rows), :]
                pltpu.make_async_remote_copy(src_ref=whole, dst_ref=whole, send_sem=ssem.at[a, 0], recv_sem=rsem.at[a],
                                             device_id=(x, y, c), device_id_type=MESH).wait_recv()

    land = lambda w: pl.BlockSpec((NDEV - 1, rows, w), lambda i, p: (0, i, 0))
    in_specs = [land(widths[0]), pl.BlockSpec((rows, widths[0]), lambda i, p: (p[1] * n + i, p[0])),
                land(widths[1]), pl.BlockSpec((rows, widths[1]), lambda i, p: (p[1] * n + i, p[0])),
                land(widths[2]), pl.BlockSpec((None, rows, widths[2]), lambda i, p: (i, 2 * p[0] + p[1], 0))]
    args = [t for pair in zip(lands, grads) for t in pair]
    return _pcall(
        body, name="sum_share", grid=(n,), prefetch=1, in_specs=in_specs, out_specs=[HBM] * 3,
        out_shape=[SDS((2 * n * rows, w), F32) for w in widths],
        scratch=[pltpu.VMEM((2, rows, w), F32) for w in widths]
        + [pltpu.SemaphoreType.DMA((3, 2)), pltpu.SemaphoreType.DMA((3, 2)), pltpu.SemaphoreType.DMA((3,))],
    )(pos, *args)


def _dev_index(p):
    return 4 * p[0] + 2 * p[1] + p[2]


def _small_rows(ref, d):
    r8 = ref.shape[0] // NDEV
    return ref.at[pl.ds(d * r8, r8), :]


def _rs_start(pack, land):
    def body(p_ref, ld, ssem, rsem, o0, o1, token):
        x, y, c = _coords()
        for mask in range(1, NDEV):
            p = _peer(x, y, c, mask)
            _remote(_small_rows(p_ref, _dev_index(p)), ld.at[mask - 1], ssem, rsem, mask - 1, p).start()
        token[...] = jnp.zeros((8, 128), F32)

    bufs = (pack, land)
    return _pcall(
        body, name="rs_start", in_specs=[HBM] * 2, out_specs=[SEM, SEM, HBM, HBM, VMEM],
        out_shape=[pltpu.SemaphoreType.DMA((NDEV - 1,)), pltpu.SemaphoreType.DMA((NDEV - 1,))] + _hbm_like(bufs) + [TOKEN],
        aliases={0: 2, 1: 3}, effect=True,
    )(*[pltpu.with_memory_space_constraint(b, pltpu.HBM) for b in bufs])


def _rs_wait(started, after):
    ssem, rsem, pack, land, _ = started

    def body(p_ref, ld, ssem, rsem, after_r, o0, o1):
        x, y, c = _coords()
        for mask in range(1, NDEV):
            got = ld.at[mask - 1]
            _remote(got, got, ssem, rsem, mask - 1, (x, y, c)).wait_recv()
        for mask in range(1, NDEV):
            sent = _small_rows(p_ref, _dev_index(_peer(x, y, c, mask)))
            _remote(sent, sent, ssem, rsem, mask - 1, (x, y, c)).wait_send()

    return _pcall(body, name="rs_wait", in_specs=[HBM, HBM, SEM, SEM, ANY], out_specs=[HBM, HBM],
                  out_shape=_hbm_like((pack, land)), aliases={0: 0, 1: 1}, effect=True)(pack, land, ssem, rsem, after)


def _small_sum(me1, pack, land):
    R = pack.shape[0]
    r8 = R // NDEV

    def body(me_r, p_ref, ld, full):
        acc = p_ref[...]
        for k in range(NDEV - 1):
            acc = acc + ld[k]
        full[...] = acc

    own = pl.BlockSpec((r8, 128), lambda i, m: (m[0], 0))
    return _pcall(body, name="small_sum", grid=(1,), prefetch=1,
                  in_specs=[own, pl.BlockSpec((NDEV - 1, r8, 128), lambda i, m: (0, 0, 0))], out_specs=own,
                  out_shape=SDS((R, 128), F32), vmem_mb=32)(me1, pack, land)


def _ag_start(full):
    def body(f_ref, ssem, rsem, o0, token):
        x, y, c = _coords()
        mine = _small_rows(f_ref, _dev_index((x, y, c)))
        for mask in range(1, NDEV):
            _remote(mine, mine, ssem, rsem, mask - 1, _peer(x, y, c, mask)).start()
        token[...] = jnp.zeros((8, 128), F32)

    return _pcall(
        body, name="ag_start", in_specs=[HBM], out_specs=[SEM, SEM, HBM, VMEM],
        out_shape=[pltpu.SemaphoreType.DMA((NDEV - 1,)), pltpu.SemaphoreType.DMA((NDEV - 1,))] + _hbm_like((full,)) + [TOKEN],
        aliases={0: 2}, effect=True,
    )(pltpu.with_memory_space_constraint(full, pltpu.HBM))


def _ag_wait(started, after):
    ssem, rsem, full, _ = started

    def body(f_ref, ssem, rsem, after_r, o0):
        x, y, c = _coords()
        mine = _small_rows(f_ref, _dev_index((x, y, c)))
        for mask in range(1, NDEV):
            got = _small_rows(f_ref, _dev_index(_peer(x, y, c, mask)))
            _remote(got, got, ssem, rsem, mask - 1, (x, y, c)).wait_recv()
            _remote(mine, mine, ssem, rsem, mask - 1, (x, y, c)).wait_send()

    return _pcall(body, name="ag_wait", in_specs=[HBM, SEM, SEM, ANY], out_specs=[HBM], out_shape=_hbm_like((full,)),
                  aliases={0: 0}, effect=True)(full, ssem, rsem, after)[0]


def _mm_in(x, g, w):
    S = x.shape[0]
    tm, tn = min(1024, S), 1280

    def body(x_ref, g_ref, w_ref, z_ref, h_ref, hs):
        @pl.when(pl.program_id(1) == 0)
        def _():
            xv = x_ref[...]
            hb = (xv * _rms_scale(xv) * g_ref[...]).astype(BF16)
            hs[...] = hb
            h_ref[...] = hb

        z_ref[...] = _dot(hs[...], w_ref[...]).astype(BF16)

    return _pcall(
        body, name="mm_in", grid=(S // tm, NIN // tn),
        in_specs=[pl.BlockSpec((tm, D), lambda i, j: (i, 0)), _full((1, D)), pl.BlockSpec((D, tn), lambda i, j: (0, j))],
        out_specs=[pl.BlockSpec((tm, tn), lambda i, j: (i, j)), pl.BlockSpec((tm, D), lambda i, j: (i, 0))],
        out_shape=[SDS((S, NIN), BF16), SDS((S, D), BF16)], scratch=[pltpu.VMEM((tm, D), BF16)],
    )(x, g, w)


def _chunk_mask():
    ri = lax.broadcasted_iota(jnp.int32, (GB, GB), 0)
    ci = lax.broadcasted_iota(jnp.int32, (GB, GB), 1)
    return (ri >= 64) | (ci < 64)


def _layernorm_parts(v):
    mu = jnp.mean(v, axis=-1, keepdims=True)
    d = v - mu
    rs = lax.rsqrt(jnp.mean(d * d, axis=-1, keepdims=True) + EPS)
    return d * rs, rs


def _branch_a(z, lg, lb, ws, bsb):
    S = z.shape[0]
    T = min(512, S)

    def body(zu, zv, zg, lg_r, lb_r, ws_r, bs_r, ya):
        vhat, _ = _layernorm_parts(zv[...].astype(F32))
        vnb = (vhat * lg_r[...] + lb_r[...]).astype(BF16)
        sil, _ = _silu_parts(zg[...].astype(F32))
        t = zu[...].astype(F32) * sil
        mask = _chunk_mask()
        for g in range(NG):
            wg = jnp.where(mask, ws_r[g], 0.0).astype(BF16)
            cs = slice(g * GB, (g + 1) * GB)
            for n in range(T // GB):
                rs = slice(n * GB, (n + 1) * GB)
                sv = _dot(wg, vnb[rs, cs]) + bs_r[g]
                ya[rs, cs] = (t[rs, cs] * sv).astype(BF16)

    zs = lambda k: pl.BlockSpec((T, D), lambda i: (i, k))
    return _pcall(
        body, name="branch_a", grid=(S // T,),
        in_specs=[zs(0), zs(1), zs(2), _full((1, D)), _full((1, D)), _full((NG, GB, GB)), _full((NG, GB, GB))],
        out_specs=pl.BlockSpec((T, D), lambda i: (i, 0)), out_shape=SDS((S, D), BF16),
    )(z, z, z, lg, lb, ws, bsb)


def _softplus_neg(lam):
    e = jnp.exp(-jnp.abs(lam))
    l1p = jnp.where(e < 1e-2, e * (1.0 - e * (0.5 - e * (1.0 / 3.0))), jnp.log(1.0 + e))
    return jnp.maximum(-lam, 0.0) + l1p


CH = 16


def _ck(c, off=0):
    return pl.ds(c * CH + off, CH)


def _half_sum(v):
    return v[0:8, :] + v[8:16, :]


def _lru_conv(xpad, c, cw, cb):
    xk = [xpad[_ck(c, 5 + k), :] for k in range(4)]
    return xk, cb + (((xk[0] * cw[0:1] + xk[1] * cw[1:2]) + xk[2] * cw[2:3]) + xk[3] * cw[3:4])


def _lru_gate_matmuls(xcb_s, wr_r, wi_r, pr_s, pi_s):
    for h in range(NG):
        cs = slice(h * GB, (h + 1) * GB)
        pr_s[:, cs] = _dot(xcb_s[:, cs], wr_r[h].astype(BF16))
        pi_s[:, cs] = _dot(xcb_s[:, cs], wi_r[h].astype(BF16))


def _lru_gates(pr, pi, br, bi, sp8):
    r = jax.nn.sigmoid(pr + br)
    ig = _sigmoid(pi + bi)
    la = sp8 * r
    a = jnp.exp(la)
    a2 = a * a
    mult = jnp.sqrt(-jnp.tanh(la) * (a2 + 1.0))
    return r, ig, a, a2, mult


def _tile_rows():
    return lax.broadcasted_iota(jnp.int32, (8, D), 0)


def _scan_forward(a_s, u_s, h_s, hcar, T):
    row = _tile_rows()

    def tile(i, hp):
        o = pl.multiple_of(i * 8, 8)
        A = a_s[pl.ds(o, 8), :]
        U = u_s[pl.ds(o, 8), :]
        for s in (1, 2, 4):
            m = row >= s
            U = jnp.where(m, U + A * pltpu.roll(U, s, 0), U)
            A = jnp.where(m, A * pltpu.roll(A, s, 0), A)
        H = U + A * hp
        h_s[pl.ds(o, 8), :] = H
        return jnp.broadcast_to(H[7:8, :], (8, D))

    hcar[...] = lax.fori_loop(0, T // 8, tile, hcar[...])


def _scan_reverse(b_s, d_s, l_s, lcar, T):
    row = _tile_rows()
    n = T // 8

    def tile(i, lp):
        o = pl.multiple_of((n - 1 - i) * 8, 8)
        B = b_s[pl.ds(o, 8), :]
        U = d_s[pl.ds(o, 8), :]
        for s in (1, 2, 4):
            m = row < 8 - s
            U = jnp.where(m, U + B * pltpu.roll(U, 8 - s, 0), U)
            B = jnp.where(m, B * pltpu.roll(B, 8 - s, 0), B)
        Lm = U + B * lp
        l_s[pl.ds(o, 8), :] = Lm
        return jnp.broadcast_to(Lm[0:1, :], (8, D))

    lcar[...] = lax.fori_loop(0, n, tile, lcar[...])


def _branch_b(z, cw, cb, wr, br, wi, bi, lam):
    S = z.shape[0]
    T = min(256, S)

    def body(zxb, zgb, cw_r, cb_r, wr_r, br_r, wi_r, bi_r, lam_r, yb, hs_o, xpad, a_s, u_s, hcar):
        @pl.when(pl.program_id(0) == 0)
        def _():
            xpad[pl.ds(0, 8), :] = jnp.zeros((8, D), F32)
            hcar[...] = jnp.zeros((8, D), F32)

        cw = cw_r[...]
        xpad[pl.ds(8, T), :] = zxb[...].astype(F32)
        xk = [xpad[pl.ds(5 + k, T), :] for k in range(4)]
        xc = cb_r[...] + (((xk[0] * cw[0:1] + xk[1] * cw[1:2]) + xk[2] * cw[2:3]) + xk[3] * cw[3:4])
        xcb = xc.astype(BF16)
        pr, pi = [], []
        for h in range(NG):
            cs = slice(h * GB, (h + 1) * GB)
            pr.append(_dot(xcb[:, cs], wr_r[h].astype(BF16)))
            pi.append(_dot(xcb[:, cs], wi_r[h].astype(BF16)))
        _, ig, a, _, mult = _lru_gates(jnp.concatenate(pr, axis=1), jnp.concatenate(pi, axis=1), br_r[...], bi_r[...],
                                       -LRU_C * _softplus_neg(lam_r[...]))
        a_s[...] = a
        u_s[...] = mult * (ig * xc)
        _scan_forward(a_s, u_s, hs_o, hcar, T)
        xpad[pl.ds(0, 8), :] = xpad[pl.ds(T, 8), :]
        sil, _ = _silu_parts(zgb[...].astype(F32))
        yb[...] = (hs_o[...] * sil).astype(BF16)

    zs = lambda k: pl.BlockSpec((T, D), lambda i: (i, k))
    row = pl.BlockSpec((T, D), lambda i: (i, 0))
    return _pcall(
        body, name="branch_b", grid=(S // T,),
        in_specs=[zs(3), zs(4), _full((8, D)), _full((1, D)), _full((NG, GB, GB)), _full((1, D)), _full((NG, GB, GB)),
                  _full((1, D)), _full((1, D))],
        out_specs=[row, row], out_shape=[SDS((S, D), BF16), SDS((S, D), F32)],
        scratch=[pltpu.VMEM((T + 8, D), F32), pltpu.VMEM((T, D), F32), pltpu.VMEM((T, D), F32), pltpu.VMEM((8, D), F32)],
    )(z, z, cw, cb, wr, br, wi, bi, lam)


def _kv(mem, g, wkv):
    def body(m_ref, g_ref, w_ref, kv_ref):
        m = m_ref[...]
        mn = (m * _rms_scale(m) * g_ref[...]).astype(BF16)
        kv_ref[...] = _dot(mn, w_ref[...]).astype(BF16)

    return _pcall(body, name="mem_kv", in_specs=[VMEM] * 3, out_specs=VMEM, out_shape=SDS((MEM, 2 * D), BF16),
                  vmem_mb=32)(mem, g, wkv)


def _softmax_rows(s):
    e = jnp.exp(s - jnp.max(s, axis=-1, keepdims=True))
    return e / jnp.sum(e, axis=-1, keepdims=True)


def _branch_c(z, kv):
    S = z.shape[0]
    T = min(512, S)

    def body(zq, zg, kv_r, yc):
        sil, _ = _silu_parts(zg[...].astype(F32))
        for h in range(NH):
            cs = slice(h * HD, (h + 1) * HD)
            p = _softmax_rows(_dot_nt(zq[:, cs], kv_r[:, cs]) * (HD ** -0.5))
            att = _dot(p.astype(BF16), kv_r[:, D + h * HD:D + (h + 1) * HD])
            yc[:, cs] = (att * sil[:, cs]).astype(BF16)

    zs = lambda k: pl.BlockSpec((T, D), lambda i: (i, k))
    return _pcall(body, name="branch_c", grid=(S // T,), in_specs=[zs(5), zs(6), _full((MEM, 2 * D))],
                  out_specs=pl.BlockSpec((T, D), lambda i: (i, 0)), out_shape=SDS((S, D), BF16))(z, z, kv)


def _merge_out(ya, yb, yc, z, wp, x, pg):
    S = x.shape[0]
    T = min(256, S)

    def body(ya_r, yb_r, yc_r, m0, m1, m2, wp_r, x_r, pg_r, pa_o, pb_o, pc_o, mg_o, o_o, xn_o):
        merged = None
        for y_r, ml, p_o, k in ((ya_r, m0, pa_o, 0), (yb_r, m1, pb_o, 1), (yc_r, m2, pc_o, 2)):
            p = _dot(y_r[...], wp_r[k])
            p_o[...] = p.astype(BF16)
            t = _sigmoid(ml[...].astype(F32)) * p
            merged = t if merged is None else merged + t
        mb = merged.astype(BF16)
        mg_o[...] = mb
        o = _dot(mb, wp_r[3])
        o_o[...] = o.astype(BF16)
        xn_o[...] = x_r[...] + o * _rms_scale(o) * pg_r[...]

    row = pl.BlockSpec((T, D), lambda i: (i, 0))
    zs = lambda k: pl.BlockSpec((T, D), lambda i: (i, k))
    return _pcall(
        body, name="merge_out", grid=(S // T,),
        in_specs=[row, row, row, zs(7), zs(8), zs(9), _full((4, D, D)), row, _full((1, D))],
        out_specs=[row] * 6, out_shape=[SDS((S, D), BF16)] * 5 + [SDS((S, D), F32)], vmem_mb=56,
    )(ya, yb, yc, z, z, z, wp, x, pg)


def _loss_head(y, t):
    S = y.shape[0]
    T = min(512, S)

    def body(y_r, t_r, loss_o, dy_o):
        @pl.when(pl.program_id(0) == 0)
        def _():
            loss_o[...] = jnp.zeros((1, 1), F32)

        e = y_r[...] - t_r[...]
        dy_o[...] = e * (1.0 / D)
        loss_o[...] += 0.5 * _rowsum(jnp.sum(e * e, axis=1, keepdims=True) * (1.0 / D))

    row = pl.BlockSpec((T, D), lambda i: (i, 0))
    return _pcall(body, name="loss_head", grid=(S // T,), in_specs=[row, row], out_specs=[_full((1, 1)), row],
                  out_shape=[SDS((1, 1), F32), SDS((S, D), F32)])(y, t)


def _accumulate(first, ref, val):
    @pl.when(first)
    def _():
        ref[...] = val

    @pl.when(jnp.logical_not(first))
    def _():
        ref[...] += val


def _out_bwd(dxn, o, pg, wp, z, pa, pb, pc):
    S = dxn.shape[0]
    T = min(256, S)

    def body(dy_r, o_r, pg_r, wp_r, m0, m1, m2, pa_r, pb_r, pc_r, do_o, dpa_o, dpb_o, dpc_o, dya_o, dyb_o, dyc_o, dz_o, dg_o):
        dy = dy_r[...]
        o = o_r[...].astype(F32)
        r2 = _rms_scale(o)
        w = dy * pg_r[...]
        do = r2 * w - o * (r2 * r2 * r2) * jnp.mean(w * o, axis=-1, keepdims=True)
        _accumulate(pl.program_id(0) == 0, dg_o, _rowsum(dy * o * r2))
        dob = do.astype(BF16)
        do_o[...] = dob
        dm = _dot_nt(dob, wp_r[3])
        for k, (ml, p_r, dp_o, dy_o) in enumerate(((m0, pa_r, dpa_o, dya_o), (m1, pb_r, dpb_o, dyb_o), (m2, pc_r, dpc_o, dyc_o))):
            gk = _sigmoid(ml[...].astype(F32))
            dz_o[k] = (dm * p_r[...].astype(F32) * gk * (1.0 - gk)).astype(BF16)
            dpk = (gk * dm).astype(BF16)
            dp_o[...] = dpk
            dy_o[...] = _dot_nt(dpk, wp_r[k]).astype(BF16)

    row = pl.BlockSpec((T, D), lambda i: (i, 0))
    zs = lambda k: pl.BlockSpec((T, D), lambda i: (i, k))
    return _pcall(
        body, name="out_bwd", grid=(S // T,),
        in_specs=[row, row, _full((1, D)), _full((4, D, D)), zs(7), zs(8), zs(9), row, row, row],
        out_specs=[row] * 7 + [pl.BlockSpec((3, T, D), lambda i: (1, i, 0)), _full((1, D))],
        out_shape=[SDS((S, D), BF16)] * 7 + [SDS((10, S, D), BF16), SDS((1, D), F32)], vmem_mb=56,
    )(dxn, o, pg, wp, z, z, z, pa, pb, pc)


def _branch_a_bwd(z, dya, lg, lb, ws, bsb, dz):
    S = z.shape[0]
    T = min(512, S)
    nblk = S // T

    def body(zu, zv, zg, dy_r, lg_r, lb_r, ws_r, bs_r, dz_in, dz_o, dws_o, dbs_o, dlg_o, dlb_o, dvn_s, bacc):
        i = pl.program_id(0)

        @pl.when(i == 0)
        def _():
            dws_o[...] = jnp.zeros((NG, GB, GB), F32)
            bacc[...] = jnp.zeros((NG, GB, GB), F32)

        vhat, rs = _layernorm_parts(zv[...].astype(F32))
        vnb = (vhat * lg_r[...] + lb_r[...]).astype(BF16)
        ga = zg[...].astype(F32)
        sil, dsil = _silu_parts(ga)
        u = zu[...].astype(F32)
        dy = dy_r[...].astype(F32)
        t = dy * sil
        dsv_all = t * u
        dga_pre = dy * u * dsil
        mask = _chunk_mask()
        for g in range(NG):
            wf = jnp.where(mask, ws_r[g], 0.0)
            wg = wf.astype(BF16)
            wgt = wf.T.astype(BF16)
            cs = slice(g * GB, (g + 1) * GB)
            dw = jnp.zeros((GB, GB), F32)
            db = jnp.zeros((GB, GB), F32)
            for n in range(T // GB):
                rsl = slice(n * GB, (n + 1) * GB)
                vb = vnb[rsl, cs]
                sv = _dot(wg, vb) + bs_r[g]
                dz_o[0, rsl, cs] = (t[rsl, cs] * sv).astype(BF16)
                dz_o[2, rsl, cs] = (dga_pre[rsl, cs] * sv).astype(BF16)
                dsv = dsv_all[rsl, cs]
                dsb = dsv.astype(BF16)
                dvn_s[rsl, cs] = _dot(wgt, dsb)
                dw = dw + _dot_nt(dsb, vb)
                db = db + dsv
            dws_o[g] += jnp.where(mask, dw, 0.0)
            bacc[g] += db
        dvn = dvn_s[...]
        dvh = dvn * lg_r[...]
        dv = rs * (dvh - jnp.mean(dvh, axis=-1, keepdims=True) - vhat * jnp.mean(dvh * vhat, axis=-1, keepdims=True))
        dz_o[1] = dv.astype(BF16)
        _accumulate(i == 0, dlg_o, _rowsum(dvn * vhat))
        _accumulate(i == 0, dlb_o, _rowsum(dvn))

        @pl.when(i == nblk - 1)
        def _():
            for g in range(NG):
                dbs_o[g:g + 1, :] = _rowsum(bacc[g].T)

    zs = lambda k: pl.BlockSpec((T, D), lambda i: (i, k))
    return _pcall(
        body, name="branch_a_bwd", grid=(nblk,),
        in_specs=[zs(0), zs(1), zs(2), pl.BlockSpec((T, D), lambda i: (i, 0)), _full((1, D)), _full((1, D)),
                  _full((NG, GB, GB)), _full((NG, GB, GB)), HBM],
        out_specs=[pl.BlockSpec((3, T, D), lambda i: (0, i, 0)), _full((NG, GB, GB)), _full((NG, GB)), _full((1, D)),
                   _full((1, D))],
        out_shape=[SDS((10, S, D), BF16), SDS((NG, GB, GB), F32), SDS((NG, GB), F32), SDS((1, D), F32), SDS((1, D), F32)],
        scratch=[pltpu.VMEM((T, D), F32), pltpu.VMEM((NG, GB, GB), F32)], aliases={8: 0},
    )(z, z, z, dya, lg, lb, ws, bsb, dz)


def _branch_b_bwd(z, hs, dyb, cw, cb, wr, br, wi, bi, lam, dz):
    S = z.shape[0]
    T = min(256, S)
    nblk = S // T

    def body(zxb, zprev, zgb, hs_r, hprev_r, dy_r, cw_r, cb_r, wr_r, br_r, wi_r, bi_r, lam_r, dz_in,
             dz_o, dcw_o, dcb_o, dwr_o, dbr_o, dwi_o, dbi_o, dlam_o, xpad, hpad, apad, dpad, xc_s, pr_s, pi_s, r_s, ig_s, m_s,
             b_s, d_s, l_s, back_s, xcb_s, dprb_s, dpib_s, lcar):
        i = pl.program_id(0)
        blk = nblk - 1 - i
        first = i == 0

        @pl.when(first)
        def _():
            apad[pl.ds(T, 8), :] = jnp.zeros((8, D), F32)
            dpad[pl.ds(T, 8), :] = jnp.zeros((8, D), F32)
            lcar[...] = jnp.zeros((8, D), F32)
            dcw_o[...] = jnp.zeros((8, D), F32)
            dwr_o[...] = jnp.zeros((NG, GB, GB), F32)
            dwi_o[...] = jnp.zeros((NG, GB, GB), F32)

        keep = (blk > 0).astype(F32)
        nck = T // CH
        cw, cb, br, bi, lam = cw_r[...], cb_r[...], br_r[...], bi_r[...], lam_r[...]
        sp8 = -LRU_C * _softplus_neg(lam)
        xpad[pl.ds(0, 8), :] = zprev[...].astype(F32)[8:16, :] * keep
        hpad[pl.ds(0, 8), :] = hprev_r[...] * keep
        for c in range(nck):
            xpad[_ck(c, 8), :] = zxb[_ck(c), :].astype(F32)
            hpad[_ck(c, 8), :] = hs_r[_ck(c), :]
            _, xc = _lru_conv(xpad, c, cw, cb)
            xc_s[_ck(c), :] = xc
            xcb_s[_ck(c), :] = xc.astype(BF16)
        _lru_gate_matmuls(xcb_s, wr_r, wi_r, pr_s, pi_s)
        for c in range(nck):
            r, ig, a, _, mult = _lru_gates(pr_s[_ck(c), :], pi_s[_ck(c), :], br, bi, sp8)
            r_s[_ck(c), :] = r
            ig_s[_ck(c), :] = ig
            m_s[_ck(c), :] = mult
            apad[_ck(c), :] = a
            sil, dsil = _silu_parts(zgb[_ck(c), :].astype(F32))
            dy = dy_r[_ck(c), :].astype(F32)
            dz_o[1, _ck(c), :] = (dy * hs_r[_ck(c), :] * dsil).astype(BF16)
            d_s[_ck(c), :] = dy * sil
        for c in range(nck):
            b_s[_ck(c), :] = apad[_ck(c, 1), :]
        _scan_reverse(b_s, d_s, l_s, lcar, T)
        s_sp = s_br = s_bi = jnp.zeros((8, D), F32)
        for c in range(nck):
            lm, r, ig, mult, a, xc = l_s[_ck(c), :], r_s[_ck(c), :], ig_s[_ck(c), :], m_s[_ck(c), :], apad[_ck(c), :], xc_s[_ck(c), :]
            t = lm * mult
            dpad[_ck(c), :] = t * ig
            dl = lm * hpad[_ck(c, 7), :] * a - (lm * ig * xc) * (a * a) / mult
            dpr = dl * sp8 * r * (1.0 - r)
            dpi = t * xc * ig * (1.0 - ig)
            s_sp = s_sp + _half_sum(dl * r)
            s_br = s_br + _half_sum(dpr)
            s_bi = s_bi + _half_sum(dpi)
            dprb_s[_ck(c), :] = dpr.astype(BF16)
            dpib_s[_ck(c), :] = dpi.astype(BF16)
        _accumulate(first, dlam_o, _rowsum(s_sp) * (LRU_C * jax.nn.sigmoid(-lam)))
        _accumulate(first, dbr_o, _rowsum(s_br))
        _accumulate(first, dbi_o, _rowsum(s_bi))
        for h in range(NG):
            cs = slice(h * GB, (h + 1) * GB)
            back_s[:, cs] = _dot_nt(dprb_s[:, cs], wr_r[h].astype(BF16)) + _dot_nt(dpib_s[:, cs], wi_r[h].astype(BF16))
            dwr_o[h] += _dot_tn(xcb_s[:, cs], dprb_s[:, cs])
            dwi_o[h] += _dot_tn(xcb_s[:, cs], dpib_s[:, cs])
        s_cb = jnp.zeros((8, D), F32)
        s_cw = [jnp.zeros((8, D), F32)] * 4
        for c in range(nck):
            dxc = dpad[_ck(c), :] + back_s[_ck(c), :]
            dpad[_ck(c), :] = dxc
            s_cb = s_cb + _half_sum(dxc)
            s_cw = [s_cw[k] + _half_sum(xpad[_ck(c, 5 + k), :] * dxc) for k in range(4)]
        _accumulate(first, dcb_o, _rowsum(s_cb))
        for k in range(4):
            dcw_o[k:k + 1, :] += _rowsum(s_cw[k])
        for c in range(nck):
            dxb = ((dpad[_ck(c, 3), :] * cw[0:1] + dpad[_ck(c, 2), :] * cw[1:2]) + dpad[_ck(c, 1), :] * cw[2:3]) + dpad[_ck(c), :] * cw[3:4]
            dz_o[0, _ck(c), :] = dxb.astype(BF16)
        apad[pl.ds(T, 8), :] = apad[pl.ds(0, 8), :]
        dpad[pl.ds(T, 8), :] = dpad[pl.ds(0, 8), :]

    rev = lambda k: pl.BlockSpec((T, D), lambda i: (nblk - 1 - i, k))
    prev16 = pl.BlockSpec((16, D), lambda i: (jnp.maximum((nblk - 1 - i) * (T // 16) - 1, 0), 3))
    prev8 = pl.BlockSpec((8, D), lambda i: (jnp.maximum((nblk - 1 - i) * (T // 8) - 1, 0), 0))
    vec, mat = _full((1, D)), _full((NG, GB, GB))
    return _pcall(
        body, name="branch_b_bwd", grid=(nblk,),
        in_specs=[rev(3), prev16, rev(4), rev(0), prev8, rev(0), _full((8, D)), vec, mat, vec, mat, vec, vec, HBM],
        out_specs=[pl.BlockSpec((2, T, D), lambda i: (3, nblk - 1 - i, 0)), _full((8, D)), vec, mat, vec, mat, vec, vec],
        out_shape=[SDS((10, S, D), BF16), SDS((8, D), F32), SDS((1, D), F32), SDS((NG, GB, GB), F32), SDS((1, D), F32),
                   SDS((NG, GB, GB), F32), SDS((1, D), F32), SDS((1, D), F32)],
        scratch=[pltpu.VMEM((T + 8, D), F32)] * 4 + [pltpu.VMEM((T, D), F32)] * 10 + [pltpu.VMEM((T, D), BF16)] * 3
        + [pltpu.VMEM((8, D), F32)],
        aliases={13: 0}, vmem_mb=56,
    )(z, z, z, hs, hs, dyb, cw, cb, wr, br, wi, bi, lam, dz)


def _branch_c_bwd(z, kv, dyc, dz):
    S = z.shape[0]
    T = min(512, S)

    def body(zq, zg, kv_r, dy_r, dz_in, dz_o, dkv_o):
        @pl.when(pl.program_id(0) == 0)
        def _():
            dkv_o[...] = jnp.zeros((MEM, 2 * D), F32)

        gc = zg[...].astype(F32)
        sil, dsil = _silu_parts(gc)
        dy = dy_r[...].astype(F32)
        datt = dy * sil
        dgc_pre = dy * dsil
        scale = HD ** -0.5
        for h in range(NH):
            cs = slice(h * HD, (h + 1) * HD)
            vs = slice(D + h * HD, D + (h + 1) * HD)
            qh = zq[:, cs]
            p = _softmax_rows(_dot_nt(qh, kv_r[:, cs]) * scale)
            pb = p.astype(BF16)
            att = _dot(pb, kv_r[:, vs])
            dz_o[1, :, cs] = (dgc_pre[:, cs] * att).astype(BF16)
            dab = datt[:, cs].astype(BF16)
            dp = _dot_nt(dab, kv_r[:, vs])
            ds = (p * (dp - jnp.sum(p * dp, axis=-1, keepdims=True)) * scale).astype(BF16)
            dz_o[0, :, cs] = _dot(ds, kv_r[:, cs]).astype(BF16)
            dkv_o[:, cs] += _dot_tn(ds, qh)
            dkv_o[:, vs] += _dot_tn(pb, dab)

    zs = lambda k: pl.BlockSpec((T, D), lambda i: (i, k))
    return _pcall(
        body, name="branch_c_bwd", grid=(S // T,),
        in_specs=[zs(5), zs(6), _full((MEM, 2 * D)), pl.BlockSpec((T, D), lambda i: (i, 0)), HBM],
        out_specs=[pl.BlockSpec((2, T, D), lambda i: (4, i, 0)), _full((MEM, 2 * D))],
        out_shape=[SDS((10, S, D), BF16), SDS((MEM, 2 * D), F32)], aliases={4: 0},
    )(z, z, kv, dyc, dz)


def _mm_dh(dz, w, x, dxn, g):
    S = x.shape[0]
    tm = min(1024, S)

    def body(dz_r, w_r, x_r, dxn_r, g_r, dx_o, dg_o, acc):
        i, k = pl.program_id(0), pl.program_id(1)
        _accumulate(k == 0, acc, _dot_nt(dz_r[0], w_r[...]))

        @pl.when(k == 9)
        def _():
            dh = acc[...]
            xv = x_r[...]
            r1 = _rms_scale(xv)
            wv = dh * g_r[...]
            dx_o[...] = dxn_r[...] + r1 * wv - xv * (r1 * r1 * r1) * jnp.mean(wv * xv, axis=-1, keepdims=True)
            _accumulate(i == 0, dg_o, _rowsum(dh * xv * r1))

    row = pl.BlockSpec((tm, D), lambda i, k: (i, 0))
    return _pcall(
        body, name="mm_dh", grid=(S // tm, 10),
        in_specs=[pl.BlockSpec((1, tm, D), lambda i, k: (k, i, 0)), pl.BlockSpec((D, D), lambda i, k: (0, _dz_col(k))),
                  row, row, _full((1, D))],
        out_specs=[row, _full((1, D))], out_shape=[SDS((S, D), F32), SDS((1, D), F32)],
        scratch=[pltpu.VMEM((tm, D), F32)],
    )(dz, w, x, dxn, g)


def _mm_dwin(h, dz):
    S = h.shape[0]
    tk = min(1024, S)
    nk = S // tk

    def body(h_r, dz_r, o_r, acc):
        k = pl.program_id(1)
        _accumulate(k == 0, acc, _dot_tn(h_r[...], dz_r[0]))

        @pl.when(k == nk - 1)
        def _():
            o_r[...] = acc[...].astype(BF16)

    return _pcall(
        body, name="mm_dwin", grid=(10, nk),
        in_specs=[pl.BlockSpec((tk, D), lambda n, k: (k, 0)), pl.BlockSpec((1, tk, D), lambda n, k: (n, k, 0))],
        out_specs=pl.BlockSpec((D, D), lambda n, k: (0, _dz_col(n))), out_shape=SDS((D, NIN), BF16),
        scratch=[pltpu.VMEM((D, D), F32)],
    )(h, dz)


def _mm_tn4(a4, b4):
    S = a4[0].shape[0]
    tk = min(1024, S)
    nk = S // tk

    def body(*refs):
        a_r, b_r, o_r, acc = refs[0:4], refs[4:8], refs[8], refs[9]
        w, k = pl.program_id(0), pl.program_id(1)
        for a in range(4):
            @pl.when(w == a)
            def _(a=a):
                _accumulate(k == 0, acc, _dot_tn(a_r[a][...], b_r[a][...]))

        @pl.when(k == nk - 1)
        def _():
            o_r[...] = acc[...].astype(BF16)

    def blk(a):
        return pl.BlockSpec((tk, D), lambda w, k: (jnp.where(w == a, k, jnp.where(w < a, 0, nk - 1)), 0))

    return _pcall(body, name="mm_tn4", grid=(4, nk), in_specs=[blk(a) for a in range(4)] * 2,
                  out_specs=pl.BlockSpec((None, D, D), lambda w, k: (w, 0, 0)), out_shape=SDS((4, D, D), BF16),
                  scratch=[pltpu.VMEM((D, D), F32)])(*a4, *b4)


def _mem_bwd(mem, g, wkv, dkv, dg_acc):
    def body(m_ref, g_ref, w_ref, dkv_ref, acc_ref, dw_ref, dg_ref):
        m = m_ref[...]
        mr = m * _rms_scale(m)
        mn = (mr * g_ref[...]).astype(BF16)
        dkb = dkv_ref[...].astype(BF16)
        dw_ref[...] = _dot_tn(mn, dkb).astype(BF16)
        dg_ref[...] = acc_ref[...] + _rowsum(_dot_nt(dkb, w_ref[...]) * mr)

    return _pcall(body, name="mem_bwd", in_specs=[VMEM] * 5, out_specs=[VMEM] * 2,
                  out_shape=[SDS((D, 2 * D), BF16), SDS((1, D), F32)], vmem_mb=48)(mem, g, wkv, dkv, dg_acc)


def _adamw_math(w, g, m, v):
    m2 = ADAM_B1 * m + (1.0 - ADAM_B1) * g
    v2 = ADAM_B2 * v + (1.0 - ADAM_B2) * (g * g)
    mh = m2 / (1.0 - ADAM_B1 ** ADAM_STEP)
    vh = v2 / (1.0 - ADAM_B2 ** ADAM_STEP)
    return -ADAM_LR * (mh / (jnp.sqrt(vh) + ADAM_EPS) + ADAM_WD * w), m2, v2


def _adamw_layer(l, w, m, v, g, prev, which=None, rows=256):
    L, R, C = w.shape

    def body(w_r, m_r, v_r, g_r, *rest):
        g_o, d_o, m_o, v_o = rest[-4:]
        g = g_r[...]
        d, m2, v2 = _adamw_math(w_r[...], g, m_r[...], v_r[...])
        g_o[...] = g
        d_o[...] = d
        m_o[...] = m2
        v_o[...] = v2

    st = pl.BlockSpec((None, rows, C), lambda i: (l, i, 0))
    gs = pl.BlockSpec((rows, C), lambda i: (i, 0)) if which is None else pl.BlockSpec((None, rows, C), lambda i: (which, i, 0))
    carried = list(prev) if prev is not None else []
    return _pcall(body, name="adamw_layer", grid=(R // rows,), in_specs=[st] * 3 + [gs] + [HBM] * len(carried),
                  out_specs=[st] * 4, out_shape=[SDS(w.shape, F32)] * 4, vmem_mb=56,
                  aliases={4 + k: k for k in range(len(carried))} or None)(w, m, v, g, *carried)


def _adamw_flat(w, m, v, g, rows):
    R, C = w.shape

    def body(w_r, m_r, v_r, g_r, d_o, m_o, v_o):
        d, m2, v2 = _adamw_math(w_r[...], g_r[...], m_r[...], v_r[...])
        d_o[...] = d
        m_o[...] = m2
        v_o[...] = v2

    blk = pl.BlockSpec((rows, C), lambda i: (i, 0))
    return _pcall(body, name="adamw_flat", grid=(R // rows,), in_specs=[blk] * 4, out_specs=[blk] * 3,
                  out_shape=[SDS((R, C), F32)] * 3)(w, m, v, g)


_SMALL = ("mem_norm_g", "pre_norm_g", "post_norm_g", "gmlp_ln_g", "gmlp_ln_b", "gmlp_ws", "gmlp_bs", "conv_b", "lru_wr",
          "lru_br", "lru_wi", "lru_bi", "lru_lambda")


def _pack_small(parts, conv_w_part):
    rows = [parts[n].reshape(-1, 128) for n in _SMALL] + [conv_w_part.reshape(-1, 128)]
    used = sum(r.shape[0] for r in rows)
    rows.append(jnp.zeros((SMALL_ROWS - used, 128), F32))
    return jnp.concatenate(rows, axis=0)


def _unpack_small(pack, shapes):
    out, at = {}, 0
    for n in _SMALL:
        size = 1
        for s in shapes[n]:
            size *= s
        out[n] = pack[at:at + size // 128].reshape(shapes[n])
        at += size // 128
    return out, at


def kernel(x, mem, mem_norm_g, pre_norm_g, post_norm_g, w_in, gmlp_ln_g, gmlp_ln_b, gmlp_ws, gmlp_bs, conv_w, conv_b, lru_wr, lru_br, lru_wi, lru_bi, lru_lambda, w_kv, w_pa, w_pb, w_pc, w_out, loss_target, m_mem_norm_g, m_pre_norm_g, m_post_norm_g, m_w_in, m_gmlp_ln_g, m_gmlp_ln_b, m_gmlp_ws, m_gmlp_bs, m_conv_w, m_conv_b, m_lru_wr, m_lru_br, m_lru_wi, m_lru_bi, m_lru_lambda, m_w_kv, m_w_pa, m_w_pb, m_w_pc, m_w_out, v_mem_norm_g, v_pre_norm_g, v_post_norm_g, v_w_in, v_gmlp_ln_g, v_gmlp_ln_b, v_gmlp_ws, v_gmlp_bs, v_conv_w, v_conv_b, v_lru_wr, v_lru_br, v_lru_wi, v_lru_bi, v_lru_lambda, v_w_kv, v_w_pa, v_w_pb, v_w_pc, v_w_out):
    L = w_in.shape[0]
    S = x.shape[1]
    xs = [x[0]]
    mem2 = mem[0]
    mg = mem_norm_g.reshape(1, D)
    vec = lambda a, l: a[l].reshape(1, D)
    ci = lax.axis_index("c")
    jpos = 2 * lax.axis_index("x") + lax.axis_index("y")
    pos = jnp.reshape(jpos, (1,)).astype(jnp.int32)
    pos2 = jnp.stack([jpos, ci]).astype(jnp.int32)

    cw8 = jnp.pad(conv_w, ((0, 0), (0, 4), (0, 0)))
    placed = [_cast_place(l, pos, w_in, w_kv, w_pa, w_pb, w_pc, w_out, cw8) for l in range(L)]
    W = [None] * L
    ALL = (0, 1, 2, 3)
    first = _gather_start("gather_start_0a", (0,), placed[0][0:1])
    rest = _gather_start("gather_start_0b", (1, 2, 3), placed[0][1:4], first[-1])
    mid = _gather_mid("gather_mid_0a", (0,), first, rest[-1])
    started = _gather_start("gather_start_1", ALL, placed[1], mid[-1])
    win0 = _gather_end("gather_end_0a", (0,), mid, started[-1])[0]
    zh0 = _mm_in(xs[0], vec(pre_norm_g, 0), win0)
    mid = _gather_mid("gather_mid_0b", (1, 2, 3), rest, zh0[1])
    W[0] = [win0] + list(_gather_end("gather_end_0b", (1, 2, 3), mid, mid[-1]))

    saved = []
    for l in range(L):
        Win, Wkv, Wp, Cw = W[l]
        z, h = zh0 if l == 0 else _mm_in(xs[l], vec(pre_norm_g, l), Win)
        bsb = jnp.broadcast_to(gmlp_bs[l][:, :, None], (NG, GB, GB))
        ya = _branch_a(z, vec(gmlp_ln_g, l), vec(gmlp_ln_b, l), gmlp_ws[l], bsb)
        yb, hs = _branch_b(z, Cw, vec(conv_b, l), lru_wr[l], vec(lru_br, l), lru_wi[l], vec(lru_bi, l), vec(lru_lambda, l))
        kv = _kv(mem2, mg, Wkv)
        yc = _branch_c(z, kv)
        pg = vec(post_norm_g, l)
        if l + 1 < L:
            mid = _gather_mid(f"gather_mid_{l + 1}", ALL, started, yc)
            if l + 2 < L:
                started = _gather_start(f"gather_start_{l + 2}", ALL, placed[l + 2], mid[-1])
                pg = pg + started[-1][0, 0]
        pa, pb, pc, mgd, o, xn = _merge_out(ya, yb, yc, z, Wp, xs[l], pg)
        if l + 1 < L:
            W[l + 1] = _gather_end(f"gather_end_{l + 1}", ALL, mid, xn)
        xs.append(xn)
        saved.append((z, h, ya, yb, yc, hs, kv, pa, pb, pc, mgd, o, bsb))

    loss11, dxn = _loss_head(xs[L], loss_target[0])
    loss = lax.psum(loss11[0, 0], ("x", "y", "c"))

    big = dict(w_in=(w_in, m_w_in, v_w_in), w_kv=(w_kv, m_w_kv, v_w_kv), w_pa=(w_pa, m_w_pa, v_w_pa),
               w_pb=(w_pb, m_w_pb, v_w_pb), w_pc=(w_pc, m_w_pc, v_w_pc), w_out=(w_out, m_w_out, v_w_out))
    out = {n: None for n in big}
    kin, nsh, nkv, rp = w_in.shape[1], w_in.shape[2], w_kv.shape[2], w_pa.shape[1]

    def finish_layer(l, a2a, after):
        g_p4, lp = _a2a_wait(f"a2a_p_wait_{l}", (2,), a2a[0], after)
        g_in, g_kv, lin, lkv = _a2a_wait(f"a2a_w_wait_{l}", (0, 1), a2a[1], after)
        g_in, g_kv, g_p = _sum_share(pos2, (lin, lkv, lp.reshape(NDEV - 1, 2 * rp, D)), (g_in, g_kv, g_p4))
        g_p = g_p.reshape(4, rp, D)
        out["w_in"] = _adamw_layer(l, *big["w_in"], g_in, out["w_in"])
        out["w_kv"] = _adamw_layer(l, *big["w_kv"], g_kv, out["w_kv"])
        for k, n in enumerate(("w_pa", "w_pb", "w_pc", "w_out")):
            out[n] = _adamw_layer(l, *big[n], g_p, out[n], which=k)

    small = {n: [None] * L for n in _SMALL}
    dconv_w = [None] * L
    dg_mem = jnp.zeros((1, D), F32)
    pending = None
    sent = []
    for l in reversed(range(L)):
        Win, Wkv, Wp, Cw = W[l]
        z, h, ya, yb, yc, hs, kv, pa, pb, pc, mgd, o, bsb = saved[l]
        pg = vec(post_norm_g, l) if pending is None else vec(post_norm_g, l) + pending[1][1][-1][0, 0]
        do, dpa, dpb, dpc, dya, dyb, dyc, dz, dgpost = _out_bwd(dxn, o, pg, Wp, z, pa, pb, pc)
        a2a_p = _a2a_start(f"a2a_p_start_{l}", (2,), (_mm_tn4((ya, yb, yc, mgd), (dpa, dpb, dpc, do)),),
                           (lax.empty((NDEV - 1, 4, rp // 2, D), BF16),))
        dz, dws, dbs, dlg, dlb = _branch_a_bwd(z, dya, vec(gmlp_ln_g, l) + a2a_p[-1][0, 0], vec(gmlp_ln_b, l), gmlp_ws[l], bsb, dz)
        dz, dcw, dcb, dwr, dbr, dwi, dbi, dlam = _branch_b_bwd(
            z, hs, dyb, Cw, vec(conv_b, l), lru_wr[l], vec(lru_br, l), lru_wi[l], vec(lru_bi, l), vec(lru_lambda, l), dz)
        dz, dkv = _branch_c_bwd(z, kv, dyc, dz)
        g_in = _mm_dwin(h, dz)
        g_kv, dg_mem = _mem_bwd(mem2, mg, Wkv, dkv, dg_mem)
        a2a_w = _a2a_start(f"a2a_w_start_{l}", (0, 1), (g_in, g_kv),
                           (lax.empty((NDEV - 1, kin // 2, nsh), BF16), lax.empty((NDEV - 1, kin // 2, nkv), BF16)))
        dx, dgpre = _mm_dh(dz, Win, xs[l], dxn, vec(pre_norm_g, l) + a2a_w[-1][0, 0])
        pending = (l, (a2a_p, a2a_w))
        sent.append(pending)
        for n, val in (("pre_norm_g", dgpre), ("post_norm_g", dgpost), ("gmlp_ln_g", dlg), ("gmlp_ln_b", dlb), ("gmlp_ws", dws),
                       ("gmlp_bs", dbs), ("conv_b", dcb), ("lru_wr", dwr), ("lru_br", dbr), ("lru_wi", dwi), ("lru_bi", dbi),
                       ("lru_lambda", dlam)):
            small[n][l] = val
        dconv_w[l] = dcw[0:4]
        dxn = dx
    grad_x = dxn.reshape(1, S, D)

    parts = {n: jnp.stack(small[n]) for n in _SMALL if n != "mem_norm_g"}
    parts["mem_norm_g"] = dg_mem
    me1 = jnp.reshape(2 * jpos + ci, (1,)).astype(jnp.int32)
    rs = _rs_start(_pack_small(parts, jnp.stack(dconv_w)), lax.empty((NDEV - 1, SMALL_ROWS // NDEV, 128), F32))
    for l, a2a in sent[:-1]:
        finish_layer(l, a2a, rs[-1])
    pack, land = _rs_wait(rs, out["w_out"][1])
    ag = _ag_start(_small_sum(me1, pack, land))
    finish_layer(pending[0], pending[1], ag[-1])
    gsum = _ag_wait(ag, out["w_out"][1])
    given = dict(mem_norm_g=(mem_norm_g, m_mem_norm_g, v_mem_norm_g), pre_norm_g=(pre_norm_g, m_pre_norm_g, v_pre_norm_g),
                 post_norm_g=(post_norm_g, m_post_norm_g, v_post_norm_g), gmlp_ln_g=(gmlp_ln_g, m_gmlp_ln_g, v_gmlp_ln_g),
                 gmlp_ln_b=(gmlp_ln_b, m_gmlp_ln_b, v_gmlp_ln_b), gmlp_ws=(gmlp_ws, m_gmlp_ws, v_gmlp_ws),
                 gmlp_bs=(gmlp_bs, m_gmlp_bs, v_gmlp_bs), conv_b=(conv_b, m_conv_b, v_conv_b), lru_wr=(lru_wr, m_lru_wr, v_lru_wr),
                 lru_br=(lru_br, m_lru_br, v_lru_br), lru_wi=(lru_wi, m_lru_wi, v_lru_wi), lru_bi=(lru_bi, m_lru_bi, v_lru_bi),
                 lru_lambda=(lru_lambda, m_lru_lambda, v_lru_lambda))
    shapes = {n: given[n][0].shape for n in _SMALL}
    zero_cw = jnp.zeros((L, 4, D), F32)
    packs = [_pack_small({n: given[n][k] for n in _SMALL}, zero_cw) for k in range(3)]
    dsm, msm, vsm = _adamw_flat(packs[0], packs[1], packs[2], gsum, 2560)
    g_small, at = _unpack_small(gsum, shapes)
    d_small, _ = _unpack_small(dsm, shapes)
    m_small, _ = _unpack_small(msm, shapes)
    v_small, _ = _unpack_small(vsm, shapes)
    for n in _SMALL:
        out[n] = (g_small[n], d_small[n], m_small[n], v_small[n])
    g_cw = lax.dynamic_slice_in_dim(gsum[at:at + L * 4 * D // 128].reshape(L * 4, D), jpos * (D // 4), D // 4, axis=1)
    d_cw, m_cw, v_cw = _adamw_flat(conv_w.reshape(L * 4, D // 4), m_conv_w.reshape(L * 4, D // 4),
                                   v_conv_w.reshape(L * 4, D // 4), g_cw, L * 4)
    out["conv_w"] = tuple(a.reshape(L, 4, D // 4) for a in (g_cw, d_cw, m_cw, v_cw))

    order = ("mem_norm_g", "pre_norm_g", "post_norm_g", "w_in", "gmlp_ln_g", "gmlp_ln_b", "gmlp_ws", "gmlp_bs", "conv_w", "conv_b",
             "lru_wr", "lru_br", "lru_wi", "lru_bi", "lru_lambda", "w_kv", "w_pa", "w_pb", "w_pc", "w_out")
    return (loss, grad_x, *[out[n][0] for n in order], *[out[n][1] for n in order], *[out[n][2] for n in order],
            *[out[n][3] for n in order])
```

```python
import functools

import jax
import jax.numpy as jnp
from jax import lax
from jax.experimental import pallas as pl
from jax.experimental.pallas import tpu as pltpu

F32 = jnp.float32
BF16 = jnp.bfloat16
SDS = jax.ShapeDtypeStruct
MESH = pl.DeviceIdType.MESH

D = 1024
NIN = 10 * D
MEM = 256
GB = 128
NG = 8
NH = 4
HD = D // NH
EPS = 1e-6
LRU_C = 8.0
ADAM_LR, ADAM_B1, ADAM_B2, ADAM_EPS, ADAM_WD, ADAM_STEP = 0.001, 0.9, 0.999, 1e-08, 0.01, 10
NDEV = 8
SMALL_ROWS = 12800

_CALL_KW = {}
HBM = pl.BlockSpec(memory_space=pltpu.HBM)
VMEM = pl.BlockSpec(memory_space=pltpu.VMEM)
SEM = pl.BlockSpec(memory_space=pltpu.SEMAPHORE)
ANY = pl.BlockSpec(memory_space=pl.ANY)
TOKEN = SDS((8, 128), F32)


def _pcall(body, *, name, in_specs, out_specs, out_shape, grid=None, scratch=(), vmem_mb=48, aliases=None, effect=False,
           prefetch=0):
    kw = dict(_CALL_KW)
    if aliases:
        kw["input_output_aliases"] = aliases
    params = dict(vmem_limit_bytes=vmem_mb << 20)
    if grid is not None:
        params["dimension_semantics"] = ("arbitrary",) * len(grid)
    if effect:
        params["has_side_effects"] = pltpu.SideEffectType.DATAFLOW_SIDE_EFFECTING
    if prefetch:
        kw["grid_spec"] = pltpu.PrefetchScalarGridSpec(num_scalar_prefetch=prefetch, grid=grid, in_specs=in_specs,
                                                       out_specs=out_specs, scratch_shapes=list(scratch))
    else:
        kw.update(in_specs=in_specs, out_specs=out_specs, scratch_shapes=list(scratch))
        if grid is not None:
            kw["grid"] = grid
    return pl.pallas_call(body, name=name, out_shape=out_shape, compiler_params=pltpu.CompilerParams(**params), **kw)


def _full(shape):
    nd = len(shape)
    return pl.BlockSpec(shape, lambda *_: (0,) * nd)


def _dot(a, b):
    return jnp.dot(a, b, preferred_element_type=F32)


def _dot_nt(a, b):
    return lax.dot_general(a, b, (((1,), (1,)), ((), ())), preferred_element_type=F32)


def _dot_tn(a, b):
    return lax.dot_general(a, b, (((0,), (0,)), ((), ())), preferred_element_type=F32)


def _rowsum(a):
    return jnp.sum(a, axis=0, keepdims=True)


def _sigmoid(x):
    return 0.5 * jnp.tanh(0.5 * x) + 0.5


def _silu_parts(g):
    s = _sigmoid(g)
    return g * s, s * (1.0 + g * (1.0 - s))


def _rms_scale(x):
    return lax.rsqrt(jnp.mean(x * x, axis=-1, keepdims=True) + EPS)


def _dz_col(k):
    return jnp.where(k < 3, k, jnp.where(k < 6, k + 4, k - 3))


def _coords():
    return lax.axis_index("x"), lax.axis_index("y"), lax.axis_index("c")


def _other_chips(x, y):
    return [(1 - x, y), (x, 1 - y), (1 - x, 1 - y)]


def _peer(x, y, c, mask):
    return (1 - x if mask & 4 else x, 1 - y if mask & 2 else y, 1 - c if mask & 1 else c)


def _remote(src, dst, ssem, rsem, k, to):
    return pltpu.make_async_remote_copy(src_ref=src, dst_ref=dst, send_sem=ssem.at[k], recv_sem=rsem.at[k], device_id=to,
                                        device_id_type=MESH)


def _w_half(a, ref, jj, cc):
    if a == 2:
        rp = ref.shape[1] // 4
        return ref.at[:, pl.ds(jj * rp + cc * (rp // 2), rp // 2), :]
    kin, nsh = ref.shape[0], ref.shape[1] // 4
    return ref.at[pl.ds(cc * (kin // 2), kin // 2), pl.ds(jj * nsh, nsh)]


def _cw_block(ref, jj):
    return ref.at[:, pl.ds(jj * (D // 4), D // 4)]


def _cast_place(l, pos, w_in, w_kv, w_pa, w_pb, w_pc, w_out, cw8):
    kin, nsh = w_in.shape[1], w_in.shape[2]
    nkv, rp = w_kv.shape[2], w_pa.shape[1]
    half = kin // 2

    def body(pos_r, win, wkv, pa, pb, pc, po, cw, Win, Wkv, Wp, Cw):
        Win[...] = win[...].astype(BF16)
        Wkv[...] = wkv[...].astype(BF16)

        @pl.when(pl.program_id(0) == 0)
        def _():
            for k, r in enumerate((pa, pb, pc, po)):
                Wp[k] = r[...].astype(BF16)
            Cw[...] = cw[...]

    proj = pl.BlockSpec((None, rp, D), lambda i, p: (l, 0, 0))
    return _pcall(
        body, name="cast_place", grid=(2,), prefetch=1,
        in_specs=[pl.BlockSpec((None, half, nsh), lambda i, p: (l, i, 0)), pl.BlockSpec((None, half, nkv), lambda i, p: (l, i, 0)),
                  proj, proj, proj, proj, pl.BlockSpec((None, 8, D // 4), lambda i, p: (l, 0, 0))],
        out_specs=[pl.BlockSpec((half, nsh), lambda i, p: (i, p[0])), pl.BlockSpec((half, nkv), lambda i, p: (i, p[0])),
                   pl.BlockSpec((4, rp, D), lambda i, p: (0, p[0], 0)), pl.BlockSpec((8, D // 4), lambda i, p: (0, p[0]))],
        out_shape=[SDS((kin, 4 * nsh), BF16), SDS((kin, 4 * nkv), BF16), SDS((4, 4 * rp, D), BF16), SDS((8, D), F32)],
    )(pos, w_in, w_kv, w_pa, w_pb, w_pc, w_out, cw8)


def _hbm_like(bufs):
    return [pltpu.HBM(b.shape, b.dtype) for b in bufs]


def _w_part(a, ref, jj, cc):
    return _cw_block(ref, jj) if a == 3 else _w_half(a, ref, jj, cc)


def _gather_start(name, kinds, bufs, after=None):
    n = len(kinds)
    extra = [] if after is None else [after]

    def body(*refs):
        w, ssem, rsem, token = refs[0:n], refs[n + len(extra)], refs[n + len(extra) + 1], refs[-1]
        x, y, c = _coords()
        j = 2 * x + y
        for k, chip in enumerate(_other_chips(x, y)):
            for i, a in enumerate(kinds):
                part = _w_part(a, w[i], j, c)
                _remote(part, part, ssem, rsem, i * 3 + k, (chip[0], chip[1], c)).start()
        token[...] = jnp.zeros((8, 128), F32)

    return _pcall(
        body, name=name, in_specs=[HBM] * n + [ANY] * len(extra), out_specs=[SEM, SEM] + [HBM] * n + [VMEM],
        out_shape=[pltpu.SemaphoreType.DMA((3 * n,)), pltpu.SemaphoreType.DMA((3 * n,))] + _hbm_like(bufs) + [TOKEN],
        aliases={i: 2 + i for i in range(n)}, effect=True,
    )(*[pltpu.with_memory_space_constraint(b, pltpu.HBM) for b in bufs], *extra)


def _gather_mid(name, kinds, started, after):
    n = len(kinds)
    fw = [i for i, a in enumerate(kinds) if a != 3]
    bufs = tuple(started[2:2 + n])

    def body(*refs):
        w, ssem, rsem, ssem2, rsem2, token = refs[0:n], refs[n], refs[n + 1], refs[n + 3], refs[n + 4], refs[-1]
        x, y, c = _coords()
        j = 2 * x + y
        me, sib = (x, y, c), (x, y, 1 - c)
        token[...] = jnp.zeros((8, 128), F32)
        chips = _other_chips(x, y)
        for k, chip in enumerate(chips):
            for i, a in enumerate(kinds):
                got = _w_part(a, w[i], 2 * chip[0] + chip[1], c)
                _remote(got, got, ssem, rsem, i * 3 + k, me).wait_recv()
        for k in range(3):
            for i, a in enumerate(kinds):
                part = _w_part(a, w[i], j, c)
                _remote(part, part, ssem, rsem, i * 3 + k, me).wait_send()
        for k, chip in enumerate(chips):
            for f, i in enumerate(fw):
                got = _w_half(kinds[i], w[i], 2 * chip[0] + chip[1], c)
                _remote(got, got, ssem2, rsem2, f * 3 + k, sib).start()

    return _pcall(
        body, name=name, in_specs=[HBM] * n + [SEM, SEM, ANY], out_specs=[SEM, SEM] + [HBM] * n + [VMEM],
        out_shape=[pltpu.SemaphoreType.DMA((3 * len(fw),)), pltpu.SemaphoreType.DMA((3 * len(fw),))] + _hbm_like(bufs) + [TOKEN],
        aliases={i: 2 + i for i in range(n)}, effect=True,
    )(*bufs, started[0], started[1], after)


def _gather_end(name, kinds, mid, after):
    n = len(kinds)
    fw = [i for i, a in enumerate(kinds) if a != 3]
    bufs = tuple(mid[2:2 + n])

    def body(*refs):
        w, ssem2, rsem2 = refs[0:n], refs[n], refs[n + 1]
        x, y, c = _coords()
        me = (x, y, c)
        for k, chip in enumerate(_other_chips(x, y)):
            for f, i in enumerate(fw):
                got = _w_half(kinds[i], w[i], 2 * chip[0] + chip[1], 1 - c)
                _remote(got, got, ssem2, rsem2, f * 3 + k, me).wait_recv()
                sent = _w_half(kinds[i], w[i], 2 * chip[0] + chip[1], c)
                _remote(sent, sent, ssem2, rsem2, f * 3 + k, me).wait_send()

    return _pcall(
        body, name=name, in_specs=[HBM] * n + [SEM, SEM, ANY], out_specs=[HBM] * n, out_shape=_hbm_like(bufs),
        aliases={i: i for i in range(n)}, effect=True,
    )(*bufs, mid[0], mid[1], after)


def _g_piece(a, ref, jd, dc):
    if a == 2:
        rp = ref.shape[1] // 4
        return ref.at[:, pl.ds(jd * rp + dc * (rp // 2), rp // 2), :]
    kin, nsh = ref.shape[0], ref.shape[1] // 4
    return ref.at[pl.ds(dc * (kin // 2), kin // 2), pl.ds(jd * nsh, nsh)]


def _a2a_start(name, kinds, grads, lands):
    n = len(kinds)

    def body(*refs):
        g, ld, ssem, rsem, token = refs[0:n], refs[n:2 * n], refs[2 * n], refs[2 * n + 1], refs[-1]
        x, y, c = _coords()
        for mask in range(1, NDEV):
            p = _peer(x, y, c, mask)
            for i, a in enumerate(kinds):
                _remote(_g_piece(a, g[i], 2 * p[0] + p[1], p[2]), ld[i].at[mask - 1], ssem, rsem, i * 7 + mask - 1, p).start()
        token[...] = jnp.zeros((8, 128), F32)

    bufs = tuple(grads) + tuple(lands)
    return _pcall(
        body, name=name, in_specs=[HBM] * (2 * n), out_specs=[SEM, SEM] + [HBM] * (2 * n) + [VMEM],
        out_shape=[pltpu.SemaphoreType.DMA((7 * n,)), pltpu.SemaphoreType.DMA((7 * n,))] + _hbm_like(bufs) + [TOKEN],
        aliases={i: 2 + i for i in range(2 * n)}, effect=True,
    )(*[pltpu.with_memory_space_constraint(b, pltpu.HBM) for b in bufs])


def _a2a_wait(name, kinds, started, after):
    n = len(kinds)
    ssem, rsem = started[0], started[1]
    bufs = tuple(started[2:2 + 2 * n])

    def body(*refs):
        g, ld, ssem, rsem = refs[0:n], refs[n:2 * n], refs[2 * n], refs[2 * n + 1]
        x, y, c = _coords()
        me = (x, y, c)
        for mask in range(1, NDEV):
            for i in range(n):
                got = ld[i].at[mask - 1]
                _remote(got, got, ssem, rsem, i * 7 + mask - 1, me).wait_recv()
        for mask in range(1, NDEV):
            p = _peer(x, y, c, mask)
            for i, a in enumerate(kinds):
                sent = _g_piece(a, g[i], 2 * p[0] + p[1], p[2])
                _remote(sent, sent, ssem, rsem, i * 7 + mask - 1, me).wait_send()

    return _pcall(
        body, name=name, in_specs=[HBM] * (2 * n) + [SEM, SEM, ANY], out_specs=[HBM] * (2 * n), out_shape=_hbm_like(bufs),
        aliases={i: i for i in range(2 * n)}, effect=True,
    )(*bufs, ssem, rsem, after)


def _sum_share(pos, lands, grads):
    rows, n = 128, 4
    widths = [ld.shape[2] for ld in lands]

    def body(pos_r, l0, w0, l1, w1, l2, w2, g0, g1, g2, b0, b1, b2, lsem, ssem, rsem):
        i = pl.program_id(0)
        x, y, c = _coords()
        sib = (x, y, 1 - c)
        ld, ow, gs, bufs = (l0, l1, l2), (w0, w1, w2), (g0, g1, g2), (b0, b1, b2)

        def dst(a, step):
            row = step * (2 * rows) + c * rows if a == 2 else c * (n * rows) + step * rows
            return gs[a].at[pl.ds(row, rows), :]

        def copies(a, step, sl):
            src = bufs[a].at[sl]
            lc = pltpu.make_async_copy(src, dst(a, step), lsem.at[a, sl])
            rc = pltpu.make_async_remote_copy(src_ref=src, dst_ref=dst(a, step), send_sem=ssem.at[a, sl], recv_sem=rsem.at[a],
                                              device_id=sib, device_id_type=MESH)
            return lc, rc

        def drain(a, step, sl):
            lc, rc = copies(a, step, sl)
            lc.wait()
            rc.wait_send()

        slot = i % 2

        @pl.when(i >= 2)
        def _():
            for a in range(3):
                drain(a, i - 2, slot)

        for a in range(3):
            acc = ow[a][...].astype(F32)
            for k in range(NDEV - 1):
                acc = acc + ld[a][k].astype(F32)
            bufs[a][slot] = acc
            lc, rc = copies(a, i, slot)
            lc.start()
            rc.start()

        @pl.when(i == n - 1)
        def _():
            for a in range(3):
                drain(a, n - 2, (n - 2) % 2)
                drain(a, n - 1, (n - 1) % 2)
                whole = gs[a].at[pl.ds(0, n * rows), :]
                pltpu.make_async_remote_copy(src_ref=whole, dst_ref=whole, send_sem=ssem.at[a, 0], recv_sem=rsem.at[a],
                                             device_id=(x, y, c), device_id_type=MESH).wait_recv()

    land = lambda w: pl.BlockSpec((NDEV - 1, rows, w), lambda i, p: (0, i, 0))
    in_specs = [land(widths[0]), pl.BlockSpec((rows, widths[0]), lambda i, p: (p[1] * n + i, p[0])),
                land(widths[1]), pl.BlockSpec((rows, widths[1]), lambda i, p: (p[1] * n + i, p[0])),
                land(widths[2]), pl.BlockSpec((None, rows, widths[2]), lambda i, p: (i, 2 * p[0] + p[1], 0))]
    args = [t for pair in zip(lands, grads) for t in pair]
    return _pcall(
        body, name="sum_share", grid=(n,), prefetch=1, in_specs=in_specs, out_specs=[HBM] * 3,
        out_shape=[SDS((2 * n * rows, w), F32) for w in widths],
        scratch=[pltpu.VMEM((2, rows, w), F32) for w in widths]
        + [pltpu.SemaphoreType.DMA((3, 2)), pltpu.SemaphoreType.DMA((3, 2)), pltpu.SemaphoreType.DMA((3,))],
    )(pos, *args)


def _dev_index(p):
    return 4 * p[0] + 2 * p[1] + p[2]


def _small_rows(ref, d):
    r8 = ref.shape[0] // NDEV
    return ref.at[pl.ds(d * r8, r8), :]


def _rs_start(pack, land):
    def body(p_ref, ld, ssem, rsem, o0, o1, token):
        x, y, c = _coords()
        for mask in range(1, NDEV):
            p = _peer(x, y, c, mask)
            _remote(_small_rows(p_ref, _dev_index(p)), ld.at[mask - 1], ssem, rsem, mask - 1, p).start()
        token[...] = jnp.zeros((8, 128), F32)

    bufs = (pack, land)
    return _pcall(
        body, name="rs_start", in_specs=[HBM] * 2, out_specs=[SEM, SEM, HBM, HBM, VMEM],
        out_shape=[pltpu.SemaphoreType.DMA((NDEV - 1,)), pltpu.SemaphoreType.DMA((NDEV - 1,))] + _hbm_like(bufs) + [TOKEN],
        aliases={0: 2, 1: 3}, effect=True,
    )(*[pltpu.with_memory_space_constraint(b, pltpu.HBM) for b in bufs])


def _rs_wait(started, after):
    ssem, rsem, pack, land, _ = started

    def body(p_ref, ld, ssem, rsem, after_r, o0, o1):
        x, y, c = _coords()
        for mask in range(1, NDEV):
            got = ld.at[mask - 1]
            _remote(got, got, ssem, rsem, mask - 1, (x, y, c)).wait_recv()
        for mask in range(1, NDEV):
            sent = _small_rows(p_ref, _dev_index(_peer(x, y, c, mask)))
            _remote(sent, sent, ssem, rsem, mask - 1, (x, y, c)).wait_send()

    return _pcall(body, name="rs_wait", in_specs=[HBM, HBM, SEM, SEM, ANY], out_specs=[HBM, HBM],
                  out_shape=_hbm_like((pack, land)), aliases={0: 0, 1: 1}, effect=True)(pack, land, ssem, rsem, after)


def _small_sum(me1, pack, land):
    R = pack.shape[0]
    r8 = R // NDEV

    def body(me_r, p_ref, ld, full):
        acc = p_ref[...]
        for k in range(NDEV - 1):
            acc = acc + ld[k]
        full[...] = acc

    own = pl.BlockSpec((r8, 128), lambda i, m: (m[0], 0))
    return _pcall(body, name="small_sum", grid=(1,), prefetch=1,
                  in_specs=[own, pl.BlockSpec((NDEV - 1, r8, 128), lambda i, m: (0, 0, 0))], out_specs=own,
                  out_shape=SDS((R, 128), F32), vmem_mb=32)(me1, pack, land)


def _ag_start(full):
    def body(f_ref, ssem, rsem, o0, token):
        x, y, c = _coords()
        mine = _small_rows(f_ref, _dev_index((x, y, c)))
        for mask in range(1, NDEV):
            _remote(mine, mine, ssem, rsem, mask - 1, _peer(x, y, c, mask)).start()
        token[...] = jnp.zeros((8, 128), F32)

    return _pcall(
        body, name="ag_start", in_specs=[HBM], out_specs=[SEM, SEM, HBM, VMEM],
        out_shape=[pltpu.SemaphoreType.DMA((NDEV - 1,)), pltpu.SemaphoreType.DMA((NDEV - 1,))] + _hbm_like((full,)) + [TOKEN],
        aliases={0: 2}, effect=True,
    )(pltpu.with_memory_space_constraint(full, pltpu.HBM))


def _ag_wait(started, after):
    ssem, rsem, full, _ = started

    def body(f_ref, ssem, rsem, after_r, o0):
        x, y, c = _coords()
        mine = _small_rows(f_ref, _dev_index((x, y, c)))
        for mask in range(1, NDEV):
            got = _small_rows(f_ref, _dev_index(_peer(x, y, c, mask)))
            _remote(got, got, ssem, rsem, mask - 1, (x, y, c)).wait_recv()
            _remote(mine, mine, ssem, rsem, mask - 1, (x, y, c)).wait_send()

    return _pcall(body, name="ag_wait", in_specs=[HBM, SEM, SEM, ANY], out_specs=[HBM], out_shape=_hbm_like((full,)),
                  aliases={0: 0}, effect=True)(full, ssem, rsem, after)[0]


def _mm_in(x, g, w):
    S = x.shape[0]
    tm, tn = min(1024, S), 1280

    def body(x_ref, g_ref, w_ref, z_ref, h_ref, hs):
        @pl.when(pl.program_id(1) == 0)
        def _():
            xv = x_ref[...]
            hb = (xv * _rms_scale(xv) * g_ref[...]).astype(BF16)
            hs[...] = hb
            h_ref[...] = hb

        z_ref[...] = _dot(hs[...], w_ref[...]).astype(BF16)

    return _pcall(
        body, name="mm_in", grid=(S // tm, NIN // tn),
        in_specs=[pl.BlockSpec((tm, D), lambda i, j: (i, 0)), _full((1, D)), pl.BlockSpec((D, tn), lambda i, j: (0, j))],
        out_specs=[pl.BlockSpec((tm, tn), lambda i, j: (i, j)), pl.BlockSpec((tm, D), lambda i, j: (i, 0))],
        out_shape=[SDS((S, NIN), BF16), SDS((S, D), BF16)], scratch=[pltpu.VMEM((tm, D), BF16)],
    )(x, g, w)


def _chunk_mask():
    ri = lax.broadcasted_iota(jnp.int32, (GB, GB), 0)
    ci = lax.broadcasted_iota(jnp.int32, (GB, GB), 1)
    return (ri >= 64) | (ci < 64)


def _layernorm_parts(v):
    mu = jnp.mean(v, axis=-1, keepdims=True)
    d = v - mu
    rs = lax.rsqrt(jnp.mean(d * d, axis=-1, keepdims=True) + EPS)
    return d * rs, rs


def _branch_a(z, lg, lb, ws, bsb):
    S = z.shape[0]
    T = min(512, S)

    def body(zu, zv, zg, lg_r, lb_r, ws_r, bs_r, ya):
        vhat, _ = _layernorm_parts(zv[...].astype(F32))
        vnb = (vhat * lg_r[...] + lb_r[...]).astype(BF16)
        sil, _ = _silu_parts(zg[...].astype(F32))
        t = zu[...].astype(F32) * sil
        mask = _chunk_mask()
        for g in range(NG):
            wg = jnp.where(mask, ws_r[g], 0.0).astype(BF16)
            cs = slice(g * GB, (g + 1) * GB)
            for n in range(T // GB):
                rs = slice(n * GB, (n + 1) * GB)
                sv = _dot(wg, vnb[rs, cs]) + bs_r[g]
                ya[rs, cs] = (t[rs, cs] * sv).astype(BF16)

    zs = lambda k: pl.BlockSpec((T, D), lambda i: (i, k))
    return _pcall(
        body, name="branch_a", grid=(S // T,),
        in_specs=[zs(0), zs(1), zs(2), _full((1, D)), _full((1, D)), _full((NG, GB, GB)), _full((NG, GB, GB))],
        out_specs=pl.BlockSpec((T, D), lambda i: (i, 0)), out_shape=SDS((S, D), BF16),
    )(z, z, z, lg, lb, ws, bsb)


def _softplus_neg(lam):
    e = jnp.exp(-jnp.abs(lam))
    l1p = jnp.where(e < 1e-2, e * (1.0 - e * (0.5 - e * (1.0 / 3.0))), jnp.log(1.0 + e))
    return jnp.maximum(-lam, 0.0) + l1p


CH = 16


def _ck(c, off=0):
    return pl.ds(c * CH + off, CH)


def _half_sum(v):
    return v[0:8, :] + v[8:16, :]


def _lru_conv(xpad, c, cw, cb):
    xk = [xpad[_ck(c, 5 + k), :] for k in range(4)]
    return xk, cb + (((xk[0] * cw[0:1] + xk[1] * cw[1:2]) + xk[2] * cw[2:3]) + xk[3] * cw[3:4])


def _lru_gate_matmuls(xcb_s, wr_r, wi_r, pr_s, pi_s):
    for h in range(NG):
        cs = slice(h * GB, (h + 1) * GB)
        pr_s[:, cs] = _dot(xcb_s[:, cs], wr_r[h].astype(BF16))
        pi_s[:, cs] = _dot(xcb_s[:, cs], wi_r[h].astype(BF16))


def _lru_gates(pr, pi, br, bi, sp8, inverse=False):
    r = jax.nn.sigmoid(pr + br)
    ig = _sigmoid(pi + bi)
    la = sp8 * r
    a = jnp.exp(la)
    a2 = a * a
    m2 = -jnp.tanh(la) * (a2 + 1.0)
    if inverse:
        inv = lax.rsqrt(m2)
        return r, ig, a, a2, m2 * inv, inv
    return r, ig, a, a2, jnp.sqrt(m2)


def _tile_rows():
    return lax.broadcasted_iota(jnp.int32, (8, D), 0)


def _scan_forward(a_s, u_s, h_s, hcar, T):
    row = _tile_rows()

    def tile(i, hp):
        o = pl.multiple_of(i * 8, 8)
        A = a_s[pl.ds(o, 8), :]
        U = u_s[pl.ds(o, 8), :]
        for s in (1, 2, 4):
            m = row >= s
            U = jnp.where(m, U + A * pltpu.roll(U, s, 0), U)
            A = jnp.where(m, A * pltpu.roll(A, s, 0), A)
        H = U + A * hp
        h_s[pl.ds(o, 8), :] = H
        return jnp.broadcast_to(H[7:8, :], (8, D))

    hcar[...] = lax.fori_loop(0, T // 8, tile, hcar[...])


def _scan_reverse(b_s, d_s, l_s, lcar, T):
    row = _tile_rows()
    n = T // 8

    def tile(i, lp):
        o = pl.multiple_of((n - 1 - i) * 8, 8)
        B = b_s[pl.ds(o, 8), :]
        U = d_s[pl.ds(o, 8), :]
        for s in (1, 2, 4):
            m = row < 8 - s
            U = jnp.where(m, U + B * pltpu.roll(U, 8 - s, 0), U)
            B = jnp.where(m, B * pltpu.roll(B, 8 - s, 0), B)
        Lm = U + B * lp
        l_s[pl.ds(o, 8), :] = Lm
        return jnp.broadcast_to(Lm[0:1, :], (8, D))

    lcar[...] = lax.fori_loop(0, n, tile, lcar[...])


def _branch_b(z, cw, cb, wr, br, wi, bi, lam):
    S = z.shape[0]
    T = min(256, S)

    def body(zxb, zgb, cw_r, cb_r, wr_r, br_r, wi_r, bi_r, lam_r, yb, hs_o, xpad, a_s, u_s, hcar):
        @pl.when(pl.program_id(0) == 0)
        def _():
            xpad[pl.ds(0, 8), :] = jnp.zeros((8, D), F32)
            hcar[...] = jnp.zeros((8, D), F32)

        cw = cw_r[...]
        xpad[pl.ds(8, T), :] = zxb[...].astype(F32)
        xk = [xpad[pl.ds(5 + k, T), :] for k in range(4)]
        xc = cb_r[...] + (((xk[0] * cw[0:1] + xk[1] * cw[1:2]) + xk[2] * cw[2:3]) + xk[3] * cw[3:4])
        xcb = xc.astype(BF16)
        pr, pi = [], []
        for h in range(NG):
            cs = slice(h * GB, (h + 1) * GB)
            pr.append(_dot(xcb[:, cs], wr_r[h].astype(BF16)))
            pi.append(_dot(xcb[:, cs], wi_r[h].astype(BF16)))
        _, ig, a, _, mult = _lru_gates(jnp.concatenate(pr, axis=1), jnp.concatenate(pi, axis=1), br_r[...], bi_r[...],
                                       -LRU_C * _softplus_neg(lam_r[...]))
        a_s[...] = a
        u_s[...] = mult * (ig * xc)
        _scan_forward(a_s, u_s, hs_o, hcar, T)
        xpad[pl.ds(0, 8), :] = xpad[pl.ds(T, 8), :]
        sil, _ = _silu_parts(zgb[...].astype(F32))
        yb[...] = (hs_o[...] * sil).astype(BF16)

    zs = lambda k: pl.BlockSpec((T, D), lambda i: (i, k))
    row = pl.BlockSpec((T, D), lambda i: (i, 0))
    return _pcall(
        body, name="branch_b", grid=(S // T,),
        in_specs=[zs(3), zs(4), _full((8, D)), _full((1, D)), _full((NG, GB, GB)), _full((1, D)), _full((NG, GB, GB)),
                  _full((1, D)), _full((1, D))],
        out_specs=[row, row], out_shape=[SDS((S, D), BF16), SDS((S, D), F32)],
        scratch=[pltpu.VMEM((T + 8, D), F32), pltpu.VMEM((T, D), F32), pltpu.VMEM((T, D), F32), pltpu.VMEM((8, D), F32)],
    )(z, z, cw, cb, wr, br, wi, bi, lam)


def _kv(mem, g, wkv):
    def body(m_ref, g_ref, w_ref, kv_ref):
        m = m_ref[...]
        mn = (m * _rms_scale(m) * g_ref[...]).astype(BF16)
        kv_ref[...] = _dot(mn, w_ref[...]).astype(BF16)

    return _pcall(body, name="mem_kv", in_specs=[VMEM] * 3, out_specs=VMEM, out_shape=SDS((MEM, 2 * D), BF16),
                  vmem_mb=32)(mem, g, wkv)


def _softmax_rows(s):
    e = jnp.exp(s - jnp.max(s, axis=-1, keepdims=True))
    return e / jnp.sum(e, axis=-1, keepdims=True)


def _branch_c(z, kv):
    S = z.shape[0]
    T = min(512, S)

    def body(zq, zg, kv_r, yc):
        sil, _ = _silu_parts(zg[...].astype(F32))
        for h in range(NH):
            cs = slice(h * HD, (h + 1) * HD)
            p = _softmax_rows(_dot_nt(zq[:, cs], kv_r[:, cs]) * (HD ** -0.5))
            att = _dot(p.astype(BF16), kv_r[:, D + h * HD:D + (h + 1) * HD])
            yc[:, cs] = (att * sil[:, cs]).astype(BF16)

    zs = lambda k: pl.BlockSpec((T, D), lambda i: (i, k))
    return _pcall(body, name="branch_c", grid=(S // T,), in_specs=[zs(5), zs(6), _full((MEM, 2 * D))],
                  out_specs=pl.BlockSpec((T, D), lambda i: (i, 0)), out_shape=SDS((S, D), BF16))(z, z, kv)


def _merge_out(ya, yb, yc, z, wp, x, pg):
    S = x.shape[0]
    T = min(256, S)

    def body(ya_r, yb_r, yc_r, m0, m1, m2, wp_r, x_r, pg_r, pa_o, pb_o, pc_o, mg_o, o_o, xn_o):
        merged = None
        for y_r, ml, p_o, k in ((ya_r, m0, pa_o, 0), (yb_r, m1, pb_o, 1), (yc_r, m2, pc_o, 2)):
            p = _dot(y_r[...], wp_r[k])
            p_o[...] = p.astype(BF16)
            t = _sigmoid(ml[...].astype(F32)) * p
            merged = t if merged is None else merged + t
        mb = merged.astype(BF16)
        mg_o[...] = mb
        o = _dot(mb, wp_r[3])
        o_o[...] = o.astype(BF16)
        xn_o[...] = x_r[...] + o * _rms_scale(o) * pg_r[...]

    row = pl.BlockSpec((T, D), lambda i: (i, 0))
    zs = lambda k: pl.BlockSpec((T, D), lambda i: (i, k))
    return _pcall(
        body, name="merge_out", grid=(S // T,),
        in_specs=[row, row, row, zs(7), zs(8), zs(9), _full((4, D, D)), row, _full((1, D))],
        out_specs=[row] * 6, out_shape=[SDS((S, D), BF16)] * 5 + [SDS((S, D), F32)], vmem_mb=56,
    )(ya, yb, yc, z, z, z, wp, x, pg)


def _loss_head(y, t):
    S = y.shape[0]
    T = min(512, S)

    def body(y_r, t_r, loss_o, dy_o):
        @pl.when(pl.program_id(0) == 0)
        def _():
            loss_o[...] = jnp.zeros((1, 1), F32)

        e = y_r[...] - t_r[...]
        dy_o[...] = e * (1.0 / D)
        loss_o[...] += 0.5 * _rowsum(jnp.sum(e * e, axis=1, keepdims=True) * (1.0 / D))

    row = pl.BlockSpec((T, D), lambda i: (i, 0))
    return _pcall(body, name="loss_head", grid=(S // T,), in_specs=[row, row], out_specs=[_full((1, 1)), row],
                  out_shape=[SDS((1, 1), F32), SDS((S, D), F32)])(y, t)


def _accumulate(first, ref, val):
    @pl.when(first)
    def _():
        ref[...] = val

    @pl.when(jnp.logical_not(first))
    def _():
        ref[...] += val


def _out_bwd(dxn, o, pg, wp, z, pa, pb, pc):
    S = dxn.shape[0]
    T = min(256, S)

    def body(dy_r, o_r, pg_r, wp_r, m0, m1, m2, pa_r, pb_r, pc_r, do_o, dpa_o, dpb_o, dpc_o, dya_o, dyb_o, dyc_o, dz_o, dg_o):
        dy = dy_r[...]
        o = o_r[...].astype(F32)
        r2 = _rms_scale(o)
        w = dy * pg_r[...]
        do = r2 * w - o * (r2 * r2 * r2) * jnp.mean(w * o, axis=-1, keepdims=True)
        _accumulate(pl.program_id(0) == 0, dg_o, _rowsum(dy * o * r2))
        dob = do.astype(BF16)
        do_o[...] = dob
        dm = _dot_nt(dob, wp_r[3])
        for k, (ml, p_r, dp_o, dy_o) in enumerate(((m0, pa_r, dpa_o, dya_o), (m1, pb_r, dpb_o, dyb_o), (m2, pc_r, dpc_o, dyc_o))):
            gk = _sigmoid(ml[...].astype(F32))
            dz_o[k] = (dm * p_r[...].astype(F32) * gk * (1.0 - gk)).astype(BF16)
            dpk = (gk * dm).astype(BF16)
            dp_o[...] = dpk
            dy_o[...] = _dot_nt(dpk, wp_r[k]).astype(BF16)

    row = pl.BlockSpec((T, D), lambda i: (i, 0))
    zs = lambda k: pl.BlockSpec((T, D), lambda i: (i, k))
    return _pcall(
        body, name="out_bwd", grid=(S // T,),
        in_specs=[row, row, _full((1, D)), _full((4, D, D)), zs(7), zs(8), zs(9), row, row, row],
        out_specs=[row] * 7 + [pl.BlockSpec((3, T, D), lambda i: (1, i, 0)), _full((1, D))],
        out_shape=[SDS((S, D), BF16)] * 7 + [SDS((10, S, D), BF16), SDS((1, D), F32)], vmem_mb=56,
    )(dxn, o, pg, wp, z, z, z, pa, pb, pc)


def _branch_a_bwd(z, dya, lg, lb, ws, bsb, dz):
    S = z.shape[0]
    T = min(512, S)
    nblk = S // T

    def body(zu, zv, zg, dy_r, lg_r, lb_r, ws_r, bs_r, dz_in, dz_o, dws_o, dbs_o, dlg_o, dlb_o, dvn_s, bacc):
        i = pl.program_id(0)

        @pl.when(i == 0)
        def _():
            dws_o[...] = jnp.zeros((NG, GB, GB), F32)
            bacc[...] = jnp.zeros((NG, GB, GB), F32)

        vhat, rs = _layernorm_parts(zv[...].astype(F32))
        vnb = (vhat * lg_r[...] + lb_r[...]).astype(BF16)
        ga = zg[...].astype(F32)
        sil, dsil = _silu_parts(ga)
        u = zu[...].astype(F32)
        dy = dy_r[...].astype(F32)
        t = dy * sil
        dsv_all = t * u
        dga_pre = dy * u * dsil
        mask = _chunk_mask()
        for g in range(NG):
            wf = jnp.where(mask, ws_r[g], 0.0)
            wg = wf.astype(BF16)
            wgt = wf.T.astype(BF16)
            cs = slice(g * GB, (g + 1) * GB)
            dw = jnp.zeros((GB, GB), F32)
            db = jnp.zeros((GB, GB), F32)
            for n in range(T // GB):
                rsl = slice(n * GB, (n + 1) * GB)
                vb = vnb[rsl, cs]
                sv = _dot(wg, vb) + bs_r[g]
                dz_o[0, rsl, cs] = (t[rsl, cs] * sv).astype(BF16)
                dz_o[2, rsl, cs] = (dga_pre[rsl, cs] * sv).astype(BF16)
                dsv = dsv_all[rsl, cs]
                dsb = dsv.astype(BF16)
                dvn_s[rsl, cs] = _dot(wgt, dsb)
                dw = dw + _dot_nt(dsb, vb)
                db = db + dsv
            dws_o[g] += jnp.where(mask, dw, 0.0)
            bacc[g] += db
        dvn = dvn_s[...]
        dvh = dvn * lg_r[...]
        dv = rs * (dvh - jnp.mean(dvh, axis=-1, keepdims=True) - vhat * jnp.mean(dvh * vhat, axis=-1, keepdims=True))
        dz_o[1] = dv.astype(BF16)
        _accumulate(i == 0, dlg_o, _rowsum(dvn * vhat))
        _accumulate(i == 0, dlb_o, _rowsum(dvn))

        @pl.when(i == nblk - 1)
        def _():
            for g in range(NG):
                dbs_o[g:g + 1, :] = _rowsum(bacc[g].T)

    zs = lambda k: pl.BlockSpec((T, D), lambda i: (i, k))
    return _pcall(
        body, name="branch_a_bwd", grid=(nblk,),
        in_specs=[zs(0), zs(1), zs(2), pl.BlockSpec((T, D), lambda i: (i, 0)), _full((1, D)), _full((1, D)),
                  _full((NG, GB, GB)), _full((NG, GB, GB)), HBM],
        out_specs=[pl.BlockSpec((3, T, D), lambda i: (0, i, 0)), _full((NG, GB, GB)), _full((NG, GB)), _full((1, D)),
                   _full((1, D))],
        out_shape=[SDS((10, S, D), BF16), SDS((NG, GB, GB), F32), SDS((NG, GB), F32), SDS((1, D), F32), SDS((1, D), F32)],
        scratch=[pltpu.VMEM((T, D), F32), pltpu.VMEM((NG, GB, GB), F32)], aliases={8: 0},
    )(z, z, z, dya, lg, lb, ws, bsb, dz)


def _branch_b_bwd(z, hs, dyb, cw, cb, wr, br, wi, bi, lam, dz):
    S = z.shape[0]
    T = min(256, S)
    nblk = S // T

    def body(zxb, zprev, zgb, hs_r, hprev_r, dy_r, cw_r, cb_r, wr_r, br_r, wi_r, bi_r, lam_r, dz_in,
             dz_o, dcw_o, dcb_o, dwr_o, dbr_o, dwi_o, dbi_o, dlam_o, xpad, hpad, apad, dpad, xc_s, pr_s, pi_s, r_s, ig_s, m_s,
             b_s, d_s, l_s, back_s, inv_s, xk0_s, xk1_s, xk2_s, xcb_s, dprb_s, dpib_s, lcar):
        xk_s = (xk0_s, xk1_s, xk2_s)
        i = pl.program_id(0)
        blk = nblk - 1 - i
        first = i == 0

        @pl.when(first)
        def _():
            apad[pl.ds(T, 8), :] = jnp.zeros((8, D), F32)
            dpad[pl.ds(T, 8), :] = jnp.zeros((8, D), F32)
            lcar[...] = jnp.zeros((8, D), F32)
            dcw_o[...] = jnp.zeros((8, D), F32)
            dwr_o[...] = jnp.zeros((NG, GB, GB), F32)
            dwi_o[...] = jnp.zeros((NG, GB, GB), F32)

        keep = (blk > 0).astype(F32)
        nck = T // CH
        cw, cb, br, bi, lam = cw_r[...], cb_r[...], br_r[...], bi_r[...], lam_r[...]
        sp8 = -LRU_C * _softplus_neg(lam)
        xpad[pl.ds(0, 8), :] = zprev[...].astype(F32)[8:16, :] * keep
        hpad[pl.ds(0, 8), :] = hprev_r[...] * keep
        for c in range(nck):
            xpad[_ck(c, 8), :] = zxb[_ck(c), :].astype(F32)
            hpad[_ck(c, 8), :] = hs_r[_ck(c), :]
            xk, xc = _lru_conv(xpad, c, cw, cb)
            for k in range(3):
                xk_s[k][_ck(c), :] = xk[k]
            xc_s[_ck(c), :] = xc
            xcb_s[_ck(c), :] = xc.astype(BF16)
        _lru_gate_matmuls(xcb_s, wr_r, wi_r, pr_s, pi_s)
        for c in range(nck):
            r, ig, a, _, mult, inv = _lru_gates(pr_s[_ck(c), :], pi_s[_ck(c), :], br, bi, sp8, inverse=True)
            r_s[_ck(c), :] = r
            ig_s[_ck(c), :] = ig
            m_s[_ck(c), :] = mult
            inv_s[_ck(c), :] = inv
            apad[_ck(c), :] = a
            sil, dsil = _silu_parts(zgb[_ck(c), :].astype(F32))
            dy = dy_r[_ck(c), :].astype(F32)
            dz_o[1, _ck(c), :] = (dy * hs_r[_ck(c), :] * dsil).astype(BF16)
            d_s[_ck(c), :] = dy * sil
        for c in range(nck):
            b_s[_ck(c), :] = apad[_ck(c, 1), :]
        _scan_reverse(b_s, d_s, l_s, lcar, T)
        s_sp = s_br = s_bi = jnp.zeros((8, D), F32)
        for c in range(nck):
            lm, r, ig, mult, a, xc = l_s[_ck(c), :], r_s[_ck(c), :], ig_s[_ck(c), :], m_s[_ck(c), :], apad[_ck(c), :], xc_s[_ck(c), :]
            t = lm * mult
            dpad[_ck(c), :] = t * ig
            dl = lm * hpad[_ck(c, 7), :] * a - (lm * ig * xc) * (a * a) * inv_s[_ck(c), :]
            dpr = dl * sp8 * r * (1.0 - r)
            dpi = t * xc * ig * (1.0 - ig)
            s_sp = s_sp + _half_sum(dl * r)
            s_br = s_br + _half_sum(dpr)
            s_bi = s_bi + _half_sum(dpi)
            dprb_s[_ck(c), :] = dpr.astype(BF16)
            dpib_s[_ck(c), :] = dpi.astype(BF16)
        _accumulate(first, dlam_o, _rowsum(s_sp) * (LRU_C * jax.nn.sigmoid(-lam)))
        _accumulate(first, dbr_o, _rowsum(s_br))
        _accumulate(first, dbi_o, _rowsum(s_bi))
        for h in range(NG):
            cs = slice(h * GB, (h + 1) * GB)
            back_s[:, cs] = _dot_nt(dprb_s[:, cs], wr_r[h].astype(BF16)) + _dot_nt(dpib_s[:, cs], wi_r[h].astype(BF16))
            dwr_o[h] += _dot_tn(xcb_s[:, cs], dprb_s[:, cs])
            dwi_o[h] += _dot_tn(xcb_s[:, cs], dpib_s[:, cs])
        s_cb = jnp.zeros((8, D), F32)
        s_cw = [jnp.zeros((8, D), F32)] * 4
        for c in range(nck):
            dxc = dpad[_ck(c), :] + back_s[_ck(c), :]
            dpad[_ck(c), :] = dxc
            s_cb = s_cb + _half_sum(dxc)
            s_cw = [s_cw[k] + _half_sum((xk_s[k][_ck(c), :] if k < 3 else xpad[_ck(c, 8), :]) * dxc) for k in range(4)]
        _accumulate(first, dcb_o, _rowsum(s_cb))
        for k in range(4):
            dcw_o[k:k + 1, :] += _rowsum(s_cw[k])
        for c in range(nck):
            dxb = ((dpad[_ck(c, 3), :] * cw[0:1] + dpad[_ck(c, 2), :] * cw[1:2]) + dpad[_ck(c, 1), :] * cw[2:3]) + dpad[_ck(c), :] * cw[3:4]
            dz_o[0, _ck(c), :] = dxb.astype(BF16)
        apad[pl.ds(T, 8), :] = apad[pl.ds(0, 8), :]
        dpad[pl.ds(T, 8), :] = dpad[pl.ds(0, 8), :]

    rev = lambda k: pl.BlockSpec((T, D), lambda i: (nblk - 1 - i, k))
    prev16 = pl.BlockSpec((16, D), lambda i: (jnp.maximum((nblk - 1 - i) * (T // 16) - 1, 0), 3))
    prev8 = pl.BlockSpec((8, D), lambda i: (jnp.maximum((nblk - 1 - i) * (T // 8) - 1, 0), 0))
    vec, mat = _full((1, D)), _full((NG, GB, GB))
    return _pcall(
        body, name="branch_b_bwd", grid=(nblk,),
        in_specs=[rev(3), prev16, rev(4), rev(0), prev8, rev(0), _full((8, D)), vec, mat, vec, mat, vec, vec, HBM],
        out_specs=[pl.BlockSpec((2, T, D), lambda i: (3, nblk - 1 - i, 0)), _full((8, D)), vec, mat, vec, mat, vec, vec],
        out_shape=[SDS((10, S, D), BF16), SDS((8, D), F32), SDS((1, D), F32), SDS((NG, GB, GB), F32), SDS((1, D), F32),
                   SDS((NG, GB, GB), F32), SDS((1, D), F32), SDS((1, D), F32)],
        scratch=[pltpu.VMEM((T + 8, D), F32)] * 4 + [pltpu.VMEM((T, D), F32)] * 14 + [pltpu.VMEM((T, D), BF16)] * 3
        + [pltpu.VMEM((8, D), F32)],
        aliases={13: 0}, vmem_mb=56,
    )(z, z, z, hs, hs, dyb, cw, cb, wr, br, wi, bi, lam, dz)


def _branch_c_bwd(z, kv, dyc, dz):
    S = z.shape[0]
    T = min(512, S)

    def body(zq, zg, kv_r, dy_r, dz_in, dz_o, dkv_o):
        @pl.when(pl.program_id(0) == 0)
        def _():
            dkv_o[...] = jnp.zeros((MEM, 2 * D), F32)

        gc = zg[...].astype(F32)
        sil, dsil = _silu_parts(gc)
        dy = dy_r[...].astype(F32)
        datt = dy * sil
        dgc_pre = dy * dsil
        scale = HD ** -0.5
        for h in range(NH):
            cs = slice(h * HD, (h + 1) * HD)
            vs = slice(D + h * HD, D + (h + 1) * HD)
            qh = zq[:, cs]
            p = _softmax_rows(_dot_nt(qh, kv_r[:, cs]) * scale)
            pb = p.astype(BF16)
            att = _dot(pb, kv_r[:, vs])
            dz_o[1, :, cs] = (dgc_pre[:, cs] * att).astype(BF16)
            dab = datt[:, cs].astype(BF16)
            dp = _dot_nt(dab, kv_r[:, vs])
            ds = (p * (dp - jnp.sum(p * dp, axis=-1, keepdims=True)) * scale).astype(BF16)
            dz_o[0, :, cs] = _dot(ds, kv_r[:, cs]).astype(BF16)
            dkv_o[:, cs] += _dot_tn(ds, qh)
            dkv_o[:, vs] += _dot_tn(pb, dab)

    zs = lambda k: pl.BlockSpec((T, D), lambda i: (i, k))
    return _pcall(
        body, name="branch_c_bwd", grid=(S // T,),
        in_specs=[zs(5), zs(6), _full((MEM, 2 * D)), pl.BlockSpec((T, D), lambda i: (i, 0)), HBM],
        out_specs=[pl.BlockSpec((2, T, D), lambda i: (4, i, 0)), _full((MEM, 2 * D))],
        out_shape=[SDS((10, S, D), BF16), SDS((MEM, 2 * D), F32)], aliases={4: 0},
    )(z, z, kv, dyc, dz)


def _mm_dh(dz, w, x, dxn, g):
    S = x.shape[0]
    tm = min(1024, S)

    def body(dz_r, w_r, x_r, dxn_r, g_r, dx_o, dg_o, acc):
        i, k = pl.program_id(0), pl.program_id(1)
        _accumulate(k == 0, acc, _dot_nt(dz_r[0], w_r[...]))

        @pl.when(k == 9)
        def _():
            dh = acc[...]
            xv = x_r[...]
            r1 = _rms_scale(xv)
            wv = dh * g_r[...]
            dx_o[...] = dxn_r[...] + r1 * wv - xv * (r1 * r1 * r1) * jnp.mean(wv * xv, axis=-1, keepdims=True)
            _accumulate(i == 0, dg_o, _rowsum(dh * xv * r1))

    row = pl.BlockSpec((tm, D), lambda i, k: (i, 0))
    return _pcall(
        body, name="mm_dh", grid=(S // tm, 10),
        in_specs=[pl.BlockSpec((1, tm, D), lambda i, k: (k, i, 0)), pl.BlockSpec((D, D), lambda i, k: (0, _dz_col(k))),
                  row, row, _full((1, D))],
        out_specs=[row, _full((1, D))], out_shape=[SDS((S, D), F32), SDS((1, D), F32)],
        scratch=[pltpu.VMEM((tm, D), F32)],
    )(dz, w, x, dxn, g)


def _mm_dwin(h, dz):
    S = h.shape[0]
    tk = min(1024, S)
    nk = S // tk

    def body(h_r, dz_r, o_r, acc):
        k = pl.program_id(1)
        _accumulate(k == 0, acc, _dot_tn(h_r[...], dz_r[0]))

        @pl.when(k == nk - 1)
        def _():
            o_r[...] = acc[...].astype(BF16)

    return _pcall(
        body, name="mm_dwin", grid=(10, nk),
        in_specs=[pl.BlockSpec((tk, D), lambda n, k: (k, 0)), pl.BlockSpec((1, tk, D), lambda n, k: (n, k, 0))],
        out_specs=pl.BlockSpec((D, D), lambda n, k: (0, _dz_col(n))), out_shape=SDS((D, NIN), BF16),
        scratch=[pltpu.VMEM((D, D), F32)],
    )(h, dz)


def _mm_tn4(a4, b4):
    S = a4[0].shape[0]
    tk = min(1024, S)
    nk = S // tk

    def body(*refs):
        a_r, b_r, o_r, acc = refs[0:4], refs[4:8], refs[8], refs[9]
        w, k = pl.program_id(0), pl.program_id(1)
        for a in range(4):
            @pl.when(w == a)
            def _(a=a):
                _accumulate(k == 0, acc, _dot_tn(a_r[a][...], b_r[a][...]))

        @pl.when(k == nk - 1)
        def _():
            o_r[...] = acc[...].astype(BF16)

    def blk(a):
        return pl.BlockSpec((tk, D), lambda w, k: (jnp.where(w == a, k, jnp.where(w < a, 0, nk - 1)), 0))

    return _pcall(body, name="mm_tn4", grid=(4, nk), in_specs=[blk(a) for a in range(4)] * 2,
                  out_specs=pl.BlockSpec((None, D, D), lambda w, k: (w, 0, 0)), out_shape=SDS((4, D, D), BF16),
                  scratch=[pltpu.VMEM((D, D), F32)])(*a4, *b4)


def _mem_bwd(mem, g, wkv, dkv, dg_acc):
    def body(m_ref, g_ref, w_ref, dkv_ref, acc_ref, dw_ref, dg_ref):
        m = m_ref[...]
        mr = m * _rms_scale(m)
        mn = (mr * g_ref[...]).astype(BF16)
        dkb = dkv_ref[...].astype(BF16)
        dw_ref[...] = _dot_tn(mn, dkb).astype(BF16)
        dg_ref[...] = acc_ref[...] + _rowsum(_dot_nt(dkb, w_ref[...]) * mr)

    return _pcall(body, name="mem_bwd", in_specs=[VMEM] * 5, out_specs=[VMEM] * 2,
                  out_shape=[SDS((D, 2 * D), BF16), SDS((1, D), F32)], vmem_mb=48)(mem, g, wkv, dkv, dg_acc)


def _adamw_math(w, g, m, v):
    m2 = ADAM_B1 * m + (1.0 - ADAM_B1) * g
    v2 = ADAM_B2 * v + (1.0 - ADAM_B2) * (g * g)
    mh = m2 / (1.0 - ADAM_B1 ** ADAM_STEP)
    vh = v2 / (1.0 - ADAM_B2 ** ADAM_STEP)
    return -ADAM_LR * (mh / (jnp.sqrt(vh) + ADAM_EPS) + ADAM_WD * w), m2, v2


def _adamw_layer(l, w, m, v, g, prev, which=None, rows=256):
    L, R, C = w.shape

    def body(w_r, m_r, v_r, g_r, *rest):
        g_o, d_o, m_o, v_o = rest[-4:]
        g = g_r[...]
        d, m2, v2 = _adamw_math(w_r[...], g, m_r[...], v_r[...])
        g_o[...] = g
        d_o[...] = d
        m_o[...] = m2
        v_o[...] = v2

    st = pl.BlockSpec((None, rows, C), lambda i: (l, i, 0))
    gs = pl.BlockSpec((rows, C), lambda i: (i, 0)) if which is None else pl.BlockSpec((None, rows, C), lambda i: (which, i, 0))
    carried = list(prev) if prev is not None else []
    return _pcall(body, name="adamw_layer", grid=(R // rows,), in_specs=[st] * 3 + [gs] + [HBM] * len(carried),
                  out_specs=[st] * 4, out_shape=[SDS(w.shape, F32)] * 4, vmem_mb=56,
                  aliases={4 + k: k for k in range(len(carried))} or None)(w, m, v, g, *carried)


def _adamw_flat(w, m, v, g, rows):
    R, C = w.shape

    def body(w_r, m_r, v_r, g_r, d_o, m_o, v_o):
        d, m2, v2 = _adamw_math(w_r[...], g_r[...], m_r[...], v_r[...])
        d_o[...] = d
        m_o[...] = m2
        v_o[...] = v2

    blk = pl.BlockSpec((rows, C), lambda i: (i, 0))
    return _pcall(body, name="adamw_flat", grid=(R // rows,), in_specs=[blk] * 4, out_specs=[blk] * 3,
                  out_shape=[SDS((R, C), F32)] * 3)(w, m, v, g)


_SMALL = ("mem_norm_g", "pre_norm_g", "post_norm_g", "gmlp_ln_g", "gmlp_ln_b", "gmlp_ws", "gmlp_bs", "conv_b", "lru_wr",
          "lru_br", "lru_wi", "lru_bi", "lru_lambda")


def _pack_small(parts, conv_w_part):
    rows = [parts[n].reshape(-1, 128) for n in _SMALL] + [conv_w_part.reshape(-1, 128)]
    used = sum(r.shape[0] for r in rows)
    rows.append(jnp.zeros((SMALL_ROWS - used, 128), F32))
    return jnp.concatenate(rows, axis=0)


def _unpack_small(pack, shapes):
    out, at = {}, 0
    for n in _SMALL:
        size = 1
        for s in shapes[n]:
            size *= s
        out[n] = pack[at:at + size // 128].reshape(shapes[n])
        at += size // 128
    return out, at


def kernel(x, mem, mem_norm_g, pre_norm_g, post_norm_g, w_in, gmlp_ln_g, gmlp_ln_b, gmlp_ws, gmlp_bs, conv_w, conv_b, lru_wr, lru_br, lru_wi, lru_bi, lru_lambda, w_kv, w_pa, w_pb, w_pc, w_out, loss_target, m_mem_norm_g, m_pre_norm_g, m_post_norm_g, m_w_in, m_gmlp_ln_g, m_gmlp_ln_b, m_gmlp_ws, m_gmlp_bs, m_conv_w, m_conv_b, m_lru_wr, m_lru_br, m_lru_wi, m_lru_bi, m_lru_lambda, m_w_kv, m_w_pa, m_w_pb, m_w_pc, m_w_out, v_mem_norm_g, v_pre_norm_g, v_post_norm_g, v_w_in, v_gmlp_ln_g, v_gmlp_ln_b, v_gmlp_ws, v_gmlp_bs, v_conv_w, v_conv_b, v_lru_wr, v_lru_br, v_lru_wi, v_lru_bi, v_lru_lambda, v_w_kv, v_w_pa, v_w_pb, v_w_pc, v_w_out):
    L = w_in.shape[0]
    S = x.shape[1]
    xs = [x[0]]
    mem2 = mem[0]
    mg = mem_norm_g.reshape(1, D)
    vec = lambda a, l: a[l].reshape(1, D)
    ci = lax.axis_index("c")
    jpos = 2 * lax.axis_index("x") + lax.axis_index("y")
    pos = jnp.reshape(jpos, (1,)).astype(jnp.int32)
    pos2 = jnp.stack([jpos, ci]).astype(jnp.int32)

    cw8 = jnp.pad(conv_w, ((0, 0), (0, 4), (0, 0)))
    placed = [_cast_place(0, pos, w_in, w_kv, w_pa, w_pb, w_pc, w_out, cw8)]
    W = [None] * L
    ALL = (0, 1, 2, 3)
    first = _gather_start("gather_start_0a", (0,), placed[0][0:1])
    rest = _gather_start("gather_start_0b", (1, 2, 3), placed[0][1:4], first[-1])
    cw8 = cw8 + rest[-1][0, 0]
    placed += [_cast_place(l, pos, w_in, w_kv, w_pa, w_pb, w_pc, w_out, cw8) for l in range(1, L)]
    mid = _gather_mid("gather_mid_0a", (0,), first, placed[L - 1][3])
    started = _gather_start("gather_start_1", ALL, placed[1], mid[-1])
    win0 = _gather_end("gather_end_0a", (0,), mid, started[-1])[0]
    zh0 = _mm_in(xs[0], vec(pre_norm_g, 0), win0)
    mid = _gather_mid("gather_mid_0b", (1, 2, 3), rest, zh0[1])
    W[0] = [win0] + list(_gather_end("gather_end_0b", (1, 2, 3), mid, mid[-1]))

    saved = []
    for l in range(L):
        Win, Wkv, Wp, Cw = W[l]
        z, h = zh0 if l == 0 else _mm_in(xs[l], vec(pre_norm_g, l), Win)
        bsb = jnp.broadcast_to(gmlp_bs[l][:, :, None], (NG, GB, GB))
        ya = _branch_a(z, vec(gmlp_ln_g, l), vec(gmlp_ln_b, l), gmlp_ws[l], bsb)
        yb, hs = _branch_b(z, Cw, vec(conv_b, l), lru_wr[l], vec(lru_br, l), lru_wi[l], vec(lru_bi, l), vec(lru_lambda, l))
        kv = _kv(mem2, mg, Wkv)
        yc = _branch_c(z, kv)
        pg = vec(post_norm_g, l)
        if l + 1 < L:
            mid = _gather_mid(f"gather_mid_{l + 1}", ALL, started, yc)
            if l + 2 < L:
                started = _gather_start(f"gather_start_{l + 2}", ALL, placed[l + 2], mid[-1])
                pg = pg + started[-1][0, 0]
        pa, pb, pc, mgd, o, xn = _merge_out(ya, yb, yc, z, Wp, xs[l], pg)
        if l + 1 < L:
            W[l + 1] = _gather_end(f"gather_end_{l + 1}", ALL, mid, xn)
        xs.append(xn)
        saved.append((z, h, ya, yb, yc, hs, kv, pa, pb, pc, mgd, o, bsb))

    loss11, dxn = _loss_head(xs[L], loss_target[0])
    loss = lax.psum(loss11[0, 0], ("x", "y", "c"))

    big = dict(w_in=(w_in, m_w_in, v_w_in), w_kv=(w_kv, m_w_kv, v_w_kv), w_pa=(w_pa, m_w_pa, v_w_pa),
               w_pb=(w_pb, m_w_pb, v_w_pb), w_pc=(w_pc, m_w_pc, v_w_pc), w_out=(w_out, m_w_out, v_w_out))
    out = {n: None for n in big}
    kin, nsh, nkv, rp = w_in.shape[1], w_in.shape[2], w_kv.shape[2], w_pa.shape[1]

    def finish_layer(l, a2a, after):
        g_p4, lp = _a2a_wait(f"a2a_p_wait_{l}", (2,), a2a[0], after)
        g_in, g_kv, lin, lkv = _a2a_wait(f"a2a_w_wait_{l}", (0, 1), a2a[1], after)
        g_in, g_kv, g_p = _sum_share(pos2, (lin, lkv, lp.reshape(NDEV - 1, 2 * rp, D)), (g_in, g_kv, g_p4))
        g_p = g_p.reshape(4, rp, D)
        out["w_in"] = _adamw_layer(l, *big["w_in"], g_in, out["w_in"])
        out["w_kv"] = _adamw_layer(l, *big["w_kv"], g_kv, out["w_kv"])
        for k, n in enumerate(("w_pa", "w_pb", "w_pc", "w_out")):
            out[n] = _adamw_layer(l, *big[n], g_p, out[n], which=k)

    small = {n: [None] * L for n in _SMALL}
    dconv_w = [None] * L
    dg_mem = jnp.zeros((1, D), F32)
    pending = None
    sent = []
    for l in reversed(range(L)):
        Win, Wkv, Wp, Cw = W[l]
        z, h, ya, yb, yc, hs, kv, pa, pb, pc, mgd, o, bsb = saved[l]
        pg = vec(post_norm_g, l) if pending is None else vec(post_norm_g, l) + pending[1][1][-1][0, 0]
        do, dpa, dpb, dpc, dya, dyb, dyc, dz, dgpost = _out_bwd(dxn, o, pg, Wp, z, pa, pb, pc)
        a2a_p = _a2a_start(f"a2a_p_start_{l}", (2,), (_mm_tn4((ya, yb, yc, mgd), (dpa, dpb, dpc, do)),),
                           (lax.empty((NDEV - 1, 4, rp // 2, D), BF16),))
        dz, dws, dbs, dlg, dlb = _branch_a_bwd(z, dya, vec(gmlp_ln_g, l) + a2a_p[-1][0, 0], vec(gmlp_ln_b, l), gmlp_ws[l], bsb, dz)
        dz, dcw, dcb, dwr, dbr, dwi, dbi, dlam = _branch_b_bwd(
            z, hs, dyb, Cw, vec(conv_b, l), lru_wr[l], vec(lru_br, l), lru_wi[l], vec(lru_bi, l), vec(lru_lambda, l), dz)
        dz, dkv = _branch_c_bwd(z, kv, dyc, dz)
        g_in = _mm_dwin(h, dz)
        g_kv, dg_mem = _mem_bwd(mem2, mg, Wkv, dkv, dg_mem)
        a2a_w = _a2a_start(f"a2a_w_start_{l}", (0, 1), (g_in, g_kv),
                           (lax.empty((NDEV - 1, kin // 2, nsh), BF16), lax.empty((NDEV - 1, kin // 2, nkv), BF16)))
        dx, dgpre = _mm_dh(dz, Win, xs[l], dxn, vec(pre_norm_g, l) + a2a_w[-1][0, 0])
        pending = (l, (a2a_p, a2a_w))
        sent.append(pending)
        for n, val in (("pre_norm_g", dgpre), ("post_norm_g", dgpost), ("gmlp_ln_g", dlg), ("gmlp_ln_b", dlb), ("gmlp_ws", dws),
                       ("gmlp_bs", dbs), ("conv_b", dcb), ("lru_wr", dwr), ("lru_br", dbr), ("lru_wi", dwi), ("lru_bi", dbi),
                       ("lru_lambda", dlam)):
            small[n][l] = val
        dconv_w[l] = dcw[0:4]
        dxn = dx
    grad_x = dxn.reshape(1, S, D)

    parts = {n: jnp.stack(small[n]) for n in _SMALL if n != "mem_norm_g"}
    parts["mem_norm_g"] = dg_mem
    me1 = jnp.reshape(2 * jpos + ci, (1,)).astype(jnp.int32)
    rs = _rs_start(_pack_small(parts, jnp.stack(dconv_w)), lax.empty((NDEV - 1, SMALL_ROWS // NDEV, 128), F32))
    for l, a2a in sent[:-1]:
        finish_layer(l, a2a, rs[-1])
    pack, land = _rs_wait(rs, out["w_out"][1])
    ag = _ag_start(_small_sum(me1, pack, land))
    finish_layer(pending[0], pending[1], ag[-1])
    gsum = _ag_wait(ag, out["w_out"][1])
    given = dict(mem_norm_g=(mem_norm_g, m_mem_norm_g, v_mem_norm_g), pre_norm_g=(pre_norm_g, m_pre_norm_g, v_pre_norm_g),
                 post_norm_g=(post_norm_g, m_post_norm_g, v_post_norm_g), gmlp_ln_g=(gmlp_ln_g, m_gmlp_ln_g, v_gmlp_ln_g),
                 gmlp_ln_b=(gmlp_ln_b, m_gmlp_ln_b, v_gmlp_ln_b), gmlp_ws=(gmlp_ws, m_gmlp_ws, v_gmlp_ws),
                 gmlp_bs=(gmlp_bs, m_gmlp_bs, v_gmlp_bs), conv_b=(conv_b, m_conv_b, v_conv_b), lru_wr=(lru_wr, m_lru_wr, v_lru_wr),
                 lru_br=(lru_br, m_lru_br, v_lru_br), lru_wi=(lru_wi, m_lru_wi, v_lru_wi), lru_bi=(lru_bi, m_lru_bi, v_lru_bi),
                 lru_lambda=(lru_lambda, m_lru_lambda, v_lru_lambda))
    shapes = {n: given[n][0].shape for n in _SMALL}
    zero_cw = jnp.zeros((L, 4, D), F32)
    packs = [_pack_small({n: given[n][k] for n in _SMALL}, zero_cw) for k in range(3)]
    dsm, msm, vsm = _adamw_flat(packs[0], packs[1], packs[2], gsum, 2560)
    g_small, at = _unpack_small(gsum, shapes)
    d_small, _ = _unpack_small(dsm, shapes)
    m_small, _ = _unpack_small(msm, shapes)
    v_small, _ = _unpack_small(vsm, shapes)
    for n in _SMALL:
        out[n] = (g_small[n], d_small[n], m_small[n], v_small[n])
    g_cw = lax.dynamic_slice_in_dim(gsum[at:at + L * 4 * D // 128].reshape(L * 4, D), jpos * (D // 4), D // 4, axis=1)
    d_cw, m_cw, v_cw = _adamw_flat(conv_w.reshape(L * 4, D // 4), m_conv_w.reshape(L * 4, D // 4),
                                   v_conv_w.reshape(L * 4, D // 4), g_cw, L * 4)
    out["conv_w"] = tuple(a.reshape(L, 4, D // 4) for a in (g_cw, d_cw, m_cw, v_cw))

    order = ("mem_norm_g", "pre_norm_g", "post_norm_g", "w_in", "gmlp_ln_g", "gmlp_ln_b", "gmlp_ws", "gmlp_bs", "conv_w", "conv_b",
             "lru_wr", "lru_br", "lru_wi", "lru_bi", "lru_lambda", "w_kv", "w_pa", "w_pb", "w_pc", "w_out")
    return (loss, grad_x, *[out[n][0] for n in order], *[out[n][1] for n in order], *[out[n][2] for n in order],
            *[out[n][3] for n in order])
```

```python
import functools

import jax
import jax.numpy as jnp
from jax import lax
from jax.experimental import pallas as pl
from jax.experimental.pallas import tpu as pltpu

F32 = jnp.float32
BF16 = jnp.bfloat16
SDS = jax.ShapeDtypeStruct
MESH = pl.DeviceIdType.MESH

D = 1024
NIN = 10 * D
MEM = 256
GB = 128
NG = 8
NH = 4
HD = D // NH
EPS = 1e-6
LRU_C = 8.0
ADAM_LR, ADAM_B1, ADAM_B2, ADAM_EPS, ADAM_WD, ADAM_STEP = 0.001, 0.9, 0.999, 1e-08, 0.01, 10
NDEV = 8
SMALL_ROWS = 12800

_CALL_KW = {}
HBM = pl.BlockSpec(memory_space=pltpu.HBM)
VMEM = pl.BlockSpec(memory_space=pltpu.VMEM)
SEM = pl.BlockSpec(memory_space=pltpu.SEMAPHORE)
ANY = pl.BlockSpec(memory_space=pl.ANY)
TOKEN = SDS((8, 128), F32)


def _pcall(body, *, name, in_specs, out_specs, out_shape, grid=None, scratch=(), vmem_mb=48, aliases=None, effect=False,
           prefetch=0):
    kw = dict(_CALL_KW)
    if aliases:
        kw["input_output_aliases"] = aliases
    params = dict(vmem_limit_bytes=vmem_mb << 20)
    if grid is not None:
        params["dimension_semantics"] = ("arbitrary",) * len(grid)
    if effect:
        params["has_side_effects"] = pltpu.SideEffectType.DATAFLOW_SIDE_EFFECTING
    if prefetch:
        kw["grid_spec"] = pltpu.PrefetchScalarGridSpec(num_scalar_prefetch=prefetch, grid=grid, in_specs=in_specs,
                                                       out_specs=out_specs, scratch_shapes=list(scratch))
    else:
        kw.update(in_specs=in_specs, out_specs=out_specs, scratch_shapes=list(scratch))
        if grid is not None:
            kw["grid"] = grid
    return pl.pallas_call(body, name=name, out_shape=out_shape, compiler_params=pltpu.CompilerParams(**params), **kw)


def _full(shape):
    nd = len(shape)
    return pl.BlockSpec(shape, lambda *_: (0,) * nd)


def _dot(a, b):
    return jnp.dot(a, b, preferred_element_type=F32)


def _dot_nt(a, b):
    return lax.dot_general(a, b, (((1,), (1,)), ((), ())), preferred_element_type=F32)


def _dot_tn(a, b):
    return lax.dot_general(a, b, (((0,), (0,)), ((), ())), preferred_element_type=F32)


def _rowsum(a):
    return jnp.sum(a, axis=0, keepdims=True)


def _sigmoid(x):
    return 0.5 * jnp.tanh(0.5 * x) + 0.5


def _silu_parts(g):
    s = _sigmoid(g)
    return g * s, s * (1.0 + g * (1.0 - s))


def _rms_scale(x):
    return lax.rsqrt(jnp.mean(x * x, axis=-1, keepdims=True) + EPS)


def _dz_col(k):
    return jnp.where(k < 3, k, jnp.where(k < 6, k + 4, k - 3))


def _coords():
    return lax.axis_index("x"), lax.axis_index("y"), lax.axis_index("c")


def _other_chips(x, y):
    return [(1 - x, y), (x, 1 - y), (1 - x, 1 - y)]


def _peer(x, y, c, mask):
    return (1 - x if mask & 4 else x, 1 - y if mask & 2 else y, 1 - c if mask & 1 else c)


def _remote(src, dst, ssem, rsem, k, to):
    return pltpu.make_async_remote_copy(src_ref=src, dst_ref=dst, send_sem=ssem.at[k], recv_sem=rsem.at[k], device_id=to,
                                        device_id_type=MESH)


def _w_half(a, ref, jj, cc):
    if a == 2:
        rp = ref.shape[1] // 4
        return ref.at[:, pl.ds(jj * rp + cc * (rp // 2), rp // 2), :]
    kin, nsh = ref.shape[0], ref.shape[1] // 4
    return ref.at[pl.ds(cc * (kin // 2), kin // 2), pl.ds(jj * nsh, nsh)]


def _cw_block(ref, jj):
    return ref.at[:, pl.ds(jj * (D // 4), D // 4)]


def _cast_place(l, pos, w_in, w_kv, w_pa, w_pb, w_pc, w_out, cw8):
    kin, nsh = w_in.shape[1], w_in.shape[2]
    nkv, rp = w_kv.shape[2], w_pa.shape[1]
    half = kin // 2

    def body(pos_r, win, wkv, pa, pb, pc, po, cw, Win, Wkv, Wp, Cw):
        Win[...] = win[...].astype(BF16)
        Wkv[...] = wkv[...].astype(BF16)

        @pl.when(pl.program_id(0) == 0)
        def _():
            for k, r in enumerate((pa, pb, pc, po)):
                Wp[k] = r[...].astype(BF16)
            Cw[...] = cw[...]

    proj = pl.BlockSpec((None, rp, D), lambda i, p: (l, 0, 0))
    return _pcall(
        body, name="cast_place", grid=(2,), prefetch=1,
        in_specs=[pl.BlockSpec((None, half, nsh), lambda i, p: (l, i, 0)), pl.BlockSpec((None, half, nkv), lambda i, p: (l, i, 0)),
                  proj, proj, proj, proj, pl.BlockSpec((None, 8, D // 4), lambda i, p: (l, 0, 0))],
        out_specs=[pl.BlockSpec((half, nsh), lambda i, p: (i, p[0])), pl.BlockSpec((half, nkv), lambda i, p: (i, p[0])),
                   pl.BlockSpec((4, rp, D), lambda i, p: (0, p[0], 0)), pl.BlockSpec((8, D // 4), lambda i, p: (0, p[0]))],
        out_shape=[SDS((kin, 4 * nsh), BF16), SDS((kin, 4 * nkv), BF16), SDS((4, 4 * rp, D), BF16), SDS((8, D), F32)],
    )(pos, w_in, w_kv, w_pa, w_pb, w_pc, w_out, cw8)


def _hbm_like(bufs):
    return [pltpu.HBM(b.shape, b.dtype) for b in bufs]


def _w_part(a, ref, jj, cc):
    return _cw_block(ref, jj) if a == 3 else _w_half(a, ref, jj, cc)


def _gather_start(name, kinds, bufs, after=None):
    n = len(kinds)
    extra = [] if after is None else [after]

    def body(*refs):
        w, ssem, rsem, token = refs[0:n], refs[n + len(extra)], refs[n + len(extra) + 1], refs[-1]
        x, y, c = _coords()
        j = 2 * x + y
        for k, chip in enumerate(_other_chips(x, y)):
            for i, a in enumerate(kinds):
                part = _w_part(a, w[i], j, c)
                _remote(part, part, ssem, rsem, i * 3 + k, (chip[0], chip[1], c)).start()
        token[...] = jnp.zeros((8, 128), F32)

    return _pcall(
        body, name=name, in_specs=[HBM] * n + [ANY] * len(extra), out_specs=[SEM, SEM] + [HBM] * n + [VMEM],
        out_shape=[pltpu.SemaphoreType.DMA((3 * n,)), pltpu.SemaphoreType.DMA((3 * n,))] + _hbm_like(bufs) + [TOKEN],
        aliases={i: 2 + i for i in range(n)}, effect=True,
    )(*[pltpu.with_memory_space_constraint(b, pltpu.HBM) for b in bufs], *extra)


def _gather_mid(name, kinds, started, after):
    n = len(kinds)
    fw = [i for i, a in enumerate(kinds) if a != 3]
    bufs = tuple(started[2:2 + n])

    def body(*refs):
        w, ssem, rsem, ssem2, rsem2, token = refs[0:n], refs[n], refs[n + 1], refs[n + 3], refs[n + 4], refs[-1]
        x, y, c = _coords()
        j = 2 * x + y
        me, sib = (x, y, c), (x, y, 1 - c)
        token[...] = jnp.zeros((8, 128), F32)
        chips = _other_chips(x, y)
        for k, chip in enumerate(chips):
            for i, a in enumerate(kinds):
                got = _w_part(a, w[i], 2 * chip[0] + chip[1], c)
                _remote(got, got, ssem, rsem, i * 3 + k, me).wait_recv()
        for k in range(3):
            for i, a in enumerate(kinds):
                part = _w_part(a, w[i], j, c)
                _remote(part, part, ssem, rsem, i * 3 + k, me).wait_send()
        for k, chip in enumerate(chips):
            for f, i in enumerate(fw):
                got = _w_half(kinds[i], w[i], 2 * chip[0] + chip[1], c)
                _remote(got, got, ssem2, rsem2, f * 3 + k, sib).start()

    return _pcall(
        body, name=name, in_specs=[HBM] * n + [SEM, SEM, ANY], out_specs=[SEM, SEM] + [HBM] * n + [VMEM],
        out_shape=[pltpu.SemaphoreType.DMA((3 * len(fw),)), pltpu.SemaphoreType.DMA((3 * len(fw),))] + _hbm_like(bufs) + [TOKEN],
        aliases={i: 2 + i for i in range(n)}, effect=True,
    )(*bufs, started[0], started[1], after)


def _gather_end(name, kinds, mid, after):
    n = len(kinds)
    fw = [i for i, a in enumerate(kinds) if a != 3]
    bufs = tuple(mid[2:2 + n])

    def body(*refs):
        w, ssem2, rsem2 = refs[0:n], refs[n], refs[n + 1]
        x, y, c = _coords()
        me = (x, y, c)
        for k, chip in enumerate(_other_chips(x, y)):
            for f, i in enumerate(fw):
                got = _w_half(kinds[i], w[i], 2 * chip[0] + chip[1], 1 - c)
                _remote(got, got, ssem2, rsem2, f * 3 + k, me).wait_recv()
                sent = _w_half(kinds[i], w[i], 2 * chip[0] + chip[1], c)
                _remote(sent, sent, ssem2, rsem2, f * 3 + k, me).wait_send()

    return _pcall(
        body, name=name, in_specs=[HBM] * n + [SEM, SEM, ANY], out_specs=[HBM] * n, out_shape=_hbm_like(bufs),
        aliases={i: i for i in range(n)}, effect=True,
    )(*bufs, mid[0], mid[1], after)


def _g_piece(a, ref, jd, dc):
    if a == 2:
        rp = ref.shape[1] // 4
        return ref.at[:, pl.ds(jd * rp + dc * (rp // 2), rp // 2), :]
    kin, nsh = ref.shape[0], ref.shape[1] // 4
    return ref.at[pl.ds(dc * (kin // 2), kin // 2), pl.ds(jd * nsh, nsh)]


def _a2a_start(name, kinds, grads, lands):
    n = len(kinds)

    def body(*refs):
        g, ld, ssem, rsem, token = refs[0:n], refs[n:2 * n], refs[2 * n], refs[2 * n + 1], refs[-1]
        x, y, c = _coords()
        for mask in range(1, NDEV):
            p = _peer(x, y, c, mask)
            for i, a in enumerate(kinds):
                _remote(_g_piece(a, g[i], 2 * p[0] + p[1], p[2]), ld[i].at[mask - 1], ssem, rsem, i * 7 + mask - 1, p).start()
        token[...] = jnp.zeros((8, 128), F32)

    bufs = tuple(grads) + tuple(lands)
    return _pcall(
        body, name=name, in_specs=[HBM] * (2 * n), out_specs=[SEM, SEM] + [HBM] * (2 * n) + [VMEM],
        out_shape=[pltpu.SemaphoreType.DMA((7 * n,)), pltpu.SemaphoreType.DMA((7 * n,))] + _hbm_like(bufs) + [TOKEN],
        aliases={i: 2 + i for i in range(2 * n)}, effect=True,
    )(*[pltpu.with_memory_space_constraint(b, pltpu.HBM) for b in bufs])


def _a2a_wait(name, kinds, started, after):
    n = len(kinds)
    ssem, rsem = started[0], started[1]
    bufs = tuple(started[2:2 + 2 * n])

    def body(*refs):
        g, ld, ssem, rsem = refs[0:n], refs[n:2 * n], refs[2 * n], refs[2 * n + 1]
        x, y, c = _coords()
        me = (x, y, c)
        for mask in range(1, NDEV):
            for i in range(n):
                got = ld[i].at[mask - 1]
                _remote(got, got, ssem, rsem, i * 7 + mask - 1, me).wait_recv()
        for mask in range(1, NDEV):
            p = _peer(x, y, c, mask)
            for i, a in enumerate(kinds):
                sent = _g_piece(a, g[i], 2 * p[0] + p[1], p[2])
                _remote(sent, sent, ssem, rsem, i * 7 + mask - 1, me).wait_send()

    return _pcall(
        body, name=name, in_specs=[HBM] * (2 * n) + [SEM, SEM, ANY], out_specs=[HBM] * (2 * n), out_shape=_hbm_like(bufs),
        aliases={i: i for i in range(2 * n)}, effect=True,
    )(*bufs, ssem, rsem, after)


def _sum_share(pos, lands, grads):
    rows, n = 128, 4
    widths = [ld.shape[2] for ld in lands]

    def body(pos_r, l0, w0, l1, w1, l2, w2, g0, g1, g2, b0, b1, b2, lsem, ssem, rsem):
        i = pl.program_id(0)
        x, y, c = _coords()
        sib = (x, y, 1 - c)
        ld, ow, gs, bufs = (l0, l1, l2), (w0, w1, w2), (g0, g1, g2), (b0, b1, b2)

        def dst(a, step):
            row = step * (2 * rows) + c * rows if a == 2 else c * (n * rows) + step * rows
            return gs[a].at[pl.ds(row, rows), :]

        def copies(a, step, sl):
            src = bufs[a].at[sl]
            lc = pltpu.make_async_copy(src, dst(a, step), lsem.at[a, sl])
            rc = pltpu.make_async_remote_copy(src_ref=src, dst_ref=dst(a, step), send_sem=ssem.at[a, sl], recv_sem=rsem.at[a],
                                              device_id=sib, device_id_type=MESH)
            return lc, rc

        def drain(a, step, sl):
            lc, rc = copies(a, step, sl)
            lc.wait()
            rc.wait_send()

        slot = i % 2

        @pl.when(i >= 2)
        def _():
            for a in range(3):
                drain(a, i - 2, slot)

        for a in range(3):
            acc = ow[a][...].astype(F32)
            for k in range(NDEV - 1):
                acc = acc + ld[a][k].astype(F32)
            bufs[a][slot] = acc
            lc, rc = copies(a, i, slot)
            lc.start()
            rc.start()

        @pl.when(i == n - 1)
        def _():
            for a in range(3):
                drain(a, n - 2, (n - 2) % 2)
                drain(a, n - 1, (n - 1) % 2)
                whole = gs[a].at[pl.ds(0, n * rows), :]
                pltpu.make_async_remote_copy(src_ref=whole, dst_ref=whole, send_sem=ssem.at[a, 0], recv_sem=rsem.at[a],
                                             device_id=(x, y, c), device_id_type=MESH).wait_recv()

    land = lambda w: pl.BlockSpec((NDEV - 1, rows, w), lambda i, p: (0, i, 0))
    in_specs = [land(widths[0]), pl.BlockSpec((rows, widths[0]), lambda i, p: (p[1] * n + i, p[0])),
                land(widths[1]), pl.BlockSpec((rows, widths[1]), lambda i, p: (p[1] * n + i, p[0])),
                land(widths[2]), pl.BlockSpec((None, rows, widths[2]), lambda i, p: (i, 2 * p[0] + p[1], 0))]
    args = [t for pair in zip(lands, grads) for t in pair]
    return _pcall(
        body, name="sum_share", grid=(n,), prefetch=1, in_specs=in_specs, out_specs=[HBM] * 3,
        out_shape=[SDS((2 * n * rows, w), F32) for w in widths],
        scratch=[pltpu.VMEM((2, rows, w), F32) for w in widths]
        + [pltpu.SemaphoreType.DMA((3, 2)), pltpu.SemaphoreType.DMA((3, 2)), pltpu.SemaphoreType.DMA((3,))],
    )(pos, *args)


def _dev_index(p):
    return 4 * p[0] + 2 * p[1] + p[2]


def _small_rows(ref, d):
    r8 = ref.shape[0] // NDEV
    return ref.at[pl.ds(d * r8, r8), :]


def _rs_start(pack, land):
    def body(p_ref, ld, ssem, rsem, o0, o1, token):
        x, y, c = _coords()
        for mask in range(1, NDEV):
            p = _peer(x, y, c, mask)
            _remote(_small_rows(p_ref, _dev_index(p)), ld.at[mask - 1], ssem, rsem, mask - 1, p).start()
        token[...] = jnp.zeros((8, 128), F32)

    bufs = (pack, land)
    return _pcall(
        body, name="rs_start", in_specs=[HBM] * 2, out_specs=[SEM, SEM, HBM, HBM, VMEM],
        out_shape=[pltpu.SemaphoreType.DMA((NDEV - 1,)), pltpu.SemaphoreType.DMA((NDEV - 1,))] + _hbm_like(bufs) + [TOKEN],
        aliases={0: 2, 1: 3}, effect=True,
    )(*[pltpu.with_memory_space_constraint(b, pltpu.HBM) for b in bufs])


def _rs_wait(started, after):
    ssem, rsem, pack, land, _ = started

    def body(p_ref, ld, ssem, rsem, after_r, o0, o1):
        x, y, c = _coords()
        for mask in range(1, NDEV):
            got = ld.at[mask - 1]
            _remote(got, got, ssem, rsem, mask - 1, (x, y, c)).wait_recv()
        for mask in range(1, NDEV):
            sent = _small_rows(p_ref, _dev_index(_peer(x, y, c, mask)))
            _remote(sent, sent, ssem, rsem, mask - 1, (x, y, c)).wait_send()

    return _pcall(body, name="rs_wait", in_specs=[HBM, HBM, SEM, SEM, ANY], out_specs=[HBM, HBM],
                  out_shape=_hbm_like((pack, land)), aliases={0: 0, 1: 1}, effect=True)(pack, land, ssem, rsem, after)


def _small_sum(me1, pack, land):
    R = pack.shape[0]
    r8 = R // NDEV

    def body(me_r, p_ref, ld, full):
        acc = p_ref[...]
        for k in range(NDEV - 1):
            acc = acc + ld[k]
        full[...] = acc

    own = pl.BlockSpec((r8, 128), lambda i, m: (m[0], 0))
    return _pcall(body, name="small_sum", grid=(1,), prefetch=1,
                  in_specs=[own, pl.BlockSpec((NDEV - 1, r8, 128), lambda i, m: (0, 0, 0))], out_specs=own,
                  out_shape=SDS((R, 128), F32), vmem_mb=32)(me1, pack, land)


def _ag_start(full):
    def body(f_ref, ssem, rsem, o0, token):
        x, y, c = _coords()
        mine = _small_rows(f_ref, _dev_index((x, y, c)))
        for mask in range(1, NDEV):
            _remote(mine, mine, ssem, rsem, mask - 1, _peer(x, y, c, mask)).start()
        token[...] = jnp.zeros((8, 128), F32)

    return _pcall(
        body, name="ag_start", in_specs=[HBM], out_specs=[SEM, SEM, HBM, VMEM],
        out_shape=[pltpu.SemaphoreType.DMA((NDEV - 1,)), pltpu.SemaphoreType.DMA((NDEV - 1,))] + _hbm_like((full,)) + [TOKEN],
        aliases={0: 2}, effect=True,
    )(pltpu.with_memory_space_constraint(full, pltpu.HBM))


def _ag_wait(started, after):
    ssem, rsem, full, _ = started

    def body(f_ref, ssem, rsem, after_r, o0):
        x, y, c = _coords()
        mine = _small_rows(f_ref, _dev_index((x, y, c)))
        for mask in range(1, NDEV):
            got = _small_rows(f_ref, _dev_index(_peer(x, y, c, mask)))
            _remote(got, got, ssem, rsem, mask - 1, (x, y, c)).wait_recv()
            _remote(mine, mine, ssem, rsem, mask - 1, (x, y, c)).wait_send()

    return _pcall(body, name="ag_wait", in_specs=[HBM, SEM, SEM, ANY], out_specs=[HBM], out_shape=_hbm_like((full,)),
                  aliases={0: 0}, effect=True)(full, ssem, rsem, after)[0]


def _mm_in(x, g, w):
    S = x.shape[0]
    tm, tn = min(1024, S), 1280

    def body(x_ref, g_ref, w_ref, z_ref, h_ref, hs):
        @pl.when(pl.program_id(1) == 0)
        def _():
            xv = x_ref[...]
            hb = (xv * _rms_scale(xv) * g_ref[...]).astype(BF16)
            hs[...] = hb
            h_ref[...] = hb

        z_ref[...] = _dot(hs[...], w_ref[...]).astype(BF16)

    return _pcall(
        body, name="mm_in", grid=(S // tm, NIN // tn),
        in_specs=[pl.BlockSpec((tm, D), lambda i, j: (i, 0)), _full((1, D)), pl.BlockSpec((D, tn), lambda i, j: (0, j))],
        out_specs=[pl.BlockSpec((tm, tn), lambda i, j: (i, j)), pl.BlockSpec((tm, D), lambda i, j: (i, 0))],
        out_shape=[SDS((S, NIN), BF16), SDS((S, D), BF16)], scratch=[pltpu.VMEM((tm, D), BF16)],
    )(x, g, w)


def _chunk_mask():
    ri = lax.broadcasted_iota(jnp.int32, (GB, GB), 0)
    ci = lax.broadcasted_iota(jnp.int32, (GB, GB), 1)
    return (ri >= 64) | (ci < 64)


def _layernorm_parts(v):
    mu = jnp.mean(v, axis=-1, keepdims=True)
    d = v - mu
    rs = lax.rsqrt(jnp.mean(d * d, axis=-1, keepdims=True) + EPS)
    return d * rs, rs


def _branch_a(z, lg, lb, ws, bsb):
    S = z.shape[0]
    T = min(512, S)

    def body(zu, zv, zg, lg_r, lb_r, ws_r, bs_r, ya):
        vhat, _ = _layernorm_parts(zv[...].astype(F32))
        vnb = (vhat * lg_r[...] + lb_r[...]).astype(BF16)
        sil, _ = _silu_parts(zg[...].astype(F32))
        t = zu[...].astype(F32) * sil
        mask = _chunk_mask()
        for g in range(NG):
            wg = jnp.where(mask, ws_r[g], 0.0).astype(BF16)
            cs = slice(g * GB, (g + 1) * GB)
            for n in range(T // GB):
                rs = slice(n * GB, (n + 1) * GB)
                sv = _dot(wg, vnb[rs, cs]) + bs_r[g]
                ya[rs, cs] = (t[rs, cs] * sv).astype(BF16)

    zs = lambda k: pl.BlockSpec((T, D), lambda i: (i, k))
    return _pcall(
        body, name="branch_a", grid=(S // T,),
        in_specs=[zs(0), zs(1), zs(2), _full((1, D)), _full((1, D)), _full((NG, GB, GB)), _full((NG, GB, GB))],
        out_specs=pl.BlockSpec((T, D), lambda i: (i, 0)), out_shape=SDS((S, D), BF16),
    )(z, z, z, lg, lb, ws, bsb)


def _softplus_neg(lam):
    e = jnp.exp(-jnp.abs(lam))
    l1p = jnp.where(e < 1e-2, e * (1.0 - e * (0.5 - e * (1.0 / 3.0))), jnp.log(1.0 + e))
    return jnp.maximum(-lam, 0.0) + l1p


CH = 16


def _ck(c, off=0):
    return pl.ds(c * CH + off, CH)


def _half_sum(v):
    return v[0:8, :] + v[8:16, :]


def _lru_conv(xpad, c, cw, cb):
    xk = [xpad[_ck(c, 5 + k), :] for k in range(4)]
    return xk, cb + (((xk[0] * cw[0:1] + xk[1] * cw[1:2]) + xk[2] * cw[2:3]) + xk[3] * cw[3:4])


def _lru_gate_matmuls(xcb_s, wr_r, wi_r, pr_s, pi_s):
    for h in range(NG):
        cs = slice(h * GB, (h + 1) * GB)
        pr_s[:, cs] = _dot(xcb_s[:, cs], wr_r[h].astype(BF16))
        pi_s[:, cs] = _dot(xcb_s[:, cs], wi_r[h].astype(BF16))


def _lru_gates(pr, pi, br, bi, sp8):
    r = jax.nn.sigmoid(pr + br)
    ig = _sigmoid(pi + bi)
    la = sp8 * r
    a = jnp.exp(la)
    a2 = a * a
    mult = jnp.sqrt(-jnp.tanh(la) * (a2 + 1.0))
    return r, ig, a, a2, mult


def _tile_rows():
    return lax.broadcasted_iota(jnp.int32, (8, D), 0)


def _scan_forward(a_s, u_s, h_s, hcar, T):
    row = _tile_rows()

    def tile(i, hp):
        o = pl.multiple_of(i * 8, 8)
        A = a_s[pl.ds(o, 8), :]
        U = u_s[pl.ds(o, 8), :]
        for s in (1, 2, 4):
            m = row >= s
            U = jnp.where(m, U + A * pltpu.roll(U, s, 0), U)
            A = jnp.where(m, A * pltpu.roll(A, s, 0), A)
        H = U + A * hp
        h_s[pl.ds(o, 8), :] = H
        return jnp.broadcast_to(H[7:8, :], (8, D))

    hcar[...] = lax.fori_loop(0, T // 8, tile, hcar[...])


def _scan_reverse(b_s, d_s, l_s, lcar, T):
    row = _tile_rows()
    n = T // 8

    def tile(i, lp):
        o = pl.multiple_of((n - 1 - i) * 8, 8)
        B = b_s[pl.ds(o, 8), :]
        U = d_s[pl.ds(o, 8), :]
        for s in (1, 2, 4):
            m = row < 8 - s
            U = jnp.where(m, U + B * pltpu.roll(U, 8 - s, 0), U)
            B = jnp.where(m, B * pltpu.roll(B, 8 - s, 0), B)
        Lm = U + B * lp
        l_s[pl.ds(o, 8), :] = Lm
        return jnp.broadcast_to(Lm[0:1, :], (8, D))

    lcar[...] = lax.fori_loop(0, n, tile, lcar[...])


def _branch_b(z, cw, cb, wr, br, wi, bi, lam):
    S = z.shape[0]
    T = min(256, S)

    def body(zxb, zgb, cw_r, cb_r, wr_r, br_r, wi_r, bi_r, lam_r, yb, hs_o, xpad, a_s, u_s, hcar):
        @pl.when(pl.program_id(0) == 0)
        def _():
            xpad[pl.ds(0, 8), :] = jnp.zeros((8, D), F32)
            hcar[...] = jnp.zeros((8, D), F32)

        cw = cw_r[...]
        xpad[pl.ds(8, T), :] = zxb[...].astype(F32)
        xk = [xpad[pl.ds(5 + k, T), :] for k in range(4)]
        xc = cb_r[...] + (((xk[0] * cw[0:1] + xk[1] * cw[1:2]) + xk[2] * cw[2:3]) + xk[3] * cw[3:4])
        xcb = xc.astype(BF16)
        pr, pi = [], []
        for h in range(NG):
            cs = slice(h * GB, (h + 1) * GB)
            pr.append(_dot(xcb[:, cs], wr_r[h].astype(BF16)))
            pi.append(_dot(xcb[:, cs], wi_r[h].astype(BF16)))
        _, ig, a, _, mult = _lru_gates(jnp.concatenate(pr, axis=1), jnp.concatenate(pi, axis=1), br_r[...], bi_r[...],
                                       -LRU_C * _softplus_neg(lam_r[...]))
        a_s[...] = a
        u_s[...] = mult * (ig * xc)
        _scan_forward(a_s, u_s, hs_o, hcar, T)
        xpad[pl.ds(0, 8), :] = xpad[pl.ds(T, 8), :]
        sil, _ = _silu_parts(zgb[...].astype(F32))
        yb[...] = (hs_o[...] * sil).astype(BF16)

    zs = lambda k: pl.BlockSpec((T, D), lambda i: (i, k))
    row = pl.BlockSpec((T, D), lambda i: (i, 0))
    return _pcall(
        body, name="branch_b", grid=(S // T,),
        in_specs=[zs(3), zs(4), _full((8, D)), _full((1, D)), _full((NG, GB, GB)), _full((1, D)), _full((NG, GB, GB)),
                  _full((1, D)), _full((1, D))],
        out_specs=[row, row], out_shape=[SDS((S, D), BF16), SDS((S, D), F32)],
        scratch=[pltpu.VMEM((T + 8, D), F32), pltpu.VMEM((T, D), F32), pltpu.VMEM((T, D), F32), pltpu.VMEM((8, D), F32)],
    )(z, z, cw, cb, wr, br, wi, bi, lam)


def _kv(mem, g, wkv):
    def body(m_ref, g_ref, w_ref, kv_ref):
        m = m_ref[...]
        mn = (m * _rms_scale(m) * g_ref[...]).astype(BF16)
        kv_ref[...] = _dot(mn, w_ref[...]).astype(BF16)

    return _pcall(body, name="mem_kv", in_specs=[VMEM] * 3, out_specs=VMEM, out_shape=SDS((MEM, 2 * D), BF16),
                  vmem_mb=32)(mem, g, wkv)


def _softmax_rows(s):
    e = jnp.exp(s - jnp.max(s, axis=-1, keepdims=True))
    return e / jnp.sum(e, axis=-1, keepdims=True)


def _branch_c(z, kv):
    S = z.shape[0]
    T = min(512, S)

    def body(zq, zg, kv_r, yc):
        sil, _ = _silu_parts(zg[...].astype(F32))
        for h in range(NH):
            cs = slice(h * HD, (h + 1) * HD)
            p = _softmax_rows(_dot_nt(zq[:, cs], kv_r[:, cs]) * (HD ** -0.5))
            att = _dot(p.astype(BF16), kv_r[:, D + h * HD:D + (h + 1) * HD])
            yc[:, cs] = (att * sil[:, cs]).astype(BF16)

    zs = lambda k: pl.BlockSpec((T, D), lambda i: (i, k))
    return _pcall(body, name="branch_c", grid=(S // T,), in_specs=[zs(5), zs(6), _full((MEM, 2 * D))],
                  out_specs=pl.BlockSpec((T, D), lambda i: (i, 0)), out_shape=SDS((S, D), BF16))(z, z, kv)


def _merge_out(ya, yb, yc, z, wp, x, pg):
    S = x.shape[0]
    T = min(256, S)

    def body(ya_r, yb_r, yc_r, m0, m1, m2, wp_r, x_r, pg_r, pa_o, pb_o, pc_o, mg_o, o_o, xn_o):
        merged = None
        for y_r, ml, p_o, k in ((ya_r, m0, pa_o, 0), (yb_r, m1, pb_o, 1), (yc_r, m2, pc_o, 2)):
            p = _dot(y_r[...], wp_r[k])
            p_o[...] = p.astype(BF16)
            t = _sigmoid(ml[...].astype(F32)) * p
            merged = t if merged is None else merged + t
        mb = merged.astype(BF16)
        mg_o[...] = mb
        o = _dot(mb, wp_r[3])
        o_o[...] = o.astype(BF16)
        xn_o[...] = x_r[...] + o * _rms_scale(o) * pg_r[...]

    row = pl.BlockSpec((T, D), lambda i: (i, 0))
    zs = lambda k: pl.BlockSpec((T, D), lambda i: (i, k))
    return _pcall(
        body, name="merge_out", grid=(S // T,),
        in_specs=[row, row, row, zs(7), zs(8), zs(9), _full((4, D, D)), row, _full((1, D))],
        out_specs=[row] * 6, out_shape=[SDS((S, D), BF16)] * 5 + [SDS((S, D), F32)], vmem_mb=56,
    )(ya, yb, yc, z, z, z, wp, x, pg)


def _loss_head(y, t):
    S = y.shape[0]
    T = min(512, S)

    def body(y_r, t_r, loss_o, dy_o):
        @pl.when(pl.program_id(0) == 0)
        def _():
            loss_o[...] = jnp.zeros((1, 1), F32)

        e = y_r[...] - t_r[...]
        dy_o[...] = e * (1.0 / D)
        loss_o[...] += 0.5 * _rowsum(jnp.sum(e * e, axis=1, keepdims=True) * (1.0 / D))

    row = pl.BlockSpec((T, D), lambda i: (i, 0))
    return _pcall(body, name="loss_head", grid=(S // T,), in_specs=[row, row], out_specs=[_full((1, 1)), row],
                  out_shape=[SDS((1, 1), F32), SDS((S, D), F32)])(y, t)


def _accumulate(first, ref, val):
    @pl.when(first)
    def _():
        ref[...] = val

    @pl.when(jnp.logical_not(first))
    def _():
        ref[...] += val


def _out_bwd(dxn, o, pg, wp, z, pa, pb, pc):
    S = dxn.shape[0]
    T = min(256, S)

    def body(dy_r, o_r, pg_r, wp_r, m0, m1, m2, pa_r, pb_r, pc_r, do_o, dpa_o, dpb_o, dpc_o, dya_o, dyb_o, dyc_o, dz_o, dg_o):
        dy = dy_r[...]
        o = o_r[...].astype(F32)
        r2 = _rms_scale(o)
        w = dy * pg_r[...]
        do = r2 * w - o * (r2 * r2 * r2) * jnp.mean(w * o, axis=-1, keepdims=True)
        _accumulate(pl.program_id(0) == 0, dg_o, _rowsum(dy * o * r2))
        dob = do.astype(BF16)
        do_o[...] = dob
        dm = _dot_nt(dob, wp_r[3])
        for k, (ml, p_r, dp_o, dy_o) in enumerate(((m0, pa_r, dpa_o, dya_o), (m1, pb_r, dpb_o, dyb_o), (m2, pc_r, dpc_o, dyc_o))):
            gk = _sigmoid(ml[...].astype(F32))
            dz_o[k] = (dm * p_r[...].astype(F32) * gk * (1.0 - gk)).astype(BF16)
            dpk = (gk * dm).astype(BF16)
            dp_o[...] = dpk
            dy_o[...] = _dot_nt(dpk, wp_r[k]).astype(BF16)

    row = pl.BlockSpec((T, D), lambda i: (i, 0))
    zs = lambda k: pl.BlockSpec((T, D), lambda i: (i, k))
    return _pcall(
        body, name="out_bwd", grid=(S // T,),
        in_specs=[row, row, _full((1, D)), _full((4, D, D)), zs(7), zs(8), zs(9), row, row, row],
        out_specs=[row] * 7 + [pl.BlockSpec((3, T, D), lambda i: (1, i, 0)), _full((1, D))],
        out_shape=[SDS((S, D), BF16)] * 7 + [SDS((10, S, D), BF16), SDS((1, D), F32)], vmem_mb=56,
    )(dxn, o, pg, wp, z, z, z, pa, pb, pc)


def _branch_a_bwd(z, dya, lg, lb, ws, bsb, dz):
    S = z.shape[0]
    T = min(512, S)
    nblk = S // T

    def body(zu, zv, zg, dy_r, lg_r, lb_r, ws_r, bs_r, dz_in, dz_o, dws_o, dbs_o, dlg_o, dlb_o, dvn_s, bacc):
        i = pl.program_id(0)

        @pl.when(i == 0)
        def _():
            dws_o[...] = jnp.zeros((NG, GB, GB), F32)
            bacc[...] = jnp.zeros((NG, GB, GB), F32)

        vhat, rs = _layernorm_parts(zv[...].astype(F32))
        vnb = (vhat * lg_r[...] + lb_r[...]).astype(BF16)
        ga = zg[...].astype(F32)
        sil, dsil = _silu_parts(ga)
        u = zu[...].astype(F32)
        dy = dy_r[...].astype(F32)
        t = dy * sil
        dsv_all = t * u
        dga_pre = dy * u * dsil
        mask = _chunk_mask()
        for g in range(NG):
            wf = jnp.where(mask, ws_r[g], 0.0)
            wg = wf.astype(BF16)
            wgt = wf.T.astype(BF16)
            cs = slice(g * GB, (g + 1) * GB)
            dw = jnp.zeros((GB, GB), F32)
            db = jnp.zeros((GB, GB), F32)
            for n in range(T // GB):
                rsl = slice(n * GB, (n + 1) * GB)
                vb = vnb[rsl, cs]
                sv = _dot(wg, vb) + bs_r[g]
                dz_o[0, rsl, cs] = (t[rsl, cs] * sv).astype(BF16)
                dz_o[2, rsl, cs] = (dga_pre[rsl, cs] * sv).astype(BF16)
                dsv = dsv_all[rsl, cs]
                dsb = dsv.astype(BF16)
                dvn_s[rsl, cs] = _dot(wgt, dsb)
                dw = dw + _dot_nt(dsb, vb)
                db = db + dsv
            dws_o[g] += jnp.where(mask, dw, 0.0)
            bacc[g] += db
        dvn = dvn_s[...]
        dvh = dvn * lg_r[...]
        dv = rs * (dvh - jnp.mean(dvh, axis=-1, keepdims=True) - vhat * jnp.mean(dvh * vhat, axis=-1, keepdims=True))
        dz_o[1] = dv.astype(BF16)
        _accumulate(i == 0, dlg_o, _rowsum(dvn * vhat))
        _accumulate(i == 0, dlb_o, _rowsum(dvn))

        @pl.when(i == nblk - 1)
        def _():
            for g in range(NG):
                dbs_o[g:g + 1, :] = _rowsum(bacc[g].T)

    zs = lambda k: pl.BlockSpec((T, D), lambda i: (i, k))
    return _pcall(
        body, name="branch_a_bwd", grid=(nblk,),
        in_specs=[zs(0), zs(1), zs(2), pl.BlockSpec((T, D), lambda i: (i, 0)), _full((1, D)), _full((1, D)),
                  _full((NG, GB, GB)), _full((NG, GB, GB)), HBM],
        out_specs=[pl.BlockSpec((3, T, D), lambda i: (0, i, 0)), _full((NG, GB, GB)), _full((NG, GB)), _full((1, D)),
                   _full((1, D))],
        out_shape=[SDS((10, S, D), BF16), SDS((NG, GB, GB), F32), SDS((NG, GB), F32), SDS((1, D), F32), SDS((1, D), F32)],
        scratch=[pltpu.VMEM((T, D), F32), pltpu.VMEM((NG, GB, GB), F32)], aliases={8: 0},
    )(z, z, z, dya, lg, lb, ws, bsb, dz)


def _branch_b_bwd(z, hs, dyb, cw, cb, wr, br, wi, bi, lam, dz):
    S = z.shape[0]
    T = min(256, S)
    nblk = S // T

    def body(zxb, zprev, zgb, hs_r, hprev_r, dy_r, cw_r, cb_r, wr_r, br_r, wi_r, bi_r, lam_r, dz_in,
             dz_o, dcw_o, dcb_o, dwr_o, dbr_o, dwi_o, dbi_o, dlam_o, xpad, hpad, apad, dpad, xc_s, pr_s, pi_s, r_s, ig_s, m_s,
             b_s, d_s, l_s, back_s, xcb_s, dprb_s, dpib_s, lcar):
        i = pl.program_id(0)
        blk = nblk - 1 - i
        first = i == 0

        @pl.when(first)
        def _():
            apad[pl.ds(T, 8), :] = jnp.zeros((8, D), F32)
            dpad[pl.ds(T, 8), :] = jnp.zeros((8, D), F32)
            lcar[...] = jnp.zeros((8, D), F32)
            dcw_o[...] = jnp.zeros((8, D), F32)
            dwr_o[...] = jnp.zeros((NG, GB, GB), F32)
            dwi_o[...] = jnp.zeros((NG, GB, GB), F32)

        keep = (blk > 0).astype(F32)
        nck = T // CH
        cw, cb, br, bi, lam = cw_r[...], cb_r[...], br_r[...], bi_r[...], lam_r[...]
        sp8 = -LRU_C * _softplus_neg(lam)
        xpad[pl.ds(0, 8), :] = zprev[...].astype(F32)[8:16, :] * keep
        hpad[pl.ds(0, 8), :] = hprev_r[...] * keep
        for c in range(nck):
            xpad[_ck(c, 8), :] = zxb[_ck(c), :].astype(F32)
            hpad[_ck(c, 8), :] = hs_r[_ck(c), :]
            _, xc = _lru_conv(xpad, c, cw, cb)
            xc_s[_ck(c), :] = xc
            xcb_s[_ck(c), :] = xc.astype(BF16)
        _lru_gate_matmuls(xcb_s, wr_r, wi_r, pr_s, pi_s)
        for c in range(nck):
            r, ig, a, _, mult = _lru_gates(pr_s[_ck(c), :], pi_s[_ck(c), :], br, bi, sp8)
            r_s[_ck(c), :] = r
            ig_s[_ck(c), :] = ig
            m_s[_ck(c), :] = mult
            apad[_ck(c), :] = a
            sil, dsil = _silu_parts(zgb[_ck(c), :].astype(F32))
            dy = dy_r[_ck(c), :].astype(F32)
            dz_o[1, _ck(c), :] = (dy * hs_r[_ck(c), :] * dsil).astype(BF16)
            d_s[_ck(c), :] = dy * sil
        for c in range(nck):
            b_s[_ck(c), :] = apad[_ck(c, 1), :]
        _scan_reverse(b_s, d_s, l_s, lcar, T)
        s_sp = s_br = s_bi = jnp.zeros((8, D), F32)
        for c in range(nck):
            lm, r, ig, mult, a, xc = l_s[_ck(c), :], r_s[_ck(c), :], ig_s[_ck(c), :], m_s[_ck(c), :], apad[_ck(c), :], xc_s[_ck(c), :]
            t = lm * mult
            dpad[_ck(c), :] = t * ig
            dl = lm * hpad[_ck(c, 7), :] * a - (lm * ig * xc) * (a * a) / mult
            dpr = dl * sp8 * r * (1.0 - r)
            dpi = t * xc * ig * (1.0 - ig)
            s_sp = s_sp + _half_sum(dl * r)
            s_br = s_br + _half_sum(dpr)
            s_bi = s_bi + _half_sum(dpi)
            dprb_s[_ck(c), :] = dpr.astype(BF16)
            dpib_s[_ck(c), :] = dpi.astype(BF16)
        _accumulate(first, dlam_o, _rowsum(s_sp) * (LRU_C * jax.nn.sigmoid(-lam)))
        _accumulate(first, dbr_o, _rowsum(s_br))
        _accumulate(first, dbi_o, _rowsum(s_bi))
        for h in range(NG):
            cs = slice(h * GB, (h + 1) * GB)
            back_s[:, cs] = _dot_nt(dprb_s[:, cs], wr_r[h].astype(BF16)) + _dot_nt(dpib_s[:, cs], wi_r[h].astype(BF16))
            dwr_o[h] += _dot_tn(xcb_s[:, cs], dprb_s[:, cs])
            dwi_o[h] += _dot_tn(xcb_s[:, cs], dpib_s[:, cs])
        s_cb = jnp.zeros((8, D), F32)
        s_cw = [jnp.zeros((8, D), F32)] * 4
        for c in range(nck):
            dxc = dpad[_ck(c), :] + back_s[_ck(c), :]
            dpad[_ck(c), :] = dxc
            s_cb = s_cb + _half_sum(dxc)
            s_cw = [s_cw[k] + _half_sum(xpad[_ck(c, 5 + k), :] * dxc) for k in range(4)]
        _accumulate(first, dcb_o, _rowsum(s_cb))
        for k in range(4):
            dcw_o[k:k + 1, :] += _rowsum(s_cw[k])
        for c in range(nck):
            dxb = ((dpad[_ck(c, 3), :] * cw[0:1] + dpad[_ck(c, 2), :] * cw[1:2]) + dpad[_ck(c, 1), :] * cw[2:3]) + dpad[_ck(c), :] * cw[3:4]
            dz_o[0, _ck(c), :] = dxb.astype(BF16)
        apad[pl.ds(T, 8), :] = apad[pl.ds(0, 8), :]
        dpad[pl.ds(T, 8), :] = dpad[pl.ds(0, 8), :]

    rev = lambda k: pl.BlockSpec((T, D), lambda i: (nblk - 1 - i, k))
    prev16 = pl.BlockSpec((16, D), lambda i: (jnp.maximum((nblk - 1 - i) * (T // 16) - 1, 0), 3))
    prev8 = pl.BlockSpec((8, D), lambda i: (jnp.maximum((nblk - 1 - i) * (T // 8) - 1, 0), 0))
    vec, mat = _full((1, D)), _full((NG, GB, GB))
    return _pcall(
        body, name="branch_b_bwd", grid=(nblk,),
        in_specs=[rev(3), prev16, rev(4), rev(0), prev8, rev(0), _full((8, D)), vec, mat, vec, mat, vec, vec, HBM],
        out_specs=[pl.BlockSpec((2, T, D), lambda i: (3, nblk - 1 - i, 0)), _full((8, D)), vec, mat, vec, mat, vec, vec],
        out_shape=[SDS((10, S, D), BF16), SDS((8, D), F32), SDS((1, D), F32), SDS((NG, GB, GB), F32), SDS((1, D), F32),
                   SDS((NG, GB, GB), F32), SDS((1, D), F32), SDS((1, D), F32)],
        scratch=[pltpu.VMEM((T + 8, D), F32)] * 4 + [pltpu.VMEM((T, D), F32)] * 10 + [pltpu.VMEM((T, D), BF16)] * 3
        + [pltpu.VMEM((8, D), F32)],
        aliases={13: 0}, vmem_mb=56,
    )(z, z, z, hs, hs, dyb, cw, cb, wr, br, wi, bi, lam, dz)


def _branch_c_bwd(z, kv, dyc, dz):
    S = z.shape[0]
    T = min(512, S)

    def body(zq, zg, kv_r, dy_r, dz_in, dz_o, dkv_o):
        @pl.when(pl.program_id(0) == 0)
        def _():
            dkv_o[...] = jnp.zeros((MEM, 2 * D), F32)

        gc = zg[...].astype(F32)
        sil, dsil = _silu_parts(gc)
        dy = dy_r[...].astype(F32)
        datt = dy * sil
        dgc_pre = dy * dsil
        scale = HD ** -0.5
        for h in range(NH):
            cs = slice(h * HD, (h + 1) * HD)
            vs = slice(D + h * HD, D + (h + 1) * HD)
            qh = zq[:, cs]
            p = _softmax_rows(_dot_nt(qh, kv_r[:, cs]) * scale)
            pb = p.astype(BF16)
            att = _dot(pb, kv_r[:, vs])
            dz_o[1, :, cs] = (dgc_pre[:, cs] * att).astype(BF16)
            dab = datt[:, cs].astype(BF16)
            dp = _dot_nt(dab, kv_r[:, vs])
            ds = (p * (dp - jnp.sum(p * dp, axis=-1, keepdims=True)) * scale).astype(BF16)
            dz_o[0, :, cs] = _dot(ds, kv_r[:, cs]).astype(BF16)
            dkv_o[:, cs] += _dot_tn(ds, qh)
            dkv_o[:, vs] += _dot_tn(pb, dab)

    zs = lambda k: pl.BlockSpec((T, D), lambda i: (i, k))
    return _pcall(
        body, name="branch_c_bwd", grid=(S // T,),
        in_specs=[zs(5), zs(6), _full((MEM, 2 * D)), pl.BlockSpec((T, D), lambda i: (i, 0)), HBM],
        out_specs=[pl.BlockSpec((2, T, D), lambda i: (4, i, 0)), _full((MEM, 2 * D))],
        out_shape=[SDS((10, S, D), BF16), SDS((MEM, 2 * D), F32)], aliases={4: 0},
    )(z, z, kv, dyc, dz)


def _mm_dh(dz, w, x, dxn, g):
    S = x.shape[0]
    tm = min(1024, S)

    def body(dz_r, w_r, x_r, dxn_r, g_r, dx_o, dg_o, acc):
        i, k = pl.program_id(0), pl.program_id(1)
        _accumulate(k == 0, acc, _dot_nt(dz_r[0], w_r[...]))

        @pl.when(k == 9)
        def _():
            dh = acc[...]
            xv = x_r[...]
            r1 = _rms_scale(xv)
            wv = dh * g_r[...]
            dx_o[...] = dxn_r[...] + r1 * wv - xv * (r1 * r1 * r1) * jnp.mean(wv * xv, axis=-1, keepdims=True)
            _accumulate(i == 0, dg_o, _rowsum(dh * xv * r1))

    row = pl.BlockSpec((tm, D), lambda i, k: (i, 0))
    return _pcall(
        body, name="mm_dh", grid=(S // tm, 10),
        in_specs=[pl.BlockSpec((1, tm, D), lambda i, k: (k, i, 0)), pl.BlockSpec((D, D), lambda i, k: (0, _dz_col(k))),
                  row, row, _full((1, D))],
        out_specs=[row, _full((1, D))], out_shape=[SDS((S, D), F32), SDS((1, D), F32)],
        scratch=[pltpu.VMEM((tm, D), F32)],
    )(dz, w, x, dxn, g)


def _mm_dwin(h, dz):
    S = h.shape[0]
    tk = min(1024, S)
    nk = S // tk

    def body(h_r, dz_r, o_r, acc):
        k = pl.program_id(1)
        _accumulate(k == 0, acc, _dot_tn(h_r[...], dz_r[0]))

        @pl.when(k == nk - 1)
        def _():
            o_r[...] = acc[...].astype(BF16)

    return _pcall(
        body, name="mm_dwin", grid=(10, nk),
        in_specs=[pl.BlockSpec((tk, D), lambda n, k: (k, 0)), pl.BlockSpec((1, tk, D), lambda n, k: (n, k, 0))],
        out_specs=pl.BlockSpec((D, D), lambda n, k: (0, _dz_col(n))), out_shape=SDS((D, NIN), BF16),
        scratch=[pltpu.VMEM((D, D), F32)],
    )(h, dz)


def _mm_tn4(a4, b4):
    S = a4[0].shape[0]
    tk = min(1024, S)
    nk = S // tk

    def body(*refs):
        a_r, b_r, o_r, acc = refs[0:4], refs[4:8], refs[8], refs[9]
        w, k = pl.program_id(0), pl.program_id(1)
        for a in range(4):
            @pl.when(w == a)
            def _(a=a):
                _accumulate(k == 0, acc, _dot_tn(a_r[a][...], b_r[a][...]))

        @pl.when(k == nk - 1)
        def _():
            o_r[...] = acc[...].astype(BF16)

    def blk(a):
        return pl.BlockSpec((tk, D), lambda w, k: (jnp.where(w == a, k, jnp.where(w < a, 0, nk - 1)), 0))

    return _pcall(body, name="mm_tn4", grid=(4, nk), in_specs=[blk(a) for a in range(4)] * 2,
                  out_specs=pl.BlockSpec((None, D, D), lambda w, k: (w, 0, 0)), out_shape=SDS((4, D, D), BF16),
                  scratch=[pltpu.VMEM((D, D), F32)])(*a4, *b4)


def _mem_bwd(mem, g, wkv, dkv, dg_acc):
    def body(m_ref, g_ref, w_ref, dkv_ref, acc_ref, dw_ref, dg_ref):
        m = m_ref[...]
        mr = m * _rms_scale(m)
        mn = (mr * g_ref[...]).astype(BF16)
        dkb = dkv_ref[...].astype(BF16)
        dw_ref[...] = _dot_tn(mn, dkb).astype(BF16)
        dg_ref[...] = acc_ref[...] + _rowsum(_dot_nt(dkb, w_ref[...]) * mr)

    return _pcall(body, name="mem_bwd", in_specs=[VMEM] * 5, out_specs=[VMEM] * 2,
                  out_shape=[SDS((D, 2 * D), BF16), SDS((1, D), F32)], vmem_mb=48)(mem, g, wkv, dkv, dg_acc)


def _adamw_math(w, g, m, v):
    m2 = ADAM_B1 * m + (1.0 - ADAM_B1) * g
    v2 = ADAM_B2 * v + (1.0 - ADAM_B2) * (g * g)
    mh = m2 / (1.0 - ADAM_B1 ** ADAM_STEP)
    vh = v2 / (1.0 - ADAM_B2 ** ADAM_STEP)
    return -ADAM_LR * (mh / (jnp.sqrt(vh) + ADAM_EPS) + ADAM_WD * w), m2, v2


def _adamw_layer(l, w, m, v, g, prev, which=None, rows=256):
    L, R, C = w.shape

    def body(w_r, m_r, v_r, g_r, *rest):
        g_o, d_o, m_o, v_o = rest[-4:]
        g = g_r[...]
        d, m2, v2 = _adamw_math(w_r[...], g, m_r[...], v_r[...])
        g_o[...] = g
        d_o[...] = d
        m_o[...] = m2
        v_o[...] = v2

    st = pl.BlockSpec((None, rows, C), lambda i: (l, i, 0))
    gs = pl.BlockSpec((rows, C), lambda i: (i, 0)) if which is None else pl.BlockSpec((None, rows, C), lambda i: (which, i, 0))
    carried = list(prev) if prev is not None else []
    return _pcall(body, name="adamw_layer", grid=(R // rows,), in_specs=[st] * 3 + [gs] + [HBM] * len(carried),
                  out_specs=[st] * 4, out_shape=[SDS(w.shape, F32)] * 4, vmem_mb=56,
                  aliases={4 + k: k for k in range(len(carried))} or None)(w, m, v, g, *carried)


def _adamw_proj(l, trios, g_p, prevs, rows=128):
    L, R, C = trios[0][0].shape

    def body(*refs):
        ins, g_r, outs = refs[0:12], refs[12], refs[-16:]
        for k in range(4):
            g = g_r[k]
            d, m2, v2 = _adamw_math(ins[3 * k][...], g, ins[3 * k + 1][...], ins[3 * k + 2][...])
            for o, val in zip(outs[4 * k:4 * k + 4], (g, d, m2, v2)):
                o[...] = val

    st = pl.BlockSpec((None, rows, C), lambda i: (l, i, 0))
    carried = [a for p in prevs for a in p] if prevs[0] is not None else []
    res = _pcall(body, name="adamw_proj", grid=(R // rows,),
                 in_specs=[st] * 12 + [pl.BlockSpec((4, rows, C), lambda i: (0, i, 0))] + [HBM] * len(carried),
                 out_specs=[st] * 16, out_shape=[SDS(trios[0][0].shape, F32)] * 16, vmem_mb=56,
                 aliases={13 + k: k for k in range(len(carried))} or None)(*[a for t in trios for a in t], g_p, *carried)
    return [res[4 * k:4 * k + 4] for k in range(4)]


def _adamw_flat(w, m, v, g, rows):
    R, C = w.shape

    def body(w_r, m_r, v_r, g_r, d_o, m_o, v_o):
        d, m2, v2 = _adamw_math(w_r[...], g_r[...], m_r[...], v_r[...])
        d_o[...] = d
        m_o[...] = m2
        v_o[...] = v2

    blk = pl.BlockSpec((rows, C), lambda i: (i, 0))
    return _pcall(body, name="adamw_flat", grid=(R // rows,), in_specs=[blk] * 4, out_specs=[blk] * 3,
                  out_shape=[SDS((R, C), F32)] * 3)(w, m, v, g)


_SMALL = ("mem_norm_g", "pre_norm_g", "post_norm_g", "gmlp_ln_g", "gmlp_ln_b", "gmlp_ws", "gmlp_bs", "conv_b", "lru_wr",
          "lru_br", "lru_wi", "lru_bi", "lru_lambda")


def _pack_small(parts, conv_w_part):
    rows = [parts[n].reshape(-1, 128) for n in _SMALL] + [conv_w_part.reshape(-1, 128)]
    used = sum(r.shape[0] for r in rows)
    rows.append(jnp.zeros((SMALL_ROWS - used, 128), F32))
    return jnp.concatenate(rows, axis=0)


def _unpack_small(pack, shapes):
    out, at = {}, 0
    for n in _SMALL:
        size = 1
        for s in shapes[n]:
            size *= s
        out[n] = pack[at:at + size // 128].reshape(shapes[n])
        at += size // 128
    return out, at


def kernel(x, mem, mem_norm_g, pre_norm_g, post_norm_g, w_in, gmlp_ln_g, gmlp_ln_b, gmlp_ws, gmlp_bs, conv_w, conv_b, lru_wr, lru_br, lru_wi, lru_bi, lru_lambda, w_kv, w_pa, w_pb, w_pc, w_out, loss_target, m_mem_norm_g, m_pre_norm_g, m_post_norm_g, m_w_in, m_gmlp_ln_g, m_gmlp_ln_b, m_gmlp_ws, m_gmlp_bs, m_conv_w, m_conv_b, m_lru_wr, m_lru_br, m_lru_wi, m_lru_bi, m_lru_lambda, m_w_kv, m_w_pa, m_w_pb, m_w_pc, m_w_out, v_mem_norm_g, v_pre_norm_g, v_post_norm_g, v_w_in, v_gmlp_ln_g, v_gmlp_ln_b, v_gmlp_ws, v_gmlp_bs, v_conv_w, v_conv_b, v_lru_wr, v_lru_br, v_lru_wi, v_lru_bi, v_lru_lambda, v_w_kv, v_w_pa, v_w_pb, v_w_pc, v_w_out):
    L = w_in.shape[0]
    S = x.shape[1]
    xs = [x[0]]
    mem2 = mem[0]
    mg = mem_norm_g.reshape(1, D)
    vec = lambda a, l: a[l].reshape(1, D)
    ci = lax.axis_index("c")
    jpos = 2 * lax.axis_index("x") + lax.axis_index("y")
    pos = jnp.reshape(jpos, (1,)).astype(jnp.int32)
    pos2 = jnp.stack([jpos, ci]).astype(jnp.int32)

    cw8 = jnp.pad(conv_w, ((0, 0), (0, 4), (0, 0)))
    given = dict(mem_norm_g=(mem_norm_g, m_mem_norm_g, v_mem_norm_g), pre_norm_g=(pre_norm_g, m_pre_norm_g, v_pre_norm_g),
                 post_norm_g=(post_norm_g, m_post_norm_g, v_post_norm_g), gmlp_ln_g=(gmlp_ln_g, m_gmlp_ln_g, v_gmlp_ln_g),
                 gmlp_ln_b=(gmlp_ln_b, m_gmlp_ln_b, v_gmlp_ln_b), gmlp_ws=(gmlp_ws, m_gmlp_ws, v_gmlp_ws),
                 gmlp_bs=(gmlp_bs, m_gmlp_bs, v_gmlp_bs), conv_b=(conv_b, m_conv_b, v_conv_b), lru_wr=(lru_wr, m_lru_wr, v_lru_wr),
                 lru_br=(lru_br, m_lru_br, v_lru_br), lru_wi=(lru_wi, m_lru_wi, v_lru_wi), lru_bi=(lru_bi, m_lru_bi, v_lru_bi),
                 lru_lambda=(lru_lambda, m_lru_lambda, v_lru_lambda))
    shapes = {n: given[n][0].shape for n in _SMALL}
    zero_cw = jnp.zeros((L, 4, D), F32)
    packs = [_pack_small({n: given[n][k] for n in _SMALL}, zero_cw) for k in range(3)]

    placed = [_cast_place(0, pos, w_in, w_kv, w_pa, w_pb, w_pc, w_out, cw8)]
    W = [None] * L
    ALL = (0, 1, 2, 3)
    first = _gather_start("gather_start_0a", (0,), placed[0][0:1])
    rest = _gather_start("gather_start_0b", (1, 2, 3), placed[0][1:4], first[-1])
    cw8 = cw8 + rest[-1][0, 0]
    placed += [_cast_place(l, pos, w_in, w_kv, w_pa, w_pb, w_pc, w_out, cw8) for l in range(1, L)]
    busy = placed[L - 1][3][0:1, 0:128] + packs[0][0:1, :] + packs[1][0:1, :] + packs[2][0:1, :]
    mid = _gather_mid("gather_mid_0a", (0,), first, busy)
    started = _gather_start("gather_start_1", ALL, placed[1], mid[-1])
    win0 = _gather_end("gather_end_0a", (0,), mid, started[-1])[0]
    zh0 = _mm_in(xs[0], vec(pre_norm_g, 0), win0)
    mid = _gather_mid("gather_mid_0b", (1, 2, 3), rest, zh0[1])
    W[0] = [win0] + list(_gather_end("gather_end_0b", (1, 2, 3), mid, mid[-1]))

    saved = []
    for l in range(L):
        Win, Wkv, Wp, Cw = W[l]
        z, h = zh0 if l == 0 else _mm_in(xs[l], vec(pre_norm_g, l), Win)
        bsb = jnp.broadcast_to(gmlp_bs[l][:, :, None], (NG, GB, GB))
        ya = _branch_a(z, vec(gmlp_ln_g, l), vec(gmlp_ln_b, l), gmlp_ws[l], bsb)
        yb, hs = _branch_b(z, Cw, vec(conv_b, l), lru_wr[l], vec(lru_br, l), lru_wi[l], vec(lru_bi, l), vec(lru_lambda, l))
        kv = _kv(mem2, mg, Wkv)
        yc = _branch_c(z, kv)
        pg = vec(post_norm_g, l)
        if l + 1 < L:
            mid = _gather_mid(f"gather_mid_{l + 1}", ALL, started, yc)
            if l + 2 < L:
                started = _gather_start(f"gather_start_{l + 2}", ALL, placed[l + 2], mid[-1])
                pg = pg + started[-1][0, 0]
        pa, pb, pc, mgd, o, xn = _merge_out(ya, yb, yc, z, Wp, xs[l], pg)
        if l + 1 < L:
            W[l + 1] = _gather_end(f"gather_end_{l + 1}", ALL, mid, xn)
        xs.append(xn)
        saved.append((z, h, ya, yb, yc, hs, kv, pa, pb, pc, mgd, o, bsb))

    loss11, dxn = _loss_head(xs[L], loss_target[0])
    loss = lax.psum(loss11[0, 0], ("x", "y", "c"))

    big = dict(w_in=(w_in, m_w_in, v_w_in), w_kv=(w_kv, m_w_kv, v_w_kv), w_pa=(w_pa, m_w_pa, v_w_pa),
               w_pb=(w_pb, m_w_pb, v_w_pb), w_pc=(w_pc, m_w_pc, v_w_pc), w_out=(w_out, m_w_out, v_w_out))
    out = {n: None for n in big}
    kin, nsh, nkv, rp = w_in.shape[1], w_in.shape[2], w_kv.shape[2], w_pa.shape[1]

    def finish_layer(l, a2a, after):
        g_p4, lp = _a2a_wait(f"a2a_p_wait_{l}", (2,), a2a[0], after)
        g_in, g_kv, lin, lkv = _a2a_wait(f"a2a_w_wait_{l}", (0, 1), a2a[1], after)
        g_in, g_kv, g_p = _sum_share(pos2, (lin, lkv, lp.reshape(NDEV - 1, 2 * rp, D)), (g_in, g_kv, g_p4))
        g_p = g_p.reshape(4, rp, D)
        out["w_in"] = _adamw_layer(l, *big["w_in"], g_in, out["w_in"])
        out["w_kv"] = _adamw_layer(l, *big["w_kv"], g_kv, out["w_kv"])
        proj = ("w_pa", "w_pb", "w_pc", "w_out")
        for n, res in zip(proj, _adamw_proj(l, [big[n] for n in proj], g_p, [out[n] for n in proj])):
            out[n] = res

    small = {n: [None] * L for n in _SMALL}
    dconv_w = [None] * L
    dg_mem = jnp.zeros((1, D), F32)
    pending = None
    sent = []
    for l in reversed(range(L)):
        Win, Wkv, Wp, Cw = W[l]
        z, h, ya, yb, yc, hs, kv, pa, pb, pc, mgd, o, bsb = saved[l]
        pg = vec(post_norm_g, l) if pending is None else vec(post_norm_g, l) + pending[1][1][-1][0, 0]
        do, dpa, dpb, dpc, dya, dyb, dyc, dz, dgpost = _out_bwd(dxn, o, pg, Wp, z, pa, pb, pc)
        a2a_p = _a2a_start(f"a2a_p_start_{l}", (2,), (_mm_tn4((ya, yb, yc, mgd), (dpa, dpb, dpc, do)),),
                           (lax.empty((NDEV - 1, 4, rp // 2, D), BF16),))
        dz, dws, dbs, dlg, dlb = _branch_a_bwd(z, dya, vec(gmlp_ln_g, l) + a2a_p[-1][0, 0], vec(gmlp_ln_b, l), gmlp_ws[l], bsb, dz)
        dz, dcw, dcb, dwr, dbr, dwi, dbi, dlam = _branch_b_bwd(
            z, hs, dyb, Cw, vec(conv_b, l), lru_wr[l], vec(lru_br, l), lru_wi[l], vec(lru_bi, l), vec(lru_lambda, l), dz)
        dz, dkv = _branch_c_bwd(z, kv, dyc, dz)
        g_in = _mm_dwin(h, dz)
        g_kv, dg_mem = _mem_bwd(mem2, mg, Wkv, dkv, dg_mem)
        a2a_w = _a2a_start(f"a2a_w_start_{l}", (0, 1), (g_in, g_kv),
                           (lax.empty((NDEV - 1, kin // 2, nsh), BF16), lax.empty((NDEV - 1, kin // 2, nkv), BF16)))
        dx, dgpre = _mm_dh(dz, Win, xs[l], dxn, vec(pre_norm_g, l) + a2a_w[-1][0, 0])
        pending = (l, (a2a_p, a2a_w))
        sent.append(pending)
        for n, val in (("pre_norm_g", dgpre), ("post_norm_g", dgpost), ("gmlp_ln_g", dlg), ("gmlp_ln_b", dlb), ("gmlp_ws", dws),
                       ("gmlp_bs", dbs), ("conv_b", dcb), ("lru_wr", dwr), ("lru_br", dbr), ("lru_wi", dwi), ("lru_bi", dbi),
                       ("lru_lambda", dlam)):
            small[n][l] = val
        dconv_w[l] = dcw[0:4]
        dxn = dx
    grad_x = dxn.reshape(1, S, D)

    parts = {n: jnp.stack(small[n]) for n in _SMALL if n != "mem_norm_g"}
    parts["mem_norm_g"] = dg_mem
    me1 = jnp.reshape(2 * jpos + ci, (1,)).astype(jnp.int32)
    rs = _rs_start(_pack_small(parts, jnp.stack(dconv_w)), lax.empty((NDEV - 1, SMALL_ROWS // NDEV, 128), F32))
    for l, a2a in sent[:-1]:
        finish_layer(l, a2a, rs[-1])
    pack, land = _rs_wait(rs, out["w_out"][1])
    ag = _ag_start(_small_sum(me1, pack, land))
    finish_layer(pending[0], pending[1], ag[-1])
    gsum = _ag_wait(ag, out["w_out"][1])
    dsm, msm, vsm = _adamw_flat(packs[0], packs[1], packs[2], gsum, 2560)
    g_small, at = _unpack_small(gsum, shapes)
    d_small, _ = _unpack_small(dsm, shapes)
    m_small, _ = _unpack_small(msm, shapes)
    v_small, _ = _unpack_small(vsm, shapes)
    for n in _SMALL:
        out[n] = (g_small[n], d_small[n], m_small[n], v_small[n])
    g_cw = lax.dynamic_slice_in_dim(gsum[at:at + L * 4 * D // 128].reshape(L * 4, D), jpos * (D // 4), D // 4, axis=1)
    d_cw, m_cw, v_cw = _adamw_flat(conv_w.reshape(L * 4, D // 4), m_conv_w.reshape(L * 4, D // 4),
                                   v_conv_w.reshape(L * 4, D // 4), g_cw, L * 4)
    out["conv_w"] = tuple(a.reshape(L, 4, D // 4) for a in (g_cw, d_cw, m_cw, v_cw))

    order = ("mem_norm_g", "pre_norm_g", "post_norm_g", "w_in", "gmlp_ln_g", "gmlp_ln_b", "gmlp_ws", "gmlp_bs", "conv_w", "conv_b",
             "lru_wr", "lru_br", "lru_wi", "lru_bi", "lru_lambda", "w_kv", "w_pa", "w_pb", "w_pc", "w_out")
    return (loss, grad_x, *[out[n][0] for n in order], *[out[n][1] for n in order], *[out[n][2] for n in order],
            *[out[n][3] for n in order])
```

```python
import functools

import jax
import jax.numpy as jnp
from jax import lax
from jax.experimental import pallas as pl
from jax.experimental.pallas import tpu as pltpu

F32 = jnp.float32
BF16 = jnp.bfloat16
SDS = jax.ShapeDtypeStruct
MESH = pl.DeviceIdType.MESH

D = 1024
NIN = 10 * D
MEM = 256
GB = 128
NG = 8
NH = 4
HD = D // NH
EPS = 1e-6
LRU_C = 8.0
ADAM_LR, ADAM_B1, ADAM_B2, ADAM_EPS, ADAM_WD, ADAM_STEP = 0.001, 0.9, 0.999, 1e-08, 0.01, 10
NDEV = 8
SMALL_ROWS = 12800

_CALL_KW = {}
HBM = pl.BlockSpec(memory_space=pltpu.HBM)
VMEM = pl.BlockSpec(memory_space=pltpu.VMEM)
SEM = pl.BlockSpec(memory_space=pltpu.SEMAPHORE)
ANY = pl.BlockSpec(memory_space=pl.ANY)
TOKEN = SDS((8, 128), F32)


def _pcall(body, *, name, in_specs, out_specs, out_shape, grid=None, scratch=(), vmem_mb=48, aliases=None, effect=False,
           prefetch=0):
    kw = dict(_CALL_KW)
    if aliases:
        kw["input_output_aliases"] = aliases
    params = dict(vmem_limit_bytes=vmem_mb << 20)
    if grid is not None:
        params["dimension_semantics"] = ("arbitrary",) * len(grid)
    if effect:
        params["has_side_effects"] = pltpu.SideEffectType.DATAFLOW_SIDE_EFFECTING
    if prefetch:
        kw["grid_spec"] = pltpu.PrefetchScalarGridSpec(num_scalar_prefetch=prefetch, grid=grid, in_specs=in_specs,
                                                       out_specs=out_specs, scratch_shapes=list(scratch))
    else:
        kw.update(in_specs=in_specs, out_specs=out_specs, scratch_shapes=list(scratch))
        if grid is not None:
            kw["grid"] = grid
    return pl.pallas_call(body, name=name, out_shape=out_shape, compiler_params=pltpu.CompilerParams(**params), **kw)


def _full(shape):
    nd = len(shape)
    return pl.BlockSpec(shape, lambda *_: (0,) * nd)


def _dot(a, b):
    return jnp.dot(a, b, preferred_element_type=F32)


def _dot_nt(a, b):
    return lax.dot_general(a, b, (((1,), (1,)), ((), ())), preferred_element_type=F32)


def _dot_tn(a, b):
    return lax.dot_general(a, b, (((0,), (0,)), ((), ())), preferred_element_type=F32)


def _rowsum(a):
    return jnp.sum(a, axis=0, keepdims=True)


def _sigmoid(x):
    return 0.5 * jnp.tanh(0.5 * x) + 0.5


def _silu_parts(g):
    s = _sigmoid(g)
    return g * s, s * (1.0 + g * (1.0 - s))


def _rms_scale(x):
    return lax.rsqrt(jnp.mean(x * x, axis=-1, keepdims=True) + EPS)


def _dz_col(k):
    return jnp.where(k < 3, k, jnp.where(k < 6, k + 4, k - 3))


def _coords():
    return lax.axis_index("x"), lax.axis_index("y"), lax.axis_index("c")


def _other_chips(x, y):
    return [(1 - x, y), (x, 1 - y), (1 - x, 1 - y)]


def _peer(x, y, c, mask):
    return (1 - x if mask & 4 else x, 1 - y if mask & 2 else y, 1 - c if mask & 1 else c)


def _remote(src, dst, ssem, rsem, k, to):
    return pltpu.make_async_remote_copy(src_ref=src, dst_ref=dst, send_sem=ssem.at[k], recv_sem=rsem.at[k], device_id=to,
                                        device_id_type=MESH)


def _w_half(a, ref, jj, cc):
    if a == 2:
        rp = ref.shape[1] // 4
        return ref.at[:, pl.ds(jj * rp + cc * (rp // 2), rp // 2), :]
    kin, nsh = ref.shape[0], ref.shape[1] // 4
    return ref.at[pl.ds(cc * (kin // 2), kin // 2), pl.ds(jj * nsh, nsh)]


def _cw_block(ref, jj):
    return ref.at[:, pl.ds(jj * (D // 4), D // 4)]


def _cast_place(l, pos, w_in, w_kv, w_pa, w_pb, w_pc, w_out, cw8):
    kin, nsh = w_in.shape[1], w_in.shape[2]
    nkv, rp = w_kv.shape[2], w_pa.shape[1]
    half = kin // 2

    def body(pos_r, win, wkv, pa, pb, pc, po, cw, Win, Wkv, Wp, Cw):
        Win[...] = win[...].astype(BF16)
        Wkv[...] = wkv[...].astype(BF16)

        @pl.when(pl.program_id(0) == 0)
        def _():
            for k, r in enumerate((pa, pb, pc, po)):
                Wp[k] = r[...].astype(BF16)
            Cw[...] = cw[...]

    proj = pl.BlockSpec((None, rp, D), lambda i, p: (l, 0, 0))
    return _pcall(
        body, name="cast_place", grid=(2,), prefetch=1,
        in_specs=[pl.BlockSpec((None, half, nsh), lambda i, p: (l, i, 0)), pl.BlockSpec((None, half, nkv), lambda i, p: (l, i, 0)),
                  proj, proj, proj, proj, pl.BlockSpec((None, 8, D // 4), lambda i, p: (l, 0, 0))],
        out_specs=[pl.BlockSpec((half, nsh), lambda i, p: (i, p[0])), pl.BlockSpec((half, nkv), lambda i, p: (i, p[0])),
                   pl.BlockSpec((4, rp, D), lambda i, p: (0, p[0], 0)), pl.BlockSpec((8, D // 4), lambda i, p: (0, p[0]))],
        out_shape=[SDS((kin, 4 * nsh), BF16), SDS((kin, 4 * nkv), BF16), SDS((4, 4 * rp, D), BF16), SDS((8, D), F32)],
    )(pos, w_in, w_kv, w_pa, w_pb, w_pc, w_out, cw8)


def _hbm_like(bufs):
    return [pltpu.HBM(b.shape, b.dtype) for b in bufs]


def _w_part(a, ref, jj, cc):
    return _cw_block(ref, jj) if a == 3 else _w_half(a, ref, jj, cc)


def _gather_start(name, kinds, bufs, after=None):
    n = len(kinds)
    extra = [] if after is None else [after]

    def body(*refs):
        w, ssem, rsem, token = refs[0:n], refs[n + len(extra)], refs[n + len(extra) + 1], refs[-1]
        x, y, c = _coords()
        j = 2 * x + y
        for k, chip in enumerate(_other_chips(x, y)):
            for i, a in enumerate(kinds):
                part = _w_part(a, w[i], j, c)
                _remote(part, part, ssem, rsem, i * 3 + k, (chip[0], chip[1], c)).start()
        token[...] = jnp.zeros((8, 128), F32)

    return _pcall(
        body, name=name, in_specs=[HBM] * n + [ANY] * len(extra), out_specs=[SEM, SEM] + [HBM] * n + [VMEM],
        out_shape=[pltpu.SemaphoreType.DMA((3 * n,)), pltpu.SemaphoreType.DMA((3 * n,))] + _hbm_like(bufs) + [TOKEN],
        aliases={i: 2 + i for i in range(n)}, effect=True,
    )(*[pltpu.with_memory_space_constraint(b, pltpu.HBM) for b in bufs], *extra)


def _gather_mid(name, kinds, started, after):
    n = len(kinds)
    fw = [i for i, a in enumerate(kinds) if a != 3]
    bufs = tuple(started[2:2 + n])

    def body(*refs):
        w, ssem, rsem, ssem2, rsem2, token = refs[0:n], refs[n], refs[n + 1], refs[n + 3], refs[n + 4], refs[-1]
        x, y, c = _coords()
        j = 2 * x + y
        me, sib = (x, y, c), (x, y, 1 - c)
        token[...] = jnp.zeros((8, 128), F32)
        chips = _other_chips(x, y)
        for k, chip in enumerate(chips):
            for i, a in enumerate(kinds):
                got = _w_part(a, w[i], 2 * chip[0] + chip[1], c)
                _remote(got, got, ssem, rsem, i * 3 + k, me).wait_recv()
        for k in range(3):
            for i, a in enumerate(kinds):
                part = _w_part(a, w[i], j, c)
                _remote(part, part, ssem, rsem, i * 3 + k, me).wait_send()
        for k, chip in enumerate(chips):
            for f, i in enumerate(fw):
                got = _w_half(kinds[i], w[i], 2 * chip[0] + chip[1], c)
                _remote(got, got, ssem2, rsem2, f * 3 + k, sib).start()

    return _pcall(
        body, name=name, in_specs=[HBM] * n + [SEM, SEM, ANY], out_specs=[SEM, SEM] + [HBM] * n + [VMEM],
        out_shape=[pltpu.SemaphoreType.DMA((3 * len(fw),)), pltpu.SemaphoreType.DMA((3 * len(fw),))] + _hbm_like(bufs) + [TOKEN],
        aliases={i: 2 + i for i in range(n)}, effect=True,
    )(*bufs, started[0], started[1], after)


def _gather_end(name, kinds, mid, after):
    n = len(kinds)
    fw = [i for i, a in enumerate(kinds) if a != 3]
    bufs = tuple(mid[2:2 + n])

    def body(*refs):
        w, ssem2, rsem2 = refs[0:n], refs[n], refs[n + 1]
        x, y, c = _coords()
        me = (x, y, c)
        for k, chip in enumerate(_other_chips(x, y)):
            for f, i in enumerate(fw):
                got = _w_half(kinds[i], w[i], 2 * chip[0] + chip[1], 1 - c)
                _remote(got, got, ssem2, rsem2, f * 3 + k, me).wait_recv()
                sent = _w_half(kinds[i], w[i], 2 * chip[0] + chip[1], c)
                _remote(sent, sent, ssem2, rsem2, f * 3 + k, me).wait_send()

    return _pcall(
        body, name=name, in_specs=[HBM] * n + [SEM, SEM, ANY], out_specs=[HBM] * n, out_shape=_hbm_like(bufs),
        aliases={i: i for i in range(n)}, effect=True,
    )(*bufs, mid[0], mid[1], after)


def _g_piece(a, ref, jd, dc):
    if a == 2:
        rp = ref.shape[1] // 4
        return ref.at[:, pl.ds(jd * rp + dc * (rp // 2), rp // 2), :]
    kin, nsh = ref.shape[0], ref.shape[1] // 4
    return ref.at[pl.ds(dc * (kin // 2), kin // 2), pl.ds(jd * nsh, nsh)]


def _a2a_start(name, kinds, grads, lands):
    n = len(kinds)

    def body(*refs):
        g, ld, ssem, rsem, token = refs[0:n], refs[n:2 * n], refs[2 * n], refs[2 * n + 1], refs[-1]
        x, y, c = _coords()
        for mask in range(1, NDEV):
            p = _peer(x, y, c, mask)
            for i, a in enumerate(kinds):
                _remote(_g_piece(a, g[i], 2 * p[0] + p[1], p[2]), ld[i].at[mask - 1], ssem, rsem, i * 7 + mask - 1, p).start()
        token[...] = jnp.zeros((8, 128), F32)

    bufs = tuple(grads) + tuple(lands)
    return _pcall(
        body, name=name, in_specs=[HBM] * (2 * n), out_specs=[SEM, SEM] + [HBM] * (2 * n) + [VMEM],
        out_shape=[pltpu.SemaphoreType.DMA((7 * n,)), pltpu.SemaphoreType.DMA((7 * n,))] + _hbm_like(bufs) + [TOKEN],
        aliases={i: 2 + i for i in range(2 * n)}, effect=True,
    )(*[pltpu.with_memory_space_constraint(b, pltpu.HBM) for b in bufs])


def _a2a_wait(name, kinds, started, after):
    n = len(kinds)
    ssem, rsem = started[0], started[1]
    bufs = tuple(started[2:2 + 2 * n])

    def body(*refs):
        g, ld, ssem, rsem = refs[0:n], refs[n:2 * n], refs[2 * n], refs[2 * n + 1]
        x, y, c = _coords()
        me = (x, y, c)
        for mask in range(1, NDEV):
            for i in range(n):
                got = ld[i].at[mask - 1]
                _remote(got, got, ssem, rsem, i * 7 + mask - 1, me).wait_recv()
        for mask in range(1, NDEV):
            p = _peer(x, y, c, mask)
            for i, a in enumerate(kinds):
                sent = _g_piece(a, g[i], 2 * p[0] + p[1], p[2])
                _remote(sent, sent, ssem, rsem, i * 7 + mask - 1, me).wait_send()

    return _pcall(
        body, name=name, in_specs=[HBM] * (2 * n) + [SEM, SEM, ANY], out_specs=[HBM] * (2 * n), out_shape=_hbm_like(bufs),
        aliases={i: i for i in range(2 * n)}, effect=True,
    )(*bufs, ssem, rsem, after)


def _sum_share(pos, lands, grads):
    rows, n = 128, 4
    widths = [ld.shape[2] for ld in lands]

    def body(pos_r, l0, w0, l1, w1, l2, w2, g0, g1, g2, b0, b1, b2, lsem, ssem, rsem):
        i = pl.program_id(0)
        x, y, c = _coords()
        sib = (x, y, 1 - c)
        ld, ow, gs, bufs = (l0, l1, l2), (w0, w1, w2), (g0, g1, g2), (b0, b1, b2)

        def dst(a, step):
            row = step * (2 * rows) + c * rows if a == 2 else c * (n * rows) + step * rows
            return gs[a].at[pl.ds(row, rows), :]

        def copies(a, step, sl):
            src = bufs[a].at[sl]
            lc = pltpu.make_async_copy(src, dst(a, step), lsem.at[a, sl])
            rc = pltpu.make_async_remote_copy(src_ref=src, dst_ref=dst(a, step), send_sem=ssem.at[a, sl], recv_sem=rsem.at[a],
                                              device_id=sib, device_id_type=MESH)
            return lc, rc

        def drain(a, step, sl):
            lc, rc = copies(a, step, sl)
            lc.wait()
            rc.wait_send()

        slot = i % 2

        @pl.when(i >= 2)
        def _():
            for a in range(3):
                drain(a, i - 2, slot)

        for a in range(3):
            acc = ow[a][...].astype(F32)
            for k in range(NDEV - 1):
                acc = acc + ld[a][k].astype(F32)
            bufs[a][slot] = acc
            lc, rc = copies(a, i, slot)
            lc.start()
            rc.start()

        @pl.when(i == n - 1)
        def _():
            for a in range(3):
                drain(a, n - 2, (n - 2) % 2)
                drain(a, n - 1, (n - 1) % 2)
                whole = gs[a].at[pl.ds(0, n * rows), :]
                pltpu.make_async_remote_copy(src_ref=whole, dst_ref=whole, send_sem=ssem.at[a, 0], recv_sem=rsem.at[a],
                                             device_id=(x, y, c), device_id_type=MESH).wait_recv()

    land = lambda w: pl.BlockSpec((NDEV - 1, rows, w), lambda i, p: (0, i, 0))
    in_specs = [land(widths[0]), pl.BlockSpec((rows, widths[0]), lambda i, p: (p[1] * n + i, p[0])),
                land(widths[1]), pl.BlockSpec((rows, widths[1]), lambda i, p: (p[1] * n + i, p[0])),
                land(widths[2]), pl.BlockSpec((None, rows, widths[2]), lambda i, p: (i, 2 * p[0] + p[1], 0))]
    args = [t for pair in zip(lands, grads) for t in pair]
    return _pcall(
        body, name="sum_share", grid=(n,), prefetch=1, in_specs=in_specs, out_specs=[HBM] * 3,
        out_shape=[SDS((2 * n * rows, w), F32) for w in widths],
        scratch=[pltpu.VMEM((2, rows, w), F32) for w in widths]
        + [pltpu.SemaphoreType.DMA((3, 2)), pltpu.SemaphoreType.DMA((3, 2)), pltpu.SemaphoreType.DMA((3,))],
    )(pos, *args)


def _dev_index(p):
    return 4 * p[0] + 2 * p[1] + p[2]


def _small_rows(ref, d):
    r8 = ref.shape[0] // NDEV
    return ref.at[pl.ds(d * r8, r8), :]


def _rs_start(pack, land):
    def body(p_ref, ld, ssem, rsem, o0, o1, token):
        x, y, c = _coords()
        for mask in range(1, NDEV):
            p = _peer(x, y, c, mask)
            _remote(_small_rows(p_ref, _dev_index(p)), ld.at[mask - 1], ssem, rsem, mask - 1, p).start()
        token[...] = jnp.zeros((8, 128), F32)

    bufs = (pack, land)
    return _pcall(
        body, name="rs_start", in_specs=[HBM] * 2, out_specs=[SEM, SEM, HBM, HBM, VMEM],
        out_shape=[pltpu.SemaphoreType.DMA((NDEV - 1,)), pltpu.SemaphoreType.DMA((NDEV - 1,))] + _hbm_like(bufs) + [TOKEN],
        aliases={0: 2, 1: 3}, effect=True,
    )(*[pltpu.with_memory_space_constraint(b, pltpu.HBM) for b in bufs])


def _rs_wait(started, after):
    ssem, rsem, pack, land, _ = started

    def body(p_ref, ld, ssem, rsem, after_r, o0, o1):
        x, y, c = _coords()
        for mask in range(1, NDEV):
            got = ld.at[mask - 1]
            _remote(got, got, ssem, rsem, mask - 1, (x, y, c)).wait_recv()
        for mask in range(1, NDEV):
            sent = _small_rows(p_ref, _dev_index(_peer(x, y, c, mask)))
            _remote(sent, sent, ssem, rsem, mask - 1, (x, y, c)).wait_send()

    return _pcall(body, name="rs_wait", in_specs=[HBM, HBM, SEM, SEM, ANY], out_specs=[HBM, HBM],
                  out_shape=_hbm_like((pack, land)), aliases={0: 0, 1: 1}, effect=True)(pack, land, ssem, rsem, after)


def _small_sum(me1, pack, land):
    R = pack.shape[0]
    r8 = R // NDEV

    def body(me_r, p_ref, ld, full):
        acc = p_ref[...]
        for k in range(NDEV - 1):
            acc = acc + ld[k]
        full[...] = acc

    own = pl.BlockSpec((r8, 128), lambda i, m: (m[0], 0))
    return _pcall(body, name="small_sum", grid=(1,), prefetch=1,
                  in_specs=[own, pl.BlockSpec((NDEV - 1, r8, 128), lambda i, m: (0, 0, 0))], out_specs=own,
                  out_shape=SDS((R, 128), F32), vmem_mb=32)(me1, pack, land)


def _ag_start(full):
    def body(f_ref, ssem, rsem, o0, token):
        x, y, c = _coords()
        mine = _small_rows(f_ref, _dev_index((x, y, c)))
        for mask in range(1, NDEV):
            _remote(mine, mine, ssem, rsem, mask - 1, _peer(x, y, c, mask)).start()
        token[...] = jnp.zeros((8, 128), F32)

    return _pcall(
        body, name="ag_start", in_specs=[HBM], out_specs=[SEM, SEM, HBM, VMEM],
        out_shape=[pltpu.SemaphoreType.DMA((NDEV - 1,)), pltpu.SemaphoreType.DMA((NDEV - 1,))] + _hbm_like((full,)) + [TOKEN],
        aliases={0: 2}, effect=True,
    )(pltpu.with_memory_space_constraint(full, pltpu.HBM))


def _ag_wait(started, after):
    ssem, rsem, full, _ = started

    def body(f_ref, ssem, rsem, after_r, o0):
        x, y, c = _coords()
        mine = _small_rows(f_ref, _dev_index((x, y, c)))
        for mask in range(1, NDEV):
            got = _small_rows(f_ref, _dev_index(_peer(x, y, c, mask)))
            _remote(got, got, ssem, rsem, mask - 1, (x, y, c)).wait_recv()
            _remote(mine, mine, ssem, rsem, mask - 1, (x, y, c)).wait_send()

    return _pcall(body, name="ag_wait", in_specs=[HBM, SEM, SEM, ANY], out_specs=[HBM], out_shape=_hbm_like((full,)),
                  aliases={0: 0}, effect=True)(full, ssem, rsem, after)[0]


def _mm_in(x, g, w):
    S = x.shape[0]
    tm, tn = min(1024, S), 1280

    def body(x_ref, g_ref, w_ref, z_ref, h_ref, hs):
        @pl.when(pl.program_id(1) == 0)
        def _():
            xv = x_ref[...]
            hb = (xv * _rms_scale(xv) * g_ref[...]).astype(BF16)
            hs[...] = hb
            h_ref[...] = hb

        z_ref[...] = _dot(hs[...], w_ref[...]).astype(BF16)

    return _pcall(
        body, name="mm_in", grid=(S // tm, NIN // tn),
        in_specs=[pl.BlockSpec((tm, D), lambda i, j: (i, 0)), _full((1, D)), pl.BlockSpec((D, tn), lambda i, j: (0, j))],
        out_specs=[pl.BlockSpec((tm, tn), lambda i, j: (i, j)), pl.BlockSpec((tm, D), lambda i, j: (i, 0))],
        out_shape=[SDS((S, NIN), BF16), SDS((S, D), BF16)], scratch=[pltpu.VMEM((tm, D), BF16)],
    )(x, g, w)


def _chunk_mask():
    ri = lax.broadcasted_iota(jnp.int32, (GB, GB), 0)
    ci = lax.broadcasted_iota(jnp.int32, (GB, GB), 1)
    return (ri >= 64) | (ci < 64)


def _layernorm_parts(v):
    mu = jnp.mean(v, axis=-1, keepdims=True)
    d = v - mu
    rs = lax.rsqrt(jnp.mean(d * d, axis=-1, keepdims=True) + EPS)
    return d * rs, rs


def _branch_a(z, lg, lb, ws, bsb):
    S = z.shape[0]
    T = min(512, S)

    def body(zu, zv, zg, lg_r, lb_r, ws_r, bs_r, ya):
        vhat, _ = _layernorm_parts(zv[...].astype(F32))
        vnb = (vhat * lg_r[...] + lb_r[...]).astype(BF16)
        sil, _ = _silu_parts(zg[...].astype(F32))
        t = zu[...].astype(F32) * sil
        mask = _chunk_mask()
        for g in range(NG):
            wg = jnp.where(mask, ws_r[g], 0.0).astype(BF16)
            cs = slice(g * GB, (g + 1) * GB)
            for n in range(T // GB):
                rs = slice(n * GB, (n + 1) * GB)
                sv = _dot(wg, vnb[rs, cs]) + bs_r[g]
                ya[rs, cs] = (t[rs, cs] * sv).astype(BF16)

    zs = lambda k: pl.BlockSpec((T, D), lambda i: (i, k))
    return _pcall(
        body, name="branch_a", grid=(S // T,),
        in_specs=[zs(0), zs(1), zs(2), _full((1, D)), _full((1, D)), _full((NG, GB, GB)), _full((NG, GB, GB))],
        out_specs=pl.BlockSpec((T, D), lambda i: (i, 0)), out_shape=SDS((S, D), BF16),
    )(z, z, z, lg, lb, ws, bsb)


def _softplus_neg(lam):
    e = jnp.exp(-jnp.abs(lam))
    l1p = jnp.where(e < 1e-2, e * (1.0 - e * (0.5 - e * (1.0 / 3.0))), jnp.log(1.0 + e))
    return jnp.maximum(-lam, 0.0) + l1p


CH = 16


def _ck(c, off=0):
    return pl.ds(c * CH + off, CH)


def _half_sum(v):
    return v[0:8, :] + v[8:16, :]


def _lru_gates(pr, pi, br, bi, sp8):
    r = jax.nn.sigmoid(pr + br)
    ig = _sigmoid(pi + bi)
    la = sp8 * r
    a = jnp.exp(la)
    a2 = a * a
    mult = jnp.sqrt(-jnp.tanh(la) * (a2 + 1.0))
    return r, ig, a, a2, mult


def _tile_rows():
    return lax.broadcasted_iota(jnp.int32, (8, D), 0)


def _scan_forward(a_s, u_s, h_s, hcar, T):
    row = _tile_rows()

    def tile(i, hp):
        o = pl.multiple_of(i * 8, 8)
        A = a_s[pl.ds(o, 8), :]
        U = u_s[pl.ds(o, 8), :]
        for s in (1, 2, 4):
            m = row >= s
            U = jnp.where(m, U + A * pltpu.roll(U, s, 0), U)
            A = jnp.where(m, A * pltpu.roll(A, s, 0), A)
        H = U + A * hp
        h_s[pl.ds(o, 8), :] = H
        return jnp.broadcast_to(H[7:8, :], (8, D))

    hcar[...] = lax.fori_loop(0, T // 8, tile, hcar[...])


def _scan_reverse(b_s, d_s, l_s, lcar, T):
    row = _tile_rows()
    n = T // 8

    def tile(i, lp):
        o = pl.multiple_of((n - 1 - i) * 8, 8)
        B = b_s[pl.ds(o, 8), :]
        U = d_s[pl.ds(o, 8), :]
        for s in (1, 2, 4):
            m = row < 8 - s
            U = jnp.where(m, U + B * pltpu.roll(U, 8 - s, 0), U)
            B = jnp.where(m, B * pltpu.roll(B, 8 - s, 0), B)
        Lm = U + B * lp
        l_s[pl.ds(o, 8), :] = Lm
        return jnp.broadcast_to(Lm[0:1, :], (8, D))

    lcar[...] = lax.fori_loop(0, n, tile, lcar[...])


def _branch_b(z, cw, cb, wr, br, wi, bi, lam):
    S = z.shape[0]
    T = min(256, S)

    def body(zxb, zgb, cw_r, cb_r, wr_r, br_r, wi_r, bi_r, lam_r, yb, hs_o, xc_o, r_o, ig_o, a_o, m_o, xpad, u_s, hcar):
        @pl.when(pl.program_id(0) == 0)
        def _():
            xpad[pl.ds(0, 8), :] = jnp.zeros((8, D), F32)
            hcar[...] = jnp.zeros((8, D), F32)

        cw = cw_r[...]
        xpad[pl.ds(8, T), :] = zxb[...].astype(F32)
        xk = [xpad[pl.ds(5 + k, T), :] for k in range(4)]
        xc = cb_r[...] + (((xk[0] * cw[0:1] + xk[1] * cw[1:2]) + xk[2] * cw[2:3]) + xk[3] * cw[3:4])
        xcb = xc.astype(BF16)
        pr, pi = [], []
        for h in range(NG):
            cs = slice(h * GB, (h + 1) * GB)
            pr.append(_dot(xcb[:, cs], wr_r[h].astype(BF16)))
            pi.append(_dot(xcb[:, cs], wi_r[h].astype(BF16)))
        r, ig, a, _, mult = _lru_gates(jnp.concatenate(pr, axis=1), jnp.concatenate(pi, axis=1), br_r[...], bi_r[...],
                                       -LRU_C * _softplus_neg(lam_r[...]))
        for o_ref, val in ((xc_o, xc), (r_o, r), (ig_o, ig), (a_o, a), (m_o, mult)):
            o_ref[...] = val
        u_s[...] = mult * (ig * xc)
        _scan_forward(a_o, u_s, hs_o, hcar, T)
        xpad[pl.ds(0, 8), :] = xpad[pl.ds(T, 8), :]
        sil, _ = _silu_parts(zgb[...].astype(F32))
        yb[...] = (hs_o[...] * sil).astype(BF16)

    zs = lambda k: pl.BlockSpec((T, D), lambda i: (i, k))
    row = pl.BlockSpec((T, D), lambda i: (i, 0))
    return _pcall(
        body, name="branch_b", grid=(S // T,),
        in_specs=[zs(3), zs(4), _full((8, D)), _full((1, D)), _full((NG, GB, GB)), _full((1, D)), _full((NG, GB, GB)),
                  _full((1, D)), _full((1, D))],
        out_specs=[row] * 7, out_shape=[SDS((S, D), BF16)] + [SDS((S, D), F32)] * 6,
        scratch=[pltpu.VMEM((T + 8, D), F32), pltpu.VMEM((T, D), F32), pltpu.VMEM((8, D), F32)],
    )(z, z, cw, cb, wr, br, wi, bi, lam)


def _kv(mem, g, wkv):
    def body(m_ref, g_ref, w_ref, kv_ref):
        m = m_ref[...]
        mn = (m * _rms_scale(m) * g_ref[...]).astype(BF16)
        kv_ref[...] = _dot(mn, w_ref[...]).astype(BF16)

    return _pcall(body, name="mem_kv", in_specs=[VMEM] * 3, out_specs=VMEM, out_shape=SDS((MEM, 2 * D), BF16),
                  vmem_mb=32)(mem, g, wkv)


def _softmax_rows(s):
    e = jnp.exp(s - jnp.max(s, axis=-1, keepdims=True))
    return e / jnp.sum(e, axis=-1, keepdims=True)


def _branch_c(z, kv):
    S = z.shape[0]
    T = min(512, S)

    def body(zq, zg, kv_r, yc):
        sil, _ = _silu_parts(zg[...].astype(F32))
        for h in range(NH):
            cs = slice(h * HD, (h + 1) * HD)
            p = _softmax_rows(_dot_nt(zq[:, cs], kv_r[:, cs]) * (HD ** -0.5))
            att = _dot(p.astype(BF16), kv_r[:, D + h * HD:D + (h + 1) * HD])
            yc[:, cs] = (att * sil[:, cs]).astype(BF16)

    zs = lambda k: pl.BlockSpec((T, D), lambda i: (i, k))
    return _pcall(body, name="branch_c", grid=(S // T,), in_specs=[zs(5), zs(6), _full((MEM, 2 * D))],
                  out_specs=pl.BlockSpec((T, D), lambda i: (i, 0)), out_shape=SDS((S, D), BF16))(z, z, kv)


def _merge_out(ya, yb, yc, z, wp, x, pg):
    S = x.shape[0]
    T = min(256, S)

    def body(ya_r, yb_r, yc_r, m0, m1, m2, wp_r, x_r, pg_r, pa_o, pb_o, pc_o, mg_o, o_o, xn_o):
        merged = None
        for y_r, ml, p_o, k in ((ya_r, m0, pa_o, 0), (yb_r, m1, pb_o, 1), (yc_r, m2, pc_o, 2)):
            p = _dot(y_r[...], wp_r[k])
            p_o[...] = p.astype(BF16)
            t = _sigmoid(ml[...].astype(F32)) * p
            merged = t if merged is None else merged + t
        mb = merged.astype(BF16)
        mg_o[...] = mb
        o = _dot(mb, wp_r[3])
        o_o[...] = o.astype(BF16)
        xn_o[...] = x_r[...] + o * _rms_scale(o) * pg_r[...]

    row = pl.BlockSpec((T, D), lambda i: (i, 0))
    zs = lambda k: pl.BlockSpec((T, D), lambda i: (i, k))
    return _pcall(
        body, name="merge_out", grid=(S // T,),
        in_specs=[row, row, row, zs(7), zs(8), zs(9), _full((4, D, D)), row, _full((1, D))],
        out_specs=[row] * 6, out_shape=[SDS((S, D), BF16)] * 5 + [SDS((S, D), F32)], vmem_mb=56,
    )(ya, yb, yc, z, z, z, wp, x, pg)


def _loss_head(y, t):
    S = y.shape[0]
    T = min(512, S)

    def body(y_r, t_r, loss_o, dy_o):
        @pl.when(pl.program_id(0) == 0)
        def _():
            loss_o[...] = jnp.zeros((1, 1), F32)

        e = y_r[...] - t_r[...]
        dy_o[...] = e * (1.0 / D)
        loss_o[...] += 0.5 * _rowsum(jnp.sum(e * e, axis=1, keepdims=True) * (1.0 / D))

    row = pl.BlockSpec((T, D), lambda i: (i, 0))
    return _pcall(body, name="loss_head", grid=(S // T,), in_specs=[row, row], out_specs=[_full((1, 1)), row],
                  out_shape=[SDS((1, 1), F32), SDS((S, D), F32)])(y, t)


def _accumulate(first, ref, val):
    @pl.when(first)
    def _():
        ref[...] = val

    @pl.when(jnp.logical_not(first))
    def _():
        ref[...] += val


def _out_bwd(dxn, o, pg, wp, z, pa, pb, pc):
    S = dxn.shape[0]
    T = min(256, S)

    def body(dy_r, o_r, pg_r, wp_r, m0, m1, m2, pa_r, pb_r, pc_r, do_o, dpa_o, dpb_o, dpc_o, dya_o, dyb_o, dyc_o, dz_o, dg_o):
        dy = dy_r[...]
        o = o_r[...].astype(F32)
        r2 = _rms_scale(o)
        w = dy * pg_r[...]
        do = r2 * w - o * (r2 * r2 * r2) * jnp.mean(w * o, axis=-1, keepdims=True)
        _accumulate(pl.program_id(0) == 0, dg_o, _rowsum(dy * o * r2))
        dob = do.astype(BF16)
        do_o[...] = dob
        dm = _dot_nt(dob, wp_r[3])
        for k, (ml, p_r, dp_o, dy_o) in enumerate(((m0, pa_r, dpa_o, dya_o), (m1, pb_r, dpb_o, dyb_o), (m2, pc_r, dpc_o, dyc_o))):
            gk = _sigmoid(ml[...].astype(F32))
            dz_o[k] = (dm * p_r[...].astype(F32) * gk * (1.0 - gk)).astype(BF16)
            dpk = (gk * dm).astype(BF16)
            dp_o[...] = dpk
            dy_o[...] = _dot_nt(dpk, wp_r[k]).astype(BF16)

    row = pl.BlockSpec((T, D), lambda i: (i, 0))
    zs = lambda k: pl.BlockSpec((T, D), lambda i: (i, k))
    return _pcall(
        body, name="out_bwd", grid=(S // T,),
        in_specs=[row, row, _full((1, D)), _full((4, D, D)), zs(7), zs(8), zs(9), row, row, row],
        out_specs=[row] * 7 + [pl.BlockSpec((3, T, D), lambda i: (1, i, 0)), _full((1, D))],
        out_shape=[SDS((S, D), BF16)] * 7 + [SDS((10, S, D), BF16), SDS((1, D), F32)], vmem_mb=56,
    )(dxn, o, pg, wp, z, z, z, pa, pb, pc)


def _branch_a_bwd(z, dya, lg, lb, ws, bsb, dz):
    S = z.shape[0]
    T = min(512, S)
    nblk = S // T

    def body(zu, zv, zg, dy_r, lg_r, lb_r, ws_r, bs_r, dz_in, dz_o, dws_o, dbs_o, dlg_o, dlb_o, dvn_s, bacc):
        i = pl.program_id(0)

        @pl.when(i == 0)
        def _():
            dws_o[...] = jnp.zeros((NG, GB, GB), F32)
            bacc[...] = jnp.zeros((NG, GB, GB), F32)

        vhat, rs = _layernorm_parts(zv[...].astype(F32))
        vnb = (vhat * lg_r[...] + lb_r[...]).astype(BF16)
        ga = zg[...].astype(F32)
        sil, dsil = _silu_parts(ga)
        u = zu[...].astype(F32)
        dy = dy_r[...].astype(F32)
        t = dy * sil
        dsv_all = t * u
        dga_pre = dy * u * dsil
        mask = _chunk_mask()
        for g in range(NG):
            wf = jnp.where(mask, ws_r[g], 0.0)
            wg = wf.astype(BF16)
            wgt = wf.T.astype(BF16)
            cs = slice(g * GB, (g + 1) * GB)
            dw = jnp.zeros((GB, GB), F32)
            db = jnp.zeros((GB, GB), F32)
            for n in range(T // GB):
                rsl = slice(n * GB, (n + 1) * GB)
                vb = vnb[rsl, cs]
                sv = _dot(wg, vb) + bs_r[g]
                dz_o[0, rsl, cs] = (t[rsl, cs] * sv).astype(BF16)
                dz_o[2, rsl, cs] = (dga_pre[rsl, cs] * sv).astype(BF16)
                dsv = dsv_all[rsl, cs]
                dsb = dsv.astype(BF16)
                dvn_s[rsl, cs] = _dot(wgt, dsb)
                dw = dw + _dot_nt(dsb, vb)
                db = db + dsv
            dws_o[g] += jnp.where(mask, dw, 0.0)
            bacc[g] += db
        dvn = dvn_s[...]
        dvh = dvn * lg_r[...]
        dv = rs * (dvh - jnp.mean(dvh, axis=-1, keepdims=True) - vhat * jnp.mean(dvh * vhat, axis=-1, keepdims=True))
        dz_o[1] = dv.astype(BF16)
        _accumulate(i == 0, dlg_o, _rowsum(dvn * vhat))
        _accumulate(i == 0, dlb_o, _rowsum(dvn))

        @pl.when(i == nblk - 1)
        def _():
            for g in range(NG):
                dbs_o[g:g + 1, :] = _rowsum(bacc[g].T)

    zs = lambda k: pl.BlockSpec((T, D), lambda i: (i, k))
    return _pcall(
        body, name="branch_a_bwd", grid=(nblk,),
        in_specs=[zs(0), zs(1), zs(2), pl.BlockSpec((T, D), lambda i: (i, 0)), _full((1, D)), _full((1, D)),
                  _full((NG, GB, GB)), _full((NG, GB, GB)), HBM],
        out_specs=[pl.BlockSpec((3, T, D), lambda i: (0, i, 0)), _full((NG, GB, GB)), _full((NG, GB)), _full((1, D)),
                   _full((1, D))],
        out_shape=[SDS((10, S, D), BF16), SDS((NG, GB, GB), F32), SDS((NG, GB), F32), SDS((1, D), F32), SDS((1, D), F32)],
        scratch=[pltpu.VMEM((T, D), F32), pltpu.VMEM((NG, GB, GB), F32)], aliases={8: 0},
    )(z, z, z, dya, lg, lb, ws, bsb, dz)


def _branch_b_bwd(z, hs, lru, dyb, cw, wr, wi, lam, dz):
    S = z.shape[0]
    T = min(256, S)
    nblk = S // T

    def body(zxb, zprev, zgb, hs_r, hprev_r, dy_r, xc_r, r_r, ig_r, a_r, m_r, cw_r, wr_r, wi_r, lam_r, dz_in,
             dz_o, dcw_o, dcb_o, dwr_o, dbr_o, dwi_o, dbi_o, dlam_o, xpad, hpad, apad, dpad,
             b_s, d_s, l_s, back_s, xcb_s, dprb_s, dpib_s, lcar):
        i = pl.program_id(0)
        blk = nblk - 1 - i
        first = i == 0

        @pl.when(first)
        def _():
            apad[pl.ds(T, 8), :] = jnp.zeros((8, D), F32)
            dpad[pl.ds(T, 8), :] = jnp.zeros((8, D), F32)
            lcar[...] = jnp.zeros((8, D), F32)
            dcw_o[...] = jnp.zeros((8, D), F32)
            dwr_o[...] = jnp.zeros((NG, GB, GB), F32)
            dwi_o[...] = jnp.zeros((NG, GB, GB), F32)

        keep = (blk > 0).astype(F32)
        nck = T // CH
        cw, lam = cw_r[...], lam_r[...]
        sp8 = -LRU_C * _softplus_neg(lam)
        xpad[pl.ds(0, 8), :] = zprev[...].astype(F32)[8:16, :] * keep
        hpad[pl.ds(0, 8), :] = hprev_r[...] * keep
        for c in range(nck):
            xpad[_ck(c, 8), :] = zxb[_ck(c), :].astype(F32)
            hpad[_ck(c, 8), :] = hs_r[_ck(c), :]
            xcb_s[_ck(c), :] = xc_r[_ck(c), :].astype(BF16)
            apad[_ck(c), :] = a_r[_ck(c), :]
            sil, dsil = _silu_parts(zgb[_ck(c), :].astype(F32))
            dy = dy_r[_ck(c), :].astype(F32)
            dz_o[1, _ck(c), :] = (dy * hs_r[_ck(c), :] * dsil).astype(BF16)
            d_s[_ck(c), :] = dy * sil
        for c in range(nck):
            b_s[_ck(c), :] = apad[_ck(c, 1), :]
        _scan_reverse(b_s, d_s, l_s, lcar, T)
        s_sp = s_br = s_bi = jnp.zeros((8, D), F32)
        for c in range(nck):
            lm, r, ig, mult, a, xc = l_s[_ck(c), :], r_r[_ck(c), :], ig_r[_ck(c), :], m_r[_ck(c), :], a_r[_ck(c), :], xc_r[_ck(c), :]
            t = lm * mult
            dpad[_ck(c), :] = t * ig
            dl = lm * hpad[_ck(c, 7), :] * a - (lm * ig * xc) * (a * a) / mult
            dpr = dl * sp8 * r * (1.0 - r)
            dpi = t * xc * ig * (1.0 - ig)
            s_sp = s_sp + _half_sum(dl * r)
            s_br = s_br + _half_sum(dpr)
            s_bi = s_bi + _half_sum(dpi)
            dprb_s[_ck(c), :] = dpr.astype(BF16)
            dpib_s[_ck(c), :] = dpi.astype(BF16)
        _accumulate(first, dlam_o, _rowsum(s_sp) * (LRU_C * jax.nn.sigmoid(-lam)))
        _accumulate(first, dbr_o, _rowsum(s_br))
        _accumulate(first, dbi_o, _rowsum(s_bi))
        for h in range(NG):
            cs = slice(h * GB, (h + 1) * GB)
            back_s[:, cs] = _dot_nt(dprb_s[:, cs], wr_r[h].astype(BF16)) + _dot_nt(dpib_s[:, cs], wi_r[h].astype(BF16))
            dwr_o[h] += _dot_tn(xcb_s[:, cs], dprb_s[:, cs])
            dwi_o[h] += _dot_tn(xcb_s[:, cs], dpib_s[:, cs])
        s_cb = jnp.zeros((8, D), F32)
        s_cw = [jnp.zeros((8, D), F32)] * 4
        for c in range(nck):
            dxc = dpad[_ck(c), :] + back_s[_ck(c), :]
            dpad[_ck(c), :] = dxc
            s_cb = s_cb + _half_sum(dxc)
            s_cw = [s_cw[k] + _half_sum(xpad[_ck(c, 5 + k), :] * dxc) for k in range(4)]
        _accumulate(first, dcb_o, _rowsum(s_cb))
        for k in range(4):
            dcw_o[k:k + 1, :] += _rowsum(s_cw[k])
        for c in range(nck):
            dxb = ((dpad[_ck(c, 3), :] * cw[0:1] + dpad[_ck(c, 2), :] * cw[1:2]) + dpad[_ck(c, 1), :] * cw[2:3]) + dpad[_ck(c), :] * cw[3:4]
            dz_o[0, _ck(c), :] = dxb.astype(BF16)
        apad[pl.ds(T, 8), :] = apad[pl.ds(0, 8), :]
        dpad[pl.ds(T, 8), :] = dpad[pl.ds(0, 8), :]

    rev = lambda k: pl.BlockSpec((T, D), lambda i: (nblk - 1 - i, k))
    prev16 = pl.BlockSpec((16, D), lambda i: (jnp.maximum((nblk - 1 - i) * (T // 16) - 1, 0), 3))
    prev8 = pl.BlockSpec((8, D), lambda i: (jnp.maximum((nblk - 1 - i) * (T // 8) - 1, 0), 0))
    vec, mat = _full((1, D)), _full((NG, GB, GB))
    return _pcall(
        body, name="branch_b_bwd", grid=(nblk,),
        in_specs=[rev(3), prev16, rev(4), rev(0), prev8] + [rev(0)] * 6 + [_full((8, D)), mat, mat, vec, HBM],
        out_specs=[pl.BlockSpec((2, T, D), lambda i: (3, nblk - 1 - i, 0)), _full((8, D)), vec, mat, vec, mat, vec, vec],
        out_shape=[SDS((10, S, D), BF16), SDS((8, D), F32), SDS((1, D), F32), SDS((NG, GB, GB), F32), SDS((1, D), F32),
                   SDS((NG, GB, GB), F32), SDS((1, D), F32), SDS((1, D), F32)],
        scratch=[pltpu.VMEM((T + 8, D), F32)] * 4 + [pltpu.VMEM((T, D), F32)] * 4 + [pltpu.VMEM((T, D), BF16)] * 3
        + [pltpu.VMEM((8, D), F32)],
        aliases={15: 0}, vmem_mb=56,
    )(z, z, z, hs, hs, dyb, *lru, cw, wr, wi, lam, dz)


def _branch_c_bwd(z, kv, dyc, dz):
    S = z.shape[0]
    T = min(512, S)

    def body(zq, zg, kv_r, dy_r, dz_in, dz_o, dkv_o):
        @pl.when(pl.program_id(0) == 0)
        def _():
            dkv_o[...] = jnp.zeros((MEM, 2 * D), F32)

        gc = zg[...].astype(F32)
        sil, dsil = _silu_parts(gc)
        dy = dy_r[...].astype(F32)
        datt = dy * sil
        dgc_pre = dy * dsil
        scale = HD ** -0.5
        for h in range(NH):
            cs = slice(h * HD, (h + 1) * HD)
            vs = slice(D + h * HD, D + (h + 1) * HD)
            qh = zq[:, cs]
            p = _softmax_rows(_dot_nt(qh, kv_r[:, cs]) * scale)
            pb = p.astype(BF16)
            att = _dot(pb, kv_r[:, vs])
            dz_o[1, :, cs] = (dgc_pre[:, cs] * att).astype(BF16)
            dab = datt[:, cs].astype(BF16)
            dp = _dot_nt(dab, kv_r[:, vs])
            ds = (p * (dp - jnp.sum(p * dp, axis=-1, keepdims=True)) * scale).astype(BF16)
            dz_o[0, :, cs] = _dot(ds, kv_r[:, cs]).astype(BF16)
            dkv_o[:, cs] += _dot_tn(ds, qh)
            dkv_o[:, vs] += _dot_tn(pb, dab)

    zs = lambda k: pl.BlockSpec((T, D), lambda i: (i, k))
    return _pcall(
        body, name="branch_c_bwd", grid=(S // T,),
        in_specs=[zs(5), zs(6), _full((MEM, 2 * D)), pl.BlockSpec((T, D), lambda i: (i, 0)), HBM],
        out_specs=[pl.BlockSpec((2, T, D), lambda i: (4, i, 0)), _full((MEM, 2 * D))],
        out_shape=[SDS((10, S, D), BF16), SDS((MEM, 2 * D), F32)], aliases={4: 0},
    )(z, z, kv, dyc, dz)


def _mm_dh(dz, w, x, dxn, g):
    S = x.shape[0]
    tm = min(1024, S)

    def body(dz_r, w_r, x_r, dxn_r, g_r, dx_o, dg_o, acc):
        i, k = pl.program_id(0), pl.program_id(1)
        _accumulate(k == 0, acc, _dot_nt(dz_r[0], w_r[...]))

        @pl.when(k == 9)
        def _():
            dh = acc[...]
            xv = x_r[...]
            r1 = _rms_scale(xv)
            wv = dh * g_r[...]
            dx_o[...] = dxn_r[...] + r1 * wv - xv * (r1 * r1 * r1) * jnp.mean(wv * xv, axis=-1, keepdims=True)
            _accumulate(i == 0, dg_o, _rowsum(dh * xv * r1))

    row = pl.BlockSpec((tm, D), lambda i, k: (i, 0))
    return _pcall(
        body, name="mm_dh", grid=(S // tm, 10),
        in_specs=[pl.BlockSpec((1, tm, D), lambda i, k: (k, i, 0)), pl.BlockSpec((D, D), lambda i, k: (0, _dz_col(k))),
                  row, row, _full((1, D))],
        out_specs=[row, _full((1, D))], out_shape=[SDS((S, D), F32), SDS((1, D), F32)],
        scratch=[pltpu.VMEM((tm, D), F32)],
    )(dz, w, x, dxn, g)


def _mm_dwin(h, dz):
    S = h.shape[0]
    tk = min(1024, S)
    nk = S // tk

    def body(h_r, dz_r, o_r, acc):
        k = pl.program_id(1)
        _accumulate(k == 0, acc, _dot_tn(h_r[...], dz_r[0]))

        @pl.when(k == nk - 1)
        def _():
            o_r[...] = acc[...].astype(BF16)

    return _pcall(
        body, name="mm_dwin", grid=(10, nk),
        in_specs=[pl.BlockSpec((tk, D), lambda n, k: (k, 0)), pl.BlockSpec((1, tk, D), lambda n, k: (n, k, 0))],
        out_specs=pl.BlockSpec((D, D), lambda n, k: (0, _dz_col(n))), out_shape=SDS((D, NIN), BF16),
        scratch=[pltpu.VMEM((D, D), F32)],
    )(h, dz)


def _mm_tn4(a4, b4):
    S = a4[0].shape[0]
    tk = min(1024, S)
    nk = S // tk

    def body(*refs):
        a_r, b_r, o_r, acc = refs[0:4], refs[4:8], refs[8], refs[9]
        w, k = pl.program_id(0), pl.program_id(1)
        for a in range(4):
            @pl.when(w == a)
            def _(a=a):
                _accumulate(k == 0, acc, _dot_tn(a_r[a][...], b_r[a][...]))

        @pl.when(k == nk - 1)
        def _():
            o_r[...] = acc[...].astype(BF16)

    def blk(a):
        return pl.BlockSpec((tk, D), lambda w, k: (jnp.where(w == a, k, jnp.where(w < a, 0, nk - 1)), 0))

    return _pcall(body, name="mm_tn4", grid=(4, nk), in_specs=[blk(a) for a in range(4)] * 2,
                  out_specs=pl.BlockSpec((None, D, D), lambda w, k: (w, 0, 0)), out_shape=SDS((4, D, D), BF16),
                  scratch=[pltpu.VMEM((D, D), F32)])(*a4, *b4)


def _mem_bwd(mem, g, wkv, dkv, dg_acc):
    def body(m_ref, g_ref, w_ref, dkv_ref, acc_ref, dw_ref, dg_ref):
        m = m_ref[...]
        mr = m * _rms_scale(m)
        mn = (mr * g_ref[...]).astype(BF16)
        dkb = dkv_ref[...].astype(BF16)
        dw_ref[...] = _dot_tn(mn, dkb).astype(BF16)
        dg_ref[...] = acc_ref[...] + _rowsum(_dot_nt(dkb, w_ref[...]) * mr)

    return _pcall(body, name="mem_bwd", in_specs=[VMEM] * 5, out_specs=[VMEM] * 2,
                  out_shape=[SDS((D, 2 * D), BF16), SDS((1, D), F32)], vmem_mb=48)(mem, g, wkv, dkv, dg_acc)


def _adamw_math(w, g, m, v):
    m2 = ADAM_B1 * m + (1.0 - ADAM_B1) * g
    v2 = ADAM_B2 * v + (1.0 - ADAM_B2) * (g * g)
    mh = m2 / (1.0 - ADAM_B1 ** ADAM_STEP)
    vh = v2 / (1.0 - ADAM_B2 ** ADAM_STEP)
    return -ADAM_LR * (mh / (jnp.sqrt(vh) + ADAM_EPS) + ADAM_WD * w), m2, v2


def _adamw_layer(l, w, m, v, g, prev, which=None, rows=256):
    L, R, C = w.shape

    def body(w_r, m_r, v_r, g_r, *rest):
        g_o, d_o, m_o, v_o = rest[-4:]
        g = g_r[...]
        d, m2, v2 = _adamw_math(w_r[...], g, m_r[...], v_r[...])
        g_o[...] = g
        d_o[...] = d
        m_o[...] = m2
        v_o[...] = v2

    st = pl.BlockSpec((None, rows, C), lambda i: (l, i, 0))
    gs = pl.BlockSpec((rows, C), lambda i: (i, 0)) if which is None else pl.BlockSpec((None, rows, C), lambda i: (which, i, 0))
    carried = list(prev) if prev is not None else []
    return _pcall(body, name="adamw_layer", grid=(R // rows,), in_specs=[st] * 3 + [gs] + [HBM] * len(carried),
                  out_specs=[st] * 4, out_shape=[SDS(w.shape, F32)] * 4, vmem_mb=56,
                  aliases={4 + k: k for k in range(len(carried))} or None)(w, m, v, g, *carried)


def _adamw_proj(l, trios, g_p, prevs, rows=128):
    L, R, C = trios[0][0].shape

    def body(*refs):
        ins, g_r, outs = refs[0:12], refs[12], refs[-16:]
        for k in range(4):
            g = g_r[k]
            d, m2, v2 = _adamw_math(ins[3 * k][...], g, ins[3 * k + 1][...], ins[3 * k + 2][...])
            for o, val in zip(outs[4 * k:4 * k + 4], (g, d, m2, v2)):
                o[...] = val

    st = pl.BlockSpec((None, rows, C), lambda i: (l, i, 0))
    carried = [a for p in prevs for a in p] if prevs[0] is not None else []
    res = _pcall(body, name="adamw_proj", grid=(R // rows,),
                 in_specs=[st] * 12 + [pl.BlockSpec((4, rows, C), lambda i: (0, i, 0))] + [HBM] * len(carried),
                 out_specs=[st] * 16, out_shape=[SDS(trios[0][0].shape, F32)] * 16, vmem_mb=56,
                 aliases={13 + k: k for k in range(len(carried))} or None)(*[a for t in trios for a in t], g_p, *carried)
    return [res[4 * k:4 * k + 4] for k in range(4)]


def _adamw_flat(w, m, v, g, rows):
    R, C = w.shape

    def body(w_r, m_r, v_r, g_r, d_o, m_o, v_o):
        d, m2, v2 = _adamw_math(w_r[...], g_r[...], m_r[...], v_r[...])
        d_o[...] = d
        m_o[...] = m2
        v_o[...] = v2

    blk = pl.BlockSpec((rows, C), lambda i: (i, 0))
    return _pcall(body, name="adamw_flat", grid=(R // rows,), in_specs=[blk] * 4, out_specs=[blk] * 3,
                  out_shape=[SDS((R, C), F32)] * 3)(w, m, v, g)


_SMALL = ("mem_norm_g", "pre_norm_g", "post_norm_g", "gmlp_ln_g", "gmlp_ln_b", "gmlp_ws", "gmlp_bs", "conv_b", "lru_wr",
          "lru_br", "lru_wi", "lru_bi", "lru_lambda")


def _pack_small(parts, conv_w_part):
    rows = [parts[n].reshape(-1, 128) for n in _SMALL] + [conv_w_part.reshape(-1, 128)]
    used = sum(r.shape[0] for r in rows)
    rows.append(jnp.zeros((SMALL_ROWS - used, 128), F32))
    return jnp.concatenate(rows, axis=0)


def _unpack_small(pack, shapes):
    out, at = {}, 0
    for n in _SMALL:
        size = 1
        for s in shapes[n]:
            size *= s
        out[n] = pack[at:at + size // 128].reshape(shapes[n])
        at += size // 128
    return out, at


def kernel(x, mem, mem_norm_g, pre_norm_g, post_norm_g, w_in, gmlp_ln_g, gmlp_ln_b, gmlp_ws, gmlp_bs, conv_w, conv_b, lru_wr, lru_br, lru_wi, lru_bi, lru_lambda, w_kv, w_pa, w_pb, w_pc, w_out, loss_target, m_mem_norm_g, m_pre_norm_g, m_post_norm_g, m_w_in, m_gmlp_ln_g, m_gmlp_ln_b, m_gmlp_ws, m_gmlp_bs, m_conv_w, m_conv_b, m_lru_wr, m_lru_br, m_lru_wi, m_lru_bi, m_lru_lambda, m_w_kv, m_w_pa, m_w_pb, m_w_pc, m_w_out, v_mem_norm_g, v_pre_norm_g, v_post_norm_g, v_w_in, v_gmlp_ln_g, v_gmlp_ln_b, v_gmlp_ws, v_gmlp_bs, v_conv_w, v_conv_b, v_lru_wr, v_lru_br, v_lru_wi, v_lru_bi, v_lru_lambda, v_w_kv, v_w_pa, v_w_pb, v_w_pc, v_w_out):
    L = w_in.shape[0]
    S = x.shape[1]
    xs = [x[0]]
    mem2 = mem[0]
    mg = mem_norm_g.reshape(1, D)
    vec = lambda a, l: a[l].reshape(1, D)
    ci = lax.axis_index("c")
    jpos = 2 * lax.axis_index("x") + lax.axis_index("y")
    pos = jnp.reshape(jpos, (1,)).astype(jnp.int32)
    pos2 = jnp.stack([jpos, ci]).astype(jnp.int32)

    cw8 = jnp.pad(conv_w, ((0, 0), (0, 4), (0, 0)))
    given = dict(mem_norm_g=(mem_norm_g, m_mem_norm_g, v_mem_norm_g), pre_norm_g=(pre_norm_g, m_pre_norm_g, v_pre_norm_g),
                 post_norm_g=(post_norm_g, m_post_norm_g, v_post_norm_g), gmlp_ln_g=(gmlp_ln_g, m_gmlp_ln_g, v_gmlp_ln_g),
                 gmlp_ln_b=(gmlp_ln_b, m_gmlp_ln_b, v_gmlp_ln_b), gmlp_ws=(gmlp_ws, m_gmlp_ws, v_gmlp_ws),
                 gmlp_bs=(gmlp_bs, m_gmlp_bs, v_gmlp_bs), conv_b=(conv_b, m_conv_b, v_conv_b), lru_wr=(lru_wr, m_lru_wr, v_lru_wr),
                 lru_br=(lru_br, m_lru_br, v_lru_br), lru_wi=(lru_wi, m_lru_wi, v_lru_wi), lru_bi=(lru_bi, m_lru_bi, v_lru_bi),
                 lru_lambda=(lru_lambda, m_lru_lambda, v_lru_lambda))
    shapes = {n: given[n][0].shape for n in _SMALL}
    zero_cw = jnp.zeros((L, 4, D), F32)
    packs = [_pack_small({n: given[n][k] for n in _SMALL}, zero_cw) for k in range(3)]

    placed = [_cast_place(0, pos, w_in, w_kv, w_pa, w_pb, w_pc, w_out, cw8)]
    W = [None] * L
    ALL = (0, 1, 2, 3)
    first = _gather_start("gather_start_0a", (0,), placed[0][0:1])
    rest = _gather_start("gather_start_0b", (1, 2, 3), placed[0][1:4], first[-1])
    cw8 = cw8 + rest[-1][0, 0]
    placed += [_cast_place(l, pos, w_in, w_kv, w_pa, w_pb, w_pc, w_out, cw8) for l in range(1, L)]
    busy = placed[L - 1][3][0:1, 0:128] + packs[0][0:1, :] + packs[1][0:1, :] + packs[2][0:1, :]
    mid = _gather_mid("gather_mid_0a", (0,), first, busy)
    started = _gather_start("gather_start_1", ALL, placed[1], mid[-1])
    win0 = _gather_end("gather_end_0a", (0,), mid, started[-1])[0]
    zh0 = _mm_in(xs[0], vec(pre_norm_g, 0), win0)
    mid = _gather_mid("gather_mid_0b", (1, 2, 3), rest, zh0[1])
    W[0] = [win0] + list(_gather_end("gather_end_0b", (1, 2, 3), mid, mid[-1]))

    saved = []
    for l in range(L):
        Win, Wkv, Wp, Cw = W[l]
        z, h = zh0 if l == 0 else _mm_in(xs[l], vec(pre_norm_g, l), Win)
        bsb = jnp.broadcast_to(gmlp_bs[l][:, :, None], (NG, GB, GB))
        ya = _branch_a(z, vec(gmlp_ln_g, l), vec(gmlp_ln_b, l), gmlp_ws[l], bsb)
        yb, hs, *lru = _branch_b(z, Cw, vec(conv_b, l), lru_wr[l], vec(lru_br, l), lru_wi[l], vec(lru_bi, l), vec(lru_lambda, l))
        kv = _kv(mem2, mg, Wkv)
        yc = _branch_c(z, kv)
        pg = vec(post_norm_g, l)
        if l + 1 < L:
            mid = _gather_mid(f"gather_mid_{l + 1}", ALL, started, yc)
            if l + 2 < L:
                started = _gather_start(f"gather_start_{l + 2}", ALL, placed[l + 2], mid[-1])
                pg = pg + started[-1][0, 0]
        pa, pb, pc, mgd, o, xn = _merge_out(ya, yb, yc, z, Wp, xs[l], pg)
        if l + 1 < L:
            W[l + 1] = _gather_end(f"gather_end_{l + 1}", ALL, mid, xn)
        xs.append(xn)
        saved.append((z, h, ya, yb, yc, hs, kv, pa, pb, pc, mgd, o, bsb, lru))

    loss11, dxn = _loss_head(xs[L], loss_target[0])
    loss = lax.psum(loss11[0, 0], ("x", "y", "c"))

    big = dict(w_in=(w_in, m_w_in, v_w_in), w_kv=(w_kv, m_w_kv, v_w_kv), w_pa=(w_pa, m_w_pa, v_w_pa),
               w_pb=(w_pb, m_w_pb, v_w_pb), w_pc=(w_pc, m_w_pc, v_w_pc), w_out=(w_out, m_w_out, v_w_out))
    out = {n: None for n in big}
    kin, nsh, nkv, rp = w_in.shape[1], w_in.shape[2], w_kv.shape[2], w_pa.shape[1]

    def finish_layer(l, a2a, after):
        g_p4, lp = _a2a_wait(f"a2a_p_wait_{l}", (2,), a2a[0], after)
        g_in, g_kv, lin, lkv = _a2a_wait(f"a2a_w_wait_{l}", (0, 1), a2a[1], after)
        g_in, g_kv, g_p = _sum_share(pos2, (lin, lkv, lp.reshape(NDEV - 1, 2 * rp, D)), (g_in, g_kv, g_p4))
        g_p = g_p.reshape(4, rp, D)
        out["w_in"] = _adamw_layer(l, *big["w_in"], g_in, out["w_in"])
        out["w_kv"] = _adamw_layer(l, *big["w_kv"], g_kv, out["w_kv"])
        proj = ("w_pa", "w_pb", "w_pc", "w_out")
        for n, res in zip(proj, _adamw_proj(l, [big[n] for n in proj], g_p, [out[n] for n in proj])):
            out[n] = res

    small = {n: [None] * L for n in _SMALL}
    dconv_w = [None] * L
    dg_mem = jnp.zeros((1, D), F32)
    pending = None
    sent = []
    for l in reversed(range(L)):
        Win, Wkv, Wp, Cw = W[l]
        z, h, ya, yb, yc, hs, kv, pa, pb, pc, mgd, o, bsb, lru = saved[l]
        pg = vec(post_norm_g, l) if pending is None else vec(post_norm_g, l) + pending[1][1][-1][0, 0]
        do, dpa, dpb, dpc, dya, dyb, dyc, dz, dgpost = _out_bwd(dxn, o, pg, Wp, z, pa, pb, pc)
        a2a_p = _a2a_start(f"a2a_p_start_{l}", (2,), (_mm_tn4((ya, yb, yc, mgd), (dpa, dpb, dpc, do)),),
                           (lax.empty((NDEV - 1, 4, rp // 2, D), BF16),))
        dz, dws, dbs, dlg, dlb = _branch_a_bwd(z, dya, vec(gmlp_ln_g, l) + a2a_p[-1][0, 0], vec(gmlp_ln_b, l), gmlp_ws[l], bsb, dz)
        dz, dcw, dcb, dwr, dbr, dwi, dbi, dlam = _branch_b_bwd(z, hs, lru, dyb, Cw, lru_wr[l], lru_wi[l], vec(lru_lambda, l), dz)
        dz, dkv = _branch_c_bwd(z, kv, dyc, dz)
        g_in = _mm_dwin(h, dz)
        g_kv, dg_mem = _mem_bwd(mem2, mg, Wkv, dkv, dg_mem)
        a2a_w = _a2a_start(f"a2a_w_start_{l}", (0, 1), (g_in, g_kv),
                           (lax.empty((NDEV - 1, kin // 2, nsh), BF16), lax.empty((NDEV - 1, kin // 2, nkv), BF16)))
        dx, dgpre = _mm_dh(dz, Win, xs[l], dxn, vec(pre_norm_g, l) + a2a_w[-1][0, 0])
        pending = (l, (a2a_p, a2a_w))
        sent.append(pending)
        for n, val in (("pre_norm_g", dgpre), ("post_norm_g", dgpost), ("gmlp_ln_g", dlg), ("gmlp_ln_b", dlb), ("gmlp_ws", dws),
                       ("gmlp_bs", dbs), ("conv_b", dcb), ("lru_wr", dwr), ("lru_br", dbr), ("lru_wi", dwi), ("lru_bi", dbi),
                       ("lru_lambda", dlam)):
            small[n][l] = val
        dconv_w[l] = dcw[0:4]
        dxn = dx
    grad_x = dxn.reshape(1, S, D)

    parts = {n: jnp.stack(small[n]) for n in _SMALL if n != "mem_norm_g"}
    parts["mem_norm_g"] = dg_mem
    me1 = jnp.reshape(2 * jpos + ci, (1,)).astype(jnp.int32)
    rs = _rs_start(_pack_small(parts, jnp.stack(dconv_w)), lax.empty((NDEV - 1, SMALL_ROWS // NDEV, 128), F32))
    for l, a2a in sent[:-1]:
        finish_layer(l, a2a, rs[-1])
    pack, land = _rs_wait(rs, out["w_out"][1])
    ag = _ag_start(_small_sum(me1, pack, land))
    finish_layer(pending[0], pending[1], ag[-1])
    gsum = _ag_wait(ag, out["w_out"][1])
    dsm, msm, vsm = _adamw_flat(packs[0], packs[1], packs[2], gsum, 2560)
    g_small, at = _unpack_small(gsum, shapes)
    d_small, _ = _unpack_small(dsm, shapes)
    m_small, _ = _unpack_small(msm, shapes)
    v_small, _ = _unpack_small(vsm, shapes)
    for n in _SMALL:
        out[n] = (g_small[n], d_small[n], m_small[n], v_small[n])
    g_cw = lax.dynamic_slice_in_dim(gsum[at:at + L * 4 * D // 128].reshape(L * 4, D), jpos * (D // 4), D // 4, axis=1)
    d_cw, m_cw, v_cw = _adamw_flat(conv_w.reshape(L * 4, D // 4), m_conv_w.reshape(L * 4, D // 4),
                                   v_conv_w.reshape(L * 4, D // 4), g_cw, L * 4)
    out["conv_w"] = tuple(a.reshape(L, 4, D // 4) for a in (g_cw, d_cw, m_cw, v_cw))

    order = ("mem_norm_g", "pre_norm_g", "post_norm_g", "w_in", "gmlp_ln_g", "gmlp_ln_b", "gmlp_ws", "gmlp_bs", "conv_w", "conv_b",
             "lru_wr", "lru_br", "lru_wi", "lru_bi", "lru_lambda", "w_kv", "w_pa", "w_pb", "w_pc", "w_out")
    return (loss, grad_x, *[out[n][0] for n in order], *[out[n][1] for n in order], *[out[n][2] for n in order],
            *[out[n][3] for n in order])
```

```python
import functools

import jax
import jax.numpy as jnp
from jax import lax
from jax.experimental import pallas as pl
from jax.experimental.pallas import tpu as pltpu

F32 = jnp.float32
BF16 = jnp.bfloat16
SDS = jax.ShapeDtypeStruct
MESH = pl.DeviceIdType.MESH

D = 1024
NIN = 10 * D
MEM = 256
GB = 128
NG = 8
NH = 4
HD = D // NH
EPS = 1e-6
LRU_C = 8.0
ADAM_LR, ADAM_B1, ADAM_B2, ADAM_EPS, ADAM_WD, ADAM_STEP = 0.001, 0.9, 0.999, 1e-08, 0.01, 10
NDEV = 8
SMALL_ROWS = 12800

_CALL_KW = {}
HBM = pl.BlockSpec(memory_space=pltpu.HBM)
VMEM = pl.BlockSpec(memory_space=pltpu.VMEM)
SEM = pl.BlockSpec(memory_space=pltpu.SEMAPHORE)
ANY = pl.BlockSpec(memory_space=pl.ANY)
TOKEN = SDS((8, 128), F32)


def _pcall(body, *, name, in_specs, out_specs, out_shape, grid=None, scratch=(), vmem_mb=48, aliases=None, effect=False,
           prefetch=0):
    kw = dict(_CALL_KW)
    if aliases:
        kw["input_output_aliases"] = aliases
    params = dict(vmem_limit_bytes=vmem_mb << 20)
    if grid is not None:
        params["dimension_semantics"] = ("arbitrary",) * len(grid)
    if effect:
        params["has_side_effects"] = pltpu.SideEffectType.DATAFLOW_SIDE_EFFECTING
    if prefetch:
        kw["grid_spec"] = pltpu.PrefetchScalarGridSpec(num_scalar_prefetch=prefetch, grid=grid, in_specs=in_specs,
                                                       out_specs=out_specs, scratch_shapes=list(scratch))
    else:
        kw.update(in_specs=in_specs, out_specs=out_specs, scratch_shapes=list(scratch))
        if grid is not None:
            kw["grid"] = grid
    return pl.pallas_call(body, name=name, out_shape=out_shape, compiler_params=pltpu.CompilerParams(**params), **kw)


def _full(shape):
    nd = len(shape)
    return pl.BlockSpec(shape, lambda *_: (0,) * nd)


def _dot(a, b):
    return jnp.dot(a, b, preferred_element_type=F32)


def _dot_nt(a, b):
    return lax.dot_general(a, b, (((1,), (1,)), ((), ())), preferred_element_type=F32)


def _dot_tn(a, b):
    return lax.dot_general(a, b, (((0,), (0,)), ((), ())), preferred_element_type=F32)


def _rowsum(a):
    return jnp.sum(a, axis=0, keepdims=True)


def _sigmoid(x):
    return 0.5 * jnp.tanh(0.5 * x) + 0.5


def _silu_parts(g):
    s = _sigmoid(g)
    return g * s, s * (1.0 + g * (1.0 - s))


def _rms_scale(x):
    return lax.rsqrt(jnp.mean(x * x, axis=-1, keepdims=True) + EPS)


def _dz_col(k):
    return jnp.where(k < 3, k, jnp.where(k < 6, k + 4, k - 3))


def _coords():
    return lax.axis_index("x"), lax.axis_index("y"), lax.axis_index("c")


def _other_chips(x, y):
    return [(1 - x, y), (x, 1 - y), (1 - x, 1 - y)]


def _peer(x, y, c, mask):
    return (1 - x if mask & 4 else x, 1 - y if mask & 2 else y, 1 - c if mask & 1 else c)


def _remote(src, dst, ssem, rsem, k, to):
    return pltpu.make_async_remote_copy(src_ref=src, dst_ref=dst, send_sem=ssem.at[k], recv_sem=rsem.at[k], device_id=to,
                                        device_id_type=MESH)


def _w_half(a, ref, jj, cc):
    if a == 2:
        rp = ref.shape[1] // 4
        return ref.at[:, pl.ds(jj * rp + cc * (rp // 2), rp // 2), :]
    kin, nsh = ref.shape[0], ref.shape[1] // 4
    return ref.at[pl.ds(cc * (kin // 2), kin // 2), pl.ds(jj * nsh, nsh)]


def _cw_block(ref, jj):
    return ref.at[:, pl.ds(jj * (D // 4), D // 4)]


def _cast_place(l, pos, w_in, w_kv, w_pa, w_pb, w_pc, w_out, cw8):
    kin, nsh = w_in.shape[1], w_in.shape[2]
    nkv, rp = w_kv.shape[2], w_pa.shape[1]
    half = kin // 2

    def body(pos_r, win, wkv, pa, pb, pc, po, cw, Win, Wkv, Wp, Cw):
        Win[...] = win[...].astype(BF16)
        Wkv[...] = wkv[...].astype(BF16)

        @pl.when(pl.program_id(0) == 0)
        def _():
            for k, r in enumerate((pa, pb, pc, po)):
                Wp[k] = r[...].astype(BF16)
            Cw[...] = cw[...]

    proj = pl.BlockSpec((None, rp, D), lambda i, p: (l, 0, 0))
    return _pcall(
        body, name="cast_place", grid=(2,), prefetch=1,
        in_specs=[pl.BlockSpec((None, half, nsh), lambda i, p: (l, i, 0)), pl.BlockSpec((None, half, nkv), lambda i, p: (l, i, 0)),
                  proj, proj, proj, proj, pl.BlockSpec((None, 8, D // 4), lambda i, p: (l, 0, 0))],
        out_specs=[pl.BlockSpec((half, nsh), lambda i, p: (i, p[0])), pl.BlockSpec((half, nkv), lambda i, p: (i, p[0])),
                   pl.BlockSpec((4, rp, D), lambda i, p: (0, p[0], 0)), pl.BlockSpec((8, D // 4), lambda i, p: (0, p[0]))],
        out_shape=[SDS((kin, 4 * nsh), BF16), SDS((kin, 4 * nkv), BF16), SDS((4, 4 * rp, D), BF16), SDS((8, D), F32)],
    )(pos, w_in, w_kv, w_pa, w_pb, w_pc, w_out, cw8)


def _hbm_like(bufs):
    return [pltpu.HBM(b.shape, b.dtype) for b in bufs]


def _w_part(a, ref, jj, cc):
    return _cw_block(ref, jj) if a == 3 else _w_half(a, ref, jj, cc)


def _gather_start(name, kinds, bufs, after=None):
    n = len(kinds)
    extra = [] if after is None else [after]

    def body(*refs):
        w, ssem, rsem, token = refs[0:n], refs[n + len(extra)], refs[n + len(extra) + 1], refs[-1]
        x, y, c = _coords()
        j = 2 * x + y
        for k, chip in enumerate(_other_chips(x, y)):
            for i, a in enumerate(kinds):
                part = _w_part(a, w[i], j, c)
                _remote(part, part, ssem, rsem, i * 3 + k, (chip[0], chip[1], c)).start()
        token[...] = jnp.zeros((8, 128), F32)

    return _pcall(
        body, name=name, in_specs=[HBM] * n + [ANY] * len(extra), out_specs=[SEM, SEM] + [HBM] * n + [VMEM],
        out_shape=[pltpu.SemaphoreType.DMA((3 * n,)), pltpu.SemaphoreType.DMA((3 * n,))] + _hbm_like(bufs) + [TOKEN],
        aliases={i: 2 + i for i in range(n)}, effect=True,
    )(*[pltpu.with_memory_space_constraint(b, pltpu.HBM) for b in bufs], *extra)


def _gather_mid(name, kinds, started, after):
    n = len(kinds)
    fw = [i for i, a in enumerate(kinds) if a != 3]
    bufs = tuple(started[2:2 + n])

    def body(*refs):
        w, ssem, rsem, ssem2, rsem2, token = refs[0:n], refs[n], refs[n + 1], refs[n + 3], refs[n + 4], refs[-1]
        x, y, c = _coords()
        j = 2 * x + y
        me, sib = (x, y, c), (x, y, 1 - c)
        token[...] = jnp.zeros((8, 128), F32)
        chips = _other_chips(x, y)
        for k, chip in enumerate(chips):
            for i, a in enumerate(kinds):
                got = _w_part(a, w[i], 2 * chip[0] + chip[1], c)
                _remote(got, got, ssem, rsem, i * 3 + k, me).wait_recv()
        for k in range(3):
            for i, a in enumerate(kinds):
                part = _w_part(a, w[i], j, c)
                _remote(part, part, ssem, rsem, i * 3 + k, me).wait_send()
        for k, chip in enumerate(chips):
            for f, i in enumerate(fw):
                got = _w_half(kinds[i], w[i], 2 * chip[0] + chip[1], c)
                _remote(got, got, ssem2, rsem2, f * 3 + k, sib).start()

    return _pcall(
        body, name=name, in_specs=[HBM] * n + [SEM, SEM, ANY], out_specs=[SEM, SEM] + [HBM] * n + [VMEM],
        out_shape=[pltpu.SemaphoreType.DMA((3 * len(fw),)), pltpu.SemaphoreType.DMA((3 * len(fw),))] + _hbm_like(bufs) + [TOKEN],
        aliases={i: 2 + i for i in range(n)}, effect=True,
    )(*bufs, started[0], started[1], after)


def _gather_end(name, kinds, mid, after):
    n = len(kinds)
    fw = [i for i, a in enumerate(kinds) if a != 3]
    bufs = tuple(mid[2:2 + n])

    def body(*refs):
        w, ssem2, rsem2 = refs[0:n], refs[n], refs[n + 1]
        x, y, c = _coords()
        me = (x, y, c)
        for k, chip in enumerate(_other_chips(x, y)):
            for f, i in enumerate(fw):
                got = _w_half(kinds[i], w[i], 2 * chip[0] + chip[1], 1 - c)
                _remote(got, got, ssem2, rsem2, f * 3 + k, me).wait_recv()
                sent = _w_half(kinds[i], w[i], 2 * chip[0] + chip[1], c)
                _remote(sent, sent, ssem2, rsem2, f * 3 + k, me).wait_send()

    return _pcall(
        body, name=name, in_specs=[HBM] * n + [SEM, SEM, ANY], out_specs=[HBM] * n, out_shape=_hbm_like(bufs),
        aliases={i: i for i in range(n)}, effect=True,
    )(*bufs, mid[0], mid[1], after)


def _g_piece(a, ref, jd, dc):
    if a == 2:
        rp = ref.shape[1] // 4
        return ref.at[:, pl.ds(jd * rp + dc * (rp // 2), rp // 2), :]
    kin, nsh = ref.shape[0], ref.shape[1] // 4
    return ref.at[pl.ds(dc * (kin // 2), kin // 2), pl.ds(jd * nsh, nsh)]


def _a2a_start(name, kinds, grads, lands):
    n = len(kinds)

    def body(*refs):
        g, ld, ssem, rsem, token = refs[0:n], refs[n:2 * n], refs[2 * n], refs[2 * n + 1], refs[-1]
        x, y, c = _coords()
        for mask in range(1, NDEV):
            p = _peer(x, y, c, mask)
            for i, a in enumerate(kinds):
                _remote(_g_piece(a, g[i], 2 * p[0] + p[1], p[2]), ld[i].at[mask - 1], ssem, rsem, i * 7 + mask - 1, p).start()
        token[...] = jnp.zeros((8, 128), F32)

    bufs = tuple(grads) + tuple(lands)
    return _pcall(
        body, name=name, in_specs=[HBM] * (2 * n), out_specs=[SEM, SEM] + [HBM] * (2 * n) + [VMEM],
        out_shape=[pltpu.SemaphoreType.DMA((7 * n,)), pltpu.SemaphoreType.DMA((7 * n,))] + _hbm_like(bufs) + [TOKEN],
        aliases={i: 2 + i for i in range(2 * n)}, effect=True,
    )(*[pltpu.with_memory_space_constraint(b, pltpu.HBM) for b in bufs])


def _a2a_wait(name, kinds, started, after):
    n = len(kinds)
    ssem, rsem = started[0], started[1]
    bufs = tuple(started[2:2 + 2 * n])

    def body(*refs):
        g, ld, ssem, rsem = refs[0:n], refs[n:2 * n], refs[2 * n], refs[2 * n + 1]
        x, y, c = _coords()
        me = (x, y, c)
        for mask in range(1, NDEV):
            for i in range(n):
                got = ld[i].at[mask - 1]
                _remote(got, got, ssem, rsem, i * 7 + mask - 1, me).wait_recv()
        for mask in range(1, NDEV):
            p = _peer(x, y, c, mask)
            for i, a in enumerate(kinds):
                sent = _g_piece(a, g[i], 2 * p[0] + p[1], p[2])
                _remote(sent, sent, ssem, rsem, i * 7 + mask - 1, me).wait_send()

    return _pcall(
        body, name=name, in_specs=[HBM] * (2 * n) + [SEM, SEM, ANY], out_specs=[HBM] * (2 * n), out_shape=_hbm_like(bufs),
        aliases={i: i for i in range(2 * n)}, effect=True,
    )(*bufs, ssem, rsem, after)


def _sum_share(pos, lands, grads):
    rows, n = 128, 4
    widths = [ld.shape[2] for ld in lands]

    def body(pos_r, l0, w0, l1, w1, l2, w2, g0, g1, g2, b0, b1, b2, lsem, ssem, rsem):
        i = pl.program_id(0)
        x, y, c = _coords()
        sib = (x, y, 1 - c)
        ld, ow, gs, bufs = (l0, l1, l2), (w0, w1, w2), (g0, g1, g2), (b0, b1, b2)

        def dst(a, step):
            row = step * (2 * rows) + c * rows if a == 2 else c * (n * rows) + step * rows
            return gs[a].at[pl.ds(row, rows), :]

        def copies(a, step, sl):
            src = bufs[a].at[sl]
            lc = pltpu.make_async_copy(src, dst(a, step), lsem.at[a, sl])
            rc = pltpu.make_async_remote_copy(src_ref=src, dst_ref=dst(a, step), send_sem=ssem.at[a, sl], recv_sem=rsem.at[a],
                                              device_id=sib, device_id_type=MESH)
            return lc, rc

        def drain(a, step, sl):
            lc, rc = copies(a, step, sl)
            lc.wait()
            rc.wait_send()

        slot = i % 2

        @pl.when(i >= 2)
        def _():
            for a in range(3):
                drain(a, i - 2, slot)

        for a in range(3):
            acc = ow[a][...].astype(F32)
            for k in range(NDEV - 1):
                acc = acc + ld[a][k].astype(F32)
            bufs[a][slot] = acc
            lc, rc = copies(a, i, slot)
            lc.start()
            rc.start()

        @pl.when(i == n - 1)
        def _():
            for a in range(3):
                drain(a, n - 2, (n - 2) % 2)
                drain(a, n - 1, (n - 1) % 2)
                whole = gs[a].at[pl.ds(0, n * rows), :]
                pltpu.make_async_remote_copy(src_ref=whole, dst_ref=whole, send_sem=ssem.at[a, 0], recv_sem=rsem.at[a],
                                             device_id=(x, y, c), device_id_type=MESH).wait_recv()

    land = lambda w: pl.BlockSpec((NDEV - 1, rows, w), lambda i, p: (0, i, 0))
    in_specs = [land(widths[0]), pl.BlockSpec((rows, widths[0]), lambda i, p: (p[1] * n + i, p[0])),
                land(widths[1]), pl.BlockSpec((rows, widths[1]), lambda i, p: (p[1] * n + i, p[0])),
                land(widths[2]), pl.BlockSpec((None, rows, widths[2]), lambda i, p: (i, 2 * p[0] + p[1], 0))]
    args = [t for pair in zip(lands, grads) for t in pair]
    return _pcall(
        body, name="sum_share", grid=(n,), prefetch=1, in_specs=in_specs, out_specs=[HBM] * 3,
        out_shape=[SDS((2 * n * rows, w), F32) for w in widths],
        scratch=[pltpu.VMEM((2, rows, w), F32) for w in widths]
        + [pltpu.SemaphoreType.DMA((3, 2)), pltpu.SemaphoreType.DMA((3, 2)), pltpu.SemaphoreType.DMA((3,))],
    )(pos, *args)


def _dev_index(p):
    return 4 * p[0] + 2 * p[1] + p[2]


def _small_rows(ref, d):
    r8 = ref.shape[0] // NDEV
    return ref.at[pl.ds(d * r8, r8), :]


def _rs_start(pack, land):
    def body(p_ref, ld, ssem, rsem, o0, o1, token):
        x, y, c = _coords()
        for mask in range(1, NDEV):
            p = _peer(x, y, c, mask)
            _remote(_small_rows(p_ref, _dev_index(p)), ld.at[mask - 1], ssem, rsem, mask - 1, p).start()
        token[...] = jnp.zeros((8, 128), F32)

    bufs = (pack, land)
    return _pcall(
        body, name="rs_start", in_specs=[HBM] * 2, out_specs=[SEM, SEM, HBM, HBM, VMEM],
        out_shape=[pltpu.SemaphoreType.DMA((NDEV - 1,)), pltpu.SemaphoreType.DMA((NDEV - 1,))] + _hbm_like(bufs) + [TOKEN],
        aliases={0: 2, 1: 3}, effect=True,
    )(*[pltpu.with_memory_space_constraint(b, pltpu.HBM) for b in bufs])


def _rs_wait(started, after):
    ssem, rsem, pack, land, _ = started

    def body(p_ref, ld, ssem, rsem, after_r, o0, o1):
        x, y, c = _coords()
        for mask in range(1, NDEV):
            got = ld.at[mask - 1]
            _remote(got, got, ssem, rsem, mask - 1, (x, y, c)).wait_recv()
        for mask in range(1, NDEV):
            sent = _small_rows(p_ref, _dev_index(_peer(x, y, c, mask)))
            _remote(sent, sent, ssem, rsem, mask - 1, (x, y, c)).wait_send()

    return _pcall(body, name="rs_wait", in_specs=[HBM, HBM, SEM, SEM, ANY], out_specs=[HBM, HBM],
                  out_shape=_hbm_like((pack, land)), aliases={0: 0, 1: 1}, effect=True)(pack, land, ssem, rsem, after)


def _small_sum(me1, pack, land):
    R = pack.shape[0]
    r8 = R // NDEV

    def body(me_r, p_ref, ld, full):
        acc = p_ref[...]
        for k in range(NDEV - 1):
            acc = acc + ld[k]
        full[...] = acc

    own = pl.BlockSpec((r8, 128), lambda i, m: (m[0], 0))
    return _pcall(body, name="small_sum", grid=(1,), prefetch=1,
                  in_specs=[own, pl.BlockSpec((NDEV - 1, r8, 128), lambda i, m: (0, 0, 0))], out_specs=own,
                  out_shape=SDS((R, 128), F32), vmem_mb=32)(me1, pack, land)


def _ag_start(full):
    def body(f_ref, ssem, rsem, o0, token):
        x, y, c = _coords()
        mine = _small_rows(f_ref, _dev_index((x, y, c)))
        for mask in range(1, NDEV):
            _remote(mine, mine, ssem, rsem, mask - 1, _peer(x, y, c, mask)).start()
        token[...] = jnp.zeros((8, 128), F32)

    return _pcall(
        body, name="ag_start", in_specs=[HBM], out_specs=[SEM, SEM, HBM, VMEM],
        out_shape=[pltpu.SemaphoreType.DMA((NDEV - 1,)), pltpu.SemaphoreType.DMA((NDEV - 1,))] + _hbm_like((full,)) + [TOKEN],
        aliases={0: 2}, effect=True,
    )(pltpu.with_memory_space_constraint(full, pltpu.HBM))


def _ag_wait(started, after):
    ssem, rsem, full, _ = started

    def body(f_ref, ssem, rsem, after_r, o0):
        x, y, c = _coords()
        mine = _small_rows(f_ref, _dev_index((x, y, c)))
        for mask in range(1, NDEV):
            got = _small_rows(f_ref, _dev_index(_peer(x, y, c, mask)))
            _remote(got, got, ssem, rsem, mask - 1, (x, y, c)).wait_recv()
            _remote(mine, mine, ssem, rsem, mask - 1, (x, y, c)).wait_send()

    return _pcall(body, name="ag_wait", in_specs=[HBM, SEM, SEM, ANY], out_specs=[HBM], out_shape=_hbm_like((full,)),
                  aliases={0: 0}, effect=True)(full, ssem, rsem, after)[0]


def _mm_in(x, g, w):
    S = x.shape[0]
    tm, tn = min(1024, S), 1280

    def body(x_ref, g_ref, w_ref, z_ref, h_ref, hs):
        @pl.when(pl.program_id(1) == 0)
        def _():
            xv = x_ref[...]
            hb = (xv * _rms_scale(xv) * g_ref[...]).astype(BF16)
            hs[...] = hb
            h_ref[...] = hb

        z_ref[...] = _dot(hs[...], w_ref[...]).astype(BF16)

    return _pcall(
        body, name="mm_in", grid=(S // tm, NIN // tn),
        in_specs=[pl.BlockSpec((tm, D), lambda i, j: (i, 0)), _full((1, D)), pl.BlockSpec((D, tn), lambda i, j: (0, j))],
        out_specs=[pl.BlockSpec((tm, tn), lambda i, j: (i, j)), pl.BlockSpec((tm, D), lambda i, j: (i, 0))],
        out_shape=[SDS((S, NIN), BF16), SDS((S, D), BF16)], scratch=[pltpu.VMEM((tm, D), BF16)],
    )(x, g, w)


def _chunk_mask():
    ri = lax.broadcasted_iota(jnp.int32, (GB, GB), 0)
    ci = lax.broadcasted_iota(jnp.int32, (GB, GB), 1)
    return (ri >= 64) | (ci < 64)


def _layernorm_parts(v):
    mu = jnp.mean(v, axis=-1, keepdims=True)
    d = v - mu
    rs = lax.rsqrt(jnp.mean(d * d, axis=-1, keepdims=True) + EPS)
    return d * rs, rs


def _branch_a(z, lg, lb, ws, bsb):
    S = z.shape[0]
    T = min(512, S)

    def body(zu, zv, zg, lg_r, lb_r, ws_r, bs_r, ya):
        vhat, _ = _layernorm_parts(zv[...].astype(F32))
        vnb = (vhat * lg_r[...] + lb_r[...]).astype(BF16)
        sil, _ = _silu_parts(zg[...].astype(F32))
        t = zu[...].astype(F32) * sil
        mask = _chunk_mask()
        for g in range(NG):
            wg = jnp.where(mask, ws_r[g], 0.0).astype(BF16)
            cs = slice(g * GB, (g + 1) * GB)
            for n in range(T // GB):
                rs = slice(n * GB, (n + 1) * GB)
                sv = _dot(wg, vnb[rs, cs]) + bs_r[g]
                ya[rs, cs] = (t[rs, cs] * sv).astype(BF16)

    zs = lambda k: pl.BlockSpec((T, D), lambda i: (i, k))
    return _pcall(
        body, name="branch_a", grid=(S // T,),
        in_specs=[zs(0), zs(1), zs(2), _full((1, D)), _full((1, D)), _full((NG, GB, GB)), _full((NG, GB, GB))],
        out_specs=pl.BlockSpec((T, D), lambda i: (i, 0)), out_shape=SDS((S, D), BF16),
    )(z, z, z, lg, lb, ws, bsb)


def _softplus_neg(lam):
    e = jnp.exp(-jnp.abs(lam))
    l1p = jnp.where(e < 1e-2, e * (1.0 - e * (0.5 - e * (1.0 / 3.0))), jnp.log(1.0 + e))
    return jnp.maximum(-lam, 0.0) + l1p


CH = 16


def _ck(c, off=0):
    return pl.ds(c * CH + off, CH)


def _half_sum(v):
    return v[0:8, :] + v[8:16, :]


def _lru_gates(pr, pi, br, bi, sp8):
    r = jax.nn.sigmoid(pr + br)
    ig = _sigmoid(pi + bi)
    la = sp8 * r
    a = jnp.exp(la)
    a2 = a * a
    mult = jnp.sqrt(-jnp.tanh(la) * (a2 + 1.0))
    return r, ig, a, a2, mult


def _tile_rows():
    return lax.broadcasted_iota(jnp.int32, (8, D), 0)


def _scan_forward(a_s, u_s, h_s, hcar, T):
    row = _tile_rows()

    def tile(i, hp):
        o = pl.multiple_of(i * 8, 8)
        A = a_s[pl.ds(o, 8), :]
        U = u_s[pl.ds(o, 8), :]
        for s in (1, 2, 4):
            m = row >= s
            U = jnp.where(m, U + A * pltpu.roll(U, s, 0), U)
            A = jnp.where(m, A * pltpu.roll(A, s, 0), A)
        H = U + A * hp
        h_s[pl.ds(o, 8), :] = H
        return jnp.broadcast_to(H[7:8, :], (8, D))

    hcar[...] = lax.fori_loop(0, T // 8, tile, hcar[...])


def _scan_reverse(b_s, d_s, l_s, lcar, T):
    row = _tile_rows()
    n = T // 8

    def tile(i, lp):
        o = pl.multiple_of((n - 1 - i) * 8, 8)
        B = b_s[pl.ds(o, 8), :]
        U = d_s[pl.ds(o, 8), :]
        for s in (1, 2, 4):
            m = row < 8 - s
            U = jnp.where(m, U + B * pltpu.roll(U, 8 - s, 0), U)
            B = jnp.where(m, B * pltpu.roll(B, 8 - s, 0), B)
        Lm = U + B * lp
        l_s[pl.ds(o, 8), :] = Lm
        return jnp.broadcast_to(Lm[0:1, :], (8, D))

    lcar[...] = lax.fori_loop(0, n, tile, lcar[...])


def _branch_b(z, cw, cb, wr, br, wi, bi, lam):
    S = z.shape[0]
    T = min(256, S)

    def body(zxb, zgb, cw_r, cb_r, wr_r, br_r, wi_r, bi_r, lam_r, yb, hs_o, xc_o, r_o, ig_o, a_o, m_o, xpad, u_s, hcar):
        @pl.when(pl.program_id(0) == 0)
        def _():
            xpad[pl.ds(0, 8), :] = jnp.zeros((8, D), F32)
            hcar[...] = jnp.zeros((8, D), F32)

        cw = cw_r[...]
        xpad[pl.ds(8, T), :] = zxb[...].astype(F32)
        xk = [xpad[pl.ds(5 + k, T), :] for k in range(4)]
        xc = cb_r[...] + (((xk[0] * cw[0:1] + xk[1] * cw[1:2]) + xk[2] * cw[2:3]) + xk[3] * cw[3:4])
        xcb = xc.astype(BF16)
        pr, pi = [], []
        for h in range(NG):
            cs = slice(h * GB, (h + 1) * GB)
            pr.append(_dot(xcb[:, cs], wr_r[h].astype(BF16)))
            pi.append(_dot(xcb[:, cs], wi_r[h].astype(BF16)))
        r, ig, a, _, mult = _lru_gates(jnp.concatenate(pr, axis=1), jnp.concatenate(pi, axis=1), br_r[...], bi_r[...],
                                       -LRU_C * _softplus_neg(lam_r[...]))
        for o_ref, val in ((xc_o, xc), (r_o, r), (ig_o, ig), (a_o, a), (m_o, mult)):
            o_ref[...] = val
        u_s[...] = mult * (ig * xc)
        _scan_forward(a_o, u_s, hs_o, hcar, T)
        xpad[pl.ds(0, 8), :] = xpad[pl.ds(T, 8), :]
        sil, _ = _silu_parts(zgb[...].astype(F32))
        yb[...] = (hs_o[...] * sil).astype(BF16)

    zs = lambda k: pl.BlockSpec((T, D), lambda i: (i, k))
    row = pl.BlockSpec((T, D), lambda i: (i, 0))
    return _pcall(
        body, name="branch_b", grid=(S // T,),
        in_specs=[zs(3), zs(4), _full((8, D)), _full((1, D)), _full((NG, GB, GB)), _full((1, D)), _full((NG, GB, GB)),
                  _full((1, D)), _full((1, D))],
        out_specs=[row] * 7, out_shape=[SDS((S, D), BF16)] + [SDS((S, D), F32)] * 6,
        scratch=[pltpu.VMEM((T + 8, D), F32), pltpu.VMEM((T, D), F32), pltpu.VMEM((8, D), F32)],
    )(z, z, cw, cb, wr, br, wi, bi, lam)


def _kv(mem, g, wkv):
    def body(m_ref, g_ref, w_ref, kv_ref):
        m = m_ref[...]
        mn = (m * _rms_scale(m) * g_ref[...]).astype(BF16)
        kv_ref[...] = _dot(mn, w_ref[...]).astype(BF16)

    return _pcall(body, name="mem_kv", in_specs=[VMEM] * 3, out_specs=VMEM, out_shape=SDS((MEM, 2 * D), BF16),
                  vmem_mb=32)(mem, g, wkv)


def _softmax_rows(s):
    e = jnp.exp(s - jnp.max(s, axis=-1, keepdims=True))
    return e / jnp.sum(e, axis=-1, keepdims=True)


def _branch_c(z, kv):
    S = z.shape[0]
    T = min(512, S)

    def body(zq, zg, kv_r, yc, p_o):
        sil, _ = _silu_parts(zg[...].astype(F32))
        for h in range(NH):
            cs = slice(h * HD, (h + 1) * HD)
            pb = _softmax_rows(_dot_nt(zq[:, cs], kv_r[:, cs]) * (HD ** -0.5)).astype(BF16)
            p_o[:, cs] = pb
            att = _dot(pb, kv_r[:, D + h * HD:D + (h + 1) * HD])
            yc[:, cs] = (att * sil[:, cs]).astype(BF16)

    zs = lambda k: pl.BlockSpec((T, D), lambda i: (i, k))
    row = pl.BlockSpec((T, D), lambda i: (i, 0))
    return _pcall(body, name="branch_c", grid=(S // T,), in_specs=[zs(5), zs(6), _full((MEM, 2 * D))],
                  out_specs=[row, row], out_shape=[SDS((S, D), BF16)] * 2)(z, z, kv)


def _merge_out(ya, yb, yc, z, wp, x, pg):
    S = x.shape[0]
    T = min(256, S)

    def body(ya_r, yb_r, yc_r, m0, m1, m2, wp_r, x_r, pg_r, pa_o, pb_o, pc_o, mg_o, o_o, xn_o):
        merged = None
        for y_r, ml, p_o, k in ((ya_r, m0, pa_o, 0), (yb_r, m1, pb_o, 1), (yc_r, m2, pc_o, 2)):
            p = _dot(y_r[...], wp_r[k])
            p_o[...] = p.astype(BF16)
            t = _sigmoid(ml[...].astype(F32)) * p
            merged = t if merged is None else merged + t
        mb = merged.astype(BF16)
        mg_o[...] = mb
        o = _dot(mb, wp_r[3])
        o_o[...] = o.astype(BF16)
        xn_o[...] = x_r[...] + o * _rms_scale(o) * pg_r[...]

    row = pl.BlockSpec((T, D), lambda i: (i, 0))
    zs = lambda k: pl.BlockSpec((T, D), lambda i: (i, k))
    return _pcall(
        body, name="merge_out", grid=(S // T,),
        in_specs=[row, row, row, zs(7), zs(8), zs(9), _full((4, D, D)), row, _full((1, D))],
        out_specs=[row] * 6, out_shape=[SDS((S, D), BF16)] * 5 + [SDS((S, D), F32)], vmem_mb=56,
    )(ya, yb, yc, z, z, z, wp, x, pg)


def _loss_head(y, t):
    S = y.shape[0]
    T = min(512, S)

    def body(y_r, t_r, loss_o, dy_o):
        @pl.when(pl.program_id(0) == 0)
        def _():
            loss_o[...] = jnp.zeros((1, 1), F32)

        e = y_r[...] - t_r[...]
        dy_o[...] = e * (1.0 / D)
        loss_o[...] += 0.5 * _rowsum(jnp.sum(e * e, axis=1, keepdims=True) * (1.0 / D))

    row = pl.BlockSpec((T, D), lambda i: (i, 0))
    return _pcall(body, name="loss_head", grid=(S // T,), in_specs=[row, row], out_specs=[_full((1, 1)), row],
                  out_shape=[SDS((1, 1), F32), SDS((S, D), F32)])(y, t)


def _accumulate(first, ref, val):
    @pl.when(first)
    def _():
        ref[...] = val

    @pl.when(jnp.logical_not(first))
    def _():
        ref[...] += val


def _out_bwd(dxn, o, pg, wp, z, pa, pb, pc):
    S = dxn.shape[0]
    T = min(256, S)

    def body(dy_r, o_r, pg_r, wp_r, m0, m1, m2, pa_r, pb_r, pc_r, do_o, dpa_o, dpb_o, dpc_o, dya_o, dyb_o, dyc_o, dz_o, dg_o):
        dy = dy_r[...]
        o = o_r[...].astype(F32)
        r2 = _rms_scale(o)
        w = dy * pg_r[...]
        do = r2 * w - o * (r2 * r2 * r2) * jnp.mean(w * o, axis=-1, keepdims=True)
        _accumulate(pl.program_id(0) == 0, dg_o, _rowsum(dy * o * r2))
        dob = do.astype(BF16)
        do_o[...] = dob
        dm = _dot_nt(dob, wp_r[3])
        for k, (ml, p_r, dp_o, dy_o) in enumerate(((m0, pa_r, dpa_o, dya_o), (m1, pb_r, dpb_o, dyb_o), (m2, pc_r, dpc_o, dyc_o))):
            gk = _sigmoid(ml[...].astype(F32))
            dz_o[k] = (dm * p_r[...].astype(F32) * gk * (1.0 - gk)).astype(BF16)
            dpk = (gk * dm).astype(BF16)
            dp_o[...] = dpk
            dy_o[...] = _dot_nt(dpk, wp_r[k]).astype(BF16)

    row = pl.BlockSpec((T, D), lambda i: (i, 0))
    zs = lambda k: pl.BlockSpec((T, D), lambda i: (i, k))
    return _pcall(
        body, name="out_bwd", grid=(S // T,),
        in_specs=[row, row, _full((1, D)), _full((4, D, D)), zs(7), zs(8), zs(9), row, row, row],
        out_specs=[row] * 7 + [pl.BlockSpec((3, T, D), lambda i: (1, i, 0)), _full((1, D))],
        out_shape=[SDS((S, D), BF16)] * 7 + [SDS((10, S, D), BF16), SDS((1, D), F32)], vmem_mb=56,
    )(dxn, o, pg, wp, z, z, z, pa, pb, pc)


def _branch_a_bwd(z, dya, lg, lb, ws, bsb, dz):
    S = z.shape[0]
    T = min(512, S)
    nblk = S // T

    def body(zu, zv, zg, dy_r, lg_r, lb_r, ws_r, bs_r, dz_in, dz_o, dws_o, dbs_o, dlg_o, dlb_o, dvn_s, bacc):
        i = pl.program_id(0)

        @pl.when(i == 0)
        def _():
            dws_o[...] = jnp.zeros((NG, GB, GB), F32)
            bacc[...] = jnp.zeros((NG, GB, GB), F32)

        vhat, rs = _layernorm_parts(zv[...].astype(F32))
        vnb = (vhat * lg_r[...] + lb_r[...]).astype(BF16)
        ga = zg[...].astype(F32)
        sil, dsil = _silu_parts(ga)
        u = zu[...].astype(F32)
        dy = dy_r[...].astype(F32)
        t = dy * sil
        dsv_all = t * u
        dga_pre = dy * u * dsil
        mask = _chunk_mask()
        for g in range(NG):
            wf = jnp.where(mask, ws_r[g], 0.0)
            wg = wf.astype(BF16)
            wgt = wf.T.astype(BF16)
            cs = slice(g * GB, (g + 1) * GB)
            dw = jnp.zeros((GB, GB), F32)
            db = jnp.zeros((GB, GB), F32)
            for n in range(T // GB):
                rsl = slice(n * GB, (n + 1) * GB)
                vb = vnb[rsl, cs]
                sv = _dot(wg, vb) + bs_r[g]
                dz_o[0, rsl, cs] = (t[rsl, cs] * sv).astype(BF16)
                dz_o[2, rsl, cs] = (dga_pre[rsl, cs] * sv).astype(BF16)
                dsv = dsv_all[rsl, cs]
                dsb = dsv.astype(BF16)
                dvn_s[rsl, cs] = _dot(wgt, dsb)
                dw = dw + _dot_nt(dsb, vb)
                db = db + dsv
            dws_o[g] += jnp.where(mask, dw, 0.0)
            bacc[g] += db
        dvn = dvn_s[...]
        dvh = dvn * lg_r[...]
        dv = rs * (dvh - jnp.mean(dvh, axis=-1, keepdims=True) - vhat * jnp.mean(dvh * vhat, axis=-1, keepdims=True))
        dz_o[1] = dv.astype(BF16)
        _accumulate(i == 0, dlg_o, _rowsum(dvn * vhat))
        _accumulate(i == 0, dlb_o, _rowsum(dvn))

        @pl.when(i == nblk - 1)
        def _():
            for g in range(NG):
                dbs_o[g:g + 1, :] = _rowsum(bacc[g].T)

    zs = lambda k: pl.BlockSpec((T, D), lambda i: (i, k))
    return _pcall(
        body, name="branch_a_bwd", grid=(nblk,),
        in_specs=[zs(0), zs(1), zs(2), pl.BlockSpec((T, D), lambda i: (i, 0)), _full((1, D)), _full((1, D)),
                  _full((NG, GB, GB)), _full((NG, GB, GB)), HBM],
        out_specs=[pl.BlockSpec((3, T, D), lambda i: (0, i, 0)), _full((NG, GB, GB)), _full((NG, GB)), _full((1, D)),
                   _full((1, D))],
        out_shape=[SDS((10, S, D), BF16), SDS((NG, GB, GB), F32), SDS((NG, GB), F32), SDS((1, D), F32), SDS((1, D), F32)],
        scratch=[pltpu.VMEM((T, D), F32), pltpu.VMEM((NG, GB, GB), F32)], aliases={8: 0},
    )(z, z, z, dya, lg, lb, ws, bsb, dz)


def _branch_b_bwd(z, hs, lru, dyb, cw, wr, wi, lam, dz):
    S = z.shape[0]
    T = min(256, S)
    nblk = S // T

    def body(zxb, zprev, zgb, hs_r, hprev_r, dy_r, xc_r, r_r, ig_r, a_r, m_r, cw_r, wr_r, wi_r, lam_r, dz_in,
             dz_o, dcw_o, dcb_o, dwr_o, dbr_o, dwi_o, dbi_o, dlam_o, xpad, hpad, apad, dpad,
             b_s, d_s, l_s, back_s, xcb_s, dprb_s, dpib_s, lcar):
        i = pl.program_id(0)
        blk = nblk - 1 - i
        first = i == 0

        @pl.when(first)
        def _():
            apad[pl.ds(T, 8), :] = jnp.zeros((8, D), F32)
            dpad[pl.ds(T, 8), :] = jnp.zeros((8, D), F32)
            lcar[...] = jnp.zeros((8, D), F32)
            dcw_o[...] = jnp.zeros((8, D), F32)
            dwr_o[...] = jnp.zeros((NG, GB, GB), F32)
            dwi_o[...] = jnp.zeros((NG, GB, GB), F32)

        keep = (blk > 0).astype(F32)
        nck = T // CH
        cw, lam = cw_r[...], lam_r[...]
        sp8 = -LRU_C * _softplus_neg(lam)
        xpad[pl.ds(0, 8), :] = zprev[...].astype(F32)[8:16, :] * keep
        hpad[pl.ds(0, 8), :] = hprev_r[...] * keep
        for c in range(nck):
            xpad[_ck(c, 8), :] = zxb[_ck(c), :].astype(F32)
            hpad[_ck(c, 8), :] = hs_r[_ck(c), :]
            xcb_s[_ck(c), :] = xc_r[_ck(c), :].astype(BF16)
            apad[_ck(c), :] = a_r[_ck(c), :]
            sil, dsil = _silu_parts(zgb[_ck(c), :].astype(F32))
            dy = dy_r[_ck(c), :].astype(F32)
            dz_o[1, _ck(c), :] = (dy * hs_r[_ck(c), :] * dsil).astype(BF16)
            d_s[_ck(c), :] = dy * sil
        for c in range(nck):
            b_s[_ck(c), :] = apad[_ck(c, 1), :]
        _scan_reverse(b_s, d_s, l_s, lcar, T)
        s_sp = s_br = s_bi = jnp.zeros((8, D), F32)
        for c in range(nck):
            lm, r, ig, mult, a, xc = l_s[_ck(c), :], r_r[_ck(c), :], ig_r[_ck(c), :], m_r[_ck(c), :], a_r[_ck(c), :], xc_r[_ck(c), :]
            t = lm * mult
            dpad[_ck(c), :] = t * ig
            dl = lm * hpad[_ck(c, 7), :] * a - (lm * ig * xc) * (a * a) / mult
            dpr = dl * sp8 * r * (1.0 - r)
            dpi = t * xc * ig * (1.0 - ig)
            s_sp = s_sp + _half_sum(dl * r)
            s_br = s_br + _half_sum(dpr)
            s_bi = s_bi + _half_sum(dpi)
            dprb_s[_ck(c), :] = dpr.astype(BF16)
            dpib_s[_ck(c), :] = dpi.astype(BF16)
        _accumulate(first, dlam_o, _rowsum(s_sp) * (LRU_C * jax.nn.sigmoid(-lam)))
        _accumulate(first, dbr_o, _rowsum(s_br))
        _accumulate(first, dbi_o, _rowsum(s_bi))
        for h in range(NG):
            cs = slice(h * GB, (h + 1) * GB)
            back_s[:, cs] = _dot_nt(dprb_s[:, cs], wr_r[h].astype(BF16)) + _dot_nt(dpib_s[:, cs], wi_r[h].astype(BF16))
            dwr_o[h] += _dot_tn(xcb_s[:, cs], dprb_s[:, cs])
            dwi_o[h] += _dot_tn(xcb_s[:, cs], dpib_s[:, cs])
        s_cb = jnp.zeros((8, D), F32)
        s_cw = [jnp.zeros((8, D), F32)] * 4
        for c in range(nck):
            dxc = dpad[_ck(c), :] + back_s[_ck(c), :]
            dpad[_ck(c), :] = dxc
            s_cb = s_cb + _half_sum(dxc)
            s_cw = [s_cw[k] + _half_sum(xpad[_ck(c, 5 + k), :] * dxc) for k in range(4)]
        _accumulate(first, dcb_o, _rowsum(s_cb))
        for k in range(4):
            dcw_o[k:k + 1, :] += _rowsum(s_cw[k])
        for c in range(nck):
            dxb = ((dpad[_ck(c, 3), :] * cw[0:1] + dpad[_ck(c, 2), :] * cw[1:2]) + dpad[_ck(c, 1), :] * cw[2:3]) + dpad[_ck(c), :] * cw[3:4]
            dz_o[0, _ck(c), :] = dxb.astype(BF16)
        apad[pl.ds(T, 8), :] = apad[pl.ds(0, 8), :]
        dpad[pl.ds(T, 8), :] = dpad[pl.ds(0, 8), :]

    rev = lambda k: pl.BlockSpec((T, D), lambda i: (nblk - 1 - i, k))
    prev16 = pl.BlockSpec((16, D), lambda i: (jnp.maximum((nblk - 1 - i) * (T // 16) - 1, 0), 3))
    prev8 = pl.BlockSpec((8, D), lambda i: (jnp.maximum((nblk - 1 - i) * (T // 8) - 1, 0), 0))
    vec, mat = _full((1, D)), _full((NG, GB, GB))
    return _pcall(
        body, name="branch_b_bwd", grid=(nblk,),
        in_specs=[rev(3), prev16, rev(4), rev(0), prev8] + [rev(0)] * 6 + [_full((8, D)), mat, mat, vec, HBM],
        out_specs=[pl.BlockSpec((2, T, D), lambda i: (3, nblk - 1 - i, 0)), _full((8, D)), vec, mat, vec, mat, vec, vec],
        out_shape=[SDS((10, S, D), BF16), SDS((8, D), F32), SDS((1, D), F32), SDS((NG, GB, GB), F32), SDS((1, D), F32),
                   SDS((NG, GB, GB), F32), SDS((1, D), F32), SDS((1, D), F32)],
        scratch=[pltpu.VMEM((T + 8, D), F32)] * 4 + [pltpu.VMEM((T, D), F32)] * 4 + [pltpu.VMEM((T, D), BF16)] * 3
        + [pltpu.VMEM((8, D), F32)],
        aliases={15: 0}, vmem_mb=56,
    )(z, z, z, hs, hs, dyb, *lru, cw, wr, wi, lam, dz)


def _branch_c_bwd(z, kv, pw, dyc, dz):
    S = z.shape[0]
    T = min(512, S)

    def body(zq, zg, kv_r, p_r, dy_r, dz_in, dz_o, dkv_o):
        @pl.when(pl.program_id(0) == 0)
        def _():
            dkv_o[...] = jnp.zeros((MEM, 2 * D), F32)

        gc = zg[...].astype(F32)
        sil, dsil = _silu_parts(gc)
        dy = dy_r[...].astype(F32)
        datt = dy * sil
        dgc_pre = dy * dsil
        scale = HD ** -0.5
        for h in range(NH):
            cs = slice(h * HD, (h + 1) * HD)
            vs = slice(D + h * HD, D + (h + 1) * HD)
            qh = zq[:, cs]
            pb = p_r[:, cs]
            p = pb.astype(F32)
            att = _dot(pb, kv_r[:, vs])
            dz_o[1, :, cs] = (dgc_pre[:, cs] * att).astype(BF16)
            dab = datt[:, cs].astype(BF16)
            dp = _dot_nt(dab, kv_r[:, vs])
            ds = (p * (dp - jnp.sum(p * dp, axis=-1, keepdims=True)) * scale).astype(BF16)
            dz_o[0, :, cs] = _dot(ds, kv_r[:, cs]).astype(BF16)
            dkv_o[:, cs] += _dot_tn(ds, qh)
            dkv_o[:, vs] += _dot_tn(pb, dab)

    zs = lambda k: pl.BlockSpec((T, D), lambda i: (i, k))
    return _pcall(
        body, name="branch_c_bwd", grid=(S // T,),
        in_specs=[zs(5), zs(6), _full((MEM, 2 * D))] + [pl.BlockSpec((T, D), lambda i: (i, 0))] * 2 + [HBM],
        out_specs=[pl.BlockSpec((2, T, D), lambda i: (4, i, 0)), _full((MEM, 2 * D))],
        out_shape=[SDS((10, S, D), BF16), SDS((MEM, 2 * D), F32)], aliases={5: 0},
    )(z, z, kv, pw, dyc, dz)


def _mm_dh(dz, w, x, dxn, g):
    S = x.shape[0]
    tm = min(1024, S)

    def body(dz_r, w_r, x_r, dxn_r, g_r, dx_o, dg_o, acc):
        i, k = pl.program_id(0), pl.program_id(1)
        _accumulate(k == 0, acc, _dot_nt(dz_r[0], w_r[...]))

        @pl.when(k == 9)
        def _():
            dh = acc[...]
            xv = x_r[...]
            r1 = _rms_scale(xv)
            wv = dh * g_r[...]
            dx_o[...] = dxn_r[...] + r1 * wv - xv * (r1 * r1 * r1) * jnp.mean(wv * xv, axis=-1, keepdims=True)
            _accumulate(i == 0, dg_o, _rowsum(dh * xv * r1))

    row = pl.BlockSpec((tm, D), lambda i, k: (i, 0))
    return _pcall(
        body, name="mm_dh", grid=(S // tm, 10),
        in_specs=[pl.BlockSpec((1, tm, D), lambda i, k: (k, i, 0)), pl.BlockSpec((D, D), lambda i, k: (0, _dz_col(k))),
                  row, row, _full((1, D))],
        out_specs=[row, _full((1, D))], out_shape=[SDS((S, D), F32), SDS((1, D), F32)],
        scratch=[pltpu.VMEM((tm, D), F32)],
    )(dz, w, x, dxn, g)


def _mm_dwin(h, dz):
    S = h.shape[0]
    tk = min(1024, S)
    nk = S // tk

    def body(h_r, dz_r, o_r, acc):
        k = pl.program_id(1)
        _accumulate(k == 0, acc, _dot_tn(h_r[...], dz_r[0]))

        @pl.when(k == nk - 1)
        def _():
            o_r[...] = acc[...].astype(BF16)

    return _pcall(
        body, name="mm_dwin", grid=(10, nk),
        in_specs=[pl.BlockSpec((tk, D), lambda n, k: (k, 0)), pl.BlockSpec((1, tk, D), lambda n, k: (n, k, 0))],
        out_specs=pl.BlockSpec((D, D), lambda n, k: (0, _dz_col(n))), out_shape=SDS((D, NIN), BF16),
        scratch=[pltpu.VMEM((D, D), F32)],
    )(h, dz)


def _mm_tn4(a4, b4):
    S = a4[0].shape[0]
    tk = min(1024, S)
    nk = S // tk

    def body(*refs):
        a_r, b_r, o_r, acc = refs[0:4], refs[4:8], refs[8], refs[9]
        w, k = pl.program_id(0), pl.program_id(1)
        for a in range(4):
            @pl.when(w == a)
            def _(a=a):
                _accumulate(k == 0, acc, _dot_tn(a_r[a][...], b_r[a][...]))

        @pl.when(k == nk - 1)
        def _():
            o_r[...] = acc[...].astype(BF16)

    def blk(a):
        return pl.BlockSpec((tk, D), lambda w, k: (jnp.where(w == a, k, jnp.where(w < a, 0, nk - 1)), 0))

    return _pcall(body, name="mm_tn4", grid=(4, nk), in_specs=[blk(a) for a in range(4)] * 2,
                  out_specs=pl.BlockSpec((None, D, D), lambda w, k: (w, 0, 0)), out_shape=SDS((4, D, D), BF16),
                  scratch=[pltpu.VMEM((D, D), F32)])(*a4, *b4)


def _mem_bwd(mem, g, wkv, dkv, dg_acc):
    def body(m_ref, g_ref, w_ref, dkv_ref, acc_ref, dw_ref, dg_ref):
        m = m_ref[...]
        mr = m * _rms_scale(m)
        mn = (mr * g_ref[...]).astype(BF16)
        dkb = dkv_ref[...].astype(BF16)
        dw_ref[...] = _dot_tn(mn, dkb).astype(BF16)
        dg_ref[...] = acc_ref[...] + _rowsum(_dot_nt(dkb, w_ref[...]) * mr)

    return _pcall(body, name="mem_bwd", in_specs=[VMEM] * 5, out_specs=[VMEM] * 2,
                  out_shape=[SDS((D, 2 * D), BF16), SDS((1, D), F32)], vmem_mb=48)(mem, g, wkv, dkv, dg_acc)


def _adamw_math(w, g, m, v):
    m2 = ADAM_B1 * m + (1.0 - ADAM_B1) * g
    v2 = ADAM_B2 * v + (1.0 - ADAM_B2) * (g * g)
    mh = m2 / (1.0 - ADAM_B1 ** ADAM_STEP)
    vh = v2 / (1.0 - ADAM_B2 ** ADAM_STEP)
    return -ADAM_LR * (mh / (jnp.sqrt(vh) + ADAM_EPS) + ADAM_WD * w), m2, v2


def _adamw_layer(l, w, m, v, g, prev, which=None, rows=256):
    L, R, C = w.shape

    def body(w_r, m_r, v_r, g_r, *rest):
        g_o, d_o, m_o, v_o = rest[-4:]
        g = g_r[...]
        d, m2, v2 = _adamw_math(w_r[...], g, m_r[...], v_r[...])
        g_o[...] = g
        d_o[...] = d
        m_o[...] = m2
        v_o[...] = v2

    st = pl.BlockSpec((None, rows, C), lambda i: (l, i, 0))
    gs = pl.BlockSpec((rows, C), lambda i: (i, 0)) if which is None else pl.BlockSpec((None, rows, C), lambda i: (which, i, 0))
    carried = list(prev) if prev is not None else []
    return _pcall(body, name="adamw_layer", grid=(R // rows,), in_specs=[st] * 3 + [gs] + [HBM] * len(carried),
                  out_specs=[st] * 4, out_shape=[SDS(w.shape, F32)] * 4, vmem_mb=56,
                  aliases={4 + k: k for k in range(len(carried))} or None)(w, m, v, g, *carried)


def _adamw_proj(l, trios, g_p, prevs, rows=128):
    L, R, C = trios[0][0].shape

    def body(*refs):
        ins, g_r, outs = refs[0:12], refs[12], refs[-16:]
        for k in range(4):
            g = g_r[k]
            d, m2, v2 = _adamw_math(ins[3 * k][...], g, ins[3 * k + 1][...], ins[3 * k + 2][...])
            for o, val in zip(outs[4 * k:4 * k + 4], (g, d, m2, v2)):
                o[...] = val

    st = pl.BlockSpec((None, rows, C), lambda i: (l, i, 0))
    carried = [a for p in prevs for a in p] if prevs[0] is not None else []
    res = _pcall(body, name="adamw_proj", grid=(R // rows,),
                 in_specs=[st] * 12 + [pl.BlockSpec((4, rows, C), lambda i: (0, i, 0))] + [HBM] * len(carried),
                 out_specs=[st] * 16, out_shape=[SDS(trios[0][0].shape, F32)] * 16, vmem_mb=56,
                 aliases={13 + k: k for k in range(len(carried))} or None)(*[a for t in trios for a in t], g_p, *carried)
    return [res[4 * k:4 * k + 4] for k in range(4)]


def _adamw_flat(w, m, v, g, rows):
    R, C = w.shape

    def body(w_r, m_r, v_r, g_r, d_o, m_o, v_o):
        d, m2, v2 = _adamw_math(w_r[...], g_r[...], m_r[...], v_r[...])
        d_o[...] = d
        m_o[...] = m2
        v_o[...] = v2

    blk = pl.BlockSpec((rows, C), lambda i: (i, 0))
    return _pcall(body, name="adamw_flat", grid=(R // rows,), in_specs=[blk] * 4, out_specs=[blk] * 3,
                  out_shape=[SDS((R, C), F32)] * 3)(w, m, v, g)


_SMALL = ("mem_norm_g", "pre_norm_g", "post_norm_g", "gmlp_ln_g", "gmlp_ln_b", "gmlp_ws", "gmlp_bs", "conv_b", "lru_wr",
          "lru_br", "lru_wi", "lru_bi", "lru_lambda")


def _pack_small(parts, conv_w_part):
    rows = [parts[n].reshape(-1, 128) for n in _SMALL] + [conv_w_part.reshape(-1, 128)]
    used = sum(r.shape[0] for r in rows)
    rows.append(jnp.zeros((SMALL_ROWS - used, 128), F32))
    return jnp.concatenate(rows, axis=0)


def _unpack_small(pack, shapes):
    out, at = {}, 0
    for n in _SMALL:
        size = 1
        for s in shapes[n]:
            size *= s
        out[n] = pack[at:at + size // 128].reshape(shapes[n])
        at += size // 128
    return out, at


def kernel(x, mem, mem_norm_g, pre_norm_g, post_norm_g, w_in, gmlp_ln_g, gmlp_ln_b, gmlp_ws, gmlp_bs, conv_w, conv_b, lru_wr, lru_br, lru_wi, lru_bi, lru_lambda, w_kv, w_pa, w_pb, w_pc, w_out, loss_target, m_mem_norm_g, m_pre_norm_g, m_post_norm_g, m_w_in, m_gmlp_ln_g, m_gmlp_ln_b, m_gmlp_ws, m_gmlp_bs, m_conv_w, m_conv_b, m_lru_wr, m_lru_br, m_lru_wi, m_lru_bi, m_lru_lambda, m_w_kv, m_w_pa, m_w_pb, m_w_pc, m_w_out, v_mem_norm_g, v_pre_norm_g, v_post_norm_g, v_w_in, v_gmlp_ln_g, v_gmlp_ln_b, v_gmlp_ws, v_gmlp_bs, v_conv_w, v_conv_b, v_lru_wr, v_lru_br, v_lru_wi, v_lru_bi, v_lru_lambda, v_w_kv, v_w_pa, v_w_pb, v_w_pc, v_w_out):
    L = w_in.shape[0]
    S = x.shape[1]
    xs = [x[0]]
    mem2 = mem[0]
    mg = mem_norm_g.reshape(1, D)
    vec = lambda a, l: a[l].reshape(1, D)
    ci = lax.axis_index("c")
    jpos = 2 * lax.axis_index("x") + lax.axis_index("y")
    pos = jnp.reshape(jpos, (1,)).astype(jnp.int32)
    pos2 = jnp.stack([jpos, ci]).astype(jnp.int32)

    cw8 = jnp.pad(conv_w, ((0, 0), (0, 4), (0, 0)))
    given = dict(mem_norm_g=(mem_norm_g, m_mem_norm_g, v_mem_norm_g), pre_norm_g=(pre_norm_g, m_pre_norm_g, v_pre_norm_g),
                 post_norm_g=(post_norm_g, m_post_norm_g, v_post_norm_g), gmlp_ln_g=(gmlp_ln_g, m_gmlp_ln_g, v_gmlp_ln_g),
                 gmlp_ln_b=(gmlp_ln_b, m_gmlp_ln_b, v_gmlp_ln_b), gmlp_ws=(gmlp_ws, m_gmlp_ws, v_gmlp_ws),
                 gmlp_bs=(gmlp_bs, m_gmlp_bs, v_gmlp_bs), conv_b=(conv_b, m_conv_b, v_conv_b), lru_wr=(lru_wr, m_lru_wr, v_lru_wr),
                 lru_br=(lru_br, m_lru_br, v_lru_br), lru_wi=(lru_wi, m_lru_wi, v_lru_wi), lru_bi=(lru_bi, m_lru_bi, v_lru_bi),
                 lru_lambda=(lru_lambda, m_lru_lambda, v_lru_lambda))
    shapes = {n: given[n][0].shape for n in _SMALL}
    zero_cw = jnp.zeros((L, 4, D), F32)
    packs = [_pack_small({n: given[n][k] for n in _SMALL}, zero_cw) for k in range(3)]

    placed = [_cast_place(0, pos, w_in, w_kv, w_pa, w_pb, w_pc, w_out, cw8)]
    W = [None] * L
    ALL = (0, 1, 2, 3)
    first = _gather_start("gather_start_0a", (0,), placed[0][0:1])
    rest = _gather_start("gather_start_0b", (1, 2, 3), placed[0][1:4], first[-1])
    cw8 = cw8 + rest[-1][0, 0]
    placed += [_cast_place(l, pos, w_in, w_kv, w_pa, w_pb, w_pc, w_out, cw8) for l in range(1, L)]
    busy = placed[L - 1][3][0:1, 0:128] + packs[0][0:1, :] + packs[1][0:1, :] + packs[2][0:1, :]
    mid = _gather_mid("gather_mid_0a", (0,), first, busy)
    started = _gather_start("gather_start_1", ALL, placed[1], mid[-1])
    win0 = _gather_end("gather_end_0a", (0,), mid, started[-1])[0]
    zh0 = _mm_in(xs[0], vec(pre_norm_g, 0), win0)
    mid = _gather_mid("gather_mid_0b", (1, 2, 3), rest, zh0[1])
    W[0] = [win0] + list(_gather_end("gather_end_0b", (1, 2, 3), mid, mid[-1]))

    saved = []
    for l in range(L):
        Win, Wkv, Wp, Cw = W[l]
        z, h = zh0 if l == 0 else _mm_in(xs[l], vec(pre_norm_g, l), Win)
        bsb = jnp.broadcast_to(gmlp_bs[l][:, :, None], (NG, GB, GB))
        ya = _branch_a(z, vec(gmlp_ln_g, l), vec(gmlp_ln_b, l), gmlp_ws[l], bsb)
        yb, hs, *lru = _branch_b(z, Cw, vec(conv_b, l), lru_wr[l], vec(lru_br, l), lru_wi[l], vec(lru_bi, l), vec(lru_lambda, l))
        kv = _kv(mem2, mg, Wkv)
        yc, pw = _branch_c(z, kv)
        pg = vec(post_norm_g, l)
        if l + 1 < L:
            mid = _gather_mid(f"gather_mid_{l + 1}", ALL, started, yc)
            if l + 2 < L:
                started = _gather_start(f"gather_start_{l + 2}", ALL, placed[l + 2], mid[-1])
                pg = pg + started[-1][0, 0]
        pa, pb, pc, mgd, o, xn = _merge_out(ya, yb, yc, z, Wp, xs[l], pg)
        if l + 1 < L:
            W[l + 1] = _gather_end(f"gather_end_{l + 1}", ALL, mid, xn)
        xs.append(xn)
        saved.append((z, h, ya, yb, yc, hs, kv, pa, pb, pc, mgd, o, bsb, lru, pw))

    loss11, dxn = _loss_head(xs[L], loss_target[0])
    loss = lax.psum(loss11[0, 0], ("x", "y", "c"))

    big = dict(w_in=(w_in, m_w_in, v_w_in), w_kv=(w_kv, m_w_kv, v_w_kv), w_pa=(w_pa, m_w_pa, v_w_pa),
               w_pb=(w_pb, m_w_pb, v_w_pb), w_pc=(w_pc, m_w_pc, v_w_pc), w_out=(w_out, m_w_out, v_w_out))
    out = {n: None for n in big}
    kin, nsh, nkv, rp = w_in.shape[1], w_in.shape[2], w_kv.shape[2], w_pa.shape[1]

    def finish_layer(l, a2a, after):
        g_p4, lp = _a2a_wait(f"a2a_p_wait_{l}", (2,), a2a[0], after)
        g_in, g_kv, lin, lkv = _a2a_wait(f"a2a_w_wait_{l}", (0, 1), a2a[1], after)
        g_in, g_kv, g_p = _sum_share(pos2, (lin, lkv, lp.reshape(NDEV - 1, 2 * rp, D)), (g_in, g_kv, g_p4))
        g_p = g_p.reshape(4, rp, D)
        out["w_in"] = _adamw_layer(l, *big["w_in"], g_in, out["w_in"])
        out["w_kv"] = _adamw_layer(l, *big["w_kv"], g_kv, out["w_kv"])
        proj = ("w_pa", "w_pb", "w_pc", "w_out")
        for n, res in zip(proj, _adamw_proj(l, [big[n] for n in proj], g_p, [out[n] for n in proj])):
            out[n] = res

    small = {n: [None] * L for n in _SMALL}
    dconv_w = [None] * L
    dg_mem = jnp.zeros((1, D), F32)
    pending = None
    sent = []
    for l in reversed(range(L)):
        Win, Wkv, Wp, Cw = W[l]
        z, h, ya, yb, yc, hs, kv, pa, pb, pc, mgd, o, bsb, lru, pw = saved[l]
        pg = vec(post_norm_g, l) if pending is None else vec(post_norm_g, l) + pending[1][1][-1][0, 0]
        do, dpa, dpb, dpc, dya, dyb, dyc, dz, dgpost = _out_bwd(dxn, o, pg, Wp, z, pa, pb, pc)
        a2a_p = _a2a_start(f"a2a_p_start_{l}", (2,), (_mm_tn4((ya, yb, yc, mgd), (dpa, dpb, dpc, do)),),
                           (lax.empty((NDEV - 1, 4, rp // 2, D), BF16),))
        dz, dws, dbs, dlg, dlb = _branch_a_bwd(z, dya, vec(gmlp_ln_g, l) + a2a_p[-1][0, 0], vec(gmlp_ln_b, l), gmlp_ws[l], bsb, dz)
        dz, dcw, dcb, dwr, dbr, dwi, dbi, dlam = _branch_b_bwd(z, hs, lru, dyb, Cw, lru_wr[l], lru_wi[l], vec(lru_lambda, l), dz)
        dz, dkv = _branch_c_bwd(z, kv, pw, dyc, dz)
        g_in = _mm_dwin(h, dz)
        g_kv, dg_mem = _mem_bwd(mem2, mg, Wkv, dkv, dg_mem)
        a2a_w = _a2a_start(f"a2a_w_start_{l}", (0, 1), (g_in, g_kv),
                           (lax.empty((NDEV - 1, kin // 2, nsh), BF16), lax.empty((NDEV - 1, kin // 2, nkv), BF16)))
        dx, dgpre = _mm_dh(dz, Win, xs[l], dxn, vec(pre_norm_g, l) + a2a_w[-1][0, 0])
        pending = (l, (a2a_p, a2a_w))
        sent.append(pending)
        for n, val in (("pre_norm_g", dgpre), ("post_norm_g", dgpost), ("gmlp_ln_g", dlg), ("gmlp_ln_b", dlb), ("gmlp_ws", dws),
                       ("gmlp_bs", dbs), ("conv_b", dcb), ("lru_wr", dwr), ("lru_br", dbr), ("lru_wi", dwi), ("lru_bi", dbi),
                       ("lru_lambda", dlam)):
            small[n][l] = val
        dconv_w[l] = dcw[0:4]
        dxn = dx
    grad_x = dxn.reshape(1, S, D)

    parts = {n: jnp.stack(small[n]) for n in _SMALL if n != "mem_norm_g"}
    parts["mem_norm_g"] = dg_mem
    me1 = jnp.reshape(2 * jpos + ci, (1,)).astype(jnp.int32)
    rs = _rs_start(_pack_small(parts, jnp.stack(dconv_w)), lax.empty((NDEV - 1, SMALL_ROWS // NDEV, 128), F32))
    for l, a2a in sent[:-1]:
        finish_layer(l, a2a, rs[-1])
    pack, land = _rs_wait(rs, out["w_out"][1])
    ag = _ag_start(_small_sum(me1, pack, land))
    finish_layer(pending[0], pending[1], ag[-1])
    gsum = _ag_wait(ag, out["w_out"][1])
    dsm, msm, vsm = _adamw_flat(packs[0], packs[1], packs[2], gsum, 2560)
    g_small, at = _unpack_small(gsum, shapes)
    d_small, _ = _unpack_small(dsm, shapes)
    m_small, _ = _unpack_small(msm, shapes)
    v_small, _ = _unpack_small(vsm, shapes)
    for n in _SMALL:
        out[n] = (g_small[n], d_small[n], m_small[n], v_small[n])
    g_cw = lax.dynamic_slice_in_dim(gsum[at:at + L * 4 * D // 128].reshape(L * 4, D), jpos * (D // 4), D // 4, axis=1)
    d_cw, m_cw, v_cw = _adamw_flat(conv_w.reshape(L * 4, D // 4), m_conv_w.reshape(L * 4, D // 4),
                                   v_conv_w.reshape(L * 4, D // 4), g_cw, L * 4)
    out["conv_w"] = tuple(a.reshape(L, 4, D // 4) for a in (g_cw, d_cw, m_cw, v_cw))

    order = ("mem_norm_g", "pre_norm_g", "post_norm_g", "w_in", "gmlp_ln_g", "gmlp_ln_b", "gmlp_ws", "gmlp_bs", "conv_w", "conv_b",
             "lru_wr", "lru_br", "lru_wi", "lru_bi", "lru_lambda", "w_kv", "w_pa", "w_pb", "w_pc", "w_out")
    return (loss, grad_x, *[out[n][0] for n in order], *[out[n][1] for n in order], *[out[n][2] for n in order],
            *[out[n][3] for n in order])
```

```python
import functools

import jax
import jax.numpy as jnp
from jax import lax
from jax.experimental import pallas as pl
from jax.experimental.pallas import tpu as pltpu

F32 = jnp.float32
BF16 = jnp.bfloat16
SDS = jax.ShapeDtypeStruct
MESH = pl.DeviceIdType.MESH

D = 1024
NIN = 10 * D
MEM = 256
GB = 128
NG = 8
NH = 4
HD = D // NH
EPS = 1e-6
LRU_C = 8.0
ADAM_LR, ADAM_B1, ADAM_B2, ADAM_EPS, ADAM_WD, ADAM_STEP = 0.001, 0.9, 0.999, 1e-08, 0.01, 10
NDEV = 8
SMALL_ROWS = 12800

_CALL_KW = {}
HBM = pl.BlockSpec(memory_space=pltpu.HBM)
VMEM = pl.BlockSpec(memory_space=pltpu.VMEM)
SEM = pl.BlockSpec(memory_space=pltpu.SEMAPHORE)
ANY = pl.BlockSpec(memory_space=pl.ANY)
TOKEN = SDS((8, 128), F32)


def _pcall(body, *, name, in_specs, out_specs, out_shape, grid=None, scratch=(), vmem_mb=48, aliases=None, effect=False,
           prefetch=0):
    kw = dict(_CALL_KW)
    if aliases:
        kw["input_output_aliases"] = aliases
    params = dict(vmem_limit_bytes=vmem_mb << 20)
    if grid is not None:
        params["dimension_semantics"] = ("arbitrary",) * len(grid)
    if effect:
        params["has_side_effects"] = pltpu.SideEffectType.DATAFLOW_SIDE_EFFECTING
    if prefetch:
        kw["grid_spec"] = pltpu.PrefetchScalarGridSpec(num_scalar_prefetch=prefetch, grid=grid, in_specs=in_specs,
                                                       out_specs=out_specs, scratch_shapes=list(scratch))
    else:
        kw.update(in_specs=in_specs, out_specs=out_specs, scratch_shapes=list(scratch))
        if grid is not None:
            kw["grid"] = grid
    return pl.pallas_call(body, name=name, out_shape=out_shape, compiler_params=pltpu.CompilerParams(**params), **kw)


def _full(shape):
    nd = len(shape)
    return pl.BlockSpec(shape, lambda *_: (0,) * nd)


def _dot(a, b):
    return jnp.dot(a, b, preferred_element_type=F32)


def _dot_nt(a, b):
    return lax.dot_general(a, b, (((1,), (1,)), ((), ())), preferred_element_type=F32)


def _dot_tn(a, b):
    return lax.dot_general(a, b, (((0,), (0,)), ((), ())), preferred_element_type=F32)


def _rowsum(a):
    return jnp.sum(a, axis=0, keepdims=True)


def _sigmoid(x):
    return 0.5 * jnp.tanh(0.5 * x) + 0.5


def _silu_parts(g):
    s = _sigmoid(g)
    return g * s, s * (1.0 + g * (1.0 - s))


def _rms_scale(x):
    return lax.rsqrt(jnp.mean(x * x, axis=-1, keepdims=True) + EPS)


def _dz_col(k):
    return jnp.where(k < 3, k, jnp.where(k < 6, k + 4, k - 3))


def _coords():
    return lax.axis_index("x"), lax.axis_index("y"), lax.axis_index("c")


def _other_chips(x, y):
    return [(1 - x, y), (x, 1 - y), (1 - x, 1 - y)]


def _peer(x, y, c, mask):
    return (1 - x if mask & 4 else x, 1 - y if mask & 2 else y, 1 - c if mask & 1 else c)


def _remote(src, dst, ssem, rsem, k, to):
    return pltpu.make_async_remote_copy(src_ref=src, dst_ref=dst, send_sem=ssem.at[k], recv_sem=rsem.at[k], device_id=to,
                                        device_id_type=MESH)


def _w_half(a, ref, jj, cc):
    if a == 2:
        rp = ref.shape[1] // 4
        return ref.at[:, pl.ds(jj * rp + cc * (rp // 2), rp // 2), :]
    kin, nsh = ref.shape[0], ref.shape[1] // 4
    return ref.at[pl.ds(cc * (kin // 2), kin // 2), pl.ds(jj * nsh, nsh)]


def _cw_block(ref, jj):
    return ref.at[:, pl.ds(jj * (D // 4), D // 4)]


def _cast_place(l, pos, w_in, w_kv, w_pa, w_pb, w_pc, w_out, cw8):
    kin, nsh = w_in.shape[1], w_in.shape[2]
    nkv, rp = w_kv.shape[2], w_pa.shape[1]
    half = kin // 2

    def body(pos_r, win, wkv, pa, pb, pc, po, cw, Win, Wkv, Wp, Cw):
        Win[...] = win[...].astype(BF16)
        Wkv[...] = wkv[...].astype(BF16)

        @pl.when(pl.program_id(0) == 0)
        def _():
            for k, r in enumerate((pa, pb, pc, po)):
                Wp[k] = r[...].astype(BF16)
            Cw[...] = cw[...]

    proj = pl.BlockSpec((None, rp, D), lambda i, p: (l, 0, 0))
    return _pcall(
        body, name="cast_place", grid=(2,), prefetch=1,
        in_specs=[pl.BlockSpec((None, half, nsh), lambda i, p: (l, i, 0)), pl.BlockSpec((None, half, nkv), lambda i, p: (l, i, 0)),
                  proj, proj, proj, proj, pl.BlockSpec((None, 8, D // 4), lambda i, p: (l, 0, 0))],
        out_specs=[pl.BlockSpec((half, nsh), lambda i, p: (i, p[0])), pl.BlockSpec((half, nkv), lambda i, p: (i, p[0])),
                   pl.BlockSpec((4, rp, D), lambda i, p: (0, p[0], 0)), pl.BlockSpec((8, D // 4), lambda i, p: (0, p[0]))],
        out_shape=[SDS((kin, 4 * nsh), BF16), SDS((kin, 4 * nkv), BF16), SDS((4, 4 * rp, D), BF16), SDS((8, D), F32)],
    )(pos, w_in, w_kv, w_pa, w_pb, w_pc, w_out, cw8)


def _hbm_like(bufs):
    return [pltpu.HBM(b.shape, b.dtype) for b in bufs]


def _w_part(a, ref, jj, cc):
    return _cw_block(ref, jj) if a == 3 else _w_half(a, ref, jj, cc)


def _gather_start(name, kinds, bufs, after=None):
    n = len(kinds)
    extra = [] if after is None else [after]

    def body(*refs):
        w, ssem, rsem, token = refs[0:n], refs[n + len(extra)], refs[n + len(extra) + 1], refs[-1]
        x, y, c = _coords()
        j = 2 * x + y
        for k, chip in enumerate(_other_chips(x, y)):
            for i, a in enumerate(kinds):
                part = _w_part(a, w[i], j, c)
                _remote(part, part, ssem, rsem, i * 3 + k, (chip[0], chip[1], c)).start()
        token[...] = jnp.zeros((8, 128), F32)

    return _pcall(
        body, name=name, in_specs=[HBM] * n + [ANY] * len(extra), out_specs=[SEM, SEM] + [HBM] * n + [VMEM],
        out_shape=[pltpu.SemaphoreType.DMA((3 * n,)), pltpu.SemaphoreType.DMA((3 * n,))] + _hbm_like(bufs) + [TOKEN],
        aliases={i: 2 + i for i in range(n)}, effect=True,
    )(*[pltpu.with_memory_space_constraint(b, pltpu.HBM) for b in bufs], *extra)


def _gather_mid(name, kinds, started, after):
    n = len(kinds)
    fw = [i for i, a in enumerate(kinds) if a != 3]
    bufs = tuple(started[2:2 + n])

    def body(*refs):
        w, ssem, rsem, ssem2, rsem2, token = refs[0:n], refs[n], refs[n + 1], refs[n + 3], refs[n + 4], refs[-1]
        x, y, c = _coords()
        j = 2 * x + y
        me, sib = (x, y, c), (x, y, 1 - c)
        token[...] = jnp.zeros((8, 128), F32)
        chips = _other_chips(x, y)
        for k, chip in enumerate(chips):
            for i, a in enumerate(kinds):
                got = _w_part(a, w[i], 2 * chip[0] + chip[1], c)
                _remote(got, got, ssem, rsem, i * 3 + k, me).wait_recv()
        for k in range(3):
            for i, a in enumerate(kinds):
                part = _w_part(a, w[i], j, c)
                _remote(part, part, ssem, rsem, i * 3 + k, me).wait_send()
        for k, chip in enumerate(chips):
            for f, i in enumerate(fw):
                got = _w_half(kinds[i], w[i], 2 * chip[0] + chip[1], c)
                _remote(got, got, ssem2, rsem2, f * 3 + k, sib).start()

    return _pcall(
        body, name=name, in_specs=[HBM] * n + [SEM, SEM, ANY], out_specs=[SEM, SEM] + [HBM] * n + [VMEM],
        out_shape=[pltpu.SemaphoreType.DMA((3 * len(fw),)), pltpu.SemaphoreType.DMA((3 * len(fw),))] + _hbm_like(bufs) + [TOKEN],
        aliases={i: 2 + i for i in range(n)}, effect=True,
    )(*bufs, started[0], started[1], after)


def _gather_end(name, kinds, mid, after):
    n = len(kinds)
    fw = [i for i, a in enumerate(kinds) if a != 3]
    bufs = tuple(mid[2:2 + n])

    def body(*refs):
        w, ssem2, rsem2 = refs[0:n], refs[n], refs[n + 1]
        x, y, c = _coords()
        me = (x, y, c)
        for k, chip in enumerate(_other_chips(x, y)):
            for f, i in enumerate(fw):
                got = _w_half(kinds[i], w[i], 2 * chip[0] + chip[1], 1 - c)
                _remote(got, got, ssem2, rsem2, f * 3 + k, me).wait_recv()
                sent = _w_half(kinds[i], w[i], 2 * chip[0] + chip[1], c)
                _remote(sent, sent, ssem2, rsem2, f * 3 + k, me).wait_send()

    return _pcall(
        body, name=name, in_specs=[HBM] * n + [SEM, SEM, ANY], out_specs=[HBM] * n, out_shape=_hbm_like(bufs),
        aliases={i: i for i in range(n)}, effect=True,
    )(*bufs, mid[0], mid[1], after)


def _g_piece(a, ref, jd, dc):
    if a == 2:
        rp = ref.shape[1] // 4
        return ref.at[:, pl.ds(jd * rp + dc * (rp // 2), rp // 2), :]
    kin, nsh = ref.shape[0], ref.shape[1] // 4
    return ref.at[pl.ds(dc * (kin // 2), kin // 2), pl.ds(jd * nsh, nsh)]


def _a2a_start(name, kinds, grads, lands):
    n = len(kinds)

    def body(*refs):
        g, ld, ssem, rsem, token = refs[0:n], refs[n:2 * n], refs[2 * n], refs[2 * n + 1], refs[-1]
        x, y, c = _coords()
        for mask in range(1, NDEV):
            p = _peer(x, y, c, mask)
            for i, a in enumerate(kinds):
                _remote(_g_piece(a, g[i], 2 * p[0] + p[1], p[2]), ld[i].at[mask - 1], ssem, rsem, i * 7 + mask - 1, p).start()
        token[...] = jnp.zeros((8, 128), F32)

    bufs = tuple(grads) + tuple(lands)
    return _pcall(
        body, name=name, in_specs=[HBM] * (2 * n), out_specs=[SEM, SEM] + [HBM] * (2 * n) + [VMEM],
        out_shape=[pltpu.SemaphoreType.DMA((7 * n,)), pltpu.SemaphoreType.DMA((7 * n,))] + _hbm_like(bufs) + [TOKEN],
        aliases={i: 2 + i for i in range(2 * n)}, effect=True,
    )(*[pltpu.with_memory_space_constraint(b, pltpu.HBM) for b in bufs])


def _a2a_wait(name, kinds, started, after):
    n = len(kinds)
    ssem, rsem = started[0], started[1]
    bufs = tuple(started[2:2 + 2 * n])

    def body(*refs):
        g, ld, ssem, rsem = refs[0:n], refs[n:2 * n], refs[2 * n], refs[2 * n + 1]
        x, y, c = _coords()
        me = (x, y, c)
        for mask in range(1, NDEV):
            for i in range(n):
                got = ld[i].at[mask - 1]
                _remote(got, got, ssem, rsem, i * 7 + mask - 1, me).wait_recv()
        for mask in range(1, NDEV):
            p = _peer(x, y, c, mask)
            for i, a in enumerate(kinds):
                sent = _g_piece(a, g[i], 2 * p[0] + p[1], p[2])
                _remote(sent, sent, ssem, rsem, i * 7 + mask - 1, me).wait_send()

    return _pcall(
        body, name=name, in_specs=[HBM] * (2 * n) + [SEM, SEM, ANY], out_specs=[HBM] * (2 * n), out_shape=_hbm_like(bufs),
        aliases={i: i for i in range(2 * n)}, effect=True,
    )(*bufs, ssem, rsem, after)


def _sum_share(pos, lands, grads):
    rows, n = 128, 4
    widths = [ld.shape[2] for ld in lands]

    def body(pos_r, l0, w0, l1, w1, l2, w2, g0, g1, g2, b0, b1, b2, lsem, ssem, rsem):
        i = pl.program_id(0)
        x, y, c = _coords()
        sib = (x, y, 1 - c)
        ld, ow, gs, bufs = (l0, l1, l2), (w0, w1, w2), (g0, g1, g2), (b0, b1, b2)

        def dst(a, step):
            row = step * (2 * rows) + c * rows if a == 2 else c * (n * rows) + step * rows
            return gs[a].at[pl.ds(row, rows), :]

        def copies(a, step, sl):
            src = bufs[a].at[sl]
            lc = pltpu.make_async_copy(src, dst(a, step), lsem.at[a, sl])
            rc = pltpu.make_async_remote_copy(src_ref=src, dst_ref=dst(a, step), send_sem=ssem.at[a, sl], recv_sem=rsem.at[a],
                                              device_id=sib, device_id_type=MESH)
            return lc, rc

        def drain(a, step, sl):
            lc, rc = copies(a, step, sl)
            lc.wait()
            rc.wait_send()

        slot = i % 2

        @pl.when(i >= 2)
        def _():
            for a in range(3):
                drain(a, i - 2, slot)

        for a in range(3):
            acc = ow[a][...].astype(F32)
            for k in range(NDEV - 1):
                acc = acc + ld[a][k].astype(F32)
            bufs[a][slot] = acc
            lc, rc = copies(a, i, slot)
            lc.start()
            rc.start()

        @pl.when(i == n - 1)
        def _():
            for a in range(3):
                drain(a, n - 2, (n - 2) % 2)
                drain(a, n - 1, (n - 1) % 2)
                whole = gs[a].at[pl.ds(0, n * rows), :]
                pltpu.make_async_remote_copy(src_ref=whole, dst_ref=whole, send_sem=ssem.at[a, 0], recv_sem=rsem.at[a],
                                             device_id=(x, y, c), device_id_type=MESH).wait_recv()

    land = lambda w: pl.BlockSpec((NDEV - 1, rows, w), lambda i, p: (0, i, 0))
    in_specs = [land(widths[0]), pl.BlockSpec((rows, widths[0]), lambda i, p: (p[1] * n + i, p[0])),
                land(widths[1]), pl.BlockSpec((rows, widths[1]), lambda i, p: (p[1] * n + i, p[0])),
                land(widths[2]), pl.BlockSpec((None, rows, widths[2]), lambda i, p: (i, 2 * p[0] + p[1], 0))]
    args = [t for pair in zip(lands, grads) for t in pair]
    return _pcall(
        body, name="sum_share", grid=(n,), prefetch=1, in_specs=in_specs, out_specs=[HBM] * 3,
        out_shape=[SDS((2 * n * rows, w), F32) for w in widths],
        scratch=[pltpu.VMEM((2, rows, w), F32) for w in widths]
        + [pltpu.SemaphoreType.DMA((3, 2)), pltpu.SemaphoreType.DMA((3, 2)), pltpu.SemaphoreType.DMA((3,))],
    )(pos, *args)


def _dev_index(p):
    return 4 * p[0] + 2 * p[1] + p[2]


def _small_rows(ref, d):
    r8 = ref.shape[0] // NDEV
    return ref.at[pl.ds(d * r8, r8), :]


def _rs_start(pack, land):
    def body(p_ref, ld, ssem, rsem, o0, o1, token):
        x, y, c = _coords()
        for mask in range(1, NDEV):
            p = _peer(x, y, c, mask)
            _remote(_small_rows(p_ref, _dev_index(p)), ld.at[mask - 1], ssem, rsem, mask - 1, p).start()
        token[...] = jnp.zeros((8, 128), F32)

    bufs = (pack, land)
    return _pcall(
        body, name="rs_start", in_specs=[HBM] * 2, out_specs=[SEM, SEM, HBM, HBM, VMEM],
        out_shape=[pltpu.SemaphoreType.DMA((NDEV - 1,)), pltpu.SemaphoreType.DMA((NDEV - 1,))] + _hbm_like(bufs) + [TOKEN],
        aliases={0: 2, 1: 3}, effect=True,
    )(*[pltpu.with_memory_space_constraint(b, pltpu.HBM) for b in bufs])


def _rs_wait(started, after):
    ssem, rsem, pack, land, _ = started

    def body(p_ref, ld, ssem, rsem, after_r, o0, o1):
        x, y, c = _coords()
        for mask in range(1, NDEV):
            got = ld.at[mask - 1]
            _remote(got, got, ssem, rsem, mask - 1, (x, y, c)).wait_recv()
        for mask in range(1, NDEV):
            sent = _small_rows(p_ref, _dev_index(_peer(x, y, c, mask)))
            _remote(sent, sent, ssem, rsem, mask - 1, (x, y, c)).wait_send()

    return _pcall(body, name="rs_wait", in_specs=[HBM, HBM, SEM, SEM, ANY], out_specs=[HBM, HBM],
                  out_shape=_hbm_like((pack, land)), aliases={0: 0, 1: 1}, effect=True)(pack, land, ssem, rsem, after)


def _small_sum(me1, pack, land):
    R = pack.shape[0]
    r8 = R // NDEV

    def body(me_r, p_ref, ld, full):
        acc = p_ref[...]
        for k in range(NDEV - 1):
            acc = acc + ld[k]
        full[...] = acc

    own = pl.BlockSpec((r8, 128), lambda i, m: (m[0], 0))
    return _pcall(body, name="small_sum", grid=(1,), prefetch=1,
                  in_specs=[own, pl.BlockSpec((NDEV - 1, r8, 128), lambda i, m: (0, 0, 0))], out_specs=own,
                  out_shape=SDS((R, 128), F32), vmem_mb=32)(me1, pack, land)


def _ag_start(full):
    def body(f_ref, ssem, rsem, o0, token):
        x, y, c = _coords()
        mine = _small_rows(f_ref, _dev_index((x, y, c)))
        for mask in range(1, NDEV):
            _remote(mine, mine, ssem, rsem, mask - 1, _peer(x, y, c, mask)).start()
        token[...] = jnp.zeros((8, 128), F32)

    return _pcall(
        body, name="ag_start", in_specs=[HBM], out_specs=[SEM, SEM, HBM, VMEM],
        out_shape=[pltpu.SemaphoreType.DMA((NDEV - 1,)), pltpu.SemaphoreType.DMA((NDEV - 1,))] + _hbm_like((full,)) + [TOKEN],
        aliases={0: 2}, effect=True,
    )(pltpu.with_memory_space_constraint(full, pltpu.HBM))


def _ag_wait(started, after):
    ssem, rsem, full, _ = started

    def body(f_ref, ssem, rsem, after_r, o0):
        x, y, c = _coords()
        mine = _small_rows(f_ref, _dev_index((x, y, c)))
        for mask in range(1, NDEV):
            got = _small_rows(f_ref, _dev_index(_peer(x, y, c, mask)))
            _remote(got, got, ssem, rsem, mask - 1, (x, y, c)).wait_recv()
            _remote(mine, mine, ssem, rsem, mask - 1, (x, y, c)).wait_send()

    return _pcall(body, name="ag_wait", in_specs=[HBM, SEM, SEM, ANY], out_specs=[HBM], out_shape=_hbm_like((full,)),
                  aliases={0: 0}, effect=True)(full, ssem, rsem, after)[0]


def _mm_in(x, g, w):
    S = x.shape[0]
    tm, tn = min(1024, S), 1280

    def body(x_ref, g_ref, w_ref, z_ref, h_ref, hs):
        @pl.when(pl.program_id(1) == 0)
        def _():
            xv = x_ref[...]
            hb = (xv * _rms_scale(xv) * g_ref[...]).astype(BF16)
            hs[...] = hb
            h_ref[...] = hb

        z_ref[...] = _dot(hs[...], w_ref[...]).astype(BF16)

    return _pcall(
        body, name="mm_in", grid=(S // tm, NIN // tn),
        in_specs=[pl.BlockSpec((tm, D), lambda i, j: (i, 0)), _full((1, D)), pl.BlockSpec((D, tn), lambda i, j: (0, j))],
        out_specs=[pl.BlockSpec((tm, tn), lambda i, j: (i, j)), pl.BlockSpec((tm, D), lambda i, j: (i, 0))],
        out_shape=[SDS((S, NIN), BF16), SDS((S, D), BF16)], scratch=[pltpu.VMEM((tm, D), BF16)],
    )(x, g, w)


def _chunk_mask():
    ri = lax.broadcasted_iota(jnp.int32, (GB, GB), 0)
    ci = lax.broadcasted_iota(jnp.int32, (GB, GB), 1)
    return (ri >= 64) | (ci < 64)


def _layernorm_parts(v):
    mu = jnp.mean(v, axis=-1, keepdims=True)
    d = v - mu
    rs = lax.rsqrt(jnp.mean(d * d, axis=-1, keepdims=True) + EPS)
    return d * rs, rs


def _branch_a(z, lg, lb, ws, bsb):
    S = z.shape[0]
    T = min(512, S)

    def body(zu, zv, zg, lg_r, lb_r, ws_r, bs_r, ya):
        vhat, _ = _layernorm_parts(zv[...].astype(F32))
        vnb = (vhat * lg_r[...] + lb_r[...]).astype(BF16)
        sil, _ = _silu_parts(zg[...].astype(F32))
        t = zu[...].astype(F32) * sil
        mask = _chunk_mask()
        for g in range(NG):
            wg = jnp.where(mask, ws_r[g], 0.0).astype(BF16)
            cs = slice(g * GB, (g + 1) * GB)
            for n in range(T // GB):
                rs = slice(n * GB, (n + 1) * GB)
                sv = _dot(wg, vnb[rs, cs]) + bs_r[g]
                ya[rs, cs] = (t[rs, cs] * sv).astype(BF16)

    zs = lambda k: pl.BlockSpec((T, D), lambda i: (i, k))
    return _pcall(
        body, name="branch_a", grid=(S // T,),
        in_specs=[zs(0), zs(1), zs(2), _full((1, D)), _full((1, D)), _full((NG, GB, GB)), _full((NG, GB, GB))],
        out_specs=pl.BlockSpec((T, D), lambda i: (i, 0)), out_shape=SDS((S, D), BF16),
    )(z, z, z, lg, lb, ws, bsb)


def _softplus_neg(lam):
    e = jnp.exp(-jnp.abs(lam))
    l1p = jnp.where(e < 1e-2, e * (1.0 - e * (0.5 - e * (1.0 / 3.0))), jnp.log(1.0 + e))
    return jnp.maximum(-lam, 0.0) + l1p


CH = 16


def _ck(c, off=0):
    return pl.ds(c * CH + off, CH)


def _half_sum(v):
    return v[0:8, :] + v[8:16, :]


def _lru_gates(pr, pi, br, bi, sp8):
    r = jax.nn.sigmoid(pr + br)
    ig = _sigmoid(pi + bi)
    la = sp8 * r
    a = jnp.exp(la)
    a2 = a * a
    mult = jnp.sqrt(-jnp.tanh(la) * (a2 + 1.0))
    return r, ig, a, a2, mult


def _tile_rows():
    return lax.broadcasted_iota(jnp.int32, (8, D), 0)


def _scan_forward(a_s, u_s, h_s, hcar, T):
    row = _tile_rows()

    def tile(i, hp):
        o = pl.multiple_of(i * 8, 8)
        A = a_s[pl.ds(o, 8), :]
        U = u_s[pl.ds(o, 8), :]
        for s in (1, 2, 4):
            m = row >= s
            U = jnp.where(m, U + A * pltpu.roll(U, s, 0), U)
            A = jnp.where(m, A * pltpu.roll(A, s, 0), A)
        H = U + A * hp
        h_s[pl.ds(o, 8), :] = H
        return jnp.broadcast_to(H[7:8, :], (8, D))

    hcar[...] = lax.fori_loop(0, T // 8, tile, hcar[...])


def _scan_reverse(b_s, d_s, l_s, lcar, T):
    row = _tile_rows()
    n = T // 8

    def tile(i, lp):
        o = pl.multiple_of((n - 1 - i) * 8, 8)
        B = b_s[pl.ds(o, 8), :]
        U = d_s[pl.ds(o, 8), :]
        for s in (1, 2, 4):
            m = row < 8 - s
            U = jnp.where(m, U + B * pltpu.roll(U, 8 - s, 0), U)
            B = jnp.where(m, B * pltpu.roll(B, 8 - s, 0), B)
        Lm = U + B * lp
        l_s[pl.ds(o, 8), :] = Lm
        return jnp.broadcast_to(Lm[0:1, :], (8, D))

    lcar[...] = lax.fori_loop(0, n, tile, lcar[...])


def _branch_b(z, cw, cb, wr, br, wi, bi, lam):
    S = z.shape[0]
    T = min(256, S)

    def body(zxb, zgb, cw_r, cb_r, wr_r, br_r, wi_r, bi_r, lam_r, yb, hs_o, xc_o, r_o, ig_o, a_o, m_o, xpad, u_s, hcar):
        @pl.when(pl.program_id(0) == 0)
        def _():
            xpad[pl.ds(0, 8), :] = jnp.zeros((8, D), F32)
            hcar[...] = jnp.zeros((8, D), F32)

        cw = cw_r[...]
        xpad[pl.ds(8, T), :] = zxb[...].astype(F32)
        xk = [xpad[pl.ds(5 + k, T), :] for k in range(4)]
        xc = cb_r[...] + (((xk[0] * cw[0:1] + xk[1] * cw[1:2]) + xk[2] * cw[2:3]) + xk[3] * cw[3:4])
        xcb = xc.astype(BF16)
        pr, pi = [], []
        for h in range(NG):
            cs = slice(h * GB, (h + 1) * GB)
            pr.append(_dot(xcb[:, cs], wr_r[h].astype(BF16)))
            pi.append(_dot(xcb[:, cs], wi_r[h].astype(BF16)))
        r, ig, a, _, mult = _lru_gates(jnp.concatenate(pr, axis=1), jnp.concatenate(pi, axis=1), br_r[...], bi_r[...],
                                       -LRU_C * _softplus_neg(lam_r[...]))
        for o_ref, val in ((xc_o, xc), (r_o, r), (ig_o, ig), (a_o, a), (m_o, mult)):
            o_ref[...] = val
        u_s[...] = mult * (ig * xc)
        _scan_forward(a_o, u_s, hs_o, hcar, T)
        xpad[pl.ds(0, 8), :] = xpad[pl.ds(T, 8), :]
        sil, _ = _silu_parts(zgb[...].astype(F32))
        yb[...] = (hs_o[...] * sil).astype(BF16)

    zs = lambda k: pl.BlockSpec((T, D), lambda i: (i, k))
    row = pl.BlockSpec((T, D), lambda i: (i, 0))
    return _pcall(
        body, name="branch_b", grid=(S // T,),
        in_specs=[zs(3), zs(4), _full((8, D)), _full((1, D)), _full((NG, GB, GB)), _full((1, D)), _full((NG, GB, GB)),
                  _full((1, D)), _full((1, D))],
        out_specs=[row] * 7, out_shape=[SDS((S, D), BF16)] + [SDS((S, D), F32)] * 6,
        scratch=[pltpu.VMEM((T + 8, D), F32), pltpu.VMEM((T, D), F32), pltpu.VMEM((8, D), F32)],
    )(z, z, cw, cb, wr, br, wi, bi, lam)


def _kv(mem, g, wkv):
    def body(m_ref, g_ref, w_ref, kv_ref):
        m = m_ref[...]
        mn = (m * _rms_scale(m) * g_ref[...]).astype(BF16)
        kv_ref[...] = _dot(mn, w_ref[...]).astype(BF16)

    return _pcall(body, name="mem_kv", in_specs=[VMEM] * 3, out_specs=VMEM, out_shape=SDS((MEM, 2 * D), BF16),
                  vmem_mb=32)(mem, g, wkv)


def _softmax_rows(s):
    e = jnp.exp(s - jnp.max(s, axis=-1, keepdims=True))
    return e / jnp.sum(e, axis=-1, keepdims=True)


def _branch_c(z, kv):
    S = z.shape[0]
    T = min(512, S)

    def body(zq, zg, kv_r, yc, p_o):
        sil, _ = _silu_parts(zg[...].astype(F32))
        for h in range(NH):
            cs = slice(h * HD, (h + 1) * HD)
            pb = _softmax_rows(_dot_nt(zq[:, cs], kv_r[:, cs]) * (HD ** -0.5)).astype(BF16)
            p_o[:, cs] = pb
            att = _dot(pb, kv_r[:, D + h * HD:D + (h + 1) * HD])
            yc[:, cs] = (att * sil[:, cs]).astype(BF16)

    zs = lambda k: pl.BlockSpec((T, D), lambda i: (i, k))
    row = pl.BlockSpec((T, D), lambda i: (i, 0))
    return _pcall(body, name="branch_c", grid=(S // T,), in_specs=[zs(5), zs(6), _full((MEM, 2 * D))],
                  out_specs=[row, row], out_shape=[SDS((S, D), BF16)] * 2)(z, z, kv)


def _merge_out(ya, yb, yc, z, wp, x, pg):
    S = x.shape[0]
    T = min(256, S)

    def body(ya_r, yb_r, yc_r, m0, m1, m2, wp_r, x_r, pg_r, pa_o, pb_o, pc_o, mg_o, o_o, xn_o):
        merged = None
        for y_r, ml, p_o, k in ((ya_r, m0, pa_o, 0), (yb_r, m1, pb_o, 1), (yc_r, m2, pc_o, 2)):
            p = _dot(y_r[...], wp_r[k])
            p_o[...] = p.astype(BF16)
            t = _sigmoid(ml[...].astype(F32)) * p
            merged = t if merged is None else merged + t
        mb = merged.astype(BF16)
        mg_o[...] = mb
        o = _dot(mb, wp_r[3])
        o_o[...] = o.astype(BF16)
        xn_o[...] = x_r[...] + o * _rms_scale(o) * pg_r[...]

    row = pl.BlockSpec((T, D), lambda i: (i, 0))
    zs = lambda k: pl.BlockSpec((T, D), lambda i: (i, k))
    return _pcall(
        body, name="merge_out", grid=(S // T,),
        in_specs=[row, row, row, zs(7), zs(8), zs(9), _full((4, D, D)), row, _full((1, D))],
        out_specs=[row] * 6, out_shape=[SDS((S, D), BF16)] * 5 + [SDS((S, D), F32)], vmem_mb=56,
    )(ya, yb, yc, z, z, z, wp, x, pg)


def _loss_head(y, t):
    S = y.shape[0]
    T = min(512, S)

    def body(y_r, t_r, loss_o, dy_o):
        @pl.when(pl.program_id(0) == 0)
        def _():
            loss_o[...] = jnp.zeros((1, 1), F32)

        e = y_r[...] - t_r[...]
        dy_o[...] = e * (1.0 / D)
        loss_o[...] += 0.5 * _rowsum(jnp.sum(e * e, axis=1, keepdims=True) * (1.0 / D))

    row = pl.BlockSpec((T, D), lambda i: (i, 0))
    return _pcall(body, name="loss_head", grid=(S // T,), in_specs=[row, row], out_specs=[_full((1, 1)), row],
                  out_shape=[SDS((1, 1), F32), SDS((S, D), F32)])(y, t)


def _accumulate(first, ref, val):
    @pl.when(first)
    def _():
        ref[...] = val

    @pl.when(jnp.logical_not(first))
    def _():
        ref[...] += val


def _out_bwd(dxn, o, pg, wp, z, pa, pb, pc):
    S = dxn.shape[0]
    T = min(256, S)

    def body(dy_r, o_r, pg_r, wp_r, m0, m1, m2, pa_r, pb_r, pc_r, do_o, dpa_o, dpb_o, dpc_o, dya_o, dyb_o, dyc_o, dz_o, dg_o):
        dy = dy_r[...]
        o = o_r[...].astype(F32)
        r2 = _rms_scale(o)
        w = dy * pg_r[...]
        do = r2 * w - o * (r2 * r2 * r2) * jnp.mean(w * o, axis=-1, keepdims=True)
        _accumulate(pl.program_id(0) == 0, dg_o, _rowsum(dy * o * r2))
        dob = do.astype(BF16)
        do_o[...] = dob
        dm = _dot_nt(dob, wp_r[3])
        for k, (ml, p_r, dp_o, dy_o) in enumerate(((m0, pa_r, dpa_o, dya_o), (m1, pb_r, dpb_o, dyb_o), (m2, pc_r, dpc_o, dyc_o))):
            gk = _sigmoid(ml[...].astype(F32))
            dz_o[k] = (dm * p_r[...].astype(F32) * gk * (1.0 - gk)).astype(BF16)
            dpk = (gk * dm).astype(BF16)
            dp_o[...] = dpk
            dy_o[...] = _dot_nt(dpk, wp_r[k]).astype(BF16)

    row = pl.BlockSpec((T, D), lambda i: (i, 0))
    zs = lambda k: pl.BlockSpec((T, D), lambda i: (i, k))
    return _pcall(
        body, name="out_bwd", grid=(S // T,),
        in_specs=[row, row, _full((1, D)), _full((4, D, D)), zs(7), zs(8), zs(9), row, row, row],
        out_specs=[row] * 7 + [pl.BlockSpec((3, T, D), lambda i: (1, i, 0)), _full((1, D))],
        out_shape=[SDS((S, D), BF16)] * 7 + [SDS((10, S, D), BF16), SDS((1, D), F32)], vmem_mb=56,
    )(dxn, o, pg, wp, z, z, z, pa, pb, pc)


def _branch_a_bwd(z, dya, lg, lb, ws, bsb, dz):
    S = z.shape[0]
    T = min(512, S)
    nblk = S // T

    def body(zu, zv, zg, dy_r, lg_r, lb_r, ws_r, bs_r, dz_in, dz_o, dws_o, dbs_o, dlg_o, dlb_o, dvn_s, bacc):
        i = pl.program_id(0)

        @pl.when(i == 0)
        def _():
            dws_o[...] = jnp.zeros((NG, GB, GB), F32)
            bacc[...] = jnp.zeros((NG, GB, GB), F32)

        vhat, rs = _layernorm_parts(zv[...].astype(F32))
        vnb = (vhat * lg_r[...] + lb_r[...]).astype(BF16)
        ga = zg[...].astype(F32)
        sil, dsil = _silu_parts(ga)
        u = zu[...].astype(F32)
        dy = dy_r[...].astype(F32)
        t = dy * sil
        dsv_all = t * u
        dga_pre = dy * u * dsil
        mask = _chunk_mask()
        for g in range(NG):
            wf = jnp.where(mask, ws_r[g], 0.0)
            wg = wf.astype(BF16)
            wgt = wf.T.astype(BF16)
            cs = slice(g * GB, (g + 1) * GB)
            dw = jnp.zeros((GB, GB), F32)
            db = jnp.zeros((GB, GB), F32)
            for n in range(T // GB):
                rsl = slice(n * GB, (n + 1) * GB)
                vb = vnb[rsl, cs]
                sv = _dot(wg, vb) + bs_r[g]
                dz_o[0, rsl, cs] = (t[rsl, cs] * sv).astype(BF16)
                dz_o[2, rsl, cs] = (dga_pre[rsl, cs] * sv).astype(BF16)
                dsv = dsv_all[rsl, cs]
                dsb = dsv.astype(BF16)
                dvn_s[rsl, cs] = _dot(wgt, dsb)
                dw = dw + _dot_nt(dsb, vb)
                db = db + dsv
            dws_o[g] += jnp.where(mask, dw, 0.0)
            bacc[g] += db
        dvn = dvn_s[...]
        dvh = dvn * lg_r[...]
        dv = rs * (dvh - jnp.mean(dvh, axis=-1, keepdims=True) - vhat * jnp.mean(dvh * vhat, axis=-1, keepdims=True))
        dz_o[1] = dv.astype(BF16)
        _accumulate(i == 0, dlg_o, _rowsum(dvn * vhat))
        _accumulate(i == 0, dlb_o, _rowsum(dvn))

        @pl.when(i == nblk - 1)
        def _():
            for g in range(NG):
                dbs_o[g:g + 1, :] = _rowsum(bacc[g].T)

    zs = lambda k: pl.BlockSpec((T, D), lambda i: (i, k))
    return _pcall(
        body, name="branch_a_bwd", grid=(nblk,),
        in_specs=[zs(0), zs(1), zs(2), pl.BlockSpec((T, D), lambda i: (i, 0)), _full((1, D)), _full((1, D)),
                  _full((NG, GB, GB)), _full((NG, GB, GB)), HBM],
        out_specs=[pl.BlockSpec((3, T, D), lambda i: (0, i, 0)), _full((NG, GB, GB)), _full((NG, GB)), _full((1, D)),
                   _full((1, D))],
        out_shape=[SDS((10, S, D), BF16), SDS((NG, GB, GB), F32), SDS((NG, GB), F32), SDS((1, D), F32), SDS((1, D), F32)],
        scratch=[pltpu.VMEM((T, D), F32), pltpu.VMEM((NG, GB, GB), F32)], aliases={8: 0},
    )(z, z, z, dya, lg, lb, ws, bsb, dz)


def _branch_b_bwd(z, hs, lru, dyb, cw, wr, wi, lam, dz):
    S = z.shape[0]
    T = min(256, S)
    nblk = S // T

    def body(zxb, zprev, zgb, hs_r, hprev_r, dy_r, xc_r, r_r, ig_r, a_r, m_r, cw_r, wr_r, wi_r, lam_r, dz_in,
             dz_o, dcw_o, dcb_o, dwr_o, dbr_o, dwi_o, dbi_o, dlam_o, xpad, hpad, apad, dpad,
             b_s, d_s, l_s, back_s, xcb_s, dprb_s, dpib_s, lcar):
        i = pl.program_id(0)
        blk = nblk - 1 - i
        first = i == 0

        @pl.when(first)
        def _():
            apad[pl.ds(T, 8), :] = jnp.zeros((8, D), F32)
            dpad[pl.ds(T, 8), :] = jnp.zeros((8, D), F32)
            lcar[...] = jnp.zeros((8, D), F32)
            dcw_o[...] = jnp.zeros((8, D), F32)
            dwr_o[...] = jnp.zeros((NG, GB, GB), F32)
            dwi_o[...] = jnp.zeros((NG, GB, GB), F32)

        keep = (blk > 0).astype(F32)
        nck = T // CH
        cw, lam = cw_r[...], lam_r[...]
        sp8 = -LRU_C * _softplus_neg(lam)
        xpad[pl.ds(0, 8), :] = zprev[...].astype(F32)[8:16, :] * keep
        hpad[pl.ds(0, 8), :] = hprev_r[...] * keep
        for c in range(nck):
            xpad[_ck(c, 8), :] = zxb[_ck(c), :].astype(F32)
            hpad[_ck(c, 8), :] = hs_r[_ck(c), :]
            xcb_s[_ck(c), :] = xc_r[_ck(c), :].astype(BF16)
            apad[_ck(c), :] = a_r[_ck(c), :]
            sil, dsil = _silu_parts(zgb[_ck(c), :].astype(F32))
            dy = dy_r[_ck(c), :].astype(F32)
            dz_o[1, _ck(c), :] = (dy * hs_r[_ck(c), :] * dsil).astype(BF16)
            d_s[_ck(c), :] = dy * sil
        for c in range(nck):
            b_s[_ck(c), :] = apad[_ck(c, 1), :]
        _scan_reverse(b_s, d_s, l_s, lcar, T)
        s_sp = s_br = s_bi = jnp.zeros((8, D), F32)
        for c in range(nck):
            lm, r, ig, mult, a, xc = l_s[_ck(c), :], r_r[_ck(c), :], ig_r[_ck(c), :], m_r[_ck(c), :], a_r[_ck(c), :], xc_r[_ck(c), :]
            t = lm * mult
            dpad[_ck(c), :] = t * ig
            dl = lm * hpad[_ck(c, 7), :] * a - (lm * ig * xc) * (a * a) / mult
            dpr = dl * sp8 * r * (1.0 - r)
            dpi = t * xc * ig * (1.0 - ig)
            s_sp = s_sp + _half_sum(dl * r)
            s_br = s_br + _half_sum(dpr)
            s_bi = s_bi + _half_sum(dpi)
            dprb_s[_ck(c), :] = dpr.astype(BF16)
            dpib_s[_ck(c), :] = dpi.astype(BF16)
        _accumulate(first, dlam_o, _rowsum(s_sp) * (LRU_C * jax.nn.sigmoid(-lam)))
        _accumulate(first, dbr_o, _rowsum(s_br))
        _accumulate(first, dbi_o, _rowsum(s_bi))
        for h in range(NG):
            cs = slice(h * GB, (h + 1) * GB)
            back_s[:, cs] = _dot_nt(dprb_s[:, cs], wr_r[h].astype(BF16)) + _dot_nt(dpib_s[:, cs], wi_r[h].astype(BF16))
            dwr_o[h] += _dot_tn(xcb_s[:, cs], dprb_s[:, cs])
            dwi_o[h] += _dot_tn(xcb_s[:, cs], dpib_s[:, cs])
        s_cb = jnp.zeros((8, D), F32)
        s_cw = [jnp.zeros((8, D), F32)] * 4
        for c in range(nck):
            dxc = dpad[_ck(c), :] + back_s[_ck(c), :]
            dpad[_ck(c), :] = dxc
            s_cb = s_cb + _half_sum(dxc)
            s_cw = [s_cw[k] + _half_sum(xpad[_ck(c, 5 + k), :] * dxc) for k in range(4)]
        _accumulate(first, dcb_o, _rowsum(s_cb))
        for k in range(4):
            dcw_o[k:k + 1, :] += _rowsum(s_cw[k])
        for c in range(nck):
            dxb = ((dpad[_ck(c, 3), :] * cw[0:1] + dpad[_ck(c, 2), :] * cw[1:2]) + dpad[_ck(c, 1), :] * cw[2:3]) + dpad[_ck(c), :] * cw[3:4]
            dz_o[0, _ck(c), :] = dxb.astype(BF16)
        apad[pl.ds(T, 8), :] = apad[pl.ds(0, 8), :]
        dpad[pl.ds(T, 8), :] = dpad[pl.ds(0, 8), :]

    rev = lambda k: pl.BlockSpec((T, D), lambda i: (nblk - 1 - i, k))
    prev16 = pl.BlockSpec((16, D), lambda i: (jnp.maximum((nblk - 1 - i) * (T // 16) - 1, 0), 3))
    prev8 = pl.BlockSpec((8, D), lambda i: (jnp.maximum((nblk - 1 - i) * (T // 8) - 1, 0), 0))
    vec, mat = _full((1, D)), _full((NG, GB, GB))
    return _pcall(
        body, name="branch_b_bwd", grid=(nblk,),
        in_specs=[rev(3), prev16, rev(4), rev(0), prev8] + [rev(0)] * 6 + [_full((8, D)), mat, mat, vec, HBM],
        out_specs=[pl.BlockSpec((2, T, D), lambda i: (3, nblk - 1 - i, 0)), _full((8, D)), vec, mat, vec, mat, vec, vec],
        out_shape=[SDS((10, S, D), BF16), SDS((8, D), F32), SDS((1, D), F32), SDS((NG, GB, GB), F32), SDS((1, D), F32),
                   SDS((NG, GB, GB), F32), SDS((1, D), F32), SDS((1, D), F32)],
        scratch=[pltpu.VMEM((T + 8, D), F32)] * 4 + [pltpu.VMEM((T, D), F32)] * 4 + [pltpu.VMEM((T, D), BF16)] * 3
        + [pltpu.VMEM((8, D), F32)],
        aliases={15: 0}, vmem_mb=56,
    )(z, z, z, hs, hs, dyb, *lru, cw, wr, wi, lam, dz)


def _branch_c_bwd(z, kv, pw, dyc, dz):
    S = z.shape[0]
    T = min(512, S)

    def body(zq, zg, kv_r, p_r, dy_r, dz_in, dz_o, dkv_o):
        @pl.when(pl.program_id(0) == 0)
        def _():
            dkv_o[...] = jnp.zeros((MEM, 2 * D), F32)

        gc = zg[...].astype(F32)
        sil, dsil = _silu_parts(gc)
        dy = dy_r[...].astype(F32)
        datt = dy * sil
        dgc_pre = dy * dsil
        scale = HD ** -0.5
        for h in range(NH):
            cs = slice(h * HD, (h + 1) * HD)
            vs = slice(D + h * HD, D + (h + 1) * HD)
            qh = zq[:, cs]
            pb = p_r[:, cs]
            p = pb.astype(F32)
            att = _dot(pb, kv_r[:, vs])
            dz_o[1, :, cs] = (dgc_pre[:, cs] * att).astype(BF16)
            dab = datt[:, cs].astype(BF16)
            dp = _dot_nt(dab, kv_r[:, vs])
            ds = (p * (dp - jnp.sum(p * dp, axis=-1, keepdims=True)) * scale).astype(BF16)
            dz_o[0, :, cs] = _dot(ds, kv_r[:, cs]).astype(BF16)
            dkv_o[:, cs] += _dot_tn(ds, qh)
            dkv_o[:, vs] += _dot_tn(pb, dab)

    zs = lambda k: pl.BlockSpec((T, D), lambda i: (i, k))
    return _pcall(
        body, name="branch_c_bwd", grid=(S // T,),
        in_specs=[zs(5), zs(6), _full((MEM, 2 * D))] + [pl.BlockSpec((T, D), lambda i: (i, 0))] * 2 + [HBM],
        out_specs=[pl.BlockSpec((2, T, D), lambda i: (4, i, 0)), _full((MEM, 2 * D))],
        out_shape=[SDS((10, S, D), BF16), SDS((MEM, 2 * D), F32)], aliases={5: 0},
    )(z, z, kv, pw, dyc, dz)


def _mm_dh(dz, w, x, dxn, g):
    S = x.shape[0]
    tm = min(1024, S)

    def body(dz_r, w_r, x_r, dxn_r, g_r, dx_o, dg_o, acc):
        i, k = pl.program_id(0), pl.program_id(1)
        _accumulate(k == 0, acc, _dot_nt(dz_r[0], w_r[...]))

        @pl.when(k == 9)
        def _():
            dh = acc[...]
            xv = x_r[...]
            r1 = _rms_scale(xv)
            wv = dh * g_r[...]
            dx_o[...] = dxn_r[...] + r1 * wv - xv * (r1 * r1 * r1) * jnp.mean(wv * xv, axis=-1, keepdims=True)
            _accumulate(i == 0, dg_o, _rowsum(dh * xv * r1))

    row = pl.BlockSpec((tm, D), lambda i, k: (i, 0))
    return _pcall(
        body, name="mm_dh", grid=(S // tm, 10),
        in_specs=[pl.BlockSpec((1, tm, D), lambda i, k: (k, i, 0)), pl.BlockSpec((D, D), lambda i, k: (0, _dz_col(k))),
                  row, row, _full((1, D))],
        out_specs=[row, _full((1, D))], out_shape=[SDS((S, D), F32), SDS((1, D), F32)],
        scratch=[pltpu.VMEM((tm, D), F32)],
    )(dz, w, x, dxn, g)


def _mm_dwin(h, dz):
    S = h.shape[0]

    def body(h_r, dz_r, o_r):
        o_r[...] = _dot_tn(h_r[...], dz_r[0]).astype(BF16)

    return _pcall(
        body, name="mm_dwin", grid=(10,),
        in_specs=[pl.BlockSpec((S, D), lambda n: (0, 0)), pl.BlockSpec((1, S, D), lambda n: (n, 0, 0))],
        out_specs=pl.BlockSpec((D, D), lambda n: (0, _dz_col(n))), out_shape=SDS((D, NIN), BF16), vmem_mb=56,
    )(h, dz)


def _mm_tn4(a4, b4):
    S = a4[0].shape[0]
    tk = min(1024, S)
    nk = S // tk

    def body(*refs):
        a_r, b_r, o_r, acc = refs[0:4], refs[4:8], refs[8], refs[9]
        w, k = pl.program_id(0), pl.program_id(1)
        for a in range(4):
            @pl.when(w == a)
            def _(a=a):
                _accumulate(k == 0, acc, _dot_tn(a_r[a][...], b_r[a][...]))

        @pl.when(k == nk - 1)
        def _():
            o_r[...] = acc[...].astype(BF16)

    def blk(a):
        return pl.BlockSpec((tk, D), lambda w, k: (jnp.where(w == a, k, jnp.where(w < a, 0, nk - 1)), 0))

    return _pcall(body, name="mm_tn4", grid=(4, nk), in_specs=[blk(a) for a in range(4)] * 2,
                  out_specs=pl.BlockSpec((None, D, D), lambda w, k: (w, 0, 0)), out_shape=SDS((4, D, D), BF16),
                  scratch=[pltpu.VMEM((D, D), F32)])(*a4, *b4)


def _mem_bwd(mem, g, wkv, dkv, dg_acc):
    def body(m_ref, g_ref, w_ref, dkv_ref, acc_ref, dw_ref, dg_ref):
        m = m_ref[...]
        mr = m * _rms_scale(m)
        mn = (mr * g_ref[...]).astype(BF16)
        dkb = dkv_ref[...].astype(BF16)
        dw_ref[...] = _dot_tn(mn, dkb).astype(BF16)
        dg_ref[...] = acc_ref[...] + _rowsum(_dot_nt(dkb, w_ref[...]) * mr)

    return _pcall(body, name="mem_bwd", in_specs=[VMEM] * 5, out_specs=[VMEM] * 2,
                  out_shape=[SDS((D, 2 * D), BF16), SDS((1, D), F32)], vmem_mb=48)(mem, g, wkv, dkv, dg_acc)


def _adamw_math(w, g, m, v):
    m2 = ADAM_B1 * m + (1.0 - ADAM_B1) * g
    v2 = ADAM_B2 * v + (1.0 - ADAM_B2) * (g * g)
    mh = m2 / (1.0 - ADAM_B1 ** ADAM_STEP)
    vh = v2 / (1.0 - ADAM_B2 ** ADAM_STEP)
    return -ADAM_LR * (mh / (jnp.sqrt(vh) + ADAM_EPS) + ADAM_WD * w), m2, v2


def _adamw_layer(l, w, m, v, g, prev, which=None, rows=256):
    L, R, C = w.shape

    def body(w_r, m_r, v_r, g_r, *rest):
        g_o, d_o, m_o, v_o = rest[-4:]
        g = g_r[...]
        d, m2, v2 = _adamw_math(w_r[...], g, m_r[...], v_r[...])
        g_o[...] = g
        d_o[...] = d
        m_o[...] = m2
        v_o[...] = v2

    st = pl.BlockSpec((None, rows, C), lambda i: (l, i, 0))
    gs = pl.BlockSpec((rows, C), lambda i: (i, 0)) if which is None else pl.BlockSpec((None, rows, C), lambda i: (which, i, 0))
    carried = list(prev) if prev is not None else []
    return _pcall(body, name="adamw_layer", grid=(R // rows,), in_specs=[st] * 3 + [gs] + [HBM] * len(carried),
                  out_specs=[st] * 4, out_shape=[SDS(w.shape, F32)] * 4, vmem_mb=56,
                  aliases={4 + k: k for k in range(len(carried))} or None)(w, m, v, g, *carried)


def _adamw_proj(l, trios, g_p, prevs, rows=128):
    L, R, C = trios[0][0].shape

    def body(*refs):
        ins, g_r, outs = refs[0:12], refs[12], refs[-16:]
        for k in range(4):
            g = g_r[k]
            d, m2, v2 = _adamw_math(ins[3 * k][...], g, ins[3 * k + 1][...], ins[3 * k + 2][...])
            for o, val in zip(outs[4 * k:4 * k + 4], (g, d, m2, v2)):
                o[...] = val

    st = pl.BlockSpec((None, rows, C), lambda i: (l, i, 0))
    carried = [a for p in prevs for a in p] if prevs[0] is not None else []
    res = _pcall(body, name="adamw_proj", grid=(R // rows,),
                 in_specs=[st] * 12 + [pl.BlockSpec((4, rows, C), lambda i: (0, i, 0))] + [HBM] * len(carried),
                 out_specs=[st] * 16, out_shape=[SDS(trios[0][0].shape, F32)] * 16, vmem_mb=56,
                 aliases={13 + k: k for k in range(len(carried))} or None)(*[a for t in trios for a in t], g_p, *carried)
    return [res[4 * k:4 * k + 4] for k in range(4)]


def _adamw_flat(w, m, v, g, rows):
    R, C = w.shape

    def body(w_r, m_r, v_r, g_r, d_o, m_o, v_o):
        d, m2, v2 = _adamw_math(w_r[...], g_r[...], m_r[...], v_r[...])
        d_o[...] = d
        m_o[...] = m2
        v_o[...] = v2

    blk = pl.BlockSpec((rows, C), lambda i: (i, 0))
    return _pcall(body, name="adamw_flat", grid=(R // rows,), in_specs=[blk] * 4, out_specs=[blk] * 3,
                  out_shape=[SDS((R, C), F32)] * 3)(w, m, v, g)


_SMALL = ("mem_norm_g", "pre_norm_g", "post_norm_g", "gmlp_ln_g", "gmlp_ln_b", "gmlp_ws", "gmlp_bs", "conv_b", "lru_wr",
          "lru_br", "lru_wi", "lru_bi", "lru_lambda")


def _pack_small(parts, conv_w_part):
    rows = [parts[n].reshape(-1, 128) for n in _SMALL] + [conv_w_part.reshape(-1, 128)]
    used = sum(r.shape[0] for r in rows)
    rows.append(jnp.zeros((SMALL_ROWS - used, 128), F32))
    return jnp.concatenate(rows, axis=0)


def _unpack_small(pack, shapes):
    out, at = {}, 0
    for n in _SMALL:
        size = 1
        for s in shapes[n]:
            size *= s
        out[n] = pack[at:at + size // 128].reshape(shapes[n])
        at += size // 128
    return out, at


def kernel(x, mem, mem_norm_g, pre_norm_g, post_norm_g, w_in, gmlp_ln_g, gmlp_ln_b, gmlp_ws, gmlp_bs, conv_w, conv_b, lru_wr, lru_br, lru_wi, lru_bi, lru_lambda, w_kv, w_pa, w_pb, w_pc, w_out, loss_target, m_mem_norm_g, m_pre_norm_g, m_post_norm_g, m_w_in, m_gmlp_ln_g, m_gmlp_ln_b, m_gmlp_ws, m_gmlp_bs, m_conv_w, m_conv_b, m_lru_wr, m_lru_br, m_lru_wi, m_lru_bi, m_lru_lambda, m_w_kv, m_w_pa, m_w_pb, m_w_pc, m_w_out, v_mem_norm_g, v_pre_norm_g, v_post_norm_g, v_w_in, v_gmlp_ln_g, v_gmlp_ln_b, v_gmlp_ws, v_gmlp_bs, v_conv_w, v_conv_b, v_lru_wr, v_lru_br, v_lru_wi, v_lru_bi, v_lru_lambda, v_w_kv, v_w_pa, v_w_pb, v_w_pc, v_w_out):
    L = w_in.shape[0]
    S = x.shape[1]
    xs = [x[0]]
    mem2 = mem[0]
    mg = mem_norm_g.reshape(1, D)
    vec = lambda a, l: a[l].reshape(1, D)
    ci = lax.axis_index("c")
    jpos = 2 * lax.axis_index("x") + lax.axis_index("y")
    pos = jnp.reshape(jpos, (1,)).astype(jnp.int32)
    pos2 = jnp.stack([jpos, ci]).astype(jnp.int32)

    cw8 = jnp.pad(conv_w, ((0, 0), (0, 4), (0, 0)))
    given = dict(mem_norm_g=(mem_norm_g, m_mem_norm_g, v_mem_norm_g), pre_norm_g=(pre_norm_g, m_pre_norm_g, v_pre_norm_g),
                 post_norm_g=(post_norm_g, m_post_norm_g, v_post_norm_g), gmlp_ln_g=(gmlp_ln_g, m_gmlp_ln_g, v_gmlp_ln_g),
                 gmlp_ln_b=(gmlp_ln_b, m_gmlp_ln_b, v_gmlp_ln_b), gmlp_ws=(gmlp_ws, m_gmlp_ws, v_gmlp_ws),
                 gmlp_bs=(gmlp_bs, m_gmlp_bs, v_gmlp_bs), conv_b=(conv_b, m_conv_b, v_conv_b), lru_wr=(lru_wr, m_lru_wr, v_lru_wr),
                 lru_br=(lru_br, m_lru_br, v_lru_br), lru_wi=(lru_wi, m_lru_wi, v_lru_wi), lru_bi=(lru_bi, m_lru_bi, v_lru_bi),
                 lru_lambda=(lru_lambda, m_lru_lambda, v_lru_lambda))
    shapes = {n: given[n][0].shape for n in _SMALL}
    zero_cw = jnp.zeros((L, 4, D), F32)
    packs = [_pack_small({n: given[n][k] for n in _SMALL}, zero_cw) for k in range(3)]

    placed = [_cast_place(0, pos, w_in, w_kv, w_pa, w_pb, w_pc, w_out, cw8)]
    W = [None] * L
    ALL = (0, 1, 2, 3)
    first = _gather_start("gather_start_0a", (0,), placed[0][0:1])
    rest = _gather_start("gather_start_0b", (1, 2, 3), placed[0][1:4], first[-1])
    cw8 = cw8 + rest[-1][0, 0]
    placed += [_cast_place(l, pos, w_in, w_kv, w_pa, w_pb, w_pc, w_out, cw8) for l in range(1, L)]
    busy = placed[L - 1][3][0:1, 0:128] + packs[0][0:1, :] + packs[1][0:1, :] + packs[2][0:1, :]
    mid = _gather_mid("gather_mid_0a", (0,), first, busy)
    started = _gather_start("gather_start_1", ALL, placed[1], mid[-1])
    win0 = _gather_end("gather_end_0a", (0,), mid, started[-1])[0]
    zh0 = _mm_in(xs[0], vec(pre_norm_g, 0), win0)
    mid = _gather_mid("gather_mid_0b", (1, 2, 3), rest, zh0[1])
    W[0] = [win0] + list(_gather_end("gather_end_0b", (1, 2, 3), mid, mid[-1]))

    saved = []
    for l in range(L):
        Win, Wkv, Wp, Cw = W[l]
        z, h = zh0 if l == 0 else _mm_in(xs[l], vec(pre_norm_g, l), Win)
        bsb = jnp.broadcast_to(gmlp_bs[l][:, :, None], (NG, GB, GB))
        ya = _branch_a(z, vec(gmlp_ln_g, l), vec(gmlp_ln_b, l), gmlp_ws[l], bsb)
        yb, hs, *lru = _branch_b(z, Cw, vec(conv_b, l), lru_wr[l], vec(lru_br, l), lru_wi[l], vec(lru_bi, l), vec(lru_lambda, l))
        kv = _kv(mem2, mg, Wkv)
        yc, pw = _branch_c(z, kv)
        pg = vec(post_norm_g, l)
        if l + 1 < L:
            mid = _gather_mid(f"gather_mid_{l + 1}", ALL, started, yc)
            if l + 2 < L:
                started = _gather_start(f"gather_start_{l + 2}", ALL, placed[l + 2], mid[-1])
                pg = pg + started[-1][0, 0]
        pa, pb, pc, mgd, o, xn = _merge_out(ya, yb, yc, z, Wp, xs[l], pg)
        if l + 1 < L:
            W[l + 1] = _gather_end(f"gather_end_{l + 1}", ALL, mid, xn)
        xs.append(xn)
        saved.append((z, h, ya, yb, yc, hs, kv, pa, pb, pc, mgd, o, bsb, lru, pw))

    loss11, dxn = _loss_head(xs[L], loss_target[0])
    loss = lax.psum(loss11[0, 0], ("x", "y", "c"))

    big = dict(w_in=(w_in, m_w_in, v_w_in), w_kv=(w_kv, m_w_kv, v_w_kv), w_pa=(w_pa, m_w_pa, v_w_pa),
               w_pb=(w_pb, m_w_pb, v_w_pb), w_pc=(w_pc, m_w_pc, v_w_pc), w_out=(w_out, m_w_out, v_w_out))
    out = {n: None for n in big}
    kin, nsh, nkv, rp = w_in.shape[1], w_in.shape[2], w_kv.shape[2], w_pa.shape[1]

    def finish_layer(l, a2a, after):
        g_p4, lp = _a2a_wait(f"a2a_p_wait_{l}", (2,), a2a[0], after)
        g_in, g_kv, lin, lkv = _a2a_wait(f"a2a_w_wait_{l}", (0, 1), a2a[1], after)
        g_in, g_kv, g_p = _sum_share(pos2, (lin, lkv, lp.reshape(NDEV - 1, 2 * rp, D)), (g_in, g_kv, g_p4))
        g_p = g_p.reshape(4, rp, D)
        out["w_in"] = _adamw_layer(l, *big["w_in"], g_in, out["w_in"])
        out["w_kv"] = _adamw_layer(l, *big["w_kv"], g_kv, out["w_kv"])
        proj = ("w_pa", "w_pb", "w_pc", "w_out")
        for n, res in zip(proj, _adamw_proj(l, [big[n] for n in proj], g_p, [out[n] for n in proj])):
            out[n] = res

    small = {n: [None] * L for n in _SMALL}
    dconv_w = [None] * L
    dg_mem = jnp.zeros((1, D), F32)
    pending = None
    sent = []
    for l in reversed(range(L)):
        Win, Wkv, Wp, Cw = W[l]
        z, h, ya, yb, yc, hs, kv, pa, pb, pc, mgd, o, bsb, lru, pw = saved[l]
        pg = vec(post_norm_g, l) if pending is None else vec(post_norm_g, l) + pending[1][1][-1][0, 0]
        do, dpa, dpb, dpc, dya, dyb, dyc, dz, dgpost = _out_bwd(dxn, o, pg, Wp, z, pa, pb, pc)
        a2a_p = _a2a_start(f"a2a_p_start_{l}", (2,), (_mm_tn4((ya, yb, yc, mgd), (dpa, dpb, dpc, do)),),
                           (lax.empty((NDEV - 1, 4, rp // 2, D), BF16),))
        dz, dws, dbs, dlg, dlb = _branch_a_bwd(z, dya, vec(gmlp_ln_g, l) + a2a_p[-1][0, 0], vec(gmlp_ln_b, l), gmlp_ws[l], bsb, dz)
        dz, dcw, dcb, dwr, dbr, dwi, dbi, dlam = _branch_b_bwd(z, hs, lru, dyb, Cw, lru_wr[l], lru_wi[l], vec(lru_lambda, l), dz)
        dz, dkv = _branch_c_bwd(z, kv, pw, dyc, dz)
        g_in = _mm_dwin(h, dz)
        g_kv, dg_mem = _mem_bwd(mem2, mg, Wkv, dkv, dg_mem)
        a2a_w = _a2a_start(f"a2a_w_start_{l}", (0, 1), (g_in, g_kv),
                           (lax.empty((NDEV - 1, kin // 2, nsh), BF16), lax.empty((NDEV - 1, kin // 2, nkv), BF16)))
        dx, dgpre = _mm_dh(dz, Win, xs[l], dxn, vec(pre_norm_g, l) + a2a_w[-1][0, 0])
        pending = (l, (a2a_p, a2a_w))
        sent.append(pending)
        for n, val in (("pre_norm_g", dgpre), ("post_norm_g", dgpost), ("gmlp_ln_g", dlg), ("gmlp_ln_b", dlb), ("gmlp_ws", dws),
                       ("gmlp_bs", dbs), ("conv_b", dcb), ("lru_wr", dwr), ("lru_br", dbr), ("lru_wi", dwi), ("lru_bi", dbi),
                       ("lru_lambda", dlam)):
            small[n][l] = val
        dconv_w[l] = dcw[0:4]
        dxn = dx
    grad_x = dxn.reshape(1, S, D)

    parts = {n: jnp.stack(small[n]) for n in _SMALL if n != "mem_norm_g"}
    parts["mem_norm_g"] = dg_mem
    me1 = jnp.reshape(2 * jpos + ci, (1,)).astype(jnp.int32)
    rs = _rs_start(_pack_small(parts, jnp.stack(dconv_w)), lax.empty((NDEV - 1, SMALL_ROWS // NDEV, 128), F32))
    for l, a2a in sent[:-1]:
        finish_layer(l, a2a, rs[-1])
    pack, land = _rs_wait(rs, out["w_out"][1])
    ag = _ag_start(_small_sum(me1, pack, land))
    finish_layer(pending[0], pending[1], ag[-1])
    gsum = _ag_wait(ag, out["w_out"][1])
    dsm, msm, vsm = _adamw_flat(packs[0], packs[1], packs[2], gsum, 2560)
    g_small, at = _unpack_small(gsum, shapes)
    d_small, _ = _unpack_small(dsm, shapes)
    m_small, _ = _unpack_small(msm, shapes)
    v_small, _ = _unpack_small(vsm, shapes)
    for n in _SMALL:
        out[n] = (g_small[n], d_small[n], m_small[n], v_small[n])
    g_cw = lax.dynamic_slice_in_dim(gsum[at:at + L * 4 * D // 128].reshape(L * 4, D), jpos * (D // 4), D // 4, axis=1)
    d_cw, m_cw, v_cw = _adamw_flat(conv_w.reshape(L * 4, D // 4), m_conv_w.reshape(L * 4, D // 4),
                                   v_conv_w.reshape(L * 4, D // 4), g_cw, L * 4)
    out["conv_w"] = tuple(a.reshape(L, 4, D // 4) for a in (g_cw, d_cw, m_cw, v_cw))

    order = ("mem_norm_g", "pre_norm_g", "post_norm_g", "w_in", "gmlp_ln_g", "gmlp_ln_b", "gmlp_ws", "gmlp_bs", "conv_w", "conv_b",
             "lru_wr", "lru_br", "lru_wi", "lru_bi", "lru_lambda", "w_kv", "w_pa", "w_pb", "w_pc", "w_out")
    return (loss, grad_x, *[out[n][0] for n in order], *[out[n][1] for n in order], *[out[n][2] for n in order],
            *[out[n][3] for n in order])
```

```python
import functools

import jax
import jax.numpy as jnp
from jax import lax
from jax.experimental import pallas as pl
from jax.experimental.pallas import tpu as pltpu

F32 = jnp.float32
BF16 = jnp.bfloat16
SDS = jax.ShapeDtypeStruct
MESH = pl.DeviceIdType.MESH

D = 1024
NIN = 10 * D
MEM = 256
GB = 128
NG = 8
NH = 4
HD = D // NH
EPS = 1e-6
LRU_C = 8.0
ADAM_LR, ADAM_B1, ADAM_B2, ADAM_EPS, ADAM_WD, ADAM_STEP = 0.001, 0.9, 0.999, 1e-08, 0.01, 10
NDEV = 8
SMALL_ROWS = 12800

_CALL_KW = {}
HBM = pl.BlockSpec(memory_space=pltpu.HBM)
VMEM = pl.BlockSpec(memory_space=pltpu.VMEM)
SEM = pl.BlockSpec(memory_space=pltpu.SEMAPHORE)
ANY = pl.BlockSpec(memory_space=pl.ANY)
TOKEN = SDS((8, 128), F32)


def _pcall(body, *, name, in_specs, out_specs, out_shape, grid=None, scratch=(), vmem_mb=48, aliases=None, effect=False,
           prefetch=0):
    kw = dict(_CALL_KW)
    if aliases:
        kw["input_output_aliases"] = aliases
    params = dict(vmem_limit_bytes=vmem_mb << 20)
    if grid is not None:
        params["dimension_semantics"] = ("arbitrary",) * len(grid)
    if effect:
        params["has_side_effects"] = pltpu.SideEffectType.DATAFLOW_SIDE_EFFECTING
    if prefetch:
        kw["grid_spec"] = pltpu.PrefetchScalarGridSpec(num_scalar_prefetch=prefetch, grid=grid, in_specs=in_specs,
                                                       out_specs=out_specs, scratch_shapes=list(scratch))
    else:
        kw.update(in_specs=in_specs, out_specs=out_specs, scratch_shapes=list(scratch))
        if grid is not None:
            kw["grid"] = grid
    return pl.pallas_call(body, name=name, out_shape=out_shape, compiler_params=pltpu.CompilerParams(**params), **kw)


def _full(shape):
    nd = len(shape)
    return pl.BlockSpec(shape, lambda *_: (0,) * nd)


def _dot(a, b):
    return jnp.dot(a, b, preferred_element_type=F32)


def _dot_nt(a, b):
    return lax.dot_general(a, b, (((1,), (1,)), ((), ())), preferred_element_type=F32)


def _dot_tn(a, b):
    return lax.dot_general(a, b, (((0,), (0,)), ((), ())), preferred_element_type=F32)


def _rowsum(a):
    return jnp.sum(a, axis=0, keepdims=True)


def _sigmoid(x):
    return 0.5 * jnp.tanh(0.5 * x) + 0.5


def _silu_parts(g):
    s = _sigmoid(g)
    return g * s, s * (1.0 + g * (1.0 - s))


def _rms_scale(x):
    return lax.rsqrt(jnp.mean(x * x, axis=-1, keepdims=True) + EPS)


def _dz_col(k):
    return jnp.where(k < 3, k, jnp.where(k < 6, k + 4, k - 3))


def _coords():
    return lax.axis_index("x"), lax.axis_index("y"), lax.axis_index("c")


def _other_chips(x, y):
    return [(1 - x, y), (x, 1 - y), (1 - x, 1 - y)]


def _peer(x, y, c, mask):
    return (1 - x if mask & 4 else x, 1 - y if mask & 2 else y, 1 - c if mask & 1 else c)


def _remote(src, dst, ssem, rsem, k, to):
    return pltpu.make_async_remote_copy(src_ref=src, dst_ref=dst, send_sem=ssem.at[k], recv_sem=rsem.at[k], device_id=to,
                                        device_id_type=MESH)


def _w_half(a, ref, jj, cc):
    if a == 2:
        rp = ref.shape[1] // 4
        return ref.at[:, pl.ds(jj * rp + cc * (rp // 2), rp // 2), :]
    kin, nsh = ref.shape[0], ref.shape[1] // 4
    return ref.at[pl.ds(cc * (kin // 2), kin // 2), pl.ds(jj * nsh, nsh)]


def _cw_block(ref, jj):
    return ref.at[:, pl.ds(jj * (D // 4), D // 4)]


def _cast_place(l, pos, w_in, w_kv, w_pa, w_pb, w_pc, w_out, cw8):
    kin, nsh = w_in.shape[1], w_in.shape[2]
    nkv, rp = w_kv.shape[2], w_pa.shape[1]
    half = kin // 2

    def body(pos_r, win, wkv, pa, pb, pc, po, cw, Win, Wkv, Wp, Cw):
        Win[...] = win[...].astype(BF16)
        Wkv[...] = wkv[...].astype(BF16)

        @pl.when(pl.program_id(0) == 0)
        def _():
            for k, r in enumerate((pa, pb, pc, po)):
                Wp[k] = r[...].astype(BF16)
            Cw[...] = cw[...]

    proj = pl.BlockSpec((None, rp, D), lambda i, p: (l, 0, 0))
    return _pcall(
        body, name="cast_place", grid=(2,), prefetch=1,
        in_specs=[pl.BlockSpec((None, half, nsh), lambda i, p: (l, i, 0)), pl.BlockSpec((None, half, nkv), lambda i, p: (l, i, 0)),
                  proj, proj, proj, proj, pl.BlockSpec((None, 8, D // 4), lambda i, p: (l, 0, 0))],
        out_specs=[pl.BlockSpec((half, nsh), lambda i, p: (i, p[0])), pl.BlockSpec((half, nkv), lambda i, p: (i, p[0])),
                   pl.BlockSpec((4, rp, D), lambda i, p: (0, p[0], 0)), pl.BlockSpec((8, D // 4), lambda i, p: (0, p[0]))],
        out_shape=[SDS((kin, 4 * nsh), BF16), SDS((kin, 4 * nkv), BF16), SDS((4, 4 * rp, D), BF16), SDS((8, D), F32)],
    )(pos, w_in, w_kv, w_pa, w_pb, w_pc, w_out, cw8)


def _hbm_like(bufs):
    return [pltpu.HBM(b.shape, b.dtype) for b in bufs]


def _w_part(a, ref, jj, cc):
    return _cw_block(ref, jj) if a == 3 else _w_half(a, ref, jj, cc)


def _gather_start(name, kinds, bufs, after=None):
    n = len(kinds)
    extra = [] if after is None else [after]

    def body(*refs):
        w, ssem, rsem, token = refs[0:n], refs[n + len(extra)], refs[n + len(extra) + 1], refs[-1]
        x, y, c = _coords()
        j = 2 * x + y
        for k, chip in enumerate(_other_chips(x, y)):
            for i, a in enumerate(kinds):
                part = _w_part(a, w[i], j, c)
                _remote(part, part, ssem, rsem, i * 3 + k, (chip[0], chip[1], c)).start()
        token[...] = jnp.zeros((8, 128), F32)

    return _pcall(
        body, name=name, in_specs=[HBM] * n + [ANY] * len(extra), out_specs=[SEM, SEM] + [HBM] * n + [VMEM],
        out_shape=[pltpu.SemaphoreType.DMA((3 * n,)), pltpu.SemaphoreType.DMA((3 * n,))] + _hbm_like(bufs) + [TOKEN],
        aliases={i: 2 + i for i in range(n)}, effect=True,
    )(*[pltpu.with_memory_space_constraint(b, pltpu.HBM) for b in bufs], *extra)


def _gather_mid(name, kinds, started, after):
    n = len(kinds)
    fw = [i for i, a in enumerate(kinds) if a != 3]
    bufs = tuple(started[2:2 + n])

    def body(*refs):
        w, ssem, rsem, ssem2, rsem2, token = refs[0:n], refs[n], refs[n + 1], refs[n + 3], refs[n + 4], refs[-1]
        x, y, c = _coords()
        j = 2 * x + y
        me, sib = (x, y, c), (x, y, 1 - c)
        token[...] = jnp.zeros((8, 128), F32)
        chips = _other_chips(x, y)
        for k, chip in enumerate(chips):
            for i, a in enumerate(kinds):
                got = _w_part(a, w[i], 2 * chip[0] + chip[1], c)
                _remote(got, got, ssem, rsem, i * 3 + k, me).wait_recv()
        for k in range(3):
            for i, a in enumerate(kinds):
                part = _w_part(a, w[i], j, c)
                _remote(part, part, ssem, rsem, i * 3 + k, me).wait_send()
        for k, chip in enumerate(chips):
            for f, i in enumerate(fw):
                got = _w_half(kinds[i], w[i], 2 * chip[0] + chip[1], c)
                _remote(got, got, ssem2, rsem2, f * 3 + k, sib).start()

    return _pcall(
        body, name=name, in_specs=[HBM] * n + [SEM, SEM, ANY], out_specs=[SEM, SEM] + [HBM] * n + [VMEM],
        out_shape=[pltpu.SemaphoreType.DMA((3 * len(fw),)), pltpu.SemaphoreType.DMA((3 * len(fw),))] + _hbm_like(bufs) + [TOKEN],
        aliases={i: 2 + i for i in range(n)}, effect=True,
    )(*bufs, started[0], started[1], after)


def _gather_end(name, kinds, mid, after):
    n = len(kinds)
    fw = [i for i, a in enumerate(kinds) if a != 3]
    bufs = tuple(mid[2:2 + n])

    def body(*refs):
        w, ssem2, rsem2 = refs[0:n], refs[n], refs[n + 1]
        x, y, c = _coords()
        me = (x, y, c)
        for k, chip in enumerate(_other_chips(x, y)):
            for f, i in enumerate(fw):
                got = _w_half(kinds[i], w[i], 2 * chip[0] + chip[1], 1 - c)
                _remote(got, got, ssem2, rsem2, f * 3 + k, me).wait_recv()
                sent = _w_half(kinds[i], w[i], 2 * chip[0] + chip[1], c)
                _remote(sent, sent, ssem2, rsem2, f * 3 + k, me).wait_send()

    return _pcall(
        body, name=name, in_specs=[HBM] * n + [SEM, SEM, ANY], out_specs=[HBM] * n, out_shape=_hbm_like(bufs),
        aliases={i: i for i in range(n)}, effect=True,
    )(*bufs, mid[0], mid[1], after)


def _g_piece(a, ref, jd, dc):
    if a == 2:
        rp = ref.shape[1] // 4
        return ref.at[:, pl.ds(jd * rp + dc * (rp // 2), rp // 2), :]
    kin, nsh = ref.shape[0], ref.shape[1] // 4
    return ref.at[pl.ds(dc * (kin // 2), kin // 2), pl.ds(jd * nsh, nsh)]


def _a2a_start(name, kinds, grads, lands):
    n = len(kinds)

    def body(*refs):
        g, ld, ssem, rsem, token = refs[0:n], refs[n:2 * n], refs[2 * n], refs[2 * n + 1], refs[-1]
        x, y, c = _coords()
        for mask in range(1, NDEV):
            p = _peer(x, y, c, mask)
            for i, a in enumerate(kinds):
                _remote(_g_piece(a, g[i], 2 * p[0] + p[1], p[2]), ld[i].at[mask - 1], ssem, rsem, i * 7 + mask - 1, p).start()
        token[...] = jnp.zeros((8, 128), F32)

    bufs = tuple(grads) + tuple(lands)
    return _pcall(
        body, name=name, in_specs=[HBM] * (2 * n), out_specs=[SEM, SEM] + [HBM] * (2 * n) + [VMEM],
        out_shape=[pltpu.SemaphoreType.DMA((7 * n,)), pltpu.SemaphoreType.DMA((7 * n,))] + _hbm_like(bufs) + [TOKEN],
        aliases={i: 2 + i for i in range(2 * n)}, effect=True,
    )(*[pltpu.with_memory_space_constraint(b, pltpu.HBM) for b in bufs])


def _a2a_wait(name, kinds, started, after):
    n = len(kinds)
    ssem, rsem = started[0], started[1]
    bufs = tuple(started[2:2 + 2 * n])

    def body(*refs):
        g, ld, ssem, rsem = refs[0:n], refs[n:2 * n], refs[2 * n], refs[2 * n + 1]
        x, y, c = _coords()
        me = (x, y, c)
        for mask in range(1, NDEV):
            for i in range(n):
                got = ld[i].at[mask - 1]
                _remote(got, got, ssem, rsem, i * 7 + mask - 1, me).wait_recv()
        for mask in range(1, NDEV):
            p = _peer(x, y, c, mask)
            for i, a in enumerate(kinds):
                sent = _g_piece(a, g[i], 2 * p[0] + p[1], p[2])
                _remote(sent, sent, ssem, rsem, i * 7 + mask - 1, me).wait_send()

    return _pcall(
        body, name=name, in_specs=[HBM] * (2 * n) + [SEM, SEM, ANY], out_specs=[HBM] * (2 * n), out_shape=_hbm_like(bufs),
        aliases={i: i for i in range(2 * n)}, effect=True,
    )(*bufs, ssem, rsem, after)


def _sum_share(pos, lands, grads):
    rows, n = 128, 4
    widths = [ld.shape[2] for ld in lands]

    def body(pos_r, l0, w0, l1, w1, l2, w2, g0, g1, g2, b0, b1, b2, lsem, ssem, rsem):
        i = pl.program_id(0)
        x, y, c = _coords()
        sib = (x, y, 1 - c)
        ld, ow, gs, bufs = (l0, l1, l2), (w0, w1, w2), (g0, g1, g2), (b0, b1, b2)

        def dst(a, step):
            row = step * (2 * rows) + c * rows if a == 2 else c * (n * rows) + step * rows
            return gs[a].at[pl.ds(row, rows), :]

        def copies(a, step, sl):
            src = bufs[a].at[sl]
            lc = pltpu.make_async_copy(src, dst(a, step), lsem.at[a, sl])
            rc = pltpu.make_async_remote_copy(src_ref=src, dst_ref=dst(a, step), send_sem=ssem.at[a, sl], recv_sem=rsem.at[a],
                                              device_id=sib, device_id_type=MESH)
            return lc, rc

        def drain(a, step, sl):
            lc, rc = copies(a, step, sl)
            lc.wait()
            rc.wait_send()

        slot = i % 2

        @pl.when(i >= 2)
        def _():
            for a in range(3):
                drain(a, i - 2, slot)

        for a in range(3):
            acc = ow[a][...].astype(F32)
            for k in range(NDEV - 1):
                acc = acc + ld[a][k].astype(F32)
            bufs[a][slot] = acc
            lc, rc = copies(a, i, slot)
            lc.start()
            rc.start()

        @pl.when(i == n - 1)
        def _():
            for a in range(3):
                drain(a, n - 2, (n - 2) % 2)
                drain(a, n - 1, (n - 1) % 2)
                whole = gs[a].at[pl.ds(0, n * rows), :]
                pltpu.make_async_remote_copy(src_ref=whole, dst_ref=whole, send_sem=ssem.at[a, 0], recv_sem=rsem.at[a],
                                             device_id=(x, y, c), device_id_type=MESH).wait_recv()

    land = lambda w: pl.BlockSpec((NDEV - 1, rows, w), lambda i, p: (0, i, 0))
    in_specs = [land(widths[0]), pl.BlockSpec((rows, widths[0]), lambda i, p: (p[1] * n + i, p[0])),
                land(widths[1]), pl.BlockSpec((rows, widths[1]), lambda i, p: (p[1] * n + i, p[0])),
                land(widths[2]), pl.BlockSpec((None, rows, widths[2]), lambda i, p: (i, 2 * p[0] + p[1], 0))]
    args = [t for pair in zip(lands, grads) for t in pair]
    return _pcall(
        body, name="sum_share", grid=(n,), prefetch=1, in_specs=in_specs, out_specs=[HBM] * 3,
        out_shape=[SDS((2 * n * rows, w), F32) for w in widths],
        scratch=[pltpu.VMEM((2, rows, w), F32) for w in widths]
        + [pltpu.SemaphoreType.DMA((3, 2)), pltpu.SemaphoreType.DMA((3, 2)), pltpu.SemaphoreType.DMA((3,))],
    )(pos, *args)


def _dev_index(p):
    return 4 * p[0] + 2 * p[1] + p[2]


def _small_rows(ref, d):
    r8 = ref.shape[0] // NDEV
    return ref.at[pl.ds(d * r8, r8), :]


def _rs_start(pack, land):
    def body(p_ref, ld, ssem, rsem, o0, o1, token):
        x, y, c = _coords()
        for mask in range(1, NDEV):
            p = _peer(x, y, c, mask)
            _remote(_small_rows(p_ref, _dev_index(p)), ld.at[mask - 1], ssem, rsem, mask - 1, p).start()
        token[...] = jnp.zeros((8, 128), F32)

    bufs = (pack, land)
    return _pcall(
        body, name="rs_start", in_specs=[HBM] * 2, out_specs=[SEM, SEM, HBM, HBM, VMEM],
        out_shape=[pltpu.SemaphoreType.DMA((NDEV - 1,)), pltpu.SemaphoreType.DMA((NDEV - 1,))] + _hbm_like(bufs) + [TOKEN],
        aliases={0: 2, 1: 3}, effect=True,
    )(*[pltpu.with_memory_space_constraint(b, pltpu.HBM) for b in bufs])


def _rs_wait(started, after):
    ssem, rsem, pack, land, _ = started

    def body(p_ref, ld, ssem, rsem, after_r, o0, o1):
        x, y, c = _coords()
        for mask in range(1, NDEV):
            got = ld.at[mask - 1]
            _remote(got, got, ssem, rsem, mask - 1, (x, y, c)).wait_recv()
        for mask in range(1, NDEV):
            sent = _small_rows(p_ref, _dev_index(_peer(x, y, c, mask)))
            _remote(sent, sent, ssem, rsem, mask - 1, (x, y, c)).wait_send()

    return _pcall(body, name="rs_wait", in_specs=[HBM, HBM, SEM, SEM, ANY], out_specs=[HBM, HBM],
                  out_shape=_hbm_like((pack, land)), aliases={0: 0, 1: 1}, effect=True)(pack, land, ssem, rsem, after)


def _small_sum(me1, pack, land):
    R = pack.shape[0]
    r8 = R // NDEV

    def body(me_r, p_ref, ld, full):
        acc = p_ref[...]
        for k in range(NDEV - 1):
            acc = acc + ld[k]
        full[...] = acc

    own = pl.BlockSpec((r8, 128), lambda i, m: (m[0], 0))
    return _pcall(body, name="small_sum", grid=(1,), prefetch=1,
                  in_specs=[own, pl.BlockSpec((NDEV - 1, r8, 128), lambda i, m: (0, 0, 0))], out_specs=own,
                  out_shape=SDS((R, 128), F32), vmem_mb=32)(me1, pack, land)


def _ag_start(full):
    def body(f_ref, ssem, rsem, o0, token):
        x, y, c = _coords()
        mine = _small_rows(f_ref, _dev_index((x, y, c)))
        for mask in range(1, NDEV):
            _remote(mine, mine, ssem, rsem, mask - 1, _peer(x, y, c, mask)).start()
        token[...] = jnp.zeros((8, 128), F32)

    return _pcall(
        body, name="ag_start", in_specs=[HBM], out_specs=[SEM, SEM, HBM, VMEM],
        out_shape=[pltpu.SemaphoreType.DMA((NDEV - 1,)), pltpu.SemaphoreType.DMA((NDEV - 1,))] + _hbm_like((full,)) + [TOKEN],
        aliases={0: 2}, effect=True,
    )(pltpu.with_memory_space_constraint(full, pltpu.HBM))


def _ag_wait(started, after):
    ssem, rsem, full, _ = started

    def body(f_ref, ssem, rsem, after_r, o0):
        x, y, c = _coords()
        mine = _small_rows(f_ref, _dev_index((x, y, c)))
        for mask in range(1, NDEV):
            got = _small_rows(f_ref, _dev_index(_peer(x, y, c, mask)))
            _remote(got, got, ssem, rsem, mask - 1, (x, y, c)).wait_recv()
            _remote(mine, mine, ssem, rsem, mask - 1, (x, y, c)).wait_send()

    return _pcall(body, name="ag_wait", in_specs=[HBM, SEM, SEM, ANY], out_specs=[HBM], out_shape=_hbm_like((full,)),
                  aliases={0: 0}, effect=True)(full, ssem, rsem, after)[0]


def _mm_in(x, g, w):
    S = x.shape[0]
    tm, tn = min(1024, S), 2560

    def body(x_ref, g_ref, w_ref, z_ref, h_ref, hs):
        @pl.when(pl.program_id(1) == 0)
        def _():
            xv = x_ref[...]
            hb = (xv * _rms_scale(xv) * g_ref[...]).astype(BF16)
            hs[...] = hb
            h_ref[...] = hb

        z_ref[...] = _dot(hs[...], w_ref[...]).astype(BF16)

    return _pcall(
        body, name="mm_in", grid=(S // tm, NIN // tn),
        in_specs=[pl.BlockSpec((tm, D), lambda i, j: (i, 0)), _full((1, D)), pl.BlockSpec((D, tn), lambda i, j: (0, j))],
        out_specs=[pl.BlockSpec((tm, tn), lambda i, j: (i, j)), pl.BlockSpec((tm, D), lambda i, j: (i, 0))],
        out_shape=[SDS((S, NIN), BF16), SDS((S, D), BF16)], scratch=[pltpu.VMEM((tm, D), BF16)], vmem_mb=56,
    )(x, g, w)


def _chunk_mask():
    ri = lax.broadcasted_iota(jnp.int32, (GB, GB), 0)
    ci = lax.broadcasted_iota(jnp.int32, (GB, GB), 1)
    return (ri >= 64) | (ci < 64)


def _layernorm_parts(v):
    mu = jnp.mean(v, axis=-1, keepdims=True)
    d = v - mu
    rs = lax.rsqrt(jnp.mean(d * d, axis=-1, keepdims=True) + EPS)
    return d * rs, rs


def _branch_a(z, lg, lb, ws, bsb):
    S = z.shape[0]
    T = min(512, S)

    def body(zu, zv, zg, lg_r, lb_r, ws_r, bs_r, ya):
        vhat, _ = _layernorm_parts(zv[...].astype(F32))
        vnb = (vhat * lg_r[...] + lb_r[...]).astype(BF16)
        sil, _ = _silu_parts(zg[...].astype(F32))
        t = zu[...].astype(F32) * sil
        mask = _chunk_mask()
        for g in range(NG):
            wg = jnp.where(mask, ws_r[g], 0.0).astype(BF16)
            cs = slice(g * GB, (g + 1) * GB)
            for n in range(T // GB):
                rs = slice(n * GB, (n + 1) * GB)
                sv = _dot(wg, vnb[rs, cs]) + bs_r[g]
                ya[rs, cs] = (t[rs, cs] * sv).astype(BF16)

    zs = lambda k: pl.BlockSpec((T, D), lambda i: (i, k))
    return _pcall(
        body, name="branch_a", grid=(S // T,),
        in_specs=[zs(0), zs(1), zs(2), _full((1, D)), _full((1, D)), _full((NG, GB, GB)), _full((NG, GB, GB))],
        out_specs=pl.BlockSpec((T, D), lambda i: (i, 0)), out_shape=SDS((S, D), BF16),
    )(z, z, z, lg, lb, ws, bsb)


def _softplus_neg(lam):
    e = jnp.exp(-jnp.abs(lam))
    l1p = jnp.where(e < 1e-2, e * (1.0 - e * (0.5 - e * (1.0 / 3.0))), jnp.log(1.0 + e))
    return jnp.maximum(-lam, 0.0) + l1p


CH = 16


def _ck(c, off=0):
    return pl.ds(c * CH + off, CH)


def _half_sum(v):
    return v[0:8, :] + v[8:16, :]


def _lru_gates(pr, pi, br, bi, sp8):
    r = jax.nn.sigmoid(pr + br)
    ig = _sigmoid(pi + bi)
    la = sp8 * r
    a = jnp.exp(la)
    a2 = a * a
    mult = jnp.sqrt(-jnp.tanh(la) * (a2 + 1.0))
    return r, ig, a, a2, mult


def _tile_rows():
    return lax.broadcasted_iota(jnp.int32, (8, D), 0)


def _scan_forward(a_s, u_s, h_s, hcar, T):
    row = _tile_rows()

    def tile(i, hp):
        o = pl.multiple_of(i * 8, 8)
        A = a_s[pl.ds(o, 8), :]
        U = u_s[pl.ds(o, 8), :]
        for s in (1, 2, 4):
            m = row >= s
            U = jnp.where(m, U + A * pltpu.roll(U, s, 0), U)
            A = jnp.where(m, A * pltpu.roll(A, s, 0), A)
        H = U + A * hp
        h_s[pl.ds(o, 8), :] = H
        return jnp.broadcast_to(H[7:8, :], (8, D))

    hcar[...] = lax.fori_loop(0, T // 8, tile, hcar[...])


def _scan_reverse(b_s, d_s, l_s, lcar, T):
    row = _tile_rows()
    n = T // 8

    def tile(i, lp):
        o = pl.multiple_of((n - 1 - i) * 8, 8)
        B = b_s[pl.ds(o, 8), :]
        U = d_s[pl.ds(o, 8), :]
        for s in (1, 2, 4):
            m = row < 8 - s
            U = jnp.where(m, U + B * pltpu.roll(U, 8 - s, 0), U)
            B = jnp.where(m, B * pltpu.roll(B, 8 - s, 0), B)
        Lm = U + B * lp
        l_s[pl.ds(o, 8), :] = Lm
        return jnp.broadcast_to(Lm[0:1, :], (8, D))

    lcar[...] = lax.fori_loop(0, n, tile, lcar[...])


def _branch_b(z, cw, cb, wr, br, wi, bi, lam):
    S = z.shape[0]
    T = min(256, S)

    def body(zxb, zgb, cw_r, cb_r, wr_r, br_r, wi_r, bi_r, lam_r, yb, hs_o, xc_o, r_o, ig_o, a_o, m_o, xpad, u_s, hcar):
        @pl.when(pl.program_id(0) == 0)
        def _():
            xpad[pl.ds(0, 8), :] = jnp.zeros((8, D), F32)
            hcar[...] = jnp.zeros((8, D), F32)

        cw = cw_r[...]
        xpad[pl.ds(8, T), :] = zxb[...].astype(F32)
        xk = [xpad[pl.ds(5 + k, T), :] for k in range(4)]
        xc = cb_r[...] + (((xk[0] * cw[0:1] + xk[1] * cw[1:2]) + xk[2] * cw[2:3]) + xk[3] * cw[3:4])
        xcb = xc.astype(BF16)
        pr, pi = [], []
        for h in range(NG):
            cs = slice(h * GB, (h + 1) * GB)
            pr.append(_dot(xcb[:, cs], wr_r[h].astype(BF16)))
            pi.append(_dot(xcb[:, cs], wi_r[h].astype(BF16)))
        r, ig, a, _, mult = _lru_gates(jnp.concatenate(pr, axis=1), jnp.concatenate(pi, axis=1), br_r[...], bi_r[...],
                                       -LRU_C * _softplus_neg(lam_r[...]))
        for o_ref, val in ((xc_o, xc), (r_o, r), (ig_o, ig), (a_o, a), (m_o, mult)):
            o_ref[...] = val
        u_s[...] = mult * (ig * xc)
        _scan_forward(a_o, u_s, hs_o, hcar, T)
        xpad[pl.ds(0, 8), :] = xpad[pl.ds(T, 8), :]
        sil, _ = _silu_parts(zgb[...].astype(F32))
        yb[...] = (hs_o[...] * sil).astype(BF16)

    zs = lambda k: pl.BlockSpec((T, D), lambda i: (i, k))
    row = pl.BlockSpec((T, D), lambda i: (i, 0))
    return _pcall(
        body, name="branch_b", grid=(S // T,),
        in_specs=[zs(3), zs(4), _full((8, D)), _full((1, D)), _full((NG, GB, GB)), _full((1, D)), _full((NG, GB, GB)),
                  _full((1, D)), _full((1, D))],
        out_specs=[row] * 7, out_shape=[SDS((S, D), BF16)] + [SDS((S, D), F32)] * 6,
        scratch=[pltpu.VMEM((T + 8, D), F32), pltpu.VMEM((T, D), F32), pltpu.VMEM((8, D), F32)],
    )(z, z, cw, cb, wr, br, wi, bi, lam)


def _kv(mem, g, wkv):
    def body(m_ref, g_ref, w_ref, kv_ref):
        m = m_ref[...]
        mn = (m * _rms_scale(m) * g_ref[...]).astype(BF16)
        kv_ref[...] = _dot(mn, w_ref[...]).astype(BF16)

    return _pcall(body, name="mem_kv", in_specs=[VMEM] * 3, out_specs=VMEM, out_shape=SDS((MEM, 2 * D), BF16),
                  vmem_mb=32)(mem, g, wkv)


def _softmax_rows(s):
    e = jnp.exp(s - jnp.max(s, axis=-1, keepdims=True))
    return e / jnp.sum(e, axis=-1, keepdims=True)


def _branch_c(z, kv):
    S = z.shape[0]
    T = min(512, S)

    def body(zq, zg, kv_r, yc, p_o):
        sil, _ = _silu_parts(zg[...].astype(F32))
        for h in range(NH):
            cs = slice(h * HD, (h + 1) * HD)
            pb = _softmax_rows(_dot_nt(zq[:, cs], kv_r[:, cs]) * (HD ** -0.5)).astype(BF16)
            p_o[:, cs] = pb
            att = _dot(pb, kv_r[:, D + h * HD:D + (h + 1) * HD])
            yc[:, cs] = (att * sil[:, cs]).astype(BF16)

    zs = lambda k: pl.BlockSpec((T, D), lambda i: (i, k))
    row = pl.BlockSpec((T, D), lambda i: (i, 0))
    return _pcall(body, name="branch_c", grid=(S // T,), in_specs=[zs(5), zs(6), _full((MEM, 2 * D))],
                  out_specs=[row, row], out_shape=[SDS((S, D), BF16)] * 2)(z, z, kv)


def _merge_out(ya, yb, yc, z, wp, x, pg):
    S = x.shape[0]
    T = min(256, S)

    def body(ya_r, yb_r, yc_r, m0, m1, m2, wp_r, x_r, pg_r, pa_o, pb_o, pc_o, mg_o, o_o, xn_o):
        merged = None
        for y_r, ml, p_o, k in ((ya_r, m0, pa_o, 0), (yb_r, m1, pb_o, 1), (yc_r, m2, pc_o, 2)):
            p = _dot(y_r[...], wp_r[k])
            p_o[...] = p.astype(BF16)
            t = _sigmoid(ml[...].astype(F32)) * p
            merged = t if merged is None else merged + t
        mb = merged.astype(BF16)
        mg_o[...] = mb
        o = _dot(mb, wp_r[3])
        o_o[...] = o.astype(BF16)
        xn_o[...] = x_r[...] + o * _rms_scale(o) * pg_r[...]

    row = pl.BlockSpec((T, D), lambda i: (i, 0))
    zs = lambda k: pl.BlockSpec((T, D), lambda i: (i, k))
    return _pcall(
        body, name="merge_out", grid=(S // T,),
        in_specs=[row, row, row, zs(7), zs(8), zs(9), _full((4, D, D)), row, _full((1, D))],
        out_specs=[row] * 6, out_shape=[SDS((S, D), BF16)] * 5 + [SDS((S, D), F32)], vmem_mb=56,
    )(ya, yb, yc, z, z, z, wp, x, pg)


def _loss_head(y, t):
    S = y.shape[0]
    T = min(512, S)

    def body(y_r, t_r, loss_o, dy_o):
        @pl.when(pl.program_id(0) == 0)
        def _():
            loss_o[...] = jnp.zeros((1, 1), F32)

        e = y_r[...] - t_r[...]
        dy_o[...] = e * (1.0 / D)
        loss_o[...] += 0.5 * _rowsum(jnp.sum(e * e, axis=1, keepdims=True) * (1.0 / D))

    row = pl.BlockSpec((T, D), lambda i: (i, 0))
    return _pcall(body, name="loss_head", grid=(S // T,), in_specs=[row, row], out_specs=[_full((1, 1)), row],
                  out_shape=[SDS((1, 1), F32), SDS((S, D), F32)])(y, t)


def _accumulate(first, ref, val):
    @pl.when(first)
    def _():
        ref[...] = val

    @pl.when(jnp.logical_not(first))
    def _():
        ref[...] += val


def _out_bwd(dxn, o, pg, wp, z, pa, pb, pc):
    S = dxn.shape[0]
    T = min(256, S)

    def body(dy_r, o_r, pg_r, wp_r, m0, m1, m2, pa_r, pb_r, pc_r, do_o, dpa_o, dpb_o, dpc_o, dya_o, dyb_o, dyc_o, dz_o, dg_o):
        dy = dy_r[...]
        o = o_r[...].astype(F32)
        r2 = _rms_scale(o)
        w = dy * pg_r[...]
        do = r2 * w - o * (r2 * r2 * r2) * jnp.mean(w * o, axis=-1, keepdims=True)
        _accumulate(pl.program_id(0) == 0, dg_o, _rowsum(dy * o * r2))
        dob = do.astype(BF16)
        do_o[...] = dob
        dm = _dot_nt(dob, wp_r[3])
        for k, (ml, p_r, dp_o, dy_o) in enumerate(((m0, pa_r, dpa_o, dya_o), (m1, pb_r, dpb_o, dyb_o), (m2, pc_r, dpc_o, dyc_o))):
            gk = _sigmoid(ml[...].astype(F32))
            dz_o[k] = (dm * p_r[...].astype(F32) * gk * (1.0 - gk)).astype(BF16)
            dpk = (gk * dm).astype(BF16)
            dp_o[...] = dpk
            dy_o[...] = _dot_nt(dpk, wp_r[k]).astype(BF16)

    row = pl.BlockSpec((T, D), lambda i: (i, 0))
    zs = lambda k: pl.BlockSpec((T, D), lambda i: (i, k))
    return _pcall(
        body, name="out_bwd", grid=(S // T,),
        in_specs=[row, row, _full((1, D)), _full((4, D, D)), zs(7), zs(8), zs(9), row, row, row],
        out_specs=[row] * 7 + [pl.BlockSpec((3, T, D), lambda i: (1, i, 0)), _full((1, D))],
        out_shape=[SDS((S, D), BF16)] * 7 + [SDS((10, S, D), BF16), SDS((1, D), F32)], vmem_mb=56,
    )(dxn, o, pg, wp, z, z, z, pa, pb, pc)


def _branch_a_bwd(z, dya, lg, lb, ws, bsb, dz):
    S = z.shape[0]
    T = min(512, S)
    nblk = S // T

    def body(zu, zv, zg, dy_r, lg_r, lb_r, ws_r, bs_r, dz_in, dz_o, dws_o, dbs_o, dlg_o, dlb_o, dvn_s, bacc):
        i = pl.program_id(0)

        @pl.when(i == 0)
        def _():
            dws_o[...] = jnp.zeros((NG, GB, GB), F32)
            bacc[...] = jnp.zeros((NG, GB, GB), F32)

        vhat, rs = _layernorm_parts(zv[...].astype(F32))
        vnb = (vhat * lg_r[...] + lb_r[...]).astype(BF16)
        ga = zg[...].astype(F32)
        sil, dsil = _silu_parts(ga)
        u = zu[...].astype(F32)
        dy = dy_r[...].astype(F32)
        t = dy * sil
        dsv_all = t * u
        dga_pre = dy * u * dsil
        mask = _chunk_mask()
        for g in range(NG):
            wf = jnp.where(mask, ws_r[g], 0.0)
            wg = wf.astype(BF16)
            wgt = wf.T.astype(BF16)
            cs = slice(g * GB, (g + 1) * GB)
            dw = jnp.zeros((GB, GB), F32)
            db = jnp.zeros((GB, GB), F32)
            for n in range(T // GB):
                rsl = slice(n * GB, (n + 1) * GB)
                vb = vnb[rsl, cs]
                sv = _dot(wg, vb) + bs_r[g]
                dz_o[0, rsl, cs] = (t[rsl, cs] * sv).astype(BF16)
                dz_o[2, rsl, cs] = (dga_pre[rsl, cs] * sv).astype(BF16)
                dsv = dsv_all[rsl, cs]
                dsb = dsv.astype(BF16)
                dvn_s[rsl, cs] = _dot(wgt, dsb)
                dw = dw + _dot_nt(dsb, vb)
                db = db + dsv
            dws_o[g] += jnp.where(mask, dw, 0.0)
            bacc[g] += db
        dvn = dvn_s[...]
        dvh = dvn * lg_r[...]
        dv = rs * (dvh - jnp.mean(dvh, axis=-1, keepdims=True) - vhat * jnp.mean(dvh * vhat, axis=-1, keepdims=True))
        dz_o[1] = dv.astype(BF16)
        _accumulate(i == 0, dlg_o, _rowsum(dvn * vhat))
        _accumulate(i == 0, dlb_o, _rowsum(dvn))

        @pl.when(i == nblk - 1)
        def _():
            for g in range(NG):
                dbs_o[g:g + 1, :] = _rowsum(bacc[g].T)

    zs = lambda k: pl.BlockSpec((T, D), lambda i: (i, k))
    return _pcall(
        body, name="branch_a_bwd", grid=(nblk,),
        in_specs=[zs(0), zs(1), zs(2), pl.BlockSpec((T, D), lambda i: (i, 0)), _full((1, D)), _full((1, D)),
                  _full((NG, GB, GB)), _full((NG, GB, GB)), HBM],
        out_specs=[pl.BlockSpec((3, T, D), lambda i: (0, i, 0)), _full((NG, GB, GB)), _full((NG, GB)), _full((1, D)),
                   _full((1, D))],
        out_shape=[SDS((10, S, D), BF16), SDS((NG, GB, GB), F32), SDS((NG, GB), F32), SDS((1, D), F32), SDS((1, D), F32)],
        scratch=[pltpu.VMEM((T, D), F32), pltpu.VMEM((NG, GB, GB), F32)], aliases={8: 0},
    )(z, z, z, dya, lg, lb, ws, bsb, dz)


def _branch_b_bwd(z, hs, lru, dyb, cw, wr, wi, lam, dz):
    S = z.shape[0]
    T = min(256, S)
    nblk = S // T

    def body(zxb, zprev, zgb, hs_r, hprev_r, dy_r, xc_r, r_r, ig_r, a_r, m_r, cw_r, wr_r, wi_r, lam_r, dz_in,
             dz_o, dcw_o, dcb_o, dwr_o, dbr_o, dwi_o, dbi_o, dlam_o, xpad, hpad, apad, dpad,
             b_s, d_s, l_s, back_s, xcb_s, dprb_s, dpib_s, lcar):
        i = pl.program_id(0)
        blk = nblk - 1 - i
        first = i == 0

        @pl.when(first)
        def _():
            apad[pl.ds(T, 8), :] = jnp.zeros((8, D), F32)
            dpad[pl.ds(T, 8), :] = jnp.zeros((8, D), F32)
            lcar[...] = jnp.zeros((8, D), F32)
            dcw_o[...] = jnp.zeros((8, D), F32)
            dwr_o[...] = jnp.zeros((NG, GB, GB), F32)
            dwi_o[...] = jnp.zeros((NG, GB, GB), F32)

        keep = (blk > 0).astype(F32)
        nck = T // CH
        cw, lam = cw_r[...], lam_r[...]
        sp8 = -LRU_C * _softplus_neg(lam)
        xpad[pl.ds(0, 8), :] = zprev[...].astype(F32)[8:16, :] * keep
        hpad[pl.ds(0, 8), :] = hprev_r[...] * keep
        for c in range(nck):
            xpad[_ck(c, 8), :] = zxb[_ck(c), :].astype(F32)
            hpad[_ck(c, 8), :] = hs_r[_ck(c), :]
            xcb_s[_ck(c), :] = xc_r[_ck(c), :].astype(BF16)
            apad[_ck(c), :] = a_r[_ck(c), :]
            sil, dsil = _silu_parts(zgb[_ck(c), :].astype(F32))
            dy = dy_r[_ck(c), :].astype(F32)
            dz_o[1, _ck(c), :] = (dy * hs_r[_ck(c), :] * dsil).astype(BF16)
            d_s[_ck(c), :] = dy * sil
        for c in range(nck):
            b_s[_ck(c), :] = apad[_ck(c, 1), :]
        _scan_reverse(b_s, d_s, l_s, lcar, T)
        s_sp = s_br = s_bi = jnp.zeros((8, D), F32)
        for c in range(nck):
            lm, r, ig, mult, a, xc = l_s[_ck(c), :], r_r[_ck(c), :], ig_r[_ck(c), :], m_r[_ck(c), :], a_r[_ck(c), :], xc_r[_ck(c), :]
            t = lm * mult
            dpad[_ck(c), :] = t * ig
            dl = lm * hpad[_ck(c, 7), :] * a - (lm * ig * xc) * (a * a) / mult
            dpr = dl * sp8 * r * (1.0 - r)
            dpi = t * xc * ig * (1.0 - ig)
            s_sp = s_sp + _half_sum(dl * r)
            s_br = s_br + _half_sum(dpr)
            s_bi = s_bi + _half_sum(dpi)
            dprb_s[_ck(c), :] = dpr.astype(BF16)
            dpib_s[_ck(c), :] = dpi.astype(BF16)
        _accumulate(first, dlam_o, _rowsum(s_sp) * (LRU_C * jax.nn.sigmoid(-lam)))
        _accumulate(first, dbr_o, _rowsum(s_br))
        _accumulate(first, dbi_o, _rowsum(s_bi))
        for h in range(NG):
            cs = slice(h * GB, (h + 1) * GB)
            back_s[:, cs] = _dot_nt(dprb_s[:, cs], wr_r[h].astype(BF16)) + _dot_nt(dpib_s[:, cs], wi_r[h].astype(BF16))
            dwr_o[h] += _dot_tn(xcb_s[:, cs], dprb_s[:, cs])
            dwi_o[h] += _dot_tn(xcb_s[:, cs], dpib_s[:, cs])
        s_cb = jnp.zeros((8, D), F32)
        s_cw = [jnp.zeros((8, D), F32)] * 4
        for c in range(nck):
            dxc = dpad[_ck(c), :] + back_s[_ck(c), :]
            dpad[_ck(c), :] = dxc
            s_cb = s_cb + _half_sum(dxc)
            s_cw = [s_cw[k] + _half_sum(xpad[_ck(c, 5 + k), :] * dxc) for k in range(4)]
        _accumulate(first, dcb_o, _rowsum(s_cb))
        for k in range(4):
            dcw_o[k:k + 1, :] += _rowsum(s_cw[k])
        for c in range(nck):
            dxb = ((dpad[_ck(c, 3), :] * cw[0:1] + dpad[_ck(c, 2), :] * cw[1:2]) + dpad[_ck(c, 1), :] * cw[2:3]) + dpad[_ck(c), :] * cw[3:4]
            dz_o[0, _ck(c), :] = dxb.astype(BF16)
        apad[pl.ds(T, 8), :] = apad[pl.ds(0, 8), :]
        dpad[pl.ds(T, 8), :] = dpad[pl.ds(0, 8), :]

    rev = lambda k: pl.BlockSpec((T, D), lambda i: (nblk - 1 - i, k))
    prev16 = pl.BlockSpec((16, D), lambda i: (jnp.maximum((nblk - 1 - i) * (T // 16) - 1, 0), 3))
    prev8 = pl.BlockSpec((8, D), lambda i: (jnp.maximum((nblk - 1 - i) * (T // 8) - 1, 0), 0))
    vec, mat = _full((1, D)), _full((NG, GB, GB))
    return _pcall(
        body, name="branch_b_bwd", grid=(nblk,),
        in_specs=[rev(3), prev16, rev(4), rev(0), prev8] + [rev(0)] * 6 + [_full((8, D)), mat, mat, vec, HBM],
        out_specs=[pl.BlockSpec((2, T, D), lambda i: (3, nblk - 1 - i, 0)), _full((8, D)), vec, mat, vec, mat, vec, vec],
        out_shape=[SDS((10, S, D), BF16), SDS((8, D), F32), SDS((1, D), F32), SDS((NG, GB, GB), F32), SDS((1, D), F32),
                   SDS((NG, GB, GB), F32), SDS((1, D), F32), SDS((1, D), F32)],
        scratch=[pltpu.VMEM((T + 8, D), F32)] * 4 + [pltpu.VMEM((T, D), F32)] * 4 + [pltpu.VMEM((T, D), BF16)] * 3
        + [pltpu.VMEM((8, D), F32)],
        aliases={15: 0}, vmem_mb=56,
    )(z, z, z, hs, hs, dyb, *lru, cw, wr, wi, lam, dz)


def _branch_c_bwd(z, kv, pw, dyc, dz):
    S = z.shape[0]
    T = min(512, S)

    def body(zq, zg, kv_r, p_r, dy_r, dz_in, dz_o, dkv_o):
        @pl.when(pl.program_id(0) == 0)
        def _():
            dkv_o[...] = jnp.zeros((MEM, 2 * D), F32)

        gc = zg[...].astype(F32)
        sil, dsil = _silu_parts(gc)
        dy = dy_r[...].astype(F32)
        datt = dy * sil
        dgc_pre = dy * dsil
        scale = HD ** -0.5
        for h in range(NH):
            cs = slice(h * HD, (h + 1) * HD)
            vs = slice(D + h * HD, D + (h + 1) * HD)
            qh = zq[:, cs]
            pb = p_r[:, cs]
            p = pb.astype(F32)
            att = _dot(pb, kv_r[:, vs])
            dz_o[1, :, cs] = (dgc_pre[:, cs] * att).astype(BF16)
            dab = datt[:, cs].astype(BF16)
            dp = _dot_nt(dab, kv_r[:, vs])
            ds = (p * (dp - jnp.sum(p * dp, axis=-1, keepdims=True)) * scale).astype(BF16)
            dz_o[0, :, cs] = _dot(ds, kv_r[:, cs]).astype(BF16)
            dkv_o[:, cs] += _dot_tn(ds, qh)
            dkv_o[:, vs] += _dot_tn(pb, dab)

    zs = lambda k: pl.BlockSpec((T, D), lambda i: (i, k))
    return _pcall(
        body, name="branch_c_bwd", grid=(S // T,),
        in_specs=[zs(5), zs(6), _full((MEM, 2 * D))] + [pl.BlockSpec((T, D), lambda i: (i, 0))] * 2 + [HBM],
        out_specs=[pl.BlockSpec((2, T, D), lambda i: (4, i, 0)), _full((MEM, 2 * D))],
        out_shape=[SDS((10, S, D), BF16), SDS((MEM, 2 * D), F32)], aliases={5: 0},
    )(z, z, kv, pw, dyc, dz)


def _mm_dh(dz, w, x, dxn, g):
    S = x.shape[0]
    tm = min(1024, S)

    def body(dz_r, w_r, x_r, dxn_r, g_r, dx_o, dg_o, acc):
        i, k = pl.program_id(0), pl.program_id(1)
        _accumulate(k == 0, acc, _dot_nt(dz_r[0], w_r[...]))

        @pl.when(k == 9)
        def _():
            dh = acc[...]
            xv = x_r[...]
            r1 = _rms_scale(xv)
            wv = dh * g_r[...]
            dx_o[...] = dxn_r[...] + r1 * wv - xv * (r1 * r1 * r1) * jnp.mean(wv * xv, axis=-1, keepdims=True)
            _accumulate(i == 0, dg_o, _rowsum(dh * xv * r1))

    row = pl.BlockSpec((tm, D), lambda i, k: (i, 0))
    return _pcall(
        body, name="mm_dh", grid=(S // tm, 10),
        in_specs=[pl.BlockSpec((1, tm, D), lambda i, k: (k, i, 0)), pl.BlockSpec((D, D), lambda i, k: (0, _dz_col(k))),
                  row, row, _full((1, D))],
        out_specs=[row, _full((1, D))], out_shape=[SDS((S, D), F32), SDS((1, D), F32)],
        scratch=[pltpu.VMEM((tm, D), F32)],
    )(dz, w, x, dxn, g)


def _mm_dwin(h, dz):
    S = h.shape[0]

    def body(h_r, dz_r, o_r):
        o_r[...] = _dot_tn(h_r[...], dz_r[0]).astype(BF16)

    return _pcall(
        body, name="mm_dwin", grid=(10,),
        in_specs=[pl.BlockSpec((S, D), lambda n: (0, 0)), pl.BlockSpec((1, S, D), lambda n: (n, 0, 0))],
        out_specs=pl.BlockSpec((D, D), lambda n: (0, _dz_col(n))), out_shape=SDS((D, NIN), BF16), vmem_mb=56,
    )(h, dz)


def _mm_tn4(a4, b4):
    S = a4[0].shape[0]
    tk = min(1024, S)
    nk = S // tk

    def body(*refs):
        a_r, b_r, o_r, acc = refs[0:4], refs[4:8], refs[8], refs[9]
        w, k = pl.program_id(0), pl.program_id(1)
        for a in range(4):
            @pl.when(w == a)
            def _(a=a):
                _accumulate(k == 0, acc, _dot_tn(a_r[a][...], b_r[a][...]))

        @pl.when(k == nk - 1)
        def _():
            o_r[...] = acc[...].astype(BF16)

    def blk(a):
        return pl.BlockSpec((tk, D), lambda w, k: (jnp.where(w == a, k, jnp.where(w < a, 0, nk - 1)), 0))

    return _pcall(body, name="mm_tn4", grid=(4, nk), in_specs=[blk(a) for a in range(4)] * 2,
                  out_specs=pl.BlockSpec((None, D, D), lambda w, k: (w, 0, 0)), out_shape=SDS((4, D, D), BF16),
                  scratch=[pltpu.VMEM((D, D), F32)])(*a4, *b4)


def _mem_bwd(mem, g, wkv, dkv, dg_acc):
    def body(m_ref, g_ref, w_ref, dkv_ref, acc_ref, dw_ref, dg_ref):
        m = m_ref[...]
        mr = m * _rms_scale(m)
        mn = (mr * g_ref[...]).astype(BF16)
        dkb = dkv_ref[...].astype(BF16)
        dw_ref[...] = _dot_tn(mn, dkb).astype(BF16)
        dg_ref[...] = acc_ref[...] + _rowsum(_dot_nt(dkb, w_ref[...]) * mr)

    return _pcall(body, name="mem_bwd", in_specs=[VMEM] * 5, out_specs=[VMEM] * 2,
                  out_shape=[SDS((D, 2 * D), BF16), SDS((1, D), F32)], vmem_mb=48)(mem, g, wkv, dkv, dg_acc)


def _adamw_math(w, g, m, v):
    m2 = ADAM_B1 * m + (1.0 - ADAM_B1) * g
    v2 = ADAM_B2 * v + (1.0 - ADAM_B2) * (g * g)
    mh = m2 / (1.0 - ADAM_B1 ** ADAM_STEP)
    vh = v2 / (1.0 - ADAM_B2 ** ADAM_STEP)
    return -ADAM_LR * (mh / (jnp.sqrt(vh) + ADAM_EPS) + ADAM_WD * w), m2, v2


def _adamw_layer(l, w, m, v, g, prev, which=None, rows=256):
    L, R, C = w.shape

    def body(w_r, m_r, v_r, g_r, *rest):
        g_o, d_o, m_o, v_o = rest[-4:]
        g = g_r[...]
        d, m2, v2 = _adamw_math(w_r[...], g, m_r[...], v_r[...])
        g_o[...] = g
        d_o[...] = d
        m_o[...] = m2
        v_o[...] = v2

    st = pl.BlockSpec((None, rows, C), lambda i: (l, i, 0))
    gs = pl.BlockSpec((rows, C), lambda i: (i, 0)) if which is None else pl.BlockSpec((None, rows, C), lambda i: (which, i, 0))
    carried = list(prev) if prev is not None else []
    return _pcall(body, name="adamw_layer", grid=(R // rows,), in_specs=[st] * 3 + [gs] + [HBM] * len(carried),
                  out_specs=[st] * 4, out_shape=[SDS(w.shape, F32)] * 4, vmem_mb=56,
                  aliases={4 + k: k for k in range(len(carried))} or None)(w, m, v, g, *carried)


def _adamw_proj(l, trios, g_p, prevs, rows=128):
    L, R, C = trios[0][0].shape

    def body(*refs):
        ins, g_r, outs = refs[0:12], refs[12], refs[-16:]
        for k in range(4):
            g = g_r[k]
            d, m2, v2 = _adamw_math(ins[3 * k][...], g, ins[3 * k + 1][...], ins[3 * k + 2][...])
            for o, val in zip(outs[4 * k:4 * k + 4], (g, d, m2, v2)):
                o[...] = val

    st = pl.BlockSpec((None, rows, C), lambda i: (l, i, 0))
    carried = [a for p in prevs for a in p] if prevs[0] is not None else []
    res = _pcall(body, name="adamw_proj", grid=(R // rows,),
                 in_specs=[st] * 12 + [pl.BlockSpec((4, rows, C), lambda i: (0, i, 0))] + [HBM] * len(carried),
                 out_specs=[st] * 16, out_shape=[SDS(trios[0][0].shape, F32)] * 16, vmem_mb=56,
                 aliases={13 + k: k for k in range(len(carried))} or None)(*[a for t in trios for a in t], g_p, *carried)
    return [res[4 * k:4 * k + 4] for k in range(4)]


def _adamw_flat(w, m, v, g, rows):
    R, C = w.shape

    def body(w_r, m_r, v_r, g_r, d_o, m_o, v_o):
        d, m2, v2 = _adamw_math(w_r[...], g_r[...], m_r[...], v_r[...])
        d_o[...] = d
        m_o[...] = m2
        v_o[...] = v2

    blk = pl.BlockSpec((rows, C), lambda i: (i, 0))
    return _pcall(body, name="adamw_flat", grid=(R // rows,), in_specs=[blk] * 4, out_specs=[blk] * 3,
                  out_shape=[SDS((R, C), F32)] * 3)(w, m, v, g)


_SMALL = ("mem_norm_g", "pre_norm_g", "post_norm_g", "gmlp_ln_g", "gmlp_ln_b", "gmlp_ws", "gmlp_bs", "conv_b", "lru_wr",
          "lru_br", "lru_wi", "lru_bi", "lru_lambda")


def _pack_small(parts, conv_w_part):
    rows = [parts[n].reshape(-1, 128) for n in _SMALL] + [conv_w_part.reshape(-1, 128)]
    used = sum(r.shape[0] for r in rows)
    rows.append(jnp.zeros((SMALL_ROWS - used, 128), F32))
    return jnp.concatenate(rows, axis=0)


def _unpack_small(pack, shapes):
    out, at = {}, 0
    for n in _SMALL:
        size = 1
        for s in shapes[n]:
            size *= s
        out[n] = pack[at:at + size // 128].reshape(shapes[n])
        at += size // 128
    return out, at


def kernel(x, mem, mem_norm_g, pre_norm_g, post_norm_g, w_in, gmlp_ln_g, gmlp_ln_b, gmlp_ws, gmlp_bs, conv_w, conv_b, lru_wr, lru_br, lru_wi, lru_bi, lru_lambda, w_kv, w_pa, w_pb, w_pc, w_out, loss_target, m_mem_norm_g, m_pre_norm_g, m_post_norm_g, m_w_in, m_gmlp_ln_g, m_gmlp_ln_b, m_gmlp_ws, m_gmlp_bs, m_conv_w, m_conv_b, m_lru_wr, m_lru_br, m_lru_wi, m_lru_bi, m_lru_lambda, m_w_kv, m_w_pa, m_w_pb, m_w_pc, m_w_out, v_mem_norm_g, v_pre_norm_g, v_post_norm_g, v_w_in, v_gmlp_ln_g, v_gmlp_ln_b, v_gmlp_ws, v_gmlp_bs, v_conv_w, v_conv_b, v_lru_wr, v_lru_br, v_lru_wi, v_lru_bi, v_lru_lambda, v_w_kv, v_w_pa, v_w_pb, v_w_pc, v_w_out):
    L = w_in.shape[0]
    S = x.shape[1]
    xs = [x[0]]
    mem2 = mem[0]
    mg = mem_norm_g.reshape(1, D)
    vec = lambda a, l: a[l].reshape(1, D)
    ci = lax.axis_index("c")
    jpos = 2 * lax.axis_index("x") + lax.axis_index("y")
    pos = jnp.reshape(jpos, (1,)).astype(jnp.int32)
    pos2 = jnp.stack([jpos, ci]).astype(jnp.int32)

    cw8 = jnp.pad(conv_w, ((0, 0), (0, 4), (0, 0)))
    given = dict(mem_norm_g=(mem_norm_g, m_mem_norm_g, v_mem_norm_g), pre_norm_g=(pre_norm_g, m_pre_norm_g, v_pre_norm_g),
                 post_norm_g=(post_norm_g, m_post_norm_g, v_post_norm_g), gmlp_ln_g=(gmlp_ln_g, m_gmlp_ln_g, v_gmlp_ln_g),
                 gmlp_ln_b=(gmlp_ln_b, m_gmlp_ln_b, v_gmlp_ln_b), gmlp_ws=(gmlp_ws, m_gmlp_ws, v_gmlp_ws),
                 gmlp_bs=(gmlp_bs, m_gmlp_bs, v_gmlp_bs), conv_b=(conv_b, m_conv_b, v_conv_b), lru_wr=(lru_wr, m_lru_wr, v_lru_wr),
                 lru_br=(lru_br, m_lru_br, v_lru_br), lru_wi=(lru_wi, m_lru_wi, v_lru_wi), lru_bi=(lru_bi, m_lru_bi, v_lru_bi),
                 lru_lambda=(lru_lambda, m_lru_lambda, v_lru_lambda))
    shapes = {n: given[n][0].shape for n in _SMALL}
    zero_cw = jnp.zeros((L, 4, D), F32)
    packs = [_pack_small({n: given[n][k] for n in _SMALL}, zero_cw) for k in range(3)]

    placed = [_cast_place(0, pos, w_in, w_kv, w_pa, w_pb, w_pc, w_out, cw8)]
    W = [None] * L
    ALL = (0, 1, 2, 3)
    first = _gather_start("gather_start_0a", (0,), placed[0][0:1])
    rest = _gather_start("gather_start_0b", (1, 2, 3), placed[0][1:4], first[-1])
    cw8 = cw8 + rest[-1][0, 0]
    placed += [_cast_place(l, pos, w_in, w_kv, w_pa, w_pb, w_pc, w_out, cw8) for l in range(1, L)]
    busy = placed[L - 1][3][0:1, 0:128] + packs[0][0:1, :] + packs[1][0:1, :] + packs[2][0:1, :]
    mid = _gather_mid("gather_mid_0a", (0,), first, busy)
    started = _gather_start("gather_start_1", ALL, placed[1], mid[-1])
    win0 = _gather_end("gather_end_0a", (0,), mid, started[-1])[0]
    zh0 = _mm_in(xs[0], vec(pre_norm_g, 0), win0)
    mid = _gather_mid("gather_mid_0b", (1, 2, 3), rest, zh0[1])
    W[0] = [win0] + list(_gather_end("gather_end_0b", (1, 2, 3), mid, mid[-1]))

    saved = []
    for l in range(L):
        Win, Wkv, Wp, Cw = W[l]
        z, h = zh0 if l == 0 else _mm_in(xs[l], vec(pre_norm_g, l), Win)
        bsb = jnp.broadcast_to(gmlp_bs[l][:, :, None], (NG, GB, GB))
        ya = _branch_a(z, vec(gmlp_ln_g, l), vec(gmlp_ln_b, l), gmlp_ws[l], bsb)
        yb, hs, *lru = _branch_b(z, Cw, vec(conv_b, l), lru_wr[l], vec(lru_br, l), lru_wi[l], vec(lru_bi, l), vec(lru_lambda, l))
        kv = _kv(mem2, mg, Wkv)
        yc, pw = _branch_c(z, kv)
        pg = vec(post_norm_g, l)
        if l + 1 < L:
            mid = _gather_mid(f"gather_mid_{l + 1}", ALL, started, yc)
            if l + 2 < L:
                started = _gather_start(f"gather_start_{l + 2}", ALL, placed[l + 2], mid[-1])
                pg = pg + started[-1][0, 0]
        pa, pb, pc, mgd, o, xn = _merge_out(ya, yb, yc, z, Wp, xs[l], pg)
        if l + 1 < L:
            W[l + 1] = _gather_end(f"gather_end_{l + 1}", ALL, mid, xn)
        xs.append(xn)
        saved.append((z, h, ya, yb, yc, hs, kv, pa, pb, pc, mgd, o, bsb, lru, pw))

    loss11, dxn = _loss_head(xs[L], loss_target[0])
    loss = lax.psum(loss11[0, 0], ("x", "y", "c"))

    big = dict(w_in=(w_in, m_w_in, v_w_in), w_kv=(w_kv, m_w_kv, v_w_kv), w_pa=(w_pa, m_w_pa, v_w_pa),
               w_pb=(w_pb, m_w_pb, v_w_pb), w_pc=(w_pc, m_w_pc, v_w_pc), w_out=(w_out, m_w_out, v_w_out))
    out = {n: None for n in big}
    kin, nsh, nkv, rp = w_in.shape[1], w_in.shape[2], w_kv.shape[2], w_pa.shape[1]

    def finish_layer(l, a2a, after):
        g_p4, lp = _a2a_wait(f"a2a_p_wait_{l}", (2,), a2a[0], after)
        g_in, g_kv, lin, lkv = _a2a_wait(f"a2a_w_wait_{l}", (0, 1), a2a[1], after)
        g_in, g_kv, g_p = _sum_share(pos2, (lin, lkv, lp.reshape(NDEV - 1, 2 * rp, D)), (g_in, g_kv, g_p4))
        g_p = g_p.reshape(4, rp, D)
        out["w_in"] = _adamw_layer(l, *big["w_in"], g_in, out["w_in"])
        out["w_kv"] = _adamw_layer(l, *big["w_kv"], g_kv, out["w_kv"])
        proj = ("w_pa", "w_pb", "w_pc", "w_out")
        for n, res in zip(proj, _adamw_proj(l, [big[n] for n in proj], g_p, [out[n] for n in proj])):
            out[n] = res

    small = {n: [None] * L for n in _SMALL}
    dconv_w = [None] * L
    dg_mem = jnp.zeros((1, D), F32)
    pending = None
    sent = []
    for l in reversed(range(L)):
        Win, Wkv, Wp, Cw = W[l]
        z, h, ya, yb, yc, hs, kv, pa, pb, pc, mgd, o, bsb, lru, pw = saved[l]
        pg = vec(post_norm_g, l) if pending is None else vec(post_norm_g, l) + pending[1][1][-1][0, 0]
        do, dpa, dpb, dpc, dya, dyb, dyc, dz, dgpost = _out_bwd(dxn, o, pg, Wp, z, pa, pb, pc)
        a2a_p = _a2a_start(f"a2a_p_start_{l}", (2,), (_mm_tn4((ya, yb, yc, mgd), (dpa, dpb, dpc, do)),),
                           (lax.empty((NDEV - 1, 4, rp // 2, D), BF16),))
        dz, dws, dbs, dlg, dlb = _branch_a_bwd(z, dya, vec(gmlp_ln_g, l) + a2a_p[-1][0, 0], vec(gmlp_ln_b, l), gmlp_ws[l], bsb, dz)
        dz, dcw, dcb, dwr, dbr, dwi, dbi, dlam = _branch_b_bwd(z, hs, lru, dyb, Cw, lru_wr[l], lru_wi[l], vec(lru_lambda, l), dz)
        dz, dkv = _branch_c_bwd(z, kv, pw, dyc, dz)
        g_in = _mm_dwin(h, dz)
        g_kv, dg_mem = _mem_bwd(mem2, mg, Wkv, dkv, dg_mem)
        a2a_w = _a2a_start(f"a2a_w_start_{l}", (0, 1), (g_in, g_kv),
                           (lax.empty((NDEV - 1, kin // 2, nsh), BF16), lax.empty((NDEV - 1, kin // 2, nkv), BF16)))
        dx, dgpre = _mm_dh(dz, Win, xs[l], dxn, vec(pre_norm_g, l) + a2a_w[-1][0, 0])
        pending = (l, (a2a_p, a2a_w))
        sent.append(pending)
        for n, val in (("pre_norm_g", dgpre), ("post_norm_g", dgpost), ("gmlp_ln_g", dlg), ("gmlp_ln_b", dlb), ("gmlp_ws", dws),
                       ("gmlp_bs", dbs), ("conv_b", dcb), ("lru_wr", dwr), ("lru_br", dbr), ("lru_wi", dwi), ("lru_bi", dbi),
                       ("lru_lambda", dlam)):
            small[n][l] = val
        dconv_w[l] = dcw[0:4]
        dxn = dx
    grad_x = dxn.reshape(1, S, D)

    parts = {n: jnp.stack(small[n]) for n in _SMALL if n != "mem_norm_g"}
    parts["mem_norm_g"] = dg_mem
    me1 = jnp.reshape(2 * jpos + ci, (1,)).astype(jnp.int32)
    rs = _rs_start(_pack_small(parts, jnp.stack(dconv_w)), lax.empty((NDEV - 1, SMALL_ROWS // NDEV, 128), F32))
    for l, a2a in sent[:-1]:
        finish_layer(l, a2a, rs[-1])
    pack, land = _rs_wait(rs, out["w_out"][1])
    ag = _ag_start(_small_sum(me1, pack, land))
    finish_layer(pending[0], pending[1], ag[-1])
    gsum = _ag_wait(ag, out["w_out"][1])
    dsm, msm, vsm = _adamw_flat(packs[0], packs[1], packs[2], gsum, 2560)
    g_small, at = _unpack_small(gsum, shapes)
    d_small, _ = _unpack_small(dsm, shapes)
    m_small, _ = _unpack_small(msm, shapes)
    v_small, _ = _unpack_small(vsm, shapes)
    for n in _SMALL:
        out[n] = (g_small[n], d_small[n], m_small[n], v_small[n])
    g_cw = lax.dynamic_slice_in_dim(gsum[at:at + L * 4 * D // 128].reshape(L * 4, D), jpos * (D // 4), D // 4, axis=1)
    d_cw, m_cw, v_cw = _adamw_flat(conv_w.reshape(L * 4, D // 4), m_conv_w.reshape(L * 4, D // 4),
                                   v_conv_w.reshape(L * 4, D // 4), g_cw, L * 4)
    out["conv_w"] = tuple(a.reshape(L, 4, D // 4) for a in (g_cw, d_cw, m_cw, v_cw))

    order = ("mem_norm_g", "pre_norm_g", "post_norm_g", "w_in", "gmlp_ln_g", "gmlp_ln_b", "gmlp_ws", "gmlp_bs", "conv_w", "conv_b",
             "lru_wr", "lru_br", "lru_wi", "lru_bi", "lru_lambda", "w_kv", "w_pa", "w_pb", "w_pc", "w_out")
    return (loss, grad_x, *[out[n][0] for n in order], *[out[n][1] for n in order], *[out[n][2] for n in order],
            *[out[n][3] for n in order])
```

```python
import functools

import jax
import jax.numpy as jnp
from jax import lax
from jax.experimental import pallas as pl
from jax.experimental.pallas import tpu as pltpu

F32 = jnp.float32
BF16 = jnp.bfloat16
SDS = jax.ShapeDtypeStruct
MESH = pl.DeviceIdType.MESH

D = 1024
NIN = 10 * D
MEM = 256
GB = 128
NG = 8
NH = 4
HD = D // NH
EPS = 1e-6
LRU_C = 8.0
ADAM_LR, ADAM_B1, ADAM_B2, ADAM_EPS, ADAM_WD, ADAM_STEP = 0.001, 0.9, 0.999, 1e-08, 0.01, 10
NDEV = 8
SMALL_ROWS = 12800

_CALL_KW = {}
HBM = pl.BlockSpec(memory_space=pltpu.HBM)
VMEM = pl.BlockSpec(memory_space=pltpu.VMEM)
SEM = pl.BlockSpec(memory_space=pltpu.SEMAPHORE)
ANY = pl.BlockSpec(memory_space=pl.ANY)
TOKEN = SDS((8, 128), F32)


def _pcall(body, *, name, in_specs, out_specs, out_shape, grid=None, scratch=(), vmem_mb=48, aliases=None, effect=False,
           prefetch=0):
    kw = dict(_CALL_KW)
    if aliases:
        kw["input_output_aliases"] = aliases
    params = dict(vmem_limit_bytes=vmem_mb << 20)
    if grid is not None:
        params["dimension_semantics"] = ("arbitrary",) * len(grid)
    if effect:
        params["has_side_effects"] = pltpu.SideEffectType.DATAFLOW_SIDE_EFFECTING
    if prefetch:
        kw["grid_spec"] = pltpu.PrefetchScalarGridSpec(num_scalar_prefetch=prefetch, grid=grid, in_specs=in_specs,
                                                       out_specs=out_specs, scratch_shapes=list(scratch))
    else:
        kw.update(in_specs=in_specs, out_specs=out_specs, scratch_shapes=list(scratch))
        if grid is not None:
            kw["grid"] = grid
    return pl.pallas_call(body, name=name, out_shape=out_shape, compiler_params=pltpu.CompilerParams(**params), **kw)


def _full(shape):
    nd = len(shape)
    return pl.BlockSpec(shape, lambda *_: (0,) * nd)


def _dot(a, b):
    return jnp.dot(a, b, preferred_element_type=F32)


def _dot_nt(a, b):
    return lax.dot_general(a, b, (((1,), (1,)), ((), ())), preferred_element_type=F32)


def _dot_tn(a, b):
    return lax.dot_general(a, b, (((0,), (0,)), ((), ())), preferred_element_type=F32)


def _rowsum(a):
    return jnp.sum(a, axis=0, keepdims=True)


def _sigmoid(x):
    return 0.5 * jnp.tanh(0.5 * x) + 0.5


def _silu_parts(g):
    s = _sigmoid(g)
    return g * s, s * (1.0 + g * (1.0 - s))


def _rms_scale(x):
    return lax.rsqrt(jnp.mean(x * x, axis=-1, keepdims=True) + EPS)


def _dz_col(k):
    return jnp.where(k < 3, k, jnp.where(k < 6, k + 4, k - 3))


def _coords():
    return lax.axis_index("x"), lax.axis_index("y"), lax.axis_index("c")


def _other_chips(x, y):
    return [(1 - x, y), (x, 1 - y), (1 - x, 1 - y)]


def _peer(x, y, c, mask):
    return (1 - x if mask & 4 else x, 1 - y if mask & 2 else y, 1 - c if mask & 1 else c)


def _remote(src, dst, ssem, rsem, k, to):
    return pltpu.make_async_remote_copy(src_ref=src, dst_ref=dst, send_sem=ssem.at[k], recv_sem=rsem.at[k], device_id=to,
                                        device_id_type=MESH)


def _w_half(a, ref, jj, cc):
    if a == 2:
        rp = ref.shape[1] // 4
        return ref.at[:, pl.ds(jj * rp + cc * (rp // 2), rp // 2), :]
    kin, nsh = ref.shape[0], ref.shape[1] // 4
    return ref.at[pl.ds(cc * (kin // 2), kin // 2), pl.ds(jj * nsh, nsh)]


def _cw_block(ref, jj):
    return ref.at[:, pl.ds(jj * (D // 4), D // 4)]


def _cast_place(l, pos, w_in, w_kv, w_pa, w_pb, w_pc, w_out, cw8):
    kin, nsh = w_in.shape[1], w_in.shape[2]
    nkv, rp = w_kv.shape[2], w_pa.shape[1]
    half = kin // 2

    def body(pos_r, win, wkv, pa, pb, pc, po, cw, Win, Wkv, Wp, Cw):
        Win[...] = win[...].astype(BF16)
        Wkv[...] = wkv[...].astype(BF16)

        @pl.when(pl.program_id(0) == 0)
        def _():
            for k, r in enumerate((pa, pb, pc, po)):
                Wp[k] = r[...].astype(BF16)
            Cw[...] = cw[...]

    proj = pl.BlockSpec((None, rp, D), lambda i, p: (l, 0, 0))
    return _pcall(
        body, name="cast_place", grid=(2,), prefetch=1,
        in_specs=[pl.BlockSpec((None, half, nsh), lambda i, p: (l, i, 0)), pl.BlockSpec((None, half, nkv), lambda i, p: (l, i, 0)),
                  proj, proj, proj, proj, pl.BlockSpec((None, 8, D // 4), lambda i, p: (l, 0, 0))],
        out_specs=[pl.BlockSpec((half, nsh), lambda i, p: (i, p[0])), pl.BlockSpec((half, nkv), lambda i, p: (i, p[0])),
                   pl.BlockSpec((4, rp, D), lambda i, p: (0, p[0], 0)), pl.BlockSpec((8, D // 4), lambda i, p: (0, p[0]))],
        out_shape=[SDS((kin, 4 * nsh), BF16), SDS((kin, 4 * nkv), BF16), SDS((4, 4 * rp, D), BF16), SDS((8, D), F32)],
    )(pos, w_in, w_kv, w_pa, w_pb, w_pc, w_out, cw8)


def _hbm_like(bufs):
    return [pltpu.HBM(b.shape, b.dtype) for b in bufs]


def _w_part(a, ref, jj, cc):
    return _cw_block(ref, jj) if a == 3 else _w_half(a, ref, jj, cc)


def _gather_start(name, kinds, bufs, after=None):
    n = len(kinds)
    extra = [] if after is None else [after]

    def body(*refs):
        w, ssem, rsem, token = refs[0:n], refs[n + len(extra)], refs[n + len(extra) + 1], refs[-1]
        x, y, c = _coords()
        j = 2 * x + y
        for k, chip in enumerate(_other_chips(x, y)):
            for i, a in enumerate(kinds):
                part = _w_part(a, w[i], j, c)
                _remote(part, part, ssem, rsem, i * 3 + k, (chip[0], chip[1], c)).start()
        token[...] = jnp.zeros((8, 128), F32)

    return _pcall(
        body, name=name, in_specs=[HBM] * n + [ANY] * len(extra), out_specs=[SEM, SEM] + [HBM] * n + [VMEM],
        out_shape=[pltpu.SemaphoreType.DMA((3 * n,)), pltpu.SemaphoreType.DMA((3 * n,))] + _hbm_like(bufs) + [TOKEN],
        aliases={i: 2 + i for i in range(n)}, effect=True,
    )(*[pltpu.with_memory_space_constraint(b, pltpu.HBM) for b in bufs], *extra)


def _gather_mid(name, kinds, started, after):
    n = len(kinds)
    fw = [i for i, a in enumerate(kinds) if a != 3]
    bufs = tuple(started[2:2 + n])

    def body(*refs):
        w, ssem, rsem, ssem2, rsem2, token = refs[0:n], refs[n], refs[n + 1], refs[n + 3], refs[n + 4], refs[-1]
        x, y, c = _coords()
        j = 2 * x + y
        me, sib = (x, y, c), (x, y, 1 - c)
        token[...] = jnp.zeros((8, 128), F32)
        chips = _other_chips(x, y)
        for k, chip in enumerate(chips):
            for i, a in enumerate(kinds):
                got = _w_part(a, w[i], 2 * chip[0] + chip[1], c)
                _remote(got, got, ssem, rsem, i * 3 + k, me).wait_recv()
        for k in range(3):
            for i, a in enumerate(kinds):
                part = _w_part(a, w[i], j, c)
                _remote(part, part, ssem, rsem, i * 3 + k, me).wait_send()
        for k, chip in enumerate(chips):
            for f, i in enumerate(fw):
                got = _w_half(kinds[i], w[i], 2 * chip[0] + chip[1], c)
                _remote(got, got, ssem2, rsem2, f * 3 + k, sib).start()

    return _pcall(
        body, name=name, in_specs=[HBM] * n + [SEM, SEM, ANY], out_specs=[SEM, SEM] + [HBM] * n + [VMEM],
        out_shape=[pltpu.SemaphoreType.DMA((3 * len(fw),)), pltpu.SemaphoreType.DMA((3 * len(fw),))] + _hbm_like(bufs) + [TOKEN],
        aliases={i: 2 + i for i in range(n)}, effect=True,
    )(*bufs, started[0], started[1], after)


def _gather_end(name, kinds, mid, after):
    n = len(kinds)
    fw = [i for i, a in enumerate(kinds) if a != 3]
    bufs = tuple(mid[2:2 + n])

    def body(*refs):
        w, ssem2, rsem2 = refs[0:n], refs[n], refs[n + 1]
        x, y, c = _coords()
        me = (x, y, c)
        for k, chip in enumerate(_other_chips(x, y)):
            for f, i in enumerate(fw):
                got = _w_half(kinds[i], w[i], 2 * chip[0] + chip[1], 1 - c)
                _remote(got, got, ssem2, rsem2, f * 3 + k, me).wait_recv()
                sent = _w_half(kinds[i], w[i], 2 * chip[0] + chip[1], c)
                _remote(sent, sent, ssem2, rsem2, f * 3 + k, me).wait_send()

    return _pcall(
        body, name=name, in_specs=[HBM] * n + [SEM, SEM, ANY], out_specs=[HBM] * n, out_shape=_hbm_like(bufs),
        aliases={i: i for i in range(n)}, effect=True,
    )(*bufs, mid[0], mid[1], after)


def _g_piece(a, ref, jd, dc):
    if a == 2:
        rp = ref.shape[1] // 4
        return ref.at[:, pl.ds(jd * rp + dc * (rp // 2), rp // 2), :]
    kin, nsh = ref.shape[0], ref.shape[1] // 4
    return ref.at[pl.ds(dc * (kin // 2), kin // 2), pl.ds(jd * nsh, nsh)]


def _a2a_start(name, kinds, grads, lands):
    n = len(kinds)

    def body(*refs):
        g, ld, ssem, rsem, token = refs[0:n], refs[n:2 * n], refs[2 * n], refs[2 * n + 1], refs[-1]
        x, y, c = _coords()
        for mask in range(1, NDEV):
            p = _peer(x, y, c, mask)
            for i, a in enumerate(kinds):
                _remote(_g_piece(a, g[i], 2 * p[0] + p[1], p[2]), ld[i].at[mask - 1], ssem, rsem, i * 7 + mask - 1, p).start()
        token[...] = jnp.zeros((8, 128), F32)

    bufs = tuple(grads) + tuple(lands)
    return _pcall(
        body, name=name, in_specs=[HBM] * (2 * n), out_specs=[SEM, SEM] + [HBM] * (2 * n) + [VMEM],
        out_shape=[pltpu.SemaphoreType.DMA((7 * n,)), pltpu.SemaphoreType.DMA((7 * n,))] + _hbm_like(bufs) + [TOKEN],
        aliases={i: 2 + i for i in range(2 * n)}, effect=True,
    )(*[pltpu.with_memory_space_constraint(b, pltpu.HBM) for b in bufs])


def _a2a_wait(name, kinds, started, after):
    n = len(kinds)
    ssem, rsem = started[0], started[1]
    bufs = tuple(started[2:2 + 2 * n])

    def body(*refs):
        g, ld, ssem, rsem = refs[0:n], refs[n:2 * n], refs[2 * n], refs[2 * n + 1]
        x, y, c = _coords()
        me = (x, y, c)
        for mask in range(1, NDEV):
            for i in range(n):
                got = ld[i].at[mask - 1]
                _remote(got, got, ssem, rsem, i * 7 + mask - 1, me).wait_recv()
        for mask in range(1, NDEV):
            p = _peer(x, y, c, mask)
            for i, a in enumerate(kinds):
                sent = _g_piece(a, g[i], 2 * p[0] + p[1], p[2])
                _remote(sent, sent, ssem, rsem, i * 7 + mask - 1, me).wait_send()

    return _pcall(
        body, name=name, in_specs=[HBM] * (2 * n) + [SEM, SEM, ANY], out_specs=[HBM] * (2 * n), out_shape=_hbm_like(bufs),
        aliases={i: i for i in range(2 * n)}, effect=True,
    )(*bufs, ssem, rsem, after)


def _sum_share(pos, lands, grads):
    rows, n = 128, 4
    widths = [ld.shape[2] for ld in lands]

    def body(pos_r, l0, w0, l1, w1, l2, w2, g0, g1, g2, b0, b1, b2, lsem, ssem, rsem):
        i = pl.program_id(0)
        x, y, c = _coords()
        sib = (x, y, 1 - c)
        ld, ow, gs, bufs = (l0, l1, l2), (w0, w1, w2), (g0, g1, g2), (b0, b1, b2)

        def dst(a, step):
            row = step * (2 * rows) + c * rows if a == 2 else c * (n * rows) + step * rows
            return gs[a].at[pl.ds(row, rows), :]

        def copies(a, step, sl):
            src = bufs[a].at[sl]
            lc = pltpu.make_async_copy(src, dst(a, step), lsem.at[a, sl])
            rc = pltpu.make_async_remote_copy(src_ref=src, dst_ref=dst(a, step), send_sem=ssem.at[a, sl], recv_sem=rsem.at[a],
                                              device_id=sib, device_id_type=MESH)
            return lc, rc

        def drain(a, step, sl):
            lc, rc = copies(a, step, sl)
            lc.wait()
            rc.wait_send()

        slot = i % 2

        @pl.when(i >= 2)
        def _():
            for a in range(3):
                drain(a, i - 2, slot)

        for a in range(3):
            acc = ow[a][...].astype(F32)
            for k in range(NDEV - 1):
                acc = acc + ld[a][k].astype(F32)
            bufs[a][slot] = acc
            lc, rc = copies(a, i, slot)
            lc.start()
            rc.start()

        @pl.when(i == n - 1)
        def _():
            for a in range(3):
                drain(a, n - 2, (n - 2) % 2)
                drain(a, n - 1, (n - 1) % 2)
                whole = gs[a].at[pl.ds(0, n * rows), :]
                pltpu.make_async_remote_copy(src_ref=whole, dst_ref=whole, send_sem=ssem.at[a, 0], recv_sem=rsem.at[a],
                                             device_id=(x, y, c), device_id_type=MESH).wait_recv()

    land = lambda w: pl.BlockSpec((NDEV - 1, rows, w), lambda i, p: (0, i, 0))
    in_specs = [land(widths[0]), pl.BlockSpec((rows, widths[0]), lambda i, p: (p[1] * n + i, p[0])),
                land(widths[1]), pl.BlockSpec((rows, widths[1]), lambda i, p: (p[1] * n + i, p[0])),
                land(widths[2]), pl.BlockSpec((None, rows, widths[2]), lambda i, p: (i, 2 * p[0] + p[1], 0))]
    args = [t for pair in zip(lands, grads) for t in pair]
    return _pcall(
        body, name="sum_share", grid=(n,), prefetch=1, in_specs=in_specs, out_specs=[HBM] * 3,
        out_shape=[SDS((2 * n * rows, w), F32) for w in widths],
        scratch=[pltpu.VMEM((2, rows, w), F32) for w in widths]
        + [pltpu.SemaphoreType.DMA((3, 2)), pltpu.SemaphoreType.DMA((3, 2)), pltpu.SemaphoreType.DMA((3,))],
    )(pos, *args)


def _dev_index(p):
    return 4 * p[0] + 2 * p[1] + p[2]


def _small_rows(ref, d):
    r8 = ref.shape[0] // NDEV
    return ref.at[pl.ds(d * r8, r8), :]


def _rs_start(pack, land):
    def body(p_ref, ld, ssem, rsem, o0, o1, token):
        x, y, c = _coords()
        for mask in range(1, NDEV):
            p = _peer(x, y, c, mask)
            _remote(_small_rows(p_ref, _dev_index(p)), ld.at[mask - 1], ssem, rsem, mask - 1, p).start()
        token[...] = jnp.zeros((8, 128), F32)

    bufs = (pack, land)
    return _pcall(
        body, name="rs_start", in_specs=[HBM] * 2, out_specs=[SEM, SEM, HBM, HBM, VMEM],
        out_shape=[pltpu.SemaphoreType.DMA((NDEV - 1,)), pltpu.SemaphoreType.DMA((NDEV - 1,))] + _hbm_like(bufs) + [TOKEN],
        aliases={0: 2, 1: 3}, effect=True,
    )(*[pltpu.with_memory_space_constraint(b, pltpu.HBM) for b in bufs])


def _rs_wait(started, after):
    ssem, rsem, pack, land, _ = started

    def body(p_ref, ld, ssem, rsem, after_r, o0, o1):
        x, y, c = _coords()
        for mask in range(1, NDEV):
            got = ld.at[mask - 1]
            _remote(got, got, ssem, rsem, mask - 1, (x, y, c)).wait_recv()
        for mask in range(1, NDEV):
            sent = _small_rows(p_ref, _dev_index(_peer(x, y, c, mask)))
            _remote(sent, sent, ssem, rsem, mask - 1, (x, y, c)).wait_send()

    return _pcall(body, name="rs_wait", in_specs=[HBM, HBM, SEM, SEM, ANY], out_specs=[HBM, HBM],
                  out_shape=_hbm_like((pack, land)), aliases={0: 0, 1: 1}, effect=True)(pack, land, ssem, rsem, after)


def _small_sum(me1, pack, land):
    R = pack.shape[0]
    r8 = R // NDEV

    def body(me_r, p_ref, ld, full):
        acc = p_ref[...]
        for k in range(NDEV - 1):
            acc = acc + ld[k]
        full[...] = acc

    own = pl.BlockSpec((r8, 128), lambda i, m: (m[0], 0))
    return _pcall(body, name="small_sum", grid=(1,), prefetch=1,
                  in_specs=[own, pl.BlockSpec((NDEV - 1, r8, 128), lambda i, m: (0, 0, 0))], out_specs=own,
                  out_shape=SDS((R, 128), F32), vmem_mb=32)(me1, pack, land)


def _ag_start(full):
    def body(f_ref, ssem, rsem, o0, token):
        x, y, c = _coords()
        mine = _small_rows(f_ref, _dev_index((x, y, c)))
        for mask in range(1, NDEV):
            _remote(mine, mine, ssem, rsem, mask - 1, _peer(x, y, c, mask)).start()
        token[...] = jnp.zeros((8, 128), F32)

    return _pcall(
        body, name="ag_start", in_specs=[HBM], out_specs=[SEM, SEM, HBM, VMEM],
        out_shape=[pltpu.SemaphoreType.DMA((NDEV - 1,)), pltpu.SemaphoreType.DMA((NDEV - 1,))] + _hbm_like((full,)) + [TOKEN],
        aliases={0: 2}, effect=True,
    )(pltpu.with_memory_space_constraint(full, pltpu.HBM))


def _ag_wait(started, after):
    ssem, rsem, full, _ = started

    def body(f_ref, ssem, rsem, after_r, o0):
        x, y, c = _coords()
        mine = _small_rows(f_ref, _dev_index((x, y, c)))
        for mask in range(1, NDEV):
            got = _small_rows(f_ref, _dev_index(_peer(x, y, c, mask)))
            _remote(got, got, ssem, rsem, mask - 1, (x, y, c)).wait_recv()
            _remote(mine, mine, ssem, rsem, mask - 1, (x, y, c)).wait_send()

    return _pcall(body, name="ag_wait", in_specs=[HBM, SEM, SEM, ANY], out_specs=[HBM], out_shape=_hbm_like((full,)),
                  aliases={0: 0}, effect=True)(full, ssem, rsem, after)[0]


def _mm_in(x, g, w):
    S = x.shape[0]
    tm, tn = min(1024, S), 2560

    def body(x_ref, g_ref, w_ref, z_ref, h_ref, hs):
        @pl.when(pl.program_id(1) == 0)
        def _():
            xv = x_ref[...]
            hb = (xv * _rms_scale(xv) * g_ref[...]).astype(BF16)
            hs[...] = hb
            h_ref[...] = hb

        z_ref[...] = _dot(hs[...], w_ref[...]).astype(BF16)

    return _pcall(
        body, name="mm_in", grid=(S // tm, NIN // tn),
        in_specs=[pl.BlockSpec((tm, D), lambda i, j: (i, 0)), _full((1, D)), pl.BlockSpec((D, tn), lambda i, j: (0, j))],
        out_specs=[pl.BlockSpec((tm, tn), lambda i, j: (i, j)), pl.BlockSpec((tm, D), lambda i, j: (i, 0))],
        out_shape=[SDS((S, NIN), BF16), SDS((S, D), BF16)], scratch=[pltpu.VMEM((tm, D), BF16)], vmem_mb=56,
    )(x, g, w)


def _chunk_mask():
    ri = lax.broadcasted_iota(jnp.int32, (GB, GB), 0)
    ci = lax.broadcasted_iota(jnp.int32, (GB, GB), 1)
    return (ri >= 64) | (ci < 64)


def _layernorm_parts(v):
    mu = jnp.mean(v, axis=-1, keepdims=True)
    d = v - mu
    rs = lax.rsqrt(jnp.mean(d * d, axis=-1, keepdims=True) + EPS)
    return d * rs, rs


def _branch_a(z, lg, lb, ws, bsb):
    S = z.shape[0]
    T = min(512, S)

    def body(zu, zv, zg, lg_r, lb_r, ws_r, bs_r, ya):
        vhat, _ = _layernorm_parts(zv[...].astype(F32))
        vnb = (vhat * lg_r[...] + lb_r[...]).astype(BF16)
        sil, _ = _silu_parts(zg[...].astype(F32))
        t = zu[...].astype(F32) * sil
        mask = _chunk_mask()
        for g in range(NG):
            wg = jnp.where(mask, ws_r[g], 0.0).astype(BF16)
            cs = slice(g * GB, (g + 1) * GB)
            for n in range(T // GB):
                rs = slice(n * GB, (n + 1) * GB)
                sv = _dot(wg, vnb[rs, cs]) + bs_r[g]
                ya[rs, cs] = (t[rs, cs] * sv).astype(BF16)

    zs = lambda k: pl.BlockSpec((T, D), lambda i: (i, k))
    return _pcall(
        body, name="branch_a", grid=(S // T,),
        in_specs=[zs(0), zs(1), zs(2), _full((1, D)), _full((1, D)), _full((NG, GB, GB)), _full((NG, GB, GB))],
        out_specs=pl.BlockSpec((T, D), lambda i: (i, 0)), out_shape=SDS((S, D), BF16),
    )(z, z, z, lg, lb, ws, bsb)


def _softplus_neg(lam):
    e = jnp.exp(-jnp.abs(lam))
    l1p = jnp.where(e < 1e-2, e * (1.0 - e * (0.5 - e * (1.0 / 3.0))), jnp.log(1.0 + e))
    return jnp.maximum(-lam, 0.0) + l1p


CH = 16


def _ck(c, off=0):
    return pl.ds(c * CH + off, CH)


def _half_sum(v):
    return v[0:8, :] + v[8:16, :]


def _lru_gates(pr, pi, br, bi, sp8):
    r = jax.nn.sigmoid(pr + br)
    ig = _sigmoid(pi + bi)
    la = sp8 * r
    a = jnp.exp(la)
    a2 = a * a
    mult = jnp.sqrt(-jnp.tanh(la) * (a2 + 1.0))
    return r, ig, a, a2, mult


def _tile_rows():
    return lax.broadcasted_iota(jnp.int32, (8, D), 0)


def _scan_forward(a_s, u_s, h_s, hcar, T):
    row = _tile_rows()

    def tile(i, hp):
        o = pl.multiple_of(i * 8, 8)
        A = a_s[pl.ds(o, 8), :]
        U = u_s[pl.ds(o, 8), :]
        for s in (1, 2, 4):
            m = row >= s
            U = jnp.where(m, U + A * pltpu.roll(U, s, 0), U)
            A = jnp.where(m, A * pltpu.roll(A, s, 0), A)
        H = U + A * hp
        h_s[pl.ds(o, 8), :] = H
        return jnp.broadcast_to(H[7:8, :], (8, D))

    hcar[...] = lax.fori_loop(0, T // 8, tile, hcar[...])


def _scan_reverse(b_s, d_s, l_s, lcar, T):
    row = _tile_rows()
    n = T // 8

    def tile(i, lp):
        o = pl.multiple_of((n - 1 - i) * 8, 8)
        B = b_s[pl.ds(o, 8), :]
        U = d_s[pl.ds(o, 8), :]
        for s in (1, 2, 4):
            m = row < 8 - s
            U = jnp.where(m, U + B * pltpu.roll(U, 8 - s, 0), U)
            B = jnp.where(m, B * pltpu.roll(B, 8 - s, 0), B)
        Lm = U + B * lp
        l_s[pl.ds(o, 8), :] = Lm
        return jnp.broadcast_to(Lm[0:1, :], (8, D))

    lcar[...] = lax.fori_loop(0, n, tile, lcar[...])


def _branch_b(z, cw, cb, wr, br, wi, bi, lam):
    S = z.shape[0]
    T = min(256, S)

    def body(zxb, zgb, cw_r, cb_r, wr_r, br_r, wi_r, bi_r, lam_r, yb, hs_o, xc_o, r_o, ig_o, a_o, m_o, xpad, u_s, hcar):
        @pl.when(pl.program_id(0) == 0)
        def _():
            xpad[pl.ds(0, 8), :] = jnp.zeros((8, D), F32)
            hcar[...] = jnp.zeros((8, D), F32)

        cw = cw_r[...]
        xpad[pl.ds(8, T), :] = zxb[...].astype(F32)
        xk = [xpad[pl.ds(5 + k, T), :] for k in range(4)]
        xc = cb_r[...] + (((xk[0] * cw[0:1] + xk[1] * cw[1:2]) + xk[2] * cw[2:3]) + xk[3] * cw[3:4])
        xcb = xc.astype(BF16)
        pr, pi = [], []
        for h in range(NG):
            cs = slice(h * GB, (h + 1) * GB)
            pr.append(_dot(xcb[:, cs], wr_r[h].astype(BF16)))
            pi.append(_dot(xcb[:, cs], wi_r[h].astype(BF16)))
        r, ig, a, _, mult = _lru_gates(jnp.concatenate(pr, axis=1), jnp.concatenate(pi, axis=1), br_r[...], bi_r[...],
                                       -LRU_C * _softplus_neg(lam_r[...]))
        for o_ref, val in ((xc_o, xc), (r_o, r), (ig_o, ig), (a_o, a), (m_o, mult)):
            o_ref[...] = val
        u_s[...] = mult * (ig * xc)
        _scan_forward(a_o, u_s, hs_o, hcar, T)
        xpad[pl.ds(0, 8), :] = xpad[pl.ds(T, 8), :]
        sil, _ = _silu_parts(zgb[...].astype(F32))
        yb[...] = (hs_o[...] * sil).astype(BF16)

    zs = lambda k: pl.BlockSpec((T, D), lambda i: (i, k))
    row = pl.BlockSpec((T, D), lambda i: (i, 0))
    return _pcall(
        body, name="branch_b", grid=(S // T,),
        in_specs=[zs(3), zs(4), _full((8, D)), _full((1, D)), _full((NG, GB, GB)), _full((1, D)), _full((NG, GB, GB)),
                  _full((1, D)), _full((1, D))],
        out_specs=[row] * 7, out_shape=[SDS((S, D), BF16)] + [SDS((S, D), F32)] * 6,
        scratch=[pltpu.VMEM((T + 8, D), F32), pltpu.VMEM((T, D), F32), pltpu.VMEM((8, D), F32)],
    )(z, z, cw, cb, wr, br, wi, bi, lam)


def _kv(mem, g, wkv):
    def body(m_ref, g_ref, w_ref, kv_ref):
        m = m_ref[...]
        mn = (m * _rms_scale(m) * g_ref[...]).astype(BF16)
        kv_ref[...] = _dot(mn, w_ref[...]).astype(BF16)

    return _pcall(body, name="mem_kv", in_specs=[VMEM] * 3, out_specs=VMEM, out_shape=SDS((MEM, 2 * D), BF16),
                  vmem_mb=32)(mem, g, wkv)


def _softmax_rows(s):
    e = jnp.exp(s - jnp.max(s, axis=-1, keepdims=True))
    return e / jnp.sum(e, axis=-1, keepdims=True)


def _branch_c(z, kv):
    S = z.shape[0]
    T = min(512, S)

    def body(zq, zg, kv_r, yc, p_o):
        sil, _ = _silu_parts(zg[...].astype(F32))
        for h in range(NH):
            cs = slice(h * HD, (h + 1) * HD)
            pb = _softmax_rows(_dot_nt(zq[:, cs], kv_r[:, cs]) * (HD ** -0.5)).astype(BF16)
            p_o[:, cs] = pb
            att = _dot(pb, kv_r[:, D + h * HD:D + (h + 1) * HD])
            yc[:, cs] = (att * sil[:, cs]).astype(BF16)

    zs = lambda k: pl.BlockSpec((T, D), lambda i: (i, k))
    row = pl.BlockSpec((T, D), lambda i: (i, 0))
    return _pcall(body, name="branch_c", grid=(S // T,), in_specs=[zs(5), zs(6), _full((MEM, 2 * D))],
                  out_specs=[row, row], out_shape=[SDS((S, D), BF16)] * 2)(z, z, kv)


def _merge_out(ya, yb, yc, z, wp, x, pg):
    S = x.shape[0]
    T = min(256, S)

    def body(ya_r, yb_r, yc_r, m0, m1, m2, wp_r, x_r, pg_r, pa_o, pb_o, pc_o, mg_o, o_o, xn_o):
        merged = None
        for y_r, ml, p_o, k in ((ya_r, m0, pa_o, 0), (yb_r, m1, pb_o, 1), (yc_r, m2, pc_o, 2)):
            yv = y_r[...]
            mg_o[k] = yv
            p = _dot(yv, wp_r[k])
            p_o[...] = p.astype(BF16)
            t = _sigmoid(ml[...].astype(F32)) * p
            merged = t if merged is None else merged + t
        mb = merged.astype(BF16)
        mg_o[3] = mb
        o = _dot(mb, wp_r[3])
        o_o[...] = o.astype(BF16)
        xn_o[...] = x_r[...] + o * _rms_scale(o) * pg_r[...]

    row = pl.BlockSpec((T, D), lambda i: (i, 0))
    zs = lambda k: pl.BlockSpec((T, D), lambda i: (i, k))
    return _pcall(
        body, name="merge_out", grid=(S // T,),
        in_specs=[row, row, row, zs(7), zs(8), zs(9), _full((4, D, D)), row, _full((1, D))],
        out_specs=[row] * 3 + [pl.BlockSpec((4, T, D), lambda i: (0, i, 0)), row, row],
        out_shape=[SDS((S, D), BF16)] * 3 + [SDS((4, S, D), BF16), SDS((S, D), BF16), SDS((S, D), F32)], vmem_mb=56,
    )(ya, yb, yc, z, z, z, wp, x, pg)


def _loss_head(y, t):
    S = y.shape[0]
    T = min(512, S)

    def body(y_r, t_r, loss_o, dy_o):
        @pl.when(pl.program_id(0) == 0)
        def _():
            loss_o[...] = jnp.zeros((1, 1), F32)

        e = y_r[...] - t_r[...]
        dy_o[...] = e * (1.0 / D)
        loss_o[...] += 0.5 * _rowsum(jnp.sum(e * e, axis=1, keepdims=True) * (1.0 / D))

    row = pl.BlockSpec((T, D), lambda i: (i, 0))
    return _pcall(body, name="loss_head", grid=(S // T,), in_specs=[row, row], out_specs=[_full((1, 1)), row],
                  out_shape=[SDS((1, 1), F32), SDS((S, D), F32)])(y, t)


def _accumulate(first, ref, val):
    @pl.when(first)
    def _():
        ref[...] = val

    @pl.when(jnp.logical_not(first))
    def _():
        ref[...] += val


def _out_bwd(dxn, o, pg, wp, z, pa, pb, pc):
    S = dxn.shape[0]
    T = min(256, S)

    def body(dy_r, o_r, pg_r, wp_r, m0, m1, m2, pa_r, pb_r, pc_r, dp4_o, dya_o, dyb_o, dyc_o, dz_o, dg_o):
        dy = dy_r[...]
        o = o_r[...].astype(F32)
        r2 = _rms_scale(o)
        w = dy * pg_r[...]
        do = r2 * w - o * (r2 * r2 * r2) * jnp.mean(w * o, axis=-1, keepdims=True)
        _accumulate(pl.program_id(0) == 0, dg_o, _rowsum(dy * o * r2))
        dob = do.astype(BF16)
        dp4_o[3] = dob
        dm = _dot_nt(dob, wp_r[3])
        for k, (ml, p_r, dy_o) in enumerate(((m0, pa_r, dya_o), (m1, pb_r, dyb_o), (m2, pc_r, dyc_o))):
            gk = _sigmoid(ml[...].astype(F32))
            dz_o[k] = (dm * p_r[...].astype(F32) * gk * (1.0 - gk)).astype(BF16)
            dpk = (gk * dm).astype(BF16)
            dp4_o[k] = dpk
            dy_o[...] = _dot_nt(dpk, wp_r[k]).astype(BF16)

    row = pl.BlockSpec((T, D), lambda i: (i, 0))
    zs = lambda k: pl.BlockSpec((T, D), lambda i: (i, k))
    return _pcall(
        body, name="out_bwd", grid=(S // T,),
        in_specs=[row, row, _full((1, D)), _full((4, D, D)), zs(7), zs(8), zs(9), row, row, row],
        out_specs=[pl.BlockSpec((4, T, D), lambda i: (0, i, 0))] + [row] * 3 + [pl.BlockSpec((3, T, D), lambda i: (1, i, 0)), _full((1, D))],
        out_shape=[SDS((4, S, D), BF16)] + [SDS((S, D), BF16)] * 3 + [SDS((10, S, D), BF16), SDS((1, D), F32)], vmem_mb=56,
    )(dxn, o, pg, wp, z, z, z, pa, pb, pc)


def _branch_a_bwd(z, dya, lg, lb, ws, bsb, dz):
    S = z.shape[0]
    T = min(512, S)
    nblk = S // T

    def body(zu, zv, zg, dy_r, lg_r, lb_r, ws_r, bs_r, dz_in, dz_o, dws_o, dbs_o, dlg_o, dlb_o, dvn_s, bacc):
        i = pl.program_id(0)

        @pl.when(i == 0)
        def _():
            dws_o[...] = jnp.zeros((NG, GB, GB), F32)
            bacc[...] = jnp.zeros((NG, GB, GB), F32)

        vhat, rs = _layernorm_parts(zv[...].astype(F32))
        vnb = (vhat * lg_r[...] + lb_r[...]).astype(BF16)
        ga = zg[...].astype(F32)
        sil, dsil = _silu_parts(ga)
        u = zu[...].astype(F32)
        dy = dy_r[...].astype(F32)
        t = dy * sil
        dsv_all = t * u
        dga_pre = dy * u * dsil
        mask = _chunk_mask()
        for g in range(NG):
            wf = jnp.where(mask, ws_r[g], 0.0)
            wg = wf.astype(BF16)
            wgt = wf.T.astype(BF16)
            cs = slice(g * GB, (g + 1) * GB)
            dw = jnp.zeros((GB, GB), F32)
            db = jnp.zeros((GB, GB), F32)
            for n in range(T // GB):
                rsl = slice(n * GB, (n + 1) * GB)
                vb = vnb[rsl, cs]
                sv = _dot(wg, vb) + bs_r[g]
                dz_o[0, rsl, cs] = (t[rsl, cs] * sv).astype(BF16)
                dz_o[2, rsl, cs] = (dga_pre[rsl, cs] * sv).astype(BF16)
                dsv = dsv_all[rsl, cs]
                dsb = dsv.astype(BF16)
                dvn_s[rsl, cs] = _dot(wgt, dsb)
                dw = dw + _dot_nt(dsb, vb)
                db = db + dsv
            dws_o[g] += jnp.where(mask, dw, 0.0)
            bacc[g] += db
        dvn = dvn_s[...]
        dvh = dvn * lg_r[...]
        dv = rs * (dvh - jnp.mean(dvh, axis=-1, keepdims=True) - vhat * jnp.mean(dvh * vhat, axis=-1, keepdims=True))
        dz_o[1] = dv.astype(BF16)
        _accumulate(i == 0, dlg_o, _rowsum(dvn * vhat))
        _accumulate(i == 0, dlb_o, _rowsum(dvn))

        @pl.when(i == nblk - 1)
        def _():
            for g in range(NG):
                dbs_o[g:g + 1, :] = _rowsum(bacc[g].T)

    zs = lambda k: pl.BlockSpec((T, D), lambda i: (i, k))
    return _pcall(
        body, name="branch_a_bwd", grid=(nblk,),
        in_specs=[zs(0), zs(1), zs(2), pl.BlockSpec((T, D), lambda i: (i, 0)), _full((1, D)), _full((1, D)),
                  _full((NG, GB, GB)), _full((NG, GB, GB)), HBM],
        out_specs=[pl.BlockSpec((3, T, D), lambda i: (0, i, 0)), _full((NG, GB, GB)), _full((NG, GB)), _full((1, D)),
                   _full((1, D))],
        out_shape=[SDS((10, S, D), BF16), SDS((NG, GB, GB), F32), SDS((NG, GB), F32), SDS((1, D), F32), SDS((1, D), F32)],
        scratch=[pltpu.VMEM((T, D), F32), pltpu.VMEM((NG, GB, GB), F32)], aliases={8: 0},
    )(z, z, z, dya, lg, lb, ws, bsb, dz)


def _branch_b_bwd(z, hs, lru, dyb, cw, wr, wi, lam, dz):
    S = z.shape[0]
    T = min(256, S)
    nblk = S // T

    def body(zxb, zprev, zgb, hs_r, hprev_r, dy_r, xc_r, r_r, ig_r, a_r, m_r, cw_r, wr_r, wi_r, lam_r, dz_in,
             dz_o, dcw_o, dcb_o, dwr_o, dbr_o, dwi_o, dbi_o, dlam_o, xpad, hpad, apad, dpad,
             b_s, d_s, l_s, back_s, xcb_s, dprb_s, dpib_s, lcar):
        i = pl.program_id(0)
        blk = nblk - 1 - i
        first = i == 0

        @pl.when(first)
        def _():
            apad[pl.ds(T, 8), :] = jnp.zeros((8, D), F32)
            dpad[pl.ds(T, 8), :] = jnp.zeros((8, D), F32)
            lcar[...] = jnp.zeros((8, D), F32)
            dcw_o[...] = jnp.zeros((8, D), F32)
            dwr_o[...] = jnp.zeros((NG, GB, GB), F32)
            dwi_o[...] = jnp.zeros((NG, GB, GB), F32)

        keep = (blk > 0).astype(F32)
        nck = T // CH
        cw, lam = cw_r[...], lam_r[...]
        sp8 = -LRU_C * _softplus_neg(lam)
        xpad[pl.ds(0, 8), :] = zprev[...].astype(F32)[8:16, :] * keep
        hpad[pl.ds(0, 8), :] = hprev_r[...] * keep
        for c in range(nck):
            xpad[_ck(c, 8), :] = zxb[_ck(c), :].astype(F32)
            hpad[_ck(c, 8), :] = hs_r[_ck(c), :]
            xcb_s[_ck(c), :] = xc_r[_ck(c), :].astype(BF16)
            apad[_ck(c), :] = a_r[_ck(c), :]
            sil, dsil = _silu_parts(zgb[_ck(c), :].astype(F32))
            dy = dy_r[_ck(c), :].astype(F32)
            dz_o[1, _ck(c), :] = (dy * hs_r[_ck(c), :] * dsil).astype(BF16)
            d_s[_ck(c), :] = dy * sil
        for c in range(nck):
            b_s[_ck(c), :] = apad[_ck(c, 1), :]
        _scan_reverse(b_s, d_s, l_s, lcar, T)
        s_sp = s_br = s_bi = jnp.zeros((8, D), F32)
        for c in range(nck):
            lm, r, ig, mult, a, xc = l_s[_ck(c), :], r_r[_ck(c), :], ig_r[_ck(c), :], m_r[_ck(c), :], a_r[_ck(c), :], xc_r[_ck(c), :]
            t = lm * mult
            dpad[_ck(c), :] = t * ig
            dl = lm * hpad[_ck(c, 7), :] * a - (lm * ig * xc) * (a * a) / mult
            dpr = dl * sp8 * r * (1.0 - r)
            dpi = t * xc * ig * (1.0 - ig)
            s_sp = s_sp + _half_sum(dl * r)
            s_br = s_br + _half_sum(dpr)
            s_bi = s_bi + _half_sum(dpi)
            dprb_s[_ck(c), :] = dpr.astype(BF16)
            dpib_s[_ck(c), :] = dpi.astype(BF16)
        _accumulate(first, dlam_o, _rowsum(s_sp) * (LRU_C * jax.nn.sigmoid(-lam)))
        _accumulate(first, dbr_o, _rowsum(s_br))
        _accumulate(first, dbi_o, _rowsum(s_bi))
        for h in range(NG):
            cs = slice(h * GB, (h + 1) * GB)
            back_s[:, cs] = _dot_nt(dprb_s[:, cs], wr_r[h].astype(BF16)) + _dot_nt(dpib_s[:, cs], wi_r[h].astype(BF16))
            dwr_o[h] += _dot_tn(xcb_s[:, cs], dprb_s[:, cs])
            dwi_o[h] += _dot_tn(xcb_s[:, cs], dpib_s[:, cs])
        s_cb = jnp.zeros((8, D), F32)
        s_cw = [jnp.zeros((8, D), F32)] * 4
        for c in range(nck):
            dxc = dpad[_ck(c), :] + back_s[_ck(c), :]
            dpad[_ck(c), :] = dxc
            s_cb = s_cb + _half_sum(dxc)
            s_cw = [s_cw[k] + _half_sum(xpad[_ck(c, 5 + k), :] * dxc) for k in range(4)]
        _accumulate(first, dcb_o, _rowsum(s_cb))
        for k in range(4):
            dcw_o[k:k + 1, :] += _rowsum(s_cw[k])
        for c in range(nck):
            dxb = ((dpad[_ck(c, 3), :] * cw[0:1] + dpad[_ck(c, 2), :] * cw[1:2]) + dpad[_ck(c, 1), :] * cw[2:3]) + dpad[_ck(c), :] * cw[3:4]
            dz_o[0, _ck(c), :] = dxb.astype(BF16)
        apad[pl.ds(T, 8), :] = apad[pl.ds(0, 8), :]
        dpad[pl.ds(T, 8), :] = dpad[pl.ds(0, 8), :]

    rev = lambda k: pl.BlockSpec((T, D), lambda i: (nblk - 1 - i, k))
    prev16 = pl.BlockSpec((16, D), lambda i: (jnp.maximum((nblk - 1 - i) * (T // 16) - 1, 0), 3))
    prev8 = pl.BlockSpec((8, D), lambda i: (jnp.maximum((nblk - 1 - i) * (T // 8) - 1, 0), 0))
    vec, mat = _full((1, D)), _full((NG, GB, GB))
    return _pcall(
        body, name="branch_b_bwd", grid=(nblk,),
        in_specs=[rev(3), prev16, rev(4), rev(0), prev8] + [rev(0)] * 6 + [_full((8, D)), mat, mat, vec, HBM],
        out_specs=[pl.BlockSpec((2, T, D), lambda i: (3, nblk - 1 - i, 0)), _full((8, D)), vec, mat, vec, mat, vec, vec],
        out_shape=[SDS((10, S, D), BF16), SDS((8, D), F32), SDS((1, D), F32), SDS((NG, GB, GB), F32), SDS((1, D), F32),
                   SDS((NG, GB, GB), F32), SDS((1, D), F32), SDS((1, D), F32)],
        scratch=[pltpu.VMEM((T + 8, D), F32)] * 4 + [pltpu.VMEM((T, D), F32)] * 4 + [pltpu.VMEM((T, D), BF16)] * 3
        + [pltpu.VMEM((8, D), F32)],
        aliases={15: 0}, vmem_mb=56,
    )(z, z, z, hs, hs, dyb, *lru, cw, wr, wi, lam, dz)


def _branch_c_bwd(z, kv, pw, dyc, dz):
    S = z.shape[0]
    T = min(512, S)

    def body(zq, zg, kv_r, p_r, dy_r, dz_in, dz_o, dkv_o):
        @pl.when(pl.program_id(0) == 0)
        def _():
            dkv_o[...] = jnp.zeros((MEM, 2 * D), F32)

        gc = zg[...].astype(F32)
        sil, dsil = _silu_parts(gc)
        dy = dy_r[...].astype(F32)
        datt = dy * sil
        dgc_pre = dy * dsil
        scale = HD ** -0.5
        for h in range(NH):
            cs = slice(h * HD, (h + 1) * HD)
            vs = slice(D + h * HD, D + (h + 1) * HD)
            qh = zq[:, cs]
            pb = p_r[:, cs]
            p = pb.astype(F32)
            att = _dot(pb, kv_r[:, vs])
            dz_o[1, :, cs] = (dgc_pre[:, cs] * att).astype(BF16)
            dab = datt[:, cs].astype(BF16)
            dp = _dot_nt(dab, kv_r[:, vs])
            ds = (p * (dp - jnp.sum(p * dp, axis=-1, keepdims=True)) * scale).astype(BF16)
            dz_o[0, :, cs] = _dot(ds, kv_r[:, cs]).astype(BF16)
            dkv_o[:, cs] += _dot_tn(ds, qh)
            dkv_o[:, vs] += _dot_tn(pb, dab)

    zs = lambda k: pl.BlockSpec((T, D), lambda i: (i, k))
    return _pcall(
        body, name="branch_c_bwd", grid=(S // T,),
        in_specs=[zs(5), zs(6), _full((MEM, 2 * D))] + [pl.BlockSpec((T, D), lambda i: (i, 0))] * 2 + [HBM],
        out_specs=[pl.BlockSpec((2, T, D), lambda i: (4, i, 0)), _full((MEM, 2 * D))],
        out_shape=[SDS((10, S, D), BF16), SDS((MEM, 2 * D), F32)], aliases={5: 0},
    )(z, z, kv, pw, dyc, dz)


def _mm_dh(dz, w, x, dxn, g):
    S = x.shape[0]
    tm = min(1024, S)

    def body(dz_r, w_r, x_r, dxn_r, g_r, dx_o, dg_o, acc):
        i, k = pl.program_id(0), pl.program_id(1)
        _accumulate(k == 0, acc, _dot_nt(dz_r[0], w_r[...]))

        @pl.when(k == 9)
        def _():
            dh = acc[...]
            xv = x_r[...]
            r1 = _rms_scale(xv)
            wv = dh * g_r[...]
            dx_o[...] = dxn_r[...] + r1 * wv - xv * (r1 * r1 * r1) * jnp.mean(wv * xv, axis=-1, keepdims=True)
            _accumulate(i == 0, dg_o, _rowsum(dh * xv * r1))

    row = pl.BlockSpec((tm, D), lambda i, k: (i, 0))
    return _pcall(
        body, name="mm_dh", grid=(S // tm, 10),
        in_specs=[pl.BlockSpec((1, tm, D), lambda i, k: (k, i, 0)), pl.BlockSpec((D, D), lambda i, k: (0, _dz_col(k))),
                  row, row, _full((1, D))],
        out_specs=[row, _full((1, D))], out_shape=[SDS((S, D), F32), SDS((1, D), F32)],
        scratch=[pltpu.VMEM((tm, D), F32)],
    )(dz, w, x, dxn, g)


def _mm_dwin(h, dz):
    S = h.shape[0]

    def body(h_r, dz_r, o_r):
        o_r[...] = _dot_tn(h_r[...], dz_r[0]).astype(BF16)

    return _pcall(
        body, name="mm_dwin", grid=(10,),
        in_specs=[pl.BlockSpec((S, D), lambda n: (0, 0)), pl.BlockSpec((1, S, D), lambda n: (n, 0, 0))],
        out_specs=pl.BlockSpec((D, D), lambda n: (0, _dz_col(n))), out_shape=SDS((D, NIN), BF16), vmem_mb=56,
    )(h, dz)


def _mm_tn4(a4, b4):
    S = a4.shape[1]

    def body(a_r, b_r, o_r):
        o_r[...] = _dot_tn(a_r[...], b_r[...]).astype(BF16)

    pair = pl.BlockSpec((None, S, D), lambda w: (w, 0, 0))
    return _pcall(body, name="mm_tn4", grid=(4,), in_specs=[pair, pair],
                  out_specs=pl.BlockSpec((None, D, D), lambda w: (w, 0, 0)), out_shape=SDS((4, D, D), BF16),
                  vmem_mb=56)(a4, b4)


def _mem_bwd(mem, g, wkv, dkv, dg_acc):
    def body(m_ref, g_ref, w_ref, dkv_ref, acc_ref, dw_ref, dg_ref):
        m = m_ref[...]
        mr = m * _rms_scale(m)
        mn = (mr * g_ref[...]).astype(BF16)
        dkb = dkv_ref[...].astype(BF16)
        dw_ref[...] = _dot_tn(mn, dkb).astype(BF16)
        dg_ref[...] = acc_ref[...] + _rowsum(_dot_nt(dkb, w_ref[...]) * mr)

    return _pcall(body, name="mem_bwd", in_specs=[VMEM] * 5, out_specs=[VMEM] * 2,
                  out_shape=[SDS((D, 2 * D), BF16), SDS((1, D), F32)], vmem_mb=48)(mem, g, wkv, dkv, dg_acc)


def _adamw_math(w, g, m, v):
    m2 = ADAM_B1 * m + (1.0 - ADAM_B1) * g
    v2 = ADAM_B2 * v + (1.0 - ADAM_B2) * (g * g)
    mh = m2 / (1.0 - ADAM_B1 ** ADAM_STEP)
    vh = v2 / (1.0 - ADAM_B2 ** ADAM_STEP)
    return -ADAM_LR * (mh / (jnp.sqrt(vh) + ADAM_EPS) + ADAM_WD * w), m2, v2


def _adamw_layer(l, w, m, v, g, prev, which=None, rows=256):
    L, R, C = w.shape

    def body(w_r, m_r, v_r, g_r, *rest):
        g_o, d_o, m_o, v_o = rest[-4:]
        g = g_r[...]
        d, m2, v2 = _adamw_math(w_r[...], g, m_r[...], v_r[...])
        g_o[...] = g
        d_o[...] = d
        m_o[...] = m2
        v_o[...] = v2

    st = pl.BlockSpec((None, rows, C), lambda i: (l, i, 0))
    gs = pl.BlockSpec((rows, C), lambda i: (i, 0)) if which is None else pl.BlockSpec((None, rows, C), lambda i: (which, i, 0))
    carried = list(prev) if prev is not None else []
    return _pcall(body, name="adamw_layer", grid=(R // rows,), in_specs=[st] * 3 + [gs] + [HBM] * len(carried),
                  out_specs=[st] * 4, out_shape=[SDS(w.shape, F32)] * 4, vmem_mb=56,
                  aliases={4 + k: k for k in range(len(carried))} or None)(w, m, v, g, *carried)


def _adamw_proj(l, trios, g_p, prevs, rows=128):
    L, R, C = trios[0][0].shape

    def body(*refs):
        ins, g_r, outs = refs[0:12], refs[12], refs[-16:]
        for k in range(4):
            g = g_r[k]
            d, m2, v2 = _adamw_math(ins[3 * k][...], g, ins[3 * k + 1][...], ins[3 * k + 2][...])
            for o, val in zip(outs[4 * k:4 * k + 4], (g, d, m2, v2)):
                o[...] = val

    st = pl.BlockSpec((None, rows, C), lambda i: (l, i, 0))
    carried = [a for p in prevs for a in p] if prevs[0] is not None else []
    res = _pcall(body, name="adamw_proj", grid=(R // rows,),
                 in_specs=[st] * 12 + [pl.BlockSpec((4, rows, C), lambda i: (0, i, 0))] + [HBM] * len(carried),
                 out_specs=[st] * 16, out_shape=[SDS(trios[0][0].shape, F32)] * 16, vmem_mb=56,
                 aliases={13 + k: k for k in range(len(carried))} or None)(*[a for t in trios for a in t], g_p, *carried)
    return [res[4 * k:4 * k + 4] for k in range(4)]


def _adamw_flat(w, m, v, g, rows):
    R, C = w.shape

    def body(w_r, m_r, v_r, g_r, d_o, m_o, v_o):
        d, m2, v2 = _adamw_math(w_r[...], g_r[...], m_r[...], v_r[...])
        d_o[...] = d
        m_o[...] = m2
        v_o[...] = v2

    blk = pl.BlockSpec((rows, C), lambda i: (i, 0))
    return _pcall(body, name="adamw_flat", grid=(R // rows,), in_specs=[blk] * 4, out_specs=[blk] * 3,
                  out_shape=[SDS((R, C), F32)] * 3)(w, m, v, g)


_SMALL = ("mem_norm_g", "pre_norm_g", "post_norm_g", "gmlp_ln_g", "gmlp_ln_b", "gmlp_ws", "gmlp_bs", "conv_b", "lru_wr",
          "lru_br", "lru_wi", "lru_bi", "lru_lambda")


def _pack_small(parts, conv_w_part):
    rows = [parts[n].reshape(-1, 128) for n in _SMALL] + [conv_w_part.reshape(-1, 128)]
    used = sum(r.shape[0] for r in rows)
    rows.append(jnp.zeros((SMALL_ROWS - used, 128), F32))
    return jnp.concatenate(rows, axis=0)


def _unpack_small(pack, shapes):
    out, at = {}, 0
    for n in _SMALL:
        size = 1
        for s in shapes[n]:
            size *= s
        out[n] = pack[at:at + size // 128].reshape(shapes[n])
        at += size // 128
    return out, at


def kernel(x, mem, mem_norm_g, pre_norm_g, post_norm_g, w_in, gmlp_ln_g, gmlp_ln_b, gmlp_ws, gmlp_bs, conv_w, conv_b, lru_wr, lru_br, lru_wi, lru_bi, lru_lambda, w_kv, w_pa, w_pb, w_pc, w_out, loss_target, m_mem_norm_g, m_pre_norm_g, m_post_norm_g, m_w_in, m_gmlp_ln_g, m_gmlp_ln_b, m_gmlp_ws, m_gmlp_bs, m_conv_w, m_conv_b, m_lru_wr, m_lru_br, m_lru_wi, m_lru_bi, m_lru_lambda, m_w_kv, m_w_pa, m_w_pb, m_w_pc, m_w_out, v_mem_norm_g, v_pre_norm_g, v_post_norm_g, v_w_in, v_gmlp_ln_g, v_gmlp_ln_b, v_gmlp_ws, v_gmlp_bs, v_conv_w, v_conv_b, v_lru_wr, v_lru_br, v_lru_wi, v_lru_bi, v_lru_lambda, v_w_kv, v_w_pa, v_w_pb, v_w_pc, v_w_out):
    L = w_in.shape[0]
    S = x.shape[1]
    xs = [x[0]]
    mem2 = mem[0]
    mg = mem_norm_g.reshape(1, D)
    vec = lambda a, l: a[l].reshape(1, D)
    ci = lax.axis_index("c")
    jpos = 2 * lax.axis_index("x") + lax.axis_index("y")
    pos = jnp.reshape(jpos, (1,)).astype(jnp.int32)
    pos2 = jnp.stack([jpos, ci]).astype(jnp.int32)

    cw8 = jnp.pad(conv_w, ((0, 0), (0, 4), (0, 0)))
    given = dict(mem_norm_g=(mem_norm_g, m_mem_norm_g, v_mem_norm_g), pre_norm_g=(pre_norm_g, m_pre_norm_g, v_pre_norm_g),
                 post_norm_g=(post_norm_g, m_post_norm_g, v_post_norm_g), gmlp_ln_g=(gmlp_ln_g, m_gmlp_ln_g, v_gmlp_ln_g),
                 gmlp_ln_b=(gmlp_ln_b, m_gmlp_ln_b, v_gmlp_ln_b), gmlp_ws=(gmlp_ws, m_gmlp_ws, v_gmlp_ws),
                 gmlp_bs=(gmlp_bs, m_gmlp_bs, v_gmlp_bs), conv_b=(conv_b, m_conv_b, v_conv_b), lru_wr=(lru_wr, m_lru_wr, v_lru_wr),
                 lru_br=(lru_br, m_lru_br, v_lru_br), lru_wi=(lru_wi, m_lru_wi, v_lru_wi), lru_bi=(lru_bi, m_lru_bi, v_lru_bi),
                 lru_lambda=(lru_lambda, m_lru_lambda, v_lru_lambda))
    shapes = {n: given[n][0].shape for n in _SMALL}
    zero_cw = jnp.zeros((L, 4, D), F32)
    packs = [_pack_small({n: given[n][k] for n in _SMALL}, zero_cw) for k in range(3)]

    placed = [_cast_place(0, pos, w_in, w_kv, w_pa, w_pb, w_pc, w_out, cw8)]
    W = [None] * L
    ALL = (0, 1, 2, 3)
    first = _gather_start("gather_start_0a", (0,), placed[0][0:1])
    rest = _gather_start("gather_start_0b", (1, 2, 3), placed[0][1:4], first[-1])
    cw8 = cw8 + rest[-1][0, 0]
    placed += [_cast_place(l, pos, w_in, w_kv, w_pa, w_pb, w_pc, w_out, cw8) for l in range(1, L)]
    busy = placed[L - 1][3][0:1, 0:128] + packs[0][0:1, :] + packs[1][0:1, :] + packs[2][0:1, :]
    mid = _gather_mid("gather_mid_0a", (0,), first, busy)
    started = _gather_start("gather_start_1", ALL, placed[1], mid[-1])
    win0 = _gather_end("gather_end_0a", (0,), mid, started[-1])[0]
    zh0 = _mm_in(xs[0], vec(pre_norm_g, 0), win0)
    mid = _gather_mid("gather_mid_0b", (1, 2, 3), rest, zh0[1])
    W[0] = [win0] + list(_gather_end("gather_end_0b", (1, 2, 3), mid, mid[-1]))

    saved = []
    for l in range(L):
        Win, Wkv, Wp, Cw = W[l]
        z, h = zh0 if l == 0 else _mm_in(xs[l], vec(pre_norm_g, l), Win)
        bsb = jnp.broadcast_to(gmlp_bs[l][:, :, None], (NG, GB, GB))
        ya = _branch_a(z, vec(gmlp_ln_g, l), vec(gmlp_ln_b, l), gmlp_ws[l], bsb)
        yb, hs, *lru = _branch_b(z, Cw, vec(conv_b, l), lru_wr[l], vec(lru_br, l), lru_wi[l], vec(lru_bi, l), vec(lru_lambda, l))
        kv = _kv(mem2, mg, Wkv)
        yc, pw = _branch_c(z, kv)
        pg = vec(post_norm_g, l)
        if l + 1 < L:
            mid = _gather_mid(f"gather_mid_{l + 1}", ALL, started, yc)
            if l + 2 < L:
                started = _gather_start(f"gather_start_{l + 2}", ALL, placed[l + 2], mid[-1])
                pg = pg + started[-1][0, 0]
        pa, pb, pc, mgd, o, xn = _merge_out(ya, yb, yc, z, Wp, xs[l], pg)
        if l + 1 < L:
            W[l + 1] = _gather_end(f"gather_end_{l + 1}", ALL, mid, xn)
        xs.append(xn)
        saved.append((z, h, ya, yb, yc, hs, kv, pa, pb, pc, mgd, o, bsb, lru, pw))

    loss11, dxn = _loss_head(xs[L], loss_target[0])
    loss = lax.psum(loss11[0, 0], ("x", "y", "c"))

    big = dict(w_in=(w_in, m_w_in, v_w_in), w_kv=(w_kv, m_w_kv, v_w_kv), w_pa=(w_pa, m_w_pa, v_w_pa),
               w_pb=(w_pb, m_w_pb, v_w_pb), w_pc=(w_pc, m_w_pc, v_w_pc), w_out=(w_out, m_w_out, v_w_out))
    out = {n: None for n in big}
    kin, nsh, nkv, rp = w_in.shape[1], w_in.shape[2], w_kv.shape[2], w_pa.shape[1]

    def finish_layer(l, a2a, after):
        g_p4, lp = _a2a_wait(f"a2a_p_wait_{l}", (2,), a2a[0], after)
        g_in, g_kv, lin, lkv = _a2a_wait(f"a2a_w_wait_{l}", (0, 1), a2a[1], after)
        g_in, g_kv, g_p = _sum_share(pos2, (lin, lkv, lp.reshape(NDEV - 1, 2 * rp, D)), (g_in, g_kv, g_p4))
        g_p = g_p.reshape(4, rp, D)
        out["w_in"] = _adamw_layer(l, *big["w_in"], g_in, out["w_in"])
        out["w_kv"] = _adamw_layer(l, *big["w_kv"], g_kv, out["w_kv"])
        proj = ("w_pa", "w_pb", "w_pc", "w_out")
        for n, res in zip(proj, _adamw_proj(l, [big[n] for n in proj], g_p, [out[n] for n in proj])):
            out[n] = res

    small = {n: [None] * L for n in _SMALL}
    dconv_w = [None] * L
    dg_mem = jnp.zeros((1, D), F32)
    pending = None
    sent = []
    for l in reversed(range(L)):
        Win, Wkv, Wp, Cw = W[l]
        z, h, ya, yb, yc, hs, kv, pa, pb, pc, mgd, o, bsb, lru, pw = saved[l]
        pg = vec(post_norm_g, l) if pending is None else vec(post_norm_g, l) + pending[1][1][-1][0, 0]
        dp4, dya, dyb, dyc, dz, dgpost = _out_bwd(dxn, o, pg, Wp, z, pa, pb, pc)
        a2a_p = _a2a_start(f"a2a_p_start_{l}", (2,), (_mm_tn4(mgd, dp4),),
                           (lax.empty((NDEV - 1, 4, rp // 2, D), BF16),))
        dz, dws, dbs, dlg, dlb = _branch_a_bwd(z, dya, vec(gmlp_ln_g, l) + a2a_p[-1][0, 0], vec(gmlp_ln_b, l), gmlp_ws[l], bsb, dz)
        dz, dcw, dcb, dwr, dbr, dwi, dbi, dlam = _branch_b_bwd(z, hs, lru, dyb, Cw, lru_wr[l], lru_wi[l], vec(lru_lambda, l), dz)
        dz, dkv = _branch_c_bwd(z, kv, pw, dyc, dz)
        g_in = _mm_dwin(h, dz)
        g_kv, dg_mem = _mem_bwd(mem2, mg, Wkv, dkv, dg_mem)
        a2a_w = _a2a_start(f"a2a_w_start_{l}", (0, 1), (g_in, g_kv),
                           (lax.empty((NDEV - 1, kin // 2, nsh), BF16), lax.empty((NDEV - 1, kin // 2, nkv), BF16)))
        dx, dgpre = _mm_dh(dz, Win, xs[l], dxn, vec(pre_norm_g, l) + a2a_w[-1][0, 0])
        pending = (l, (a2a_p, a2a_w))
        sent.append(pending)
        for n, val in (("pre_norm_g", dgpre), ("post_norm_g", dgpost), ("gmlp_ln_g", dlg), ("gmlp_ln_b", dlb), ("gmlp_ws", dws),
                       ("gmlp_bs", dbs), ("conv_b", dcb), ("lru_wr", dwr), ("lru_br", dbr), ("lru_wi", dwi), ("lru_bi", dbi),
                       ("lru_lambda", dlam)):
            small[n][l] = val
        dconv_w[l] = dcw[0:4]
        dxn = dx
    grad_x = dxn.reshape(1, S, D)

    parts = {n: jnp.stack(small[n]) for n in _SMALL if n != "mem_norm_g"}
    parts["mem_norm_g"] = dg_mem
    me1 = jnp.reshape(2 * jpos + ci, (1,)).astype(jnp.int32)
    rs = _rs_start(_pack_small(parts, jnp.stack(dconv_w)), lax.empty((NDEV - 1, SMALL_ROWS // NDEV, 128), F32))
    for l, a2a in sent[:-1]:
        finish_layer(l, a2a, rs[-1])
    pack, land = _rs_wait(rs, out["w_out"][1])
    ag = _ag_start(_small_sum(me1, pack, land))
    finish_layer(pending[0], pending[1], ag[-1])
    gsum = _ag_wait(ag, out["w_out"][1])
    dsm, msm, vsm = _adamw_flat(packs[0], packs[1], packs[2], gsum, 2560)
    g_small, at = _unpack_small(gsum, shapes)
    d_small, _ = _unpack_small(dsm, shapes)
    m_small, _ = _unpack_small(msm, shapes)
    v_small, _ = _unpack_small(vsm, shapes)
    for n in _SMALL:
        out[n] = (g_small[n], d_small[n], m_small[n], v_small[n])
    g_cw = lax.dynamic_slice_in_dim(gsum[at:at + L * 4 * D // 128].reshape(L * 4, D), jpos * (D // 4), D // 4, axis=1)
    d_cw, m_cw, v_cw = _adamw_flat(conv_w.reshape(L * 4, D // 4), m_conv_w.reshape(L * 4, D // 4),
                                   v_conv_w.reshape(L * 4, D // 4), g_cw, L * 4)
    out["conv_w"] = tuple(a.reshape(L, 4, D // 4) for a in (g_cw, d_cw, m_cw, v_cw))

    order = ("mem_norm_g", "pre_norm_g", "post_norm_g", "w_in", "gmlp_ln_g", "gmlp_ln_b", "gmlp_ws", "gmlp_bs", "conv_w", "conv_b",
             "lru_wr", "lru_br", "lru_wi", "lru_bi", "lru_lambda", "w_kv", "w_pa", "w_pb", "w_pc", "w_out")
    return (loss, grad_x, *[out[n][0] for n in order], *[out[n][1] for n in order], *[out[n][2] for n in order],
            *[out[n][3] for n in order])
```
